```python
import jax, jax.numpy as jnp
from jax import lax
import numpy as np

D_MODEL = 1024
BATCH = 16
SEQ = 2048
DEPTH = 4

N_MIXERS = 2
D_FF = 2816
CONV_WIDTH = 31
DN_HEADS = 8
DN_HEAD_DIM = 128
DN_WIDTH = DN_HEADS * DN_HEAD_DIM
SHORT_CONV = 4
CHUNK = 64
N_CONV_LAYERS = (DEPTH + 1) // 2
N_DN_LAYERS = DEPTH // 2
N_SUB = 3
EPS = 1e-6
FFN_RES_WEIGHT = 0.5

kernel_name = "hybrid_conformer_gated_deltanet_block"


def rms_norm(x, g):
    xf = x.astype(jnp.float32)
    y = xf * lax.rsqrt(jnp.mean(xf * xf, axis=-1, keepdims=True) + EPS)
    return (y * g.astype(jnp.float32)).astype(x.dtype)


def layer_norm(x, g, b):
    xf = x.astype(jnp.float32)
    mu = jnp.mean(xf, axis=-1, keepdims=True)
    xc = xf - mu
    var = jnp.mean(xc * xc, axis=-1, keepdims=True)
    y = xc * lax.rsqrt(var + EPS) * g.astype(jnp.float32) + b.astype(jnp.float32)
    return y.astype(x.dtype)


def causal_dwconv(x, w):
    K, C = w.shape
    return lax.conv_general_dilated(
        x, w[:, None, :].astype(x.dtype), window_strides=(1,), padding=[(K - 1, 0)],
        dimension_numbers=("NWC", "WIO", "NWC"), feature_group_count=C)


def swiglu_ffn(h, w_in, w_out):
    gate, up = jnp.split(h @ w_in, 2, axis=-1)
    return (jax.nn.silu(gate) * up) @ w_out


def conv_module(h, w_glu, b_glu, w_dw, b_dw, ln_g, ln_b, w_pw, b_pw):
    a, b = jnp.split(h @ w_glu + b_glu, 2, axis=-1)
    u = a * jax.nn.sigmoid(b)
    u = causal_dwconv(u, w_dw) + b_dw
    u = jax.nn.silu(layer_norm(u, ln_g, ln_b))
    return u @ w_pw + b_pw


def l2norm(t):
    return t * lax.rsqrt(jnp.sum(t * t, axis=-1, keepdims=True) + EPS)


def chunk_gated_delta_rule(q, k, v, g, beta):
    f32 = jnp.float32
    B, T, H, Dh = q.shape
    N = T // CHUNK
    q = l2norm(q.astype(f32)) * (Dh ** -0.5)
    k = l2norm(k.astype(f32))
    v = v.astype(f32)

    def to_chunks(t):
        return t.reshape(B, N, CHUNK, H, -1).transpose(1, 0, 3, 2, 4)

    def to_chunks_s(t):
        return t.reshape(B, N, CHUNK, H).transpose(1, 0, 3, 2)

    q, k, v = to_chunks(q), to_chunks(k), to_chunks(v)
    beta = to_chunks_s(beta.astype(f32))
    g = jnp.cumsum(to_chunks_s(g.astype(f32)), axis=-1)

    causal = jnp.tril(jnp.ones((CHUNK, CHUNK), dtype=bool))
    strict = jnp.tril(jnp.ones((CHUNK, CHUNK), dtype=bool), -1)
    decay = jnp.exp(jnp.where(causal, g[..., :, None] - g[..., None, :], -jnp.inf))

    kb = k * beta[..., None]
    vb = v * beta[..., None]
    A = jnp.where(strict, jnp.einsum("nbhid,nbhjd->nbhij", kb, k) * decay, 0.0)
    eye = jnp.broadcast_to(jnp.eye(CHUNK, dtype=f32), A.shape)
    Tm = lax.linalg.triangular_solve(A, eye, left_side=True, lower=True, unit_diagonal=True)

    u = jnp.einsum("nbhij,nbhjd->nbhid", Tm, vb)
    w = jnp.einsum("nbhij,nbhjd->nbhid", Tm, kb * jnp.exp(g)[..., None])
    qg = q * jnp.exp(g)[..., None]
    intra = jnp.einsum("nbhid,nbhjd->nbhij", q, k) * decay
    g_last = g[..., -1]
    kd = k * jnp.exp(g_last[..., None] - g)[..., None]

    def step(S, inp):
        qg_i, w_i, u_i, intra_i, kd_i, gl_i = inp
        v_new = u_i - jnp.einsum("bhcd,bhde->bhce", w_i, S)
        o_i = jnp.einsum("bhcd,bhde->bhce", qg_i, S) + jnp.einsum("bhij,bhje->bhie", intra_i, v_new)
        S = S * jnp.exp(gl_i)[..., None, None] + jnp.einsum("bhcd,bhce->bhde", kd_i, v_new)
        return S, o_i

    S0 = jnp.zeros((B, H, Dh, v.shape[-1]), dtype=f32)
    _, o = lax.scan(step, S0, (qg, w, u, intra, kd, g_last))
    return o.transpose(1, 0, 3, 2, 4).reshape(B, T, H, -1)


def gated_deltanet(h, w_in, w_sconv, a_log, dt_bias, o_g, w_out):
    B, T, _ = h.shape
    W, H = DN_WIDTH, DN_HEADS
    proj = h @ w_in
    qkv = proj[..., :3 * W]
    z = proj[..., 3 * W:4 * W]
    a = proj[..., 4 * W:4 * W + H]
    b = proj[..., 4 * W + H:]
    qkv = jax.nn.silu(causal_dwconv(qkv, w_sconv))
    q, k, v = [t.reshape(B, T, H, DN_HEAD_DIM) for t in jnp.split(qkv, 3, axis=-1)]
    g = -jnp.exp(a_log.astype(jnp.float32)) * jax.nn.softplus(a.astype(jnp.float32) + dt_bias.astype(jnp.float32))
    beta = jax.nn.sigmoid(b.astype(jnp.float32))
    o = chunk_gated_delta_rule(q, k, v, g, beta).astype(h.dtype)
    o = rms_norm(o, o_g) * jax.nn.silu(z.reshape(B, T, H, DN_HEAD_DIM))
    return o.reshape(B, T, W) @ w_out


def _fwd_setup_inputs(seed: int = 0) -> dict:
    key = jax.random.key(seed)
    ks = iter(jax.random.split(key, 32))
    f32 = jnp.float32
    D, F, W, H = D_MODEL, D_FF, DN_WIDTH, DN_HEADS
    NA, NB = N_CONV_LAYERS, N_DN_LAYERS

    def nrm(shape, fan_in, mult=1.0):
        return jax.random.normal(next(ks), shape, f32) * (mult * fan_in ** -0.5)

    def small(shape, s=0.02):
        return jax.random.normal(next(ks), shape, f32) * s

    x = jax.random.normal(next(ks), (BATCH, SEQ, D), f32)
    c = jax.random.normal(next(ks), (BATCH, D), f32)
    norm_g = 1.0 + small((DEPTH, N_SUB, D))
    w_ada = nrm((DEPTH, D, N_SUB * 3 * D), D, 0.2)
    b_ada = small((DEPTH, N_SUB * 3 * D))
    w_ffn_in = nrm((DEPTH, 2, D, 2 * F), D)
    w_ffn_out = nrm((DEPTH, 2, F, D), F)
    cm_w_glu = nrm((NA, D, 2 * D), D)
    cm_b_glu = small((NA, 2 * D))
    cm_w_dw = nrm((NA, CONV_WIDTH, D), CONV_WIDTH)
    cm_b_dw = small((NA, D))
    cm_ln_g = 1.0 + small((NA, D))
    cm_ln_b = small((NA, D))
    cm_w_pw = nrm((NA, D, D), D)
    cm_b_pw = small((NA, D))
    dn_w_in = nrm((NB, D, 4 * W + 2 * H), D)
    dn_w_sconv = nrm((NB, SHORT_CONV, 3 * W), SHORT_CONV)
    dn_a_log = jnp.log(jax.random.uniform(next(ks), (NB, H), f32, 1.0, 16.0))
    dt = jnp.exp(jax.random.uniform(next(ks), (NB, H), f32, np.log(1e-3), np.log(1e-1)))
    dn_dt_bias = dt + jnp.log(-jnp.expm1(-dt))
    dn_o_g = 1.0 + small((NB, DN_HEAD_DIM))
    dn_w_out = nrm((NB, W, D), W)
    final_g = 1.0 + small((D,))
    return {"x": x, "c": c, "norm_g": norm_g, "w_ada": w_ada, "b_ada": b_ada,
            "w_ffn_in": w_ffn_in, "w_ffn_out": w_ffn_out,
            "cm_w_glu": cm_w_glu, "cm_b_glu": cm_b_glu, "cm_w_dw": cm_w_dw, "cm_b_dw": cm_b_dw,
            "cm_ln_g": cm_ln_g, "cm_ln_b": cm_ln_b, "cm_w_pw": cm_w_pw, "cm_b_pw": cm_b_pw,
            "dn_w_in": dn_w_in, "dn_w_sconv": dn_w_sconv, "dn_a_log": dn_a_log,
            "dn_dt_bias": dn_dt_bias, "dn_o_g": dn_o_g, "dn_w_out": dn_w_out,
            "final_g": final_g}


def _fwd_reference(x, c, norm_g, w_ada, b_ada, w_ffn_in, w_ffn_out,
              cm_w_glu, cm_b_glu, cm_w_dw, cm_b_dw, cm_ln_g, cm_ln_b, cm_w_pw, cm_b_pw,
              dn_w_in, dn_w_sconv, dn_a_log, dn_dt_bias, dn_o_g, dn_w_out, final_g):
    B = x.shape[0]
    D = x.shape[-1]
    cs = jax.nn.silu(c)
    for i in range(DEPTH):
        mod = (cs @ w_ada[i] + b_ada[i]).reshape(B, N_SUB, 3, D)

        def modulated(x, j):
            shift, scale = mod[:, j, 0][:, None, :], mod[:, j, 1][:, None, :]
            return rms_norm(x, norm_g[i, j]) * (1.0 + scale) + shift

        def gate(j):
            return 1.0 + mod[:, j, 2][:, None, :]

        h = modulated(x, 0)
        x = x + FFN_RES_WEIGHT * gate(0) * swiglu_ffn(h, w_ffn_in[i, 0], w_ffn_out[i, 0])

        h = modulated(x, 1)
        if i % N_MIXERS == 0:
            a = i // N_MIXERS
            y = conv_module(h, cm_w_glu[a], cm_b_glu[a], cm_w_dw[a], cm_b_dw[a],
                            cm_ln_g[a], cm_ln_b[a], cm_w_pw[a], cm_b_pw[a])
        else:
            m = i // N_MIXERS
            y = gated_deltanet(h, dn_w_in[m], dn_w_sconv[m], dn_a_log[m], dn_dt_bias[m],
                               dn_o_g[m], dn_w_out[m])
        x = x + gate(1) * y

        h = modulated(x, 2)
        x = x + FFN_RES_WEIGHT * gate(2) * swiglu_ffn(h, w_ffn_in[i, 1], w_ffn_out[i, 1])
    return rms_norm(x, final_g)


import jax as _jax
import jax.numpy as _jnp

TWIN_FORMAT = 'train_step'
FWD_PARAMS = ['x', 'c', 'norm_g', 'w_ada', 'b_ada', 'w_ffn_in', 'w_ffn_out', 'cm_w_glu', 'cm_b_glu', 'cm_w_dw', 'cm_b_dw', 'cm_ln_g', 'cm_ln_b', 'cm_w_pw', 'cm_b_pw', 'dn_w_in', 'dn_w_sconv', 'dn_a_log', 'dn_dt_bias', 'dn_o_g', 'dn_w_out', 'final_g']
TWIN_WEIGHTS = ['norm_g', 'w_ada', 'b_ada', 'w_ffn_in', 'w_ffn_out', 'cm_w_glu', 'cm_b_glu', 'cm_w_dw', 'cm_b_dw', 'cm_ln_g', 'cm_ln_b', 'cm_w_pw', 'cm_b_pw', 'dn_w_in', 'dn_w_sconv', 'dn_a_log', 'dn_dt_bias', 'dn_o_g', 'dn_w_out', 'final_g']
TWIN_DIFF_INPUT = 'x'
TWIN_INPUTS = ['x', 'c', 'norm_g', 'w_ada', 'b_ada', 'w_ffn_in', 'w_ffn_out', 'cm_w_glu', 'cm_b_glu', 'cm_w_dw', 'cm_b_dw', 'cm_ln_g', 'cm_ln_b', 'cm_w_pw', 'cm_b_pw', 'dn_w_in', 'dn_w_sconv', 'dn_a_log', 'dn_dt_bias', 'dn_o_g', 'dn_w_out', 'final_g', 'loss_target', 'm_norm_g', 'm_w_ada', 'm_b_ada', 'm_w_ffn_in', 'm_w_ffn_out', 'm_cm_w_glu', 'm_cm_b_glu', 'm_cm_w_dw', 'm_cm_b_dw', 'm_cm_ln_g', 'm_cm_ln_b', 'm_cm_w_pw', 'm_cm_b_pw', 'm_dn_w_in', 'm_dn_w_sconv', 'm_dn_a_log', 'm_dn_dt_bias', 'm_dn_o_g', 'm_dn_w_out', 'm_final_g', 'v_norm_g', 'v_w_ada', 'v_b_ada', 'v_w_ffn_in', 'v_w_ffn_out', 'v_cm_w_glu', 'v_cm_b_glu', 'v_cm_w_dw', 'v_cm_b_dw', 'v_cm_ln_g', 'v_cm_ln_b', 'v_cm_w_pw', 'v_cm_b_pw', 'v_dn_w_in', 'v_dn_w_sconv', 'v_dn_a_log', 'v_dn_dt_bias', 'v_dn_o_g', 'v_dn_w_out', 'v_final_g']
TWIN_OUTPUTS = ['loss', 'grad_x', 'grad_norm_g', 'grad_w_ada', 'grad_b_ada', 'grad_w_ffn_in', 'grad_w_ffn_out', 'grad_cm_w_glu', 'grad_cm_b_glu', 'grad_cm_w_dw', 'grad_cm_b_dw', 'grad_cm_ln_g', 'grad_cm_ln_b', 'grad_cm_w_pw', 'grad_cm_b_pw', 'grad_dn_w_in', 'grad_dn_w_sconv', 'grad_dn_a_log', 'grad_dn_dt_bias', 'grad_dn_o_g', 'grad_dn_w_out', 'grad_final_g', 'delta_norm_g', 'delta_w_ada', 'delta_b_ada', 'delta_w_ffn_in', 'delta_w_ffn_out', 'delta_cm_w_glu', 'delta_cm_b_glu', 'delta_cm_w_dw', 'delta_cm_b_dw', 'delta_cm_ln_g', 'delta_cm_ln_b', 'delta_cm_w_pw', 'delta_cm_b_pw', 'delta_dn_w_in', 'delta_dn_w_sconv', 'delta_dn_a_log', 'delta_dn_dt_bias', 'delta_dn_o_g', 'delta_dn_w_out', 'delta_final_g', 'new_m_norm_g', 'new_m_w_ada', 'new_m_b_ada', 'new_m_w_ffn_in', 'new_m_w_ffn_out', 'new_m_cm_w_glu', 'new_m_cm_b_glu', 'new_m_cm_w_dw', 'new_m_cm_b_dw', 'new_m_cm_ln_g', 'new_m_cm_ln_b', 'new_m_cm_w_pw', 'new_m_cm_b_pw', 'new_m_dn_w_in', 'new_m_dn_w_sconv', 'new_m_dn_a_log', 'new_m_dn_dt_bias', 'new_m_dn_o_g', 'new_m_dn_w_out', 'new_m_final_g', 'new_v_norm_g', 'new_v_w_ada', 'new_v_b_ada', 'new_v_w_ffn_in', 'new_v_w_ffn_out', 'new_v_cm_w_glu', 'new_v_cm_b_glu', 'new_v_cm_w_dw', 'new_v_cm_b_dw', 'new_v_cm_ln_g', 'new_v_cm_ln_b', 'new_v_cm_w_pw', 'new_v_cm_b_pw', 'new_v_dn_w_in', 'new_v_dn_w_sconv', 'new_v_dn_a_log', 'new_v_dn_dt_bias', 'new_v_dn_o_g', 'new_v_dn_w_out', 'new_v_final_g']
TWIN_LEAF_KINDS = {'loss': 'loss', 'grad_x': 'grad_x', 'grad_norm_g': 'grad_w', 'grad_w_ada': 'grad_w', 'grad_b_ada': 'grad_w', 'grad_w_ffn_in': 'grad_w', 'grad_w_ffn_out': 'grad_w', 'grad_cm_w_glu': 'grad_w', 'grad_cm_b_glu': 'grad_w', 'grad_cm_w_dw': 'grad_w', 'grad_cm_b_dw': 'grad_w', 'grad_cm_ln_g': 'grad_w', 'grad_cm_ln_b': 'grad_w', 'grad_cm_w_pw': 'grad_w', 'grad_cm_b_pw': 'grad_w', 'grad_dn_w_in': 'grad_w', 'grad_dn_w_sconv': 'grad_w', 'grad_dn_a_log': 'grad_w', 'grad_dn_dt_bias': 'grad_w', 'grad_dn_o_g': 'grad_w', 'grad_dn_w_out': 'grad_w', 'grad_final_g': 'grad_w', 'delta_norm_g': 'delta_w', 'delta_w_ada': 'delta_w', 'delta_b_ada': 'delta_w', 'delta_w_ffn_in': 'delta_w', 'delta_w_ffn_out': 'delta_w', 'delta_cm_w_glu': 'delta_w', 'delta_cm_b_glu': 'delta_w', 'delta_cm_w_dw': 'delta_w', 'delta_cm_b_dw': 'delta_w', 'delta_cm_ln_g': 'delta_w', 'delta_cm_ln_b': 'delta_w', 'delta_cm_w_pw': 'delta_w', 'delta_cm_b_pw': 'delta_w', 'delta_dn_w_in': 'delta_w', 'delta_dn_w_sconv': 'delta_w', 'delta_dn_a_log': 'delta_w', 'delta_dn_dt_bias': 'delta_w', 'delta_dn_o_g': 'delta_w', 'delta_dn_w_out': 'delta_w', 'delta_final_g': 'delta_w', 'new_m_norm_g': 'new_m', 'new_m_w_ada': 'new_m', 'new_m_b_ada': 'new_m', 'new_m_w_ffn_in': 'new_m', 'new_m_w_ffn_out': 'new_m', 'new_m_cm_w_glu': 'new_m', 'new_m_cm_b_glu': 'new_m', 'new_m_cm_w_dw': 'new_m', 'new_m_cm_b_dw': 'new_m', 'new_m_cm_ln_g': 'new_m', 'new_m_cm_ln_b': 'new_m', 'new_m_cm_w_pw': 'new_m', 'new_m_cm_b_pw': 'new_m', 'new_m_dn_w_in': 'new_m', 'new_m_dn_w_sconv': 'new_m', 'new_m_dn_a_log': 'new_m', 'new_m_dn_dt_bias': 'new_m', 'new_m_dn_o_g': 'new_m', 'new_m_dn_w_out': 'new_m', 'new_m_final_g': 'new_m', 'new_v_norm_g': 'new_v', 'new_v_w_ada': 'new_v', 'new_v_b_ada': 'new_v', 'new_v_w_ffn_in': 'new_v', 'new_v_w_ffn_out': 'new_v', 'new_v_cm_w_glu': 'new_v', 'new_v_cm_b_glu': 'new_v', 'new_v_cm_w_dw': 'new_v', 'new_v_cm_b_dw': 'new_v', 'new_v_cm_ln_g': 'new_v', 'new_v_cm_ln_b': 'new_v', 'new_v_cm_w_pw': 'new_v', 'new_v_cm_b_pw': 'new_v', 'new_v_dn_w_in': 'new_v', 'new_v_dn_w_sconv': 'new_v', 'new_v_dn_a_log': 'new_v', 'new_v_dn_dt_bias': 'new_v', 'new_v_dn_o_g': 'new_v', 'new_v_dn_w_out': 'new_v', 'new_v_final_g': 'new_v'}


def _forward(args):
    return _fwd_reference(*[args[k] for k in FWD_PARAMS])


def _output_shape():
    out = _jax.eval_shape(lambda: _forward(_fwd_setup_inputs(0)))
    return out.shape, out.dtype

N_MICROBATCH = 1
ADAM_LR = 0.001
ADAM_B1 = 0.9
ADAM_B2 = 0.999
ADAM_EPS = 1e-08
ADAM_WD = 0.01
ADAM_STEP = 10
PER_EXAMPLE_BATCH_AXIS = {'x': 0, 'c': 0, 'loss_target': 0}
SHARED_INPUTS = []
_WEIGHT_DTYPES = {'norm_g': _jnp.float32, 'w_ada': _jnp.float32, 'b_ada': _jnp.float32, 'w_ffn_in': _jnp.float32, 'w_ffn_out': _jnp.float32, 'cm_w_glu': _jnp.float32, 'cm_b_glu': _jnp.float32, 'cm_w_dw': _jnp.float32, 'cm_b_dw': _jnp.float32, 'cm_ln_g': _jnp.float32, 'cm_ln_b': _jnp.float32, 'cm_w_pw': _jnp.float32, 'cm_b_pw': _jnp.float32, 'dn_w_in': _jnp.float32, 'dn_w_sconv': _jnp.float32, 'dn_a_log': _jnp.float32, 'dn_dt_bias': _jnp.float32, 'dn_o_g': _jnp.float32, 'dn_w_out': _jnp.float32, 'final_g': _jnp.float32}
MOMENT_SCALE = {'norm_g': 9.588006e-02, 'w_ada': 5.533951e-02, 'b_ada': 9.524903e-02, 'w_ffn_in': 3.221420e-02, 'w_ffn_out': 5.251807e-02, 'cm_w_glu': 8.584122e-02, 'cm_b_glu': 9.627182e-02, 'cm_w_dw': 1.129399e-01, 'cm_b_dw': 2.654731e-01, 'cm_ln_g': 1.441833e-01, 'cm_ln_b': 1.218907e-01, 'cm_w_pw': 1.112137e-01, 'cm_b_pw': 2.055235e-01, 'dn_w_in': 6.565119e-02, 'dn_w_sconv': 6.046147e-02, 'dn_a_log': 3.559391e-01, 'dn_dt_bias': 3.403394e-01, 'dn_o_g': 2.905269e-01, 'dn_w_out': 7.925804e-02, 'final_g': 3.211042e+01}


def _to_microbatches(a, axis):
    t = _jnp.moveaxis(a, axis, 0)
    t = t.reshape((N_MICROBATCH, t.shape[0] // N_MICROBATCH) + t.shape[1:])
    return _jnp.moveaxis(t, 1, axis + 1)


def setup_inputs(seed: int = 0) -> dict:
    inp = _fwd_setup_inputs(seed)
    key = _jax.random.fold_in(_jax.random.key(seed), 7919)
    shape, _ = _output_shape()
    out = dict(inp)
    out["loss_target"] = _jax.random.normal(_jax.random.fold_in(key, 0), shape, _jnp.float32)
    for i, name in enumerate(TWIN_WEIGHTS):
        w = inp[name].astype(_jnp.float32)
        if MOMENT_SCALE is None:
            s = _jnp.sqrt(_jnp.mean(_jnp.square(w)) + 1e-30)
        else:
            s = MOMENT_SCALE[name]
        km, kv = _jax.random.split(_jax.random.fold_in(key, i + 1))
        out[name] = w
        out["m_" + name] = s * _jax.random.normal(km, w.shape, _jnp.float32)
        out["v_" + name] = (s * s) * _jax.random.uniform(kv, w.shape, _jnp.float32, 0.5, 1.5)
    if N_MICROBATCH > 1:
        for name, axis in PER_EXAMPLE_BATCH_AXIS.items():
            out[name] = _to_microbatches(out[name], axis)
    return {'x': out['x'], 'c': out['c'], 'norm_g': out['norm_g'], 'w_ada': out['w_ada'], 'b_ada': out['b_ada'], 'w_ffn_in': out['w_ffn_in'], 'w_ffn_out': out['w_ffn_out'], 'cm_w_glu': out['cm_w_glu'], 'cm_b_glu': out['cm_b_glu'], 'cm_w_dw': out['cm_w_dw'], 'cm_b_dw': out['cm_b_dw'], 'cm_ln_g': out['cm_ln_g'], 'cm_ln_b': out['cm_ln_b'], 'cm_w_pw': out['cm_w_pw'], 'cm_b_pw': out['cm_b_pw'], 'dn_w_in': out['dn_w_in'], 'dn_w_sconv': out['dn_w_sconv'], 'dn_a_log': out['dn_a_log'], 'dn_dt_bias': out['dn_dt_bias'], 'dn_o_g': out['dn_o_g'], 'dn_w_out': out['dn_w_out'], 'final_g': out['final_g'], 'loss_target': out['loss_target'], 'm_norm_g': out['m_norm_g'], 'm_w_ada': out['m_w_ada'], 'm_b_ada': out['m_b_ada'], 'm_w_ffn_in': out['m_w_ffn_in'], 'm_w_ffn_out': out['m_w_ffn_out'], 'm_cm_w_glu': out['m_cm_w_glu'], 'm_cm_b_glu': out['m_cm_b_glu'], 'm_cm_w_dw': out['m_cm_w_dw'], 'm_cm_b_dw': out['m_cm_b_dw'], 'm_cm_ln_g': out['m_cm_ln_g'], 'm_cm_ln_b': out['m_cm_ln_b'], 'm_cm_w_pw': out['m_cm_w_pw'], 'm_cm_b_pw': out['m_cm_b_pw'], 'm_dn_w_in': out['m_dn_w_in'], 'm_dn_w_sconv': out['m_dn_w_sconv'], 'm_dn_a_log': out['m_dn_a_log'], 'm_dn_dt_bias': out['m_dn_dt_bias'], 'm_dn_o_g': out['m_dn_o_g'], 'm_dn_w_out': out['m_dn_w_out'], 'm_final_g': out['m_final_g'], 'v_norm_g': out['v_norm_g'], 'v_w_ada': out['v_w_ada'], 'v_b_ada': out['v_b_ada'], 'v_w_ffn_in': out['v_w_ffn_in'], 'v_w_ffn_out': out['v_w_ffn_out'], 'v_cm_w_glu': out['v_cm_w_glu'], 'v_cm_b_glu': out['v_cm_b_glu'], 'v_cm_w_dw': out['v_cm_w_dw'], 'v_cm_b_dw': out['v_cm_b_dw'], 'v_cm_ln_g': out['v_cm_ln_g'], 'v_cm_ln_b': out['v_cm_ln_b'], 'v_cm_w_pw': out['v_cm_w_pw'], 'v_cm_b_pw': out['v_cm_b_pw'], 'v_dn_w_in': out['v_dn_w_in'], 'v_dn_w_sconv': out['v_dn_w_sconv'], 'v_dn_a_log': out['v_dn_a_log'], 'v_dn_dt_bias': out['v_dn_dt_bias'], 'v_dn_o_g': out['v_dn_o_g'], 'v_dn_w_out': out['v_dn_w_out'], 'v_final_g': out['v_final_g']}


def _loss(weights, diff, rest, loss_target):
    with _jax.named_scope("forward"):
        args = {**rest, TWIN_DIFF_INPUT: diff, **{k: w.astype(_WEIGHT_DTYPES[k]) for k, w in weights.items()}}
        y = _forward(args)
    with _jax.named_scope("loss_head"):
        err = _jnp.square(y.astype(_jnp.float32) - loss_target)
        return 0.5 * _jnp.sum(_jnp.mean(err, axis=-1)) if err.ndim else 0.5 * err


def _adamw(w, g, m, v):
    m = ADAM_B1 * m + (1.0 - ADAM_B1) * g
    v = ADAM_B2 * v + (1.0 - ADAM_B2) * _jnp.square(g)
    m_hat = m / (1.0 - ADAM_B1 ** ADAM_STEP)
    v_hat = v / (1.0 - ADAM_B2 ** ADAM_STEP)
    delta = -ADAM_LR * (m_hat / (_jnp.sqrt(v_hat) + ADAM_EPS) + ADAM_WD * w)
    return delta, m, v


def reference(x, c, norm_g, w_ada, b_ada, w_ffn_in, w_ffn_out, cm_w_glu, cm_b_glu, cm_w_dw, cm_b_dw, cm_ln_g, cm_ln_b, cm_w_pw, cm_b_pw, dn_w_in, dn_w_sconv, dn_a_log, dn_dt_bias, dn_o_g, dn_w_out, final_g, loss_target, m_norm_g, m_w_ada, m_b_ada, m_w_ffn_in, m_w_ffn_out, m_cm_w_glu, m_cm_b_glu, m_cm_w_dw, m_cm_b_dw, m_cm_ln_g, m_cm_ln_b, m_cm_w_pw, m_cm_b_pw, m_dn_w_in, m_dn_w_sconv, m_dn_a_log, m_dn_dt_bias, m_dn_o_g, m_dn_w_out, m_final_g, v_norm_g, v_w_ada, v_b_ada, v_w_ffn_in, v_w_ffn_out, v_cm_w_glu, v_cm_b_glu, v_cm_w_dw, v_cm_b_dw, v_cm_ln_g, v_cm_ln_b, v_cm_w_pw, v_cm_b_pw, v_dn_w_in, v_dn_w_sconv, v_dn_a_log, v_dn_dt_bias, v_dn_o_g, v_dn_w_out, v_final_g):
    given = dict(x=x, c=c, norm_g=norm_g, w_ada=w_ada, b_ada=b_ada, w_ffn_in=w_ffn_in, w_ffn_out=w_ffn_out, cm_w_glu=cm_w_glu, cm_b_glu=cm_b_glu, cm_w_dw=cm_w_dw, cm_b_dw=cm_b_dw, cm_ln_g=cm_ln_g, cm_ln_b=cm_ln_b, cm_w_pw=cm_w_pw, cm_b_pw=cm_b_pw, dn_w_in=dn_w_in, dn_w_sconv=dn_w_sconv, dn_a_log=dn_a_log, dn_dt_bias=dn_dt_bias, dn_o_g=dn_o_g, dn_w_out=dn_w_out, final_g=final_g, loss_target=loss_target, m_norm_g=m_norm_g, m_w_ada=m_w_ada, m_b_ada=m_b_ada, m_w_ffn_in=m_w_ffn_in, m_w_ffn_out=m_w_ffn_out, m_cm_w_glu=m_cm_w_glu, m_cm_b_glu=m_cm_b_glu, m_cm_w_dw=m_cm_w_dw, m_cm_b_dw=m_cm_b_dw, m_cm_ln_g=m_cm_ln_g, m_cm_ln_b=m_cm_ln_b, m_cm_w_pw=m_cm_w_pw, m_cm_b_pw=m_cm_b_pw, m_dn_w_in=m_dn_w_in, m_dn_w_sconv=m_dn_w_sconv, m_dn_a_log=m_dn_a_log, m_dn_dt_bias=m_dn_dt_bias, m_dn_o_g=m_dn_o_g, m_dn_w_out=m_dn_w_out, m_final_g=m_final_g, v_norm_g=v_norm_g, v_w_ada=v_w_ada, v_b_ada=v_b_ada, v_w_ffn_in=v_w_ffn_in, v_w_ffn_out=v_w_ffn_out, v_cm_w_glu=v_cm_w_glu, v_cm_b_glu=v_cm_b_glu, v_cm_w_dw=v_cm_w_dw, v_cm_b_dw=v_cm_b_dw, v_cm_ln_g=v_cm_ln_g, v_cm_ln_b=v_cm_ln_b, v_cm_w_pw=v_cm_w_pw, v_cm_b_pw=v_cm_b_pw, v_dn_w_in=v_dn_w_in, v_dn_w_sconv=v_dn_w_sconv, v_dn_a_log=v_dn_a_log, v_dn_dt_bias=v_dn_dt_bias, v_dn_o_g=v_dn_o_g, v_dn_w_out=v_dn_w_out, v_final_g=v_final_g)
    weights = {n: given[n] for n in TWIN_WEIGHTS}
    shared = {n: given[n] for n in SHARED_INPUTS}
    per_example = {n: given[n] for n in ['x', 'c']}
    grad_fn = _jax.value_and_grad(_loss, argnums=(0, 1))

    def one_microbatch(ex, loss_target):
        ex = dict(ex)
        diff = ex.pop(TWIN_DIFF_INPUT)
        return grad_fn(weights, diff, {**shared, **ex}, loss_target)

    if N_MICROBATCH == 1:
        loss, (grad_w, grad_x) = one_microbatch(per_example, given["loss_target"])
    else:
        def body(carry, xs):
            loss_sum, grad_sum = carry
            l_k, (gw_k, gx_k) = one_microbatch(xs[0], xs[1])
            with _jax.named_scope("update"):
                return (loss_sum + l_k, _jax.tree.map(_jnp.add, grad_sum, gw_k)), gx_k

        init = (_jnp.zeros((), _jnp.float32), _jax.tree.map(_jnp.zeros_like, weights))
        (loss, grad_w), grad_x = _jax.lax.scan(body, init, (per_example, given["loss_target"]))
    with _jax.named_scope("update"):
        delta_w, new_m, new_v = {}, {}, {}
        for n in TWIN_WEIGHTS:
            delta_w[n], new_m[n], new_v[n] = _adamw(weights[n], grad_w[n], given["m_" + n], given["v_" + n])
    return (loss, grad_x, *[grad_w[n] for n in TWIN_WEIGHTS], *[delta_w[n] for n in TWIN_WEIGHTS],
            *[new_m[n] for n in TWIN_WEIGHTS], *[new_v[n] for n in TWIN_WEIGHTS])
```

```python
import functools

import jax
import jax.numpy as jnp
from jax import lax
from jax.experimental import pallas as pl
from jax.experimental.pallas import tpu as pltpu

F32 = jnp.float32
BF16 = jnp.bfloat16
HI = lax.Precision.HIGHEST
MESH = pl.DeviceIdType.MESH
AXES = ("x", "y", "c")

NDEV = 8
D = 1024
T = 2048
BL = 2
FF = 2816
NH = 8
DH = 128
CW = 31
SCW = 4
CHUNK = 64
DEPTH = 4
EPS = 1e-6
LR, B1, B2, AEPS, WD, STEP = 0.001, 0.9, 0.999, 1e-08, 0.01, 10

VMEM_LIMIT_BYTES = 56 * 1024 * 1024
HALO = 32
SHALO = 8


def _pcall(body, **kw):
    return pl.pallas_call(body, **kw)


def _cp(sem=None):
    return pltpu.CompilerParams(dimension_semantics=sem, vmem_limit_bytes=VMEM_LIMIT_BYTES)


def _sds(shape, dtype):
    return jax.ShapeDtypeStruct(tuple(shape), dtype)


def _dot(a, b):
    return jnp.dot(a, b, preferred_element_type=F32)


def _dot_nt(a, b):
    return lax.dot_general(a, b, (((1,), (1,)), ((), ())), preferred_element_type=F32)


def _dot_tn(a, b):
    return lax.dot_general(a, b, (((0,), (0,)), ((), ())), preferred_element_type=F32)


def _modulate(x, ng, scale, shift):
    r = lax.rsqrt(jnp.mean(x * x, axis=-1, keepdims=True) + EPS)
    return (x * r * ng) * (1.0 + scale) + shift


def _acc_rows(ref, first, rows):
    @pl.when(first)
    def _():
        ref[...] = jnp.zeros_like(ref)

    for r, val in enumerate(rows):
        ref[r:r + 1, :] += val


def _my_pos():
    return lax.axis_index("x"), lax.axis_index("y"), lax.axis_index("c")


def _all_gather(arrs, name):
    n = len(arrs)

    def body(*refs):
        ins, outs = refs[:n], refs[n:2 * n]
        send, recv, loc = refs[2 * n:]
        x, y, c = _my_pos()
        me, sibling = (x, y, c), (x, y, 1 - c)
        chips = [(1 - x, y), (x, 1 - y), (1 - x, 1 - y)]

        def copy(a, k, block, to, src=None):
            dst = outs[a].at[4 * block[0] + 2 * block[1] + block[2]]
            return pltpu.make_async_remote_copy(
                src_ref=dst if src is None else src, dst_ref=dst,
                send_sem=send.at[7 * a + k], recv_sem=recv.at[7 * a + k],
                device_id=to, device_id_type=MESH)

        mine, first, passed = [], [], []
        for a in range(n):
            m = pltpu.make_async_copy(ins[a], outs[a].at[4 * x + 2 * y + c], loc.at[a])
            m.start()
            mine.append(m)
            f = [copy(a, 0, me, sibling, src=ins[a])]
            f += [copy(a, 1 + j, me, (*chip, c), src=ins[a]) for j, chip in enumerate(chips)]
            for cp in f:
                cp.start()
            first += f
        for a in range(n):
            for j, chip in enumerate(chips):
                copy(a, 1 + j, (*chip, c), me).wait_recv()
                p = copy(a, 4 + j, (*chip, c), sibling)
                p.start()
                passed.append(p)
        for a in range(n):
            copy(a, 0, sibling, me).wait_recv()
            for j, chip in enumerate(chips):
                copy(a, 4 + j, (*chip, 1 - c), me).wait_recv()
        for cp in first + passed:
            cp.wait_send()
        for m in mine:
            m.wait()

    hbm = pl.BlockSpec(memory_space=pl.ANY)
    return _pcall(
        body, name=name,
        out_shape=[_sds((NDEV,) + a.shape, a.dtype) for a in arrs],
        in_specs=[hbm] * n, out_specs=[hbm] * n,
        scratch_shapes=[pltpu.SemaphoreType.DMA((7 * n,)), pltpu.SemaphoreType.DMA((7 * n,)),
                        pltpu.SemaphoreType.DMA((n,))],
    )(*arrs)


def _all_to_all(arrs, name):
    n = len(arrs)

    def body(*refs):
        ins, outs = refs[:n], refs[n:2 * n]
        send, recv, loc = refs[2 * n:]
        x, y, c = _my_pos()
        me = 4 * x + 2 * y + c

        def peer(k):
            px = 1 - x if k & 4 else x
            py = 1 - y if k & 2 else y
            pc = 1 - c if k & 1 else c
            return px, py, pc

        def copy(a, k):
            p = peer(k)
            return pltpu.make_async_remote_copy(
                src_ref=ins[a].at[4 * p[0] + 2 * p[1] + p[2]], dst_ref=outs[a].at[me],
                send_sem=send.at[7 * a + k - 1], recv_sem=recv.at[7 * a + k - 1],
                device_id=p, device_id_type=MESH)

        def arrival(a, k):
            p = peer(k)
            dst = outs[a].at[4 * p[0] + 2 * p[1] + p[2]]
            return pltpu.make_async_remote_copy(
                src_ref=dst, dst_ref=dst, send_sem=send.at[7 * a + k - 1], recv_sem=recv.at[7 * a + k - 1],
                device_id=p, device_id_type=MESH)

        mine, sent = [], []
        for a in range(n):
            m = pltpu.make_async_copy(ins[a].at[me], outs[a].at[me], loc.at[a])
            m.start()
            mine.append(m)
            for k in range(1, 8):
                cp = copy(a, k)
                cp.start()
                sent.append(cp)
        for a in range(n):
            for k in range(1, 8):
                arrival(a, k).wait_recv()
        for cp in sent:
            cp.wait_send()
        for m in mine:
            m.wait()

    hbm = pl.BlockSpec(memory_space=pl.ANY)
    return _pcall(
        body, name=name,
        out_shape=[_sds(a.shape, a.dtype) for a in arrs],
        in_specs=[hbm] * n, out_specs=[hbm] * n,
        scratch_shapes=[pltpu.SemaphoreType.DMA((7 * n,)), pltpu.SemaphoreType.DMA((7 * n,)),
                        pltpu.SemaphoreType.DMA((n,))],
    )(*arrs)


def _ffn_tiles():
    tm = min(512, T)
    return tm, T // tm


def _ffn_fwd(x, ssg, ng, w_in, w_out):
    n = x.shape[0]
    _, nf, _, tf = w_in.shape
    tm, tpb = _ffn_tiles()

    def body(x_ref, ssg_ref, ng_ref, win_ref, wout_ref, xn_ref, gu_ref, hid_ref, y_ref, h_scr, acc):
        j = pl.program_id(1)

        @pl.when(j == 0)
        def _():
            s = ssg_ref[0]
            h_scr[...] = _modulate(x_ref[...], ng_ref[...], s[1:2], s[0:1]).astype(BF16)
            acc[...] = jnp.zeros_like(acc)

        h = h_scr[...]
        g = _dot(h, win_ref[0])
        u = _dot(h, win_ref[1])
        gu_ref[0] = g.astype(BF16)
        gu_ref[1] = u.astype(BF16)
        hid = (g * jax.nn.sigmoid(g) * u).astype(BF16)
        hid_ref[...] = hid
        acc[...] += _dot(hid, wout_ref[...])

        @pl.when(j == nf - 1)
        def _():
            yv = acc[...]
            y_ref[...] = yv.astype(BF16)
            xn_ref[...] = x_ref[...] + (0.5 * (1.0 + ssg_ref[0][2:3])) * yv

    return _pcall(
        body, name="ffn_fwd", grid=(n // tm, nf),
        in_specs=[pl.BlockSpec((tm, D), lambda i, j: (i, 0)),
                  pl.BlockSpec((1, 3, D), lambda i, j: (i // tpb, 0, 0)),
                  pl.BlockSpec((1, D), lambda i, j: (0, 0)),
                  pl.BlockSpec((2, None, D, tf), lambda i, j: (0, j, 0, 0)),
                  pl.BlockSpec((None, tf, D), lambda i, j: (j, 0, 0))],
        out_specs=[pl.BlockSpec((tm, D), lambda i, j: (i, 0)),
                   pl.BlockSpec((2, None, tm, tf), lambda i, j: (0, j, i, 0)),
                   pl.BlockSpec((None, tm, tf), lambda i, j: (j, i, 0)),
                   pl.BlockSpec((tm, D), lambda i, j: (i, 0))],
        out_shape=[_sds((n, D), F32), _sds((2, nf, n, tf), BF16), _sds((nf, n, tf), BF16), _sds((n, D), BF16)],
        scratch_shapes=[pltpu.VMEM((tm, D), BF16), pltpu.VMEM((tm, D), F32)],
        compiler_params=_cp(("arbitrary", "arbitrary")),
    )(x, ssg, ng, w_in, w_out)


def _ffn_bwd_a(x, dxn, ssg, ng, y, gu, w_in, w_out):
    n = x.shape[0]
    _, nf, _, tf = w_in.shape
    tm, tpb = _ffn_tiles()

    def body(x_ref, dxn_ref, ssg_ref, ng_ref, y_ref, gu_ref, win_ref, wout_ref,
             dx_ref, dgu_ref, h_ref, dout_ref, dssg_ref, dng_ref, dout_scr, dh_acc):
        i, j = pl.program_id(0), pl.program_id(1)

        @pl.when(j == 0)
        def _():
            db = ((0.5 * (1.0 + ssg_ref[0][2:3])) * dxn_ref[...]).astype(BF16)
            dout_scr[...] = db
            dout_ref[...] = db
            dh_acc[...] = jnp.zeros_like(dh_acc)

        dhid = _dot_nt(dout_scr[...], wout_ref[...])
        g = gu_ref[0].astype(F32)
        u = gu_ref[1].astype(F32)
        sig = jax.nn.sigmoid(g)
        dg = (dhid * u * (sig * (1.0 + g * (1.0 - sig)))).astype(BF16)
        du = (dhid * (g * sig)).astype(BF16)
        dgu_ref[0] = dg
        dgu_ref[1] = du
        dh_acc[...] += _dot_nt(dg, win_ref[0]) + _dot_nt(du, win_ref[1])

        @pl.when(j == nf - 1)
        def _():
            s = ssg_ref[0]
            h, vjp = jax.vjp(_modulate, x_ref[...], ng_ref[...], s[1:2], s[0:1])
            dx_, dng_, dsc_, dsh_ = vjp(dh_acc[...])
            h_ref[...] = h.astype(BF16)
            dxn = dxn_ref[...]
            dx_ref[...] = dxn + dx_
            dgate = jnp.sum(0.5 * dxn * y_ref[...].astype(F32), axis=0, keepdims=True)
            _acc_rows(dssg_ref.at[0], i % tpb == 0, [dsh_, dsc_, dgate])
            _acc_rows(dng_ref, i == 0, [dng_])

    return _pcall(
        body, name="ffn_bwd_a", grid=(n // tm, nf),
        in_specs=[pl.BlockSpec((tm, D), lambda i, j: (i, 0)),
                  pl.BlockSpec((tm, D), lambda i, j: (i, 0)),
                  pl.BlockSpec((1, 3, D), lambda i, j: (i // tpb, 0, 0)),
                  pl.BlockSpec((1, D), lambda i, j: (0, 0)),
                  pl.BlockSpec((tm, D), lambda i, j: (i, 0)),
                  pl.BlockSpec((2, None, tm, tf), lambda i, j: (0, j, i, 0)),
                  pl.BlockSpec((2, None, D, tf), lambda i, j: (0, j, 0, 0)),
                  pl.BlockSpec((None, tf, D), lambda i, j: (j, 0, 0))],
        out_specs=[pl.BlockSpec((tm, D), lambda i, j: (i, 0)),
                   pl.BlockSpec((2, None, tm, tf), lambda i, j: (0, j, i, 0)),
                   pl.BlockSpec((tm, D), lambda i, j: (i, 0)),
                   pl.BlockSpec((tm, D), lambda i, j: (i, 0)),
                   pl.BlockSpec((1, 3, D), lambda i, j: (i // tpb, 0, 0)),
                   pl.BlockSpec((1, D), lambda i, j: (0, 0))],
        out_shape=[_sds((n, D), F32), _sds((2, nf, n, tf), BF16), _sds((n, D), BF16), _sds((n, D), BF16),
                   _sds((BL, 3, D), F32), _sds((1, D), F32)],
        scratch_shapes=[pltpu.VMEM((tm, D), BF16), pltpu.VMEM((tm, D), F32)],
        compiler_params=_cp(("arbitrary", "arbitrary")),
    )(x, dxn, ssg, ng, y, gu, w_in, w_out)


def _ffn_bwd_w(h, dgu, hid, dout):
    n = h.shape[0]
    _, nf, _, tf = dgu.shape
    tm, _ = _ffn_tiles()
    ni = n // tm

    def body(h_ref, dgu_ref, hid_ref, dout_ref, dwin_ref, dwout_ref, acc_g, acc_u, acc_o):
        i = pl.program_id(1)

        @pl.when(i == 0)
        def _():
            acc_g[...] = jnp.zeros_like(acc_g)
            acc_u[...] = jnp.zeros_like(acc_u)
            acc_o[...] = jnp.zeros_like(acc_o)

        hv = h_ref[...]
        acc_g[...] += _dot_tn(hv, dgu_ref[0])
        acc_u[...] += _dot_tn(hv, dgu_ref[1])
        acc_o[...] += _dot_tn(hid_ref[...], dout_ref[...])

        @pl.when(i == ni - 1)
        def _():
            dwin_ref[0] = acc_g[...].astype(BF16)
            dwin_ref[1] = acc_u[...].astype(BF16)
            dwout_ref[...] = acc_o[...].astype(BF16)

    return _pcall(
        body, name="ffn_bwd_w", grid=(nf, ni),
        in_specs=[pl.BlockSpec((tm, D), lambda j, i: (i, 0)),
                  pl.BlockSpec((2, None, tm, tf), lambda j, i: (0, j, i, 0)),
                  pl.BlockSpec((None, tm, tf), lambda j, i: (j, i, 0)),
                  pl.BlockSpec((tm, D), lambda j, i: (i, 0))],
        out_specs=[pl.BlockSpec((2, None, D, tf), lambda j, i: (0, j, 0, 0)),
                   pl.BlockSpec((None, tf, D), lambda j, i: (j, 0, 0))],
        out_shape=[_sds((2, nf, D, tf), BF16), _sds((nf, tf, D), BF16)],
        scratch_shapes=[pltpu.VMEM((D, tf), F32), pltpu.VMEM((D, tf), F32), pltpu.VMEM((tf, D), F32)],
        compiler_params=_cp(("arbitrary", "arbitrary")),
    )(h, dgu, hid, dout)


def _premod_matmul(x, ssg, ng, w, bias, tn):
    n = x.shape[0]
    m = w.shape[1]
    tm = min(256, T)
    tpb = T // tm

    def body(x_ref, ssg_ref, ng_ref, w_ref, b_ref, h_ref, o_ref, h_scr):
        @pl.when(pl.program_id(1) == 0)
        def _():
            s = ssg_ref[0]
            hb = _modulate(x_ref[...], ng_ref[...], s[1:2], s[0:1]).astype(BF16)
            h_scr[...] = hb
            h_ref[...] = hb

        o_ref[...] = _dot(h_scr[...], w_ref[...]) + b_ref[...]

    return _pcall(
        body, name="premod_matmul", grid=(n // tm, m // tn),
        in_specs=[pl.BlockSpec((tm, D), lambda i, j: (i, 0)),
                  pl.BlockSpec((1, 3, D), lambda i, j: (i // tpb, 0, 0)),
                  pl.BlockSpec((1, D), lambda i, j: (0, 0)),
                  pl.BlockSpec((D, tn), lambda i, j: (0, j)),
                  pl.BlockSpec((1, tn), lambda i, j: (0, j))],
        out_specs=[pl.BlockSpec((tm, D), lambda i, j: (i, 0)),
                   pl.BlockSpec((tm, tn), lambda i, j: (i, j))],
        out_shape=[_sds((n, D), BF16), _sds((n, m), F32)],
        scratch_shapes=[pltpu.VMEM((tm, D), BF16)],
        compiler_params=_cp(("arbitrary", "arbitrary")),
    )(x, ssg, ng, w, bias)


def _premod_matmul_bwd(x, dxn, ssg, ng, douts, ws):
    n = x.shape[0]
    k = len(douts)
    tm = min(256, T)
    tpb = T // tm

    def body(*refs):
        x_ref, dxn_ref, ssg_ref, ng_ref = refs[:4]
        do_refs, w_refs = refs[4:4 + k], refs[4 + k:4 + 2 * k]
        dx_ref, dssg_ref, dng_ref = refs[4 + 2 * k:]
        i = pl.program_id(0)
        dh = _dot_nt(do_refs[0][...], w_refs[0][...])
        for q in range(1, k):
            dh += _dot_nt(do_refs[q][...], w_refs[q][...])
        s = ssg_ref[0]
        _, vjp = jax.vjp(_modulate, x_ref[...], ng_ref[...], s[1:2], s[0:1])
        dx_, dng_, dsc_, dsh_ = vjp(dh)
        dx_ref[...] = dxn_ref[...] + dx_
        _acc_rows(dssg_ref.at[0], i % tpb == 0, [dsh_, dsc_, jnp.zeros_like(dsh_)])
        _acc_rows(dng_ref, i == 0, [dng_])

    return _pcall(
        body, name="premod_matmul_bwd", grid=(n // tm,),
        in_specs=[pl.BlockSpec((tm, D), lambda i: (i, 0)),
                  pl.BlockSpec((tm, D), lambda i: (i, 0)),
                  pl.BlockSpec((1, 3, D), lambda i: (i // tpb, 0, 0)),
                  pl.BlockSpec((1, D), lambda i: (0, 0))]
                 + [pl.BlockSpec((tm, a.shape[1]), lambda i: (i, 0)) for a in douts]
                 + [pl.BlockSpec(w.shape, lambda i: (0, 0)) for w in ws],
        out_specs=[pl.BlockSpec((tm, D), lambda i: (i, 0)),
                   pl.BlockSpec((1, 3, D), lambda i: (i // tpb, 0, 0)),
                   pl.BlockSpec((1, D), lambda i: (0, 0))],
        out_shape=[_sds((n, D), F32), _sds((BL, 3, D), F32), _sds((1, D), F32)],
        compiler_params=_cp(("arbitrary",)),
    )(x, dxn, ssg, ng, *douts, *ws)


def _matmul_res(x, a, ssg, w, bias):
    n, kd = a.shape
    tm = min(512, T)
    tpb = T // tm

    def body(x_ref, a_ref, ssg_ref, w_ref, b_ref, xn_ref, y_ref):
        yv = _dot(a_ref[...], w_ref[...]) + b_ref[...]
        y_ref[...] = yv.astype(BF16)
        xn_ref[...] = x_ref[...] + (1.0 + ssg_ref[0][2:3]) * yv

    return _pcall(
        body, name="matmul_res", grid=(n // tm,),
        in_specs=[pl.BlockSpec((tm, D), lambda i: (i, 0)),
                  pl.BlockSpec((tm, kd), lambda i: (i, 0)),
                  pl.BlockSpec((1, 3, D), lambda i: (i // tpb, 0, 0)),
                  pl.BlockSpec((kd, D), lambda i: (0, 0)),
                  pl.BlockSpec((1, D), lambda i: (0, 0))],
        out_specs=[pl.BlockSpec((tm, D), lambda i: (i, 0)), pl.BlockSpec((tm, D), lambda i: (i, 0))],
        out_shape=[_sds((n, D), F32), _sds((n, D), BF16)],
        compiler_params=_cp(("arbitrary",)),
    )(x, a, ssg, w, bias)


def _matmul_res_bwd(dxn, y, ssg, w):
    n = dxn.shape[0]
    kd = w.shape[0]
    tm = min(512, T)
    tpb = T // tm

    def body(dxn_ref, y_ref, ssg_ref, w_ref, da_ref, dy_ref, dgate_ref, dbias_ref):
        i = pl.program_id(0)
        dxn = dxn_ref[...]
        dy = (1.0 + ssg_ref[0][2:3]) * dxn
        dyb = dy.astype(BF16)
        dy_ref[...] = dyb
        da_ref[...] = _dot_nt(dyb, w_ref[...])
        _acc_rows(dgate_ref.at[0], i % tpb == 0, [jnp.sum(dxn * y_ref[...].astype(F32), axis=0, keepdims=True)])
        _acc_rows(dbias_ref, i == 0, [jnp.sum(dy, axis=0, keepdims=True)])

    return _pcall(
        body, name="matmul_res_bwd", grid=(n // tm,),
        in_specs=[pl.BlockSpec((tm, D), lambda i: (i, 0)),
                  pl.BlockSpec((tm, D), lambda i: (i, 0)),
                  pl.BlockSpec((1, 3, D), lambda i: (i // tpb, 0, 0)),
                  pl.BlockSpec((kd, D), lambda i: (0, 0))],
        out_specs=[pl.BlockSpec((tm, kd), lambda i: (i, 0)),
                   pl.BlockSpec((tm, D), lambda i: (i, 0)),
                   pl.BlockSpec((1, 1, D), lambda i: (i // tpb, 0, 0)),
                   pl.BlockSpec((1, D), lambda i: (0, 0))],
        out_shape=[_sds((n, kd), F32), _sds((n, D), BF16), _sds((BL, 1, D), F32), _sds((1, D), F32)],
        compiler_params=_cp(("arbitrary",)),
    )(dxn, y, ssg, w)


def _wgrad(a, b):
    n, kd = a.shape
    m = b.shape[1]
    tm = min(512, T)
    tk = min(512, kd)
    ni = n // tm

    def body(a_ref, b_ref, o_ref, acc):
        i = pl.program_id(1)

        @pl.when(i == 0)
        def _():
            acc[...] = jnp.zeros_like(acc)

        acc[...] += _dot_tn(a_ref[...], b_ref[...])

        @pl.when(i == ni - 1)
        def _():
            o_ref[...] = acc[...].astype(BF16)

    return _pcall(
        body, name="wgrad", grid=(kd // tk, ni),
        in_specs=[pl.BlockSpec((tm, tk), lambda q, i: (i, q)), pl.BlockSpec((tm, m), lambda q, i: (i, 0))],
        out_specs=pl.BlockSpec((tk, m), lambda q, i: (q, 0)),
        out_shape=_sds((kd, m), BF16),
        scratch_shapes=[pltpu.VMEM((tk, m), F32)],
        compiler_params=_cp(("arbitrary", "arbitrary")),
    )(a, b)


def _ln_silu(u1, g, b):
    mu = jnp.mean(u1, axis=-1, keepdims=True)
    xc = u1 - mu
    var = jnp.mean(xc * xc, axis=-1, keepdims=True)
    ln = xc * lax.rsqrt(var + EPS) * g + b
    return ln * jax.nn.sigmoid(ln)


def _conv_tiles():
    tt = min(256, T)
    return tt, T // tt


def _prev_halo_spec(cols, tt, halo):
    r = tt // halo
    return pl.BlockSpec((halo, cols), lambda b, i: (jnp.maximum(b * (T // halo) + i * r - 1, 0), 0))


def _next_halo_spec(cols, tt, halo):
    r = tt // halo
    last = BL * T // halo - 1
    return pl.BlockSpec((halo, cols), lambda b, i: (jnp.minimum(b * (T // halo) + (i + 1) * r, last), 0))


def _cm_mid_fwd(ab, w_dw, b_dw, ln_g, ln_b):
    n = ab.shape[0]
    tt, nt = _conv_tiles()

    def body(ab_ref, halo_ref, w_ref, bdw_ref, g_ref, b_ref, u1_ref, u2_ref, win):
        i = pl.program_id(1)
        hv = halo_ref[...]
        u0h = hv[:, :D] * jax.nn.sigmoid(hv[:, D:])
        win[0:HALO, :] = jnp.where(i == 0, 0.0, u0h)
        cv = ab_ref[...]
        win[HALO:HALO + tt, :] = cv[:, :D] * jax.nn.sigmoid(cv[:, D:])
        acc = jnp.zeros((tt, D), F32) + bdw_ref[...]
        for k in range(CW):
            acc += w_ref[k:k + 1, :] * win[pl.ds(HALO - (CW - 1) + k, tt), :]
        u1_ref[...] = acc
        u2_ref[...] = _ln_silu(acc, g_ref[...], b_ref[...]).astype(BF16)

    row = lambda b, i: (b * nt + i, 0)
    vec = pl.BlockSpec((1, D), lambda b, i: (0, 0))
    return _pcall(
        body, name="cm_mid_fwd", grid=(BL, nt),
        in_specs=[pl.BlockSpec((tt, 2 * D), row), _prev_halo_spec(2 * D, tt, HALO),
                  pl.BlockSpec((HALO, D), lambda b, i: (0, 0)), vec, vec, vec],
        out_specs=[pl.BlockSpec((tt, D), row), pl.BlockSpec((tt, D), row)],
        out_shape=[_sds((n, D), F32), _sds((n, D), BF16)],
        scratch_shapes=[pltpu.VMEM((HALO + tt, D), F32)],
        compiler_params=_cp(("arbitrary", "arbitrary")),
    )(ab, ab, w_dw, b_dw, ln_g, ln_b)


def _cm_mid_bwd_a(du2, u1, ln_g, ln_b):
    n = du2.shape[0]
    tm = min(256, T)

    def body(du2_ref, u1_ref, g_ref, b_ref, du1_ref, dln_ref):
        _, vjp = jax.vjp(_ln_silu, u1_ref[...], g_ref[...], b_ref[...])
        du1, dg, db = vjp(du2_ref[...])
        du1_ref[...] = du1
        _acc_rows(dln_ref, pl.program_id(0) == 0, [dg, db])

    vec = pl.BlockSpec((1, D), lambda i: (0, 0))
    return _pcall(
        body, name="cm_mid_bwd_a", grid=(n // tm,),
        in_specs=[pl.BlockSpec((tm, D), lambda i: (i, 0)), pl.BlockSpec((tm, D), lambda i: (i, 0)), vec, vec],
        out_specs=[pl.BlockSpec((tm, D), lambda i: (i, 0)), pl.BlockSpec((2, D), lambda i: (0, 0))],
        out_shape=[_sds((n, D), F32), _sds((2, D), F32)],
        compiler_params=_cp(("arbitrary",)),
    )(du2, u1, ln_g, ln_b)


def _cm_mid_bwd_b(du1, ab, w_dw):
    n = du1.shape[0]
    tt, nt = _conv_tiles()

    def body(du1_ref, nxt_ref, ab_ref, halo_ref, w_ref, dab_ref, dw_ref, dbdw_ref, dbglu_ref, dwin, uwin):
        b, i = pl.program_id(0), pl.program_id(1)
        first = jnp.logical_and(b == 0, i == 0)
        d1 = du1_ref[...]
        dwin[0:tt, :] = d1
        dwin[tt:tt + HALO, :] = jnp.where(i == nt - 1, 0.0, nxt_ref[...])
        hv = halo_ref[...]
        uwin[0:HALO, :] = jnp.where(i == 0, 0.0, hv[:, :D] * jax.nn.sigmoid(hv[:, D:]))
        cv = ab_ref[...]
        av, sg = cv[:, :D], jax.nn.sigmoid(cv[:, D:])
        uwin[HALO:HALO + tt, :] = av * sg
        du0 = jnp.zeros((tt, D), F32)
        dws = []
        for k in range(CW):
            du0 += w_ref[k:k + 1, :] * dwin[pl.ds(CW - 1 - k, tt), :]
            dws.append(jnp.sum(d1 * uwin[pl.ds(HALO - (CW - 1) + k, tt), :], axis=0, keepdims=True))
        dws += [jnp.zeros((1, D), F32)] * (HALO - CW)
        _acc_rows(dw_ref, first, dws)
        _acc_rows(dbdw_ref, first, [jnp.sum(d1, axis=0, keepdims=True)])
        da = du0 * sg
        db = du0 * av * sg * (1.0 - sg)
        dab_ref[:, :D] = da.astype(BF16)
        dab_ref[:, D:] = db.astype(BF16)
        _acc_rows(dbglu_ref.at[:, 0:D], first, [jnp.sum(da, axis=0, keepdims=True)])
        _acc_rows(dbglu_ref.at[:, D:2 * D], first, [jnp.sum(db, axis=0, keepdims=True)])

    row = lambda b, i: (b * nt + i, 0)
    return _pcall(
        body, name="cm_mid_bwd_b", grid=(BL, nt),
        in_specs=[pl.BlockSpec((tt, D), row), _next_halo_spec(D, tt, HALO),
                  pl.BlockSpec((tt, 2 * D), row), _prev_halo_spec(2 * D, tt, HALO),
                  pl.BlockSpec((HALO, D), lambda b, i: (0, 0))],
        out_specs=[pl.BlockSpec((tt, 2 * D), row), pl.BlockSpec((HALO, D), lambda b, i: (0, 0)),
                   pl.BlockSpec((1, D), lambda b, i: (0, 0)), pl.BlockSpec((1, 2 * D), lambda b, i: (0, 0))],
        out_shape=[_sds((n, 2 * D), BF16), _sds((HALO, D), F32), _sds((1, D), F32), _sds((1, 2 * D), F32)],
        scratch_shapes=[pltpu.VMEM((tt + HALO, D), F32), pltpu.VMEM((HALO + tt, D), F32)],
        compiler_params=_cp(("arbitrary", "arbitrary")),
    )(du1, du1, ab, ab, w_dw)


def _softplus(v):
    return jnp.maximum(v, 0.0) + jnp.log(1.0 + jnp.exp(-jnp.abs(v)))


def _g_beta(ab, alog, dtb):
    return -jnp.exp(alog) * _softplus(ab + dtb), jax.nn.sigmoid(ab)


def _dn_sconv_fwd(proj, w_sc, alog, dtb):
    n = proj.shape[0]
    tt, nt = _conv_tiles()
    w3 = 3 * D

    def body(qkv_ref, halo_ref, ab_ref, w_ref, alog_ref, dtb_ref, conv_ref, q_ref, k_ref, v_ref, gb_ref, bb_ref, win):
        i = pl.program_id(1)
        win[0:SHALO, :] = jnp.where(i == 0, 0.0, halo_ref[...])
        win[SHALO:SHALO + tt, :] = qkv_ref[...]
        acc = jnp.zeros((tt, w3), F32)
        for k in range(SCW):
            acc += w_ref[k:k + 1, :] * win[pl.ds(SHALO - (SCW - 1) + k, tt), :]
        conv_ref[...] = acc
        act = acc * jax.nn.sigmoid(acc)
        gfull, bfull = _g_beta(ab_ref[...], alog_ref[...], dtb_ref[...])
        for h in range(NH):
            q_ref[0, h] = act[:, h * DH:(h + 1) * DH]
            k_ref[0, h] = act[:, D + h * DH:D + (h + 1) * DH]
            v_ref[0, h] = act[:, 2 * D + h * DH:2 * D + (h + 1) * DH]
            gb_ref[0, h] = jnp.broadcast_to(gfull[:, h:h + 1], (tt, DH))
            bb_ref[0, h] = jnp.broadcast_to(bfull[:, NH + h:NH + h + 1], (tt, DH))

    row = lambda b, i: (b * nt + i, 0)
    head = pl.BlockSpec((1, NH, tt, DH), lambda b, i: (b, 0, i, 0))
    vec = pl.BlockSpec((1, 128), lambda b, i: (0, 0))
    hs = _sds((BL, NH, T, DH), F32)
    return _pcall(
        body, name="dn_sconv_fwd", grid=(BL, nt),
        in_specs=[pl.BlockSpec((tt, w3), row), _prev_halo_spec(w3, tt, SHALO),
                  pl.BlockSpec((tt, 128), lambda b, i: (b * nt + i, 4 * D // 128)),
                  pl.BlockSpec((SHALO, w3), lambda b, i: (0, 0)), vec, vec],
        out_specs=[pl.BlockSpec((tt, w3), row), head, head, head, head, head],
        out_shape=[_sds((n, w3), F32), hs, hs, hs, hs, hs],
        scratch_shapes=[pltpu.VMEM((SHALO + tt, w3), F32)],
        compiler_params=_cp(("arbitrary", "arbitrary")),
    )(proj, proj, proj, w_sc, alog, dtb)


def _bmm_raw(a, b):
    return jnp.einsum("gij,gjk->gik", a, b, preferred_element_type=F32, precision=HI)


def _bmm_nt_raw(a, b):
    return jnp.einsum("gid,gjd->gij", a, b, preferred_element_type=F32, precision=HI)


def _bmm_tn_raw(a, b):
    return jnp.einsum("gcd,gce->gde", a, b, preferred_element_type=F32, precision=HI)


@jax.custom_vjp
def _bmm(a, b):
    return _bmm_raw(a, b)


@jax.custom_vjp
def _bmm_nt(a, b):
    return _bmm_nt_raw(a, b)


@jax.custom_vjp
def _bmm_tn(a, b):
    return _bmm_tn_raw(a, b)


_bmm.defvjp(lambda a, b: (_bmm_raw(a, b), (a, b)),
            lambda r, dc: (_bmm_nt(dc, r[1]), _bmm_tn(r[0], dc)))
_bmm_nt.defvjp(lambda a, b: (_bmm_nt_raw(a, b), (a, b)),
               lambda r, dc: (_bmm(dc, r[1]), _bmm_tn(dc, r[0])))
_bmm_tn.defvjp(lambda a, b: (_bmm_tn_raw(a, b), (a, b)),
               lambda r, dc: (_bmm_nt(r[1], dc), _bmm(r[0], dc)))


@jax.custom_vjp
def _unit_lower_inverse(a):
    eye = (lax.broadcasted_iota(jnp.int32, a.shape, 1) == lax.broadcasted_iota(jnp.int32, a.shape, 2)).astype(F32)
    t = eye - a
    p = a
    for _ in range(CHUNK.bit_length() - 2):
        p = _bmm_raw(p, p)
        t = _bmm_raw(t, eye + p)
    return t


def _uli_fwd(a):
    t = _unit_lower_inverse(a)
    return t, t


def _uli_bwd(t, dt):
    return (-_bmm_nt(_bmm_tn(t, dt), t),)


_unit_lower_inverse.defvjp(_uli_fwd, _uli_bwd)


def _dn_pre(q, k, v, gb, bb):
    shape = (q.shape[0], CHUNK, CHUNK)
    ri = lax.broadcasted_iota(jnp.int32, shape, 1)
    ci = lax.broadcasted_iota(jnp.int32, shape, 2)
    causal, strict = ri >= ci, ri > ci
    qn = q * lax.rsqrt(jnp.sum(q * q, axis=-1, keepdims=True) + EPS) * (DH ** -0.5)
    kn = k * lax.rsqrt(jnp.sum(k * k, axis=-1, keepdims=True) + EPS)
    gcs = _bmm(causal.astype(F32), gb)
    gcol = gcs[:, :, :CHUNK]
    decay = jnp.exp(jnp.where(causal, gcol - jnp.swapaxes(gcol, 1, 2), -jnp.inf))
    eg = jnp.exp(gcs)
    kb = kn * bb
    a = jnp.where(strict, _bmm_nt(kb, kn) * decay, 0.0)
    tm = _unit_lower_inverse(a)
    u = _bmm(tm, v * bb)
    w = _bmm(tm, kb * eg)
    qg = qn * eg
    intra = _bmm_nt(qn, kn) * decay
    glast = gcs[:, CHUNK - 1:CHUNK, :]
    kd = kn * jnp.exp(glast - gcs)
    egl = jnp.broadcast_to(jnp.exp(glast), (q.shape[0], 8, DH))
    return u, w, qg, kd, intra, egl


def _pre_tiles():
    gcn = min(8, T // CHUNK)
    return gcn, T // (CHUNK * gcn)


def _dn_pre_specs():
    gcn, _ = _pre_tiles()
    tok = pl.BlockSpec((None, None, gcn * CHUNK, DH), lambda b, h, i: (b, h, i, 0))
    sq = pl.BlockSpec((None, None, gcn * CHUNK, CHUNK), lambda b, h, i: (b, h, i, 0))
    per = pl.BlockSpec((None, None, gcn * 8, DH), lambda b, h, i: (b, h, i, 0))
    return tok, sq, per


def _dn_pre_fwd(q, k, v, gb, bb):
    gcn, ng = _pre_tiles()
    tok, sq, per = _dn_pre_specs()

    def body(q_ref, k_ref, v_ref, gb_ref, bb_ref, u_ref, w_ref, qg_ref, kd_ref, in_ref, egl_ref):
        args = [r[...].reshape(gcn, CHUNK, DH) for r in (q_ref, k_ref, v_ref, gb_ref, bb_ref)]
        u, w, qg, kd, intra, egl = _dn_pre(*args)
        for r, val in ((u_ref, u), (w_ref, w), (qg_ref, qg), (kd_ref, kd)):
            r[...] = val.reshape(gcn * CHUNK, DH)
        in_ref[...] = intra.reshape(gcn * CHUNK, CHUNK)
        egl_ref[...] = egl.reshape(gcn * 8, DH)

    hs = _sds((BL, NH, T, DH), F32)
    return _pcall(
        body, name="dn_pre_fwd", grid=(BL, NH, ng),
        in_specs=[tok] * 5, out_specs=[tok, tok, tok, tok, sq, per],
        out_shape=[hs, hs, hs, hs, _sds((BL, NH, T, CHUNK), F32), _sds((BL, NH, T // CHUNK * 8, DH), F32)],
        compiler_params=_cp(("arbitrary",) * 3),
    )(q, k, v, gb, bb)


def _dn_pre_bwd(q, k, v, gb, bb, du, dw, dqg, dkd, dintra, degl):
    gcn, ng = _pre_tiles()
    tok, sq, per = _dn_pre_specs()

    def body(q_ref, k_ref, v_ref, gb_ref, bb_ref, du_ref, dw_ref, dqg_ref, dkd_ref, din_ref, degl_ref,
             dq_ref, dk_ref, dv_ref, dgb_ref, dbb_ref):
        args = [r[...].reshape(gcn, CHUNK, DH) for r in (q_ref, k_ref, v_ref, gb_ref, bb_ref)]
        _, vjp = jax.vjp(_dn_pre, *args)
        cts = [r[...].reshape(gcn, CHUNK, DH) for r in (du_ref, dw_ref, dqg_ref, dkd_ref)]
        de = degl_ref[...].reshape(gcn, 8, DH)
        one = jnp.logical_and(lax.broadcasted_iota(jnp.int32, de.shape, 1) == 0,
                              lax.broadcasted_iota(jnp.int32, de.shape, 2) == 0)
        outs = vjp((*cts, din_ref[...].reshape(gcn, CHUNK, CHUNK), jnp.where(one, de, 0.0)))
        for r, val in zip((dq_ref, dk_ref, dv_ref, dgb_ref, dbb_ref), outs):
            r[...] = val.reshape(gcn * CHUNK, DH)

    hs = _sds((BL, NH, T, DH), F32)
    return _pcall(
        body, name="dn_pre_bwd", grid=(BL, NH, ng),
        in_specs=[tok] * 9 + [sq, per], out_specs=[tok] * 5, out_shape=[hs] * 5,
        compiler_params=_cp(("arbitrary",) * 3),
    )(q, k, v, gb, bb, du, dw, dqg, dkd, dintra, degl)


def _scan_tiles():
    cs = min(2, T // CHUNK)
    return cs, T // (CHUNK * cs)


def _dn_scan_fwd(u, w, qg, kd, intra, egl):
    cs, ns = _scan_tiles()
    g = BL * NH
    nc = T // CHUNK

    def body(u_ref, w_ref, qg_ref, kd_ref, in_ref, egl_ref, o_ref, vn_ref, s0_ref, s_scr):
        @pl.when(pl.program_id(0) == 0)
        def _():
            s_scr[...] = jnp.zeros_like(s_scr)

        for c in range(cs):
            rows = pl.ds(c * CHUNK, CHUNK)
            s = s_scr[...]
            s0_ref[:, :, c] = s.reshape(BL, NH, DH, DH)

            def ld(r, m=DH):
                return r[:, :, rows, :].reshape(g, CHUNK, m)

            vn = ld(u_ref) - _bmm_raw(ld(w_ref), s)
            o = _bmm_raw(ld(qg_ref), s) + _bmm_raw(ld(in_ref, CHUNK), vn)
            e = egl_ref[:, :, pl.ds(c * 8, 1), :].reshape(g, 1, DH)
            s_scr[...] = s * e + _bmm_tn_raw(ld(kd_ref), vn)
            vn_ref[:, :, rows, :] = vn.reshape(BL, NH, CHUNK, DH)
            o_ref[:, :, rows, :] = o.reshape(BL, NH, CHUNK, DH)

    tok = pl.BlockSpec((BL, NH, cs * CHUNK, DH), lambda i: (0, 0, i, 0))
    hs = _sds((BL, NH, T, DH), F32)
    return _pcall(
        body, name="dn_scan_fwd", grid=(ns,),
        in_specs=[tok, tok, tok, tok, pl.BlockSpec((BL, NH, cs * CHUNK, CHUNK), lambda i: (0, 0, i, 0)),
                  pl.BlockSpec((BL, NH, cs * 8, DH), lambda i: (0, 0, i, 0))],
        out_specs=[tok, tok, pl.BlockSpec((BL, NH, cs, DH, DH), lambda i: (0, 0, i, 0, 0))],
        out_shape=[hs, hs, _sds((BL, NH, nc, DH, DH), F32)],
        scratch_shapes=[pltpu.VMEM((g, DH, DH), F32)],
        compiler_params=_cp(("arbitrary",)),
    )(u, w, qg, kd, intra, egl)


def _dn_scan_bwd(do, w, qg, kd, intra, egl, vn, s0):
    cs, ns = _scan_tiles()
    g = BL * NH
    nc = T // CHUNK

    def body(do_ref, w_ref, qg_ref, kd_ref, in_ref, egl_ref, vn_ref, s0_ref,
             du_ref, dw_ref, dqg_ref, dkd_ref, din_ref, degl_ref, ds_scr):
        @pl.when(pl.program_id(0) == 0)
        def _():
            ds_scr[...] = jnp.zeros_like(ds_scr)

        for c in reversed(range(cs)):
            rows = pl.ds(c * CHUNK, CHUNK)

            def ld(r, m=DH):
                return r[:, :, rows, :].reshape(g, CHUNK, m)

            def st(r, val, m=DH):
                r[:, :, rows, :] = val.reshape(BL, NH, CHUNK, m)

            s = s0_ref[:, :, c].reshape(g, DH, DH)
            ds = ds_scr[...]
            dov, vnv, kdv, wv, qgv, inv = ld(do_ref), ld(vn_ref), ld(kd_ref), ld(w_ref), ld(qg_ref), ld(in_ref, CHUNK)
            dv = _bmm_tn_raw(inv, dov) + _bmm_raw(kdv, ds)
            st(din_ref, _bmm_nt_raw(dov, vnv), CHUNK)
            st(dqg_ref, _bmm_nt_raw(dov, s))
            st(dkd_ref, _bmm_nt_raw(vnv, ds))
            st(du_ref, dv)
            st(dw_ref, -_bmm_nt_raw(dv, s))
            de = jnp.sum(jnp.sum(ds * s, axis=2, keepdims=True), axis=1, keepdims=True)
            degl_ref[:, :, pl.ds(c * 8, 8), :] = jnp.broadcast_to(de, (g, 8, DH)).reshape(BL, NH, 8, DH)
            e = egl_ref[:, :, pl.ds(c * 8, 1), :].reshape(g, 1, DH)
            ds_scr[...] = ds * e + _bmm_tn_raw(qgv, dov) - _bmm_tn_raw(wv, dv)

    rev = lambda i: (0, 0, ns - 1 - i, 0)
    tok = pl.BlockSpec((BL, NH, cs * CHUNK, DH), rev)
    sq = pl.BlockSpec((BL, NH, cs * CHUNK, CHUNK), rev)
    per = pl.BlockSpec((BL, NH, cs * 8, DH), rev)
    hs = _sds((BL, NH, T, DH), F32)
    return _pcall(
        body, name="dn_scan_bwd", grid=(ns,),
        in_specs=[tok, tok, tok, tok, sq, per, tok,
                  pl.BlockSpec((BL, NH, cs, DH, DH), lambda i: (0, 0, ns - 1 - i, 0, 0))],
        out_specs=[tok, tok, tok, tok, sq, per],
        out_shape=[hs, hs, hs, hs, _sds((BL, NH, T, CHUNK), F32), _sds((BL, NH, nc * 8, DH), F32)],
        scratch_shapes=[pltpu.VMEM((g, DH, DH), F32)],
        compiler_params=_cp(("arbitrary",)),
    )(do, w, qg, kd, intra, egl, vn, s0)


def _gated_norm(o_h, z_h, og):
    r = lax.rsqrt(jnp.mean(o_h * o_h, axis=-1, keepdims=True) + EPS)
    return (o_h * r * og) * (z_h * jax.nn.sigmoid(z_h))


def _dn_gnorm_fwd(o, proj, o_g):
    tm = min(256, T)
    nt = T // tm

    def body(o_ref, z_ref, g_ref, og_ref):
        z = z_ref[...]
        for h in range(NH):
            og_ref[:, h * DH:(h + 1) * DH] = _gated_norm(o_ref[0, h], z[:, h * DH:(h + 1) * DH], g_ref[...]).astype(BF16)

    return _pcall(
        body, name="dn_gnorm_fwd", grid=(BL, nt),
        in_specs=[pl.BlockSpec((1, NH, tm, DH), lambda b, i: (b, 0, i, 0)),
                  pl.BlockSpec((tm, D), lambda b, i: (b * nt + i, 3)),
                  pl.BlockSpec((1, DH), lambda b, i: (0, 0))],
        out_specs=pl.BlockSpec((tm, D), lambda b, i: (b * nt + i, 0)),
        out_shape=_sds((BL * T, D), BF16),
        compiler_params=_cp(("arbitrary", "arbitrary")),
    )(o, proj, o_g)


def _dn_gnorm_bwd(dog, o, proj, o_g):
    tm = min(256, T)
    nt = T // tm

    def body(dog_ref, o_ref, z_ref, g_ref, do_ref, dz_ref, dg_ref):
        z = z_ref[...]
        dog = dog_ref[...]
        dg = jnp.zeros((1, DH), F32)
        for h in range(NH):
            cols = slice(h * DH, (h + 1) * DH)
            _, vjp = jax.vjp(_gated_norm, o_ref[0, h], z[:, cols], g_ref[...])
            do_h, dz_h, dg_h = vjp(dog[:, cols])
            do_ref[0, h] = do_h
            dz_ref[:, cols] = dz_h.astype(BF16)
            dg += dg_h
        _acc_rows(dg_ref, jnp.logical_and(pl.program_id(0) == 0, pl.program_id(1) == 0), [dg])

    return _pcall(
        body, name="dn_gnorm_bwd", grid=(BL, nt),
        in_specs=[pl.BlockSpec((tm, D), lambda b, i: (b * nt + i, 0)),
                  pl.BlockSpec((1, NH, tm, DH), lambda b, i: (b, 0, i, 0)),
                  pl.BlockSpec((tm, D), lambda b, i: (b * nt + i, 3)),
                  pl.BlockSpec((1, DH), lambda b, i: (0, 0))],
        out_specs=[pl.BlockSpec((1, NH, tm, DH), lambda b, i: (b, 0, i, 0)),
                   pl.BlockSpec((tm, D), lambda b, i: (b * nt + i, 0)),
                   pl.BlockSpec((1, DH), lambda b, i: (0, 0))],
        out_shape=[_sds((BL, NH, T, DH), F32), _sds((BL * T, D), BF16), _sds((1, DH), F32)],
        compiler_params=_cp(("arbitrary", "arbitrary")),
    )(dog, o, proj, o_g)


def _dn_prep_bwd(dq, dk, dv, dgb, dbb, conv, proj, alog, dtb):
    n = conv.shape[0]
    tt, nt = _conv_tiles()
    w3 = 3 * D

    def body(dq_ref, dk_ref, dv_ref, dgb_ref, dbb_ref, conv_ref, ab_ref, alog_ref, dtb_ref, dconv_ref, dab_ref, dhead_ref):
        cv = conv_ref[...]
        sg = jax.nn.sigmoid(cv)
        dact = sg * (1.0 + cv * (1.0 - sg))
        lane = lax.broadcasted_iota(jnp.int32, (tt, 128), 1)
        cg = jnp.zeros((tt, 128), F32)
        cb = jnp.zeros((tt, 128), F32)
        for h in range(NH):
            cols = slice(h * DH, (h + 1) * DH)
            dconv_ref[:, h * DH:(h + 1) * DH] = dq_ref[0, h] * dact[:, cols]
            dconv_ref[:, D + h * DH:D + (h + 1) * DH] = dk_ref[0, h] * dact[:, D + h * DH:D + (h + 1) * DH]
            dconv_ref[:, 2 * D + h * DH:2 * D + (h + 1) * DH] = dv_ref[0, h] * dact[:, 2 * D + h * DH:2 * D + (h + 1) * DH]
            cg = jnp.where(lane == h, jnp.sum(dgb_ref[0, h], axis=-1, keepdims=True), cg)
            cb = jnp.where(lane == NH + h, jnp.sum(dbb_ref[0, h], axis=-1, keepdims=True), cb)
        _, vjp = jax.vjp(_g_beta, ab_ref[...], alog_ref[...], dtb_ref[...])
        dab, dalog, ddtb = vjp((cg, cb))
        dab_ref[...] = dab.astype(BF16)
        _acc_rows(dhead_ref, jnp.logical_and(pl.program_id(0) == 0, pl.program_id(1) == 0), [dalog, ddtb])

    row = lambda b, i: (b * nt + i, 0)
    head = pl.BlockSpec((1, NH, tt, DH), lambda b, i: (b, 0, i, 0))
    vec = pl.BlockSpec((1, 128), lambda b, i: (0, 0))
    return _pcall(
        body, name="dn_prep_bwd", grid=(BL, nt),
        in_specs=[head] * 5 + [pl.BlockSpec((tt, w3), row),
                               pl.BlockSpec((tt, 128), lambda b, i: (b * nt + i, 4 * D // 128)), vec, vec],
        out_specs=[pl.BlockSpec((tt, w3), row), pl.BlockSpec((tt, 128), row), pl.BlockSpec((2, 128), lambda b, i: (0, 0))],
        out_shape=[_sds((n, w3), F32), _sds((n, 128), BF16), _sds((2, 128), F32)],
        compiler_params=_cp(("arbitrary", "arbitrary")),
    )(dq, dk, dv, dgb, dbb, conv, proj, alog, dtb)


def _dn_sconv_bwd(dconv, proj, w_sc):
    n = dconv.shape[0]
    tt, nt = _conv_tiles()
    w3 = 3 * D

    def body(dc_ref, nxt_ref, qkv_ref, halo_ref, w_ref, dpre_ref, dw_ref, dwin, pwin):
        b, i = pl.program_id(0), pl.program_id(1)
        dc = dc_ref[...]
        dwin[0:tt, :] = dc
        dwin[tt:tt + SHALO, :] = jnp.where(i == nt - 1, 0.0, nxt_ref[...])
        pwin[0:SHALO, :] = jnp.where(i == 0, 0.0, halo_ref[...])
        pwin[SHALO:SHALO + tt, :] = qkv_ref[...]
        dpre = jnp.zeros((tt, w3), F32)
        dws = []
        for k in range(SCW):
            dpre += w_ref[k:k + 1, :] * dwin[pl.ds(SCW - 1 - k, tt), :]
            dws.append(jnp.sum(dc * pwin[pl.ds(SHALO - (SCW - 1) + k, tt), :], axis=0, keepdims=True))
        dws += [jnp.zeros((1, w3), F32)] * (SHALO - SCW)
        dpre_ref[...] = dpre.astype(BF16)
        _acc_rows(dw_ref, jnp.logical_and(b == 0, i == 0), dws)

    row = lambda b, i: (b * nt + i, 0)
    return _pcall(
        body, name="dn_sconv_bwd", grid=(BL, nt),
        in_specs=[pl.BlockSpec((tt, w3), row), _next_halo_spec(w3, tt, SHALO),
                  pl.BlockSpec((tt, w3), row), _prev_halo_spec(w3, tt, SHALO),
                  pl.BlockSpec((SHALO, w3), lambda b, i: (0, 0))],
        out_specs=[pl.BlockSpec((tt, w3), row), pl.BlockSpec((SHALO, w3), lambda b, i: (0, 0))],
        out_shape=[_sds((n, w3), BF16), _sds((SHALO, w3), F32)],
        scratch_shapes=[pltpu.VMEM((tt + SHALO, w3), F32), pltpu.VMEM((SHALO + tt, w3), F32)],
        compiler_params=_cp(("arbitrary", "arbitrary")),
    )(dconv, dconv, proj, proj, w_sc)


def _ada_fwd(c_all, w_ada, b_cols):
    nl, _, m = w_ada.shape
    nb = c_all.shape[0]

    def body(c_ref, w_ref, b_ref, o_ref):
        cv = c_ref[...]
        cs = (cv * jax.nn.sigmoid(cv)).astype(BF16)
        o_ref[...] = _dot(cs, w_ref[...].astype(BF16)) + b_ref[...]

    return _pcall(
        body, name="ada_fwd", grid=(nl,),
        in_specs=[pl.BlockSpec((nb, D), lambda l: (0, 0)), pl.BlockSpec((None, D, m), lambda l: (l, 0, 0)),
                  pl.BlockSpec((None, 1, m), lambda l: (l, 0, 0))],
        out_specs=pl.BlockSpec((None, nb, m), lambda l: (l, 0, 0)),
        out_shape=_sds((nl, nb, m), F32),
        compiler_params=_cp(("arbitrary",)),
    )(c_all, w_ada, b_cols)


def _ada_bwd(c_all, dmod_cols):
    nl, nb, m = dmod_cols.shape

    def body(c_ref, d_ref, o_ref):
        cv = c_ref[...]
        cs = (cv * jax.nn.sigmoid(cv)).astype(BF16)
        o_ref[0] = _dot_tn(cs, d_ref[...].astype(BF16))

    return _pcall(
        body, name="ada_bwd", grid=(nl,),
        in_specs=[pl.BlockSpec((nb, D), lambda l: (0, 0)), pl.BlockSpec((None, nb, m), lambda l: (l, 0, 0))],
        out_specs=pl.BlockSpec((1, D, m), lambda l: (0, l, 0)),
        out_shape=_sds((1, nl * D, m), F32),
        compiler_params=_cp(("arbitrary",)),
    )(c_all, dmod_cols)


def _loss_head(x, tgt, fg):
    n = x.shape[0]
    tm = min(512, T)

    def f(xv, g, t):
        r = lax.rsqrt(jnp.mean(xv * xv, axis=-1, keepdims=True) + EPS)
        e = xv * r * g - t
        return 0.5 * jnp.sum(e * e, axis=0, keepdims=True) * (1.0 / D)

    def body(x_ref, t_ref, g_ref, dx_ref, st_ref):
        t = t_ref[...]
        lrow, vjp = jax.vjp(lambda xv, g: f(xv, g, t), x_ref[...], g_ref[...])
        dx, dg = vjp(jnp.ones_like(lrow))
        dx_ref[...] = dx
        _acc_rows(st_ref, pl.program_id(0) == 0, [dg, lrow])

    return _pcall(
        body, name="loss_head", grid=(n // tm,),
        in_specs=[pl.BlockSpec((tm, D), lambda i: (i, 0)), pl.BlockSpec((tm, D), lambda i: (i, 0)),
                  pl.BlockSpec((1, D), lambda i: (0, 0))],
        out_specs=[pl.BlockSpec((tm, D), lambda i: (i, 0)), pl.BlockSpec((2, D), lambda i: (0, 0))],
        out_shape=[_sds((n, D), F32), _sds((2, D), F32)],
        compiler_params=_cp(("arbitrary",)),
    )(x, tgt, fg)


def _adamw(parts, w, m, v):
    p, r, c = parts.shape
    tr = r
    for cand in (256, 128, 64, 32, 16, 8):
        if r % cand == 0:
            tr = cand
            break
    k1 = 1.0 - B1 ** STEP
    k2 = 1.0 - B2 ** STEP

    def body(p_ref, w_ref, m_ref, v_ref, g_ref, d_ref, nm_ref, nv_ref):
        g = p_ref[0].astype(F32)
        for q in range(1, p):
            g += p_ref[q].astype(F32)
        mn = B1 * m_ref[...] + (1.0 - B1) * g
        vn = B2 * v_ref[...] + (1.0 - B2) * (g * g)
        g_ref[...] = g
        nm_ref[...] = mn
        nv_ref[...] = vn
        d_ref[...] = -LR * ((mn / k1) / (jnp.sqrt(vn / k2) + AEPS) + WD * w_ref[...])

    blk = pl.BlockSpec((tr, c), lambda i: (i, 0))
    return _pcall(
        body, name="adamw", grid=(r // tr,),
        in_specs=[pl.BlockSpec((p, tr, c), lambda i: (0, i, 0)), blk, blk, blk],
        out_specs=[blk] * 4, out_shape=[_sds((r, c), F32)] * 4,
        compiler_params=_cp(("arbitrary",)),
    )(parts, w, m, v)


def _pack(arrs):
    flat = jnp.concatenate([a.reshape(-1) for a in arrs])
    pad = (-flat.shape[0]) % 1024
    return jnp.pad(flat, (0, pad)).reshape(-1, 128)


def _unpack(buf, shapes):
    flat = buf.reshape(-1)
    out, off = [], 0
    for s in shapes:
        size = 1
        for d in s:
            size *= d
        out.append(flat[off:off + size].reshape(s))
        off += size
    return out


def _natural_cols(g):
    return jnp.transpose(g, (1, 0, 2)).reshape(g.shape[1], -1)


def _shard_cols(w):
    k = w.shape[0]
    return jnp.transpose(w.reshape(k, NDEV, -1), (1, 0, 2))


def kernel(x, c, norm_g, w_ada, b_ada, w_ffn_in, w_ffn_out, cm_w_glu, cm_b_glu, cm_w_dw, cm_b_dw, cm_ln_g, cm_ln_b, cm_w_pw, cm_b_pw, dn_w_in, dn_w_sconv, dn_a_log, dn_dt_bias, dn_o_g, dn_w_out, final_g, loss_target, m_norm_g, m_w_ada, m_b_ada, m_w_ffn_in, m_w_ffn_out, m_cm_w_glu, m_cm_b_glu, m_cm_w_dw, m_cm_b_dw, m_cm_ln_g, m_cm_ln_b, m_cm_w_pw, m_cm_b_pw, m_dn_w_in, m_dn_w_sconv, m_dn_a_log, m_dn_dt_bias, m_dn_o_g, m_dn_w_out, m_final_g, v_norm_g, v_w_ada, v_b_ada, v_w_ffn_in, v_w_ffn_out, v_cm_w_glu, v_cm_b_glu, v_cm_w_dw, v_cm_b_dw, v_cm_ln_g, v_cm_ln_b, v_cm_w_pw, v_cm_b_pw, v_dn_w_in, v_dn_w_sconv, v_dn_a_log, v_dn_dt_bias, v_dn_o_g, v_dn_w_out, v_final_g):
    me = 4 * lax.axis_index("x") + 2 * lax.axis_index("y") + lax.axis_index("c")
    n = BL * T
    nf = 4
    tf = FF // nf
    na, nb = cm_w_glu.shape[0], dn_w_in.shape[0]
    mcols = w_ada.shape[2]
    dsh = D // NDEV

    small_shapes = [c.shape, norm_g.shape, cm_w_dw.shape, dn_w_sconv.shape]
    small_g, = _all_gather([_pack([c, norm_g, cm_w_dw, dn_w_sconv])], "gather_small")
    sm = [_unpack(small_g[d], small_shapes) for d in range(NDEV)]
    c_all = jnp.concatenate([s[0] for s in sm], axis=0)
    norm_g_f = jnp.concatenate([s[1] for s in sm], axis=-1)
    w_dw_f = jnp.concatenate([s[2] for s in sm], axis=-1)
    w_sc_f = jnp.concatenate([s[3] for s in sm], axis=-1)

    big = [w_ffn_in, w_ffn_out, cm_w_glu, cm_w_pw, dn_w_in, dn_w_out]
    g_ffn_in, g_ffn_out, g_glu, g_pw, g_dnin, g_dnout = _all_gather([w.astype(BF16) for w in big], "gather_weights")

    b_cols = lax.dynamic_slice_in_dim(b_ada, me * mcols, mcols, axis=1)[:, None, :]
    mod_cols = _ada_fwd(c_all, w_ada, b_cols)
    mod_g, = _all_gather([mod_cols], "gather_mod")
    mod_all = jnp.transpose(mod_g, (1, 2, 0, 3)).reshape(DEPTH, NDEV * BL, 9 * D)
    mod = lax.dynamic_slice_in_dim(mod_all, me * BL, BL, axis=1).reshape(DEPTH, BL, 3, 3, D)

    def ffn_weights(l, s):
        w_in = g_ffn_in[:, l, s].reshape(2, nf, D, tf)
        w_out = g_ffn_out[:, l, s].reshape(nf, tf, D)
        return w_in, w_out

    xs = x.reshape(n, D)
    saved = []
    for l in range(DEPTH):
        rec = {}
        for s, j in ((0, 0), (1, 2)):
            w_in, w_out = ffn_weights(l, s)
            ssg, ng = mod[l, :, j], norm_g_f[l, j][None]
            if j == 2:
                ssg1, ng1 = mod[l, :, 1], norm_g_f[l, 1][None]
                if l % 2 == 0:
                    a = l // 2
                    w_glu = _natural_cols(g_glu[:, a])
                    w_pw = g_pw[:, a].reshape(D, D)
                    w_dw = jnp.pad(w_dw_f[a], ((0, HALO - CW), (0, 0)))
                    h1, ab = _premod_matmul(xs, ssg1, ng1, w_glu, cm_b_glu[a][None], min(1024, 2 * D))
                    u1, u2 = _cm_mid_fwd(ab, w_dw, cm_b_dw[a][None], cm_ln_g[a][None], cm_ln_b[a][None])
                    xn, ymix = _matmul_res(xs, u2, ssg1, w_pw, cm_b_pw[a][None])
                    rec["mix"] = dict(x=xs, h=h1, ab=ab, u1=u1, u2=u2, y=ymix, w_glu=w_glu, w_pw=w_pw, w_dw=w_dw)
                else:
                    mi = l // 2
                    w_nat = _natural_cols(g_dnin[:, mi])
                    w_proj = jnp.pad(w_nat, ((0, 0), (0, 128 - 2 * NH)))
                    w_o = g_dnout[:, mi].reshape(D, D)
                    w_sc = jnp.pad(w_sc_f[mi], ((0, SHALO - SCW), (0, 0)))
                    alog = jnp.pad(dn_a_log[mi], (0, 128 - NH))[None]
                    dtb = jnp.pad(dn_dt_bias[mi], (0, 128 - NH))[None]
                    h1, proj = _premod_matmul(xs, ssg1, ng1, w_proj, jnp.zeros((1, w_proj.shape[1]), F32),
                                              (4 * D + 128) // 3 if (4 * D + 128) % 384 == 0 else 128)
                    conv, q, k, v, gb, bb = _dn_sconv_fwd(proj, w_sc, alog, dtb)
                    u, w, qg, kd, intra, egl = _dn_pre_fwd(q, k, v, gb, bb)
                    o, vn, s0 = _dn_scan_fwd(u, w, qg, kd, intra, egl)
                    og = _dn_gnorm_fwd(o, proj, dn_o_g[mi][None])
                    xn, ymix = _matmul_res(xs, og, ssg1, w_o, jnp.zeros((1, D), F32))
                    rec["mix"] = dict(x=xs, h=h1, proj=proj, conv=conv, q=q, k=k, v=v, gb=gb, bb=bb, w=w, qg=qg, kd=kd,
                                      intra=intra, egl=egl, o=o, vn=vn, s0=s0, og=og, y=ymix, w_proj=w_proj, w_o=w_o,
                                      w_sc=w_sc, alog=alog, dtb=dtb)
                xs = xn
            xn, gu, hid, y = _ffn_fwd(xs, ssg, ng, w_in, w_out)
            rec[s] = dict(x=xs, gu=gu, hid=hid, y=y)
            xs = xn
        saved.append(rec)

    dx, stats = _loss_head(xs, loss_target.reshape(n, D), final_g[None])
    loss = lax.psum(jnp.sum(stats[1]), AXES)
    d_final_g = stats[0]

    d_mod = [[None] * 3 for _ in range(DEPTH)]
    d_norm = [[None] * 3 for _ in range(DEPTH)]
    dw_ffn_in = [[None] * 2 for _ in range(DEPTH)]
    dw_ffn_out = [[None] * 2 for _ in range(DEPTH)]
    dcm = [dict() for _ in range(na)]
    ddn = [dict() for _ in range(nb)]
    for l in reversed(range(DEPTH)):
        rec = saved[l]
        for s, j in ((1, 2), (0, 0)):
            w_in, w_out = ffn_weights(l, s)
            ssg, ng = mod[l, :, j], norm_g_f[l, j][None]
            r = rec[s]
            dx, dgu, hb, dout, dssg, dng = _ffn_bwd_a(r["x"], dx, ssg, ng, r["y"], r["gu"], w_in, w_out)
            dw_ffn_in[l][s], dw_ffn_out[l][s] = _ffn_bwd_w(hb, dgu, r["hid"], dout)
            d_mod[l][j], d_norm[l][j] = dssg, dng[0]
            if j == 2:
                ssg1, ng1 = mod[l, :, 1], norm_g_f[l, 1][None]
                r = rec["mix"]
                if l % 2 == 0:
                    a = l // 2
                    du2, dy, dgate, db_pw = _matmul_res_bwd(dx, r["y"], ssg1, r["w_pw"])
                    du1, dln = _cm_mid_bwd_a(du2, r["u1"], cm_ln_g[a][None], cm_ln_b[a][None])
                    dab, dw_dw, db_dw, db_glu = _cm_mid_bwd_b(du1, r["ab"], r["w_dw"])
                    dx, dssg, dng = _premod_matmul_bwd(r["x"], dx, ssg1, ng1, [dab], [r["w_glu"]])
                    dcm[a] = dict(w_glu=_shard_cols(_wgrad(r["h"], dab)), w_pw=_wgrad(r["u2"], dy).reshape(NDEV, dsh, D),
                                  b_glu=db_glu[0], w_dw=dw_dw[:CW], b_dw=db_dw[0], ln_g=dln[0], ln_b=dln[1], b_pw=db_pw[0])
                else:
                    mi = l // 2
                    dog, dy, dgate, _ = _matmul_res_bwd(dx, r["y"], ssg1, r["w_o"])
                    do, dz, d_og = _dn_gnorm_bwd(dog, r["o"], r["proj"], dn_o_g[mi][None])
                    du, dw, dqg, dkd, dintra, degl = _dn_scan_bwd(do, r["w"], r["qg"], r["kd"], r["intra"], r["egl"],
                                                                   r["vn"], r["s0"])
                    dq, dk, dv, dgb, dbb = _dn_pre_bwd(r["q"], r["k"], r["v"], r["gb"], r["bb"], du, dw, dqg, dkd, dintra, degl)
                    dconv, dab16, dhead = _dn_prep_bwd(dq, dk, dv, dgb, dbb, r["conv"], r["proj"], r["alog"], r["dtb"])
                    dpre, dw_sc = _dn_sconv_bwd(dconv, r["proj"], r["w_sc"])
                    wp = r["w_proj"]
                    dx, dssg, dng = _premod_matmul_bwd(r["x"], dx, ssg1, ng1, [dpre, dz, dab16],
                                                       [wp[:, :3 * D], wp[:, 3 * D:4 * D], wp[:, 4 * D:]])
                    dw_in = jnp.concatenate([_wgrad(r["h"], dpre), _wgrad(r["h"], dz),
                                             _wgrad(r["h"], dab16)[:, :2 * NH]], axis=1)
                    ddn[mi] = dict(w_in=_shard_cols(dw_in), w_out=_wgrad(r["og"], dy).reshape(NDEV, dsh, D),
                                   w_sconv=dw_sc[:SCW], a_log=dhead[0, :NH], dt_bias=dhead[1, :NH], o_g=d_og[0])
                d_mod[l][1] = dssg.at[:, 2].set(dgate[:, 0])
                d_norm[l][1] = dng[0]
    grad_x = dx.reshape(BL, T, D)

    dmod_loc = jnp.stack([jnp.stack(d_mod[l], axis=1) for l in range(DEPTH)]).reshape(DEPTH, BL, 9 * D)
    dmod_g, = _all_gather([dmod_loc], "gather_dmod")
    dmod_all = jnp.transpose(dmod_g, (1, 0, 2, 3)).reshape(DEPTH, NDEV * BL, 9 * D)
    g_w_ada = _ada_bwd(c_all, lax.dynamic_slice_in_dim(dmod_all, me * mcols, mcols, axis=2))

    stack = lambda rows: jnp.stack([jnp.stack(r, axis=1) for r in rows], axis=1)
    send = [stack([[dw_ffn_in[l][s].reshape(NDEV, D, tf) for s in range(2)] for l in range(DEPTH)]),
            stack([[dw_ffn_out[l][s].reshape(NDEV, FF // NDEV, D) for s in range(2)] for l in range(DEPTH)]),
            jnp.stack([d["w_glu"] for d in dcm], axis=1), jnp.stack([d["w_pw"] for d in dcm], axis=1),
            jnp.stack([d["w_in"] for d in ddn], axis=1), jnp.stack([d["w_out"] for d in ddn], axis=1)]
    recv = _all_to_all(send, "exchange_grads")

    small = [jnp.sum(dmod_loc, axis=1), jnp.stack([jnp.stack(d_norm[l]) for l in range(DEPTH)]),
             jnp.stack([d["b_glu"] for d in dcm]), jnp.stack([d["w_dw"] for d in dcm]), jnp.stack([d["b_dw"] for d in dcm]),
             jnp.stack([d["ln_g"] for d in dcm]), jnp.stack([d["ln_b"] for d in dcm]), jnp.stack([d["b_pw"] for d in dcm]),
             jnp.stack([d["w_sconv"] for d in ddn]), jnp.stack([d["a_log"] for d in ddn]),
             jnp.stack([d["dt_bias"] for d in ddn]), jnp.stack([d["o_g"] for d in ddn]), d_final_g]
    small_parts, = _all_gather([_pack(small)], "gather_small_grads")
    names = ["b_ada", "norm_g", "cm_b_glu", "cm_w_dw", "cm_b_dw", "cm_ln_g", "cm_ln_b", "cm_b_pw",
             "dn_w_sconv", "dn_a_log", "dn_dt_bias", "dn_o_g", "final_g"]
    cols = lambda a, width: lax.dynamic_slice_in_dim(a, me * width, width, axis=a.ndim - 1)
    local = {"norm_g": lambda a: cols(a, dsh), "cm_w_dw": lambda a: cols(a, dsh), "dn_w_sconv": lambda a: cols(a, 3 * dsh)}
    full_shapes = [a.shape for a in small]
    parts = [_unpack(small_parts[d], full_shapes) for d in range(NDEV)]
    parts = [[local.get(nm, lambda a: a)(p) for nm, p in zip(names, ps)] for ps in parts]
    small_w = dict(b_ada=(b_ada, m_b_ada, v_b_ada), norm_g=(norm_g, m_norm_g, v_norm_g),
                   cm_b_glu=(cm_b_glu, m_cm_b_glu, v_cm_b_glu), cm_w_dw=(cm_w_dw, m_cm_w_dw, v_cm_w_dw),
                   cm_b_dw=(cm_b_dw, m_cm_b_dw, v_cm_b_dw), cm_ln_g=(cm_ln_g, m_cm_ln_g, v_cm_ln_g),
                   cm_ln_b=(cm_ln_b, m_cm_ln_b, v_cm_ln_b), cm_b_pw=(cm_b_pw, m_cm_b_pw, v_cm_b_pw),
                   dn_w_sconv=(dn_w_sconv, m_dn_w_sconv, v_dn_w_sconv), dn_a_log=(dn_a_log, m_dn_a_log, v_dn_a_log),
                   dn_dt_bias=(dn_dt_bias, m_dn_dt_bias, v_dn_dt_bias), dn_o_g=(dn_o_g, m_dn_o_g, v_dn_o_g),
                   final_g=(final_g, m_final_g, v_final_g))
    loc_shapes = [small_w[nm][0].shape for nm in names]
    packed_parts = jnp.stack([_pack(ps) for ps in parts])
    sres = _adamw(packed_parts, *[_pack([small_w[nm][q] for nm in names]) for q in range(3)])
    sres = [dict(zip(names, _unpack(r, loc_shapes))) for r in sres]

    res = {}
    bigs = dict(w_ffn_in=(recv[0], w_ffn_in, m_w_ffn_in, v_w_ffn_in), w_ffn_out=(recv[1], w_ffn_out, m_w_ffn_out, v_w_ffn_out),
                cm_w_glu=(recv[2], cm_w_glu, m_cm_w_glu, v_cm_w_glu), cm_w_pw=(recv[3], cm_w_pw, m_cm_w_pw, v_cm_w_pw),
                dn_w_in=(recv[4], dn_w_in, m_dn_w_in, v_dn_w_in), dn_w_out=(recv[5], dn_w_out, m_dn_w_out, v_dn_w_out),
                w_ada=(g_w_ada, w_ada, m_w_ada, v_w_ada))
    for nm, (p, w, m, v) in bigs.items():
        cdim = w.shape[-1]
        out = _adamw(p.reshape(p.shape[0], -1, cdim), w.reshape(-1, cdim), m.reshape(-1, cdim), v.reshape(-1, cdim))
        res[nm] = [o.reshape(w.shape) for o in out]
    for nm in names:
        res[nm] = [sres[q][nm] for q in range(4)]

    order = ["norm_g", "w_ada", "b_ada", "w_ffn_in", "w_ffn_out", "cm_w_glu", "cm_b_glu", "cm_w_dw", "cm_b_dw", "cm_ln_g",
             "cm_ln_b", "cm_w_pw", "cm_b_pw", "dn_w_in", "dn_w_sconv", "dn_a_log", "dn_dt_bias", "dn_o_g", "dn_w_out", "final_g"]
    return (loss, grad_x, *[res[nm][0] for nm in order], *[res[nm][1] for nm in order],
            *[res[nm][2] for nm in order], *[res[nm][3] for nm in order])
```

```python
import functools

import jax
import jax.numpy as jnp
from jax import lax
from jax.experimental import pallas as pl
from jax.experimental.pallas import tpu as pltpu

F32 = jnp.float32
BF16 = jnp.bfloat16
HI = lax.Precision.HIGHEST
MESH = pl.DeviceIdType.MESH
AXES = ("x", "y", "c")

NDEV = 8
D = 1024
T = 2048
BL = 2
FF = 2816
NH = 8
DH = 128
CW = 31
SCW = 4
CHUNK = 64
DEPTH = 4
EPS = 1e-6
LR, B1, B2, AEPS, WD, STEP = 0.001, 0.9, 0.999, 1e-08, 0.01, 10

VMEM_LIMIT_BYTES = 56 * 1024 * 1024
HALO = 32
SHALO = 8


def _pcall(body, **kw):
    return pl.pallas_call(body, **kw)


def _cp(sem=None):
    return pltpu.CompilerParams(dimension_semantics=sem, vmem_limit_bytes=VMEM_LIMIT_BYTES)


def _sds(shape, dtype):
    return jax.ShapeDtypeStruct(tuple(shape), dtype)


def _dot(a, b):
    return jnp.dot(a, b, preferred_element_type=F32)


def _dot_nt(a, b):
    return lax.dot_general(a, b, (((1,), (1,)), ((), ())), preferred_element_type=F32)


def _dot_tn(a, b):
    return lax.dot_general(a, b, (((0,), (0,)), ((), ())), preferred_element_type=F32)


def _modulate(x, ng, scale, shift):
    r = lax.rsqrt(jnp.mean(x * x, axis=-1, keepdims=True) + EPS)
    return (x * r * ng) * (1.0 + scale) + shift


def _acc_rows(ref, first, rows):
    @pl.when(first)
    def _():
        ref[...] = jnp.zeros_like(ref)

    for r, val in enumerate(rows):
        ref[r:r + 1, :] += val


def _my_pos():
    return lax.axis_index("x"), lax.axis_index("y"), lax.axis_index("c")


def _all_gather(arrs, name):
    n = len(arrs)

    def body(*refs):
        ins, outs = refs[:n], refs[n:2 * n]
        send, recv, loc = refs[2 * n:]
        x, y, c = _my_pos()
        me, sibling = (x, y, c), (x, y, 1 - c)
        chips = [(1 - x, y), (x, 1 - y), (1 - x, 1 - y)]

        def copy(a, k, block, to, src=None):
            dst = outs[a].at[4 * block[0] + 2 * block[1] + block[2]]
            return pltpu.make_async_remote_copy(
                src_ref=dst if src is None else src, dst_ref=dst,
                send_sem=send.at[7 * a + k], recv_sem=recv.at[7 * a + k],
                device_id=to, device_id_type=MESH)

        mine, first, passed = [], [], []
        for a in range(n):
            m = pltpu.make_async_copy(ins[a], outs[a].at[4 * x + 2 * y + c], loc.at[a])
            m.start()
            mine.append(m)
            f = [copy(a, 0, me, sibling, src=ins[a])]
            f += [copy(a, 1 + j, me, (*chip, c), src=ins[a]) for j, chip in enumerate(chips)]
            for cp in f:
                cp.start()
            first += f
        for a in range(n):
            for j, chip in enumerate(chips):
                copy(a, 1 + j, (*chip, c), me).wait_recv()
                p = copy(a, 4 + j, (*chip, c), sibling)
                p.start()
                passed.append(p)
        for a in range(n):
            copy(a, 0, sibling, me).wait_recv()
            for j, chip in enumerate(chips):
                copy(a, 4 + j, (*chip, 1 - c), me).wait_recv()
        for cp in first + passed:
            cp.wait_send()
        for m in mine:
            m.wait()

    hbm = pl.BlockSpec(memory_space=pl.ANY)
    return _pcall(
        body, name=name,
        out_shape=[_sds((NDEV,) + a.shape, a.dtype) for a in arrs],
        in_specs=[hbm] * n, out_specs=[hbm] * n,
        scratch_shapes=[pltpu.SemaphoreType.DMA((7 * n,)), pltpu.SemaphoreType.DMA((7 * n,)),
                        pltpu.SemaphoreType.DMA((n,))],
    )(*arrs)


def _peer(k):
    x, y, c = _my_pos()
    return (1 - x if k & 4 else x, 1 - y if k & 2 else y, 1 - c if k & 1 else c)


def _dev_index(p):
    return 4 * p[0] + 2 * p[1] + p[2]


_HBM = pl.BlockSpec(memory_space=pltpu.HBM)
_SEM = pl.BlockSpec(memory_space=pltpu.SEMAPHORE)
_EFFECT = pltpu.SideEffectType.DATAFLOW_SIDE_EFFECTING


def _exchange_start(srcs, gather, name):
    n = len(srcs)
    me = _dev_index(_my_pos())
    lands = []
    for s in srcs:
        own = s if gather else lax.dynamic_index_in_dim(s, me, 0, keepdims=False)
        shape = (NDEV,) + s.shape if gather else s.shape
        lands.append(lax.dynamic_update_index_in_dim(lax.empty(shape, s.dtype), own, me, 0))

    def body(*refs):
        src_refs, land_refs = refs[:n], refs[n:2 * n]
        sends, recvs = refs[2 * n:3 * n], refs[3 * n:4 * n]
        token = refs[-1]
        mine = _dev_index(_my_pos())
        for a in range(n):
            for k in range(1, 8):
                p = _peer(k)
                pltpu.make_async_remote_copy(
                    src_ref=src_refs[a] if gather else src_refs[a].at[_dev_index(p)],
                    dst_ref=land_refs[a].at[mine], send_sem=sends[a], recv_sem=recvs[a],
                    device_id=p, device_id_type=MESH).start()
        token[...] = jnp.zeros_like(token)

    out = pl.pallas_call(
        body, name=name,
        out_shape=(*[pltpu.SemaphoreType.DMA(())] * (2 * n),
                   *[pltpu.HBM(a.shape, a.dtype) for a in srcs], *[pltpu.HBM(a.shape, a.dtype) for a in lands],
                   _sds((8, 128), F32)),
        in_specs=[_HBM] * (2 * n),
        out_specs=(*[_SEM] * (2 * n), *[_HBM] * (2 * n), pl.BlockSpec(memory_space=pltpu.VMEM)),
        input_output_aliases={i: 2 * n + i for i in range(2 * n)},
        compiler_params=pltpu.CompilerParams(has_side_effects=_EFFECT),
    )(*[pltpu.with_memory_space_constraint(a, pltpu.HBM) for a in srcs],
      *[pltpu.with_memory_space_constraint(a, pltpu.HBM) for a in lands])
    state = (out[:n], out[n:2 * n], out[2 * n:3 * n], out[3 * n:4 * n])
    return state, out[-1][0, 0]


def _exchange_wait(state, after, name):
    sends, recvs, srcs, lands = state
    n = len(srcs)

    def body(*refs):
        land_refs = refs[n:2 * n]
        send_refs, recv_refs = refs[2 * n:3 * n], refs[3 * n:4 * n]
        for a in range(n):
            seven = land_refs[a].at[pl.ds(0, NDEV - 1)]
            cp = pltpu.make_async_remote_copy(src_ref=seven, dst_ref=seven, send_sem=send_refs[a], recv_sem=recv_refs[a],
                                              device_id=_peer(1), device_id_type=MESH)
            cp.wait_send()
            cp.wait_recv()

    out = pl.pallas_call(
        body, name=name,
        out_shape=(*[pltpu.HBM(a.shape, a.dtype) for a in srcs], *[pltpu.HBM(a.shape, a.dtype) for a in lands]),
        in_specs=(*[_HBM] * (2 * n), *[_SEM] * (2 * n), pl.BlockSpec(memory_space=pl.ANY)),
        out_specs=tuple([_HBM] * (2 * n)),
        input_output_aliases={i: i for i in range(2 * n)},
        compiler_params=pltpu.CompilerParams(has_side_effects=_EFFECT),
    )(*srcs, *lands, *sends, *recvs, after)
    return list(out[n:])


def _ffn_tiles():
    tm = min(512, T)
    return tm, T // tm


def _ffn_fwd(x, ssg, ng, w_in, w_out):
    n = x.shape[0]
    _, nf, tf, _ = w_in.shape
    tm, tpb = _ffn_tiles()

    def body(x_ref, ssg_ref, ng_ref, win_ref, wout_ref, xn_ref, gu_ref, hid_ref, y_ref, h_scr, acc):
        j = pl.program_id(1)

        @pl.when(j == 0)
        def _():
            s = ssg_ref[0]
            h_scr[...] = _modulate(x_ref[...], ng_ref[...], s[1:2], s[0:1]).astype(BF16)
            acc[...] = jnp.zeros_like(acc)

        h = h_scr[...]
        g = _dot_nt(h, win_ref[0])
        u = _dot_nt(h, win_ref[1])
        gu_ref[0] = g.astype(BF16)
        gu_ref[1] = u.astype(BF16)
        hid = (g * jax.nn.sigmoid(g) * u).astype(BF16)
        hid_ref[...] = hid
        acc[...] += _dot(hid, wout_ref[...])

        @pl.when(j == nf - 1)
        def _():
            yv = acc[...]
            y_ref[...] = yv.astype(BF16)
            xn_ref[...] = x_ref[...] + (0.5 * (1.0 + ssg_ref[0][2:3])) * yv

    return _pcall(
        body, name="ffn_fwd", grid=(n // tm, nf),
        in_specs=[pl.BlockSpec((tm, D), lambda i, j: (i, 0)),
                  pl.BlockSpec((1, 3, D), lambda i, j: (i // tpb, 0, 0)),
                  pl.BlockSpec((1, D), lambda i, j: (0, 0)),
                  pl.BlockSpec((2, None, tf, D), lambda i, j: (0, j, 0, 0)),
                  pl.BlockSpec((None, tf, D), lambda i, j: (j, 0, 0))],
        out_specs=[pl.BlockSpec((tm, D), lambda i, j: (i, 0)),
                   pl.BlockSpec((2, None, tm, tf), lambda i, j: (0, j, i, 0)),
                   pl.BlockSpec((None, tm, tf), lambda i, j: (j, i, 0)),
                   pl.BlockSpec((tm, D), lambda i, j: (i, 0))],
        out_shape=[_sds((n, D), F32), _sds((2, nf, n, tf), BF16), _sds((nf, n, tf), BF16), _sds((n, D), BF16)],
        scratch_shapes=[pltpu.VMEM((tm, D), BF16), pltpu.VMEM((tm, D), F32)],
        compiler_params=_cp(("arbitrary", "arbitrary")),
    )(x, ssg, ng, w_in, w_out)


def _ffn_bwd_a(x, dxn, ssg, ng, y, gu, w_in, w_out):
    n = x.shape[0]
    _, nf, tf, _ = w_in.shape
    tm, tpb = _ffn_tiles()

    def body(x_ref, dxn_ref, ssg_ref, ng_ref, y_ref, gu_ref, win_ref, wout_ref,
             dx_ref, dgu_ref, h_ref, dout_ref, dssg_ref, dng_ref, dout_scr, dh_acc):
        i, j = pl.program_id(0), pl.program_id(1)

        @pl.when(j == 0)
        def _():
            db = ((0.5 * (1.0 + ssg_ref[0][2:3])) * dxn_ref[...]).astype(BF16)
            dout_scr[...] = db
            dout_ref[...] = db
            dh_acc[...] = jnp.zeros_like(dh_acc)

        dhid = _dot_nt(dout_scr[...], wout_ref[...])
        g = gu_ref[0].astype(F32)
        u = gu_ref[1].astype(F32)
        sig = jax.nn.sigmoid(g)
        dg = (dhid * u * (sig * (1.0 + g * (1.0 - sig)))).astype(BF16)
        du = (dhid * (g * sig)).astype(BF16)
        dgu_ref[0] = dg
        dgu_ref[1] = du
        dh_acc[...] += _dot(dg, win_ref[0]) + _dot(du, win_ref[1])

        @pl.when(j == nf - 1)
        def _():
            s = ssg_ref[0]
            h, vjp = jax.vjp(_modulate, x_ref[...], ng_ref[...], s[1:2], s[0:1])
            dx_, dng_, dsc_, dsh_ = vjp(dh_acc[...])
            h_ref[...] = h.astype(BF16)
            dxn = dxn_ref[...]
            dx_ref[...] = dxn + dx_
            dgate = jnp.sum(0.5 * dxn * y_ref[...].astype(F32), axis=0, keepdims=True)
            _acc_rows(dssg_ref.at[0], i % tpb == 0, [dsh_, dsc_, dgate])
            _acc_rows(dng_ref, i == 0, [dng_])

    return _pcall(
        body, name="ffn_bwd_a", grid=(n // tm, nf),
        in_specs=[pl.BlockSpec((tm, D), lambda i, j: (i, 0)),
                  pl.BlockSpec((tm, D), lambda i, j: (i, 0)),
                  pl.BlockSpec((1, 3, D), lambda i, j: (i // tpb, 0, 0)),
                  pl.BlockSpec((1, D), lambda i, j: (0, 0)),
                  pl.BlockSpec((tm, D), lambda i, j: (i, 0)),
                  pl.BlockSpec((2, None, tm, tf), lambda i, j: (0, j, i, 0)),
                  pl.BlockSpec((2, None, tf, D), lambda i, j: (0, j, 0, 0)),
                  pl.BlockSpec((None, tf, D), lambda i, j: (j, 0, 0))],
        out_specs=[pl.BlockSpec((tm, D), lambda i, j: (i, 0)),
                   pl.BlockSpec((2, None, tm, tf), lambda i, j: (0, j, i, 0)),
                   pl.BlockSpec((tm, D), lambda i, j: (i, 0)),
                   pl.BlockSpec((tm, D), lambda i, j: (i, 0)),
                   pl.BlockSpec((1, 3, D), lambda i, j: (i // tpb, 0, 0)),
                   pl.BlockSpec((1, D), lambda i, j: (0, 0))],
        out_shape=[_sds((n, D), F32), _sds((2, nf, n, tf), BF16), _sds((n, D), BF16), _sds((n, D), BF16),
                   _sds((BL, 3, D), F32), _sds((1, D), F32)],
        scratch_shapes=[pltpu.VMEM((tm, D), BF16), pltpu.VMEM((tm, D), F32)],
        compiler_params=_cp(("arbitrary", "arbitrary")),
    )(x, dxn, ssg, ng, y, gu, w_in, w_out)


def _ffn_bwd_w(h, dgu, hid, dout):
    n = h.shape[0]
    _, nf, _, tf = dgu.shape
    tm, _ = _ffn_tiles()
    ni = n // tm

    def body(h_ref, dgu_ref, hid_ref, dout_ref, dwin_ref, dwout_ref, acc_g, acc_u, acc_o):
        i = pl.program_id(1)

        @pl.when(i == 0)
        def _():
            acc_g[...] = jnp.zeros_like(acc_g)
            acc_u[...] = jnp.zeros_like(acc_u)
            acc_o[...] = jnp.zeros_like(acc_o)

        hv = h_ref[...]
        acc_g[...] += _dot_tn(dgu_ref[0], hv)
        acc_u[...] += _dot_tn(dgu_ref[1], hv)
        acc_o[...] += _dot_tn(hid_ref[...], dout_ref[...])

        @pl.when(i == ni - 1)
        def _():
            dwin_ref[0] = acc_g[...].astype(BF16)
            dwin_ref[1] = acc_u[...].astype(BF16)
            dwout_ref[...] = acc_o[...].astype(BF16)

    return _pcall(
        body, name="ffn_bwd_w", grid=(nf, ni),
        in_specs=[pl.BlockSpec((tm, D), lambda j, i: (i, 0)),
                  pl.BlockSpec((2, None, tm, tf), lambda j, i: (0, j, i, 0)),
                  pl.BlockSpec((None, tm, tf), lambda j, i: (j, i, 0)),
                  pl.BlockSpec((tm, D), lambda j, i: (i, 0))],
        out_specs=[pl.BlockSpec((2, None, tf, D), lambda j, i: (0, j, 0, 0)),
                   pl.BlockSpec((None, tf, D), lambda j, i: (j, 0, 0))],
        out_shape=[_sds((2, nf, tf, D), BF16), _sds((nf, tf, D), BF16)],
        scratch_shapes=[pltpu.VMEM((tf, D), F32), pltpu.VMEM((tf, D), F32), pltpu.VMEM((tf, D), F32)],
        compiler_params=_cp(("arbitrary", "arbitrary")),
    )(h, dgu, hid, dout)


def _premod_matmul(x, ssg, ng, w, bias, tn):
    n = x.shape[0]
    shards = w.ndim == 3
    m = w.shape[0] * w.shape[2] if shards else w.shape[0]
    tm = min(256, T)
    tpb = T // tm
    w_spec = (pl.BlockSpec((None, D, tn), lambda i, j: (j, 0, 0)) if shards
              else pl.BlockSpec((tn, D), lambda i, j: (j, 0)))

    def body(x_ref, ssg_ref, ng_ref, w_ref, b_ref, h_ref, o_ref, h_scr):
        @pl.when(pl.program_id(1) == 0)
        def _():
            s = ssg_ref[0]
            hb = _modulate(x_ref[...], ng_ref[...], s[1:2], s[0:1]).astype(BF16)
            h_scr[...] = hb
            h_ref[...] = hb

        hv = h_scr[...]
        o_ref[...] = (_dot(hv, w_ref[...]) if shards else _dot_nt(hv, w_ref[...])) + b_ref[...]

    return _pcall(
        body, name="premod_matmul", grid=(n // tm, m // tn),
        in_specs=[pl.BlockSpec((tm, D), lambda i, j: (i, 0)),
                  pl.BlockSpec((1, 3, D), lambda i, j: (i // tpb, 0, 0)),
                  pl.BlockSpec((1, D), lambda i, j: (0, 0)),
                  w_spec,
                  pl.BlockSpec((1, tn), lambda i, j: (0, j))],
        out_specs=[pl.BlockSpec((tm, D), lambda i, j: (i, 0)),
                   pl.BlockSpec((tm, tn), lambda i, j: (i, j))],
        out_shape=[_sds((n, D), BF16), _sds((n, m), F32)],
        scratch_shapes=[pltpu.VMEM((tm, D), BF16)],
        compiler_params=_cp(("arbitrary", "arbitrary")),
    )(x, ssg, ng, w, bias)


def _premod_matmul_bwd(x, dxn, ssg, ng, douts, w):
    n = x.shape[0]
    k = len(douts)
    shards = w.ndim == 3
    tm = min(256, T)
    tpb = T // tm

    def body(*refs):
        x_ref, dxn_ref, ssg_ref, ng_ref = refs[:4]
        do_refs, w_ref = refs[4:4 + k], refs[4 + k]
        dx_ref, dssg_ref, dng_ref = refs[5 + k:]
        i = pl.program_id(0)
        dh = jnp.zeros((tm, D), F32)
        if shards:
            cs = w.shape[2]
            dov = do_refs[0][...]
            for j in range(w.shape[0]):
                dh += _dot_nt(dov[:, j * cs:(j + 1) * cs], w_ref[j])
        else:
            off = 0
            for q in range(k):
                mk = douts[q].shape[1]
                dh += _dot(do_refs[q][...], w_ref[off:off + mk, :])
                off += mk
        s = ssg_ref[0]
        _, vjp = jax.vjp(_modulate, x_ref[...], ng_ref[...], s[1:2], s[0:1])
        dx_, dng_, dsc_, dsh_ = vjp(dh)
        dx_ref[...] = dxn_ref[...] + dx_
        _acc_rows(dssg_ref.at[0], i % tpb == 0, [dsh_, dsc_, jnp.zeros_like(dsh_)])
        _acc_rows(dng_ref, i == 0, [dng_])

    return _pcall(
        body, name="premod_matmul_bwd", grid=(n // tm,),
        in_specs=[pl.BlockSpec((tm, D), lambda i: (i, 0)),
                  pl.BlockSpec((tm, D), lambda i: (i, 0)),
                  pl.BlockSpec((1, 3, D), lambda i: (i // tpb, 0, 0)),
                  pl.BlockSpec((1, D), lambda i: (0, 0))]
                 + [pl.BlockSpec((tm, a.shape[1]), lambda i: (i, 0)) for a in douts]
                 + [pl.BlockSpec(w.shape, (lambda i: (0, 0, 0)) if shards else (lambda i: (0, 0)))],
        out_specs=[pl.BlockSpec((tm, D), lambda i: (i, 0)),
                   pl.BlockSpec((1, 3, D), lambda i: (i // tpb, 0, 0)),
                   pl.BlockSpec((1, D), lambda i: (0, 0))],
        out_shape=[_sds((n, D), F32), _sds((BL, 3, D), F32), _sds((1, D), F32)],
        compiler_params=_cp(("arbitrary",)),
    )(x, dxn, ssg, ng, *douts, w)


def _matmul_res(x, a, ssg, w, bias):
    n, kd = a.shape
    tm = min(512, T)
    tpb = T // tm

    def body(x_ref, a_ref, ssg_ref, w_ref, b_ref, xn_ref, y_ref):
        yv = _dot(a_ref[...], w_ref[...]) + b_ref[...]
        y_ref[...] = yv.astype(BF16)
        xn_ref[...] = x_ref[...] + (1.0 + ssg_ref[0][2:3]) * yv

    return _pcall(
        body, name="matmul_res", grid=(n // tm,),
        in_specs=[pl.BlockSpec((tm, D), lambda i: (i, 0)),
                  pl.BlockSpec((tm, kd), lambda i: (i, 0)),
                  pl.BlockSpec((1, 3, D), lambda i: (i // tpb, 0, 0)),
                  pl.BlockSpec((kd, D), lambda i: (0, 0)),
                  pl.BlockSpec((1, D), lambda i: (0, 0))],
        out_specs=[pl.BlockSpec((tm, D), lambda i: (i, 0)), pl.BlockSpec((tm, D), lambda i: (i, 0))],
        out_shape=[_sds((n, D), F32), _sds((n, D), BF16)],
        compiler_params=_cp(("arbitrary",)),
    )(x, a, ssg, w, bias)


def _matmul_res_bwd(dxn, y, ssg, w):
    n = dxn.shape[0]
    kd = w.shape[0]
    tm = min(512, T)
    tpb = T // tm

    def body(dxn_ref, y_ref, ssg_ref, w_ref, da_ref, dy_ref, dgate_ref, dbias_ref):
        i = pl.program_id(0)
        dxn = dxn_ref[...]
        dy = (1.0 + ssg_ref[0][2:3]) * dxn
        dyb = dy.astype(BF16)
        dy_ref[...] = dyb
        da_ref[...] = _dot_nt(dyb, w_ref[...])
        _acc_rows(dgate_ref.at[0], i % tpb == 0, [jnp.sum(dxn * y_ref[...].astype(F32), axis=0, keepdims=True)])
        _acc_rows(dbias_ref, i == 0, [jnp.sum(dy, axis=0, keepdims=True)])

    return _pcall(
        body, name="matmul_res_bwd", grid=(n // tm,),
        in_specs=[pl.BlockSpec((tm, D), lambda i: (i, 0)),
                  pl.BlockSpec((tm, D), lambda i: (i, 0)),
                  pl.BlockSpec((1, 3, D), lambda i: (i // tpb, 0, 0)),
                  pl.BlockSpec((kd, D), lambda i: (0, 0))],
        out_specs=[pl.BlockSpec((tm, kd), lambda i: (i, 0)),
                   pl.BlockSpec((tm, D), lambda i: (i, 0)),
                   pl.BlockSpec((1, 1, D), lambda i: (i // tpb, 0, 0)),
                   pl.BlockSpec((1, D), lambda i: (0, 0))],
        out_shape=[_sds((n, kd), F32), _sds((n, D), BF16), _sds((BL, 1, D), F32), _sds((1, D), F32)],
        compiler_params=_cp(("arbitrary",)),
    )(dxn, y, ssg, w)


def _wgrad_shards(a, b, ns):
    n, kd = a.shape
    cs = b.shape[1] // ns
    tm = min(512, T)
    ni = n // tm

    def body(a_ref, b_ref, o_ref, acc):
        i = pl.program_id(1)

        @pl.when(i == 0)
        def _():
            acc[...] = jnp.zeros_like(acc)

        acc[...] += _dot_tn(a_ref[...], b_ref[...])

        @pl.when(i == ni - 1)
        def _():
            o_ref[...] = acc[...].astype(BF16)

    return _pcall(
        body, name="wgrad_shards", grid=(ns, ni),
        in_specs=[pl.BlockSpec((tm, kd), lambda q, i: (i, 0)), pl.BlockSpec((tm, cs), lambda q, i: (i, q))],
        out_specs=pl.BlockSpec((None, kd, cs), lambda q, i: (q, 0, 0)),
        out_shape=_sds((ns, kd, cs), BF16),
        scratch_shapes=[pltpu.VMEM((kd, cs), F32)],
        compiler_params=_cp(("arbitrary", "arbitrary")),
    )(a, b)


def _wgrad(a, b):
    n, kd = a.shape
    m = b.shape[1]
    tm = min(512, T)
    tk = min(512, kd)
    ni = n // tm

    def body(a_ref, b_ref, o_ref, acc):
        i = pl.program_id(1)

        @pl.when(i == 0)
        def _():
            acc[...] = jnp.zeros_like(acc)

        acc[...] += _dot_tn(a_ref[...], b_ref[...])

        @pl.when(i == ni - 1)
        def _():
            o_ref[...] = acc[...].astype(BF16)

    return _pcall(
        body, name="wgrad", grid=(kd // tk, ni),
        in_specs=[pl.BlockSpec((tm, tk), lambda q, i: (i, q)), pl.BlockSpec((tm, m), lambda q, i: (i, 0))],
        out_specs=pl.BlockSpec((tk, m), lambda q, i: (q, 0)),
        out_shape=_sds((kd, m), BF16),
        scratch_shapes=[pltpu.VMEM((tk, m), F32)],
        compiler_params=_cp(("arbitrary", "arbitrary")),
    )(a, b)


def _ln_silu(u1, g, b):
    mu = jnp.mean(u1, axis=-1, keepdims=True)
    xc = u1 - mu
    var = jnp.mean(xc * xc, axis=-1, keepdims=True)
    ln = xc * lax.rsqrt(var + EPS) * g + b
    return ln * jax.nn.sigmoid(ln)


def _conv_tiles():
    tt = min(256, T)
    return tt, T // tt


def _prev_halo_spec(cols, tt, halo):
    r = tt // halo
    return pl.BlockSpec((halo, cols), lambda b, i: (jnp.maximum(b * (T // halo) + i * r - 1, 0), 0))


def _next_halo_spec(cols, tt, halo):
    r = tt // halo
    last = BL * T // halo - 1
    return pl.BlockSpec((halo, cols), lambda b, i: (jnp.minimum(b * (T // halo) + (i + 1) * r, last), 0))


def _cm_mid_fwd(ab, w_dw, b_dw, ln_g, ln_b):
    n = ab.shape[0]
    tt, nt = _conv_tiles()

    def body(ab_ref, halo_ref, w_ref, bdw_ref, g_ref, b_ref, u1_ref, u2_ref, win):
        i = pl.program_id(1)
        hv = halo_ref[...]
        u0h = hv[:, :D] * jax.nn.sigmoid(hv[:, D:])
        win[0:HALO, :] = jnp.where(i == 0, 0.0, u0h)
        cv = ab_ref[...]
        win[HALO:HALO + tt, :] = cv[:, :D] * jax.nn.sigmoid(cv[:, D:])
        acc = jnp.zeros((tt, D), F32) + bdw_ref[...]
        for k in range(CW):
            acc += w_ref[k:k + 1, :] * win[pl.ds(HALO - (CW - 1) + k, tt), :]
        u1_ref[...] = acc
        u2_ref[...] = _ln_silu(acc, g_ref[...], b_ref[...]).astype(BF16)

    row = lambda b, i: (b * nt + i, 0)
    vec = pl.BlockSpec((1, D), lambda b, i: (0, 0))
    return _pcall(
        body, name="cm_mid_fwd", grid=(BL, nt),
        in_specs=[pl.BlockSpec((tt, 2 * D), row), _prev_halo_spec(2 * D, tt, HALO),
                  pl.BlockSpec((HALO, D), lambda b, i: (0, 0)), vec, vec, vec],
        out_specs=[pl.BlockSpec((tt, D), row), pl.BlockSpec((tt, D), row)],
        out_shape=[_sds((n, D), F32), _sds((n, D), BF16)],
        scratch_shapes=[pltpu.VMEM((HALO + tt, D), F32)],
        compiler_params=_cp(("arbitrary", "arbitrary")),
    )(ab, ab, w_dw, b_dw, ln_g, ln_b)


def _cm_mid_bwd_a(du2, u1, ln_g, ln_b):
    n = du2.shape[0]
    tm = min(256, T)

    def body(du2_ref, u1_ref, g_ref, b_ref, du1_ref, dln_ref):
        _, vjp = jax.vjp(_ln_silu, u1_ref[...], g_ref[...], b_ref[...])
        du1, dg, db = vjp(du2_ref[...])
        du1_ref[...] = du1
        _acc_rows(dln_ref, pl.program_id(0) == 0, [dg, db])

    vec = pl.BlockSpec((1, D), lambda i: (0, 0))
    return _pcall(
        body, name="cm_mid_bwd_a", grid=(n // tm,),
        in_specs=[pl.BlockSpec((tm, D), lambda i: (i, 0)), pl.BlockSpec((tm, D), lambda i: (i, 0)), vec, vec],
        out_specs=[pl.BlockSpec((tm, D), lambda i: (i, 0)), pl.BlockSpec((2, D), lambda i: (0, 0))],
        out_shape=[_sds((n, D), F32), _sds((2, D), F32)],
        compiler_params=_cp(("arbitrary",)),
    )(du2, u1, ln_g, ln_b)


def _cm_mid_bwd_b(du1, ab, w_dw):
    n = du1.shape[0]
    tt, nt = _conv_tiles()

    def body(du1_ref, nxt_ref, ab_ref, halo_ref, w_ref, dab_ref, dw_ref, dbdw_ref, dbglu_ref, dwin, uwin):
        b, i = pl.program_id(0), pl.program_id(1)
        first = jnp.logical_and(b == 0, i == 0)
        d1 = du1_ref[...]
        dwin[0:tt, :] = d1
        dwin[tt:tt + HALO, :] = jnp.where(i == nt - 1, 0.0, nxt_ref[...])
        hv = halo_ref[...]
        uwin[0:HALO, :] = jnp.where(i == 0, 0.0, hv[:, :D] * jax.nn.sigmoid(hv[:, D:]))
        cv = ab_ref[...]
        av, sg = cv[:, :D], jax.nn.sigmoid(cv[:, D:])
        uwin[HALO:HALO + tt, :] = av * sg
        du0 = jnp.zeros((tt, D), F32)
        dws = []
        for k in range(CW):
            du0 += w_ref[k:k + 1, :] * dwin[pl.ds(CW - 1 - k, tt), :]
            dws.append(jnp.sum(d1 * uwin[pl.ds(HALO - (CW - 1) + k, tt), :], axis=0, keepdims=True))
        dws += [jnp.zeros((1, D), F32)] * (HALO - CW)
        _acc_rows(dw_ref, first, dws)
        _acc_rows(dbdw_ref, first, [jnp.sum(d1, axis=0, keepdims=True)])
        da = du0 * sg
        db = du0 * av * sg * (1.0 - sg)
        dab_ref[:, :D] = da.astype(BF16)
        dab_ref[:, D:] = db.astype(BF16)
        _acc_rows(dbglu_ref.at[:, 0:D], first, [jnp.sum(da, axis=0, keepdims=True)])
        _acc_rows(dbglu_ref.at[:, D:2 * D], first, [jnp.sum(db, axis=0, keepdims=True)])

    row = lambda b, i: (b * nt + i, 0)
    return _pcall(
        body, name="cm_mid_bwd_b", grid=(BL, nt),
        in_specs=[pl.BlockSpec((tt, D), row), _next_halo_spec(D, tt, HALO),
                  pl.BlockSpec((tt, 2 * D), row), _prev_halo_spec(2 * D, tt, HALO),
                  pl.BlockSpec((HALO, D), lambda b, i: (0, 0))],
        out_specs=[pl.BlockSpec((tt, 2 * D), row), pl.BlockSpec((HALO, D), lambda b, i: (0, 0)),
                   pl.BlockSpec((1, D), lambda b, i: (0, 0)), pl.BlockSpec((1, 2 * D), lambda b, i: (0, 0))],
        out_shape=[_sds((n, 2 * D), BF16), _sds((HALO, D), F32), _sds((1, D), F32), _sds((1, 2 * D), F32)],
        scratch_shapes=[pltpu.VMEM((tt + HALO, D), F32), pltpu.VMEM((HALO + tt, D), F32)],
        compiler_params=_cp(("arbitrary", "arbitrary")),
    )(du1, du1, ab, ab, w_dw)


def _softplus(v):
    return jnp.maximum(v, 0.0) + jnp.log(1.0 + jnp.exp(-jnp.abs(v)))


def _g_beta(ab, alog, dtb):
    return -jnp.exp(alog) * _softplus(ab + dtb), jax.nn.sigmoid(ab)


def _dn_sconv_fwd(proj, w_sc, alog, dtb):
    n = proj.shape[0]
    tt, nt = _conv_tiles()
    w3 = 3 * D

    def body(qkv_ref, halo_ref, ab_ref, w_ref, alog_ref, dtb_ref, conv_ref, q_ref, k_ref, v_ref, gb_ref, bb_ref, win):
        i = pl.program_id(1)
        win[0:SHALO, :] = jnp.where(i == 0, 0.0, halo_ref[...])
        win[SHALO:SHALO + tt, :] = qkv_ref[...]
        acc = jnp.zeros((tt, w3), F32)
        for k in range(SCW):
            acc += w_ref[k:k + 1, :] * win[pl.ds(SHALO - (SCW - 1) + k, tt), :]
        conv_ref[...] = acc
        act = acc * jax.nn.sigmoid(acc)
        gfull, bfull = _g_beta(ab_ref[...], alog_ref[...], dtb_ref[...])
        for h in range(NH):
            q_ref[0, h] = act[:, h * DH:(h + 1) * DH]
            k_ref[0, h] = act[:, D + h * DH:D + (h + 1) * DH]
            v_ref[0, h] = act[:, 2 * D + h * DH:2 * D + (h + 1) * DH]
            gb_ref[0, h] = jnp.broadcast_to(gfull[:, h:h + 1], (tt, DH))
            bb_ref[0, h] = jnp.broadcast_to(bfull[:, NH + h:NH + h + 1], (tt, DH))

    row = lambda b, i: (b * nt + i, 0)
    head = pl.BlockSpec((1, NH, tt, DH), lambda b, i: (b, 0, i, 0))
    vec = pl.BlockSpec((1, 128), lambda b, i: (0, 0))
    hs = _sds((BL, NH, T, DH), F32)
    return _pcall(
        body, name="dn_sconv_fwd", grid=(BL, nt),
        in_specs=[pl.BlockSpec((tt, w3), row), _prev_halo_spec(w3, tt, SHALO),
                  pl.BlockSpec((tt, 128), lambda b, i: (b * nt + i, 4 * D // 128)),
                  pl.BlockSpec((SHALO, w3), lambda b, i: (0, 0)), vec, vec],
        out_specs=[pl.BlockSpec((tt, w3), row), head, head, head, head, head],
        out_shape=[_sds((n, w3), F32), hs, hs, hs, hs, hs],
        scratch_shapes=[pltpu.VMEM((SHALO + tt, w3), F32)],
        compiler_params=_cp(("arbitrary", "arbitrary")),
    )(proj, proj, proj, w_sc, alog, dtb)


def _bmm_raw(a, b):
    return jnp.einsum("gij,gjk->gik", a, b, preferred_element_type=F32, precision=HI)


def _bmm_nt_raw(a, b):
    return jnp.einsum("gid,gjd->gij", a, b, preferred_element_type=F32, precision=HI)


def _bmm_tn_raw(a, b):
    return jnp.einsum("gcd,gce->gde", a, b, preferred_element_type=F32, precision=HI)


@jax.custom_vjp
def _bmm(a, b):
    return _bmm_raw(a, b)


@jax.custom_vjp
def _bmm_nt(a, b):
    return _bmm_nt_raw(a, b)


@jax.custom_vjp
def _bmm_tn(a, b):
    return _bmm_tn_raw(a, b)


_bmm.defvjp(lambda a, b: (_bmm_raw(a, b), (a, b)),
            lambda r, dc: (_bmm_nt(dc, r[1]), _bmm_tn(r[0], dc)))
_bmm_nt.defvjp(lambda a, b: (_bmm_nt_raw(a, b), (a, b)),
               lambda r, dc: (_bmm(dc, r[1]), _bmm_tn(dc, r[0])))
_bmm_tn.defvjp(lambda a, b: (_bmm_tn_raw(a, b), (a, b)),
               lambda r, dc: (_bmm_nt(r[1], dc), _bmm(r[0], dc)))


@jax.custom_vjp
def _unit_lower_inverse(a):
    eye = (lax.broadcasted_iota(jnp.int32, a.shape, 1) == lax.broadcasted_iota(jnp.int32, a.shape, 2)).astype(F32)
    t = eye - a
    p = a
    for _ in range(CHUNK.bit_length() - 2):
        p = _bmm_raw(p, p)
        t = _bmm_raw(t, eye + p)
    return t


def _uli_fwd(a):
    t = _unit_lower_inverse(a)
    return t, t


def _uli_bwd(t, dt):
    return (-_bmm_nt(_bmm_tn(t, dt), t),)


_unit_lower_inverse.defvjp(_uli_fwd, _uli_bwd)


def _dn_pre(q, k, v, gb, bb):
    shape = (q.shape[0], CHUNK, CHUNK)
    ri = lax.broadcasted_iota(jnp.int32, shape, 1)
    ci = lax.broadcasted_iota(jnp.int32, shape, 2)
    causal, strict = ri >= ci, ri > ci
    qn = q * lax.rsqrt(jnp.sum(q * q, axis=-1, keepdims=True) + EPS) * (DH ** -0.5)
    kn = k * lax.rsqrt(jnp.sum(k * k, axis=-1, keepdims=True) + EPS)
    gcs = _bmm(causal.astype(F32), gb)
    gcol = gcs[:, :, :CHUNK]
    decay = jnp.exp(jnp.where(causal, gcol - jnp.swapaxes(gcol, 1, 2), -jnp.inf))
    eg = jnp.exp(gcs)
    kb = kn * bb
    a = jnp.where(strict, _bmm_nt(kb, kn) * decay, 0.0)
    tm = _unit_lower_inverse(a)
    u = _bmm(tm, v * bb)
    w = _bmm(tm, kb * eg)
    qg = qn * eg
    intra = _bmm_nt(qn, kn) * decay
    glast = gcs[:, CHUNK - 1:CHUNK, :]
    kd = kn * jnp.exp(glast - gcs)
    egl = jnp.broadcast_to(jnp.exp(glast), (q.shape[0], 8, DH))
    return u, w, qg, kd, intra, egl


def _pre_tiles():
    gcn = min(8, T // CHUNK)
    return gcn, T // (CHUNK * gcn)


def _dn_pre_specs():
    gcn, _ = _pre_tiles()
    tok = pl.BlockSpec((None, None, gcn * CHUNK, DH), lambda b, h, i: (b, h, i, 0))
    sq = pl.BlockSpec((None, None, gcn * CHUNK, CHUNK), lambda b, h, i: (b, h, i, 0))
    per = pl.BlockSpec((None, None, gcn * 8, DH), lambda b, h, i: (b, h, i, 0))
    return tok, sq, per


def _dn_pre_fwd(q, k, v, gb, bb):
    gcn, ng = _pre_tiles()
    tok, sq, per = _dn_pre_specs()

    def body(q_ref, k_ref, v_ref, gb_ref, bb_ref, u_ref, w_ref, qg_ref, kd_ref, in_ref, egl_ref):
        args = [r[...].reshape(gcn, CHUNK, DH) for r in (q_ref, k_ref, v_ref, gb_ref, bb_ref)]
        u, w, qg, kd, intra, egl = _dn_pre(*args)
        for r, val in ((u_ref, u), (w_ref, w), (qg_ref, qg), (kd_ref, kd)):
            r[...] = val.reshape(gcn * CHUNK, DH)
        in_ref[...] = intra.reshape(gcn * CHUNK, CHUNK)
        egl_ref[...] = egl.reshape(gcn * 8, DH)

    hs = _sds((BL, NH, T, DH), F32)
    return _pcall(
        body, name="dn_pre_fwd", grid=(BL, NH, ng),
        in_specs=[tok] * 5, out_specs=[tok, tok, tok, tok, sq, per],
        out_shape=[hs, hs, hs, hs, _sds((BL, NH, T, CHUNK), F32), _sds((BL, NH, T // CHUNK * 8, DH), F32)],
        compiler_params=_cp(("arbitrary",) * 3),
    )(q, k, v, gb, bb)


def _dn_pre_bwd(q, k, v, gb, bb, du, dw, dqg, dkd, dintra, degl):
    gcn, ng = _pre_tiles()
    tok, sq, per = _dn_pre_specs()

    def body(q_ref, k_ref, v_ref, gb_ref, bb_ref, du_ref, dw_ref, dqg_ref, dkd_ref, din_ref, degl_ref,
             dq_ref, dk_ref, dv_ref, dgb_ref, dbb_ref):
        args = [r[...].reshape(gcn, CHUNK, DH) for r in (q_ref, k_ref, v_ref, gb_ref, bb_ref)]
        _, vjp = jax.vjp(_dn_pre, *args)
        cts = [r[...].reshape(gcn, CHUNK, DH) for r in (du_ref, dw_ref, dqg_ref, dkd_ref)]
        de = degl_ref[...].reshape(gcn, 8, DH)
        one = jnp.logical_and(lax.broadcasted_iota(jnp.int32, de.shape, 1) == 0,
                              lax.broadcasted_iota(jnp.int32, de.shape, 2) == 0)
        outs = vjp((*cts, din_ref[...].reshape(gcn, CHUNK, CHUNK), jnp.where(one, de, 0.0)))
        for r, val in zip((dq_ref, dk_ref, dv_ref, dgb_ref, dbb_ref), outs):
            r[...] = val.reshape(gcn * CHUNK, DH)

    hs = _sds((BL, NH, T, DH), F32)
    return _pcall(
        body, name="dn_pre_bwd", grid=(BL, NH, ng),
        in_specs=[tok] * 9 + [sq, per], out_specs=[tok] * 5, out_shape=[hs] * 5,
        compiler_params=_cp(("arbitrary",) * 3),
    )(q, k, v, gb, bb, du, dw, dqg, dkd, dintra, degl)


def _scan_tiles():
    cs = min(2, T // CHUNK)
    return cs, T // (CHUNK * cs)


def _dn_scan_fwd(u, w, qg, kd, intra, egl):
    cs, ns = _scan_tiles()
    g = BL * NH
    nc = T // CHUNK

    def body(u_ref, w_ref, qg_ref, kd_ref, in_ref, egl_ref, o_ref, vn_ref, s0_ref, s_scr):
        @pl.when(pl.program_id(0) == 0)
        def _():
            s_scr[...] = jnp.zeros_like(s_scr)

        for c in range(cs):
            rows = pl.ds(c * CHUNK, CHUNK)
            s = s_scr[...]
            s0_ref[:, :, c] = s.reshape(BL, NH, DH, DH)

            def ld(r, m=DH):
                return r[:, :, rows, :].reshape(g, CHUNK, m)

            vn = ld(u_ref) - _bmm_raw(ld(w_ref), s)
            o = _bmm_raw(ld(qg_ref), s) + _bmm_raw(ld(in_ref, CHUNK), vn)
            e = egl_ref[:, :, pl.ds(c * 8, 1), :].reshape(g, 1, DH)
            s_scr[...] = s * e + _bmm_tn_raw(ld(kd_ref), vn)
            vn_ref[:, :, rows, :] = vn.reshape(BL, NH, CHUNK, DH)
            o_ref[:, :, rows, :] = o.reshape(BL, NH, CHUNK, DH)

    tok = pl.BlockSpec((BL, NH, cs * CHUNK, DH), lambda i: (0, 0, i, 0))
    hs = _sds((BL, NH, T, DH), F32)
    return _pcall(
        body, name="dn_scan_fwd", grid=(ns,),
        in_specs=[tok, tok, tok, tok, pl.BlockSpec((BL, NH, cs * CHUNK, CHUNK), lambda i: (0, 0, i, 0)),
                  pl.BlockSpec((BL, NH, cs * 8, DH), lambda i: (0, 0, i, 0))],
        out_specs=[tok, tok, pl.BlockSpec((BL, NH, cs, DH, DH), lambda i: (0, 0, i, 0, 0))],
        out_shape=[hs, hs, _sds((BL, NH, nc, DH, DH), F32)],
        scratch_shapes=[pltpu.VMEM((g, DH, DH), F32)],
        compiler_params=_cp(("arbitrary",)),
    )(u, w, qg, kd, intra, egl)


def _dn_scan_bwd(do, w, qg, kd, intra, egl, vn, s0):
    cs, ns = _scan_tiles()
    g = BL * NH
    nc = T // CHUNK

    def body(do_ref, w_ref, qg_ref, kd_ref, in_ref, egl_ref, vn_ref, s0_ref,
             du_ref, dw_ref, dqg_ref, dkd_ref, din_ref, degl_ref, ds_scr):
        @pl.when(pl.program_id(0) == 0)
        def _():
            ds_scr[...] = jnp.zeros_like(ds_scr)

        for c in reversed(range(cs)):
            rows = pl.ds(c * CHUNK, CHUNK)

            def ld(r, m=DH):
                return r[:, :, rows, :].reshape(g, CHUNK, m)

            def st(r, val, m=DH):
                r[:, :, rows, :] = val.reshape(BL, NH, CHUNK, m)

            s = s0_ref[:, :, c].reshape(g, DH, DH)
            ds = ds_scr[...]
            dov, vnv, kdv, wv, qgv, inv = ld(do_ref), ld(vn_ref), ld(kd_ref), ld(w_ref), ld(qg_ref), ld(in_ref, CHUNK)
            dv = _bmm_tn_raw(inv, dov) + _bmm_raw(kdv, ds)
            st(din_ref, _bmm_nt_raw(dov, vnv), CHUNK)
            st(dqg_ref, _bmm_nt_raw(dov, s))
            st(dkd_ref, _bmm_nt_raw(vnv, ds))
            st(du_ref, dv)
            st(dw_ref, -_bmm_nt_raw(dv, s))
            de = jnp.sum(jnp.sum(ds * s, axis=2, keepdims=True), axis=1, keepdims=True)
            degl_ref[:, :, pl.ds(c * 8, 8), :] = jnp.broadcast_to(de, (g, 8, DH)).reshape(BL, NH, 8, DH)
            e = egl_ref[:, :, pl.ds(c * 8, 1), :].reshape(g, 1, DH)
            ds_scr[...] = ds * e + _bmm_tn_raw(qgv, dov) - _bmm_tn_raw(wv, dv)

    rev = lambda i: (0, 0, ns - 1 - i, 0)
    tok = pl.BlockSpec((BL, NH, cs * CHUNK, DH), rev)
    sq = pl.BlockSpec((BL, NH, cs * CHUNK, CHUNK), rev)
    per = pl.BlockSpec((BL, NH, cs * 8, DH), rev)
    hs = _sds((BL, NH, T, DH), F32)
    return _pcall(
        body, name="dn_scan_bwd", grid=(ns,),
        in_specs=[tok, tok, tok, tok, sq, per, tok,
                  pl.BlockSpec((BL, NH, cs, DH, DH), lambda i: (0, 0, ns - 1 - i, 0, 0))],
        out_specs=[tok, tok, tok, tok, sq, per],
        out_shape=[hs, hs, hs, hs, _sds((BL, NH, T, CHUNK), F32), _sds((BL, NH, nc * 8, DH), F32)],
        scratch_shapes=[pltpu.VMEM((g, DH, DH), F32)],
        compiler_params=_cp(("arbitrary",)),
    )(do, w, qg, kd, intra, egl, vn, s0)


def _gated_norm(o_h, z_h, og):
    r = lax.rsqrt(jnp.mean(o_h * o_h, axis=-1, keepdims=True) + EPS)
    return (o_h * r * og) * (z_h * jax.nn.sigmoid(z_h))


def _dn_gnorm_fwd(o, proj, o_g):
    tm = min(256, T)
    nt = T // tm

    def body(o_ref, z_ref, g_ref, og_ref):
        z = z_ref[...]
        for h in range(NH):
            og_ref[:, h * DH:(h + 1) * DH] = _gated_norm(o_ref[0, h], z[:, h * DH:(h + 1) * DH], g_ref[...]).astype(BF16)

    return _pcall(
        body, name="dn_gnorm_fwd", grid=(BL, nt),
        in_specs=[pl.BlockSpec((1, NH, tm, DH), lambda b, i: (b, 0, i, 0)),
                  pl.BlockSpec((tm, D), lambda b, i: (b * nt + i, 3)),
                  pl.BlockSpec((1, DH), lambda b, i: (0, 0))],
        out_specs=pl.BlockSpec((tm, D), lambda b, i: (b * nt + i, 0)),
        out_shape=_sds((BL * T, D), BF16),
        compiler_params=_cp(("arbitrary", "arbitrary")),
    )(o, proj, o_g)


def _dn_gnorm_bwd(dog, o, proj, o_g):
    tm = min(256, T)
    nt = T // tm

    def body(dog_ref, o_ref, z_ref, g_ref, do_ref, dz_ref, dg_ref):
        z = z_ref[...]
        dog = dog_ref[...]
        dg = jnp.zeros((1, DH), F32)
        for h in range(NH):
            cols = slice(h * DH, (h + 1) * DH)
            _, vjp = jax.vjp(_gated_norm, o_ref[0, h], z[:, cols], g_ref[...])
            do_h, dz_h, dg_h = vjp(dog[:, cols])
            do_ref[0, h] = do_h
            dz_ref[:, cols] = dz_h.astype(BF16)
            dg += dg_h
        _acc_rows(dg_ref, jnp.logical_and(pl.program_id(0) == 0, pl.program_id(1) == 0), [dg])

    return _pcall(
        body, name="dn_gnorm_bwd", grid=(BL, nt),
        in_specs=[pl.BlockSpec((tm, D), lambda b, i: (b * nt + i, 0)),
                  pl.BlockSpec((1, NH, tm, DH), lambda b, i: (b, 0, i, 0)),
                  pl.BlockSpec((tm, D), lambda b, i: (b * nt + i, 3)),
                  pl.BlockSpec((1, DH), lambda b, i: (0, 0))],
        out_specs=[pl.BlockSpec((1, NH, tm, DH), lambda b, i: (b, 0, i, 0)),
                   pl.BlockSpec((tm, D), lambda b, i: (b * nt + i, 0)),
                   pl.BlockSpec((1, DH), lambda b, i: (0, 0))],
        out_shape=[_sds((BL, NH, T, DH), F32), _sds((BL * T, D), BF16), _sds((1, DH), F32)],
        compiler_params=_cp(("arbitrary", "arbitrary")),
    )(dog, o, proj, o_g)


def _dn_prep_bwd(dq, dk, dv, dgb, dbb, conv, proj, alog, dtb):
    n = conv.shape[0]
    tt, nt = _conv_tiles()
    w3 = 3 * D

    def body(dq_ref, dk_ref, dv_ref, dgb_ref, dbb_ref, conv_ref, ab_ref, alog_ref, dtb_ref, dconv_ref, dab_ref, dhead_ref):
        cv = conv_ref[...]
        sg = jax.nn.sigmoid(cv)
        dact = sg * (1.0 + cv * (1.0 - sg))
        lane = lax.broadcasted_iota(jnp.int32, (tt, 128), 1)
        cg = jnp.zeros((tt, 128), F32)
        cb = jnp.zeros((tt, 128), F32)
        for h in range(NH):
            cols = slice(h * DH, (h + 1) * DH)
            dconv_ref[:, h * DH:(h + 1) * DH] = dq_ref[0, h] * dact[:, cols]
            dconv_ref[:, D + h * DH:D + (h + 1) * DH] = dk_ref[0, h] * dact[:, D + h * DH:D + (h + 1) * DH]
            dconv_ref[:, 2 * D + h * DH:2 * D + (h + 1) * DH] = dv_ref[0, h] * dact[:, 2 * D + h * DH:2 * D + (h + 1) * DH]
            cg = jnp.where(lane == h, jnp.sum(dgb_ref[0, h], axis=-1, keepdims=True), cg)
            cb = jnp.where(lane == NH + h, jnp.sum(dbb_ref[0, h], axis=-1, keepdims=True), cb)
        _, vjp = jax.vjp(_g_beta, ab_ref[...], alog_ref[...], dtb_ref[...])
        dab, dalog, ddtb = vjp((cg, cb))
        dab_ref[...] = dab.astype(BF16)
        _acc_rows(dhead_ref, jnp.logical_and(pl.program_id(0) == 0, pl.program_id(1) == 0), [dalog, ddtb])

    row = lambda b, i: (b * nt + i, 0)
    head = pl.BlockSpec((1, NH, tt, DH), lambda b, i: (b, 0, i, 0))
    vec = pl.BlockSpec((1, 128), lambda b, i: (0, 0))
    return _pcall(
        body, name="dn_prep_bwd", grid=(BL, nt),
        in_specs=[head] * 5 + [pl.BlockSpec((tt, w3), row),
                               pl.BlockSpec((tt, 128), lambda b, i: (b * nt + i, 4 * D // 128)), vec, vec],
        out_specs=[pl.BlockSpec((tt, w3), row), pl.BlockSpec((tt, 128), row), pl.BlockSpec((2, 128), lambda b, i: (0, 0))],
        out_shape=[_sds((n, w3), F32), _sds((n, 128), BF16), _sds((2, 128), F32)],
        compiler_params=_cp(("arbitrary", "arbitrary")),
    )(dq, dk, dv, dgb, dbb, conv, proj, alog, dtb)


def _dn_sconv_bwd(dconv, proj, w_sc):
    n = dconv.shape[0]
    tt, nt = _conv_tiles()
    w3 = 3 * D

    def body(dc_ref, nxt_ref, qkv_ref, halo_ref, w_ref, dpre_ref, dw_ref, dwin, pwin):
        b, i = pl.program_id(0), pl.program_id(1)
        dc = dc_ref[...]
        dwin[0:tt, :] = dc
        dwin[tt:tt + SHALO, :] = jnp.where(i == nt - 1, 0.0, nxt_ref[...])
        pwin[0:SHALO, :] = jnp.where(i == 0, 0.0, halo_ref[...])
        pwin[SHALO:SHALO + tt, :] = qkv_ref[...]
        dpre = jnp.zeros((tt, w3), F32)
        dws = []
        for k in range(SCW):
            dpre += w_ref[k:k + 1, :] * dwin[pl.ds(SCW - 1 - k, tt), :]
            dws.append(jnp.sum(dc * pwin[pl.ds(SHALO - (SCW - 1) + k, tt), :], axis=0, keepdims=True))
        dws += [jnp.zeros((1, w3), F32)] * (SHALO - SCW)
        dpre_ref[...] = dpre.astype(BF16)
        _acc_rows(dw_ref, jnp.logical_and(b == 0, i == 0), dws)

    row = lambda b, i: (b * nt + i, 0)
    return _pcall(
        body, name="dn_sconv_bwd", grid=(BL, nt),
        in_specs=[pl.BlockSpec((tt, w3), row), _next_halo_spec(w3, tt, SHALO),
                  pl.BlockSpec((tt, w3), row), _prev_halo_spec(w3, tt, SHALO),
                  pl.BlockSpec((SHALO, w3), lambda b, i: (0, 0))],
        out_specs=[pl.BlockSpec((tt, w3), row), pl.BlockSpec((SHALO, w3), lambda b, i: (0, 0))],
        out_shape=[_sds((n, w3), BF16), _sds((SHALO, w3), F32)],
        scratch_shapes=[pltpu.VMEM((tt + SHALO, w3), F32), pltpu.VMEM((SHALO + tt, w3), F32)],
        compiler_params=_cp(("arbitrary", "arbitrary")),
    )(dconv, dconv, proj, proj, w_sc)


def _ada_fwd(c_all, w_ada, b_cols):
    nl, _, m = w_ada.shape
    nb = c_all.shape[0]

    def body(c_ref, w_ref, b_ref, o_ref):
        cv = c_ref[...]
        cs = (cv * jax.nn.sigmoid(cv)).astype(BF16)
        o_ref[...] = _dot(cs, w_ref[...].astype(BF16)) + b_ref[...]

    return _pcall(
        body, name="ada_fwd", grid=(nl,),
        in_specs=[pl.BlockSpec((nb, D), lambda l: (0, 0)), pl.BlockSpec((None, D, m), lambda l: (l, 0, 0)),
                  pl.BlockSpec((None, 1, m), lambda l: (l, 0, 0))],
        out_specs=pl.BlockSpec((None, nb, m), lambda l: (l, 0, 0)),
        out_shape=_sds((nl, nb, m), F32),
        compiler_params=_cp(("arbitrary",)),
    )(c_all, w_ada, b_cols)


def _ada_bwd(c_all, dmod_cols):
    nl, nb, m = dmod_cols.shape

    def body(c_ref, d_ref, o_ref):
        cv = c_ref[...]
        cs = (cv * jax.nn.sigmoid(cv)).astype(BF16)
        o_ref[0] = _dot_tn(cs, d_ref[...].astype(BF16))

    return _pcall(
        body, name="ada_bwd", grid=(nl,),
        in_specs=[pl.BlockSpec((nb, D), lambda l: (0, 0)), pl.BlockSpec((None, nb, m), lambda l: (l, 0, 0))],
        out_specs=pl.BlockSpec((1, D, m), lambda l: (0, l, 0)),
        out_shape=_sds((1, nl * D, m), F32),
        compiler_params=_cp(("arbitrary",)),
    )(c_all, dmod_cols)


def _loss_head(x, tgt, fg):
    n = x.shape[0]
    tm = min(512, T)

    def f(xv, g, t):
        r = lax.rsqrt(jnp.mean(xv * xv, axis=-1, keepdims=True) + EPS)
        e = xv * r * g - t
        return 0.5 * jnp.sum(e * e, axis=0, keepdims=True) * (1.0 / D)

    def body(x_ref, t_ref, g_ref, dx_ref, st_ref):
        t = t_ref[...]
        lrow, vjp = jax.vjp(lambda xv, g: f(xv, g, t), x_ref[...], g_ref[...])
        dx, dg = vjp(jnp.ones_like(lrow))
        dx_ref[...] = dx
        _acc_rows(st_ref, pl.program_id(0) == 0, [dg, lrow])

    return _pcall(
        body, name="loss_head", grid=(n // tm,),
        in_specs=[pl.BlockSpec((tm, D), lambda i: (i, 0)), pl.BlockSpec((tm, D), lambda i: (i, 0)),
                  pl.BlockSpec((1, D), lambda i: (0, 0))],
        out_specs=[pl.BlockSpec((tm, D), lambda i: (i, 0)), pl.BlockSpec((2, D), lambda i: (0, 0))],
        out_shape=[_sds((n, D), F32), _sds((2, D), F32)],
        compiler_params=_cp(("arbitrary",)),
    )(x, tgt, fg)


def _adamw(parts, w, m, v):
    p, r, c = parts.shape
    tr = r
    for cand in (256, 128, 64, 32, 16, 8):
        if r % cand == 0:
            tr = cand
            break
    k1 = 1.0 - B1 ** STEP
    k2 = 1.0 - B2 ** STEP

    def body(p_ref, w_ref, m_ref, v_ref, g_ref, d_ref, nm_ref, nv_ref):
        g = p_ref[0].astype(F32)
        for q in range(1, p):
            g += p_ref[q].astype(F32)
        mn = B1 * m_ref[...] + (1.0 - B1) * g
        vn = B2 * v_ref[...] + (1.0 - B2) * (g * g)
        g_ref[...] = g
        nm_ref[...] = mn
        nv_ref[...] = vn
        d_ref[...] = -LR * ((mn / k1) / (jnp.sqrt(vn / k2) + AEPS) + WD * w_ref[...])

    blk = pl.BlockSpec((tr, c), lambda i: (i, 0))
    return _pcall(
        body, name="adamw", grid=(r // tr,),
        in_specs=[pl.BlockSpec((p, tr, c), lambda i: (0, i, 0)), blk, blk, blk],
        out_specs=[blk] * 4, out_shape=[_sds((r, c), F32)] * 4,
        compiler_params=_cp(("arbitrary",)),
    )(parts, w, m, v)


def _adamw_slot(parts, w, m, v, outs, row0, col):
    p, r, c = parts.shape
    tr = r
    for cand in (256, 128, 64, 32, 16, 8):
        if r % cand == 0:
            tr = cand
            break
    if r % 352 == 0:
        tr = 352
    nt = r // tr
    k1 = 1.0 - B1 ** STEP
    k2 = 1.0 - B2 ** STEP

    def body(p_ref, w_ref, m_ref, v_ref, g0, d0, m0, v0, g_ref, d_ref, nm_ref, nv_ref):
        g = p_ref[0].astype(F32)
        for q in range(1, p):
            g += p_ref[q].astype(F32)
        mn = B1 * m_ref[...] + (1.0 - B1) * g
        vn = B2 * v_ref[...] + (1.0 - B2) * (g * g)
        g_ref[...] = g
        nm_ref[...] = mn
        nv_ref[...] = vn
        d_ref[...] = -LR * ((mn / k1) / (jnp.sqrt(vn / k2) + AEPS) + WD * w_ref[...])

    blk = pl.BlockSpec((tr, c), lambda i: (row0 * nt + i, col))
    anyspec = pl.BlockSpec(memory_space=pl.ANY)
    return _pcall(
        body, name="adamw_slot", grid=(nt,),
        in_specs=[pl.BlockSpec((p, tr, c), lambda i: (0, i, 0)), blk, blk, blk] + [anyspec] * 4,
        out_specs=[blk] * 4, out_shape=[_sds(w.shape, F32)] * 4,
        input_output_aliases={4: 0, 5: 1, 6: 2, 7: 3},
        compiler_params=_cp(("arbitrary",)),
    )(parts, w, m, v, *outs)


def _pack(arrs):
    flat = jnp.concatenate([a.reshape(-1) for a in arrs])
    pad = (-flat.shape[0]) % 1024
    return jnp.pad(flat, (0, pad)).reshape(-1, 128)


def _unpack(buf, shapes):
    flat = buf.reshape(-1)
    out, off = [], 0
    for s in shapes:
        size = 1
        for d in s:
            size *= d
        out.append(flat[off:off + size].reshape(s))
        off += size
    return out


def kernel(x, c, norm_g, w_ada, b_ada, w_ffn_in, w_ffn_out, cm_w_glu, cm_b_glu, cm_w_dw, cm_b_dw, cm_ln_g, cm_ln_b, cm_w_pw, cm_b_pw, dn_w_in, dn_w_sconv, dn_a_log, dn_dt_bias, dn_o_g, dn_w_out, final_g, loss_target, m_norm_g, m_w_ada, m_b_ada, m_w_ffn_in, m_w_ffn_out, m_cm_w_glu, m_cm_b_glu, m_cm_w_dw, m_cm_b_dw, m_cm_ln_g, m_cm_ln_b, m_cm_w_pw, m_cm_b_pw, m_dn_w_in, m_dn_w_sconv, m_dn_a_log, m_dn_dt_bias, m_dn_o_g, m_dn_w_out, m_final_g, v_norm_g, v_w_ada, v_b_ada, v_w_ffn_in, v_w_ffn_out, v_cm_w_glu, v_cm_b_glu, v_cm_w_dw, v_cm_b_dw, v_cm_ln_g, v_cm_ln_b, v_cm_w_pw, v_cm_b_pw, v_dn_w_in, v_dn_w_sconv, v_dn_a_log, v_dn_dt_bias, v_dn_o_g, v_dn_w_out, v_final_g):
    me = 4 * lax.axis_index("x") + 2 * lax.axis_index("y") + lax.axis_index("c")
    n = BL * T
    nf = 4
    tf = FF // nf
    na, nb = cm_w_glu.shape[0], dn_w_in.shape[0]
    mcols = w_ada.shape[2]
    dsh = D // NDEV

    tr_ffn = lambda a: jnp.swapaxes(a, 2, 3)
    tr_dn = lambda a: jnp.transpose(a, (2, 0, 1))
    wt_ffn_in, wt_dn_in = tr_ffn(w_ffn_in), tr_dn(dn_w_in)

    def layer_weights(l):
        mix = (cm_w_glu[l // 2], cm_w_pw[l // 2]) if l % 2 == 0 else (wt_dn_in[:, l // 2], dn_w_out[l // 2])
        return [w.astype(BF16) for w in (wt_ffn_in[l, 0], wt_ffn_in[l, 1], w_ffn_out[l, 0], w_ffn_out[l, 1], *mix)]

    gathers = [_exchange_start(layer_weights(l), True, f"gather_start_{l}")[0] for l in range(DEPTH)]

    small_shapes = [c.shape, norm_g.shape, cm_w_dw.shape, dn_w_sconv.shape]
    small_g, = _all_gather([_pack([c, norm_g, cm_w_dw, dn_w_sconv])], "gather_small")
    sm = [_unpack(small_g[d], small_shapes) for d in range(NDEV)]
    c_all = jnp.concatenate([s[0] for s in sm], axis=0)
    norm_g_f = jnp.concatenate([s[1] for s in sm], axis=-1)
    w_dw_f = jnp.concatenate([s[2] for s in sm], axis=-1)
    w_sc_f = jnp.concatenate([s[3] for s in sm], axis=-1)

    b_cols = lax.dynamic_slice_in_dim(b_ada, me * mcols, mcols, axis=1)[:, None, :]
    mod_cols = _ada_fwd(c_all, w_ada, b_cols)
    mod_g, = _all_gather([mod_cols], "gather_mod")
    mod_all = jnp.transpose(mod_g, (1, 2, 0, 3)).reshape(DEPTH, NDEV * BL, 9 * D)
    mod = lax.dynamic_slice_in_dim(mod_all, me * BL, BL, axis=1).reshape(DEPTH, BL, 3, 3, D)

    gathered = [None] * DEPTH

    def ffn_weights(l, s):
        return gathered[l][s].reshape(2, nf, tf, D), gathered[l][2 + s].reshape(nf, tf, D)

    xs = x.reshape(n, D)
    saved = []
    for l in range(DEPTH):
        rec = {}
        gathered[l] = _exchange_wait(gathers[l], xs, f"gather_wait_{l}")
        for s, j in ((0, 0), (1, 2)):
            w_in, w_out = ffn_weights(l, s)
            ssg, ng = mod[l, :, j], norm_g_f[l, j][None]
            if j == 2:
                ssg1, ng1 = mod[l, :, 1], norm_g_f[l, 1][None]
                if l % 2 == 0:
                    a = l // 2
                    w_glu = gathered[l][4]
                    w_pw = gathered[l][5].reshape(D, D)
                    w_dw = jnp.pad(w_dw_f[a], ((0, HALO - CW), (0, 0)))
                    h1, ab = _premod_matmul(xs, ssg1, ng1, w_glu, cm_b_glu[a][None], w_glu.shape[2])
                    u1, u2 = _cm_mid_fwd(ab, w_dw, cm_b_dw[a][None], cm_ln_g[a][None], cm_ln_b[a][None])
                    xn, ymix = _matmul_res(xs, u2, ssg1, w_pw, cm_b_pw[a][None])
                    rec["mix"] = dict(x=xs, h=h1, ab=ab, u1=u1, u2=u2, y=ymix, w_glu=w_glu, w_pw=w_pw, w_dw=w_dw)
                else:
                    mi = l // 2
                    w_proj = jnp.pad(gathered[l][4].reshape(4 * D + 2 * NH, D), ((0, 128 - 2 * NH), (0, 0)))
                    w_o = gathered[l][5].reshape(D, D)
                    w_sc = jnp.pad(w_sc_f[mi], ((0, SHALO - SCW), (0, 0)))
                    alog = jnp.pad(dn_a_log[mi], (0, 128 - NH))[None]
                    dtb = jnp.pad(dn_dt_bias[mi], (0, 128 - NH))[None]
                    h1, proj = _premod_matmul(xs, ssg1, ng1, w_proj, jnp.zeros((1, w_proj.shape[0]), F32),
                                              (4 * D + 128) // 3 if (4 * D + 128) % 384 == 0 else 128)
                    conv, q, k, v, gb, bb = _dn_sconv_fwd(proj, w_sc, alog, dtb)
                    u, w, qg, kd, intra, egl = _dn_pre_fwd(q, k, v, gb, bb)
                    o, vn, s0 = _dn_scan_fwd(u, w, qg, kd, intra, egl)
                    og = _dn_gnorm_fwd(o, proj, dn_o_g[mi][None])
                    xn, ymix = _matmul_res(xs, og, ssg1, w_o, jnp.zeros((1, D), F32))
                    rec["mix"] = dict(x=xs, h=h1, proj=proj, conv=conv, q=q, k=k, v=v, gb=gb, bb=bb, w=w, qg=qg, kd=kd,
                                      intra=intra, egl=egl, o=o, vn=vn, s0=s0, og=og, y=ymix, w_proj=w_proj, w_o=w_o,
                                      w_sc=w_sc, alog=alog, dtb=dtb)
                xs = xn
            xn, gu, hid, y = _ffn_fwd(xs, ssg, ng, w_in, w_out)
            rec[s] = dict(x=xs, gu=gu, hid=hid, y=y)
            xs = xn
        saved.append(rec)

    dx, stats = _loss_head(xs, loss_target.reshape(n, D), final_g[None])
    loss = lax.psum(jnp.sum(stats[1]), AXES)
    d_final_g = stats[0]

    d_mod = [[None] * 3 for _ in range(DEPTH)]
    d_norm = [[None] * 3 for _ in range(DEPTH)]
    dw_ffn_in = [[None] * 2 for _ in range(DEPTH)]
    dw_ffn_out = [[None] * 2 for _ in range(DEPTH)]
    dcm = [dict() for _ in range(na)]
    ddn = [dict() for _ in range(nb)]
    exchanges = [None] * DEPTH
    token = jnp.zeros((), F32)
    for l in reversed(range(DEPTH)):
        rec = saved[l]
        for s, j in ((1, 2), (0, 0)):
            w_in, w_out = ffn_weights(l, s)
            ssg, ng = mod[l, :, j] + token, norm_g_f[l, j][None]
            r = rec[s]
            dx, dgu, hb, dout, dssg, dng = _ffn_bwd_a(r["x"], dx, ssg, ng, r["y"], r["gu"], w_in, w_out)
            dw_ffn_in[l][s], dw_ffn_out[l][s] = _ffn_bwd_w(hb, dgu, r["hid"], dout)
            d_mod[l][j], d_norm[l][j] = dssg, dng[0]
            if j == 2:
                ssg1, ng1 = mod[l, :, 1], norm_g_f[l, 1][None]
                r = rec["mix"]
                if l % 2 == 0:
                    a = l // 2
                    du2, dy, dgate, db_pw = _matmul_res_bwd(dx, r["y"], ssg1, r["w_pw"])
                    du1, dln = _cm_mid_bwd_a(du2, r["u1"], cm_ln_g[a][None], cm_ln_b[a][None])
                    dab, dw_dw, db_dw, db_glu = _cm_mid_bwd_b(du1, r["ab"], r["w_dw"])
                    dx, dssg, dng = _premod_matmul_bwd(r["x"], dx, ssg1, ng1, [dab], r["w_glu"])
                    dcm[a] = dict(w_glu=_wgrad_shards(r["h"], dab, NDEV), w_pw=_wgrad(r["u2"], dy).reshape(NDEV, dsh, D),
                                  b_glu=db_glu[0], w_dw=dw_dw[:CW], b_dw=db_dw[0], ln_g=dln[0], ln_b=dln[1], b_pw=db_pw[0])
                else:
                    mi = l // 2
                    dog, dy, dgate, _ = _matmul_res_bwd(dx, r["y"], ssg1, r["w_o"])
                    do, dz, d_og = _dn_gnorm_bwd(dog, r["o"], r["proj"], dn_o_g[mi][None])
                    du, dw, dqg, dkd, dintra, degl = _dn_scan_bwd(do, r["w"], r["qg"], r["kd"], r["intra"], r["egl"],
                                                                   r["vn"], r["s0"])
                    dq, dk, dv, dgb, dbb = _dn_pre_bwd(r["q"], r["k"], r["v"], r["gb"], r["bb"], du, dw, dqg, dkd, dintra, degl)
                    dconv, dab16, dhead = _dn_prep_bwd(dq, dk, dv, dgb, dbb, r["conv"], r["proj"], r["alog"], r["dtb"])
                    dpre, dw_sc = _dn_sconv_bwd(dconv, r["proj"], r["w_sc"])
                    dx, dssg, dng = _premod_matmul_bwd(r["x"], dx, ssg1, ng1, [dpre, dz, dab16], r["w_proj"])
                    dw_in = jnp.concatenate([_wgrad(dpre, r["h"]), _wgrad(dz, r["h"]),
                                             _wgrad(dab16, r["h"])[:2 * NH]], axis=0)
                    ddn[mi] = dict(w_in=dw_in.reshape(NDEV, -1, D), w_out=_wgrad(r["og"], dy).reshape(NDEV, dsh, D),
                                   w_sconv=dw_sc[:SCW], a_log=dhead[0, :NH], dt_bias=dhead[1, :NH], o_g=d_og[0])
                d_mod[l][1] = dssg.at[:, 2].set(dgate[:, 0])
                d_norm[l][1] = dng[0]
        mix = [dcm[l // 2]["w_glu"], dcm[l // 2]["w_pw"]] if l % 2 == 0 else [ddn[l // 2]["w_in"], ddn[l // 2]["w_out"]]
        exchanges[l], token = _exchange_start(
            [dw_ffn_in[l][0].reshape(NDEV, tf, D), dw_ffn_in[l][1].reshape(NDEV, tf, D),
             dw_ffn_out[l][0].reshape(NDEV, FF // NDEV, D), dw_ffn_out[l][1].reshape(NDEV, FF // NDEV, D), *mix],
            False, f"grads_start_{l}")
    grad_x = dx.reshape(BL, T, D)

    dmod_loc = jnp.stack([jnp.stack(d_mod[l], axis=1) for l in range(DEPTH)]).reshape(DEPTH, BL, 9 * D)
    dmod_g, = _all_gather([dmod_loc], "gather_dmod")
    dmod_all = jnp.transpose(dmod_g, (1, 0, 2, 3)).reshape(DEPTH, NDEV * BL, 9 * D)
    g_w_ada = _ada_bwd(c_all, lax.dynamic_slice_in_dim(dmod_all, me * mcols, mcols, axis=2))

    got = [_exchange_wait(exchanges[l], dx, f"grads_wait_{l}") for l in reversed(range(DEPTH))][::-1]

    small = [jnp.sum(dmod_loc, axis=1), jnp.stack([jnp.stack(d_norm[l]) for l in range(DEPTH)]),
             jnp.stack([d["b_glu"] for d in dcm]), jnp.stack([d["w_dw"] for d in dcm]), jnp.stack([d["b_dw"] for d in dcm]),
             jnp.stack([d["ln_g"] for d in dcm]), jnp.stack([d["ln_b"] for d in dcm]), jnp.stack([d["b_pw"] for d in dcm]),
             jnp.stack([d["w_sconv"] for d in ddn]), jnp.stack([d["a_log"] for d in ddn]),
             jnp.stack([d["dt_bias"] for d in ddn]), jnp.stack([d["o_g"] for d in ddn]), d_final_g]
    small_parts, = _all_gather([_pack(small)], "gather_small_grads")
    names = ["b_ada", "norm_g", "cm_b_glu", "cm_w_dw", "cm_b_dw", "cm_ln_g", "cm_ln_b", "cm_b_pw",
             "dn_w_sconv", "dn_a_log", "dn_dt_bias", "dn_o_g", "final_g"]
    cols = lambda a, width: lax.dynamic_slice_in_dim(a, me * width, width, axis=a.ndim - 1)
    local = {"norm_g": lambda a: cols(a, dsh), "cm_w_dw": lambda a: cols(a, dsh), "dn_w_sconv": lambda a: cols(a, 3 * dsh)}
    full_shapes = [a.shape for a in small]
    parts = [_unpack(small_parts[d], full_shapes) for d in range(NDEV)]
    parts = [[local.get(nm, lambda a: a)(p) for nm, p in zip(names, ps)] for ps in parts]
    small_w = dict(b_ada=(b_ada, m_b_ada, v_b_ada), norm_g=(norm_g, m_norm_g, v_norm_g),
                   cm_b_glu=(cm_b_glu, m_cm_b_glu, v_cm_b_glu), cm_w_dw=(cm_w_dw, m_cm_w_dw, v_cm_w_dw),
                   cm_b_dw=(cm_b_dw, m_cm_b_dw, v_cm_b_dw), cm_ln_g=(cm_ln_g, m_cm_ln_g, v_cm_ln_g),
                   cm_ln_b=(cm_ln_b, m_cm_ln_b, v_cm_ln_b), cm_b_pw=(cm_b_pw, m_cm_b_pw, v_cm_b_pw),
                   dn_w_sconv=(dn_w_sconv, m_dn_w_sconv, v_dn_w_sconv), dn_a_log=(dn_a_log, m_dn_a_log, v_dn_a_log),
                   dn_dt_bias=(dn_dt_bias, m_dn_dt_bias, v_dn_dt_bias), dn_o_g=(dn_o_g, m_dn_o_g, v_dn_o_g),
                   final_g=(final_g, m_final_g, v_final_g))
    loc_shapes = [small_w[nm][0].shape for nm in names]
    packed_parts = jnp.stack([_pack(ps) for ps in parts])
    sres = _adamw(packed_parts, *[_pack([small_w[nm][q] for nm in names]) for q in range(3)])
    sres = [dict(zip(names, _unpack(r, loc_shapes))) for r in sres]

    res = {}

    def update(slots, wmv, view, back):
        w2, m2, v2 = [view(a) for a in wmv]
        outs = [lax.empty(w2.shape, F32) for _ in range(4)]
        for p, row0, col in slots:
            outs = _adamw_slot(p, w2, m2, v2, outs, row0, col)
        return [back(o) for o in outs]

    res["w_ffn_in"] = update([(got[l][s], 2 * l + s, 0) for l in range(DEPTH) for s in range(2)],
                             (w_ffn_in, m_w_ffn_in, v_w_ffn_in), lambda a: tr_ffn(a).reshape(-1, D),
                             lambda o: jnp.swapaxes(o.reshape(DEPTH, 2, tf, D), 2, 3))
    res["w_ffn_out"] = update([(got[l][2 + s], 2 * l + s, 0) for l in range(DEPTH) for s in range(2)],
                              (w_ffn_out, m_w_ffn_out, v_w_ffn_out), lambda a: a.reshape(-1, D),
                              lambda o: o.reshape(w_ffn_out.shape))
    cgl = cm_w_glu.shape[2]
    res["cm_w_glu"] = update([(got[2 * a][4], a, 0) for a in range(na)], (cm_w_glu, m_cm_w_glu, v_cm_w_glu),
                             lambda a: a.reshape(-1, cgl), lambda o: o.reshape(cm_w_glu.shape))
    res["cm_w_pw"] = update([(got[2 * a][5], a, 0) for a in range(na)], (cm_w_pw, m_cm_w_pw, v_cm_w_pw),
                            lambda a: a.reshape(-1, D), lambda o: o.reshape(cm_w_pw.shape))
    cdn = dn_w_in.shape[2]
    res["dn_w_in"] = update([(got[2 * i + 1][4], 0, i) for i in range(nb)], (dn_w_in, m_dn_w_in, v_dn_w_in),
                            lambda a: tr_dn(a).reshape(cdn, nb * D),
                            lambda o: jnp.transpose(o.reshape(cdn, nb, D), (1, 2, 0)))
    res["dn_w_out"] = update([(got[2 * i + 1][5], i, 0) for i in range(nb)], (dn_w_out, m_dn_w_out, v_dn_w_out),
                             lambda a: a.reshape(-1, D), lambda o: o.reshape(dn_w_out.shape))
    res["w_ada"] = [o.reshape(w_ada.shape) for o in
                    _adamw(g_w_ada, *[a.reshape(-1, mcols) for a in (w_ada, m_w_ada, v_w_ada)])]
    for nm in names:
        res[nm] = [sres[q][nm] for q in range(4)]

    order = ["norm_g", "w_ada", "b_ada", "w_ffn_in", "w_ffn_out", "cm_w_glu", "cm_b_glu", "cm_w_dw", "cm_b_dw", "cm_ln_g",
             "cm_ln_b", "cm_w_pw", "cm_b_pw", "dn_w_in", "dn_w_sconv", "dn_a_log", "dn_dt_bias", "dn_o_g", "dn_w_out", "final_g"]
    return (loss, grad_x, *[res[nm][0] for nm in order], *[res[nm][1] for nm in order],
            *[res[nm][2] for nm in order], *[res[nm][3] for nm in order])
```

```python
import functools

import jax
import jax.numpy as jnp
from jax import lax
from jax.experimental import pallas as pl
from jax.experimental.pallas import tpu as pltpu

F32 = jnp.float32
BF16 = jnp.bfloat16
HI = lax.Precision.HIGHEST
INV_PREC = lax.Precision.HIGH
MESH = pl.DeviceIdType.MESH
AXES = ("x", "y", "c")

NDEV = 8
D = 1024
T = 2048
BL = 2
FF = 2816
NH = 8
DH = 128
CW = 31
SCW = 4
CHUNK = 64
DEPTH = 4
EPS = 1e-6
LR, B1, B2, AEPS, WD, STEP = 0.001, 0.9, 0.999, 1e-08, 0.01, 10

VMEM_LIMIT_BYTES = 56 * 1024 * 1024
HALO = 32
SHALO = 8


def _pcall(body, **kw):
    return pl.pallas_call(body, **kw)


def _cp(sem=None):
    return pltpu.CompilerParams(dimension_semantics=sem, vmem_limit_bytes=VMEM_LIMIT_BYTES)


def _sds(shape, dtype):
    return jax.ShapeDtypeStruct(tuple(shape), dtype)


def _dot(a, b):
    return jnp.dot(a, b, preferred_element_type=F32)


def _dot_nt(a, b):
    return lax.dot_general(a, b, (((1,), (1,)), ((), ())), preferred_element_type=F32)


def _dot_tn(a, b):
    return lax.dot_general(a, b, (((0,), (0,)), ((), ())), preferred_element_type=F32)


def _modulate(x, ng, scale, shift):
    r = lax.rsqrt(jnp.mean(x * x, axis=-1, keepdims=True) + EPS)
    return (x * r * ng) * (1.0 + scale) + shift


def _acc_rows(ref, first, rows):
    @pl.when(first)
    def _():
        ref[...] = jnp.zeros_like(ref)

    for r, val in enumerate(rows):
        ref[r:r + 1, :] += val


def _my_pos():
    return lax.axis_index("x"), lax.axis_index("y"), lax.axis_index("c")


def _all_gather(arrs, name):
    n = len(arrs)

    def body(*refs):
        ins, outs = refs[:n], refs[n:2 * n]
        send, recv, loc = refs[2 * n:]
        x, y, c = _my_pos()
        me, sibling = (x, y, c), (x, y, 1 - c)
        chips = [(1 - x, y), (x, 1 - y), (1 - x, 1 - y)]

        def copy(a, k, block, to, src=None):
            dst = outs[a].at[4 * block[0] + 2 * block[1] + block[2]]
            return pltpu.make_async_remote_copy(
                src_ref=dst if src is None else src, dst_ref=dst,
                send_sem=send.at[7 * a + k], recv_sem=recv.at[7 * a + k],
                device_id=to, device_id_type=MESH)

        mine, first, passed = [], [], []
        for a in range(n):
            m = pltpu.make_async_copy(ins[a], outs[a].at[4 * x + 2 * y + c], loc.at[a])
            m.start()
            mine.append(m)
            f = [copy(a, 0, me, sibling, src=ins[a])]
            f += [copy(a, 1 + j, me, (*chip, c), src=ins[a]) for j, chip in enumerate(chips)]
            for cp in f:
                cp.start()
            first += f
        for a in range(n):
            for j, chip in enumerate(chips):
                copy(a, 1 + j, (*chip, c), me).wait_recv()
                p = copy(a, 4 + j, (*chip, c), sibling)
                p.start()
                passed.append(p)
        for a in range(n):
            copy(a, 0, sibling, me).wait_recv()
            for j, chip in enumerate(chips):
                copy(a, 4 + j, (*chip, 1 - c), me).wait_recv()
        for cp in first + passed:
            cp.wait_send()
        for m in mine:
            m.wait()

    hbm = pl.BlockSpec(memory_space=pl.ANY)
    return _pcall(
        body, name=name,
        out_shape=[_sds((NDEV,) + a.shape, a.dtype) for a in arrs],
        in_specs=[hbm] * n, out_specs=[hbm] * n,
        scratch_shapes=[pltpu.SemaphoreType.DMA((7 * n,)), pltpu.SemaphoreType.DMA((7 * n,)),
                        pltpu.SemaphoreType.DMA((n,))],
    )(*arrs)


def _peer(k):
    x, y, c = _my_pos()
    return (1 - x if k & 4 else x, 1 - y if k & 2 else y, 1 - c if k & 1 else c)


def _dev_index(p):
    return 4 * p[0] + 2 * p[1] + p[2]


_HBM = pl.BlockSpec(memory_space=pltpu.HBM)
_SEM = pl.BlockSpec(memory_space=pltpu.SEMAPHORE)
_EFFECT = pltpu.SideEffectType.DATAFLOW_SIDE_EFFECTING


def _exchange_start(srcs, gather, name):
    n = len(srcs)
    me = _dev_index(_my_pos())
    lands = []
    for s in srcs:
        own = s if gather else lax.dynamic_index_in_dim(s, me, 0, keepdims=False)
        shape = (NDEV,) + s.shape if gather else s.shape
        lands.append(lax.dynamic_update_index_in_dim(lax.empty(shape, s.dtype), own, me, 0))

    def body(*refs):
        src_refs, land_refs = refs[:n], refs[n:2 * n]
        sends, recvs = refs[2 * n:3 * n], refs[3 * n:4 * n]
        token = refs[-1]
        mine = _dev_index(_my_pos())
        for a in range(n):
            for k in range(1, 8):
                p = _peer(k)
                pltpu.make_async_remote_copy(
                    src_ref=src_refs[a] if gather else src_refs[a].at[_dev_index(p)],
                    dst_ref=land_refs[a].at[mine], send_sem=sends[a], recv_sem=recvs[a],
                    device_id=p, device_id_type=MESH).start()
        token[...] = jnp.zeros_like(token)

    out = pl.pallas_call(
        body, name=name,
        out_shape=(*[pltpu.SemaphoreType.DMA(())] * (2 * n),
                   *[pltpu.HBM(a.shape, a.dtype) for a in srcs], *[pltpu.HBM(a.shape, a.dtype) for a in lands],
                   _sds((8, 128), F32)),
        in_specs=[_HBM] * (2 * n),
        out_specs=(*[_SEM] * (2 * n), *[_HBM] * (2 * n), pl.BlockSpec(memory_space=pltpu.VMEM)),
        input_output_aliases={i: 2 * n + i for i in range(2 * n)},
        compiler_params=pltpu.CompilerParams(has_side_effects=_EFFECT),
    )(*[pltpu.with_memory_space_constraint(a, pltpu.HBM) for a in srcs],
      *[pltpu.with_memory_space_constraint(a, pltpu.HBM) for a in lands])
    state = (out[:n], out[n:2 * n], out[2 * n:3 * n], out[3 * n:4 * n])
    return state, out[-1][0, 0]


def _exchange_wait(state, after, name):
    sends, recvs, srcs, lands = state
    n = len(srcs)

    def body(*refs):
        land_refs = refs[n:2 * n]
        send_refs, recv_refs = refs[2 * n:3 * n], refs[3 * n:4 * n]
        for a in range(n):
            seven = land_refs[a].at[pl.ds(0, NDEV - 1)]
            cp = pltpu.make_async_remote_copy(src_ref=seven, dst_ref=seven, send_sem=send_refs[a], recv_sem=recv_refs[a],
                                              device_id=_peer(1), device_id_type=MESH)
            cp.wait_send()
            cp.wait_recv()

    out = pl.pallas_call(
        body, name=name,
        out_shape=(*[pltpu.HBM(a.shape, a.dtype) for a in srcs], *[pltpu.HBM(a.shape, a.dtype) for a in lands]),
        in_specs=(*[_HBM] * (2 * n), *[_SEM] * (2 * n), pl.BlockSpec(memory_space=pl.ANY)),
        out_specs=tuple([_HBM] * (2 * n)),
        input_output_aliases={i: i for i in range(2 * n)},
        compiler_params=pltpu.CompilerParams(has_side_effects=_EFFECT),
    )(*srcs, *lands, *sends, *recvs, after)
    return list(out[n:])


def _ffn_tiles():
    tm = min(512, T)
    return tm, T // tm


def _ffn_fwd(x, ssg, ng, w_in, w_out):
    n = x.shape[0]
    _, nf, tf, _ = w_in.shape
    tm, tpb = _ffn_tiles()

    def body(x_ref, ssg_ref, ng_ref, win_ref, wout_ref, xn_ref, gu_ref, hid_ref, y_ref, h_scr, acc):
        j = pl.program_id(1)

        @pl.when(j == 0)
        def _():
            s = ssg_ref[0]
            h_scr[...] = _modulate(x_ref[...], ng_ref[...], s[1:2], s[0:1]).astype(BF16)
            acc[...] = jnp.zeros_like(acc)

        h = h_scr[...]
        g = _dot_nt(h, win_ref[0])
        u = _dot_nt(h, win_ref[1])
        gu_ref[0] = g.astype(BF16)
        gu_ref[1] = u.astype(BF16)
        hid = (g * jax.nn.sigmoid(g) * u).astype(BF16)
        hid_ref[...] = hid
        acc[...] += _dot(hid, wout_ref[...])

        @pl.when(j == nf - 1)
        def _():
            yv = acc[...]
            y_ref[...] = yv.astype(BF16)
            xn_ref[...] = x_ref[...] + (0.5 * (1.0 + ssg_ref[0][2:3])) * yv

    return _pcall(
        body, name="ffn_fwd", grid=(n // tm, nf),
        in_specs=[pl.BlockSpec((tm, D), lambda i, j: (i, 0)),
                  pl.BlockSpec((1, 3, D), lambda i, j: (i // tpb, 0, 0)),
                  pl.BlockSpec((1, D), lambda i, j: (0, 0)),
                  pl.BlockSpec((2, None, tf, D), lambda i, j: (0, j, 0, 0)),
                  pl.BlockSpec((None, tf, D), lambda i, j: (j, 0, 0))],
        out_specs=[pl.BlockSpec((tm, D), lambda i, j: (i, 0)),
                   pl.BlockSpec((2, None, tm, tf), lambda i, j: (0, j, i, 0)),
                   pl.BlockSpec((None, tm, tf), lambda i, j: (j, i, 0)),
                   pl.BlockSpec((tm, D), lambda i, j: (i, 0))],
        out_shape=[_sds((n, D), F32), _sds((2, nf, n, tf), BF16), _sds((nf, n, tf), BF16), _sds((n, D), BF16)],
        scratch_shapes=[pltpu.VMEM((tm, D), BF16), pltpu.VMEM((tm, D), F32)],
        compiler_params=_cp(("arbitrary", "arbitrary")),
    )(x, ssg, ng, w_in, w_out)


def _ffn_bwd_a(x, dxn, ssg, ng, y, gu, w_in, w_out):
    n = x.shape[0]
    _, nf, tf, _ = w_in.shape
    tm, tpb = _ffn_tiles()

    def body(x_ref, dxn_ref, ssg_ref, ng_ref, y_ref, gu_ref, win_ref, wout_ref,
             dx_ref, dgu_ref, h_ref, dout_ref, dssg_ref, dng_ref, dout_scr, dh_acc):
        i, j = pl.program_id(0), pl.program_id(1)

        @pl.when(j == 0)
        def _():
            db = ((0.5 * (1.0 + ssg_ref[0][2:3])) * dxn_ref[...]).astype(BF16)
            dout_scr[...] = db
            dout_ref[...] = db
            dh_acc[...] = jnp.zeros_like(dh_acc)

        dhid = _dot_nt(dout_scr[...], wout_ref[...])
        g = gu_ref[0].astype(F32)
        u = gu_ref[1].astype(F32)
        sig = jax.nn.sigmoid(g)
        dg = (dhid * u * (sig * (1.0 + g * (1.0 - sig)))).astype(BF16)
        du = (dhid * (g * sig)).astype(BF16)
        dgu_ref[0] = dg
        dgu_ref[1] = du
        dh_acc[...] += _dot(dg, win_ref[0]) + _dot(du, win_ref[1])

        @pl.when(j == nf - 1)
        def _():
            s = ssg_ref[0]
            h, vjp = jax.vjp(_modulate, x_ref[...], ng_ref[...], s[1:2], s[0:1])
            dx_, dng_, dsc_, dsh_ = vjp(dh_acc[...])
            h_ref[...] = h.astype(BF16)
            dxn = dxn_ref[...]
            dx_ref[...] = dxn + dx_
            dgate = jnp.sum(0.5 * dxn * y_ref[...].astype(F32), axis=0, keepdims=True)
            _acc_rows(dssg_ref.at[0], i % tpb == 0, [dsh_, dsc_, dgate])
            _acc_rows(dng_ref, i == 0, [dng_])

    return _pcall(
        body, name="ffn_bwd_a", grid=(n // tm, nf),
        in_specs=[pl.BlockSpec((tm, D), lambda i, j: (i, 0)),
                  pl.BlockSpec((tm, D), lambda i, j: (i, 0)),
                  pl.BlockSpec((1, 3, D), lambda i, j: (i // tpb, 0, 0)),
                  pl.BlockSpec((1, D), lambda i, j: (0, 0)),
                  pl.BlockSpec((tm, D), lambda i, j: (i, 0)),
                  pl.BlockSpec((2, None, tm, tf), lambda i, j: (0, j, i, 0)),
                  pl.BlockSpec((2, None, tf, D), lambda i, j: (0, j, 0, 0)),
                  pl.BlockSpec((None, tf, D), lambda i, j: (j, 0, 0))],
        out_specs=[pl.BlockSpec((tm, D), lambda i, j: (i, 0)),
                   pl.BlockSpec((2, None, tm, tf), lambda i, j: (0, j, i, 0)),
                   pl.BlockSpec((tm, D), lambda i, j: (i, 0)),
                   pl.BlockSpec((tm, D), lambda i, j: (i, 0)),
                   pl.BlockSpec((1, 3, D), lambda i, j: (i // tpb, 0, 0)),
                   pl.BlockSpec((1, D), lambda i, j: (0, 0))],
        out_shape=[_sds((n, D), F32), _sds((2, nf, n, tf), BF16), _sds((n, D), BF16), _sds((n, D), BF16),
                   _sds((BL, 3, D), F32), _sds((1, D), F32)],
        scratch_shapes=[pltpu.VMEM((tm, D), BF16), pltpu.VMEM((tm, D), F32)],
        compiler_params=_cp(("arbitrary", "arbitrary")),
    )(x, dxn, ssg, ng, y, gu, w_in, w_out)


def _ffn_bwd_w(h, dgu, hid, dout):
    n = h.shape[0]
    _, nf, _, tf = dgu.shape
    tm, _ = _ffn_tiles()
    ni = n // tm

    def body(h_ref, dgu_ref, hid_ref, dout_ref, dwin_ref, dwout_ref, acc_g, acc_u, acc_o):
        i = pl.program_id(1)

        @pl.when(i == 0)
        def _():
            acc_g[...] = jnp.zeros_like(acc_g)
            acc_u[...] = jnp.zeros_like(acc_u)
            acc_o[...] = jnp.zeros_like(acc_o)

        hv = h_ref[...]
        acc_g[...] += _dot_tn(dgu_ref[0], hv)
        acc_u[...] += _dot_tn(dgu_ref[1], hv)
        acc_o[...] += _dot_tn(hid_ref[...], dout_ref[...])

        @pl.when(i == ni - 1)
        def _():
            dwin_ref[0] = acc_g[...].astype(BF16)
            dwin_ref[1] = acc_u[...].astype(BF16)
            dwout_ref[...] = acc_o[...].astype(BF16)

    return _pcall(
        body, name="ffn_bwd_w", grid=(nf, ni),
        in_specs=[pl.BlockSpec((tm, D), lambda j, i: (i, 0)),
                  pl.BlockSpec((2, None, tm, tf), lambda j, i: (0, j, i, 0)),
                  pl.BlockSpec((None, tm, tf), lambda j, i: (j, i, 0)),
                  pl.BlockSpec((tm, D), lambda j, i: (i, 0))],
        out_specs=[pl.BlockSpec((2, None, tf, D), lambda j, i: (0, j, 0, 0)),
                   pl.BlockSpec((None, tf, D), lambda j, i: (j, 0, 0))],
        out_shape=[_sds((2, nf, tf, D), BF16), _sds((nf, tf, D), BF16)],
        scratch_shapes=[pltpu.VMEM((tf, D), F32), pltpu.VMEM((tf, D), F32), pltpu.VMEM((tf, D), F32)],
        compiler_params=_cp(("arbitrary", "arbitrary")),
    )(h, dgu, hid, dout)


def _premod_matmul(x, ssg, ng, w, bias, tn):
    n = x.shape[0]
    shards = w.ndim == 3
    m = w.shape[0] * w.shape[2] if shards else w.shape[0]
    tm = min(256, T)
    tpb = T // tm
    w_spec = (pl.BlockSpec((None, D, tn), lambda i, j: (j, 0, 0)) if shards
              else pl.BlockSpec((tn, D), lambda i, j: (j, 0)))

    def body(x_ref, ssg_ref, ng_ref, w_ref, b_ref, h_ref, o_ref, h_scr):
        @pl.when(pl.program_id(1) == 0)
        def _():
            s = ssg_ref[0]
            hb = _modulate(x_ref[...], ng_ref[...], s[1:2], s[0:1]).astype(BF16)
            h_scr[...] = hb
            h_ref[...] = hb

        hv = h_scr[...]
        o_ref[...] = (_dot(hv, w_ref[...]) if shards else _dot_nt(hv, w_ref[...])) + b_ref[...]

    return _pcall(
        body, name="premod_matmul", grid=(n // tm, m // tn),
        in_specs=[pl.BlockSpec((tm, D), lambda i, j: (i, 0)),
                  pl.BlockSpec((1, 3, D), lambda i, j: (i // tpb, 0, 0)),
                  pl.BlockSpec((1, D), lambda i, j: (0, 0)),
                  w_spec,
                  pl.BlockSpec((1, tn), lambda i, j: (0, j))],
        out_specs=[pl.BlockSpec((tm, D), lambda i, j: (i, 0)),
                   pl.BlockSpec((tm, tn), lambda i, j: (i, j))],
        out_shape=[_sds((n, D), BF16), _sds((n, m), F32)],
        scratch_shapes=[pltpu.VMEM((tm, D), BF16)],
        compiler_params=_cp(("arbitrary", "arbitrary")),
    )(x, ssg, ng, w, bias)


def _premod_matmul_bwd(x, dxn, ssg, ng, douts, w):
    n = x.shape[0]
    k = len(douts)
    shards = w.ndim == 3
    tm = min(256, T)
    tpb = T // tm

    def body(*refs):
        x_ref, dxn_ref, ssg_ref, ng_ref = refs[:4]
        do_refs, w_ref = refs[4:4 + k], refs[4 + k]
        dx_ref, dssg_ref, dng_ref = refs[5 + k:]
        i = pl.program_id(0)
        dh = jnp.zeros((tm, D), F32)
        if shards:
            cs = w.shape[2]
            dov = do_refs[0][...]
            for j in range(w.shape[0]):
                dh += _dot_nt(dov[:, j * cs:(j + 1) * cs], w_ref[j])
        else:
            off = 0
            for q in range(k):
                mk = douts[q].shape[1]
                dh += _dot(do_refs[q][...], w_ref[off:off + mk, :])
                off += mk
        s = ssg_ref[0]
        _, vjp = jax.vjp(_modulate, x_ref[...], ng_ref[...], s[1:2], s[0:1])
        dx_, dng_, dsc_, dsh_ = vjp(dh)
        dx_ref[...] = dxn_ref[...] + dx_
        _acc_rows(dssg_ref.at[0], i % tpb == 0, [dsh_, dsc_, jnp.zeros_like(dsh_)])
        _acc_rows(dng_ref, i == 0, [dng_])

    return _pcall(
        body, name="premod_matmul_bwd", grid=(n // tm,),
        in_specs=[pl.BlockSpec((tm, D), lambda i: (i, 0)),
                  pl.BlockSpec((tm, D), lambda i: (i, 0)),
                  pl.BlockSpec((1, 3, D), lambda i: (i // tpb, 0, 0)),
                  pl.BlockSpec((1, D), lambda i: (0, 0))]
                 + [pl.BlockSpec((tm, a.shape[1]), lambda i: (i, 0)) for a in douts]
                 + [pl.BlockSpec(w.shape, (lambda i: (0, 0, 0)) if shards else (lambda i: (0, 0)))],
        out_specs=[pl.BlockSpec((tm, D), lambda i: (i, 0)),
                   pl.BlockSpec((1, 3, D), lambda i: (i // tpb, 0, 0)),
                   pl.BlockSpec((1, D), lambda i: (0, 0))],
        out_shape=[_sds((n, D), F32), _sds((BL, 3, D), F32), _sds((1, D), F32)],
        compiler_params=_cp(("arbitrary",)),
    )(x, dxn, ssg, ng, *douts, w)


def _matmul_res(x, a, ssg, w, bias):
    n, kd = a.shape
    tm = min(512, T)
    tpb = T // tm

    def body(x_ref, a_ref, ssg_ref, w_ref, b_ref, xn_ref, y_ref):
        yv = _dot(a_ref[...], w_ref[...]) + b_ref[...]
        y_ref[...] = yv.astype(BF16)
        xn_ref[...] = x_ref[...] + (1.0 + ssg_ref[0][2:3]) * yv

    return _pcall(
        body, name="matmul_res", grid=(n // tm,),
        in_specs=[pl.BlockSpec((tm, D), lambda i: (i, 0)),
                  pl.BlockSpec((tm, kd), lambda i: (i, 0)),
                  pl.BlockSpec((1, 3, D), lambda i: (i // tpb, 0, 0)),
                  pl.BlockSpec((kd, D), lambda i: (0, 0)),
                  pl.BlockSpec((1, D), lambda i: (0, 0))],
        out_specs=[pl.BlockSpec((tm, D), lambda i: (i, 0)), pl.BlockSpec((tm, D), lambda i: (i, 0))],
        out_shape=[_sds((n, D), F32), _sds((n, D), BF16)],
        compiler_params=_cp(("arbitrary",)),
    )(x, a, ssg, w, bias)


def _matmul_res_bwd(dxn, y, ssg, w):
    n = dxn.shape[0]
    kd = w.shape[0]
    tm = min(512, T)
    tpb = T // tm

    def body(dxn_ref, y_ref, ssg_ref, w_ref, da_ref, dy_ref, dgate_ref, dbias_ref):
        i = pl.program_id(0)
        dxn = dxn_ref[...]
        dy = (1.0 + ssg_ref[0][2:3]) * dxn
        dyb = dy.astype(BF16)
        dy_ref[...] = dyb
        da_ref[...] = _dot_nt(dyb, w_ref[...])
        _acc_rows(dgate_ref.at[0], i % tpb == 0, [jnp.sum(dxn * y_ref[...].astype(F32), axis=0, keepdims=True)])
        _acc_rows(dbias_ref, i == 0, [jnp.sum(dy, axis=0, keepdims=True)])

    return _pcall(
        body, name="matmul_res_bwd", grid=(n // tm,),
        in_specs=[pl.BlockSpec((tm, D), lambda i: (i, 0)),
                  pl.BlockSpec((tm, D), lambda i: (i, 0)),
                  pl.BlockSpec((1, 3, D), lambda i: (i // tpb, 0, 0)),
                  pl.BlockSpec((kd, D), lambda i: (0, 0))],
        out_specs=[pl.BlockSpec((tm, kd), lambda i: (i, 0)),
                   pl.BlockSpec((tm, D), lambda i: (i, 0)),
                   pl.BlockSpec((1, 1, D), lambda i: (i // tpb, 0, 0)),
                   pl.BlockSpec((1, D), lambda i: (0, 0))],
        out_shape=[_sds((n, kd), F32), _sds((n, D), BF16), _sds((BL, 1, D), F32), _sds((1, D), F32)],
        compiler_params=_cp(("arbitrary",)),
    )(dxn, y, ssg, w)


def _wgrad_shards(a, b, ns):
    n, kd = a.shape
    cs = b.shape[1] // ns
    tm = min(512, T)
    ni = n // tm

    def body(a_ref, b_ref, o_ref, acc):
        i = pl.program_id(1)

        @pl.when(i == 0)
        def _():
            acc[...] = jnp.zeros_like(acc)

        acc[...] += _dot_tn(a_ref[...], b_ref[...])

        @pl.when(i == ni - 1)
        def _():
            o_ref[...] = acc[...].astype(BF16)

    return _pcall(
        body, name="wgrad_shards", grid=(ns, ni),
        in_specs=[pl.BlockSpec((tm, kd), lambda q, i: (i, 0)), pl.BlockSpec((tm, cs), lambda q, i: (i, q))],
        out_specs=pl.BlockSpec((None, kd, cs), lambda q, i: (q, 0, 0)),
        out_shape=_sds((ns, kd, cs), BF16),
        scratch_shapes=[pltpu.VMEM((kd, cs), F32)],
        compiler_params=_cp(("arbitrary", "arbitrary")),
    )(a, b)


def _wgrad(a, b):
    n, kd = a.shape
    m = b.shape[1]
    tm = min(512, T)
    tk = min(512, kd)
    ni = n // tm

    def body(a_ref, b_ref, o_ref, acc):
        i = pl.program_id(1)

        @pl.when(i == 0)
        def _():
            acc[...] = jnp.zeros_like(acc)

        acc[...] += _dot_tn(a_ref[...], b_ref[...])

        @pl.when(i == ni - 1)
        def _():
            o_ref[...] = acc[...].astype(BF16)

    return _pcall(
        body, name="wgrad", grid=(kd // tk, ni),
        in_specs=[pl.BlockSpec((tm, tk), lambda q, i: (i, q)), pl.BlockSpec((tm, m), lambda q, i: (i, 0))],
        out_specs=pl.BlockSpec((tk, m), lambda q, i: (q, 0)),
        out_shape=_sds((kd, m), BF16),
        scratch_shapes=[pltpu.VMEM((tk, m), F32)],
        compiler_params=_cp(("arbitrary", "arbitrary")),
    )(a, b)


def _ln_silu(u1, g, b):
    mu = jnp.mean(u1, axis=-1, keepdims=True)
    xc = u1 - mu
    var = jnp.mean(xc * xc, axis=-1, keepdims=True)
    ln = xc * lax.rsqrt(var + EPS) * g + b
    return ln * jax.nn.sigmoid(ln)


def _conv_tiles():
    tt = min(256, T)
    return tt, T // tt


def _prev_halo_spec(cols, tt, halo):
    r = tt // halo
    return pl.BlockSpec((halo, cols), lambda b, i: (jnp.maximum(b * (T // halo) + i * r - 1, 0), 0))


def _next_halo_spec(cols, tt, halo):
    r = tt // halo
    last = BL * T // halo - 1
    return pl.BlockSpec((halo, cols), lambda b, i: (jnp.minimum(b * (T // halo) + (i + 1) * r, last), 0))


def _cm_mid_fwd(ab, w_dw, b_dw, ln_g, ln_b):
    n = ab.shape[0]
    tt, nt = _conv_tiles()

    def body(ab_ref, halo_ref, w_ref, bdw_ref, g_ref, b_ref, u1_ref, u2_ref, win):
        i = pl.program_id(1)
        hv = halo_ref[...]
        u0h = hv[:, :D] * jax.nn.sigmoid(hv[:, D:])
        win[0:HALO, :] = jnp.where(i == 0, 0.0, u0h)
        cv = ab_ref[...]
        win[HALO:HALO + tt, :] = cv[:, :D] * jax.nn.sigmoid(cv[:, D:])
        acc = jnp.zeros((tt, D), F32) + bdw_ref[...]
        for k in range(CW):
            acc += w_ref[k:k + 1, :] * win[pl.ds(HALO - (CW - 1) + k, tt), :]
        u1_ref[...] = acc
        u2_ref[...] = _ln_silu(acc, g_ref[...], b_ref[...]).astype(BF16)

    row = lambda b, i: (b * nt + i, 0)
    vec = pl.BlockSpec((1, D), lambda b, i: (0, 0))
    return _pcall(
        body, name="cm_mid_fwd", grid=(BL, nt),
        in_specs=[pl.BlockSpec((tt, 2 * D), row), _prev_halo_spec(2 * D, tt, HALO),
                  pl.BlockSpec((HALO, D), lambda b, i: (0, 0)), vec, vec, vec],
        out_specs=[pl.BlockSpec((tt, D), row), pl.BlockSpec((tt, D), row)],
        out_shape=[_sds((n, D), F32), _sds((n, D), BF16)],
        scratch_shapes=[pltpu.VMEM((HALO + tt, D), F32)],
        compiler_params=_cp(("arbitrary", "arbitrary")),
    )(ab, ab, w_dw, b_dw, ln_g, ln_b)


def _cm_mid_bwd_a(du2, u1, ln_g, ln_b):
    n = du2.shape[0]
    tm = min(256, T)

    def body(du2_ref, u1_ref, g_ref, b_ref, du1_ref, dln_ref):
        _, vjp = jax.vjp(_ln_silu, u1_ref[...], g_ref[...], b_ref[...])
        du1, dg, db = vjp(du2_ref[...])
        du1_ref[...] = du1
        _acc_rows(dln_ref, pl.program_id(0) == 0, [dg, db])

    vec = pl.BlockSpec((1, D), lambda i: (0, 0))
    return _pcall(
        body, name="cm_mid_bwd_a", grid=(n // tm,),
        in_specs=[pl.BlockSpec((tm, D), lambda i: (i, 0)), pl.BlockSpec((tm, D), lambda i: (i, 0)), vec, vec],
        out_specs=[pl.BlockSpec((tm, D), lambda i: (i, 0)), pl.BlockSpec((2, D), lambda i: (0, 0))],
        out_shape=[_sds((n, D), F32), _sds((2, D), F32)],
        compiler_params=_cp(("arbitrary",)),
    )(du2, u1, ln_g, ln_b)


def _cm_mid_bwd_b(du1, ab, w_dw):
    n = du1.shape[0]
    tt, nt = _conv_tiles()

    def body(du1_ref, nxt_ref, ab_ref, halo_ref, w_ref, dab_ref, dw_ref, dbdw_ref, dbglu_ref, dwin, uwin):
        b, i = pl.program_id(0), pl.program_id(1)
        first = jnp.logical_and(b == 0, i == 0)
        d1 = du1_ref[...]
        dwin[0:tt, :] = d1
        dwin[tt:tt + HALO, :] = jnp.where(i == nt - 1, 0.0, nxt_ref[...])
        hv = halo_ref[...]
        uwin[0:HALO, :] = jnp.where(i == 0, 0.0, hv[:, :D] * jax.nn.sigmoid(hv[:, D:]))
        cv = ab_ref[...]
        av, sg = cv[:, :D], jax.nn.sigmoid(cv[:, D:])
        uwin[HALO:HALO + tt, :] = av * sg
        du0 = jnp.zeros((tt, D), F32)
        dws = []
        for k in range(CW):
            du0 += w_ref[k:k + 1, :] * dwin[pl.ds(CW - 1 - k, tt), :]
            dws.append(jnp.sum(d1 * uwin[pl.ds(HALO - (CW - 1) + k, tt), :], axis=0, keepdims=True))
        dws += [jnp.zeros((1, D), F32)] * (HALO - CW)
        _acc_rows(dw_ref, first, dws)
        _acc_rows(dbdw_ref, first, [jnp.sum(d1, axis=0, keepdims=True)])
        da = du0 * sg
        db = du0 * av * sg * (1.0 - sg)
        dab_ref[:, :D] = da.astype(BF16)
        dab_ref[:, D:] = db.astype(BF16)
        _acc_rows(dbglu_ref.at[:, 0:D], first, [jnp.sum(da, axis=0, keepdims=True)])
        _acc_rows(dbglu_ref.at[:, D:2 * D], first, [jnp.sum(db, axis=0, keepdims=True)])

    row = lambda b, i: (b * nt + i, 0)
    return _pcall(
        body, name="cm_mid_bwd_b", grid=(BL, nt),
        in_specs=[pl.BlockSpec((tt, D), row), _next_halo_spec(D, tt, HALO),
                  pl.BlockSpec((tt, 2 * D), row), _prev_halo_spec(2 * D, tt, HALO),
                  pl.BlockSpec((HALO, D), lambda b, i: (0, 0))],
        out_specs=[pl.BlockSpec((tt, 2 * D), row), pl.BlockSpec((HALO, D), lambda b, i: (0, 0)),
                   pl.BlockSpec((1, D), lambda b, i: (0, 0)), pl.BlockSpec((1, 2 * D), lambda b, i: (0, 0))],
        out_shape=[_sds((n, 2 * D), BF16), _sds((HALO, D), F32), _sds((1, D), F32), _sds((1, 2 * D), F32)],
        scratch_shapes=[pltpu.VMEM((tt + HALO, D), F32), pltpu.VMEM((HALO + tt, D), F32)],
        compiler_params=_cp(("arbitrary", "arbitrary")),
    )(du1, du1, ab, ab, w_dw)


def _softplus(v):
    return jnp.maximum(v, 0.0) + jnp.log(1.0 + jnp.exp(-jnp.abs(v)))


def _g_beta(ab, alog, dtb):
    return -jnp.exp(alog) * _softplus(ab + dtb), jax.nn.sigmoid(ab)


def _dn_sconv_fwd(proj, w_sc, alog, dtb):
    n = proj.shape[0]
    tt, nt = _conv_tiles()
    w3 = 3 * D

    def body(qkv_ref, halo_ref, ab_ref, w_ref, alog_ref, dtb_ref, conv_ref, q_ref, k_ref, v_ref, gb_ref, bb_ref, win):
        i = pl.program_id(1)
        win[0:SHALO, :] = jnp.where(i == 0, 0.0, halo_ref[...])
        win[SHALO:SHALO + tt, :] = qkv_ref[...]
        acc = jnp.zeros((tt, w3), F32)
        for k in range(SCW):
            acc += w_ref[k:k + 1, :] * win[pl.ds(SHALO - (SCW - 1) + k, tt), :]
        conv_ref[...] = acc
        act = acc * jax.nn.sigmoid(acc)
        gfull, bfull = _g_beta(ab_ref[...], alog_ref[...], dtb_ref[...])
        for h in range(NH):
            q_ref[0, h] = act[:, h * DH:(h + 1) * DH]
            k_ref[0, h] = act[:, D + h * DH:D + (h + 1) * DH]
            v_ref[0, h] = act[:, 2 * D + h * DH:2 * D + (h + 1) * DH]
            gb_ref[0, h] = jnp.broadcast_to(gfull[:, h:h + 1], (tt, DH))
            bb_ref[0, h] = jnp.broadcast_to(bfull[:, NH + h:NH + h + 1], (tt, DH))

    row = lambda b, i: (b * nt + i, 0)
    head = pl.BlockSpec((1, NH, tt, DH), lambda b, i: (b, 0, i, 0))
    vec = pl.BlockSpec((1, 128), lambda b, i: (0, 0))
    hs = _sds((BL, NH, T, DH), F32)
    return _pcall(
        body, name="dn_sconv_fwd", grid=(BL, nt),
        in_specs=[pl.BlockSpec((tt, w3), row), _prev_halo_spec(w3, tt, SHALO),
                  pl.BlockSpec((tt, 128), lambda b, i: (b * nt + i, 4 * D // 128)),
                  pl.BlockSpec((SHALO, w3), lambda b, i: (0, 0)), vec, vec],
        out_specs=[pl.BlockSpec((tt, w3), row), head, head, head, head, head],
        out_shape=[_sds((n, w3), F32), hs, hs, hs, hs, hs],
        scratch_shapes=[pltpu.VMEM((SHALO + tt, w3), F32)],
        compiler_params=_cp(("arbitrary", "arbitrary")),
    )(proj, proj, proj, w_sc, alog, dtb)


_BMM_SPEC = {"nn": "gij,gjk->gik", "nt": "gid,gjd->gij", "tn": "gcd,gce->gde"}


def _mm(kind, a, b, prec):
    if prec is None:
        return jnp.einsum(_BMM_SPEC[kind], a.astype(BF16), b.astype(BF16), preferred_element_type=F32)
    return jnp.einsum(_BMM_SPEC[kind], a, b, preferred_element_type=F32, precision=prec)


@functools.partial(jax.custom_vjp, nondiff_argnums=(0, 3))
def _bmm_k(kind, a, b, prec):
    return _mm(kind, a, b, prec)


def _bmm_k_fwd(kind, a, b, prec):
    return _mm(kind, a, b, prec), (a, b)


def _bmm_k_bwd(kind, prec, res, dc):
    a, b = res
    if kind == "nn":
        return _bmm_k("nt", dc, b, prec), _bmm_k("tn", a, dc, prec)
    if kind == "nt":
        return _bmm_k("nn", dc, b, prec), _bmm_k("tn", dc, a, prec)
    return _bmm_k("nt", b, dc, prec), _bmm_k("nn", a, dc, prec)


_bmm_k.defvjp(_bmm_k_fwd, _bmm_k_bwd)


def _bmm(a, b, prec=None):
    return _bmm_k("nn", a, b, prec)


def _bmm_nt(a, b, prec=None):
    return _bmm_k("nt", a, b, prec)


def _bmm_tn(a, b, prec=None):
    return _bmm_k("tn", a, b, prec)


def _bmm_raw(a, b):
    return _mm("nn", a, b, None)


def _bmm_nt_raw(a, b):
    return _mm("nt", a, b, None)


def _bmm_tn_raw(a, b):
    return _mm("tn", a, b, None)


@jax.custom_vjp
def _unit_lower_inverse(a):
    eye = (lax.broadcasted_iota(jnp.int32, a.shape, 1) == lax.broadcasted_iota(jnp.int32, a.shape, 2)).astype(F32)
    t = eye - a
    p = a
    for _ in range(CHUNK.bit_length() - 2):
        p = _mm("nn", p, p, INV_PREC)
        t = _mm("nn", t, eye + p, INV_PREC)
    return t


def _uli_fwd(a):
    t = _unit_lower_inverse(a)
    return t, t


def _uli_bwd(t, dt):
    return (-_bmm_nt(_bmm_tn(t, dt, INV_PREC), t, INV_PREC),)


_unit_lower_inverse.defvjp(_uli_fwd, _uli_bwd)


def _dn_pre(q, k, v, gb, bb):
    shape = (q.shape[0], CHUNK, CHUNK)
    ri = lax.broadcasted_iota(jnp.int32, shape, 1)
    ci = lax.broadcasted_iota(jnp.int32, shape, 2)
    causal, strict = ri >= ci, ri > ci
    qn = q * lax.rsqrt(jnp.sum(q * q, axis=-1, keepdims=True) + EPS) * (DH ** -0.5)
    kn = k * lax.rsqrt(jnp.sum(k * k, axis=-1, keepdims=True) + EPS)
    gcs = _bmm(causal.astype(F32), gb, HI)
    gcol = gcs[:, :, :CHUNK]
    decay = jnp.exp(jnp.where(causal, gcol - jnp.swapaxes(gcol, 1, 2), -jnp.inf))
    eg = jnp.exp(gcs)
    kb = kn * bb
    a = jnp.where(strict, _bmm_nt(kb, kn) * decay, 0.0)
    tm = _unit_lower_inverse(a)
    u = _bmm(tm, v * bb)
    w = _bmm(tm, kb * eg)
    qg = qn * eg
    intra = _bmm_nt(qn, kn) * decay
    glast = gcs[:, CHUNK - 1:CHUNK, :]
    kd = kn * jnp.exp(glast - gcs)
    egl = jnp.broadcast_to(jnp.exp(glast), (q.shape[0], 8, DH))
    return u, w, qg, kd, intra, egl


def _pre_tiles():
    gcn = min(8, T // CHUNK)
    return gcn, T // (CHUNK * gcn)


def _dn_pre_specs():
    gcn, _ = _pre_tiles()
    tok = pl.BlockSpec((None, None, gcn * CHUNK, DH), lambda b, h, i: (b, h, i, 0))
    sq = pl.BlockSpec((None, None, gcn * CHUNK, CHUNK), lambda b, h, i: (b, h, i, 0))
    per = pl.BlockSpec((None, None, gcn * 8, DH), lambda b, h, i: (b, h, i, 0))
    return tok, sq, per


def _dn_pre_fwd(q, k, v, gb, bb):
    gcn, ng = _pre_tiles()
    tok, sq, per = _dn_pre_specs()

    def body(q_ref, k_ref, v_ref, gb_ref, bb_ref, u_ref, w_ref, qg_ref, kd_ref, in_ref, egl_ref):
        args = [r[...].reshape(gcn, CHUNK, DH) for r in (q_ref, k_ref, v_ref, gb_ref, bb_ref)]
        u, w, qg, kd, intra, egl = _dn_pre(*args)
        for r, val in ((u_ref, u), (w_ref, w), (qg_ref, qg), (kd_ref, kd)):
            r[...] = val.reshape(gcn * CHUNK, DH)
        in_ref[...] = intra.reshape(gcn * CHUNK, CHUNK)
        egl_ref[...] = egl.reshape(gcn * 8, DH)

    hs = _sds((BL, NH, T, DH), F32)
    return _pcall(
        body, name="dn_pre_fwd", grid=(BL, NH, ng),
        in_specs=[tok] * 5, out_specs=[tok, tok, tok, tok, sq, per],
        out_shape=[hs, hs, hs, hs, _sds((BL, NH, T, CHUNK), F32), _sds((BL, NH, T // CHUNK * 8, DH), F32)],
        compiler_params=_cp(("arbitrary",) * 3),
    )(q, k, v, gb, bb)


def _dn_pre_bwd(q, k, v, gb, bb, du, dw, dqg, dkd, dintra, degl):
    gcn, ng = _pre_tiles()
    tok, sq, per = _dn_pre_specs()

    def body(q_ref, k_ref, v_ref, gb_ref, bb_ref, du_ref, dw_ref, dqg_ref, dkd_ref, din_ref, degl_ref,
             dq_ref, dk_ref, dv_ref, dgb_ref, dbb_ref):
        args = [r[...].reshape(gcn, CHUNK, DH) for r in (q_ref, k_ref, v_ref, gb_ref, bb_ref)]
        _, vjp = jax.vjp(_dn_pre, *args)
        cts = [r[...].reshape(gcn, CHUNK, DH) for r in (du_ref, dw_ref, dqg_ref, dkd_ref)]
        de = degl_ref[...].reshape(gcn, 8, DH)
        one = jnp.logical_and(lax.broadcasted_iota(jnp.int32, de.shape, 1) == 0,
                              lax.broadcasted_iota(jnp.int32, de.shape, 2) == 0)
        outs = vjp((*cts, din_ref[...].reshape(gcn, CHUNK, CHUNK), jnp.where(one, de, 0.0)))
        for r, val in zip((dq_ref, dk_ref, dv_ref, dgb_ref, dbb_ref), outs):
            r[...] = val.reshape(gcn * CHUNK, DH)

    hs = _sds((BL, NH, T, DH), F32)
    return _pcall(
        body, name="dn_pre_bwd", grid=(BL, NH, ng),
        in_specs=[tok] * 9 + [sq, per], out_specs=[tok] * 5, out_shape=[hs] * 5,
        compiler_params=_cp(("arbitrary",) * 3),
    )(q, k, v, gb, bb, du, dw, dqg, dkd, dintra, degl)


def _scan_tiles():
    cs = min(2, T // CHUNK)
    return cs, T // (CHUNK * cs)


def _dn_scan_fwd(u, w, qg, kd, intra, egl):
    cs, ns = _scan_tiles()
    g = BL * NH
    nc = T // CHUNK

    def body(u_ref, w_ref, qg_ref, kd_ref, in_ref, egl_ref, o_ref, vn_ref, s0_ref, s_scr):
        @pl.when(pl.program_id(0) == 0)
        def _():
            s_scr[...] = jnp.zeros_like(s_scr)

        for c in range(cs):
            rows = pl.ds(c * CHUNK, CHUNK)
            s = s_scr[...]
            s0_ref[:, :, c] = s.reshape(BL, NH, DH, DH)

            def ld(r, m=DH):
                return r[:, :, rows, :].reshape(g, CHUNK, m)

            vn = ld(u_ref) - _bmm_raw(ld(w_ref), s)
            o = _bmm_raw(ld(qg_ref), s) + _bmm_raw(ld(in_ref, CHUNK), vn)
            e = egl_ref[:, :, pl.ds(c * 8, 1), :].reshape(g, 1, DH)
            s_scr[...] = s * e + _bmm_tn_raw(ld(kd_ref), vn)
            vn_ref[:, :, rows, :] = vn.reshape(BL, NH, CHUNK, DH)
            o_ref[:, :, rows, :] = o.reshape(BL, NH, CHUNK, DH)

    tok = pl.BlockSpec((BL, NH, cs * CHUNK, DH), lambda i: (0, 0, i, 0))
    hs = _sds((BL, NH, T, DH), F32)
    return _pcall(
        body, name="dn_scan_fwd", grid=(ns,),
        in_specs=[tok, tok, tok, tok, pl.BlockSpec((BL, NH, cs * CHUNK, CHUNK), lambda i: (0, 0, i, 0)),
                  pl.BlockSpec((BL, NH, cs * 8, DH), lambda i: (0, 0, i, 0))],
        out_specs=[tok, tok, pl.BlockSpec((BL, NH, cs, DH, DH), lambda i: (0, 0, i, 0, 0))],
        out_shape=[hs, hs, _sds((BL, NH, nc, DH, DH), F32)],
        scratch_shapes=[pltpu.VMEM((g, DH, DH), F32)],
        compiler_params=_cp(("arbitrary",)),
    )(u, w, qg, kd, intra, egl)


def _dn_scan_bwd(do, w, qg, kd, intra, egl, vn, s0):
    cs, ns = _scan_tiles()
    g = BL * NH
    nc = T // CHUNK

    def body(do_ref, w_ref, qg_ref, kd_ref, in_ref, egl_ref, vn_ref, s0_ref,
             du_ref, dw_ref, dqg_ref, dkd_ref, din_ref, degl_ref, ds_scr):
        @pl.when(pl.program_id(0) == 0)
        def _():
            ds_scr[...] = jnp.zeros_like(ds_scr)

        for c in reversed(range(cs)):
            rows = pl.ds(c * CHUNK, CHUNK)

            def ld(r, m=DH):
                return r[:, :, rows, :].reshape(g, CHUNK, m)

            def st(r, val, m=DH):
                r[:, :, rows, :] = val.reshape(BL, NH, CHUNK, m)

            s = s0_ref[:, :, c].reshape(g, DH, DH)
            ds = ds_scr[...]
            dov, vnv, kdv, wv, qgv, inv = ld(do_ref), ld(vn_ref), ld(kd_ref), ld(w_ref), ld(qg_ref), ld(in_ref, CHUNK)
            dv = _bmm_tn_raw(inv, dov) + _bmm_raw(kdv, ds)
            st(din_ref, _bmm_nt_raw(dov, vnv), CHUNK)
            st(dqg_ref, _bmm_nt_raw(dov, s))
            st(dkd_ref, _bmm_nt_raw(vnv, ds))
            st(du_ref, dv)
            st(dw_ref, -_bmm_nt_raw(dv, s))
            de = jnp.sum(jnp.sum(ds * s, axis=2, keepdims=True), axis=1, keepdims=True)
            degl_ref[:, :, pl.ds(c * 8, 8), :] = jnp.broadcast_to(de, (g, 8, DH)).reshape(BL, NH, 8, DH)
            e = egl_ref[:, :, pl.ds(c * 8, 1), :].reshape(g, 1, DH)
            ds_scr[...] = ds * e + _bmm_tn_raw(qgv, dov) - _bmm_tn_raw(wv, dv)

    rev = lambda i: (0, 0, ns - 1 - i, 0)
    tok = pl.BlockSpec((BL, NH, cs * CHUNK, DH), rev)
    sq = pl.BlockSpec((BL, NH, cs * CHUNK, CHUNK), rev)
    per = pl.BlockSpec((BL, NH, cs * 8, DH), rev)
    hs = _sds((BL, NH, T, DH), F32)
    return _pcall(
        body, name="dn_scan_bwd", grid=(ns,),
        in_specs=[tok, tok, tok, tok, sq, per, tok,
                  pl.BlockSpec((BL, NH, cs, DH, DH), lambda i: (0, 0, ns - 1 - i, 0, 0))],
        out_specs=[tok, tok, tok, tok, sq, per],
        out_shape=[hs, hs, hs, hs, _sds((BL, NH, T, CHUNK), F32), _sds((BL, NH, nc * 8, DH), F32)],
        scratch_shapes=[pltpu.VMEM((g, DH, DH), F32)],
        compiler_params=_cp(("arbitrary",)),
    )(do, w, qg, kd, intra, egl, vn, s0)


def _gated_norm(o_h, z_h, og):
    r = lax.rsqrt(jnp.mean(o_h * o_h, axis=-1, keepdims=True) + EPS)
    return (o_h * r * og) * (z_h * jax.nn.sigmoid(z_h))


def _dn_gnorm_fwd(o, proj, o_g):
    tm = min(256, T)
    nt = T // tm

    def body(o_ref, z_ref, g_ref, og_ref):
        z = z_ref[...]
        for h in range(NH):
            og_ref[:, h * DH:(h + 1) * DH] = _gated_norm(o_ref[0, h], z[:, h * DH:(h + 1) * DH], g_ref[...]).astype(BF16)

    return _pcall(
        body, name="dn_gnorm_fwd", grid=(BL, nt),
        in_specs=[pl.BlockSpec((1, NH, tm, DH), lambda b, i: (b, 0, i, 0)),
                  pl.BlockSpec((tm, D), lambda b, i: (b * nt + i, 3)),
                  pl.BlockSpec((1, DH), lambda b, i: (0, 0))],
        out_specs=pl.BlockSpec((tm, D), lambda b, i: (b * nt + i, 0)),
        out_shape=_sds((BL * T, D), BF16),
        compiler_params=_cp(("arbitrary", "arbitrary")),
    )(o, proj, o_g)


def _dn_gnorm_bwd(dog, o, proj, o_g):
    tm = min(256, T)
    nt = T // tm

    def body(dog_ref, o_ref, z_ref, g_ref, do_ref, dz_ref, dg_ref):
        z = z_ref[...]
        dog = dog_ref[...]
        dg = jnp.zeros((1, DH), F32)
        for h in range(NH):
            cols = slice(h * DH, (h + 1) * DH)
            _, vjp = jax.vjp(_gated_norm, o_ref[0, h], z[:, cols], g_ref[...])
            do_h, dz_h, dg_h = vjp(dog[:, cols])
            do_ref[0, h] = do_h
            dz_ref[:, cols] = dz_h.astype(BF16)
            dg += dg_h
        _acc_rows(dg_ref, jnp.logical_and(pl.program_id(0) == 0, pl.program_id(1) == 0), [dg])

    return _pcall(
        body, name="dn_gnorm_bwd", grid=(BL, nt),
        in_specs=[pl.BlockSpec((tm, D), lambda b, i: (b * nt + i, 0)),
                  pl.BlockSpec((1, NH, tm, DH), lambda b, i: (b, 0, i, 0)),
                  pl.BlockSpec((tm, D), lambda b, i: (b * nt + i, 3)),
                  pl.BlockSpec((1, DH), lambda b, i: (0, 0))],
        out_specs=[pl.BlockSpec((1, NH, tm, DH), lambda b, i: (b, 0, i, 0)),
                   pl.BlockSpec((tm, D), lambda b, i: (b * nt + i, 0)),
                   pl.BlockSpec((1, DH), lambda b, i: (0, 0))],
        out_shape=[_sds((BL, NH, T, DH), F32), _sds((BL * T, D), BF16), _sds((1, DH), F32)],
        compiler_params=_cp(("arbitrary", "arbitrary")),
    )(dog, o, proj, o_g)


def _dn_prep_bwd(dq, dk, dv, dgb, dbb, conv, proj, alog, dtb):
    n = conv.shape[0]
    tt, nt = _conv_tiles()
    w3 = 3 * D

    def body(dq_ref, dk_ref, dv_ref, dgb_ref, dbb_ref, conv_ref, ab_ref, alog_ref, dtb_ref, dconv_ref, dab_ref, dhead_ref):
        cv = conv_ref[...]
        sg = jax.nn.sigmoid(cv)
        dact = sg * (1.0 + cv * (1.0 - sg))
        lane = lax.broadcasted_iota(jnp.int32, (tt, 128), 1)
        cg = jnp.zeros((tt, 128), F32)
        cb = jnp.zeros((tt, 128), F32)
        for h in range(NH):
            cols = slice(h * DH, (h + 1) * DH)
            dconv_ref[:, h * DH:(h + 1) * DH] = dq_ref[0, h] * dact[:, cols]
            dconv_ref[:, D + h * DH:D + (h + 1) * DH] = dk_ref[0, h] * dact[:, D + h * DH:D + (h + 1) * DH]
            dconv_ref[:, 2 * D + h * DH:2 * D + (h + 1) * DH] = dv_ref[0, h] * dact[:, 2 * D + h * DH:2 * D + (h + 1) * DH]
            cg = jnp.where(lane == h, jnp.sum(dgb_ref[0, h], axis=-1, keepdims=True), cg)
            cb = jnp.where(lane == NH + h, jnp.sum(dbb_ref[0, h], axis=-1, keepdims=True), cb)
        _, vjp = jax.vjp(_g_beta, ab_ref[...], alog_ref[...], dtb_ref[...])
        dab, dalog, ddtb = vjp((cg, cb))
        dab_ref[...] = dab.astype(BF16)
        _acc_rows(dhead_ref, jnp.logical_and(pl.program_id(0) == 0, pl.program_id(1) == 0), [dalog, ddtb])

    row = lambda b, i: (b * nt + i, 0)
    head = pl.BlockSpec((1, NH, tt, DH), lambda b, i: (b, 0, i, 0))
    vec = pl.BlockSpec((1, 128), lambda b, i: (0, 0))
    return _pcall(
        body, name="dn_prep_bwd", grid=(BL, nt),
        in_specs=[head] * 5 + [pl.BlockSpec((tt, w3), row),
                               pl.BlockSpec((tt, 128), lambda b, i: (b * nt + i, 4 * D // 128)), vec, vec],
        out_specs=[pl.BlockSpec((tt, w3), row), pl.BlockSpec((tt, 128), row), pl.BlockSpec((2, 128), lambda b, i: (0, 0))],
        out_shape=[_sds((n, w3), F32), _sds((n, 128), BF16), _sds((2, 128), F32)],
        compiler_params=_cp(("arbitrary", "arbitrary")),
    )(dq, dk, dv, dgb, dbb, conv, proj, alog, dtb)


def _dn_sconv_bwd(dconv, proj, w_sc):
    n = dconv.shape[0]
    tt, nt = _conv_tiles()
    w3 = 3 * D

    def body(dc_ref, nxt_ref, qkv_ref, halo_ref, w_ref, dpre_ref, dw_ref, dwin, pwin):
        b, i = pl.program_id(0), pl.program_id(1)
        dc = dc_ref[...]
        dwin[0:tt, :] = dc
        dwin[tt:tt + SHALO, :] = jnp.where(i == nt - 1, 0.0, nxt_ref[...])
        pwin[0:SHALO, :] = jnp.where(i == 0, 0.0, halo_ref[...])
        pwin[SHALO:SHALO + tt, :] = qkv_ref[...]
        dpre = jnp.zeros((tt, w3), F32)
        dws = []
        for k in range(SCW):
            dpre += w_ref[k:k + 1, :] * dwin[pl.ds(SCW - 1 - k, tt), :]
            dws.append(jnp.sum(dc * pwin[pl.ds(SHALO - (SCW - 1) + k, tt), :], axis=0, keepdims=True))
        dws += [jnp.zeros((1, w3), F32)] * (SHALO - SCW)
        dpre_ref[...] = dpre.astype(BF16)
        _acc_rows(dw_ref, jnp.logical_and(b == 0, i == 0), dws)

    row = lambda b, i: (b * nt + i, 0)
    return _pcall(
        body, name="dn_sconv_bwd", grid=(BL, nt),
        in_specs=[pl.BlockSpec((tt, w3), row), _next_halo_spec(w3, tt, SHALO),
                  pl.BlockSpec((tt, w3), row), _prev_halo_spec(w3, tt, SHALO),
                  pl.BlockSpec((SHALO, w3), lambda b, i: (0, 0))],
        out_specs=[pl.BlockSpec((tt, w3), row), pl.BlockSpec((SHALO, w3), lambda b, i: (0, 0))],
        out_shape=[_sds((n, w3), BF16), _sds((SHALO, w3), F32)],
        scratch_shapes=[pltpu.VMEM((tt + SHALO, w3), F32), pltpu.VMEM((SHALO + tt, w3), F32)],
        compiler_params=_cp(("arbitrary", "arbitrary")),
    )(dconv, dconv, proj, proj, w_sc)


def _ada_fwd(c_all, w_ada, b_cols):
    nl, _, m = w_ada.shape
    nb = c_all.shape[0]

    def body(c_ref, w_ref, b_ref, o_ref):
        cv = c_ref[...]
        cs = (cv * jax.nn.sigmoid(cv)).astype(BF16)
        o_ref[...] = _dot(cs, w_ref[...].astype(BF16)) + b_ref[...]

    return _pcall(
        body, name="ada_fwd", grid=(nl,),
        in_specs=[pl.BlockSpec((nb, D), lambda l: (0, 0)), pl.BlockSpec((None, D, m), lambda l: (l, 0, 0)),
                  pl.BlockSpec((None, 1, m), lambda l: (l, 0, 0))],
        out_specs=pl.BlockSpec((None, nb, m), lambda l: (l, 0, 0)),
        out_shape=_sds((nl, nb, m), F32),
        compiler_params=_cp(("arbitrary",)),
    )(c_all, w_ada, b_cols)


def _ada_bwd(c_all, dmod_cols):
    nl, nb, m = dmod_cols.shape

    def body(c_ref, d_ref, o_ref):
        cv = c_ref[...]
        cs = (cv * jax.nn.sigmoid(cv)).astype(BF16)
        o_ref[0] = _dot_tn(cs, d_ref[...].astype(BF16))

    return _pcall(
        body, name="ada_bwd", grid=(nl,),
        in_specs=[pl.BlockSpec((nb, D), lambda l: (0, 0)), pl.BlockSpec((None, nb, m), lambda l: (l, 0, 0))],
        out_specs=pl.BlockSpec((1, D, m), lambda l: (0, l, 0)),
        out_shape=_sds((1, nl * D, m), F32),
        compiler_params=_cp(("arbitrary",)),
    )(c_all, dmod_cols)


def _loss_head(x, tgt, fg):
    n = x.shape[0]
    tm = min(512, T)

    def f(xv, g, t):
        r = lax.rsqrt(jnp.mean(xv * xv, axis=-1, keepdims=True) + EPS)
        e = xv * r * g - t
        return 0.5 * jnp.sum(e * e, axis=0, keepdims=True) * (1.0 / D)

    def body(x_ref, t_ref, g_ref, dx_ref, st_ref):
        t = t_ref[...]
        lrow, vjp = jax.vjp(lambda xv, g: f(xv, g, t), x_ref[...], g_ref[...])
        dx, dg = vjp(jnp.ones_like(lrow))
        dx_ref[...] = dx
        _acc_rows(st_ref, pl.program_id(0) == 0, [dg, lrow])

    return _pcall(
        body, name="loss_head", grid=(n // tm,),
        in_specs=[pl.BlockSpec((tm, D), lambda i: (i, 0)), pl.BlockSpec((tm, D), lambda i: (i, 0)),
                  pl.BlockSpec((1, D), lambda i: (0, 0))],
        out_specs=[pl.BlockSpec((tm, D), lambda i: (i, 0)), pl.BlockSpec((2, D), lambda i: (0, 0))],
        out_shape=[_sds((n, D), F32), _sds((2, D), F32)],
        compiler_params=_cp(("arbitrary",)),
    )(x, tgt, fg)


def _adamw(parts, w, m, v):
    p, r, c = parts.shape
    tr = r
    for cand in (256, 128, 64, 32, 16, 8):
        if r % cand == 0:
            tr = cand
            break
    k1 = 1.0 - B1 ** STEP
    k2 = 1.0 - B2 ** STEP

    def body(p_ref, w_ref, m_ref, v_ref, g_ref, d_ref, nm_ref, nv_ref):
        g = p_ref[0].astype(F32)
        for q in range(1, p):
            g += p_ref[q].astype(F32)
        mn = B1 * m_ref[...] + (1.0 - B1) * g
        vn = B2 * v_ref[...] + (1.0 - B2) * (g * g)
        g_ref[...] = g
        nm_ref[...] = mn
        nv_ref[...] = vn
        d_ref[...] = -LR * ((mn / k1) / (jnp.sqrt(vn / k2) + AEPS) + WD * w_ref[...])

    blk = pl.BlockSpec((tr, c), lambda i: (i, 0))
    return _pcall(
        body, name="adamw", grid=(r // tr,),
        in_specs=[pl.BlockSpec((p, tr, c), lambda i: (0, i, 0)), blk, blk, blk],
        out_specs=[blk] * 4, out_shape=[_sds((r, c), F32)] * 4,
        compiler_params=_cp(("arbitrary",)),
    )(parts, w, m, v)


def _adamw_slot(parts, w, m, v, outs, row0, col):
    p, r, c = parts.shape
    tr = r
    for cand in (256, 128, 64, 32, 16, 8):
        if r % cand == 0:
            tr = cand
            break
    if r % 352 == 0:
        tr = 352
    nt = r // tr
    k1 = 1.0 - B1 ** STEP
    k2 = 1.0 - B2 ** STEP

    def body(p_ref, w_ref, m_ref, v_ref, g0, d0, m0, v0, g_ref, d_ref, nm_ref, nv_ref):
        g = p_ref[0].astype(F32)
        for q in range(1, p):
            g += p_ref[q].astype(F32)
        mn = B1 * m_ref[...] + (1.0 - B1) * g
        vn = B2 * v_ref[...] + (1.0 - B2) * (g * g)
        g_ref[...] = g
        nm_ref[...] = mn
        nv_ref[...] = vn
        d_ref[...] = -LR * ((mn / k1) / (jnp.sqrt(vn / k2) + AEPS) + WD * w_ref[...])

    blk = pl.BlockSpec((tr, c), lambda i: (row0 * nt + i, col))
    anyspec = pl.BlockSpec(memory_space=pl.ANY)
    return _pcall(
        body, name="adamw_slot", grid=(nt,),
        in_specs=[pl.BlockSpec((p, tr, c), lambda i: (0, i, 0)), blk, blk, blk] + [anyspec] * 4,
        out_specs=[blk] * 4, out_shape=[_sds(w.shape, F32)] * 4,
        input_output_aliases={4: 0, 5: 1, 6: 2, 7: 3},
        compiler_params=_cp(("arbitrary",)),
    )(parts, w, m, v, *outs)


def _pack(arrs):
    flat = jnp.concatenate([a.reshape(-1) for a in arrs])
    pad = (-flat.shape[0]) % 1024
    return jnp.pad(flat, (0, pad)).reshape(-1, 128)


def _unpack(buf, shapes):
    flat = buf.reshape(-1)
    out, off = [], 0
    for s in shapes:
        size = 1
        for d in s:
            size *= d
        out.append(flat[off:off + size].reshape(s))
        off += size
    return out


def kernel(x, c, norm_g, w_ada, b_ada, w_ffn_in, w_ffn_out, cm_w_glu, cm_b_glu, cm_w_dw, cm_b_dw, cm_ln_g, cm_ln_b, cm_w_pw, cm_b_pw, dn_w_in, dn_w_sconv, dn_a_log, dn_dt_bias, dn_o_g, dn_w_out, final_g, loss_target, m_norm_g, m_w_ada, m_b_ada, m_w_ffn_in, m_w_ffn_out, m_cm_w_glu, m_cm_b_glu, m_cm_w_dw, m_cm_b_dw, m_cm_ln_g, m_cm_ln_b, m_cm_w_pw, m_cm_b_pw, m_dn_w_in, m_dn_w_sconv, m_dn_a_log, m_dn_dt_bias, m_dn_o_g, m_dn_w_out, m_final_g, v_norm_g, v_w_ada, v_b_ada, v_w_ffn_in, v_w_ffn_out, v_cm_w_glu, v_cm_b_glu, v_cm_w_dw, v_cm_b_dw, v_cm_ln_g, v_cm_ln_b, v_cm_w_pw, v_cm_b_pw, v_dn_w_in, v_dn_w_sconv, v_dn_a_log, v_dn_dt_bias, v_dn_o_g, v_dn_w_out, v_final_g):
    me = 4 * lax.axis_index("x") + 2 * lax.axis_index("y") + lax.axis_index("c")
    n = BL * T
    nf = 4
    tf = FF // nf
    na, nb = cm_w_glu.shape[0], dn_w_in.shape[0]
    mcols = w_ada.shape[2]
    dsh = D // NDEV

    tr_ffn = lambda a: jnp.swapaxes(a, 2, 3)
    tr_dn = lambda a: jnp.transpose(a, (2, 0, 1))
    wt_ffn_in, wt_dn_in = tr_ffn(w_ffn_in), tr_dn(dn_w_in)

    def unit_weights(l, part):
        if part == 0:
            ws = (wt_ffn_in[l, 0], w_ffn_out[l, 0])
        else:
            mix = (cm_w_glu[l // 2], cm_w_pw[l // 2]) if l % 2 == 0 else (wt_dn_in[:, l // 2], dn_w_out[l // 2])
            ws = (wt_ffn_in[l, 1], w_ffn_out[l, 1], *mix)
        return [w.astype(BF16) for w in ws]

    gathers, all_started = {}, jnp.zeros((8, 128), F32)
    for l in range(DEPTH):
        for part in range(2):
            gathers[l, part], tok = _exchange_start(unit_weights(l, part), True, f"gather_start_{l}_{part}")
            all_started = all_started + tok

    small_shapes = [c.shape, norm_g.shape, cm_w_dw.shape, dn_w_sconv.shape]
    small_g, = _all_gather([_pack([c, norm_g, cm_w_dw, dn_w_sconv])], "gather_small")
    sm = [_unpack(small_g[d], small_shapes) for d in range(NDEV)]
    c_all = jnp.concatenate([s[0] for s in sm], axis=0)
    norm_g_f = jnp.concatenate([s[1] for s in sm], axis=-1)
    w_dw_f = jnp.concatenate([s[2] for s in sm], axis=-1)
    w_sc_f = jnp.concatenate([s[3] for s in sm], axis=-1)

    b_cols = lax.dynamic_slice_in_dim(b_ada, me * mcols, mcols, axis=1)[:, None, :]
    mod_cols = _ada_fwd(c_all, w_ada, b_cols)
    mod_g, = _all_gather([mod_cols], "gather_mod")
    mod_all = jnp.transpose(mod_g, (1, 2, 0, 3)).reshape(DEPTH, NDEV * BL, 9 * D)
    mod = lax.dynamic_slice_in_dim(mod_all, me * BL, BL, axis=1).reshape(DEPTH, BL, 3, 3, D)

    gathered = [None] * DEPTH

    def ffn_weights(l, s):
        return gathered[l][s].reshape(2, nf, tf, D), gathered[l][2 + s].reshape(nf, tf, D)

    xs = x.reshape(n, D)
    saved = []
    for l in range(DEPTH):
        rec = {}
        ga = _exchange_wait(gathers[l, 0], all_started if l == 0 else xs, f"gather_wait_{l}_0")
        gathered[l] = [ga[0], None, ga[1], None, None, None]
        for s, j in ((0, 0), (1, 2)):
            if j == 2:
                gb = _exchange_wait(gathers[l, 1], xs, f"gather_wait_{l}_1")
                gathered[l] = [ga[0], gb[0], ga[1], gb[1], gb[2], gb[3]]
            w_in, w_out = ffn_weights(l, s)
            ssg, ng = mod[l, :, j], norm_g_f[l, j][None]
            if j == 2:
                ssg1, ng1 = mod[l, :, 1], norm_g_f[l, 1][None]
                if l % 2 == 0:
                    a = l // 2
                    w_glu = gathered[l][4]
                    w_pw = gathered[l][5].reshape(D, D)
                    w_dw = jnp.pad(w_dw_f[a], ((0, HALO - CW), (0, 0)))
                    h1, ab = _premod_matmul(xs, ssg1, ng1, w_glu, cm_b_glu[a][None], w_glu.shape[2])
                    u1, u2 = _cm_mid_fwd(ab, w_dw, cm_b_dw[a][None], cm_ln_g[a][None], cm_ln_b[a][None])
                    xn, ymix = _matmul_res(xs, u2, ssg1, w_pw, cm_b_pw[a][None])
                    rec["mix"] = dict(x=xs, h=h1, ab=ab, u1=u1, u2=u2, y=ymix, w_glu=w_glu, w_pw=w_pw, w_dw=w_dw)
                else:
                    mi = l // 2
                    w_proj = jnp.pad(gathered[l][4].reshape(4 * D + 2 * NH, D), ((0, 128 - 2 * NH), (0, 0)))
                    w_o = gathered[l][5].reshape(D, D)
                    w_sc = jnp.pad(w_sc_f[mi], ((0, SHALO - SCW), (0, 0)))
                    alog = jnp.pad(dn_a_log[mi], (0, 128 - NH))[None]
                    dtb = jnp.pad(dn_dt_bias[mi], (0, 128 - NH))[None]
                    h1, proj = _premod_matmul(xs, ssg1, ng1, w_proj, jnp.zeros((1, w_proj.shape[0]), F32),
                                              (4 * D + 128) // 3 if (4 * D + 128) % 384 == 0 else 128)
                    conv, q, k, v, gb, bb = _dn_sconv_fwd(proj, w_sc, alog, dtb)
                    u, w, qg, kd, intra, egl = _dn_pre_fwd(q, k, v, gb, bb)
                    o, vn, s0 = _dn_scan_fwd(u, w, qg, kd, intra, egl)
                    og = _dn_gnorm_fwd(o, proj, dn_o_g[mi][None])
                    xn, ymix = _matmul_res(xs, og, ssg1, w_o, jnp.zeros((1, D), F32))
                    rec["mix"] = dict(x=xs, h=h1, proj=proj, conv=conv, q=q, k=k, v=v, gb=gb, bb=bb, w=w, qg=qg, kd=kd,
                                      intra=intra, egl=egl, o=o, vn=vn, s0=s0, og=og, y=ymix, w_proj=w_proj, w_o=w_o,
                                      w_sc=w_sc, alog=alog, dtb=dtb)
                xs = xn
            xn, gu, hid, y = _ffn_fwd(xs, ssg, ng, w_in, w_out)
            rec[s] = dict(x=xs, gu=gu, hid=hid, y=y)
            xs = xn
        saved.append(rec)

    dx, stats = _loss_head(xs, loss_target.reshape(n, D), final_g[None])
    loss = lax.psum(jnp.sum(stats[1]), AXES)
    d_final_g = stats[0]

    d_mod = [[None] * 3 for _ in range(DEPTH)]
    d_norm = [[None] * 3 for _ in range(DEPTH)]
    dw_ffn_in = [[None] * 2 for _ in range(DEPTH)]
    dw_ffn_out = [[None] * 2 for _ in range(DEPTH)]
    dcm = [dict() for _ in range(na)]
    ddn = [dict() for _ in range(nb)]
    exchanges = {}
    token = jnp.zeros((), F32)
    for l in reversed(range(DEPTH)):
        rec = saved[l]
        for s, j in ((1, 2), (0, 0)):
            w_in, w_out = ffn_weights(l, s)
            ssg, ng = mod[l, :, j] + token, norm_g_f[l, j][None]
            r = rec[s]
            dx, dgu, hb, dout, dssg, dng = _ffn_bwd_a(r["x"], dx, ssg, ng, r["y"], r["gu"], w_in, w_out)
            dw_ffn_in[l][s], dw_ffn_out[l][s] = _ffn_bwd_w(hb, dgu, r["hid"], dout)
            d_mod[l][j], d_norm[l][j] = dssg, dng[0]
            if j == 2:
                ssg1, ng1 = mod[l, :, 1], norm_g_f[l, 1][None]
                r = rec["mix"]
                if l % 2 == 0:
                    a = l // 2
                    du2, dy, dgate, db_pw = _matmul_res_bwd(dx, r["y"], ssg1, r["w_pw"])
                    du1, dln = _cm_mid_bwd_a(du2, r["u1"], cm_ln_g[a][None], cm_ln_b[a][None])
                    dab, dw_dw, db_dw, db_glu = _cm_mid_bwd_b(du1, r["ab"], r["w_dw"])
                    dx, dssg, dng = _premod_matmul_bwd(r["x"], dx, ssg1, ng1, [dab], r["w_glu"])
                    dcm[a] = dict(w_glu=_wgrad_shards(r["h"], dab, NDEV), w_pw=_wgrad(r["u2"], dy).reshape(NDEV, dsh, D),
                                  b_glu=db_glu[0], w_dw=dw_dw[:CW], b_dw=db_dw[0], ln_g=dln[0], ln_b=dln[1], b_pw=db_pw[0])
                else:
                    mi = l // 2
                    dog, dy, dgate, _ = _matmul_res_bwd(dx, r["y"], ssg1, r["w_o"])
                    do, dz, d_og = _dn_gnorm_bwd(dog, r["o"], r["proj"], dn_o_g[mi][None])
                    du, dw, dqg, dkd, dintra, degl = _dn_scan_bwd(do, r["w"], r["qg"], r["kd"], r["intra"], r["egl"],
                                                                   r["vn"], r["s0"])
                    dq, dk, dv, dgb, dbb = _dn_pre_bwd(r["q"], r["k"], r["v"], r["gb"], r["bb"], du, dw, dqg, dkd, dintra, degl)
                    dconv, dab16, dhead = _dn_prep_bwd(dq, dk, dv, dgb, dbb, r["conv"], r["proj"], r["alog"], r["dtb"])
                    dpre, dw_sc = _dn_sconv_bwd(dconv, r["proj"], r["w_sc"])
                    dx, dssg, dng = _premod_matmul_bwd(r["x"], dx, ssg1, ng1, [dpre, dz, dab16], r["w_proj"])
                    dw_in = jnp.concatenate([_wgrad(dpre, r["h"]), _wgrad(dz, r["h"]),
                                             _wgrad(dab16, r["h"])[:2 * NH]], axis=0)
                    ddn[mi] = dict(w_in=dw_in.reshape(NDEV, -1, D), w_out=_wgrad(r["og"], dy).reshape(NDEV, dsh, D),
                                   w_sconv=dw_sc[:SCW], a_log=dhead[0, :NH], dt_bias=dhead[1, :NH], o_g=d_og[0])
                d_mod[l][1] = dssg.at[:, 2].set(dgate[:, 0])
                d_norm[l][1] = dng[0]
            unit = [dw_ffn_in[l][s].reshape(NDEV, tf, D), dw_ffn_out[l][s].reshape(NDEV, FF // NDEV, D)]
            if j == 2:
                g = dcm[l // 2] if l % 2 == 0 else ddn[l // 2]
                unit += [g["w_glu"], g["w_pw"]] if l % 2 == 0 else [g["w_in"], g["w_out"]]
            exchanges[l, s], token = _exchange_start(unit, False, f"grads_start_{l}_{s}")
    grad_x = dx.reshape(BL, T, D)

    dmod_loc = jnp.stack([jnp.stack(d_mod[l], axis=1) for l in range(DEPTH)]).reshape(DEPTH, BL, 9 * D)
    dmod_g, = _all_gather([dmod_loc], "gather_dmod")
    dmod_all = jnp.transpose(dmod_g, (1, 0, 2, 3)).reshape(DEPTH, NDEV * BL, 9 * D)
    g_w_ada = _ada_bwd(c_all, lax.dynamic_slice_in_dim(dmod_all, me * mcols, mcols, axis=2))

    got = []
    for l in range(DEPTH):
        ea = _exchange_wait(exchanges[l, 0], dx, f"grads_wait_{l}_0")
        eb = _exchange_wait(exchanges[l, 1], dx, f"grads_wait_{l}_1")
        got.append([ea[0], eb[0], ea[1], eb[1], eb[2], eb[3]])

    small = [jnp.sum(dmod_loc, axis=1), jnp.stack([jnp.stack(d_norm[l]) for l in range(DEPTH)]),
             jnp.stack([d["b_glu"] for d in dcm]), jnp.stack([d["w_dw"] for d in dcm]), jnp.stack([d["b_dw"] for d in dcm]),
             jnp.stack([d["ln_g"] for d in dcm]), jnp.stack([d["ln_b"] for d in dcm]), jnp.stack([d["b_pw"] for d in dcm]),
             jnp.stack([d["w_sconv"] for d in ddn]), jnp.stack([d["a_log"] for d in ddn]),
             jnp.stack([d["dt_bias"] for d in ddn]), jnp.stack([d["o_g"] for d in ddn]), d_final_g]
    small_parts, = _all_gather([_pack(small)], "gather_small_grads")
    names = ["b_ada", "norm_g", "cm_b_glu", "cm_w_dw", "cm_b_dw", "cm_ln_g", "cm_ln_b", "cm_b_pw",
             "dn_w_sconv", "dn_a_log", "dn_dt_bias", "dn_o_g", "final_g"]
    cols = lambda a, width: lax.dynamic_slice_in_dim(a, me * width, width, axis=a.ndim - 1)
    local = {"norm_g": lambda a: cols(a, dsh), "cm_w_dw": lambda a: cols(a, dsh), "dn_w_sconv": lambda a: cols(a, 3 * dsh)}
    full_shapes = [a.shape for a in small]
    parts = [_unpack(small_parts[d], full_shapes) for d in range(NDEV)]
    parts = [[local.get(nm, lambda a: a)(p) for nm, p in zip(names, ps)] for ps in parts]
    small_w = dict(b_ada=(b_ada, m_b_ada, v_b_ada), norm_g=(norm_g, m_norm_g, v_norm_g),
                   cm_b_glu=(cm_b_glu, m_cm_b_glu, v_cm_b_glu), cm_w_dw=(cm_w_dw, m_cm_w_dw, v_cm_w_dw),
                   cm_b_dw=(cm_b_dw, m_cm_b_dw, v_cm_b_dw), cm_ln_g=(cm_ln_g, m_cm_ln_g, v_cm_ln_g),
                   cm_ln_b=(cm_ln_b, m_cm_ln_b, v_cm_ln_b), cm_b_pw=(cm_b_pw, m_cm_b_pw, v_cm_b_pw),
                   dn_w_sconv=(dn_w_sconv, m_dn_w_sconv, v_dn_w_sconv), dn_a_log=(dn_a_log, m_dn_a_log, v_dn_a_log),
                   dn_dt_bias=(dn_dt_bias, m_dn_dt_bias, v_dn_dt_bias), dn_o_g=(dn_o_g, m_dn_o_g, v_dn_o_g),
                   final_g=(final_g, m_final_g, v_final_g))
    loc_shapes = [small_w[nm][0].shape for nm in names]
    packed_parts = jnp.stack([_pack(ps) for ps in parts])
    sres = _adamw(packed_parts, *[_pack([small_w[nm][q] for nm in names]) for q in range(3)])
    sres = [dict(zip(names, _unpack(r, loc_shapes))) for r in sres]

    res = {}

    def update(slots, wmv, view, back):
        w2, m2, v2 = [view(a) for a in wmv]
        outs = [lax.empty(w2.shape, F32) for _ in range(4)]
        for p, row0, col in slots:
            outs = _adamw_slot(p, w2, m2, v2, outs, row0, col)
        return [back(o) for o in outs]

    res["w_ffn_in"] = update([(got[l][s], 2 * l + s, 0) for l in reversed(range(DEPTH)) for s in (1, 0)],
                             (w_ffn_in, m_w_ffn_in, v_w_ffn_in), lambda a: tr_ffn(a).reshape(-1, D),
                             lambda o: jnp.swapaxes(o.reshape(DEPTH, 2, tf, D), 2, 3))
    res["w_ffn_out"] = update([(got[l][2 + s], 2 * l + s, 0) for l in reversed(range(DEPTH)) for s in (1, 0)],
                              (w_ffn_out, m_w_ffn_out, v_w_ffn_out), lambda a: a.reshape(-1, D),
                              lambda o: o.reshape(w_ffn_out.shape))
    cgl = cm_w_glu.shape[2]
    res["cm_w_glu"] = update([(got[2 * a][4], a, 0) for a in range(na)], (cm_w_glu, m_cm_w_glu, v_cm_w_glu),
                             lambda a: a.reshape(-1, cgl), lambda o: o.reshape(cm_w_glu.shape))
    res["cm_w_pw"] = update([(got[2 * a][5], a, 0) for a in range(na)], (cm_w_pw, m_cm_w_pw, v_cm_w_pw),
                            lambda a: a.reshape(-1, D), lambda o: o.reshape(cm_w_pw.shape))
    cdn = dn_w_in.shape[2]
    res["dn_w_in"] = update([(got[2 * i + 1][4], 0, i) for i in range(nb)], (dn_w_in, m_dn_w_in, v_dn_w_in),
                            lambda a: tr_dn(a).reshape(cdn, nb * D),
                            lambda o: jnp.transpose(o.reshape(cdn, nb, D), (1, 2, 0)))
    res["dn_w_out"] = update([(got[2 * i + 1][5], i, 0) for i in range(nb)], (dn_w_out, m_dn_w_out, v_dn_w_out),
                             lambda a: a.reshape(-1, D), lambda o: o.reshape(dn_w_out.shape))
    res["w_ada"] = [o.reshape(w_ada.shape) for o in
                    _adamw(g_w_ada, *[a.reshape(-1, mcols) for a in (w_ada, m_w_ada, v_w_ada)])]
    for nm in names:
        res[nm] = [sres[q][nm] for q in range(4)]

    order = ["norm_g", "w_ada", "b_ada", "w_ffn_in", "w_ffn_out", "cm_w_glu", "cm_b_glu", "cm_w_dw", "cm_b_dw", "cm_ln_g",
             "cm_ln_b", "cm_w_pw", "cm_b_pw", "dn_w_in", "dn_w_sconv", "dn_a_log", "dn_dt_bias", "dn_o_g", "dn_w_out", "final_g"]
    return (loss, grad_x, *[res[nm][0] for nm in order], *[res[nm][1] for nm in order],
            *[res[nm][2] for nm in order], *[res[nm][3] for nm in order])
```

```python
import functools

import jax
import jax.numpy as jnp
from jax import lax
from jax.experimental import pallas as pl
from jax.experimental.pallas import tpu as pltpu

F32 = jnp.float32
BF16 = jnp.bfloat16
HI = lax.Precision.HIGHEST
INV_PREC = lax.Precision.HIGH
MESH = pl.DeviceIdType.MESH
AXES = ("x", "y", "c")

NDEV = 8
D = 1024
T = 2048
BL = 2
FF = 2816
NH = 8
DH = 128
CW = 31
SCW = 4
CHUNK = 64
DEPTH = 4
EPS = 1e-6
LR, B1, B2, AEPS, WD, STEP = 0.001, 0.9, 0.999, 1e-08, 0.01, 10

VMEM_LIMIT_BYTES = 56 * 1024 * 1024
HALO = 32
SHALO = 8


def _pcall(body, **kw):
    return pl.pallas_call(body, **kw)


def _cp(sem=None):
    return pltpu.CompilerParams(dimension_semantics=sem, vmem_limit_bytes=VMEM_LIMIT_BYTES)


def _sds(shape, dtype):
    return jax.ShapeDtypeStruct(tuple(shape), dtype)


def _dot(a, b):
    return jnp.dot(a, b, preferred_element_type=F32)


def _dot_nt(a, b):
    return lax.dot_general(a, b, (((1,), (1,)), ((), ())), preferred_element_type=F32)


def _dot_tn(a, b):
    return lax.dot_general(a, b, (((0,), (0,)), ((), ())), preferred_element_type=F32)


def _modulate(x, ng, scale, shift):
    r = lax.rsqrt(jnp.mean(x * x, axis=-1, keepdims=True) + EPS)
    return (x * r * ng) * (1.0 + scale) + shift


def _acc_rows(ref, first, rows):
    @pl.when(first)
    def _():
        ref[...] = jnp.zeros_like(ref)

    for r, val in enumerate(rows):
        ref[r:r + 1, :] += val


def _my_pos():
    return lax.axis_index("x"), lax.axis_index("y"), lax.axis_index("c")


def _all_gather(arrs, name):
    n = len(arrs)

    def body(*refs):
        ins, outs = refs[:n], refs[n:2 * n]
        send, recv, loc = refs[2 * n:]
        x, y, c = _my_pos()
        me, sibling = (x, y, c), (x, y, 1 - c)
        chips = [(1 - x, y), (x, 1 - y), (1 - x, 1 - y)]

        def copy(a, k, block, to, src=None):
            dst = outs[a].at[4 * block[0] + 2 * block[1] + block[2]]
            return pltpu.make_async_remote_copy(
                src_ref=dst if src is None else src, dst_ref=dst,
                send_sem=send.at[7 * a + k], recv_sem=recv.at[7 * a + k],
                device_id=to, device_id_type=MESH)

        mine, first, passed = [], [], []
        for a in range(n):
            m = pltpu.make_async_copy(ins[a], outs[a].at[4 * x + 2 * y + c], loc.at[a])
            m.start()
            mine.append(m)
            f = [copy(a, 0, me, sibling, src=ins[a])]
            f += [copy(a, 1 + j, me, (*chip, c), src=ins[a]) for j, chip in enumerate(chips)]
            for cp in f:
                cp.start()
            first += f
        for a in range(n):
            for j, chip in enumerate(chips):
                copy(a, 1 + j, (*chip, c), me).wait_recv()
                p = copy(a, 4 + j, (*chip, c), sibling)
                p.start()
                passed.append(p)
        for a in range(n):
            copy(a, 0, sibling, me).wait_recv()
            for j, chip in enumerate(chips):
                copy(a, 4 + j, (*chip, 1 - c), me).wait_recv()
        for cp in first + passed:
            cp.wait_send()
        for m in mine:
            m.wait()

    hbm = pl.BlockSpec(memory_space=pl.ANY)
    return _pcall(
        body, name=name,
        out_shape=[_sds((NDEV,) + a.shape, a.dtype) for a in arrs],
        in_specs=[hbm] * n, out_specs=[hbm] * n,
        scratch_shapes=[pltpu.SemaphoreType.DMA((7 * n,)), pltpu.SemaphoreType.DMA((7 * n,)),
                        pltpu.SemaphoreType.DMA((n,))],
    )(*arrs)


def _peer(k):
    x, y, c = _my_pos()
    return (1 - x if k & 4 else x, 1 - y if k & 2 else y, 1 - c if k & 1 else c)


def _dev_index(p):
    return 4 * p[0] + 2 * p[1] + p[2]


_HBM = pl.BlockSpec(memory_space=pltpu.HBM)
_SEM = pl.BlockSpec(memory_space=pltpu.SEMAPHORE)
_EFFECT = pltpu.SideEffectType.DATAFLOW_SIDE_EFFECTING


def _exchange_start(srcs, gather, name):
    n = len(srcs)
    me = _dev_index(_my_pos())
    lands = []
    for s in srcs:
        own = s if gather else lax.dynamic_index_in_dim(s, me, 0, keepdims=False)
        shape = (NDEV,) + s.shape if gather else s.shape
        lands.append(lax.dynamic_update_index_in_dim(lax.empty(shape, s.dtype), own, me, 0))

    def body(*refs):
        src_refs, land_refs = refs[:n], refs[n:2 * n]
        sends, recvs = refs[2 * n:3 * n], refs[3 * n:4 * n]
        token = refs[-1]
        mine = _dev_index(_my_pos())
        for a in range(n):
            for k in range(1, 8):
                p = _peer(k)
                pltpu.make_async_remote_copy(
                    src_ref=src_refs[a] if gather else src_refs[a].at[_dev_index(p)],
                    dst_ref=land_refs[a].at[mine], send_sem=sends[a], recv_sem=recvs[a],
                    device_id=p, device_id_type=MESH).start()
        token[...] = jnp.zeros_like(token)

    out = pl.pallas_call(
        body, name=name,
        out_shape=(*[pltpu.SemaphoreType.DMA(())] * (2 * n),
                   *[pltpu.HBM(a.shape, a.dtype) for a in srcs], *[pltpu.HBM(a.shape, a.dtype) for a in lands],
                   _sds((8, 128), F32)),
        in_specs=[_HBM] * (2 * n),
        out_specs=(*[_SEM] * (2 * n), *[_HBM] * (2 * n), pl.BlockSpec(memory_space=pltpu.VMEM)),
        input_output_aliases={i: 2 * n + i for i in range(2 * n)},
        compiler_params=pltpu.CompilerParams(has_side_effects=_EFFECT),
    )(*[pltpu.with_memory_space_constraint(a, pltpu.HBM) for a in srcs],
      *[pltpu.with_memory_space_constraint(a, pltpu.HBM) for a in lands])
    state = (out[:n], out[n:2 * n], out[2 * n:3 * n], out[3 * n:4 * n])
    return state, out[-1][0, 0]


def _exchange_wait(state, after, name):
    sends, recvs, srcs, lands = state
    n = len(srcs)
    after = list(after) if isinstance(after, (list, tuple)) else [after]

    def body(*refs):
        land_refs = refs[n:2 * n]
        send_refs, recv_refs = refs[2 * n:3 * n], refs[3 * n:4 * n]
        for a in range(n):
            seven = land_refs[a].at[pl.ds(0, NDEV - 1)]
            cp = pltpu.make_async_remote_copy(src_ref=seven, dst_ref=seven, send_sem=send_refs[a], recv_sem=recv_refs[a],
                                              device_id=_peer(1), device_id_type=MESH)
            cp.wait_send()
            cp.wait_recv()

    out = pl.pallas_call(
        body, name=name,
        out_shape=(*[pltpu.HBM(a.shape, a.dtype) for a in srcs], *[pltpu.HBM(a.shape, a.dtype) for a in lands]),
        in_specs=(*[_HBM] * (2 * n), *[_SEM] * (2 * n), *[pl.BlockSpec(memory_space=pl.ANY)] * len(after)),
        out_specs=tuple([_HBM] * (2 * n)),
        input_output_aliases={i: i for i in range(2 * n)},
        compiler_params=pltpu.CompilerParams(has_side_effects=_EFFECT),
    )(*srcs, *lands, *sends, *recvs, *after)
    return list(out[n:])


def _ffn_tiles():
    tm = min(512, T)
    return tm, T // tm


def _ffn_fwd(x, ssg, ng, w_in, w_out):
    n = x.shape[0]
    _, nf, tf, _ = w_in.shape
    tm, tpb = _ffn_tiles()

    def body(x_ref, ssg_ref, ng_ref, win_ref, wout_ref, xn_ref, gu_ref, hid_ref, y_ref, h_scr, acc):
        j = pl.program_id(1)

        @pl.when(j == 0)
        def _():
            s = ssg_ref[0]
            h_scr[...] = _modulate(x_ref[...], ng_ref[...], s[1:2], s[0:1]).astype(BF16)
            acc[...] = jnp.zeros_like(acc)

        h = h_scr[...]
        g = _dot_nt(h, win_ref[0])
        u = _dot_nt(h, win_ref[1])
        gu_ref[0] = g.astype(BF16)
        gu_ref[1] = u.astype(BF16)
        hid = (g * jax.nn.sigmoid(g) * u).astype(BF16)
        hid_ref[...] = hid
        acc[...] += _dot(hid, wout_ref[...])

        @pl.when(j == nf - 1)
        def _():
            yv = acc[...]
            y_ref[...] = yv.astype(BF16)
            xn_ref[...] = x_ref[...] + (0.5 * (1.0 + ssg_ref[0][2:3])) * yv

    return _pcall(
        body, name="ffn_fwd", grid=(n // tm, nf),
        in_specs=[pl.BlockSpec((tm, D), lambda i, j: (i, 0)),
                  pl.BlockSpec((1, 3, D), lambda i, j: (i // tpb, 0, 0)),
                  pl.BlockSpec((1, D), lambda i, j: (0, 0)),
                  pl.BlockSpec((2, None, tf, D), lambda i, j: (0, j, 0, 0)),
                  pl.BlockSpec((None, tf, D), lambda i, j: (j, 0, 0))],
        out_specs=[pl.BlockSpec((tm, D), lambda i, j: (i, 0)),
                   pl.BlockSpec((2, None, tm, tf), lambda i, j: (0, j, i, 0)),
                   pl.BlockSpec((None, tm, tf), lambda i, j: (j, i, 0)),
                   pl.BlockSpec((tm, D), lambda i, j: (i, 0))],
        out_shape=[_sds((n, D), F32), _sds((2, nf, n, tf), BF16), _sds((nf, n, tf), BF16), _sds((n, D), BF16)],
        scratch_shapes=[pltpu.VMEM((tm, D), BF16), pltpu.VMEM((tm, D), F32)],
        compiler_params=_cp(("arbitrary", "arbitrary")),
    )(x, ssg, ng, w_in, w_out)


def _ffn_bwd_a(x, dxn, ssg, ng, y, gu, w_in, w_out):
    n = x.shape[0]
    _, nf, tf, _ = w_in.shape
    tm, tpb = _ffn_tiles()

    def body(x_ref, dxn_ref, ssg_ref, ng_ref, y_ref, gu_ref, win_ref, wout_ref,
             dx_ref, dgu_ref, h_ref, dout_ref, dssg_ref, dng_ref, dout_scr, dh_acc):
        i, j = pl.program_id(0), pl.program_id(1)

        @pl.when(j == 0)
        def _():
            db = ((0.5 * (1.0 + ssg_ref[0][2:3])) * dxn_ref[...]).astype(BF16)
            dout_scr[...] = db
            dout_ref[...] = db
            dh_acc[...] = jnp.zeros_like(dh_acc)

        dhid = _dot_nt(dout_scr[...], wout_ref[...]).astype(BF16)
        g = gu_ref[0]
        u = gu_ref[1]
        sig = jax.nn.sigmoid(g)
        dg = dhid * u * (sig * (1.0 + g * (1.0 - sig)))
        du = dhid * (g * sig)
        dgu_ref[0] = dg
        dgu_ref[1] = du
        dh_acc[...] += _dot(dg, win_ref[0]) + _dot(du, win_ref[1])

        @pl.when(j == nf - 1)
        def _():
            s = ssg_ref[0]
            h, vjp = jax.vjp(_modulate, x_ref[...], ng_ref[...], s[1:2], s[0:1])
            dx_, dng_, dsc_, dsh_ = vjp(dh_acc[...])
            h_ref[...] = h.astype(BF16)
            dxn = dxn_ref[...]
            dx_ref[...] = dxn + dx_
            dgate = jnp.sum(0.5 * dxn * y_ref[...].astype(F32), axis=0, keepdims=True)
            _acc_rows(dssg_ref.at[0], i % tpb == 0, [dsh_, dsc_, dgate])
            _acc_rows(dng_ref, i == 0, [dng_])

    return _pcall(
        body, name="ffn_bwd_a", grid=(n // tm, nf),
        in_specs=[pl.BlockSpec((tm, D), lambda i, j: (i, 0)),
                  pl.BlockSpec((tm, D), lambda i, j: (i, 0)),
                  pl.BlockSpec((1, 3, D), lambda i, j: (i // tpb, 0, 0)),
                  pl.BlockSpec((1, D), lambda i, j: (0, 0)),
                  pl.BlockSpec((tm, D), lambda i, j: (i, 0)),
                  pl.BlockSpec((2, None, tm, tf), lambda i, j: (0, j, i, 0)),
                  pl.BlockSpec((2, None, tf, D), lambda i, j: (0, j, 0, 0)),
                  pl.BlockSpec((None, tf, D), lambda i, j: (j, 0, 0))],
        out_specs=[pl.BlockSpec((tm, D), lambda i, j: (i, 0)),
                   pl.BlockSpec((2, None, tm, tf), lambda i, j: (0, j, i, 0)),
                   pl.BlockSpec((tm, D), lambda i, j: (i, 0)),
                   pl.BlockSpec((tm, D), lambda i, j: (i, 0)),
                   pl.BlockSpec((1, 3, D), lambda i, j: (i // tpb, 0, 0)),
                   pl.BlockSpec((1, D), lambda i, j: (0, 0))],
        out_shape=[_sds((n, D), F32), _sds((2, nf, n, tf), BF16), _sds((n, D), BF16), _sds((n, D), BF16),
                   _sds((BL, 3, D), F32), _sds((1, D), F32)],
        scratch_shapes=[pltpu.VMEM((tm, D), BF16), pltpu.VMEM((tm, D), F32)],
        compiler_params=_cp(("arbitrary", "arbitrary")),
    )(x, dxn, ssg, ng, y, gu, w_in, w_out)


def _ffn_bwd_w(h, dgu, hid, dout):
    n = h.shape[0]
    _, nf, _, tf = dgu.shape
    tm, _ = _ffn_tiles()
    ni = n // tm

    def body(h_ref, dgu_ref, hid_ref, dout_ref, dwin_ref, dwout_ref, acc_g, acc_u, acc_o):
        i = pl.program_id(1)

        @pl.when(i == 0)
        def _():
            acc_g[...] = jnp.zeros_like(acc_g)
            acc_u[...] = jnp.zeros_like(acc_u)
            acc_o[...] = jnp.zeros_like(acc_o)

        hv = h_ref[...]
        acc_g[...] += _dot_tn(dgu_ref[0], hv)
        acc_u[...] += _dot_tn(dgu_ref[1], hv)
        acc_o[...] += _dot_tn(hid_ref[...], dout_ref[...])

        @pl.when(i == ni - 1)
        def _():
            dwin_ref[0] = acc_g[...].astype(BF16)
            dwin_ref[1] = acc_u[...].astype(BF16)
            dwout_ref[...] = acc_o[...].astype(BF16)

    return _pcall(
        body, name="ffn_bwd_w", grid=(nf, ni),
        in_specs=[pl.BlockSpec((tm, D), lambda j, i: (i, 0)),
                  pl.BlockSpec((2, None, tm, tf), lambda j, i: (0, j, i, 0)),
                  pl.BlockSpec((None, tm, tf), lambda j, i: (j, i, 0)),
                  pl.BlockSpec((tm, D), lambda j, i: (i, 0))],
        out_specs=[pl.BlockSpec((2, None, tf, D), lambda j, i: (0, j, 0, 0)),
                   pl.BlockSpec((None, tf, D), lambda j, i: (j, 0, 0))],
        out_shape=[_sds((2, nf, tf, D), BF16), _sds((nf, tf, D), BF16)],
        scratch_shapes=[pltpu.VMEM((tf, D), F32), pltpu.VMEM((tf, D), F32), pltpu.VMEM((tf, D), F32)],
        compiler_params=_cp(("arbitrary", "arbitrary")),
    )(h, dgu, hid, dout)


def _premod_matmul(x, ssg, ng, w, bias, tn):
    n = x.shape[0]
    shards = w.ndim == 3
    m = w.shape[0] * w.shape[2] if shards else w.shape[0]
    tm = min(512, T)
    tpb = T // tm
    to = m if shards else tn
    w_spec = (pl.BlockSpec(w.shape, lambda i, j: (0, 0, 0)) if shards
              else pl.BlockSpec((tn, D), lambda i, j: (j, 0)))

    def body(x_ref, ssg_ref, ng_ref, w_ref, b_ref, h_ref, o_ref, h_scr):
        @pl.when(pl.program_id(1) == 0)
        def _():
            s = ssg_ref[0]
            hb = _modulate(x_ref[...], ng_ref[...], s[1:2], s[0:1]).astype(BF16)
            h_scr[...] = hb
            h_ref[...] = hb

        hv = h_scr[...]
        if shards:
            for q in range(w.shape[0]):
                cols = slice(q * tn, (q + 1) * tn)
                o_ref[:, cols] = _dot(hv, w_ref[q]) + b_ref[:, cols]
        else:
            o_ref[...] = _dot_nt(hv, w_ref[...]) + b_ref[...]

    return _pcall(
        body, name="premod_matmul", grid=(n // tm, m // to),
        in_specs=[pl.BlockSpec((tm, D), lambda i, j: (i, 0)),
                  pl.BlockSpec((1, 3, D), lambda i, j: (i // tpb, 0, 0)),
                  pl.BlockSpec((1, D), lambda i, j: (0, 0)),
                  w_spec,
                  pl.BlockSpec((1, to), lambda i, j: (0, j))],
        out_specs=[pl.BlockSpec((tm, D), lambda i, j: (i, 0)),
                   pl.BlockSpec((tm, to), lambda i, j: (i, j))],
        out_shape=[_sds((n, D), BF16), _sds((n, m), F32)],
        scratch_shapes=[pltpu.VMEM((tm, D), BF16)],
        compiler_params=_cp(("arbitrary", "arbitrary")),
    )(x, ssg, ng, w, bias)


def _premod_matmul_bwd(x, dxn, ssg, ng, douts, w):
    n = x.shape[0]
    k = len(douts)
    shards = w.ndim == 3
    tm = min(256, T)
    tpb = T // tm

    def body(*refs):
        x_ref, dxn_ref, ssg_ref, ng_ref = refs[:4]
        do_refs, w_ref = refs[4:4 + k], refs[4 + k]
        dx_ref, dssg_ref, dng_ref = refs[5 + k:]
        i = pl.program_id(0)
        dh = jnp.zeros((tm, D), F32)
        if shards:
            cs = w.shape[2]
            dov = do_refs[0][...]
            for j in range(w.shape[0]):
                dh += _dot_nt(dov[:, j * cs:(j + 1) * cs], w_ref[j])
        else:
            off = 0
            for q in range(k):
                mk = douts[q].shape[1]
                dh += _dot(do_refs[q][...], w_ref[off:off + mk, :])
                off += mk
        s = ssg_ref[0]
        _, vjp = jax.vjp(_modulate, x_ref[...], ng_ref[...], s[1:2], s[0:1])
        dx_, dng_, dsc_, dsh_ = vjp(dh)
        dx_ref[...] = dxn_ref[...] + dx_
        _acc_rows(dssg_ref.at[0], i % tpb == 0, [dsh_, dsc_, jnp.zeros_like(dsh_)])
        _acc_rows(dng_ref, i == 0, [dng_])

    return _pcall(
        body, name="premod_matmul_bwd", grid=(n // tm,),
        in_specs=[pl.BlockSpec((tm, D), lambda i: (i, 0)),
                  pl.BlockSpec((tm, D), lambda i: (i, 0)),
                  pl.BlockSpec((1, 3, D), lambda i: (i // tpb, 0, 0)),
                  pl.BlockSpec((1, D), lambda i: (0, 0))]
                 + [pl.BlockSpec((tm, a.shape[1]), lambda i: (i, 0)) for a in douts]
                 + [pl.BlockSpec(w.shape, (lambda i: (0, 0, 0)) if shards else (lambda i: (0, 0)))],
        out_specs=[pl.BlockSpec((tm, D), lambda i: (i, 0)),
                   pl.BlockSpec((1, 3, D), lambda i: (i // tpb, 0, 0)),
                   pl.BlockSpec((1, D), lambda i: (0, 0))],
        out_shape=[_sds((n, D), F32), _sds((BL, 3, D), F32), _sds((1, D), F32)],
        compiler_params=_cp(("arbitrary",)),
    )(x, dxn, ssg, ng, *douts, w)


def _matmul_res(x, a, ssg, w, bias):
    n, kd = a.shape
    tm = min(512, T)
    tpb = T // tm

    def body(x_ref, a_ref, ssg_ref, w_ref, b_ref, xn_ref, y_ref):
        yv = _dot(a_ref[...], w_ref[...]) + b_ref[...]
        y_ref[...] = yv.astype(BF16)
        xn_ref[...] = x_ref[...] + (1.0 + ssg_ref[0][2:3]) * yv

    return _pcall(
        body, name="matmul_res", grid=(n // tm,),
        in_specs=[pl.BlockSpec((tm, D), lambda i: (i, 0)),
                  pl.BlockSpec((tm, kd), lambda i: (i, 0)),
                  pl.BlockSpec((1, 3, D), lambda i: (i // tpb, 0, 0)),
                  pl.BlockSpec((kd, D), lambda i: (0, 0)),
                  pl.BlockSpec((1, D), lambda i: (0, 0))],
        out_specs=[pl.BlockSpec((tm, D), lambda i: (i, 0)), pl.BlockSpec((tm, D), lambda i: (i, 0))],
        out_shape=[_sds((n, D), F32), _sds((n, D), BF16)],
        compiler_params=_cp(("arbitrary",)),
    )(x, a, ssg, w, bias)


def _matmul_res_bwd(dxn, y, ssg, w):
    n = dxn.shape[0]
    kd = w.shape[0]
    tm = min(512, T)
    tpb = T // tm

    def body(dxn_ref, y_ref, ssg_ref, w_ref, da_ref, dy_ref, dgate_ref, dbias_ref):
        i = pl.program_id(0)
        dxn = dxn_ref[...]
        dy = (1.0 + ssg_ref[0][2:3]) * dxn
        dyb = dy.astype(BF16)
        dy_ref[...] = dyb
        da_ref[...] = _dot_nt(dyb, w_ref[...])
        _acc_rows(dgate_ref.at[0], i % tpb == 0, [jnp.sum(dxn * y_ref[...].astype(F32), axis=0, keepdims=True)])
        _acc_rows(dbias_ref, i == 0, [jnp.sum(dy, axis=0, keepdims=True)])

    return _pcall(
        body, name="matmul_res_bwd", grid=(n // tm,),
        in_specs=[pl.BlockSpec((tm, D), lambda i: (i, 0)),
                  pl.BlockSpec((tm, D), lambda i: (i, 0)),
                  pl.BlockSpec((1, 3, D), lambda i: (i // tpb, 0, 0)),
                  pl.BlockSpec((kd, D), lambda i: (0, 0))],
        out_specs=[pl.BlockSpec((tm, kd), lambda i: (i, 0)),
                   pl.BlockSpec((tm, D), lambda i: (i, 0)),
                   pl.BlockSpec((1, 1, D), lambda i: (i // tpb, 0, 0)),
                   pl.BlockSpec((1, D), lambda i: (0, 0))],
        out_shape=[_sds((n, kd), F32), _sds((n, D), BF16), _sds((BL, 1, D), F32), _sds((1, D), F32)],
        compiler_params=_cp(("arbitrary",)),
    )(dxn, y, ssg, w)


def _wgrad_shards(a, b, ns):
    n, kd = a.shape
    cs = b.shape[1] // ns
    tm = min(512, T)
    ni = n // tm

    def body(a_ref, b_ref, o_ref, acc):
        i = pl.program_id(1)

        @pl.when(i == 0)
        def _():
            acc[...] = jnp.zeros_like(acc)

        acc[...] += _dot_tn(a_ref[...], b_ref[...])

        @pl.when(i == ni - 1)
        def _():
            o_ref[...] = acc[...].astype(BF16)

    return _pcall(
        body, name="wgrad_shards", grid=(ns, ni),
        in_specs=[pl.BlockSpec((tm, kd), lambda q, i: (i, 0)), pl.BlockSpec((tm, cs), lambda q, i: (i, q))],
        out_specs=pl.BlockSpec((None, kd, cs), lambda q, i: (q, 0, 0)),
        out_shape=_sds((ns, kd, cs), BF16),
        scratch_shapes=[pltpu.VMEM((kd, cs), F32)],
        compiler_params=_cp(("arbitrary", "arbitrary")),
    )(a, b)


def _wgrad(a, b):
    n, kd = a.shape
    m = b.shape[1]
    tm = min(512, T)
    tk = min(512, kd)
    ni = n // tm

    def body(a_ref, b_ref, o_ref, acc):
        i = pl.program_id(1)

        @pl.when(i == 0)
        def _():
            acc[...] = jnp.zeros_like(acc)

        acc[...] += _dot_tn(a_ref[...], b_ref[...])

        @pl.when(i == ni - 1)
        def _():
            o_ref[...] = acc[...].astype(BF16)

    return _pcall(
        body, name="wgrad", grid=(kd // tk, ni),
        in_specs=[pl.BlockSpec((tm, tk), lambda q, i: (i, q)), pl.BlockSpec((tm, m), lambda q, i: (i, 0))],
        out_specs=pl.BlockSpec((tk, m), lambda q, i: (q, 0)),
        out_shape=_sds((kd, m), BF16),
        scratch_shapes=[pltpu.VMEM((tk, m), F32)],
        compiler_params=_cp(("arbitrary", "arbitrary")),
    )(a, b)


def _ln_silu(u1, g, b):
    mu = jnp.mean(u1, axis=-1, keepdims=True)
    xc = u1 - mu
    var = jnp.mean(xc * xc, axis=-1, keepdims=True)
    ln = xc * lax.rsqrt(var + EPS) * g + b
    return ln * jax.nn.sigmoid(ln)


def _conv_tiles():
    tt = min(256, T)
    return tt, T // tt


def _prev_halo_spec(cols, tt, halo):
    r = tt // halo
    return pl.BlockSpec((halo, cols), lambda b, i: (jnp.maximum(b * (T // halo) + i * r - 1, 0), 0))


def _next_halo_spec(cols, tt, halo):
    r = tt // halo
    last = BL * T // halo - 1
    return pl.BlockSpec((halo, cols), lambda b, i: (jnp.minimum(b * (T // halo) + (i + 1) * r, last), 0))


def _cm_mid_fwd(ab, w_dw, b_dw, ln_g, ln_b):
    n = ab.shape[0]
    tt, nt = _conv_tiles()

    def body(ab_ref, halo_ref, w_ref, bdw_ref, g_ref, b_ref, u1_ref, u2_ref, win):
        i = pl.program_id(1)
        hv = halo_ref[...]
        u0h = hv[:, :D] * jax.nn.sigmoid(hv[:, D:])
        win[0:HALO, :] = jnp.where(i == 0, 0.0, u0h)
        cv = ab_ref[...]
        win[HALO:HALO + tt, :] = cv[:, :D] * jax.nn.sigmoid(cv[:, D:])
        acc = jnp.zeros((tt, D), F32) + bdw_ref[...]
        for k in range(CW):
            acc += w_ref[k:k + 1, :] * win[pl.ds(HALO - (CW - 1) + k, tt), :]
        u1_ref[...] = acc
        u2_ref[...] = _ln_silu(acc, g_ref[...], b_ref[...]).astype(BF16)

    row = lambda b, i: (b * nt + i, 0)
    vec = pl.BlockSpec((1, D), lambda b, i: (0, 0))
    return _pcall(
        body, name="cm_mid_fwd", grid=(BL, nt),
        in_specs=[pl.BlockSpec((tt, 2 * D), row), _prev_halo_spec(2 * D, tt, HALO),
                  pl.BlockSpec((HALO, D), lambda b, i: (0, 0)), vec, vec, vec],
        out_specs=[pl.BlockSpec((tt, D), row), pl.BlockSpec((tt, D), row)],
        out_shape=[_sds((n, D), F32), _sds((n, D), BF16)],
        scratch_shapes=[pltpu.VMEM((HALO + tt, D), F32)],
        compiler_params=_cp(("arbitrary", "arbitrary")),
    )(ab, ab, w_dw, b_dw, ln_g, ln_b)


def _cm_mid_bwd_a(du2, u1, ln_g, ln_b):
    n = du2.shape[0]
    tm = min(256, T)

    def body(du2_ref, u1_ref, g_ref, b_ref, du1_ref, dln_ref):
        _, vjp = jax.vjp(_ln_silu, u1_ref[...], g_ref[...], b_ref[...])
        du1, dg, db = vjp(du2_ref[...])
        du1_ref[...] = du1
        _acc_rows(dln_ref, pl.program_id(0) == 0, [dg, db])

    vec = pl.BlockSpec((1, D), lambda i: (0, 0))
    return _pcall(
        body, name="cm_mid_bwd_a", grid=(n // tm,),
        in_specs=[pl.BlockSpec((tm, D), lambda i: (i, 0)), pl.BlockSpec((tm, D), lambda i: (i, 0)), vec, vec],
        out_specs=[pl.BlockSpec((tm, D), lambda i: (i, 0)), pl.BlockSpec((2, D), lambda i: (0, 0))],
        out_shape=[_sds((n, D), F32), _sds((2, D), F32)],
        compiler_params=_cp(("arbitrary",)),
    )(du2, u1, ln_g, ln_b)


def _cm_mid_bwd_b(du1, ab, w_dw):
    n = du1.shape[0]
    tt, nt = _conv_tiles()

    def body(du1_ref, nxt_ref, ab_ref, halo_ref, w_ref, dab_ref, dw_ref, dbdw_ref, dbglu_ref, dwin, uwin):
        b, i = pl.program_id(0), pl.program_id(1)
        first = jnp.logical_and(b == 0, i == 0)
        d1 = du1_ref[...]
        dwin[0:tt, :] = d1
        dwin[tt:tt + HALO, :] = jnp.where(i == nt - 1, 0.0, nxt_ref[...])
        hv = halo_ref[...]
        uwin[0:HALO, :] = jnp.where(i == 0, 0.0, hv[:, :D] * jax.nn.sigmoid(hv[:, D:]))
        cv = ab_ref[...]
        av, sg = cv[:, :D], jax.nn.sigmoid(cv[:, D:])
        uwin[HALO:HALO + tt, :] = av * sg
        du0 = jnp.zeros((tt, D), F32)
        dws = []
        for k in range(CW):
            du0 += w_ref[k:k + 1, :] * dwin[pl.ds(CW - 1 - k, tt), :]
            dws.append(jnp.sum(d1 * uwin[pl.ds(HALO - (CW - 1) + k, tt), :], axis=0, keepdims=True))
        dws += [jnp.zeros((1, D), F32)] * (HALO - CW)
        _acc_rows(dw_ref, first, dws)
        _acc_rows(dbdw_ref, first, [jnp.sum(d1, axis=0, keepdims=True)])
        da = du0 * sg
        db = du0 * av * sg * (1.0 - sg)
        dab_ref[:, :D] = da.astype(BF16)
        dab_ref[:, D:] = db.astype(BF16)
        _acc_rows(dbglu_ref.at[:, 0:D], first, [jnp.sum(da, axis=0, keepdims=True)])
        _acc_rows(dbglu_ref.at[:, D:2 * D], first, [jnp.sum(db, axis=0, keepdims=True)])

    row = lambda b, i: (b * nt + i, 0)
    return _pcall(
        body, name="cm_mid_bwd_b", grid=(BL, nt),
        in_specs=[pl.BlockSpec((tt, D), row), _next_halo_spec(D, tt, HALO),
                  pl.BlockSpec((tt, 2 * D), row), _prev_halo_spec(2 * D, tt, HALO),
                  pl.BlockSpec((HALO, D), lambda b, i: (0, 0))],
        out_specs=[pl.BlockSpec((tt, 2 * D), row), pl.BlockSpec((HALO, D), lambda b, i: (0, 0)),
                   pl.BlockSpec((1, D), lambda b, i: (0, 0)), pl.BlockSpec((1, 2 * D), lambda b, i: (0, 0))],
        out_shape=[_sds((n, 2 * D), BF16), _sds((HALO, D), F32), _sds((1, D), F32), _sds((1, 2 * D), F32)],
        scratch_shapes=[pltpu.VMEM((tt + HALO, D), F32), pltpu.VMEM((HALO + tt, D), F32)],
        compiler_params=_cp(("arbitrary", "arbitrary")),
    )(du1, du1, ab, ab, w_dw)


def _softplus(v):
    return jnp.maximum(v, 0.0) + jnp.log(1.0 + jnp.exp(-jnp.abs(v)))


def _g_beta(ab, alog, dtb):
    return -jnp.exp(alog) * _softplus(ab + dtb), jax.nn.sigmoid(ab)


def _dn_sconv_fwd(proj, w_sc, alog, dtb):
    n = proj.shape[0]
    tt, nt = _conv_tiles()
    w3 = 3 * D

    def body(qkv_ref, halo_ref, ab_ref, w_ref, alog_ref, dtb_ref, conv_ref, q_ref, k_ref, v_ref, gb_ref, bb_ref, win):
        i = pl.program_id(1)
        win[0:SHALO, :] = jnp.where(i == 0, 0.0, halo_ref[...])
        win[SHALO:SHALO + tt, :] = qkv_ref[...]
        acc = jnp.zeros((tt, w3), F32)
        for k in range(SCW):
            acc += w_ref[k:k + 1, :] * win[pl.ds(SHALO - (SCW - 1) + k, tt), :]
        conv_ref[...] = acc
        act = acc * jax.nn.sigmoid(acc)
        gfull, bfull = _g_beta(ab_ref[...], alog_ref[...], dtb_ref[...])
        for h in range(NH):
            q_ref[0, h] = act[:, h * DH:(h + 1) * DH]
            k_ref[0, h] = act[:, D + h * DH:D + (h + 1) * DH]
            v_ref[0, h] = act[:, 2 * D + h * DH:2 * D + (h + 1) * DH]
            gb_ref[0, h] = jnp.broadcast_to(gfull[:, h:h + 1], (tt, DH))
            bb_ref[0, h] = jnp.broadcast_to(bfull[:, NH + h:NH + h + 1], (tt, DH))

    row = lambda b, i: (b * nt + i, 0)
    head = pl.BlockSpec((1, NH, tt, DH), lambda b, i: (b, 0, i, 0))
    vec = pl.BlockSpec((1, 128), lambda b, i: (0, 0))
    hs = _sds((BL, NH, T, DH), F32)
    return _pcall(
        body, name="dn_sconv_fwd", grid=(BL, nt),
        in_specs=[pl.BlockSpec((tt, w3), row), _prev_halo_spec(w3, tt, SHALO),
                  pl.BlockSpec((tt, 128), lambda b, i: (b * nt + i, 4 * D // 128)),
                  pl.BlockSpec((SHALO, w3), lambda b, i: (0, 0)), vec, vec],
        out_specs=[pl.BlockSpec((tt, w3), row), head, head, head, head, head],
        out_shape=[_sds((n, w3), F32), hs, hs, hs, hs, hs],
        scratch_shapes=[pltpu.VMEM((SHALO + tt, w3), F32)],
        compiler_params=_cp(("arbitrary", "arbitrary")),
    )(proj, proj, proj, w_sc, alog, dtb)


_BMM_SPEC = {"nn": "gij,gjk->gik", "nt": "gid,gjd->gij", "tn": "gcd,gce->gde"}


def _mm(kind, a, b, prec):
    if prec is None:
        return jnp.einsum(_BMM_SPEC[kind], a.astype(BF16), b.astype(BF16), preferred_element_type=F32)
    return jnp.einsum(_BMM_SPEC[kind], a, b, preferred_element_type=F32, precision=prec)


@functools.partial(jax.custom_vjp, nondiff_argnums=(0, 3))
def _bmm_k(kind, a, b, prec):
    return _mm(kind, a, b, prec)


def _bmm_k_fwd(kind, a, b, prec):
    return _mm(kind, a, b, prec), (a, b)


def _bmm_k_bwd(kind, prec, res, dc):
    a, b = res
    if kind == "nn":
        return _bmm_k("nt", dc, b, prec), _bmm_k("tn", a, dc, prec)
    if kind == "nt":
        return _bmm_k("nn", dc, b, prec), _bmm_k("tn", dc, a, prec)
    return _bmm_k("nt", b, dc, prec), _bmm_k("nn", a, dc, prec)


_bmm_k.defvjp(_bmm_k_fwd, _bmm_k_bwd)


def _bmm(a, b, prec=None):
    return _bmm_k("nn", a, b, prec)


def _bmm_nt(a, b, prec=None):
    return _bmm_k("nt", a, b, prec)


def _bmm_tn(a, b, prec=None):
    return _bmm_k("tn", a, b, prec)


def _bmm_raw(a, b):
    return _mm("nn", a, b, None)


def _bmm_nt_raw(a, b):
    return _mm("nt", a, b, None)


def _bmm_tn_raw(a, b):
    return _mm("tn", a, b, None)


@jax.custom_vjp
def _unit_lower_inverse(a):
    eye = (lax.broadcasted_iota(jnp.int32, a.shape, 1) == lax.broadcasted_iota(jnp.int32, a.shape, 2)).astype(F32)
    t = eye - a
    p = a
    for _ in range(CHUNK.bit_length() - 2):
        p = _mm("nn", p, p, INV_PREC)
        t = _mm("nn", t, eye + p, INV_PREC)
    return t


def _uli_fwd(a):
    t = _unit_lower_inverse(a)
    return t, t


def _uli_bwd(t, dt):
    return (-_bmm_nt(_bmm_tn(t, dt, INV_PREC), t, INV_PREC),)


_unit_lower_inverse.defvjp(_uli_fwd, _uli_bwd)


def _dn_pre(q, k, v, gb, bb):
    shape = (q.shape[0], CHUNK, CHUNK)
    ri = lax.broadcasted_iota(jnp.int32, shape, 1)
    ci = lax.broadcasted_iota(jnp.int32, shape, 2)
    causal, strict = ri >= ci, ri > ci
    qn = q * lax.rsqrt(jnp.sum(q * q, axis=-1, keepdims=True) + EPS) * (DH ** -0.5)
    kn = k * lax.rsqrt(jnp.sum(k * k, axis=-1, keepdims=True) + EPS)
    gcs = _bmm(causal.astype(F32), gb, HI)
    gcol = gcs[:, :, :CHUNK]
    decay = jnp.exp(jnp.where(causal, gcol - jnp.swapaxes(gcol, 1, 2), -jnp.inf))
    eg = jnp.exp(gcs)
    kb = kn * bb
    a = jnp.where(strict, _bmm_nt(kb, kn) * decay, 0.0)
    tm = _unit_lower_inverse(a)
    u = _bmm(tm, v * bb)
    w = _bmm(tm, kb * eg)
    qg = qn * eg
    intra = _bmm_nt(qn, kn) * decay
    glast = gcs[:, CHUNK - 1:CHUNK, :]
    kd = kn * jnp.exp(glast - gcs)
    egl = jnp.broadcast_to(jnp.exp(glast), (q.shape[0], 8, DH))
    return u, w, qg, kd, intra, egl


def _pre_tiles():
    gcn = min(8, T // CHUNK)
    return gcn, T // (CHUNK * gcn)


def _dn_pre_specs():
    gcn, _ = _pre_tiles()
    tok = pl.BlockSpec((None, None, gcn * CHUNK, DH), lambda b, h, i: (b, h, i, 0))
    sq = pl.BlockSpec((None, None, gcn * CHUNK, CHUNK), lambda b, h, i: (b, h, i, 0))
    per = pl.BlockSpec((None, None, gcn * 8, DH), lambda b, h, i: (b, h, i, 0))
    return tok, sq, per


def _dn_pre_fwd(q, k, v, gb, bb):
    gcn, ng = _pre_tiles()
    tok, sq, per = _dn_pre_specs()

    def body(q_ref, k_ref, v_ref, gb_ref, bb_ref, u_ref, w_ref, qg_ref, kd_ref, in_ref, egl_ref):
        args = [r[...].reshape(gcn, CHUNK, DH) for r in (q_ref, k_ref, v_ref, gb_ref, bb_ref)]
        u, w, qg, kd, intra, egl = _dn_pre(*args)
        for r, val in ((u_ref, u), (w_ref, w), (qg_ref, qg), (kd_ref, kd)):
            r[...] = val.reshape(gcn * CHUNK, DH)
        in_ref[...] = intra.reshape(gcn * CHUNK, CHUNK)
        egl_ref[...] = egl.reshape(gcn * 8, DH)

    hs = _sds((BL, NH, T, DH), F32)
    return _pcall(
        body, name="dn_pre_fwd", grid=(BL, NH, ng),
        in_specs=[tok] * 5, out_specs=[tok, tok, tok, tok, sq, per],
        out_shape=[hs, hs, hs, hs, _sds((BL, NH, T, CHUNK), F32), _sds((BL, NH, T // CHUNK * 8, DH), F32)],
        compiler_params=_cp(("arbitrary",) * 3),
    )(q, k, v, gb, bb)


def _dn_pre_bwd(q, k, v, gb, bb, du, dw, dqg, dkd, dintra, degl):
    gcn, ng = _pre_tiles()
    tok, sq, per = _dn_pre_specs()

    def body(q_ref, k_ref, v_ref, gb_ref, bb_ref, du_ref, dw_ref, dqg_ref, dkd_ref, din_ref, degl_ref,
             dq_ref, dk_ref, dv_ref, dgb_ref, dbb_ref):
        args = [r[...].reshape(gcn, CHUNK, DH) for r in (q_ref, k_ref, v_ref, gb_ref, bb_ref)]
        _, vjp = jax.vjp(_dn_pre, *args)
        cts = [r[...].reshape(gcn, CHUNK, DH) for r in (du_ref, dw_ref, dqg_ref, dkd_ref)]
        de = degl_ref[...].reshape(gcn, 8, DH)
        one = jnp.logical_and(lax.broadcasted_iota(jnp.int32, de.shape, 1) == 0,
                              lax.broadcasted_iota(jnp.int32, de.shape, 2) == 0)
        outs = vjp((*cts, din_ref[...].reshape(gcn, CHUNK, CHUNK), jnp.where(one, de, 0.0)))
        for r, val in zip((dq_ref, dk_ref, dv_ref, dgb_ref, dbb_ref), outs):
            r[...] = val.reshape(gcn * CHUNK, DH)

    hs = _sds((BL, NH, T, DH), F32)
    return _pcall(
        body, name="dn_pre_bwd", grid=(BL, NH, ng),
        in_specs=[tok] * 9 + [sq, per], out_specs=[tok] * 5, out_shape=[hs] * 5,
        compiler_params=_cp(("arbitrary",) * 3),
    )(q, k, v, gb, bb, du, dw, dqg, dkd, dintra, degl)


def _scan_tiles():
    cs = min(2, T // CHUNK)
    return cs, T // (CHUNK * cs)


def _dn_scan_fwd(u, w, qg, kd, intra, egl):
    cs, ns = _scan_tiles()
    g = BL * NH
    nc = T // CHUNK

    def body(u_ref, w_ref, qg_ref, kd_ref, in_ref, egl_ref, o_ref, vn_ref, s0_ref, s_scr):
        @pl.when(pl.program_id(0) == 0)
        def _():
            s_scr[...] = jnp.zeros_like(s_scr)

        for c in range(cs):
            rows = pl.ds(c * CHUNK, CHUNK)
            s = s_scr[...]
            s0_ref[:, :, c] = s.reshape(BL, NH, DH, DH)

            def ld(r, m=DH):
                return r[:, :, rows, :].reshape(g, CHUNK, m)

            vn = ld(u_ref) - _bmm_raw(ld(w_ref), s)
            o = _bmm_raw(ld(qg_ref), s) + _bmm_raw(ld(in_ref, CHUNK), vn)
            e = egl_ref[:, :, pl.ds(c * 8, 1), :].reshape(g, 1, DH)
            s_scr[...] = s * e + _bmm_tn_raw(ld(kd_ref), vn)
            vn_ref[:, :, rows, :] = vn.reshape(BL, NH, CHUNK, DH)
            o_ref[:, :, rows, :] = o.reshape(BL, NH, CHUNK, DH)

    tok = pl.BlockSpec((BL, NH, cs * CHUNK, DH), lambda i: (0, 0, i, 0))
    hs = _sds((BL, NH, T, DH), F32)
    return _pcall(
        body, name="dn_scan_fwd", grid=(ns,),
        in_specs=[tok, tok, tok, tok, pl.BlockSpec((BL, NH, cs * CHUNK, CHUNK), lambda i: (0, 0, i, 0)),
                  pl.BlockSpec((BL, NH, cs * 8, DH), lambda i: (0, 0, i, 0))],
        out_specs=[tok, tok, pl.BlockSpec((BL, NH, cs, DH, DH), lambda i: (0, 0, i, 0, 0))],
        out_shape=[hs, hs, _sds((BL, NH, nc, DH, DH), F32)],
        scratch_shapes=[pltpu.VMEM((g, DH, DH), F32)],
        compiler_params=_cp(("arbitrary",)),
    )(u, w, qg, kd, intra, egl)


def _dn_scan_bwd(do, w, qg, kd, intra, egl, vn, s0):
    cs, ns = _scan_tiles()
    g = BL * NH
    nc = T // CHUNK

    def body(do_ref, w_ref, qg_ref, kd_ref, in_ref, egl_ref, vn_ref, s0_ref,
             du_ref, dw_ref, dqg_ref, dkd_ref, din_ref, degl_ref, ds_scr):
        @pl.when(pl.program_id(0) == 0)
        def _():
            ds_scr[...] = jnp.zeros_like(ds_scr)

        for c in reversed(range(cs)):
            rows = pl.ds(c * CHUNK, CHUNK)

            def ld(r, m=DH):
                return r[:, :, rows, :].reshape(g, CHUNK, m)

            def st(r, val, m=DH):
                r[:, :, rows, :] = val.reshape(BL, NH, CHUNK, m)

            s = s0_ref[:, :, c].reshape(g, DH, DH)
            ds = ds_scr[...]
            dov, vnv, kdv, wv, qgv, inv = ld(do_ref), ld(vn_ref), ld(kd_ref), ld(w_ref), ld(qg_ref), ld(in_ref, CHUNK)
            dv = _bmm_tn_raw(inv, dov) + _bmm_raw(kdv, ds)
            st(din_ref, _bmm_nt_raw(dov, vnv), CHUNK)
            st(dqg_ref, _bmm_nt_raw(dov, s))
            st(dkd_ref, _bmm_nt_raw(vnv, ds))
            st(du_ref, dv)
            st(dw_ref, -_bmm_nt_raw(dv, s))
            de = jnp.sum(jnp.sum(ds * s, axis=2, keepdims=True), axis=1, keepdims=True)
            degl_ref[:, :, pl.ds(c * 8, 8), :] = jnp.broadcast_to(de, (g, 8, DH)).reshape(BL, NH, 8, DH)
            e = egl_ref[:, :, pl.ds(c * 8, 1), :].reshape(g, 1, DH)
            ds_scr[...] = ds * e + _bmm_tn_raw(qgv, dov) - _bmm_tn_raw(wv, dv)

    rev = lambda i: (0, 0, ns - 1 - i, 0)
    tok = pl.BlockSpec((BL, NH, cs * CHUNK, DH), rev)
    sq = pl.BlockSpec((BL, NH, cs * CHUNK, CHUNK), rev)
    per = pl.BlockSpec((BL, NH, cs * 8, DH), rev)
    hs = _sds((BL, NH, T, DH), F32)
    return _pcall(
        body, name="dn_scan_bwd", grid=(ns,),
        in_specs=[tok, tok, tok, tok, sq, per, tok,
                  pl.BlockSpec((BL, NH, cs, DH, DH), lambda i: (0, 0, ns - 1 - i, 0, 0))],
        out_specs=[tok, tok, tok, tok, sq, per],
        out_shape=[hs, hs, hs, hs, _sds((BL, NH, T, CHUNK), F32), _sds((BL, NH, nc * 8, DH), F32)],
        scratch_shapes=[pltpu.VMEM((g, DH, DH), F32)],
        compiler_params=_cp(("arbitrary",)),
    )(do, w, qg, kd, intra, egl, vn, s0)


def _gated_norm(o_h, z_h, og):
    r = lax.rsqrt(jnp.mean(o_h * o_h, axis=-1, keepdims=True) + EPS)
    return (o_h * r * og) * (z_h * jax.nn.sigmoid(z_h))


def _dn_gnorm_fwd(o, proj, o_g):
    tm = min(256, T)
    nt = T // tm

    def body(o_ref, z_ref, g_ref, og_ref):
        z = z_ref[...]
        for h in range(NH):
            og_ref[:, h * DH:(h + 1) * DH] = _gated_norm(o_ref[0, h], z[:, h * DH:(h + 1) * DH], g_ref[...]).astype(BF16)

    return _pcall(
        body, name="dn_gnorm_fwd", grid=(BL, nt),
        in_specs=[pl.BlockSpec((1, NH, tm, DH), lambda b, i: (b, 0, i, 0)),
                  pl.BlockSpec((tm, D), lambda b, i: (b * nt + i, 3)),
                  pl.BlockSpec((1, DH), lambda b, i: (0, 0))],
        out_specs=pl.BlockSpec((tm, D), lambda b, i: (b * nt + i, 0)),
        out_shape=_sds((BL * T, D), BF16),
        compiler_params=_cp(("arbitrary", "arbitrary")),
    )(o, proj, o_g)


def _dn_gnorm_bwd(dog, o, proj, o_g):
    tm = min(256, T)
    nt = T // tm

    def body(dog_ref, o_ref, z_ref, g_ref, do_ref, dz_ref, dg_ref):
        z = z_ref[...]
        dog = dog_ref[...]
        dg = jnp.zeros((1, DH), F32)
        for h in range(NH):
            cols = slice(h * DH, (h + 1) * DH)
            _, vjp = jax.vjp(_gated_norm, o_ref[0, h], z[:, cols], g_ref[...])
            do_h, dz_h, dg_h = vjp(dog[:, cols])
            do_ref[0, h] = do_h
            dz_ref[:, cols] = dz_h.astype(BF16)
            dg += dg_h
        _acc_rows(dg_ref, jnp.logical_and(pl.program_id(0) == 0, pl.program_id(1) == 0), [dg])

    return _pcall(
        body, name="dn_gnorm_bwd", grid=(BL, nt),
        in_specs=[pl.BlockSpec((tm, D), lambda b, i: (b * nt + i, 0)),
                  pl.BlockSpec((1, NH, tm, DH), lambda b, i: (b, 0, i, 0)),
                  pl.BlockSpec((tm, D), lambda b, i: (b * nt + i, 3)),
                  pl.BlockSpec((1, DH), lambda b, i: (0, 0))],
        out_specs=[pl.BlockSpec((1, NH, tm, DH), lambda b, i: (b, 0, i, 0)),
                   pl.BlockSpec((tm, D), lambda b, i: (b * nt + i, 0)),
                   pl.BlockSpec((1, DH), lambda b, i: (0, 0))],
        out_shape=[_sds((BL, NH, T, DH), F32), _sds((BL * T, D), BF16), _sds((1, DH), F32)],
        compiler_params=_cp(("arbitrary", "arbitrary")),
    )(dog, o, proj, o_g)


def _dn_prep_bwd(dq, dk, dv, dgb, dbb, conv, proj, alog, dtb):
    n = conv.shape[0]
    tt, nt = _conv_tiles()
    w3 = 3 * D

    def body(dq_ref, dk_ref, dv_ref, dgb_ref, dbb_ref, conv_ref, ab_ref, alog_ref, dtb_ref, dconv_ref, dab_ref, dhead_ref):
        cv = conv_ref[...]
        sg = jax.nn.sigmoid(cv)
        dact = sg * (1.0 + cv * (1.0 - sg))
        lane = lax.broadcasted_iota(jnp.int32, (tt, 128), 1)
        cg = jnp.zeros((tt, 128), F32)
        cb = jnp.zeros((tt, 128), F32)
        for h in range(NH):
            cols = slice(h * DH, (h + 1) * DH)
            dconv_ref[:, h * DH:(h + 1) * DH] = dq_ref[0, h] * dact[:, cols]
            dconv_ref[:, D + h * DH:D + (h + 1) * DH] = dk_ref[0, h] * dact[:, D + h * DH:D + (h + 1) * DH]
            dconv_ref[:, 2 * D + h * DH:2 * D + (h + 1) * DH] = dv_ref[0, h] * dact[:, 2 * D + h * DH:2 * D + (h + 1) * DH]
            cg = jnp.where(lane == h, jnp.sum(dgb_ref[0, h], axis=-1, keepdims=True), cg)
            cb = jnp.where(lane == NH + h, jnp.sum(dbb_ref[0, h], axis=-1, keepdims=True), cb)
        _, vjp = jax.vjp(_g_beta, ab_ref[...], alog_ref[...], dtb_ref[...])
        dab, dalog, ddtb = vjp((cg, cb))
        dab_ref[...] = dab.astype(BF16)
        _acc_rows(dhead_ref, jnp.logical_and(pl.program_id(0) == 0, pl.program_id(1) == 0), [dalog, ddtb])

    row = lambda b, i: (b * nt + i, 0)
    head = pl.BlockSpec((1, NH, tt, DH), lambda b, i: (b, 0, i, 0))
    vec = pl.BlockSpec((1, 128), lambda b, i: (0, 0))
    return _pcall(
        body, name="dn_prep_bwd", grid=(BL, nt),
        in_specs=[head] * 5 + [pl.BlockSpec((tt, w3), row),
                               pl.BlockSpec((tt, 128), lambda b, i: (b * nt + i, 4 * D // 128)), vec, vec],
        out_specs=[pl.BlockSpec((tt, w3), row), pl.BlockSpec((tt, 128), row), pl.BlockSpec((2, 128), lambda b, i: (0, 0))],
        out_shape=[_sds((n, w3), F32), _sds((n, 128), BF16), _sds((2, 128), F32)],
        compiler_params=_cp(("arbitrary", "arbitrary")),
    )(dq, dk, dv, dgb, dbb, conv, proj, alog, dtb)


def _dn_sconv_bwd(dconv, proj, w_sc):
    n = dconv.shape[0]
    tt, nt = _conv_tiles()
    w3 = 3 * D

    def body(dc_ref, nxt_ref, qkv_ref, halo_ref, w_ref, dpre_ref, dw_ref, dwin, pwin):
        b, i = pl.program_id(0), pl.program_id(1)
        dc = dc_ref[...]
        dwin[0:tt, :] = dc
        dwin[tt:tt + SHALO, :] = jnp.where(i == nt - 1, 0.0, nxt_ref[...])
        pwin[0:SHALO, :] = jnp.where(i == 0, 0.0, halo_ref[...])
        pwin[SHALO:SHALO + tt, :] = qkv_ref[...]
        dpre = jnp.zeros((tt, w3), F32)
        dws = []
        for k in range(SCW):
            dpre += w_ref[k:k + 1, :] * dwin[pl.ds(SCW - 1 - k, tt), :]
            dws.append(jnp.sum(dc * pwin[pl.ds(SHALO - (SCW - 1) + k, tt), :], axis=0, keepdims=True))
        dws += [jnp.zeros((1, w3), F32)] * (SHALO - SCW)
        dpre_ref[...] = dpre.astype(BF16)
        _acc_rows(dw_ref, jnp.logical_and(b == 0, i == 0), dws)

    row = lambda b, i: (b * nt + i, 0)
    return _pcall(
        body, name="dn_sconv_bwd", grid=(BL, nt),
        in_specs=[pl.BlockSpec((tt, w3), row), _next_halo_spec(w3, tt, SHALO),
                  pl.BlockSpec((tt, w3), row), _prev_halo_spec(w3, tt, SHALO),
                  pl.BlockSpec((SHALO, w3), lambda b, i: (0, 0))],
        out_specs=[pl.BlockSpec((tt, w3), row), pl.BlockSpec((SHALO, w3), lambda b, i: (0, 0))],
        out_shape=[_sds((n, w3), BF16), _sds((SHALO, w3), F32)],
        scratch_shapes=[pltpu.VMEM((tt + SHALO, w3), F32), pltpu.VMEM((SHALO + tt, w3), F32)],
        compiler_params=_cp(("arbitrary", "arbitrary")),
    )(dconv, dconv, proj, proj, w_sc)


def _ada_fwd(c_all, w_ada, b_cols):
    nl, _, m = w_ada.shape
    nb = c_all.shape[0]

    def body(c_ref, w_ref, b_ref, o_ref):
        cv = c_ref[...]
        cs = (cv * jax.nn.sigmoid(cv)).astype(BF16)
        o_ref[...] = _dot(cs, w_ref[...].astype(BF16)) + b_ref[...]

    return _pcall(
        body, name="ada_fwd", grid=(nl,),
        in_specs=[pl.BlockSpec((nb, D), lambda l: (0, 0)), pl.BlockSpec((None, D, m), lambda l: (l, 0, 0)),
                  pl.BlockSpec((None, 1, m), lambda l: (l, 0, 0))],
        out_specs=pl.BlockSpec((None, nb, m), lambda l: (l, 0, 0)),
        out_shape=_sds((nl, nb, m), F32),
        compiler_params=_cp(("arbitrary",)),
    )(c_all, w_ada, b_cols)


def _ada_bwd(c_all, dmod_cols):
    nl, nb, m = dmod_cols.shape

    def body(c_ref, d_ref, o_ref):
        cv = c_ref[...]
        cs = (cv * jax.nn.sigmoid(cv)).astype(BF16)
        o_ref[0] = _dot_tn(cs, d_ref[...].astype(BF16))

    return _pcall(
        body, name="ada_bwd", grid=(nl,),
        in_specs=[pl.BlockSpec((nb, D), lambda l: (0, 0)), pl.BlockSpec((None, nb, m), lambda l: (l, 0, 0))],
        out_specs=pl.BlockSpec((1, D, m), lambda l: (0, l, 0)),
        out_shape=_sds((1, nl * D, m), F32),
        compiler_params=_cp(("arbitrary",)),
    )(c_all, dmod_cols)


def _loss_head(x, tgt, fg):
    n = x.shape[0]
    tm = min(512, T)

    def f(xv, g, t):
        r = lax.rsqrt(jnp.mean(xv * xv, axis=-1, keepdims=True) + EPS)
        e = xv * r * g - t
        return 0.5 * jnp.sum(e * e, axis=0, keepdims=True) * (1.0 / D)

    def body(x_ref, t_ref, g_ref, dx_ref, st_ref):
        t = t_ref[...]
        lrow, vjp = jax.vjp(lambda xv, g: f(xv, g, t), x_ref[...], g_ref[...])
        dx, dg = vjp(jnp.ones_like(lrow))
        dx_ref[...] = dx
        _acc_rows(st_ref, pl.program_id(0) == 0, [dg, lrow])

    return _pcall(
        body, name="loss_head", grid=(n // tm,),
        in_specs=[pl.BlockSpec((tm, D), lambda i: (i, 0)), pl.BlockSpec((tm, D), lambda i: (i, 0)),
                  pl.BlockSpec((1, D), lambda i: (0, 0))],
        out_specs=[pl.BlockSpec((tm, D), lambda i: (i, 0)), pl.BlockSpec((2, D), lambda i: (0, 0))],
        out_shape=[_sds((n, D), F32), _sds((2, D), F32)],
        compiler_params=_cp(("arbitrary",)),
    )(x, tgt, fg)


def _adamw(parts, w, m, v):
    p, r, c = parts.shape
    tr = r
    for cand in (256, 128, 64, 32, 16, 8):
        if r % cand == 0:
            tr = cand
            break
    k1 = 1.0 - B1 ** STEP
    k2 = 1.0 - B2 ** STEP

    def body(p_ref, w_ref, m_ref, v_ref, g_ref, d_ref, nm_ref, nv_ref):
        g = p_ref[0].astype(F32)
        for q in range(1, p):
            g += p_ref[q].astype(F32)
        mn = B1 * m_ref[...] + (1.0 - B1) * g
        vn = B2 * v_ref[...] + (1.0 - B2) * (g * g)
        g_ref[...] = g
        nm_ref[...] = mn
        nv_ref[...] = vn
        d_ref[...] = -LR * ((mn / k1) / (jnp.sqrt(vn / k2) + AEPS) + WD * w_ref[...])

    blk = pl.BlockSpec((tr, c), lambda i: (i, 0))
    return _pcall(
        body, name="adamw", grid=(r // tr,),
        in_specs=[pl.BlockSpec((p, tr, c), lambda i: (0, i, 0)), blk, blk, blk],
        out_specs=[blk] * 4, out_shape=[_sds((r, c), F32)] * 4,
        compiler_params=_cp(("arbitrary",)),
    )(parts, w, m, v)


def _adamw_slot(parts, w, m, v, outs, row0, col):
    p, r, c = parts.shape
    tr = r
    for cand in (256, 128, 64, 32, 16, 8):
        if r % cand == 0:
            tr = cand
            break
    if r % 352 == 0:
        tr = 352
    nt = r // tr
    k1 = 1.0 - B1 ** STEP
    k2 = 1.0 - B2 ** STEP

    def body(p_ref, w_ref, m_ref, v_ref, g0, d0, m0, v0, g_ref, d_ref, nm_ref, nv_ref):
        g = p_ref[0].astype(F32)
        for q in range(1, p):
            g += p_ref[q].astype(F32)
        mn = B1 * m_ref[...] + (1.0 - B1) * g
        vn = B2 * v_ref[...] + (1.0 - B2) * (g * g)
        g_ref[...] = g
        nm_ref[...] = mn
        nv_ref[...] = vn
        d_ref[...] = -LR * ((mn / k1) / (jnp.sqrt(vn / k2) + AEPS) + WD * w_ref[...])

    blk = pl.BlockSpec((tr, c), lambda i: (row0 * nt + i, col))
    anyspec = pl.BlockSpec(memory_space=pl.ANY)
    return _pcall(
        body, name="adamw_slot", grid=(nt,),
        in_specs=[pl.BlockSpec((p, tr, c), lambda i: (0, i, 0)), blk, blk, blk] + [anyspec] * 4,
        out_specs=[blk] * 4, out_shape=[_sds(w.shape, F32)] * 4,
        input_output_aliases={4: 0, 5: 1, 6: 2, 7: 3},
        compiler_params=_cp(("arbitrary",)),
    )(parts, w, m, v, *outs)


def _pack(arrs):
    flat = jnp.concatenate([a.reshape(-1) for a in arrs])
    pad = (-flat.shape[0]) % 1024
    return jnp.pad(flat, (0, pad)).reshape(-1, 128)


def _unpack(buf, shapes):
    flat = buf.reshape(-1)
    out, off = [], 0
    for s in shapes:
        size = 1
        for d in s:
            size *= d
        out.append(flat[off:off + size].reshape(s))
        off += size
    return out


def kernel(x, c, norm_g, w_ada, b_ada, w_ffn_in, w_ffn_out, cm_w_glu, cm_b_glu, cm_w_dw, cm_b_dw, cm_ln_g, cm_ln_b, cm_w_pw, cm_b_pw, dn_w_in, dn_w_sconv, dn_a_log, dn_dt_bias, dn_o_g, dn_w_out, final_g, loss_target, m_norm_g, m_w_ada, m_b_ada, m_w_ffn_in, m_w_ffn_out, m_cm_w_glu, m_cm_b_glu, m_cm_w_dw, m_cm_b_dw, m_cm_ln_g, m_cm_ln_b, m_cm_w_pw, m_cm_b_pw, m_dn_w_in, m_dn_w_sconv, m_dn_a_log, m_dn_dt_bias, m_dn_o_g, m_dn_w_out, m_final_g, v_norm_g, v_w_ada, v_b_ada, v_w_ffn_in, v_w_ffn_out, v_cm_w_glu, v_cm_b_glu, v_cm_w_dw, v_cm_b_dw, v_cm_ln_g, v_cm_ln_b, v_cm_w_pw, v_cm_b_pw, v_dn_w_in, v_dn_w_sconv, v_dn_a_log, v_dn_dt_bias, v_dn_o_g, v_dn_w_out, v_final_g):
    me = 4 * lax.axis_index("x") + 2 * lax.axis_index("y") + lax.axis_index("c")
    n = BL * T
    nf = 4
    tf = FF // nf
    na, nb = cm_w_glu.shape[0], dn_w_in.shape[0]
    mcols = w_ada.shape[2]
    dsh = D // NDEV

    tr_ffn = lambda a: jnp.swapaxes(a, 2, 3)
    tr_dn = lambda a: jnp.transpose(a, (2, 0, 1))
    wt_ffn_in, wt_dn_in = tr_ffn(w_ffn_in), tr_dn(dn_w_in)

    def unit_weights(l, part):
        if part == 0:
            ws = (wt_ffn_in[l, 0], w_ffn_out[l, 0])
        else:
            mix = (cm_w_glu[l // 2], cm_w_pw[l // 2]) if l % 2 == 0 else (wt_dn_in[:, l // 2], dn_w_out[l // 2])
            ws = (wt_ffn_in[l, 1], w_ffn_out[l, 1], *mix)
        return [w.astype(BF16) for w in ws]

    gathers, all_started = {}, jnp.zeros((8, 128), F32)
    for l in range(DEPTH):
        for part in range(2):
            gathers[l, part], tok = _exchange_start(unit_weights(l, part), True, f"gather_start_{l}_{part}")
            all_started = all_started + tok

    small_shapes = [c.shape, norm_g.shape, cm_w_dw.shape, dn_w_sconv.shape]
    small_g, = _all_gather([_pack([c, norm_g, cm_w_dw, dn_w_sconv])], "gather_small")
    sm = [_unpack(small_g[d], small_shapes) for d in range(NDEV)]
    c_all = jnp.concatenate([s[0] for s in sm], axis=0)
    norm_g_f = jnp.concatenate([s[1] for s in sm], axis=-1)
    w_dw_f = jnp.concatenate([s[2] for s in sm], axis=-1)
    w_sc_f = jnp.concatenate([s[3] for s in sm], axis=-1)

    b_cols = lax.dynamic_slice_in_dim(b_ada, me * mcols, mcols, axis=1)[:, None, :]
    mod_cols = _ada_fwd(c_all, w_ada, b_cols)
    mod_g, = _all_gather([mod_cols], "gather_mod")
    mod_all = jnp.transpose(mod_g, (1, 2, 0, 3)).reshape(DEPTH, NDEV * BL, 9 * D)
    mod = lax.dynamic_slice_in_dim(mod_all, me * BL, BL, axis=1).reshape(DEPTH, BL, 3, 3, D)

    gathered = [None] * DEPTH

    def ffn_weights(l, s):
        return gathered[l][s].reshape(2, nf, tf, D), gathered[l][2 + s].reshape(nf, tf, D)

    xs = x.reshape(n, D)
    saved = []
    for l in range(DEPTH):
        rec = {}
        ga = _exchange_wait(gathers[l, 0], all_started if l == 0 else xs, f"gather_wait_{l}_0")
        gathered[l] = [ga[0], None, ga[1], None, None, None]
        for s, j in ((0, 0), (1, 2)):
            if j == 2:
                gb = _exchange_wait(gathers[l, 1], xs, f"gather_wait_{l}_1")
                gathered[l] = [ga[0], gb[0], ga[1], gb[1], gb[2], gb[3]]
            w_in, w_out = ffn_weights(l, s)
            ssg, ng = mod[l, :, j], norm_g_f[l, j][None]
            if j == 2:
                ssg1, ng1 = mod[l, :, 1], norm_g_f[l, 1][None]
                if l % 2 == 0:
                    a = l // 2
                    w_glu = gathered[l][4]
                    w_pw = gathered[l][5].reshape(D, D)
                    w_dw = jnp.pad(w_dw_f[a], ((0, HALO - CW), (0, 0)))
                    h1, ab = _premod_matmul(xs, ssg1, ng1, w_glu, cm_b_glu[a][None], w_glu.shape[2])
                    u1, u2 = _cm_mid_fwd(ab, w_dw, cm_b_dw[a][None], cm_ln_g[a][None], cm_ln_b[a][None])
                    xn, ymix = _matmul_res(xs, u2, ssg1, w_pw, cm_b_pw[a][None])
                    rec["mix"] = dict(x=xs, h=h1, ab=ab, u1=u1, u2=u2, y=ymix, w_glu=w_glu, w_pw=w_pw, w_dw=w_dw)
                else:
                    mi = l // 2
                    w_proj = jnp.pad(gathered[l][4].reshape(4 * D + 2 * NH, D), ((0, 128 - 2 * NH), (0, 0)))
                    w_o = gathered[l][5].reshape(D, D)
                    w_sc = jnp.pad(w_sc_f[mi], ((0, SHALO - SCW), (0, 0)))
                    alog = jnp.pad(dn_a_log[mi], (0, 128 - NH))[None]
                    dtb = jnp.pad(dn_dt_bias[mi], (0, 128 - NH))[None]
                    h1, proj = _premod_matmul(xs, ssg1, ng1, w_proj, jnp.zeros((1, w_proj.shape[0]), F32),
                                              (4 * D + 128) // 3 if (4 * D + 128) % 384 == 0 else 128)
                    conv, q, k, v, gb, bb = _dn_sconv_fwd(proj, w_sc, alog, dtb)
                    u, w, qg, kd, intra, egl = _dn_pre_fwd(q, k, v, gb, bb)
                    o, vn, s0 = _dn_scan_fwd(u, w, qg, kd, intra, egl)
                    og = _dn_gnorm_fwd(o, proj, dn_o_g[mi][None])
                    xn, ymix = _matmul_res(xs, og, ssg1, w_o, jnp.zeros((1, D), F32))
                    rec["mix"] = dict(x=xs, h=h1, proj=proj, conv=conv, q=q, k=k, v=v, gb=gb, bb=bb, w=w, qg=qg, kd=kd,
                                      intra=intra, egl=egl, o=o, vn=vn, s0=s0, og=og, y=ymix, w_proj=w_proj, w_o=w_o,
                                      w_sc=w_sc, alog=alog, dtb=dtb)
                xs = xn
            xn, gu, hid, y = _ffn_fwd(xs, ssg, ng, w_in, w_out)
            rec[s] = dict(x=xs, gu=gu, hid=hid, y=y)
            xs = xn
        saved.append(rec)

    dx, stats = _loss_head(xs, loss_target.reshape(n, D), final_g[None])
    loss = lax.psum(jnp.sum(stats[1]), AXES)
    d_final_g = stats[0]

    d_mod = [[None] * 3 for _ in range(DEPTH)]
    d_norm = [[None] * 3 for _ in range(DEPTH)]
    dw_ffn_in = [[None] * 2 for _ in range(DEPTH)]
    dw_ffn_out = [[None] * 2 for _ in range(DEPTH)]
    dcm = [dict() for _ in range(na)]
    ddn = [dict() for _ in range(nb)]
    exchanges = {}

    def gather_small_grads():
        dmod_loc = jnp.stack([jnp.stack(d_mod[l], axis=1) for l in range(DEPTH)]).reshape(DEPTH, BL, 9 * D)
        small = [jnp.sum(dmod_loc, axis=1), jnp.stack([jnp.stack(d_norm[l]) for l in range(DEPTH)]),
                 jnp.stack([d["b_glu"] for d in dcm]), jnp.stack([d["w_dw"] for d in dcm]), jnp.stack([d["b_dw"] for d in dcm]),
                 jnp.stack([d["ln_g"] for d in dcm]), jnp.stack([d["ln_b"] for d in dcm]), jnp.stack([d["b_pw"] for d in dcm]),
                 jnp.stack([d["w_sconv"] for d in ddn]), jnp.stack([d["a_log"] for d in ddn]),
                 jnp.stack([d["dt_bias"] for d in ddn]), jnp.stack([d["o_g"] for d in ddn]), d_final_g]
        dmod_g, small_parts = _all_gather([dmod_loc, _pack(small)], "gather_small_grads")
        return dmod_g, small_parts, [a.shape for a in small]

    token = jnp.zeros((), F32)
    for l in reversed(range(DEPTH)):
        rec = saved[l]
        for s, j in ((1, 2), (0, 0)):
            w_in, w_out = ffn_weights(l, s)
            ssg, ng = mod[l, :, j] + token, norm_g_f[l, j][None]
            r = rec[s]
            dx, dgu, hb, dout, dssg, dng = _ffn_bwd_a(r["x"], dx, ssg, ng, r["y"], r["gu"], w_in, w_out)
            dw_ffn_in[l][s], dw_ffn_out[l][s] = _ffn_bwd_w(hb, dgu, r["hid"], dout)
            d_mod[l][j], d_norm[l][j] = dssg, dng[0]
            if j == 2:
                ssg1, ng1 = mod[l, :, 1], norm_g_f[l, 1][None]
                r = rec["mix"]
                if l % 2 == 0:
                    a = l // 2
                    du2, dy, dgate, db_pw = _matmul_res_bwd(dx, r["y"], ssg1, r["w_pw"])
                    du1, dln = _cm_mid_bwd_a(du2, r["u1"], cm_ln_g[a][None], cm_ln_b[a][None])
                    dab, dw_dw, db_dw, db_glu = _cm_mid_bwd_b(du1, r["ab"], r["w_dw"])
                    dx, dssg, dng = _premod_matmul_bwd(r["x"], dx, ssg1, ng1, [dab], r["w_glu"])
                    dcm[a] = dict(w_glu=_wgrad_shards(r["h"], dab, NDEV), w_pw=_wgrad(r["u2"], dy).reshape(NDEV, dsh, D),
                                  b_glu=db_glu[0], w_dw=dw_dw[:CW], b_dw=db_dw[0], ln_g=dln[0], ln_b=dln[1], b_pw=db_pw[0])
                else:
                    mi = l // 2
                    dog, dy, dgate, _ = _matmul_res_bwd(dx, r["y"], ssg1, r["w_o"])
                    do, dz, d_og = _dn_gnorm_bwd(dog, r["o"], r["proj"], dn_o_g[mi][None])
                    du, dw, dqg, dkd, dintra, degl = _dn_scan_bwd(do, r["w"], r["qg"], r["kd"], r["intra"], r["egl"],
                                                                   r["vn"], r["s0"])
                    dq, dk, dv, dgb, dbb = _dn_pre_bwd(r["q"], r["k"], r["v"], r["gb"], r["bb"], du, dw, dqg, dkd, dintra, degl)
                    dconv, dab16, dhead = _dn_prep_bwd(dq, dk, dv, dgb, dbb, r["conv"], r["proj"], r["alog"], r["dtb"])
                    dpre, dw_sc = _dn_sconv_bwd(dconv, r["proj"], r["w_sc"])
                    dx, dssg, dng = _premod_matmul_bwd(r["x"], dx, ssg1, ng1, [dpre, dz, dab16], r["w_proj"])
                    dw_in = jnp.concatenate([_wgrad(dpre, r["h"]), _wgrad(dz, r["h"]),
                                             _wgrad(dab16, r["h"])[:2 * NH]], axis=0)
                    ddn[mi] = dict(w_in=dw_in.reshape(NDEV, -1, D), w_out=_wgrad(r["og"], dy).reshape(NDEV, dsh, D),
                                   w_sconv=dw_sc[:SCW], a_log=dhead[0, :NH], dt_bias=dhead[1, :NH], o_g=d_og[0])
                d_mod[l][1] = dssg.at[:, 2].set(dgate[:, 0])
                d_norm[l][1] = dng[0]
            unit = [dw_ffn_in[l][s].reshape(NDEV, tf, D), dw_ffn_out[l][s].reshape(NDEV, FF // NDEV, D)]
            if j == 2:
                g = dcm[l // 2] if l % 2 == 0 else ddn[l // 2]
                unit += [g["w_glu"], g["w_pw"]] if l % 2 == 0 else [g["w_in"], g["w_out"]]
            if l == 0 and s == 0:
                dmod_g, small_parts, full_shapes = gather_small_grads()
                unit[0], dmod_g, small_parts = lax.optimization_barrier((unit[0], dmod_g, small_parts))
                small_gathered = (dmod_g, small_parts, full_shapes)
            exchanges[l, s], token = _exchange_start(unit, False, f"grads_start_{l}_{s}")
    grad_x = dx.reshape(BL, T, D)

    dmod_g, small_parts, full_shapes = small_gathered
    dmod_all = jnp.transpose(dmod_g, (1, 0, 2, 3)).reshape(DEPTH, NDEV * BL, 9 * D)
    g_w_ada = _ada_bwd(c_all, lax.dynamic_slice_in_dim(dmod_all, me * mcols, mcols, axis=2))

    got = []
    for l in range(DEPTH):
        ea = _exchange_wait(exchanges[l, 0], dx, f"grads_wait_{l}_0") if l > 0 else [None, None]
        eb = _exchange_wait(exchanges[l, 1], dx, f"grads_wait_{l}_1")
        got.append([ea[0], eb[0], ea[1], eb[1], eb[2], eb[3]])

    names = ["b_ada", "norm_g", "cm_b_glu", "cm_w_dw", "cm_b_dw", "cm_ln_g", "cm_ln_b", "cm_b_pw",
             "dn_w_sconv", "dn_a_log", "dn_dt_bias", "dn_o_g", "final_g"]
    cols = lambda a, width: lax.dynamic_slice_in_dim(a, me * width, width, axis=a.ndim - 1)
    local = {"norm_g": lambda a: cols(a, dsh), "cm_w_dw": lambda a: cols(a, dsh), "dn_w_sconv": lambda a: cols(a, 3 * dsh)}
    parts = [_unpack(small_parts[d], full_shapes) for d in range(NDEV)]
    parts = [[local.get(nm, lambda a: a)(p) for nm, p in zip(names, ps)] for ps in parts]
    small_w = dict(b_ada=(b_ada, m_b_ada, v_b_ada), norm_g=(norm_g, m_norm_g, v_norm_g),
                   cm_b_glu=(cm_b_glu, m_cm_b_glu, v_cm_b_glu), cm_w_dw=(cm_w_dw, m_cm_w_dw, v_cm_w_dw),
                   cm_b_dw=(cm_b_dw, m_cm_b_dw, v_cm_b_dw), cm_ln_g=(cm_ln_g, m_cm_ln_g, v_cm_ln_g),
                   cm_ln_b=(cm_ln_b, m_cm_ln_b, v_cm_ln_b), cm_b_pw=(cm_b_pw, m_cm_b_pw, v_cm_b_pw),
                   dn_w_sconv=(dn_w_sconv, m_dn_w_sconv, v_dn_w_sconv), dn_a_log=(dn_a_log, m_dn_a_log, v_dn_a_log),
                   dn_dt_bias=(dn_dt_bias, m_dn_dt_bias, v_dn_dt_bias), dn_o_g=(dn_o_g, m_dn_o_g, v_dn_o_g),
                   final_g=(final_g, m_final_g, v_final_g))
    loc_shapes = [small_w[nm][0].shape for nm in names]
    packed_parts = jnp.stack([_pack(ps) for ps in parts])
    sres_raw = _adamw(packed_parts, *[_pack([small_w[nm][q] for nm in names]) for q in range(3)])
    sres = [dict(zip(names, _unpack(r, loc_shapes))) for r in sres_raw]

    res = {}

    def update(slots, wmv, view, back=None, outs=None):
        w2, m2, v2 = [view(a) for a in wmv]
        outs = [lax.empty(w2.shape, F32) for _ in range(4)] if outs is None else outs
        for p, row0, col in slots:
            outs = _adamw_slot(p, w2, m2, v2, outs, row0, col)
        return outs if back is None else [back(o) for o in outs]

    ffn_slots = [(l, s) for l in reversed(range(DEPTH)) for s in (1, 0)][:-1]
    wmv_in, view_in = (w_ffn_in, m_w_ffn_in, v_w_ffn_in), lambda a: tr_ffn(a).reshape(-1, D)
    wmv_out, view_out = (w_ffn_out, m_w_ffn_out, v_w_ffn_out), lambda a: a.reshape(-1, D)
    part_in = update([(got[l][s], 2 * l + s, 0) for l, s in ffn_slots], wmv_in, view_in)
    part_out = update([(got[l][2 + s], 2 * l + s, 0) for l, s in ffn_slots], wmv_out, view_out)
    cgl = cm_w_glu.shape[2]
    res["cm_w_glu"] = update([(got[2 * a][4], a, 0) for a in range(na)], (cm_w_glu, m_cm_w_glu, v_cm_w_glu),
                             lambda a: a.reshape(-1, cgl), lambda o: o.reshape(cm_w_glu.shape))
    res["cm_w_pw"] = update([(got[2 * a][5], a, 0) for a in range(na)], (cm_w_pw, m_cm_w_pw, v_cm_w_pw),
                            lambda a: a.reshape(-1, D), lambda o: o.reshape(cm_w_pw.shape))
    cdn = dn_w_in.shape[2]
    res["dn_w_in"] = update([(got[2 * i + 1][4], 0, i) for i in range(nb)], (dn_w_in, m_dn_w_in, v_dn_w_in),
                            lambda a: tr_dn(a).reshape(cdn, nb * D),
                            lambda o: jnp.transpose(o.reshape(cdn, nb, D), (1, 2, 0)))
    res["dn_w_out"] = update([(got[2 * i + 1][5], i, 0) for i in range(nb)], (dn_w_out, m_dn_w_out, v_dn_w_out),
                             lambda a: a.reshape(-1, D), lambda o: o.reshape(dn_w_out.shape))
    res["w_ada"] = [o.reshape(w_ada.shape) for o in
                    _adamw(g_w_ada, *[a.reshape(-1, mcols) for a in (w_ada, m_w_ada, v_w_ada)])]
    done = [part_in[0], part_out[0], sres_raw[0]] + [res[nm][0] for nm in ("cm_w_glu", "cm_w_pw", "dn_w_in", "dn_w_out", "w_ada")]
    last = _exchange_wait(exchanges[0, 0], done, "grads_wait_0_0")
    res["w_ffn_in"] = update([(last[0], 0, 0)], wmv_in, view_in,
                             lambda o: jnp.swapaxes(o.reshape(DEPTH, 2, tf, D), 2, 3), part_in)
    res["w_ffn_out"] = update([(last[1], 0, 0)], wmv_out, view_out, lambda o: o.reshape(w_ffn_out.shape), part_out)
    for nm in names:
        res[nm] = [sres[q][nm] for q in range(4)]

    order = ["norm_g", "w_ada", "b_ada", "w_ffn_in", "w_ffn_out", "cm_w_glu", "cm_b_glu", "cm_w_dw", "cm_b_dw", "cm_ln_g",
             "cm_ln_b", "cm_w_pw", "cm_b_pw", "dn_w_in", "dn_w_sconv", "dn_a_log", "dn_dt_bias", "dn_o_g", "dn_w_out", "final_g"]
    return (loss, grad_x, *[res[nm][0] for nm in order], *[res[nm][1] for nm in order],
            *[res[nm][2] for nm in order], *[res[nm][3] for nm in order])
```

```python
import functools

import jax
import jax.numpy as jnp
from jax import lax
from jax.experimental import pallas as pl
from jax.experimental.pallas import tpu as pltpu

F32 = jnp.float32
BF16 = jnp.bfloat16
HI = lax.Precision.HIGHEST
INV_PREC = lax.Precision.HIGH
MESH = pl.DeviceIdType.MESH
AXES = ("x", "y", "c")

NDEV = 8
D = 1024
T = 2048
BL = 2
FF = 2816
NH = 8
DH = 128
CW = 31
SCW = 4
CHUNK = 64
DEPTH = 4
EPS = 1e-6
LR, B1, B2, AEPS, WD, STEP = 0.001, 0.9, 0.999, 1e-08, 0.01, 10

VMEM_LIMIT_BYTES = 56 * 1024 * 1024
HALO = 32
SHALO = 8


def _pcall(body, **kw):
    return pl.pallas_call(body, **kw)


def _cp(sem=None):
    return pltpu.CompilerParams(dimension_semantics=sem, vmem_limit_bytes=VMEM_LIMIT_BYTES)


def _sds(shape, dtype):
    return jax.ShapeDtypeStruct(tuple(shape), dtype)


def _dot(a, b):
    return jnp.dot(a, b, preferred_element_type=F32)


def _dot_nt(a, b):
    return lax.dot_general(a, b, (((1,), (1,)), ((), ())), preferred_element_type=F32)


def _dot_tn(a, b):
    return lax.dot_general(a, b, (((0,), (0,)), ((), ())), preferred_element_type=F32)


def _modulate(x, ng, scale, shift):
    r = lax.rsqrt(jnp.mean(x * x, axis=-1, keepdims=True) + EPS)
    return (x * r * ng) * (1.0 + scale) + shift


def _modulate_bwd(x, ng, scale, shift, dh):
    r = lax.rsqrt(jnp.mean(x * x, axis=-1, keepdims=True) + EPS)
    xh = x * r
    xg = xh * ng
    h = xg * (1.0 + scale) + shift
    a = dh * (1.0 + scale)
    dxh = a * ng
    dx = r * (dxh - xh * jnp.mean(dxh * xh, axis=-1, keepdims=True))
    return (h, dx, jnp.sum(a * xh, axis=0, keepdims=True), jnp.sum(dh * xg, axis=0, keepdims=True),
            jnp.sum(dh, axis=0, keepdims=True))


def _acc_rows(ref, first, rows):
    @pl.when(first)
    def _():
        ref[...] = jnp.zeros_like(ref)

    for r, val in enumerate(rows):
        ref[r:r + 1, :] += val


def _my_pos():
    return lax.axis_index("x"), lax.axis_index("y"), lax.axis_index("c")


def _all_gather(arrs, name):
    n = len(arrs)

    def body(*refs):
        ins, outs = refs[:n], refs[n:2 * n]
        send, recv, loc = refs[2 * n:]
        x, y, c = _my_pos()
        me, sibling = (x, y, c), (x, y, 1 - c)
        chips = [(1 - x, y), (x, 1 - y), (1 - x, 1 - y)]

        def copy(a, k, block, to, src=None):
            dst = outs[a].at[4 * block[0] + 2 * block[1] + block[2]]
            return pltpu.make_async_remote_copy(
                src_ref=dst if src is None else src, dst_ref=dst,
                send_sem=send.at[7 * a + k], recv_sem=recv.at[7 * a + k],
                device_id=to, device_id_type=MESH)

        mine, first, passed = [], [], []
        for a in range(n):
            m = pltpu.make_async_copy(ins[a], outs[a].at[4 * x + 2 * y + c], loc.at[a])
            m.start()
            mine.append(m)
            f = [copy(a, 0, me, sibling, src=ins[a])]
            f += [copy(a, 1 + j, me, (*chip, c), src=ins[a]) for j, chip in enumerate(chips)]
            for cp in f:
                cp.start()
            first += f
        for a in range(n):
            for j, chip in enumerate(chips):
                copy(a, 1 + j, (*chip, c), me).wait_recv()
                p = copy(a, 4 + j, (*chip, c), sibling)
                p.start()
                passed.append(p)
        for a in range(n):
            copy(a, 0, sibling, me).wait_recv()
            for j, chip in enumerate(chips):
                copy(a, 4 + j, (*chip, 1 - c), me).wait_recv()
        for cp in first + passed:
            cp.wait_send()
        for m in mine:
            m.wait()

    hbm = pl.BlockSpec(memory_space=pl.ANY)
    return _pcall(
        body, name=name,
        out_shape=[_sds((NDEV,) + a.shape, a.dtype) for a in arrs],
        in_specs=[hbm] * n, out_specs=[hbm] * n,
        scratch_shapes=[pltpu.SemaphoreType.DMA((7 * n,)), pltpu.SemaphoreType.DMA((7 * n,)),
                        pltpu.SemaphoreType.DMA((n,))],
    )(*arrs)


def _peer(k):
    x, y, c = _my_pos()
    return (1 - x if k & 4 else x, 1 - y if k & 2 else y, 1 - c if k & 1 else c)


def _dev_index(p):
    return 4 * p[0] + 2 * p[1] + p[2]


_HBM = pl.BlockSpec(memory_space=pltpu.HBM)
_SEM = pl.BlockSpec(memory_space=pltpu.SEMAPHORE)
_EFFECT = pltpu.SideEffectType.DATAFLOW_SIDE_EFFECTING


def _exchange_start(srcs, gather, name):
    n = len(srcs)
    me = _dev_index(_my_pos())
    lands = []
    for s in srcs:
        own = s if gather else lax.dynamic_index_in_dim(s, me, 0, keepdims=False)
        shape = (NDEV,) + s.shape if gather else s.shape
        lands.append(lax.dynamic_update_index_in_dim(lax.empty(shape, s.dtype), own, me, 0))

    def body(*refs):
        src_refs, land_refs = refs[:n], refs[n:2 * n]
        sends, recvs = refs[2 * n:3 * n], refs[3 * n:4 * n]
        token = refs[-1]
        mine = _dev_index(_my_pos())
        for a in range(n):
            for k in range(1, 8):
                p = _peer(k)
                pltpu.make_async_remote_copy(
                    src_ref=src_refs[a] if gather else src_refs[a].at[_dev_index(p)],
                    dst_ref=land_refs[a].at[mine], send_sem=sends[a], recv_sem=recvs[a],
                    device_id=p, device_id_type=MESH).start()
        token[...] = jnp.zeros_like(token)

    out = pl.pallas_call(
        body, name=name,
        out_shape=(*[pltpu.SemaphoreType.DMA(())] * (2 * n),
                   *[pltpu.HBM(a.shape, a.dtype) for a in srcs], *[pltpu.HBM(a.shape, a.dtype) for a in lands],
                   _sds((8, 128), F32)),
        in_specs=[_HBM] * (2 * n),
        out_specs=(*[_SEM] * (2 * n), *[_HBM] * (2 * n), pl.BlockSpec(memory_space=pltpu.VMEM)),
        input_output_aliases={i: 2 * n + i for i in range(2 * n)},
        compiler_params=pltpu.CompilerParams(has_side_effects=_EFFECT),
    )(*[pltpu.with_memory_space_constraint(a, pltpu.HBM) for a in srcs],
      *[pltpu.with_memory_space_constraint(a, pltpu.HBM) for a in lands])
    state = (out[:n], out[n:2 * n], out[2 * n:3 * n], out[3 * n:4 * n])
    return state, out[-1][0, 0]


def _exchange_wait(state, after, name):
    sends, recvs, srcs, lands = state
    n = len(srcs)
    after = list(after) if isinstance(after, (list, tuple)) else [after]

    def body(*refs):
        land_refs = refs[n:2 * n]
        send_refs, recv_refs = refs[2 * n:3 * n], refs[3 * n:4 * n]
        for a in range(n):
            seven = land_refs[a].at[pl.ds(0, NDEV - 1)]
            cp = pltpu.make_async_remote_copy(src_ref=seven, dst_ref=seven, send_sem=send_refs[a], recv_sem=recv_refs[a],
                                              device_id=_peer(1), device_id_type=MESH)
            cp.wait_send()
            cp.wait_recv()

    out = pl.pallas_call(
        body, name=name,
        out_shape=(*[pltpu.HBM(a.shape, a.dtype) for a in srcs], *[pltpu.HBM(a.shape, a.dtype) for a in lands]),
        in_specs=(*[_HBM] * (2 * n), *[_SEM] * (2 * n), *[pl.BlockSpec(memory_space=pl.ANY)] * len(after)),
        out_specs=tuple([_HBM] * (2 * n)),
        input_output_aliases={i: i for i in range(2 * n)},
        compiler_params=pltpu.CompilerParams(has_side_effects=_EFFECT),
    )(*srcs, *lands, *sends, *recvs, *after)
    return list(out[n:])


def _ffn_tiles():
    tm = min(512, T)
    return tm, T // tm


def _ffn_fwd(x, ssg, ng, w_in, w_out):
    n = x.shape[0]
    _, nf, tf, _ = w_in.shape
    tm, tpb = _ffn_tiles()

    def body(x_ref, ssg_ref, ng_ref, win_ref, wout_ref, xn_ref, gu_ref, hid_ref, y_ref, h_scr, acc):
        j = pl.program_id(1)

        @pl.when(j == 0)
        def _():
            s = ssg_ref[0]
            h_scr[...] = _modulate(x_ref[...], ng_ref[...], s[1:2], s[0:1]).astype(BF16)
            acc[...] = jnp.zeros_like(acc)

        h = h_scr[...]
        g = _dot_nt(h, win_ref[0])
        u = _dot_nt(h, win_ref[1])
        gu_ref[0] = g.astype(BF16)
        gu_ref[1] = u.astype(BF16)
        hid = (g * jax.nn.sigmoid(g) * u).astype(BF16)
        hid_ref[...] = hid
        acc[...] += _dot(hid, wout_ref[...])

        @pl.when(j == nf - 1)
        def _():
            yv = acc[...]
            y_ref[...] = yv.astype(BF16)
            xn_ref[...] = x_ref[...] + (0.5 * (1.0 + ssg_ref[0][2:3])) * yv

    return _pcall(
        body, name="ffn_fwd", grid=(n // tm, nf),
        in_specs=[pl.BlockSpec((tm, D), lambda i, j: (i, 0)),
                  pl.BlockSpec((1, 3, D), lambda i, j: (i // tpb, 0, 0)),
                  pl.BlockSpec((1, D), lambda i, j: (0, 0)),
                  pl.BlockSpec((2, None, tf, D), lambda i, j: (0, j, 0, 0)),
                  pl.BlockSpec((None, tf, D), lambda i, j: (j, 0, 0))],
        out_specs=[pl.BlockSpec((tm, D), lambda i, j: (i, 0)),
                   pl.BlockSpec((2, None, tm, tf), lambda i, j: (0, j, i, 0)),
                   pl.BlockSpec((None, tm, tf), lambda i, j: (j, i, 0)),
                   pl.BlockSpec((tm, D), lambda i, j: (i, 0))],
        out_shape=[_sds((n, D), F32), _sds((2, nf, n, tf), BF16), _sds((nf, n, tf), BF16), _sds((n, D), BF16)],
        scratch_shapes=[pltpu.VMEM((tm, D), BF16), pltpu.VMEM((tm, D), F32)],
        compiler_params=_cp(("arbitrary", "arbitrary")),
    )(x, ssg, ng, w_in, w_out)


def _ffn_bwd_a(x, dxn, ssg, ng, y, gu, w_in, w_out):
    n = x.shape[0]
    _, nf, tf, _ = w_in.shape
    tm, tpb = _ffn_tiles()

    def body(x_ref, dxn_ref, ssg_ref, ng_ref, y_ref, gu_ref, win_ref, wout_ref,
             dx_ref, dgu_ref, h_ref, dout_ref, dssg_ref, dng_ref, dout_scr, dh_acc):
        i, j = pl.program_id(0), pl.program_id(1)

        @pl.when(j == 0)
        def _():
            db = ((0.5 * (1.0 + ssg_ref[0][2:3])) * dxn_ref[...]).astype(BF16)
            dout_scr[...] = db
            dout_ref[...] = db
            dh_acc[...] = jnp.zeros_like(dh_acc)

        dhid = _dot_nt(dout_scr[...], wout_ref[...]).astype(BF16)
        g = gu_ref[0]
        u = gu_ref[1]
        sig = jax.nn.sigmoid(g)
        dg = dhid * u * (sig * (1.0 + g * (1.0 - sig)))
        du = dhid * (g * sig)
        dgu_ref[0] = dg
        dgu_ref[1] = du
        dh_acc[...] += _dot(dg, win_ref[0])
        dh_acc[...] += _dot(du, win_ref[1])

        @pl.when(j == nf - 1)
        def _():
            s = ssg_ref[0]
            h, dx_, dng_, dsc_, dsh_ = _modulate_bwd(x_ref[...], ng_ref[...], s[1:2], s[0:1], dh_acc[...])
            h_ref[...] = h.astype(BF16)
            dxn = dxn_ref[...]
            dx_ref[...] = dxn + dx_
            dgate = jnp.sum(0.5 * dxn * y_ref[...].astype(F32), axis=0, keepdims=True)
            _acc_rows(dssg_ref.at[0], i % tpb == 0, [dsh_, dsc_, dgate])
            _acc_rows(dng_ref, i == 0, [dng_])

    return _pcall(
        body, name="ffn_bwd_a", grid=(n // tm, nf),
        in_specs=[pl.BlockSpec((tm, D), lambda i, j: (i, 0)),
                  pl.BlockSpec((tm, D), lambda i, j: (i, 0)),
                  pl.BlockSpec((1, 3, D), lambda i, j: (i // tpb, 0, 0)),
                  pl.BlockSpec((1, D), lambda i, j: (0, 0)),
                  pl.BlockSpec((tm, D), lambda i, j: (i, 0)),
                  pl.BlockSpec((2, None, tm, tf), lambda i, j: (0, j, i, 0)),
                  pl.BlockSpec((2, None, tf, D), lambda i, j: (0, j, 0, 0)),
                  pl.BlockSpec((None, tf, D), lambda i, j: (j, 0, 0))],
        out_specs=[pl.BlockSpec((tm, D), lambda i, j: (i, 0)),
                   pl.BlockSpec((2, None, tm, tf), lambda i, j: (0, j, i, 0)),
                   pl.BlockSpec((tm, D), lambda i, j: (i, 0)),
                   pl.BlockSpec((tm, D), lambda i, j: (i, 0)),
                   pl.BlockSpec((1, 3, D), lambda i, j: (i // tpb, 0, 0)),
                   pl.BlockSpec((1, D), lambda i, j: (0, 0))],
        out_shape=[_sds((n, D), F32), _sds((2, nf, n, tf), BF16), _sds((n, D), BF16), _sds((n, D), BF16),
                   _sds((BL, 3, D), F32), _sds((1, D), F32)],
        scratch_shapes=[pltpu.VMEM((tm, D), BF16), pltpu.VMEM((tm, D), F32)],
        compiler_params=_cp(("arbitrary", "arbitrary")),
    )(x, dxn, ssg, ng, y, gu, w_in, w_out)


def _ffn_bwd_w(h, dgu, hid, dout):
    n = h.shape[0]
    _, nf, _, tf = dgu.shape
    tm, _ = _ffn_tiles()
    ni = n // tm

    def body(h_ref, dgu_ref, hid_ref, dout_ref, dwin_ref, dwout_ref, acc_g, acc_u, acc_o):
        i = pl.program_id(1)

        @pl.when(i == 0)
        def _():
            acc_g[...] = jnp.zeros_like(acc_g)
            acc_u[...] = jnp.zeros_like(acc_u)
            acc_o[...] = jnp.zeros_like(acc_o)

        hv = h_ref[...]
        acc_g[...] += _dot_tn(dgu_ref[0], hv)
        acc_u[...] += _dot_tn(dgu_ref[1], hv)
        acc_o[...] += _dot_tn(hid_ref[...], dout_ref[...])

        @pl.when(i == ni - 1)
        def _():
            dwin_ref[0] = acc_g[...].astype(BF16)
            dwin_ref[1] = acc_u[...].astype(BF16)
            dwout_ref[...] = acc_o[...].astype(BF16)

    return _pcall(
        body, name="ffn_bwd_w", grid=(nf, ni),
        in_specs=[pl.BlockSpec((tm, D), lambda j, i: (i, 0)),
                  pl.BlockSpec((2, None, tm, tf), lambda j, i: (0, j, i, 0)),
                  pl.BlockSpec((None, tm, tf), lambda j, i: (j, i, 0)),
                  pl.BlockSpec((tm, D), lambda j, i: (i, 0))],
        out_specs=[pl.BlockSpec((2, None, tf, D), lambda j, i: (0, j, 0, 0)),
                   pl.BlockSpec((None, tf, D), lambda j, i: (j, 0, 0))],
        out_shape=[_sds((2, nf, tf, D), BF16), _sds((nf, tf, D), BF16)],
        scratch_shapes=[pltpu.VMEM((tf, D), F32), pltpu.VMEM((tf, D), F32), pltpu.VMEM((tf, D), F32)],
        compiler_params=_cp(("arbitrary", "arbitrary")),
    )(h, dgu, hid, dout)


def _premod_matmul(x, ssg, ng, w, bias, tn):
    n = x.shape[0]
    shards = w.ndim == 3
    m = w.shape[0] * w.shape[2] if shards else w.shape[0]
    tm = min(512, T)
    tpb = T // tm
    to = m if shards else tn
    w_spec = (pl.BlockSpec(w.shape, lambda i, j: (0, 0, 0)) if shards
              else pl.BlockSpec((tn, D), lambda i, j: (j, 0)))

    def body(x_ref, ssg_ref, ng_ref, w_ref, b_ref, h_ref, o_ref, h_scr):
        @pl.when(pl.program_id(1) == 0)
        def _():
            s = ssg_ref[0]
            hb = _modulate(x_ref[...], ng_ref[...], s[1:2], s[0:1]).astype(BF16)
            h_scr[...] = hb
            h_ref[...] = hb

        hv = h_scr[...]
        if shards:
            for q in range(w.shape[0]):
                cols = slice(q * tn, (q + 1) * tn)
                o_ref[:, cols] = _dot(hv, w_ref[q]) + b_ref[:, cols]
        else:
            o_ref[...] = _dot_nt(hv, w_ref[...]) + b_ref[...]

    return _pcall(
        body, name="premod_matmul", grid=(n // tm, m // to),
        in_specs=[pl.BlockSpec((tm, D), lambda i, j: (i, 0)),
                  pl.BlockSpec((1, 3, D), lambda i, j: (i // tpb, 0, 0)),
                  pl.BlockSpec((1, D), lambda i, j: (0, 0)),
                  w_spec,
                  pl.BlockSpec((1, to), lambda i, j: (0, j))],
        out_specs=[pl.BlockSpec((tm, D), lambda i, j: (i, 0)),
                   pl.BlockSpec((tm, to), lambda i, j: (i, j))],
        out_shape=[_sds((n, D), BF16), _sds((n, m), F32)],
        scratch_shapes=[pltpu.VMEM((tm, D), BF16)],
        compiler_params=_cp(("arbitrary", "arbitrary")),
    )(x, ssg, ng, w, bias)


def _premod_matmul_bwd(x, dxn, ssg, ng, douts, w):
    n = x.shape[0]
    k = len(douts)
    shards = w.ndim == 3
    tm = min(256, T)
    tpb = T // tm

    def body(*refs):
        x_ref, dxn_ref, ssg_ref, ng_ref = refs[:4]
        do_refs, w_ref = refs[4:4 + k], refs[4 + k]
        dx_ref, dssg_ref, dng_ref = refs[5 + k:]
        i = pl.program_id(0)
        dh = jnp.zeros((tm, D), F32)
        if shards:
            cs = w.shape[2]
            dov = do_refs[0][...]
            for j in range(w.shape[0]):
                dh += _dot_nt(dov[:, j * cs:(j + 1) * cs], w_ref[j])
        else:
            off = 0
            for q in range(k):
                mk = douts[q].shape[1]
                dh += _dot(do_refs[q][...], w_ref[off:off + mk, :])
                off += mk
        s = ssg_ref[0]
        _, dx_, dng_, dsc_, dsh_ = _modulate_bwd(x_ref[...], ng_ref[...], s[1:2], s[0:1], dh)
        dx_ref[...] = dxn_ref[...] + dx_
        _acc_rows(dssg_ref.at[0], i % tpb == 0, [dsh_, dsc_, jnp.zeros_like(dsh_)])
        _acc_rows(dng_ref, i == 0, [dng_])

    return _pcall(
        body, name="premod_matmul_bwd", grid=(n // tm,),
        in_specs=[pl.BlockSpec((tm, D), lambda i: (i, 0)),
                  pl.BlockSpec((tm, D), lambda i: (i, 0)),
                  pl.BlockSpec((1, 3, D), lambda i: (i // tpb, 0, 0)),
                  pl.BlockSpec((1, D), lambda i: (0, 0))]
                 + [pl.BlockSpec((tm, a.shape[1]), lambda i: (i, 0)) for a in douts]
                 + [pl.BlockSpec(w.shape, (lambda i: (0, 0, 0)) if shards else (lambda i: (0, 0)))],
        out_specs=[pl.BlockSpec((tm, D), lambda i: (i, 0)),
                   pl.BlockSpec((1, 3, D), lambda i: (i // tpb, 0, 0)),
                   pl.BlockSpec((1, D), lambda i: (0, 0))],
        out_shape=[_sds((n, D), F32), _sds((BL, 3, D), F32), _sds((1, D), F32)],
        compiler_params=_cp(("arbitrary",)),
    )(x, dxn, ssg, ng, *douts, w)


def _matmul_res(x, a, ssg, w, bias):
    n, kd = a.shape
    tm = min(512, T)
    tpb = T // tm

    def body(x_ref, a_ref, ssg_ref, w_ref, b_ref, xn_ref, y_ref):
        yv = _dot(a_ref[...], w_ref[...]) + b_ref[...]
        y_ref[...] = yv.astype(BF16)
        xn_ref[...] = x_ref[...] + (1.0 + ssg_ref[0][2:3]) * yv

    return _pcall(
        body, name="matmul_res", grid=(n // tm,),
        in_specs=[pl.BlockSpec((tm, D), lambda i: (i, 0)),
                  pl.BlockSpec((tm, kd), lambda i: (i, 0)),
                  pl.BlockSpec((1, 3, D), lambda i: (i // tpb, 0, 0)),
                  pl.BlockSpec((kd, D), lambda i: (0, 0)),
                  pl.BlockSpec((1, D), lambda i: (0, 0))],
        out_specs=[pl.BlockSpec((tm, D), lambda i: (i, 0)), pl.BlockSpec((tm, D), lambda i: (i, 0))],
        out_shape=[_sds((n, D), F32), _sds((n, D), BF16)],
        compiler_params=_cp(("arbitrary",)),
    )(x, a, ssg, w, bias)


def _matmul_res_bwd(dxn, y, ssg, w):
    n = dxn.shape[0]
    kd = w.shape[0]
    tm = min(512, T)
    tpb = T // tm

    def body(dxn_ref, y_ref, ssg_ref, w_ref, da_ref, dy_ref, dgate_ref, dbias_ref):
        i = pl.program_id(0)
        dxn = dxn_ref[...]
        dy = (1.0 + ssg_ref[0][2:3]) * dxn
        dyb = dy.astype(BF16)
        dy_ref[...] = dyb
        da_ref[...] = _dot_nt(dyb, w_ref[...])
        _acc_rows(dgate_ref.at[0], i % tpb == 0, [jnp.sum(dxn * y_ref[...].astype(F32), axis=0, keepdims=True)])
        _acc_rows(dbias_ref, i == 0, [jnp.sum(dy, axis=0, keepdims=True)])

    return _pcall(
        body, name="matmul_res_bwd", grid=(n // tm,),
        in_specs=[pl.BlockSpec((tm, D), lambda i: (i, 0)),
                  pl.BlockSpec((tm, D), lambda i: (i, 0)),
                  pl.BlockSpec((1, 3, D), lambda i: (i // tpb, 0, 0)),
                  pl.BlockSpec((kd, D), lambda i: (0, 0))],
        out_specs=[pl.BlockSpec((tm, kd), lambda i: (i, 0)),
                   pl.BlockSpec((tm, D), lambda i: (i, 0)),
                   pl.BlockSpec((1, 1, D), lambda i: (i // tpb, 0, 0)),
                   pl.BlockSpec((1, D), lambda i: (0, 0))],
        out_shape=[_sds((n, kd), F32), _sds((n, D), BF16), _sds((BL, 1, D), F32), _sds((1, D), F32)],
        compiler_params=_cp(("arbitrary",)),
    )(dxn, y, ssg, w)


def _wgrad_shards(a, b, ns):
    n, kd = a.shape
    cs = b.shape[1] // ns
    tm = min(512, T)
    ni = n // tm

    def body(a_ref, b_ref, o_ref, acc):
        i = pl.program_id(1)

        @pl.when(i == 0)
        def _():
            acc[...] = jnp.zeros_like(acc)

        acc[...] += _dot_tn(a_ref[...], b_ref[...])

        @pl.when(i == ni - 1)
        def _():
            o_ref[...] = acc[...].astype(BF16)

    return _pcall(
        body, name="wgrad_shards", grid=(ns, ni),
        in_specs=[pl.BlockSpec((tm, kd), lambda q, i: (i, 0)), pl.BlockSpec((tm, cs), lambda q, i: (i, q))],
        out_specs=pl.BlockSpec((None, kd, cs), lambda q, i: (q, 0, 0)),
        out_shape=_sds((ns, kd, cs), BF16),
        scratch_shapes=[pltpu.VMEM((kd, cs), F32)],
        compiler_params=_cp(("arbitrary", "arbitrary")),
    )(a, b)


def _wgrad(a, b):
    n, kd = a.shape
    m = b.shape[1]
    tm = min(512, T)
    tk = min(512, kd)
    ni = n // tm

    def body(a_ref, b_ref, o_ref, acc):
        i = pl.program_id(1)

        @pl.when(i == 0)
        def _():
            acc[...] = jnp.zeros_like(acc)

        acc[...] += _dot_tn(a_ref[...], b_ref[...])

        @pl.when(i == ni - 1)
        def _():
            o_ref[...] = acc[...].astype(BF16)

    return _pcall(
        body, name="wgrad", grid=(kd // tk, ni),
        in_specs=[pl.BlockSpec((tm, tk), lambda q, i: (i, q)), pl.BlockSpec((tm, m), lambda q, i: (i, 0))],
        out_specs=pl.BlockSpec((tk, m), lambda q, i: (q, 0)),
        out_shape=_sds((kd, m), BF16),
        scratch_shapes=[pltpu.VMEM((tk, m), F32)],
        compiler_params=_cp(("arbitrary", "arbitrary")),
    )(a, b)


def _ln_silu(u1, g, b):
    mu = jnp.mean(u1, axis=-1, keepdims=True)
    xc = u1 - mu
    var = jnp.mean(xc * xc, axis=-1, keepdims=True)
    ln = xc * lax.rsqrt(var + EPS) * g + b
    return ln * jax.nn.sigmoid(ln)


def _conv_tiles():
    tt = min(256, T)
    return tt, T // tt


def _prev_halo_spec(cols, tt, halo):
    r = tt // halo
    return pl.BlockSpec((halo, cols), lambda b, i: (jnp.maximum(b * (T // halo) + i * r - 1, 0), 0))


def _next_halo_spec(cols, tt, halo):
    r = tt // halo
    last = BL * T // halo - 1
    return pl.BlockSpec((halo, cols), lambda b, i: (jnp.minimum(b * (T // halo) + (i + 1) * r, last), 0))


ROWS = 32


def _fill_rotations(rot, win, rows):
    for r in range(8):
        rot[r, 0:rows, :] = win[pl.ds(r, rows), :]


def _window(rot, off, start, size):
    return rot[off % 8, pl.ds(pl.multiple_of(start + (off // 8) * 8, 8), size), :]


def _cm_mid_fwd(ab, w_dw, b_dw, ln_g, ln_b):
    n = ab.shape[0]
    tt, nt = _conv_tiles()

    def body(ab_ref, halo_ref, w_ref, bdw_ref, g_ref, b_ref, u1_ref, u2_ref, win, rot):
        i = pl.program_id(1)
        hv = halo_ref[...]
        u0h = hv[:, :D] * jax.nn.sigmoid(hv[:, D:])
        win[0:HALO, :] = jnp.where(i == 0, 0.0, u0h)
        cv = ab_ref[...]
        win[HALO:HALO + tt, :] = cv[:, :D] * jax.nn.sigmoid(cv[:, D:])
        win[HALO + tt:, :] = jnp.zeros((8, D), F32)
        _fill_rotations(rot, win, tt + HALO)

        def chunk(c, carry):
            r0 = pl.multiple_of(c * ROWS, ROWS)
            acc = jnp.zeros((ROWS, D), F32) + bdw_ref[...]
            for k in range(CW):
                acc += w_ref[k:k + 1, :] * _window(rot, HALO - (CW - 1) + k, r0, ROWS)
            u1_ref[pl.ds(r0, ROWS), :] = acc
            u2_ref[pl.ds(r0, ROWS), :] = _ln_silu(acc, g_ref[...], b_ref[...]).astype(BF16)
            return carry

        lax.fori_loop(0, tt // ROWS, chunk, 0)

    row = lambda b, i: (b * nt + i, 0)
    vec = pl.BlockSpec((1, D), lambda b, i: (0, 0))
    return _pcall(
        body, name="cm_mid_fwd", grid=(BL, nt),
        in_specs=[pl.BlockSpec((tt, 2 * D), row), _prev_halo_spec(2 * D, tt, HALO),
                  pl.BlockSpec((HALO, D), lambda b, i: (0, 0)), vec, vec, vec],
        out_specs=[pl.BlockSpec((tt, D), row), pl.BlockSpec((tt, D), row)],
        out_shape=[_sds((n, D), F32), _sds((n, D), BF16)],
        scratch_shapes=[pltpu.VMEM((HALO + tt + 8, D), F32), pltpu.VMEM((8, tt + HALO, D), F32)],
        compiler_params=_cp(("arbitrary", "arbitrary")),
    )(ab, ab, w_dw, b_dw, ln_g, ln_b)


def _cm_mid_bwd_a(du2, u1, ln_g, ln_b):
    n = du2.shape[0]
    tm = min(256, T)

    def body(du2_ref, u1_ref, g_ref, b_ref, du1_ref, dln_ref):
        _, vjp = jax.vjp(_ln_silu, u1_ref[...], g_ref[...], b_ref[...])
        du1, dg, db = vjp(du2_ref[...])
        du1_ref[...] = du1
        _acc_rows(dln_ref, pl.program_id(0) == 0, [dg, db])

    vec = pl.BlockSpec((1, D), lambda i: (0, 0))
    return _pcall(
        body, name="cm_mid_bwd_a", grid=(n // tm,),
        in_specs=[pl.BlockSpec((tm, D), lambda i: (i, 0)), pl.BlockSpec((tm, D), lambda i: (i, 0)), vec, vec],
        out_specs=[pl.BlockSpec((tm, D), lambda i: (i, 0)), pl.BlockSpec((2, D), lambda i: (0, 0))],
        out_shape=[_sds((n, D), F32), _sds((2, D), F32)],
        compiler_params=_cp(("arbitrary",)),
    )(du2, u1, ln_g, ln_b)


def _cm_mid_bwd_b(du1, ab, w_dw):
    n = du1.shape[0]
    tt, nt = _conv_tiles()

    def body(du1_ref, nxt_ref, ab_ref, halo_ref, w_ref, dab_ref, dw_ref, dbdw_ref, dbglu_ref,
             dwin, uwin, rotd, rotu, accw, accv):
        b, i = pl.program_id(0), pl.program_id(1)
        first = jnp.logical_and(b == 0, i == 0)
        dwin[0:tt, :] = du1_ref[...]
        dwin[tt:tt + HALO, :] = jnp.where(i == nt - 1, 0.0, nxt_ref[...])
        dwin[tt + HALO:, :] = jnp.zeros((8, D), F32)
        hv = halo_ref[...]
        uwin[0:HALO, :] = jnp.where(i == 0, 0.0, hv[:, :D] * jax.nn.sigmoid(hv[:, D:]))
        cv = ab_ref[...]
        uwin[HALO:HALO + tt, :] = cv[:, :D] * jax.nn.sigmoid(cv[:, D:])
        uwin[HALO + tt:, :] = jnp.zeros((8, D), F32)
        _fill_rotations(rotd, dwin, tt + HALO)
        _fill_rotations(rotu, uwin, tt + HALO)
        accw[...] = jnp.zeros_like(accw)
        accv[...] = jnp.zeros_like(accv)

        def fold(v):
            return jnp.sum(v.reshape(ROWS // 8, 8, D), axis=0)

        def chunk(c, carry):
            r0 = pl.multiple_of(c * ROWS, ROWS)
            d1 = du1_ref[pl.ds(r0, ROWS), :]
            du0 = jnp.zeros((ROWS, D), F32)
            for k in range(CW):
                du0 += w_ref[k:k + 1, :] * _window(rotd, CW - 1 - k, r0, ROWS)
                accw[k] += fold(d1 * _window(rotu, HALO - (CW - 1) + k, r0, ROWS))
            cvc = ab_ref[pl.ds(r0, ROWS), :]
            av, sg = cvc[:, :D], jax.nn.sigmoid(cvc[:, D:])
            da = du0 * sg
            db = du0 * av * sg * (1.0 - sg)
            dab_ref[pl.ds(r0, ROWS), 0:D] = da.astype(BF16)
            dab_ref[pl.ds(r0, ROWS), D:2 * D] = db.astype(BF16)
            accv[0] += fold(d1)
            accv[1] += fold(da)
            accv[2] += fold(db)
            return carry

        lax.fori_loop(0, tt // ROWS, chunk, 0)
        dws = [jnp.sum(accw[k], axis=0, keepdims=True) for k in range(CW)]
        dws += [jnp.zeros((1, D), F32)] * (HALO - CW)
        _acc_rows(dw_ref, first, dws)
        _acc_rows(dbdw_ref, first, [jnp.sum(accv[0], axis=0, keepdims=True)])
        _acc_rows(dbglu_ref.at[:, 0:D], first, [jnp.sum(accv[1], axis=0, keepdims=True)])
        _acc_rows(dbglu_ref.at[:, D:2 * D], first, [jnp.sum(accv[2], axis=0, keepdims=True)])

    row = lambda b, i: (b * nt + i, 0)
    return _pcall(
        body, name="cm_mid_bwd_b", grid=(BL, nt),
        in_specs=[pl.BlockSpec((tt, D), row), _next_halo_spec(D, tt, HALO),
                  pl.BlockSpec((tt, 2 * D), row), _prev_halo_spec(2 * D, tt, HALO),
                  pl.BlockSpec((HALO, D), lambda b, i: (0, 0))],
        out_specs=[pl.BlockSpec((tt, 2 * D), row), pl.BlockSpec((HALO, D), lambda b, i: (0, 0)),
                   pl.BlockSpec((1, D), lambda b, i: (0, 0)), pl.BlockSpec((1, 2 * D), lambda b, i: (0, 0))],
        out_shape=[_sds((n, 2 * D), BF16), _sds((HALO, D), F32), _sds((1, D), F32), _sds((1, 2 * D), F32)],
        scratch_shapes=[pltpu.VMEM((tt + HALO + 8, D), F32), pltpu.VMEM((HALO + tt + 8, D), F32),
                        pltpu.VMEM((8, tt + HALO, D), F32), pltpu.VMEM((8, tt + HALO, D), F32),
                        pltpu.VMEM((HALO, 8, D), F32), pltpu.VMEM((3, 8, D), F32)],
        compiler_params=_cp(("arbitrary", "arbitrary")),
    )(du1, du1, ab, ab, w_dw)


def _softplus(v):
    return jnp.maximum(v, 0.0) + jnp.log(1.0 + jnp.exp(-jnp.abs(v)))


def _g_beta(ab, alog, dtb):
    return -jnp.exp(alog) * _softplus(ab + dtb), jax.nn.sigmoid(ab)


def _dn_sconv_fwd(proj, w_sc, alog, dtb):
    n = proj.shape[0]
    tt, nt = _conv_tiles()
    w3 = 3 * D

    def body(qkv_ref, halo_ref, ab_ref, w_ref, alog_ref, dtb_ref, conv_ref, q_ref, k_ref, v_ref, gb_ref, bb_ref, win):
        i = pl.program_id(1)
        win[0:SHALO, :] = jnp.where(i == 0, 0.0, halo_ref[...])
        win[SHALO:SHALO + tt, :] = qkv_ref[...]
        acc = jnp.zeros((tt, w3), F32)
        for k in range(SCW):
            acc += w_ref[k:k + 1, :] * win[pl.ds(SHALO - (SCW - 1) + k, tt), :]
        conv_ref[...] = acc
        act = acc * jax.nn.sigmoid(acc)
        gfull, bfull = _g_beta(ab_ref[...], alog_ref[...], dtb_ref[...])
        for h in range(NH):
            q_ref[0, h] = act[:, h * DH:(h + 1) * DH]
            k_ref[0, h] = act[:, D + h * DH:D + (h + 1) * DH]
            v_ref[0, h] = act[:, 2 * D + h * DH:2 * D + (h + 1) * DH]
            gb_ref[0, h] = jnp.broadcast_to(gfull[:, h:h + 1], (tt, DH))
            bb_ref[0, h] = jnp.broadcast_to(bfull[:, NH + h:NH + h + 1], (tt, DH))

    row = lambda b, i: (b * nt + i, 0)
    head = pl.BlockSpec((1, NH, tt, DH), lambda b, i: (b, 0, i, 0))
    vec = pl.BlockSpec((1, 128), lambda b, i: (0, 0))
    hs = _sds((BL, NH, T, DH), F32)
    return _pcall(
        body, name="dn_sconv_fwd", grid=(BL, nt),
        in_specs=[pl.BlockSpec((tt, w3), row), _prev_halo_spec(w3, tt, SHALO),
                  pl.BlockSpec((tt, 128), lambda b, i: (b * nt + i, 4 * D // 128)),
                  pl.BlockSpec((SHALO, w3), lambda b, i: (0, 0)), vec, vec],
        out_specs=[pl.BlockSpec((tt, w3), row), head, head, head, head, head],
        out_shape=[_sds((n, w3), F32), hs, hs, hs, hs, hs],
        scratch_shapes=[pltpu.VMEM((SHALO + tt, w3), F32)],
        compiler_params=_cp(("arbitrary", "arbitrary")),
    )(proj, proj, proj, w_sc, alog, dtb)


_BMM_SPEC = {"nn": "gij,gjk->gik", "nt": "gid,gjd->gij", "tn": "gcd,gce->gde"}


def _mm(kind, a, b, prec):
    if prec is None:
        return jnp.einsum(_BMM_SPEC[kind], a.astype(BF16), b.astype(BF16), preferred_element_type=F32)
    return jnp.einsum(_BMM_SPEC[kind], a, b, preferred_element_type=F32, precision=prec)


@functools.partial(jax.custom_vjp, nondiff_argnums=(0, 3))
def _bmm_k(kind, a, b, prec):
    return _mm(kind, a, b, prec)


def _bmm_k_fwd(kind, a, b, prec):
    return _mm(kind, a, b, prec), (a, b)


def _bmm_k_bwd(kind, prec, res, dc):
    a, b = res
    if kind == "nn":
        return _bmm_k("nt", dc, b, prec), _bmm_k("tn", a, dc, prec)
    if kind == "nt":
        return _bmm_k("nn", dc, b, prec), _bmm_k("tn", dc, a, prec)
    return _bmm_k("nt", b, dc, prec), _bmm_k("nn", a, dc, prec)


_bmm_k.defvjp(_bmm_k_fwd, _bmm_k_bwd)


def _bmm(a, b, prec=None):
    return _bmm_k("nn", a, b, prec)


def _bmm_nt(a, b, prec=None):
    return _bmm_k("nt", a, b, prec)


def _bmm_tn(a, b, prec=None):
    return _bmm_k("tn", a, b, prec)


def _bmm_raw(a, b):
    return _mm("nn", a, b, None)


def _bmm_nt_raw(a, b):
    return _mm("nt", a, b, None)


def _bmm_tn_raw(a, b):
    return _mm("tn", a, b, None)


@jax.custom_vjp
def _unit_lower_inverse(a):
    eye = (lax.broadcasted_iota(jnp.int32, a.shape, 1) == lax.broadcasted_iota(jnp.int32, a.shape, 2)).astype(F32)
    t = eye - a
    p = a
    for _ in range(CHUNK.bit_length() - 2):
        p = _mm("nn", p, p, INV_PREC)
        t = _mm("nn", t, eye + p, INV_PREC)
    return t


def _uli_fwd(a):
    t = _unit_lower_inverse(a)
    return t, t


def _uli_bwd(t, dt):
    return (-_bmm_nt(_bmm_tn(t, dt, INV_PREC), t, INV_PREC),)


_unit_lower_inverse.defvjp(_uli_fwd, _uli_bwd)


def _dn_pre(q, k, v, gb, bb):
    shape = (q.shape[0], CHUNK, CHUNK)
    ri = lax.broadcasted_iota(jnp.int32, shape, 1)
    ci = lax.broadcasted_iota(jnp.int32, shape, 2)
    causal, strict = ri >= ci, ri > ci
    qn = q * lax.rsqrt(jnp.sum(q * q, axis=-1, keepdims=True) + EPS) * (DH ** -0.5)
    kn = k * lax.rsqrt(jnp.sum(k * k, axis=-1, keepdims=True) + EPS)
    gcs = _bmm(causal.astype(F32), gb, HI)
    gcol = gcs[:, :, :CHUNK]
    decay = jnp.exp(jnp.where(causal, gcol - jnp.swapaxes(gcol, 1, 2), -jnp.inf))
    eg = jnp.exp(gcs)
    kb = kn * bb
    a = jnp.where(strict, _bmm_nt(kb, kn) * decay, 0.0)
    tm = _unit_lower_inverse(a)
    u = _bmm(tm, v * bb)
    w = _bmm(tm, kb * eg)
    qg = qn * eg
    intra = _bmm_nt(qn, kn) * decay
    glast = gcs[:, CHUNK - 1:CHUNK, :]
    kd = kn * jnp.exp(glast - gcs)
    egl = jnp.broadcast_to(jnp.exp(glast), (q.shape[0], 8, DH))
    return u, w, qg, kd, intra, egl


def _pre_tiles():
    gcn = min(16, T // CHUNK)
    return gcn, T // (CHUNK * gcn)


def _dn_pre_specs():
    gcn, _ = _pre_tiles()
    tok = pl.BlockSpec((None, None, gcn * CHUNK, DH), lambda b, h, i: (b, h, i, 0))
    sq = pl.BlockSpec((None, None, gcn * CHUNK, CHUNK), lambda b, h, i: (b, h, i, 0))
    per = pl.BlockSpec((None, None, gcn * 8, DH), lambda b, h, i: (b, h, i, 0))
    return tok, sq, per


def _dn_pre_fwd(q, k, v, gb, bb):
    gcn, ng = _pre_tiles()
    tok, sq, per = _dn_pre_specs()

    def body(q_ref, k_ref, v_ref, gb_ref, bb_ref, u_ref, w_ref, qg_ref, kd_ref, in_ref, egl_ref):
        args = [r[...].reshape(gcn, CHUNK, DH) for r in (q_ref, k_ref, v_ref, gb_ref, bb_ref)]
        u, w, qg, kd, intra, egl = _dn_pre(*args)
        for r, val in ((u_ref, u), (w_ref, w), (qg_ref, qg), (kd_ref, kd)):
            r[...] = val.reshape(gcn * CHUNK, DH)
        in_ref[...] = intra.reshape(gcn * CHUNK, CHUNK)
        egl_ref[...] = egl.reshape(gcn * 8, DH)

    hs = _sds((BL, NH, T, DH), F32)
    return _pcall(
        body, name="dn_pre_fwd", grid=(BL, NH, ng),
        in_specs=[tok] * 5, out_specs=[tok, tok, tok, tok, sq, per],
        out_shape=[hs, hs, hs, hs, _sds((BL, NH, T, CHUNK), F32), _sds((BL, NH, T // CHUNK * 8, DH), F32)],
        compiler_params=_cp(("arbitrary",) * 3),
    )(q, k, v, gb, bb)


def _dn_pre_bwd(q, k, v, gb, bb, du, dw, dqg, dkd, dintra, degl):
    gcn, ng = _pre_tiles()
    tok, sq, per = _dn_pre_specs()

    def body(q_ref, k_ref, v_ref, gb_ref, bb_ref, du_ref, dw_ref, dqg_ref, dkd_ref, din_ref, degl_ref,
             dq_ref, dk_ref, dv_ref, dgb_ref, dbb_ref):
        args = [r[...].reshape(gcn, CHUNK, DH) for r in (q_ref, k_ref, v_ref, gb_ref, bb_ref)]
        _, vjp = jax.vjp(_dn_pre, *args)
        cts = [r[...].reshape(gcn, CHUNK, DH) for r in (du_ref, dw_ref, dqg_ref, dkd_ref)]
        de = degl_ref[...].reshape(gcn, 8, DH)
        one = jnp.logical_and(lax.broadcasted_iota(jnp.int32, de.shape, 1) == 0,
                              lax.broadcasted_iota(jnp.int32, de.shape, 2) == 0)
        outs = vjp((*cts, din_ref[...].reshape(gcn, CHUNK, CHUNK), jnp.where(one, de, 0.0)))
        for r, val in zip((dq_ref, dk_ref, dv_ref, dgb_ref, dbb_ref), outs):
            r[...] = val.reshape(gcn * CHUNK, DH)

    hs = _sds((BL, NH, T, DH), F32)
    return _pcall(
        body, name="dn_pre_bwd", grid=(BL, NH, ng),
        in_specs=[tok] * 9 + [sq, per], out_specs=[tok] * 5, out_shape=[hs] * 5,
        compiler_params=_cp(("arbitrary",) * 3),
    )(q, k, v, gb, bb, du, dw, dqg, dkd, dintra, degl)


def _scan_tiles():
    cs = min(2, T // CHUNK)
    return cs, T // (CHUNK * cs)


def _dn_scan_fwd(u, w, qg, kd, intra, egl):
    cs, ns = _scan_tiles()
    g = BL * NH
    nc = T // CHUNK

    def body(u_ref, w_ref, qg_ref, kd_ref, in_ref, egl_ref, o_ref, vn_ref, s0_ref, s_scr):
        @pl.when(pl.program_id(0) == 0)
        def _():
            s_scr[...] = jnp.zeros_like(s_scr)

        for c in range(cs):
            rows = pl.ds(c * CHUNK, CHUNK)
            s = s_scr[...]
            s0_ref[:, :, c] = s.reshape(BL, NH, DH, DH)

            def ld(r, m=DH):
                return r[:, :, rows, :].reshape(g, CHUNK, m)

            vn = ld(u_ref) - _bmm_raw(ld(w_ref), s)
            o = _bmm_raw(ld(qg_ref), s) + _bmm_raw(ld(in_ref, CHUNK), vn)
            e = egl_ref[:, :, pl.ds(c * 8, 1), :].reshape(g, 1, DH)
            s_scr[...] = s * e + _bmm_tn_raw(ld(kd_ref), vn)
            vn_ref[:, :, rows, :] = vn.reshape(BL, NH, CHUNK, DH)
            o_ref[:, :, rows, :] = o.reshape(BL, NH, CHUNK, DH)

    tok = pl.BlockSpec((BL, NH, cs * CHUNK, DH), lambda i: (0, 0, i, 0))
    hs = _sds((BL, NH, T, DH), F32)
    return _pcall(
        body, name="dn_scan_fwd", grid=(ns,),
        in_specs=[tok, tok, tok, tok, pl.BlockSpec((BL, NH, cs * CHUNK, CHUNK), lambda i: (0, 0, i, 0)),
                  pl.BlockSpec((BL, NH, cs * 8, DH), lambda i: (0, 0, i, 0))],
        out_specs=[tok, tok, pl.BlockSpec((BL, NH, cs, DH, DH), lambda i: (0, 0, i, 0, 0))],
        out_shape=[hs, hs, _sds((BL, NH, nc, DH, DH), F32)],
        scratch_shapes=[pltpu.VMEM((g, DH, DH), F32)],
        compiler_params=_cp(("arbitrary",)),
    )(u, w, qg, kd, intra, egl)


def _dn_scan_bwd(do, w, qg, kd, intra, egl, vn, s0):
    cs, ns = _scan_tiles()
    g = BL * NH
    nc = T // CHUNK

    def body(do_ref, w_ref, qg_ref, kd_ref, in_ref, egl_ref, vn_ref, s0_ref,
             du_ref, dw_ref, dqg_ref, dkd_ref, din_ref, degl_ref, ds_scr):
        @pl.when(pl.program_id(0) == 0)
        def _():
            ds_scr[...] = jnp.zeros_like(ds_scr)

        for c in reversed(range(cs)):
            rows = pl.ds(c * CHUNK, CHUNK)

            def ld(r, m=DH):
                return r[:, :, rows, :].reshape(g, CHUNK, m)

            def st(r, val, m=DH):
                r[:, :, rows, :] = val.reshape(BL, NH, CHUNK, m)

            s = s0_ref[:, :, c].reshape(g, DH, DH)
            ds = ds_scr[...]
            dov, vnv, kdv, wv, qgv, inv = ld(do_ref), ld(vn_ref), ld(kd_ref), ld(w_ref), ld(qg_ref), ld(in_ref, CHUNK)
            dv = _bmm_tn_raw(inv, dov) + _bmm_raw(kdv, ds)
            st(din_ref, _bmm_nt_raw(dov, vnv), CHUNK)
            st(dqg_ref, _bmm_nt_raw(dov, s))
            st(dkd_ref, _bmm_nt_raw(vnv, ds))
            st(du_ref, dv)
            st(dw_ref, -_bmm_nt_raw(dv, s))
            de = jnp.sum(jnp.sum(ds * s, axis=2, keepdims=True), axis=1, keepdims=True)
            degl_ref[:, :, pl.ds(c * 8, 8), :] = jnp.broadcast_to(de, (g, 8, DH)).reshape(BL, NH, 8, DH)
            e = egl_ref[:, :, pl.ds(c * 8, 1), :].reshape(g, 1, DH)
            ds_scr[...] = ds * e + _bmm_tn_raw(qgv, dov) - _bmm_tn_raw(wv, dv)

    rev = lambda i: (0, 0, ns - 1 - i, 0)
    tok = pl.BlockSpec((BL, NH, cs * CHUNK, DH), rev)
    sq = pl.BlockSpec((BL, NH, cs * CHUNK, CHUNK), rev)
    per = pl.BlockSpec((BL, NH, cs * 8, DH), rev)
    hs = _sds((BL, NH, T, DH), F32)
    return _pcall(
        body, name="dn_scan_bwd", grid=(ns,),
        in_specs=[tok, tok, tok, tok, sq, per, tok,
                  pl.BlockSpec((BL, NH, cs, DH, DH), lambda i: (0, 0, ns - 1 - i, 0, 0))],
        out_specs=[tok, tok, tok, tok, sq, per],
        out_shape=[hs, hs, hs, hs, _sds((BL, NH, T, CHUNK), F32), _sds((BL, NH, nc * 8, DH), F32)],
        scratch_shapes=[pltpu.VMEM((g, DH, DH), F32)],
        compiler_params=_cp(("arbitrary",)),
    )(do, w, qg, kd, intra, egl, vn, s0)


def _gated_norm(o_h, z_h, og):
    r = lax.rsqrt(jnp.mean(o_h * o_h, axis=-1, keepdims=True) + EPS)
    return (o_h * r * og) * (z_h * jax.nn.sigmoid(z_h))


def _dn_gnorm_fwd(o, proj, o_g):
    tm = min(256, T)
    nt = T // tm

    def body(o_ref, z_ref, g_ref, og_ref):
        z = z_ref[...]
        for h in range(NH):
            og_ref[:, h * DH:(h + 1) * DH] = _gated_norm(o_ref[0, h], z[:, h * DH:(h + 1) * DH], g_ref[...]).astype(BF16)

    return _pcall(
        body, name="dn_gnorm_fwd", grid=(BL, nt),
        in_specs=[pl.BlockSpec((1, NH, tm, DH), lambda b, i: (b, 0, i, 0)),
                  pl.BlockSpec((tm, D), lambda b, i: (b * nt + i, 3)),
                  pl.BlockSpec((1, DH), lambda b, i: (0, 0))],
        out_specs=pl.BlockSpec((tm, D), lambda b, i: (b * nt + i, 0)),
        out_shape=_sds((BL * T, D), BF16),
        compiler_params=_cp(("arbitrary", "arbitrary")),
    )(o, proj, o_g)


def _dn_gnorm_bwd(dog, o, proj, o_g):
    tm = min(256, T)
    nt = T // tm

    def body(dog_ref, o_ref, z_ref, g_ref, do_ref, dz_ref, dg_ref):
        z = z_ref[...]
        dog = dog_ref[...]
        dg = jnp.zeros((1, DH), F32)
        for h in range(NH):
            cols = slice(h * DH, (h + 1) * DH)
            _, vjp = jax.vjp(_gated_norm, o_ref[0, h], z[:, cols], g_ref[...])
            do_h, dz_h, dg_h = vjp(dog[:, cols])
            do_ref[0, h] = do_h
            dz_ref[:, cols] = dz_h.astype(BF16)
            dg += dg_h
        _acc_rows(dg_ref, jnp.logical_and(pl.program_id(0) == 0, pl.program_id(1) == 0), [dg])

    return _pcall(
        body, name="dn_gnorm_bwd", grid=(BL, nt),
        in_specs=[pl.BlockSpec((tm, D), lambda b, i: (b * nt + i, 0)),
                  pl.BlockSpec((1, NH, tm, DH), lambda b, i: (b, 0, i, 0)),
                  pl.BlockSpec((tm, D), lambda b, i: (b * nt + i, 3)),
                  pl.BlockSpec((1, DH), lambda b, i: (0, 0))],
        out_specs=[pl.BlockSpec((1, NH, tm, DH), lambda b, i: (b, 0, i, 0)),
                   pl.BlockSpec((tm, D), lambda b, i: (b * nt + i, 0)),
                   pl.BlockSpec((1, DH), lambda b, i: (0, 0))],
        out_shape=[_sds((BL, NH, T, DH), F32), _sds((BL * T, D), BF16), _sds((1, DH), F32)],
        compiler_params=_cp(("arbitrary", "arbitrary")),
    )(dog, o, proj, o_g)


def _dn_prep_bwd(dq, dk, dv, dgb, dbb, conv, proj, alog, dtb):
    n = conv.shape[0]
    tt, nt = _conv_tiles()
    w3 = 3 * D

    def body(dq_ref, dk_ref, dv_ref, dgb_ref, dbb_ref, conv_ref, ab_ref, alog_ref, dtb_ref, dconv_ref, dab_ref, dhead_ref):
        cv = conv_ref[...]
        sg = jax.nn.sigmoid(cv)
        dact = sg * (1.0 + cv * (1.0 - sg))
        lane = lax.broadcasted_iota(jnp.int32, (tt, 128), 1)
        cg = jnp.zeros((tt, 128), F32)
        cb = jnp.zeros((tt, 128), F32)
        for h in range(NH):
            cols = slice(h * DH, (h + 1) * DH)
            dconv_ref[:, h * DH:(h + 1) * DH] = dq_ref[0, h] * dact[:, cols]
            dconv_ref[:, D + h * DH:D + (h + 1) * DH] = dk_ref[0, h] * dact[:, D + h * DH:D + (h + 1) * DH]
            dconv_ref[:, 2 * D + h * DH:2 * D + (h + 1) * DH] = dv_ref[0, h] * dact[:, 2 * D + h * DH:2 * D + (h + 1) * DH]
            cg = jnp.where(lane == h, jnp.sum(dgb_ref[0, h], axis=-1, keepdims=True), cg)
            cb = jnp.where(lane == NH + h, jnp.sum(dbb_ref[0, h], axis=-1, keepdims=True), cb)
        _, vjp = jax.vjp(_g_beta, ab_ref[...], alog_ref[...], dtb_ref[...])
        dab, dalog, ddtb = vjp((cg, cb))
        dab_ref[...] = dab.astype(BF16)
        _acc_rows(dhead_ref, jnp.logical_and(pl.program_id(0) == 0, pl.program_id(1) == 0), [dalog, ddtb])

    row = lambda b, i: (b * nt + i, 0)
    head = pl.BlockSpec((1, NH, tt, DH), lambda b, i: (b, 0, i, 0))
    vec = pl.BlockSpec((1, 128), lambda b, i: (0, 0))
    return _pcall(
        body, name="dn_prep_bwd", grid=(BL, nt),
        in_specs=[head] * 5 + [pl.BlockSpec((tt, w3), row),
                               pl.BlockSpec((tt, 128), lambda b, i: (b * nt + i, 4 * D // 128)), vec, vec],
        out_specs=[pl.BlockSpec((tt, w3), row), pl.BlockSpec((tt, 128), row), pl.BlockSpec((2, 128), lambda b, i: (0, 0))],
        out_shape=[_sds((n, w3), F32), _sds((n, 128), BF16), _sds((2, 128), F32)],
        compiler_params=_cp(("arbitrary", "arbitrary")),
    )(dq, dk, dv, dgb, dbb, conv, proj, alog, dtb)


def _dn_sconv_bwd(dconv, proj, w_sc):
    n = dconv.shape[0]
    tt, nt = _conv_tiles()
    w3 = 3 * D

    def body(dc_ref, nxt_ref, qkv_ref, halo_ref, w_ref, dpre_ref, dw_ref, dwin, pwin):
        b, i = pl.program_id(0), pl.program_id(1)
        dc = dc_ref[...]
        dwin[0:tt, :] = dc
        dwin[tt:tt + SHALO, :] = jnp.where(i == nt - 1, 0.0, nxt_ref[...])
        pwin[0:SHALO, :] = jnp.where(i == 0, 0.0, halo_ref[...])
        pwin[SHALO:SHALO + tt, :] = qkv_ref[...]
        dpre = jnp.zeros((tt, w3), F32)
        dws = []
        for k in range(SCW):
            dpre += w_ref[k:k + 1, :] * dwin[pl.ds(SCW - 1 - k, tt), :]
            dws.append(jnp.sum(dc * pwin[pl.ds(SHALO - (SCW - 1) + k, tt), :], axis=0, keepdims=True))
        dws += [jnp.zeros((1, w3), F32)] * (SHALO - SCW)
        dpre_ref[...] = dpre.astype(BF16)
        _acc_rows(dw_ref, jnp.logical_and(b == 0, i == 0), dws)

    row = lambda b, i: (b * nt + i, 0)
    return _pcall(
        body, name="dn_sconv_bwd", grid=(BL, nt),
        in_specs=[pl.BlockSpec((tt, w3), row), _next_halo_spec(w3, tt, SHALO),
                  pl.BlockSpec((tt, w3), row), _prev_halo_spec(w3, tt, SHALO),
                  pl.BlockSpec((SHALO, w3), lambda b, i: (0, 0))],
        out_specs=[pl.BlockSpec((tt, w3), row), pl.BlockSpec((SHALO, w3), lambda b, i: (0, 0))],
        out_shape=[_sds((n, w3), BF16), _sds((SHALO, w3), F32)],
        scratch_shapes=[pltpu.VMEM((tt + SHALO, w3), F32), pltpu.VMEM((SHALO + tt, w3), F32)],
        compiler_params=_cp(("arbitrary", "arbitrary")),
    )(dconv, dconv, proj, proj, w_sc)


def _ada_fwd(c_all, w_ada, b_cols):
    nl, _, m = w_ada.shape
    nb = c_all.shape[0]

    def body(c_ref, w_ref, b_ref, o_ref):
        cv = c_ref[...]
        cs = (cv * jax.nn.sigmoid(cv)).astype(BF16)
        o_ref[...] = _dot(cs, w_ref[...].astype(BF16)) + b_ref[...]

    return _pcall(
        body, name="ada_fwd", grid=(nl,),
        in_specs=[pl.BlockSpec((nb, D), lambda l: (0, 0)), pl.BlockSpec((None, D, m), lambda l: (l, 0, 0)),
                  pl.BlockSpec((None, 1, m), lambda l: (l, 0, 0))],
        out_specs=pl.BlockSpec((None, nb, m), lambda l: (l, 0, 0)),
        out_shape=_sds((nl, nb, m), F32),
        compiler_params=_cp(("arbitrary",)),
    )(c_all, w_ada, b_cols)


def _ada_bwd(c_all, dmod_cols):
    nl, nb, m = dmod_cols.shape

    def body(c_ref, d_ref, o_ref):
        cv = c_ref[...]
        cs = (cv * jax.nn.sigmoid(cv)).astype(BF16)
        o_ref[0] = _dot_tn(cs, d_ref[...].astype(BF16))

    return _pcall(
        body, name="ada_bwd", grid=(nl,),
        in_specs=[pl.BlockSpec((nb, D), lambda l: (0, 0)), pl.BlockSpec((None, nb, m), lambda l: (l, 0, 0))],
        out_specs=pl.BlockSpec((1, D, m), lambda l: (0, l, 0)),
        out_shape=_sds((1, nl * D, m), F32),
        compiler_params=_cp(("arbitrary",)),
    )(c_all, dmod_cols)


def _loss_head(x, tgt, fg):
    n = x.shape[0]
    tm = min(512, T)

    def f(xv, g, t):
        r = lax.rsqrt(jnp.mean(xv * xv, axis=-1, keepdims=True) + EPS)
        e = xv * r * g - t
        return 0.5 * jnp.sum(e * e, axis=0, keepdims=True) * (1.0 / D)

    def body(x_ref, t_ref, g_ref, dx_ref, st_ref):
        t = t_ref[...]
        lrow, vjp = jax.vjp(lambda xv, g: f(xv, g, t), x_ref[...], g_ref[...])
        dx, dg = vjp(jnp.ones_like(lrow))
        dx_ref[...] = dx
        _acc_rows(st_ref, pl.program_id(0) == 0, [dg, lrow])

    return _pcall(
        body, name="loss_head", grid=(n // tm,),
        in_specs=[pl.BlockSpec((tm, D), lambda i: (i, 0)), pl.BlockSpec((tm, D), lambda i: (i, 0)),
                  pl.BlockSpec((1, D), lambda i: (0, 0))],
        out_specs=[pl.BlockSpec((tm, D), lambda i: (i, 0)), pl.BlockSpec((2, D), lambda i: (0, 0))],
        out_shape=[_sds((n, D), F32), _sds((2, D), F32)],
        compiler_params=_cp(("arbitrary",)),
    )(x, tgt, fg)


def _adamw(parts, w, m, v):
    p, r, c = parts.shape
    tr = r
    for cand in (256, 128, 64, 32, 16, 8):
        if r % cand == 0:
            tr = cand
            break
    k1 = 1.0 - B1 ** STEP
    k2 = 1.0 - B2 ** STEP

    def body(p_ref, w_ref, m_ref, v_ref, g_ref, d_ref, nm_ref, nv_ref):
        g = p_ref[0].astype(F32)
        for q in range(1, p):
            g += p_ref[q].astype(F32)
        mn = B1 * m_ref[...] + (1.0 - B1) * g
        vn = B2 * v_ref[...] + (1.0 - B2) * (g * g)
        g_ref[...] = g
        nm_ref[...] = mn
        nv_ref[...] = vn
        d_ref[...] = -LR * ((mn / k1) / (jnp.sqrt(vn / k2) + AEPS) + WD * w_ref[...])

    blk = pl.BlockSpec((tr, c), lambda i: (i, 0))
    return _pcall(
        body, name="adamw", grid=(r // tr,),
        in_specs=[pl.BlockSpec((p, tr, c), lambda i: (0, i, 0)), blk, blk, blk],
        out_specs=[blk] * 4, out_shape=[_sds((r, c), F32)] * 4,
        compiler_params=_cp(("arbitrary",)),
    )(parts, w, m, v)


def _adamw_slot(parts, w, m, v, outs, row0, col):
    p, r, c = parts.shape
    tr = r
    for cand in (256, 128, 64, 32, 16, 8):
        if r % cand == 0:
            tr = cand
            break
    if r % 352 == 0:
        tr = 352
    nt = r // tr
    k1 = 1.0 - B1 ** STEP
    k2 = 1.0 - B2 ** STEP

    def body(p_ref, w_ref, m_ref, v_ref, g0, d0, m0, v0, g_ref, d_ref, nm_ref, nv_ref):
        g = p_ref[0].astype(F32)
        for q in range(1, p):
            g += p_ref[q].astype(F32)
        mn = B1 * m_ref[...] + (1.0 - B1) * g
        vn = B2 * v_ref[...] + (1.0 - B2) * (g * g)
        g_ref[...] = g
        nm_ref[...] = mn
        nv_ref[...] = vn
        d_ref[...] = -LR * ((mn / k1) / (jnp.sqrt(vn / k2) + AEPS) + WD * w_ref[...])

    blk = pl.BlockSpec((tr, c), lambda i: (row0 * nt + i, col))
    anyspec = pl.BlockSpec(memory_space=pl.ANY)
    return _pcall(
        body, name="adamw_slot", grid=(nt,),
        in_specs=[pl.BlockSpec((p, tr, c), lambda i: (0, i, 0)), blk, blk, blk] + [anyspec] * 4,
        out_specs=[blk] * 4, out_shape=[_sds(w.shape, F32)] * 4,
        input_output_aliases={4: 0, 5: 1, 6: 2, 7: 3},
        compiler_params=_cp(("arbitrary",)),
    )(parts, w, m, v, *outs)


def _pack(arrs):
    flat = jnp.concatenate([a.reshape(-1) for a in arrs])
    pad = (-flat.shape[0]) % 1024
    return jnp.pad(flat, (0, pad)).reshape(-1, 128)


def _unpack(buf, shapes):
    flat = buf.reshape(-1)
    out, off = [], 0
    for s in shapes:
        size = 1
        for d in s:
            size *= d
        out.append(flat[off:off + size].reshape(s))
        off += size
    return out


def kernel(x, c, norm_g, w_ada, b_ada, w_ffn_in, w_ffn_out, cm_w_glu, cm_b_glu, cm_w_dw, cm_b_dw, cm_ln_g, cm_ln_b, cm_w_pw, cm_b_pw, dn_w_in, dn_w_sconv, dn_a_log, dn_dt_bias, dn_o_g, dn_w_out, final_g, loss_target, m_norm_g, m_w_ada, m_b_ada, m_w_ffn_in, m_w_ffn_out, m_cm_w_glu, m_cm_b_glu, m_cm_w_dw, m_cm_b_dw, m_cm_ln_g, m_cm_ln_b, m_cm_w_pw, m_cm_b_pw, m_dn_w_in, m_dn_w_sconv, m_dn_a_log, m_dn_dt_bias, m_dn_o_g, m_dn_w_out, m_final_g, v_norm_g, v_w_ada, v_b_ada, v_w_ffn_in, v_w_ffn_out, v_cm_w_glu, v_cm_b_glu, v_cm_w_dw, v_cm_b_dw, v_cm_ln_g, v_cm_ln_b, v_cm_w_pw, v_cm_b_pw, v_dn_w_in, v_dn_w_sconv, v_dn_a_log, v_dn_dt_bias, v_dn_o_g, v_dn_w_out, v_final_g):
    me = 4 * lax.axis_index("x") + 2 * lax.axis_index("y") + lax.axis_index("c")
    n = BL * T
    nf = 4
    tf = FF // nf
    na, nb = cm_w_glu.shape[0], dn_w_in.shape[0]
    mcols = w_ada.shape[2]
    dsh = D // NDEV

    tr_ffn = lambda a: jnp.swapaxes(a, 2, 3)
    tr_dn = lambda a: jnp.transpose(a, (2, 0, 1))
    wt_ffn_in, wt_dn_in = tr_ffn(w_ffn_in), tr_dn(dn_w_in)

    def unit_weights(l, part):
        if part == 0:
            ws = (wt_ffn_in[l, 0], w_ffn_out[l, 0])
        else:
            mix = (cm_w_glu[l // 2], cm_w_pw[l // 2]) if l % 2 == 0 else (wt_dn_in[:, l // 2], dn_w_out[l // 2])
            ws = (wt_ffn_in[l, 1], w_ffn_out[l, 1], *mix)
        return [w.astype(BF16) for w in ws]

    gathers, all_started = {}, jnp.zeros((8, 128), F32)
    for l in range(DEPTH):
        for part in range(2):
            gathers[l, part], tok = _exchange_start(unit_weights(l, part), True, f"gather_start_{l}_{part}")
            all_started = all_started + tok

    small_shapes = [c.shape, norm_g.shape, cm_w_dw.shape, dn_w_sconv.shape]
    small_g, = _all_gather([_pack([c, norm_g, cm_w_dw, dn_w_sconv])], "gather_small")
    sm = [_unpack(small_g[d], small_shapes) for d in range(NDEV)]
    c_all = jnp.concatenate([s[0] for s in sm], axis=0)
    norm_g_f = jnp.concatenate([s[1] for s in sm], axis=-1)
    w_dw_f = jnp.concatenate([s[2] for s in sm], axis=-1)
    w_sc_f = jnp.concatenate([s[3] for s in sm], axis=-1)

    b_cols = lax.dynamic_slice_in_dim(b_ada, me * mcols, mcols, axis=1)[:, None, :]
    mod_cols = _ada_fwd(c_all, w_ada, b_cols)
    mod_g, = _all_gather([mod_cols], "gather_mod")
    mod_all = jnp.transpose(mod_g, (1, 2, 0, 3)).reshape(DEPTH, NDEV * BL, 9 * D)
    mod = lax.dynamic_slice_in_dim(mod_all, me * BL, BL, axis=1).reshape(DEPTH, BL, 3, 3, D)

    gathered = [None] * DEPTH

    def ffn_weights(l, s):
        return gathered[l][s].reshape(2, nf, tf, D), gathered[l][2 + s].reshape(nf, tf, D)

    xs = x.reshape(n, D)
    saved = []
    for l in range(DEPTH):
        rec = {}
        ga = _exchange_wait(gathers[l, 0], all_started if l == 0 else xs, f"gather_wait_{l}_0")
        gathered[l] = [ga[0], None, ga[1], None, None, None]
        for s, j in ((0, 0), (1, 2)):
            if j == 2:
                gb = _exchange_wait(gathers[l, 1], xs, f"gather_wait_{l}_1")
                gathered[l] = [ga[0], gb[0], ga[1], gb[1], gb[2], gb[3]]
            w_in, w_out = ffn_weights(l, s)
            ssg, ng = mod[l, :, j], norm_g_f[l, j][None]
            if j == 2:
                ssg1, ng1 = mod[l, :, 1], norm_g_f[l, 1][None]
                if l % 2 == 0:
                    a = l // 2
                    w_glu = gathered[l][4]
                    w_pw = gathered[l][5].reshape(D, D)
                    w_dw = jnp.pad(w_dw_f[a], ((0, HALO - CW), (0, 0)))
                    h1, ab = _premod_matmul(xs, ssg1, ng1, w_glu, cm_b_glu[a][None], w_glu.shape[2])
                    u1, u2 = _cm_mid_fwd(ab, w_dw, cm_b_dw[a][None], cm_ln_g[a][None], cm_ln_b[a][None])
                    xn, ymix = _matmul_res(xs, u2, ssg1, w_pw, cm_b_pw[a][None])
                    rec["mix"] = dict(x=xs, h=h1, ab=ab, u1=u1, u2=u2, y=ymix, w_glu=w_glu, w_pw=w_pw, w_dw=w_dw)
                else:
                    mi = l // 2
                    w_proj = jnp.pad(gathered[l][4].reshape(4 * D + 2 * NH, D), ((0, 128 - 2 * NH), (0, 0)))
                    w_o = gathered[l][5].reshape(D, D)
                    w_sc = jnp.pad(w_sc_f[mi], ((0, SHALO - SCW), (0, 0)))
                    alog = jnp.pad(dn_a_log[mi], (0, 128 - NH))[None]
                    dtb = jnp.pad(dn_dt_bias[mi], (0, 128 - NH))[None]
                    h1, proj = _premod_matmul(xs, ssg1, ng1, w_proj, jnp.zeros((1, w_proj.shape[0]), F32),
                                              (4 * D + 128) // 3 if (4 * D + 128) % 384 == 0 else 128)
                    conv, q, k, v, gb, bb = _dn_sconv_fwd(proj, w_sc, alog, dtb)
                    u, w, qg, kd, intra, egl = _dn_pre_fwd(q, k, v, gb, bb)
                    o, vn, s0 = _dn_scan_fwd(u, w, qg, kd, intra, egl)
                    og = _dn_gnorm_fwd(o, proj, dn_o_g[mi][None])
                    xn, ymix = _matmul_res(xs, og, ssg1, w_o, jnp.zeros((1, D), F32))
                    rec["mix"] = dict(x=xs, h=h1, proj=proj, conv=conv, q=q, k=k, v=v, gb=gb, bb=bb, w=w, qg=qg, kd=kd,
                                      intra=intra, egl=egl, o=o, vn=vn, s0=s0, og=og, y=ymix, w_proj=w_proj, w_o=w_o,
                                      w_sc=w_sc, alog=alog, dtb=dtb)
                xs = xn
            xn, gu, hid, y = _ffn_fwd(xs, ssg, ng, w_in, w_out)
            rec[s] = dict(x=xs, gu=gu, hid=hid, y=y)
            xs = xn
        saved.append(rec)

    dx, stats = _loss_head(xs, loss_target.reshape(n, D), final_g[None])
    loss = lax.psum(jnp.sum(stats[1]), AXES)
    d_final_g = stats[0]

    d_mod = [[None] * 3 for _ in range(DEPTH)]
    d_norm = [[None] * 3 for _ in range(DEPTH)]
    dw_ffn_in = [[None] * 2 for _ in range(DEPTH)]
    dw_ffn_out = [[None] * 2 for _ in range(DEPTH)]
    dcm = [dict() for _ in range(na)]
    ddn = [dict() for _ in range(nb)]
    exchanges = {}

    def gather_small_grads():
        dmod_loc = jnp.stack([jnp.stack(d_mod[l], axis=1) for l in range(DEPTH)]).reshape(DEPTH, BL, 9 * D)
        small = [jnp.sum(dmod_loc, axis=1), jnp.stack([jnp.stack(d_norm[l]) for l in range(DEPTH)]),
                 jnp.stack([d["b_glu"] for d in dcm]), jnp.stack([d["w_dw"] for d in dcm]), jnp.stack([d["b_dw"] for d in dcm]),
                 jnp.stack([d["ln_g"] for d in dcm]), jnp.stack([d["ln_b"] for d in dcm]), jnp.stack([d["b_pw"] for d in dcm]),
                 jnp.stack([d["w_sconv"] for d in ddn]), jnp.stack([d["a_log"] for d in ddn]),
                 jnp.stack([d["dt_bias"] for d in ddn]), jnp.stack([d["o_g"] for d in ddn]), d_final_g]
        dmod_g, small_parts = _all_gather([dmod_loc, _pack(small)], "gather_small_grads")
        return dmod_g, small_parts, [a.shape for a in small]

    token = jnp.zeros((), F32)
    for l in reversed(range(DEPTH)):
        rec = saved[l]
        for s, j in ((1, 2), (0, 0)):
            w_in, w_out = ffn_weights(l, s)
            ssg, ng = mod[l, :, j] + token, norm_g_f[l, j][None]
            r = rec[s]
            dx, dgu, hb, dout, dssg, dng = _ffn_bwd_a(r["x"], dx, ssg, ng, r["y"], r["gu"], w_in, w_out)
            dw_ffn_in[l][s], dw_ffn_out[l][s] = _ffn_bwd_w(hb, dgu, r["hid"], dout)
            d_mod[l][j], d_norm[l][j] = dssg, dng[0]
            if j == 2:
                ssg1, ng1 = mod[l, :, 1], norm_g_f[l, 1][None]
                r = rec["mix"]
                if l % 2 == 0:
                    a = l // 2
                    du2, dy, dgate, db_pw = _matmul_res_bwd(dx, r["y"], ssg1, r["w_pw"])
                    du1, dln = _cm_mid_bwd_a(du2, r["u1"], cm_ln_g[a][None], cm_ln_b[a][None])
                    dab, dw_dw, db_dw, db_glu = _cm_mid_bwd_b(du1, r["ab"], r["w_dw"])
                    dx, dssg, dng = _premod_matmul_bwd(r["x"], dx, ssg1, ng1, [dab], r["w_glu"])
                    dcm[a] = dict(w_glu=_wgrad_shards(r["h"], dab, NDEV), w_pw=_wgrad(r["u2"], dy).reshape(NDEV, dsh, D),
                                  b_glu=db_glu[0], w_dw=dw_dw[:CW], b_dw=db_dw[0], ln_g=dln[0], ln_b=dln[1], b_pw=db_pw[0])
                else:
                    mi = l // 2
                    dog, dy, dgate, _ = _matmul_res_bwd(dx, r["y"], ssg1, r["w_o"])
                    do, dz, d_og = _dn_gnorm_bwd(dog, r["o"], r["proj"], dn_o_g[mi][None])
                    du, dw, dqg, dkd, dintra, degl = _dn_scan_bwd(do, r["w"], r["qg"], r["kd"], r["intra"], r["egl"],
                                                                   r["vn"], r["s0"])
                    dq, dk, dv, dgb, dbb = _dn_pre_bwd(r["q"], r["k"], r["v"], r["gb"], r["bb"], du, dw, dqg, dkd, dintra, degl)
                    dconv, dab16, dhead = _dn_prep_bwd(dq, dk, dv, dgb, dbb, r["conv"], r["proj"], r["alog"], r["dtb"])
                    dpre, dw_sc = _dn_sconv_bwd(dconv, r["proj"], r["w_sc"])
                    dx, dssg, dng = _premod_matmul_bwd(r["x"], dx, ssg1, ng1, [dpre, dz, dab16], r["w_proj"])
                    dw_in = jnp.concatenate([_wgrad(dpre, r["h"]), _wgrad(dz, r["h"]),
                                             _wgrad(dab16, r["h"])[:2 * NH]], axis=0)
                    ddn[mi] = dict(w_in=dw_in.reshape(NDEV, -1, D), w_out=_wgrad(r["og"], dy).reshape(NDEV, dsh, D),
                                   w_sconv=dw_sc[:SCW], a_log=dhead[0, :NH], dt_bias=dhead[1, :NH], o_g=d_og[0])
                d_mod[l][1] = dssg.at[:, 2].set(dgate[:, 0])
                d_norm[l][1] = dng[0]
            unit = [dw_ffn_in[l][s].reshape(NDEV, tf, D), dw_ffn_out[l][s].reshape(NDEV, FF // NDEV, D)]
            if j == 2:
                g = dcm[l // 2] if l % 2 == 0 else ddn[l // 2]
                unit += [g["w_glu"], g["w_pw"]] if l % 2 == 0 else [g["w_in"], g["w_out"]]
            if l == 0 and s == 0:
                dmod_g, small_parts, full_shapes = gather_small_grads()
                unit[0], dmod_g, small_parts = lax.optimization_barrier((unit[0], dmod_g, small_parts))
                small_gathered = (dmod_g, small_parts, full_shapes)
            exchanges[l, s], token = _exchange_start(unit, False, f"grads_start_{l}_{s}")
    grad_x = dx.reshape(BL, T, D)

    dmod_g, small_parts, full_shapes = small_gathered
    dmod_all = jnp.transpose(dmod_g, (1, 0, 2, 3)).reshape(DEPTH, NDEV * BL, 9 * D)
    g_w_ada = _ada_bwd(c_all, lax.dynamic_slice_in_dim(dmod_all, me * mcols, mcols, axis=2))

    got = []
    for l in range(DEPTH):
        ea = _exchange_wait(exchanges[l, 0], dx, f"grads_wait_{l}_0") if l > 0 else [None, None]
        eb = _exchange_wait(exchanges[l, 1], dx, f"grads_wait_{l}_1")
        got.append([ea[0], eb[0], ea[1], eb[1], eb[2], eb[3]])

    names = ["b_ada", "norm_g", "cm_b_glu", "cm_w_dw", "cm_b_dw", "cm_ln_g", "cm_ln_b", "cm_b_pw",
             "dn_w_sconv", "dn_a_log", "dn_dt_bias", "dn_o_g", "final_g"]
    cols = lambda a, width: lax.dynamic_slice_in_dim(a, me * width, width, axis=a.ndim - 1)
    local = {"norm_g": lambda a: cols(a, dsh), "cm_w_dw": lambda a: cols(a, dsh), "dn_w_sconv": lambda a: cols(a, 3 * dsh)}
    parts = [_unpack(small_parts[d], full_shapes) for d in range(NDEV)]
    parts = [[local.get(nm, lambda a: a)(p) for nm, p in zip(names, ps)] for ps in parts]
    small_w = dict(b_ada=(b_ada, m_b_ada, v_b_ada), norm_g=(norm_g, m_norm_g, v_norm_g),
                   cm_b_glu=(cm_b_glu, m_cm_b_glu, v_cm_b_glu), cm_w_dw=(cm_w_dw, m_cm_w_dw, v_cm_w_dw),
                   cm_b_dw=(cm_b_dw, m_cm_b_dw, v_cm_b_dw), cm_ln_g=(cm_ln_g, m_cm_ln_g, v_cm_ln_g),
                   cm_ln_b=(cm_ln_b, m_cm_ln_b, v_cm_ln_b), cm_b_pw=(cm_b_pw, m_cm_b_pw, v_cm_b_pw),
                   dn_w_sconv=(dn_w_sconv, m_dn_w_sconv, v_dn_w_sconv), dn_a_log=(dn_a_log, m_dn_a_log, v_dn_a_log),
                   dn_dt_bias=(dn_dt_bias, m_dn_dt_bias, v_dn_dt_bias), dn_o_g=(dn_o_g, m_dn_o_g, v_dn_o_g),
                   final_g=(final_g, m_final_g, v_final_g))
    loc_shapes = [small_w[nm][0].shape for nm in names]
    packed_parts = jnp.stack([_pack(ps) for ps in parts])
    sres_raw = _adamw(packed_parts, *[_pack([small_w[nm][q] for nm in names]) for q in range(3)])
    sres = [dict(zip(names, _unpack(r, loc_shapes))) for r in sres_raw]

    res = {}

    def update(slots, wmv, view, back=None, outs=None):
        w2, m2, v2 = [view(a) for a in wmv]
        outs = [lax.empty(w2.shape, F32) for _ in range(4)] if outs is None else outs
        for p, row0, col in slots:
            outs = _adamw_slot(p, w2, m2, v2, outs, row0, col)
        return outs if back is None else [back(o) for o in outs]

    ffn_slots = [(l, s) for l in reversed(range(DEPTH)) for s in (1, 0)][:-1]
    wmv_in, view_in = (w_ffn_in, m_w_ffn_in, v_w_ffn_in), lambda a: tr_ffn(a).reshape(-1, D)
    wmv_out, view_out = (w_ffn_out, m_w_ffn_out, v_w_ffn_out), lambda a: a.reshape(-1, D)
    part_in = update([(got[l][s], 2 * l + s, 0) for l, s in ffn_slots], wmv_in, view_in)
    part_out = update([(got[l][2 + s], 2 * l + s, 0) for l, s in ffn_slots], wmv_out, view_out)
    cgl = cm_w_glu.shape[2]
    res["cm_w_glu"] = update([(got[2 * a][4], a, 0) for a in range(na)], (cm_w_glu, m_cm_w_glu, v_cm_w_glu),
                             lambda a: a.reshape(-1, cgl), lambda o: o.reshape(cm_w_glu.shape))
    res["cm_w_pw"] = update([(got[2 * a][5], a, 0) for a in range(na)], (cm_w_pw, m_cm_w_pw, v_cm_w_pw),
                            lambda a: a.reshape(-1, D), lambda o: o.reshape(cm_w_pw.shape))
    cdn = dn_w_in.shape[2]
    res["dn_w_in"] = update([(got[2 * i + 1][4], 0, i) for i in range(nb)], (dn_w_in, m_dn_w_in, v_dn_w_in),
                            lambda a: tr_dn(a).reshape(cdn, nb * D),
                            lambda o: jnp.transpose(o.reshape(cdn, nb, D), (1, 2, 0)))
    res["dn_w_out"] = update([(got[2 * i + 1][5], i, 0) for i in range(nb)], (dn_w_out, m_dn_w_out, v_dn_w_out),
                             lambda a: a.reshape(-1, D), lambda o: o.reshape(dn_w_out.shape))
    res["w_ada"] = [o.reshape(w_ada.shape) for o in
                    _adamw(g_w_ada, *[a.reshape(-1, mcols) for a in (w_ada, m_w_ada, v_w_ada)])]
    done = [part_in[0], part_out[0], sres_raw[0]] + [res[nm][0] for nm in ("cm_w_glu", "cm_w_pw", "dn_w_in", "dn_w_out", "w_ada")]
    last = _exchange_wait(exchanges[0, 0], done, "grads_wait_0_0")
    res["w_ffn_in"] = update([(last[0], 0, 0)], wmv_in, view_in,
                             lambda o: jnp.swapaxes(o.reshape(DEPTH, 2, tf, D), 2, 3), part_in)
    res["w_ffn_out"] = update([(last[1], 0, 0)], wmv_out, view_out, lambda o: o.reshape(w_ffn_out.shape), part_out)
    for nm in names:
        res[nm] = [sres[q][nm] for q in range(4)]

    order = ["norm_g", "w_ada", "b_ada", "w_ffn_in", "w_ffn_out", "cm_w_glu", "cm_b_glu", "cm_w_dw", "cm_b_dw", "cm_ln_g",
             "cm_ln_b", "cm_w_pw", "cm_b_pw", "dn_w_in", "dn_w_sconv", "dn_a_log", "dn_dt_bias", "dn_o_g", "dn_w_out", "final_g"]
    return (loss, grad_x, *[res[nm][0] for nm in order], *[res[nm][1] for nm in order],
            *[res[nm][2] for nm in order], *[res[nm][3] for nm in order])
```

```python
import functools

import jax
import jax.numpy as jnp
from jax import lax
from jax.experimental import pallas as pl
from jax.experimental.pallas import tpu as pltpu

F32 = jnp.float32
BF16 = jnp.bfloat16
HI = lax.Precision.HIGHEST
INV_PREC = lax.Precision.HIGH
MESH = pl.DeviceIdType.MESH
AXES = ("x", "y", "c")

NDEV = 8
D = 1024
T = 2048
BL = 2
FF = 2816
NH = 8
DH = 128
CW = 31
SCW = 4
CHUNK = 64
DEPTH = 4
EPS = 1e-6
LR, B1, B2, AEPS, WD, STEP = 0.001, 0.9, 0.999, 1e-08, 0.01, 10

VMEM_LIMIT_BYTES = 56 * 1024 * 1024
HALO = 32
SHALO = 8


def _pcall(body, **kw):
    return pl.pallas_call(body, **kw)


def _cp(sem=None):
    return pltpu.CompilerParams(dimension_semantics=sem, vmem_limit_bytes=VMEM_LIMIT_BYTES)


def _sds(shape, dtype):
    return jax.ShapeDtypeStruct(tuple(shape), dtype)


def _dot(a, b):
    return jnp.dot(a, b, preferred_element_type=F32)


def _dot_nt(a, b):
    return lax.dot_general(a, b, (((1,), (1,)), ((), ())), preferred_element_type=F32)


def _dot_tn(a, b):
    return lax.dot_general(a, b, (((0,), (0,)), ((), ())), preferred_element_type=F32)


def _modulate(x, ng, scale, shift):
    r = lax.rsqrt(jnp.mean(x * x, axis=-1, keepdims=True) + EPS)
    return (x * r * ng) * (1.0 + scale) + shift


def _modulate_bwd(x, ng, scale, shift, dh):
    r = lax.rsqrt(jnp.mean(x * x, axis=-1, keepdims=True) + EPS)
    xh = x * r
    xg = xh * ng
    h = xg * (1.0 + scale) + shift
    a = dh * (1.0 + scale)
    dxh = a * ng
    dx = r * (dxh - xh * jnp.mean(dxh * xh, axis=-1, keepdims=True))
    return (h, dx, jnp.sum(a * xh, axis=0, keepdims=True), jnp.sum(dh * xg, axis=0, keepdims=True),
            jnp.sum(dh, axis=0, keepdims=True))


def _acc_rows(ref, first, rows):
    @pl.when(first)
    def _():
        ref[...] = jnp.zeros_like(ref)

    for r, val in enumerate(rows):
        ref[r:r + 1, :] += val


def _my_pos():
    return lax.axis_index("x"), lax.axis_index("y"), lax.axis_index("c")


def _all_gather(arrs, name):
    n = len(arrs)

    def body(*refs):
        ins, outs = refs[:n], refs[n:2 * n]
        send, recv, loc = refs[2 * n:]
        x, y, c = _my_pos()
        me, sibling = (x, y, c), (x, y, 1 - c)
        chips = [(1 - x, y), (x, 1 - y), (1 - x, 1 - y)]

        def copy(a, k, block, to, src=None):
            dst = outs[a].at[4 * block[0] + 2 * block[1] + block[2]]
            return pltpu.make_async_remote_copy(
                src_ref=dst if src is None else src, dst_ref=dst,
                send_sem=send.at[7 * a + k], recv_sem=recv.at[7 * a + k],
                device_id=to, device_id_type=MESH)

        mine, first, passed = [], [], []
        for a in range(n):
            m = pltpu.make_async_copy(ins[a], outs[a].at[4 * x + 2 * y + c], loc.at[a])
            m.start()
            mine.append(m)
            f = [copy(a, 0, me, sibling, src=ins[a])]
            f += [copy(a, 1 + j, me, (*chip, c), src=ins[a]) for j, chip in enumerate(chips)]
            for cp in f:
                cp.start()
            first += f
        for a in range(n):
            for j, chip in enumerate(chips):
                copy(a, 1 + j, (*chip, c), me).wait_recv()
                p = copy(a, 4 + j, (*chip, c), sibling)
                p.start()
                passed.append(p)
        for a in range(n):
            copy(a, 0, sibling, me).wait_recv()
            for j, chip in enumerate(chips):
                copy(a, 4 + j, (*chip, 1 - c), me).wait_recv()
        for cp in first + passed:
            cp.wait_send()
        for m in mine:
            m.wait()

    hbm = pl.BlockSpec(memory_space=pl.ANY)
    return _pcall(
        body, name=name,
        out_shape=[_sds((NDEV,) + a.shape, a.dtype) for a in arrs],
        in_specs=[hbm] * n, out_specs=[hbm] * n,
        scratch_shapes=[pltpu.SemaphoreType.DMA((7 * n,)), pltpu.SemaphoreType.DMA((7 * n,)),
                        pltpu.SemaphoreType.DMA((n,))],
    )(*arrs)


def _peer(k):
    x, y, c = _my_pos()
    return (1 - x if k & 4 else x, 1 - y if k & 2 else y, 1 - c if k & 1 else c)


def _dev_index(p):
    return 4 * p[0] + 2 * p[1] + p[2]


_HBM = pl.BlockSpec(memory_space=pltpu.HBM)
_SEM = pl.BlockSpec(memory_space=pltpu.SEMAPHORE)
_EFFECT = pltpu.SideEffectType.DATAFLOW_SIDE_EFFECTING


def _exchange_start(srcs, gather, name):
    n = len(srcs)
    me = _dev_index(_my_pos())
    lands = []
    for s in srcs:
        own = s if gather else lax.dynamic_index_in_dim(s, me, 0, keepdims=False)
        shape = (NDEV,) + s.shape if gather else s.shape
        lands.append(lax.dynamic_update_index_in_dim(lax.empty(shape, s.dtype), own, me, 0))

    def body(*refs):
        src_refs, land_refs = refs[:n], refs[n:2 * n]
        sends, recvs = refs[2 * n:3 * n], refs[3 * n:4 * n]
        token = refs[-1]
        mine = _dev_index(_my_pos())
        for a in range(n):
            for k in range(1, 8):
                p = _peer(k)
                pltpu.make_async_remote_copy(
                    src_ref=src_refs[a] if gather else src_refs[a].at[_dev_index(p)],
                    dst_ref=land_refs[a].at[mine], send_sem=sends[a], recv_sem=recvs[a],
                    device_id=p, device_id_type=MESH).start()
        token[...] = jnp.zeros_like(token)

    out = pl.pallas_call(
        body, name=name,
        out_shape=(*[pltpu.SemaphoreType.DMA(())] * (2 * n),
                   *[pltpu.HBM(a.shape, a.dtype) for a in srcs], *[pltpu.HBM(a.shape, a.dtype) for a in lands],
                   _sds((8, 128), F32)),
        in_specs=[_HBM] * (2 * n),
        out_specs=(*[_SEM] * (2 * n), *[_HBM] * (2 * n), pl.BlockSpec(memory_space=pltpu.VMEM)),
        input_output_aliases={i: 2 * n + i for i in range(2 * n)},
        compiler_params=pltpu.CompilerParams(has_side_effects=_EFFECT),
    )(*[pltpu.with_memory_space_constraint(a, pltpu.HBM) for a in srcs],
      *[pltpu.with_memory_space_constraint(a, pltpu.HBM) for a in lands])
    state = (out[:n], out[n:2 * n], out[2 * n:3 * n], out[3 * n:4 * n])
    return state, out[-1][0, 0]


def _exchange_wait(state, after, name):
    sends, recvs, srcs, lands = state
    n = len(srcs)
    after = list(after) if isinstance(after, (list, tuple)) else [after]

    def body(*refs):
        land_refs = refs[n:2 * n]
        send_refs, recv_refs = refs[2 * n:3 * n], refs[3 * n:4 * n]
        for a in range(n):
            seven = land_refs[a].at[pl.ds(0, NDEV - 1)]
            cp = pltpu.make_async_remote_copy(src_ref=seven, dst_ref=seven, send_sem=send_refs[a], recv_sem=recv_refs[a],
                                              device_id=_peer(1), device_id_type=MESH)
            cp.wait_send()
            cp.wait_recv()

    out = pl.pallas_call(
        body, name=name,
        out_shape=(*[pltpu.HBM(a.shape, a.dtype) for a in srcs], *[pltpu.HBM(a.shape, a.dtype) for a in lands]),
        in_specs=(*[_HBM] * (2 * n), *[_SEM] * (2 * n), *[pl.BlockSpec(memory_space=pl.ANY)] * len(after)),
        out_specs=tuple([_HBM] * (2 * n)),
        input_output_aliases={i: i for i in range(2 * n)},
        compiler_params=pltpu.CompilerParams(has_side_effects=_EFFECT),
    )(*srcs, *lands, *sends, *recvs, *after)
    return list(out[n:])


def _ffn_tiles():
    tm = min(512, T)
    return tm, T // tm


def _ffn_fwd(x, ssg, ng, w_in, w_out):
    n = x.shape[0]
    _, nf, tf, _ = w_in.shape
    tm, tpb = _ffn_tiles()

    def body(x_ref, ssg_ref, ng_ref, win_ref, wout_ref, xn_ref, gu_ref, hid_ref, y_ref, h_scr, acc):
        j = pl.program_id(1)

        @pl.when(j == 0)
        def _():
            s = ssg_ref[0]
            h_scr[...] = _modulate(x_ref[...], ng_ref[...], s[1:2], s[0:1]).astype(BF16)
            acc[...] = jnp.zeros_like(acc)

        h = h_scr[...]
        g = _dot_nt(h, win_ref[0])
        u = _dot_nt(h, win_ref[1])
        gu_ref[0] = g.astype(BF16)
        gu_ref[1] = u.astype(BF16)
        hid = (g * jax.nn.sigmoid(g) * u).astype(BF16)
        hid_ref[...] = hid
        acc[...] += _dot(hid, wout_ref[...])

        @pl.when(j == nf - 1)
        def _():
            yv = acc[...]
            y_ref[...] = yv.astype(BF16)
            xn_ref[...] = x_ref[...] + (0.5 * (1.0 + ssg_ref[0][2:3])) * yv

    return _pcall(
        body, name="ffn_fwd", grid=(n // tm, nf),
        in_specs=[pl.BlockSpec((tm, D), lambda i, j: (i, 0)),
                  pl.BlockSpec((1, 3, D), lambda i, j: (i // tpb, 0, 0)),
                  pl.BlockSpec((1, D), lambda i, j: (0, 0)),
                  pl.BlockSpec((2, None, tf, D), lambda i, j: (0, j, 0, 0)),
                  pl.BlockSpec((None, tf, D), lambda i, j: (j, 0, 0))],
        out_specs=[pl.BlockSpec((tm, D), lambda i, j: (i, 0)),
                   pl.BlockSpec((2, None, tm, tf), lambda i, j: (0, j, i, 0)),
                   pl.BlockSpec((None, tm, tf), lambda i, j: (j, i, 0)),
                   pl.BlockSpec((tm, D), lambda i, j: (i, 0))],
        out_shape=[_sds((n, D), F32), _sds((2, nf, n, tf), BF16), _sds((nf, n, tf), BF16), _sds((n, D), BF16)],
        scratch_shapes=[pltpu.VMEM((tm, D), BF16), pltpu.VMEM((tm, D), F32)],
        compiler_params=_cp(("arbitrary", "arbitrary")),
    )(x, ssg, ng, w_in, w_out)


def _ffn_bwd_a(x, dxn, ssg, ng, y, gu, w_in, w_out):
    n = x.shape[0]
    _, nf, tf, _ = w_in.shape
    tm, tpb = _ffn_tiles()

    def body(x_ref, dxn_ref, ssg_ref, ng_ref, y_ref, gu_ref, win_ref, wout_ref,
             dx_ref, dgu_ref, h_ref, dout_ref, dssg_ref, dng_ref, dout_scr, dh_acc):
        i, j = pl.program_id(0), pl.program_id(1)

        @pl.when(j == 0)
        def _():
            db = ((0.5 * (1.0 + ssg_ref[0][2:3])) * dxn_ref[...]).astype(BF16)
            dout_scr[...] = db
            dout_ref[...] = db
            dh_acc[...] = jnp.zeros_like(dh_acc)

        dhid = _dot_nt(dout_scr[...], wout_ref[...]).astype(BF16)
        g = gu_ref[0]
        u = gu_ref[1]
        sig = jax.nn.sigmoid(g)
        dg = dhid * u * (sig * (1.0 + g * (1.0 - sig)))
        du = dhid * (g * sig)
        dgu_ref[0] = dg
        dgu_ref[1] = du
        dh_acc[...] += _dot(dg, win_ref[0])
        dh_acc[...] += _dot(du, win_ref[1])

        @pl.when(j == nf - 1)
        def _():
            s = ssg_ref[0]
            h, dx_, dng_, dsc_, dsh_ = _modulate_bwd(x_ref[...], ng_ref[...], s[1:2], s[0:1], dh_acc[...])
            h_ref[...] = h.astype(BF16)
            dxn = dxn_ref[...]
            dx_ref[...] = dxn + dx_
            dgate = jnp.sum(0.5 * dxn * y_ref[...].astype(F32), axis=0, keepdims=True)
            _acc_rows(dssg_ref.at[0], i % tpb == 0, [dsh_, dsc_, dgate])
            _acc_rows(dng_ref, i == 0, [dng_])

    return _pcall(
        body, name="ffn_bwd_a", grid=(n // tm, nf),
        in_specs=[pl.BlockSpec((tm, D), lambda i, j: (i, 0)),
                  pl.BlockSpec((tm, D), lambda i, j: (i, 0)),
                  pl.BlockSpec((1, 3, D), lambda i, j: (i // tpb, 0, 0)),
                  pl.BlockSpec((1, D), lambda i, j: (0, 0)),
                  pl.BlockSpec((tm, D), lambda i, j: (i, 0)),
                  pl.BlockSpec((2, None, tm, tf), lambda i, j: (0, j, i, 0)),
                  pl.BlockSpec((2, None, tf, D), lambda i, j: (0, j, 0, 0)),
                  pl.BlockSpec((None, tf, D), lambda i, j: (j, 0, 0))],
        out_specs=[pl.BlockSpec((tm, D), lambda i, j: (i, 0)),
                   pl.BlockSpec((2, None, tm, tf), lambda i, j: (0, j, i, 0)),
                   pl.BlockSpec((tm, D), lambda i, j: (i, 0)),
                   pl.BlockSpec((tm, D), lambda i, j: (i, 0)),
                   pl.BlockSpec((1, 3, D), lambda i, j: (i // tpb, 0, 0)),
                   pl.BlockSpec((1, D), lambda i, j: (0, 0))],
        out_shape=[_sds((n, D), F32), _sds((2, nf, n, tf), BF16), _sds((n, D), BF16), _sds((n, D), BF16),
                   _sds((BL, 3, D), F32), _sds((1, D), F32)],
        scratch_shapes=[pltpu.VMEM((tm, D), BF16), pltpu.VMEM((tm, D), F32)],
        compiler_params=_cp(("arbitrary", "arbitrary")),
    )(x, dxn, ssg, ng, y, gu, w_in, w_out)


def _ffn_bwd_w(h, dgu, hid, dout):
    n = h.shape[0]
    _, nf, _, tf = dgu.shape
    tm, _ = _ffn_tiles()
    ni = n // tm

    def body(h_ref, dgu_ref, hid_ref, dout_ref, dwin_ref, dwout_ref, acc_g, acc_u, acc_o):
        i = pl.program_id(1)

        @pl.when(i == 0)
        def _():
            acc_g[...] = jnp.zeros_like(acc_g)
            acc_u[...] = jnp.zeros_like(acc_u)
            acc_o[...] = jnp.zeros_like(acc_o)

        hv = h_ref[...]
        acc_g[...] += _dot_tn(dgu_ref[0], hv)
        acc_u[...] += _dot_tn(dgu_ref[1], hv)
        acc_o[...] += _dot_tn(hid_ref[...], dout_ref[...])

        @pl.when(i == ni - 1)
        def _():
            dwin_ref[0] = acc_g[...].astype(BF16)
            dwin_ref[1] = acc_u[...].astype(BF16)
            dwout_ref[...] = acc_o[...].astype(BF16)

    return _pcall(
        body, name="ffn_bwd_w", grid=(nf, ni),
        in_specs=[pl.BlockSpec((tm, D), lambda j, i: (i, 0)),
                  pl.BlockSpec((2, None, tm, tf), lambda j, i: (0, j, i, 0)),
                  pl.BlockSpec((None, tm, tf), lambda j, i: (j, i, 0)),
                  pl.BlockSpec((tm, D), lambda j, i: (i, 0))],
        out_specs=[pl.BlockSpec((2, None, tf, D), lambda j, i: (0, j, 0, 0)),
                   pl.BlockSpec((None, tf, D), lambda j, i: (j, 0, 0))],
        out_shape=[_sds((2, nf, tf, D), BF16), _sds((nf, tf, D), BF16)],
        scratch_shapes=[pltpu.VMEM((tf, D), F32), pltpu.VMEM((tf, D), F32), pltpu.VMEM((tf, D), F32)],
        compiler_params=_cp(("arbitrary", "arbitrary")),
    )(h, dgu, hid, dout)


def _premod_matmul(x, ssg, ng, w, bias, tn):
    n = x.shape[0]
    shards = w.ndim == 3
    m = w.shape[0] * w.shape[2] if shards else w.shape[0]
    tm = min(512, T)
    tpb = T // tm
    to = m if shards else tn
    w_spec = (pl.BlockSpec(w.shape, lambda i, j: (0, 0, 0)) if shards
              else pl.BlockSpec((tn, D), lambda i, j: (j, 0)))

    def body(x_ref, ssg_ref, ng_ref, w_ref, b_ref, h_ref, o_ref, h_scr):
        @pl.when(pl.program_id(1) == 0)
        def _():
            s = ssg_ref[0]
            hb = _modulate(x_ref[...], ng_ref[...], s[1:2], s[0:1]).astype(BF16)
            h_scr[...] = hb
            h_ref[...] = hb

        hv = h_scr[...]
        if shards:
            for q in range(w.shape[0]):
                cols = slice(q * tn, (q + 1) * tn)
                o_ref[:, cols] = _dot(hv, w_ref[q]) + b_ref[:, cols]
        else:
            o_ref[...] = _dot_nt(hv, w_ref[...]) + b_ref[...]

    return _pcall(
        body, name="premod_matmul", grid=(n // tm, m // to),
        in_specs=[pl.BlockSpec((tm, D), lambda i, j: (i, 0)),
                  pl.BlockSpec((1, 3, D), lambda i, j: (i // tpb, 0, 0)),
                  pl.BlockSpec((1, D), lambda i, j: (0, 0)),
                  w_spec,
                  pl.BlockSpec((1, to), lambda i, j: (0, j))],
        out_specs=[pl.BlockSpec((tm, D), lambda i, j: (i, 0)),
                   pl.BlockSpec((tm, to), lambda i, j: (i, j))],
        out_shape=[_sds((n, D), BF16), _sds((n, m), F32)],
        scratch_shapes=[pltpu.VMEM((tm, D), BF16)],
        compiler_params=_cp(("arbitrary", "arbitrary")),
    )(x, ssg, ng, w, bias)


def _premod_matmul_bwd(x, dxn, ssg, ng, douts, w):
    n = x.shape[0]
    k = len(douts)
    shards = w.ndim == 3
    tm = min(256, T)
    tpb = T // tm

    def body(*refs):
        x_ref, dxn_ref, ssg_ref, ng_ref = refs[:4]
        do_refs, w_ref = refs[4:4 + k], refs[4 + k]
        dx_ref, dssg_ref, dng_ref = refs[5 + k:]
        i = pl.program_id(0)
        dh = jnp.zeros((tm, D), F32)
        if shards:
            cs = w.shape[2]
            dov = do_refs[0][...]
            for j in range(w.shape[0]):
                dh += _dot_nt(dov[:, j * cs:(j + 1) * cs], w_ref[j])
        else:
            off = 0
            for q in range(k):
                mk = douts[q].shape[1]
                dh += _dot(do_refs[q][...], w_ref[off:off + mk, :])
                off += mk
        s = ssg_ref[0]
        _, dx_, dng_, dsc_, dsh_ = _modulate_bwd(x_ref[...], ng_ref[...], s[1:2], s[0:1], dh)
        dx_ref[...] = dxn_ref[...] + dx_
        _acc_rows(dssg_ref.at[0], i % tpb == 0, [dsh_, dsc_, jnp.zeros_like(dsh_)])
        _acc_rows(dng_ref, i == 0, [dng_])

    return _pcall(
        body, name="premod_matmul_bwd", grid=(n // tm,),
        in_specs=[pl.BlockSpec((tm, D), lambda i: (i, 0)),
                  pl.BlockSpec((tm, D), lambda i: (i, 0)),
                  pl.BlockSpec((1, 3, D), lambda i: (i // tpb, 0, 0)),
                  pl.BlockSpec((1, D), lambda i: (0, 0))]
                 + [pl.BlockSpec((tm, a.shape[1]), lambda i: (i, 0)) for a in douts]
                 + [pl.BlockSpec(w.shape, (lambda i: (0, 0, 0)) if shards else (lambda i: (0, 0)))],
        out_specs=[pl.BlockSpec((tm, D), lambda i: (i, 0)),
                   pl.BlockSpec((1, 3, D), lambda i: (i // tpb, 0, 0)),
                   pl.BlockSpec((1, D), lambda i: (0, 0))],
        out_shape=[_sds((n, D), F32), _sds((BL, 3, D), F32), _sds((1, D), F32)],
        compiler_params=_cp(("arbitrary",)),
    )(x, dxn, ssg, ng, *douts, w)


def _matmul_res(x, a, ssg, w, bias):
    n, kd = a.shape
    tm = min(512, T)
    tpb = T // tm

    def body(x_ref, a_ref, ssg_ref, w_ref, b_ref, xn_ref, y_ref):
        yv = _dot(a_ref[...], w_ref[...]) + b_ref[...]
        y_ref[...] = yv.astype(BF16)
        xn_ref[...] = x_ref[...] + (1.0 + ssg_ref[0][2:3]) * yv

    return _pcall(
        body, name="matmul_res", grid=(n // tm,),
        in_specs=[pl.BlockSpec((tm, D), lambda i: (i, 0)),
                  pl.BlockSpec((tm, kd), lambda i: (i, 0)),
                  pl.BlockSpec((1, 3, D), lambda i: (i // tpb, 0, 0)),
                  pl.BlockSpec((kd, D), lambda i: (0, 0)),
                  pl.BlockSpec((1, D), lambda i: (0, 0))],
        out_specs=[pl.BlockSpec((tm, D), lambda i: (i, 0)), pl.BlockSpec((tm, D), lambda i: (i, 0))],
        out_shape=[_sds((n, D), F32), _sds((n, D), BF16)],
        compiler_params=_cp(("arbitrary",)),
    )(x, a, ssg, w, bias)


def _matmul_res_bwd(dxn, y, ssg, w):
    n = dxn.shape[0]
    kd = w.shape[0]
    tm = min(512, T)
    tpb = T // tm

    def body(dxn_ref, y_ref, ssg_ref, w_ref, da_ref, dy_ref, dgate_ref, dbias_ref):
        i = pl.program_id(0)
        dxn = dxn_ref[...]
        dy = (1.0 + ssg_ref[0][2:3]) * dxn
        dyb = dy.astype(BF16)
        dy_ref[...] = dyb
        da_ref[...] = _dot_nt(dyb, w_ref[...])
        _acc_rows(dgate_ref.at[0], i % tpb == 0, [jnp.sum(dxn * y_ref[...].astype(F32), axis=0, keepdims=True)])
        _acc_rows(dbias_ref, i == 0, [jnp.sum(dy, axis=0, keepdims=True)])

    return _pcall(
        body, name="matmul_res_bwd", grid=(n // tm,),
        in_specs=[pl.BlockSpec((tm, D), lambda i: (i, 0)),
                  pl.BlockSpec((tm, D), lambda i: (i, 0)),
                  pl.BlockSpec((1, 3, D), lambda i: (i // tpb, 0, 0)),
                  pl.BlockSpec((kd, D), lambda i: (0, 0))],
        out_specs=[pl.BlockSpec((tm, kd), lambda i: (i, 0)),
                   pl.BlockSpec((tm, D), lambda i: (i, 0)),
                   pl.BlockSpec((1, 1, D), lambda i: (i // tpb, 0, 0)),
                   pl.BlockSpec((1, D), lambda i: (0, 0))],
        out_shape=[_sds((n, kd), F32), _sds((n, D), BF16), _sds((BL, 1, D), F32), _sds((1, D), F32)],
        compiler_params=_cp(("arbitrary",)),
    )(dxn, y, ssg, w)


def _wgrad_shards(a, b, ns):
    n, kd = a.shape
    cs = b.shape[1] // ns
    tm = min(512, T)
    ni = n // tm

    def body(a_ref, b_ref, o_ref, acc):
        i = pl.program_id(0)

        @pl.when(i == 0)
        def _():
            acc[...] = jnp.zeros_like(acc)

        at = a_ref[...].T
        for q in range(ns):
            acc[q] += _dot(at, b_ref[:, q * cs:(q + 1) * cs])

        @pl.when(i == ni - 1)
        def _():
            o_ref[...] = acc[...].astype(BF16)

    return _pcall(
        body, name="wgrad_shards", grid=(ni,),
        in_specs=[pl.BlockSpec((tm, kd), lambda i: (i, 0)), pl.BlockSpec((tm, ns * cs), lambda i: (i, 0))],
        out_specs=pl.BlockSpec((ns, kd, cs), lambda i: (0, 0, 0)),
        out_shape=_sds((ns, kd, cs), BF16),
        scratch_shapes=[pltpu.VMEM((ns, kd, cs), F32)],
        compiler_params=_cp(("arbitrary",)),
    )(a, b)


def _wgrad(a, b):
    n, kd = a.shape
    m = b.shape[1]
    tm = min(512, T)
    tk = min(512, kd)
    ni = n // tm

    def body(a_ref, b_ref, o_ref, acc):
        i = pl.program_id(1)

        @pl.when(i == 0)
        def _():
            acc[...] = jnp.zeros_like(acc)

        acc[...] += _dot_tn(a_ref[...], b_ref[...])

        @pl.when(i == ni - 1)
        def _():
            o_ref[...] = acc[...].astype(BF16)

    return _pcall(
        body, name="wgrad", grid=(kd // tk, ni),
        in_specs=[pl.BlockSpec((tm, tk), lambda q, i: (i, q)), pl.BlockSpec((tm, m), lambda q, i: (i, 0))],
        out_specs=pl.BlockSpec((tk, m), lambda q, i: (q, 0)),
        out_shape=_sds((kd, m), BF16),
        scratch_shapes=[pltpu.VMEM((tk, m), F32)],
        compiler_params=_cp(("arbitrary", "arbitrary")),
    )(a, b)


def _ln_silu(u1, g, b):
    mu = jnp.mean(u1, axis=-1, keepdims=True)
    xc = u1 - mu
    var = jnp.mean(xc * xc, axis=-1, keepdims=True)
    ln = xc * lax.rsqrt(var + EPS) * g + b
    return ln * jax.nn.sigmoid(ln)


def _conv_tiles():
    tt = min(256, T)
    return tt, T // tt


def _prev_halo_spec(cols, tt, halo):
    r = tt // halo
    return pl.BlockSpec((halo, cols), lambda b, i: (jnp.maximum(b * (T // halo) + i * r - 1, 0), 0))


def _next_halo_spec(cols, tt, halo):
    r = tt // halo
    last = BL * T // halo - 1
    return pl.BlockSpec((halo, cols), lambda b, i: (jnp.minimum(b * (T // halo) + (i + 1) * r, last), 0))


ROWS = 32
SROWS = 8


def _fill_rotations(rot, win, rows):
    for r in range(8):
        rot[r, 0:rows, :] = win[pl.ds(r, rows), :]


def _window(rot, off, start, size):
    return rot[off % 8, pl.ds(pl.multiple_of(start + (off // 8) * 8, 8), size), :]


def _cm_mid_fwd(ab, w_dw, b_dw, ln_g, ln_b):
    n = ab.shape[0]
    tt, nt = _conv_tiles()

    def body(ab_ref, halo_ref, w_ref, bdw_ref, g_ref, b_ref, u1_ref, u2_ref, win, rot):
        i = pl.program_id(1)
        hv = halo_ref[...]
        u0h = hv[:, :D] * jax.nn.sigmoid(hv[:, D:])
        win[0:HALO, :] = jnp.where(i == 0, 0.0, u0h)
        cv = ab_ref[...]
        win[HALO:HALO + tt, :] = cv[:, :D] * jax.nn.sigmoid(cv[:, D:])
        win[HALO + tt:, :] = jnp.zeros((8, D), F32)
        _fill_rotations(rot, win, tt + HALO)

        def chunk(c, carry):
            r0 = pl.multiple_of(c * ROWS, ROWS)
            acc = jnp.zeros((ROWS, D), F32) + bdw_ref[...]
            for k in range(CW):
                acc += w_ref[k:k + 1, :] * _window(rot, HALO - (CW - 1) + k, r0, ROWS)
            u1_ref[pl.ds(r0, ROWS), :] = acc
            u2_ref[pl.ds(r0, ROWS), :] = _ln_silu(acc, g_ref[...], b_ref[...]).astype(BF16)
            return carry

        lax.fori_loop(0, tt // ROWS, chunk, 0)

    row = lambda b, i: (b * nt + i, 0)
    vec = pl.BlockSpec((1, D), lambda b, i: (0, 0))
    return _pcall(
        body, name="cm_mid_fwd", grid=(BL, nt),
        in_specs=[pl.BlockSpec((tt, 2 * D), row), _prev_halo_spec(2 * D, tt, HALO),
                  pl.BlockSpec((HALO, D), lambda b, i: (0, 0)), vec, vec, vec],
        out_specs=[pl.BlockSpec((tt, D), row), pl.BlockSpec((tt, D), row)],
        out_shape=[_sds((n, D), F32), _sds((n, D), BF16)],
        scratch_shapes=[pltpu.VMEM((HALO + tt + 8, D), F32), pltpu.VMEM((8, tt + HALO, D), F32)],
        compiler_params=_cp(("arbitrary", "arbitrary")),
    )(ab, ab, w_dw, b_dw, ln_g, ln_b)


def _cm_mid_bwd_a(du2, u1, ln_g, ln_b):
    n = du2.shape[0]
    tm = min(256, T)

    def body(du2_ref, u1_ref, g_ref, b_ref, du1_ref, dln_ref):
        _, vjp = jax.vjp(_ln_silu, u1_ref[...], g_ref[...], b_ref[...])
        du1, dg, db = vjp(du2_ref[...])
        du1_ref[...] = du1
        _acc_rows(dln_ref, pl.program_id(0) == 0, [dg, db])

    vec = pl.BlockSpec((1, D), lambda i: (0, 0))
    return _pcall(
        body, name="cm_mid_bwd_a", grid=(n // tm,),
        in_specs=[pl.BlockSpec((tm, D), lambda i: (i, 0)), pl.BlockSpec((tm, D), lambda i: (i, 0)), vec, vec],
        out_specs=[pl.BlockSpec((tm, D), lambda i: (i, 0)), pl.BlockSpec((2, D), lambda i: (0, 0))],
        out_shape=[_sds((n, D), F32), _sds((2, D), F32)],
        compiler_params=_cp(("arbitrary",)),
    )(du2, u1, ln_g, ln_b)


def _cm_mid_bwd_b(du1, ab, w_dw):
    n = du1.shape[0]
    tt, nt = _conv_tiles()

    def body(du1_ref, nxt_ref, ab_ref, halo_ref, w_ref, dab_ref, dw_ref, dbdw_ref, dbglu_ref,
             dwin, uwin, rotd, rotu, accw, accv):
        b, i = pl.program_id(0), pl.program_id(1)
        first = jnp.logical_and(b == 0, i == 0)
        dwin[0:tt, :] = du1_ref[...]
        dwin[tt:tt + HALO, :] = jnp.where(i == nt - 1, 0.0, nxt_ref[...])
        dwin[tt + HALO:, :] = jnp.zeros((8, D), F32)
        hv = halo_ref[...]
        uwin[0:HALO, :] = jnp.where(i == 0, 0.0, hv[:, :D] * jax.nn.sigmoid(hv[:, D:]))
        cv = ab_ref[...]
        uwin[HALO:HALO + tt, :] = cv[:, :D] * jax.nn.sigmoid(cv[:, D:])
        uwin[HALO + tt:, :] = jnp.zeros((8, D), F32)
        _fill_rotations(rotd, dwin, tt + HALO)
        _fill_rotations(rotu, uwin, tt + HALO)
        accw[...] = jnp.zeros_like(accw)
        accv[...] = jnp.zeros_like(accv)

        def fold(v):
            return jnp.sum(v.reshape(ROWS // 8, 8, D), axis=0)

        def chunk(c, carry):
            r0 = pl.multiple_of(c * ROWS, ROWS)
            d1 = du1_ref[pl.ds(r0, ROWS), :]
            du0 = jnp.zeros((ROWS, D), F32)
            for k in range(CW):
                du0 += w_ref[k:k + 1, :] * _window(rotd, CW - 1 - k, r0, ROWS)
                accw[k] += fold(d1 * _window(rotu, HALO - (CW - 1) + k, r0, ROWS))
            cvc = ab_ref[pl.ds(r0, ROWS), :]
            av, sg = cvc[:, :D], jax.nn.sigmoid(cvc[:, D:])
            da = du0 * sg
            db = du0 * av * sg * (1.0 - sg)
            dab_ref[pl.ds(r0, ROWS), 0:D] = da.astype(BF16)
            dab_ref[pl.ds(r0, ROWS), D:2 * D] = db.astype(BF16)
            accv[0] += fold(d1)
            accv[1] += fold(da)
            accv[2] += fold(db)
            return carry

        lax.fori_loop(0, tt // ROWS, chunk, 0)
        dws = [jnp.sum(accw[k], axis=0, keepdims=True) for k in range(CW)]
        dws += [jnp.zeros((1, D), F32)] * (HALO - CW)
        _acc_rows(dw_ref, first, dws)
        _acc_rows(dbdw_ref, first, [jnp.sum(accv[0], axis=0, keepdims=True)])
        _acc_rows(dbglu_ref.at[:, 0:D], first, [jnp.sum(accv[1], axis=0, keepdims=True)])
        _acc_rows(dbglu_ref.at[:, D:2 * D], first, [jnp.sum(accv[2], axis=0, keepdims=True)])

    row = lambda b, i: (b * nt + i, 0)
    return _pcall(
        body, name="cm_mid_bwd_b", grid=(BL, nt),
        in_specs=[pl.BlockSpec((tt, D), row), _next_halo_spec(D, tt, HALO),
                  pl.BlockSpec((tt, 2 * D), row), _prev_halo_spec(2 * D, tt, HALO),
                  pl.BlockSpec((HALO, D), lambda b, i: (0, 0))],
        out_specs=[pl.BlockSpec((tt, 2 * D), row), pl.BlockSpec((HALO, D), lambda b, i: (0, 0)),
                   pl.BlockSpec((1, D), lambda b, i: (0, 0)), pl.BlockSpec((1, 2 * D), lambda b, i: (0, 0))],
        out_shape=[_sds((n, 2 * D), BF16), _sds((HALO, D), F32), _sds((1, D), F32), _sds((1, 2 * D), F32)],
        scratch_shapes=[pltpu.VMEM((tt + HALO + 8, D), F32), pltpu.VMEM((HALO + tt + 8, D), F32),
                        pltpu.VMEM((8, tt + HALO, D), F32), pltpu.VMEM((8, tt + HALO, D), F32),
                        pltpu.VMEM((HALO, 8, D), F32), pltpu.VMEM((3, 8, D), F32)],
        compiler_params=_cp(("arbitrary", "arbitrary")),
    )(du1, du1, ab, ab, w_dw)


def _softplus(v):
    return jnp.maximum(v, 0.0) + jnp.log(1.0 + jnp.exp(-jnp.abs(v)))


def _g_beta(ab, alog, dtb):
    return -jnp.exp(alog) * _softplus(ab + dtb), jax.nn.sigmoid(ab)


def _dn_sconv_fwd(proj, w_sc, alog, dtb):
    n = proj.shape[0]
    tt, nt = _conv_tiles()
    w3 = 3 * D

    def body(qkv_ref, halo_ref, ab_ref, w_ref, alog_ref, dtb_ref, conv_ref, q_ref, k_ref, v_ref, gb_ref, bb_ref,
             win, rot, gsc, bsc):
        i = pl.program_id(1)
        win[0:SHALO, :] = jnp.where(i == 0, 0.0, halo_ref[...])
        win[SHALO:SHALO + tt, :] = qkv_ref[...]
        for k in range(SCW - 1):
            rot[k] = win[pl.ds(SHALO - (SCW - 1) + k, tt), :]
        gsc[...], bsc[...] = _g_beta(ab_ref[...], alog_ref[...], dtb_ref[...])

        def chunk(c, carry):
            rows = pl.ds(pl.multiple_of(c * SROWS, SROWS), SROWS)
            acc = w_ref[SCW - 1:SCW, :] * win[pl.ds(pl.multiple_of(c * SROWS + SHALO, SROWS), SROWS), :]
            for k in range(SCW - 1):
                acc += w_ref[k:k + 1, :] * rot[k, rows, :]
            conv_ref[rows, :] = acc
            act = acc * jax.nn.sigmoid(acc)
            gfull, bfull = gsc[rows, :], bsc[rows, :]
            for h in range(NH):
                q_ref[0, h, rows, :] = act[:, h * DH:(h + 1) * DH]
                k_ref[0, h, rows, :] = act[:, D + h * DH:D + (h + 1) * DH]
                v_ref[0, h, rows, :] = act[:, 2 * D + h * DH:2 * D + (h + 1) * DH]
                gb_ref[0, h, rows, :] = jnp.broadcast_to(gfull[:, h:h + 1], (SROWS, DH))
                bb_ref[0, h, rows, :] = jnp.broadcast_to(bfull[:, NH + h:NH + h + 1], (SROWS, DH))
            return carry

        lax.fori_loop(0, tt // SROWS, chunk, 0)

    row = lambda b, i: (b * nt + i, 0)
    head = pl.BlockSpec((1, NH, tt, DH), lambda b, i: (b, 0, i, 0))
    vec = pl.BlockSpec((1, 128), lambda b, i: (0, 0))
    hs = _sds((BL, NH, T, DH), F32)
    return _pcall(
        body, name="dn_sconv_fwd", grid=(BL, nt),
        in_specs=[pl.BlockSpec((tt, w3), row), _prev_halo_spec(w3, tt, SHALO),
                  pl.BlockSpec((tt, 128), lambda b, i: (b * nt + i, 4 * D // 128)),
                  pl.BlockSpec((SHALO, w3), lambda b, i: (0, 0)), vec, vec],
        out_specs=[pl.BlockSpec((tt, w3), row), head, head, head, head, head],
        out_shape=[_sds((n, w3), F32), hs, hs, hs, hs, hs],
        scratch_shapes=[pltpu.VMEM((SHALO + tt, w3), F32), pltpu.VMEM((SCW - 1, tt, w3), F32),
                        pltpu.VMEM((tt, 128), F32), pltpu.VMEM((tt, 128), F32)],
        compiler_params=_cp(("arbitrary", "arbitrary")),
    )(proj, proj, proj, w_sc, alog, dtb)


_BMM_SPEC = {"nn": "gij,gjk->gik", "nt": "gid,gjd->gij", "tn": "gcd,gce->gde"}


def _mm(kind, a, b, prec):
    if prec is None:
        return jnp.einsum(_BMM_SPEC[kind], a.astype(BF16), b.astype(BF16), preferred_element_type=F32)
    return jnp.einsum(_BMM_SPEC[kind], a, b, preferred_element_type=F32, precision=prec)


@functools.partial(jax.custom_vjp, nondiff_argnums=(0, 3))
def _bmm_k(kind, a, b, prec):
    return _mm(kind, a, b, prec)


def _bmm_k_fwd(kind, a, b, prec):
    return _mm(kind, a, b, prec), (a, b)


def _bmm_k_bwd(kind, prec, res, dc):
    a, b = res
    if kind == "nn":
        return _bmm_k("nt", dc, b, prec), _bmm_k("tn", a, dc, prec)
    if kind == "nt":
        return _bmm_k("nn", dc, b, prec), _bmm_k("tn", dc, a, prec)
    return _bmm_k("nt", b, dc, prec), _bmm_k("nn", a, dc, prec)


_bmm_k.defvjp(_bmm_k_fwd, _bmm_k_bwd)


def _bmm(a, b, prec=None):
    return _bmm_k("nn", a, b, prec)


def _bmm_nt(a, b, prec=None):
    return _bmm_k("nt", a, b, prec)


def _bmm_tn(a, b, prec=None):
    return _bmm_k("tn", a, b, prec)


def _bmm_raw(a, b):
    return _mm("nn", a, b, None)


def _bmm_nt_raw(a, b):
    return _mm("nt", a, b, None)


def _bmm_tn_raw(a, b):
    return _mm("tn", a, b, None)


@jax.custom_vjp
def _unit_lower_inverse(a):
    eye = (lax.broadcasted_iota(jnp.int32, a.shape, 1) == lax.broadcasted_iota(jnp.int32, a.shape, 2)).astype(F32)
    t = eye - a
    p = a
    for _ in range(CHUNK.bit_length() - 2):
        p = _mm("nn", p, p, INV_PREC)
        t = _mm("nn", t, eye + p, INV_PREC)
    return t


def _uli_fwd(a):
    t = _unit_lower_inverse(a)
    return t, t


def _uli_bwd(t, dt):
    return (-_bmm_nt(_bmm_tn(t, dt, INV_PREC), t, INV_PREC),)


_unit_lower_inverse.defvjp(_uli_fwd, _uli_bwd)


def _dn_pre(q, k, v, gb, bb):
    shape = (q.shape[0], CHUNK, CHUNK)
    ri = lax.broadcasted_iota(jnp.int32, shape, 1)
    ci = lax.broadcasted_iota(jnp.int32, shape, 2)
    causal, strict = ri >= ci, ri > ci
    qn = q * lax.rsqrt(jnp.sum(q * q, axis=-1, keepdims=True) + EPS) * (DH ** -0.5)
    kn = k * lax.rsqrt(jnp.sum(k * k, axis=-1, keepdims=True) + EPS)
    gcs = _bmm(causal.astype(F32), gb, HI)
    gcol = gcs[:, :, :CHUNK]
    decay = jnp.exp(jnp.where(causal, gcol - jnp.swapaxes(gcol, 1, 2), -jnp.inf))
    eg = jnp.exp(gcs)
    kb = kn * bb
    a = jnp.where(strict, _bmm_nt(kb, kn) * decay, 0.0)
    tm = _unit_lower_inverse(a)
    u = _bmm(tm, v * bb)
    w = _bmm(tm, kb * eg)
    qg = qn * eg
    intra = _bmm_nt(qn, kn) * decay
    glast = gcs[:, CHUNK - 1:CHUNK, :]
    kd = kn * jnp.exp(glast - gcs)
    egl = jnp.broadcast_to(jnp.exp(glast), (q.shape[0], 8, DH))
    return u, w, qg, kd, intra, egl


def _pre_tiles():
    gcn = min(16, T // CHUNK)
    return gcn, T // (CHUNK * gcn)


def _dn_pre_specs():
    gcn, _ = _pre_tiles()
    tok = pl.BlockSpec((None, None, gcn * CHUNK, DH), lambda b, h, i: (b, h, i, 0))
    sq = pl.BlockSpec((None, None, gcn * CHUNK, CHUNK), lambda b, h, i: (b, h, i, 0))
    per = pl.BlockSpec((None, None, gcn * 8, DH), lambda b, h, i: (b, h, i, 0))
    return tok, sq, per


def _dn_pre_fwd(q, k, v, gb, bb):
    gcn, ng = _pre_tiles()
    tok, sq, per = _dn_pre_specs()

    def body(q_ref, k_ref, v_ref, gb_ref, bb_ref, u_ref, w_ref, qg_ref, kd_ref, in_ref, egl_ref):
        args = [r[...].reshape(gcn, CHUNK, DH) for r in (q_ref, k_ref, v_ref, gb_ref, bb_ref)]
        u, w, qg, kd, intra, egl = _dn_pre(*args)
        for r, val in ((u_ref, u), (w_ref, w), (qg_ref, qg), (kd_ref, kd)):
            r[...] = val.reshape(gcn * CHUNK, DH)
        in_ref[...] = intra.reshape(gcn * CHUNK, CHUNK)
        egl_ref[...] = egl.reshape(gcn * 8, DH)

    hs = _sds((BL, NH, T, DH), F32)
    return _pcall(
        body, name="dn_pre_fwd", grid=(BL, NH, ng),
        in_specs=[tok] * 5, out_specs=[tok, tok, tok, tok, sq, per],
        out_shape=[hs, hs, hs, hs, _sds((BL, NH, T, CHUNK), F32), _sds((BL, NH, T // CHUNK * 8, DH), F32)],
        compiler_params=_cp(("arbitrary",) * 3),
    )(q, k, v, gb, bb)


def _dn_pre_bwd(q, k, v, gb, bb, du, dw, dqg, dkd, dintra, degl):
    gcn, ng = _pre_tiles()
    tok, sq, per = _dn_pre_specs()

    def body(q_ref, k_ref, v_ref, gb_ref, bb_ref, du_ref, dw_ref, dqg_ref, dkd_ref, din_ref, degl_ref,
             dq_ref, dk_ref, dv_ref, dgb_ref, dbb_ref):
        args = [r[...].reshape(gcn, CHUNK, DH) for r in (q_ref, k_ref, v_ref, gb_ref, bb_ref)]
        _, vjp = jax.vjp(_dn_pre, *args)
        cts = [r[...].reshape(gcn, CHUNK, DH) for r in (du_ref, dw_ref, dqg_ref, dkd_ref)]
        de = degl_ref[...].reshape(gcn, 8, DH)
        one = jnp.logical_and(lax.broadcasted_iota(jnp.int32, de.shape, 1) == 0,
                              lax.broadcasted_iota(jnp.int32, de.shape, 2) == 0)
        outs = vjp((*cts, din_ref[...].reshape(gcn, CHUNK, CHUNK), jnp.where(one, de, 0.0)))
        for r, val in zip((dq_ref, dk_ref, dv_ref, dgb_ref, dbb_ref), outs):
            r[...] = val.reshape(gcn * CHUNK, DH)

    hs = _sds((BL, NH, T, DH), F32)
    return _pcall(
        body, name="dn_pre_bwd", grid=(BL, NH, ng),
        in_specs=[tok] * 9 + [sq, per], out_specs=[tok] * 5, out_shape=[hs] * 5,
        compiler_params=_cp(("arbitrary",) * 3),
    )(q, k, v, gb, bb, du, dw, dqg, dkd, dintra, degl)


def _scan_tiles():
    cs = min(2, T // CHUNK)
    return cs, T // (CHUNK * cs)


def _dn_scan_fwd(u, w, qg, kd, intra, egl):
    cs, ns = _scan_tiles()
    g = BL * NH
    nc = T // CHUNK

    def body(u_ref, w_ref, qg_ref, kd_ref, in_ref, egl_ref, o_ref, vn_ref, s0_ref, s_scr):
        @pl.when(pl.program_id(0) == 0)
        def _():
            s_scr[...] = jnp.zeros_like(s_scr)

        for c in range(cs):
            rows = pl.ds(c * CHUNK, CHUNK)
            s = s_scr[...]
            s0_ref[:, :, c] = s.reshape(BL, NH, DH, DH)

            def ld(r, m=DH):
                return r[:, :, rows, :].reshape(g, CHUNK, m)

            vn = ld(u_ref) - _bmm_raw(ld(w_ref), s)
            o = _bmm_raw(ld(qg_ref), s) + _bmm_raw(ld(in_ref, CHUNK), vn)
            e = egl_ref[:, :, pl.ds(c * 8, 1), :].reshape(g, 1, DH)
            s_scr[...] = s * e + _bmm_tn_raw(ld(kd_ref), vn)
            vn_ref[:, :, rows, :] = vn.reshape(BL, NH, CHUNK, DH)
            o_ref[:, :, rows, :] = o.reshape(BL, NH, CHUNK, DH)

    tok = pl.BlockSpec((BL, NH, cs * CHUNK, DH), lambda i: (0, 0, i, 0))
    hs = _sds((BL, NH, T, DH), F32)
    return _pcall(
        body, name="dn_scan_fwd", grid=(ns,),
        in_specs=[tok, tok, tok, tok, pl.BlockSpec((BL, NH, cs * CHUNK, CHUNK), lambda i: (0, 0, i, 0)),
                  pl.BlockSpec((BL, NH, cs * 8, DH), lambda i: (0, 0, i, 0))],
        out_specs=[tok, tok, pl.BlockSpec((BL, NH, cs, DH, DH), lambda i: (0, 0, i, 0, 0))],
        out_shape=[hs, hs, _sds((BL, NH, nc, DH, DH), F32)],
        scratch_shapes=[pltpu.VMEM((g, DH, DH), F32)],
        compiler_params=_cp(("arbitrary",)),
    )(u, w, qg, kd, intra, egl)


def _dn_scan_bwd(do, w, qg, kd, intra, egl, vn, s0):
    cs, ns = _scan_tiles()
    g = BL * NH
    nc = T // CHUNK

    def body(do_ref, w_ref, qg_ref, kd_ref, in_ref, egl_ref, vn_ref, s0_ref,
             du_ref, dw_ref, dqg_ref, dkd_ref, din_ref, degl_ref, ds_scr):
        @pl.when(pl.program_id(0) == 0)
        def _():
            ds_scr[...] = jnp.zeros_like(ds_scr)

        for c in reversed(range(cs)):
            rows = pl.ds(c * CHUNK, CHUNK)

            def ld(r, m=DH):
                return r[:, :, rows, :].reshape(g, CHUNK, m)

            def st(r, val, m=DH):
                r[:, :, rows, :] = val.reshape(BL, NH, CHUNK, m)

            s = s0_ref[:, :, c].reshape(g, DH, DH)
            ds = ds_scr[...]
            dov, vnv, kdv, wv, qgv, inv = ld(do_ref), ld(vn_ref), ld(kd_ref), ld(w_ref), ld(qg_ref), ld(in_ref, CHUNK)
            dv = _bmm_tn_raw(inv, dov) + _bmm_raw(kdv, ds)
            st(din_ref, _bmm_nt_raw(dov, vnv), CHUNK)
            st(dqg_ref, _bmm_nt_raw(dov, s))
            st(dkd_ref, _bmm_nt_raw(vnv, ds))
            st(du_ref, dv)
            st(dw_ref, -_bmm_nt_raw(dv, s))
            de = jnp.sum(jnp.sum(ds * s, axis=2, keepdims=True), axis=1, keepdims=True)
            degl_ref[:, :, pl.ds(c * 8, 8), :] = jnp.broadcast_to(de, (g, 8, DH)).reshape(BL, NH, 8, DH)
            e = egl_ref[:, :, pl.ds(c * 8, 1), :].reshape(g, 1, DH)
            ds_scr[...] = ds * e + _bmm_tn_raw(qgv, dov) - _bmm_tn_raw(wv, dv)

    rev = lambda i: (0, 0, ns - 1 - i, 0)
    tok = pl.BlockSpec((BL, NH, cs * CHUNK, DH), rev)
    sq = pl.BlockSpec((BL, NH, cs * CHUNK, CHUNK), rev)
    per = pl.BlockSpec((BL, NH, cs * 8, DH), rev)
    hs = _sds((BL, NH, T, DH), F32)
    return _pcall(
        body, name="dn_scan_bwd", grid=(ns,),
        in_specs=[tok, tok, tok, tok, sq, per, tok,
                  pl.BlockSpec((BL, NH, cs, DH, DH), lambda i: (0, 0, ns - 1 - i, 0, 0))],
        out_specs=[tok, tok, tok, tok, sq, per],
        out_shape=[hs, hs, hs, hs, _sds((BL, NH, T, CHUNK), F32), _sds((BL, NH, nc * 8, DH), F32)],
        scratch_shapes=[pltpu.VMEM((g, DH, DH), F32)],
        compiler_params=_cp(("arbitrary",)),
    )(do, w, qg, kd, intra, egl, vn, s0)


def _gated_norm(o_h, z_h, og):
    r = lax.rsqrt(jnp.mean(o_h * o_h, axis=-1, keepdims=True) + EPS)
    return (o_h * r * og) * (z_h * jax.nn.sigmoid(z_h))


def _dn_gnorm_fwd(o, proj, o_g):
    tm = min(256, T)
    nt = T // tm

    def body(o_ref, z_ref, g_ref, og_ref):
        z = z_ref[...]
        for h in range(NH):
            og_ref[:, h * DH:(h + 1) * DH] = _gated_norm(o_ref[0, h], z[:, h * DH:(h + 1) * DH], g_ref[...]).astype(BF16)

    return _pcall(
        body, name="dn_gnorm_fwd", grid=(BL, nt),
        in_specs=[pl.BlockSpec((1, NH, tm, DH), lambda b, i: (b, 0, i, 0)),
                  pl.BlockSpec((tm, D), lambda b, i: (b * nt + i, 3)),
                  pl.BlockSpec((1, DH), lambda b, i: (0, 0))],
        out_specs=pl.BlockSpec((tm, D), lambda b, i: (b * nt + i, 0)),
        out_shape=_sds((BL * T, D), BF16),
        compiler_params=_cp(("arbitrary", "arbitrary")),
    )(o, proj, o_g)


def _dn_gnorm_bwd(dog, o, proj, o_g):
    tm = min(256, T)
    nt = T // tm

    def body(dog_ref, o_ref, z_ref, g_ref, do_ref, dz_ref, dg_ref):
        z = z_ref[...]
        dog = dog_ref[...]
        dg = jnp.zeros((1, DH), F32)
        for h in range(NH):
            cols = slice(h * DH, (h + 1) * DH)
            _, vjp = jax.vjp(_gated_norm, o_ref[0, h], z[:, cols], g_ref[...])
            do_h, dz_h, dg_h = vjp(dog[:, cols])
            do_ref[0, h] = do_h
            dz_ref[:, cols] = dz_h.astype(BF16)
            dg += dg_h
        _acc_rows(dg_ref, jnp.logical_and(pl.program_id(0) == 0, pl.program_id(1) == 0), [dg])

    return _pcall(
        body, name="dn_gnorm_bwd", grid=(BL, nt),
        in_specs=[pl.BlockSpec((tm, D), lambda b, i: (b * nt + i, 0)),
                  pl.BlockSpec((1, NH, tm, DH), lambda b, i: (b, 0, i, 0)),
                  pl.BlockSpec((tm, D), lambda b, i: (b * nt + i, 3)),
                  pl.BlockSpec((1, DH), lambda b, i: (0, 0))],
        out_specs=[pl.BlockSpec((1, NH, tm, DH), lambda b, i: (b, 0, i, 0)),
                   pl.BlockSpec((tm, D), lambda b, i: (b * nt + i, 0)),
                   pl.BlockSpec((1, DH), lambda b, i: (0, 0))],
        out_shape=[_sds((BL, NH, T, DH), F32), _sds((BL * T, D), BF16), _sds((1, DH), F32)],
        compiler_params=_cp(("arbitrary", "arbitrary")),
    )(dog, o, proj, o_g)


def _dn_prep_bwd(dq, dk, dv, dgb, dbb, conv, proj, alog, dtb):
    n = conv.shape[0]
    tt, nt = _conv_tiles()
    w3 = 3 * D

    def body(dq_ref, dk_ref, dv_ref, dgb_ref, dbb_ref, conv_ref, ab_ref, alog_ref, dtb_ref, dconv_ref, dab_ref, dhead_ref):
        cv = conv_ref[...]
        sg = jax.nn.sigmoid(cv)
        dact = sg * (1.0 + cv * (1.0 - sg))
        lane = lax.broadcasted_iota(jnp.int32, (tt, 128), 1)
        cg = jnp.zeros((tt, 128), F32)
        cb = jnp.zeros((tt, 128), F32)
        for h in range(NH):
            cols = slice(h * DH, (h + 1) * DH)
            dconv_ref[:, h * DH:(h + 1) * DH] = dq_ref[0, h] * dact[:, cols]
            dconv_ref[:, D + h * DH:D + (h + 1) * DH] = dk_ref[0, h] * dact[:, D + h * DH:D + (h + 1) * DH]
            dconv_ref[:, 2 * D + h * DH:2 * D + (h + 1) * DH] = dv_ref[0, h] * dact[:, 2 * D + h * DH:2 * D + (h + 1) * DH]
            cg = jnp.where(lane == h, jnp.sum(dgb_ref[0, h], axis=-1, keepdims=True), cg)
            cb = jnp.where(lane == NH + h, jnp.sum(dbb_ref[0, h], axis=-1, keepdims=True), cb)
        _, vjp = jax.vjp(_g_beta, ab_ref[...], alog_ref[...], dtb_ref[...])
        dab, dalog, ddtb = vjp((cg, cb))
        dab_ref[...] = dab.astype(BF16)
        _acc_rows(dhead_ref, jnp.logical_and(pl.program_id(0) == 0, pl.program_id(1) == 0), [dalog, ddtb])

    row = lambda b, i: (b * nt + i, 0)
    head = pl.BlockSpec((1, NH, tt, DH), lambda b, i: (b, 0, i, 0))
    vec = pl.BlockSpec((1, 128), lambda b, i: (0, 0))
    return _pcall(
        body, name="dn_prep_bwd", grid=(BL, nt),
        in_specs=[head] * 5 + [pl.BlockSpec((tt, w3), row),
                               pl.BlockSpec((tt, 128), lambda b, i: (b * nt + i, 4 * D // 128)), vec, vec],
        out_specs=[pl.BlockSpec((tt, w3), row), pl.BlockSpec((tt, 128), row), pl.BlockSpec((2, 128), lambda b, i: (0, 0))],
        out_shape=[_sds((n, w3), F32), _sds((n, 128), BF16), _sds((2, 128), F32)],
        compiler_params=_cp(("arbitrary", "arbitrary")),
    )(dq, dk, dv, dgb, dbb, conv, proj, alog, dtb)


def _dn_sconv_bwd(dconv, proj, w_sc):
    n = dconv.shape[0]
    tt, nt = _conv_tiles()
    w3 = 3 * D

    def body(dc_ref, nxt_ref, qkv_ref, halo_ref, w_ref, dpre_ref, dw_ref, dwin, pwin, rotd, rotp, dsc, accw):
        b, i = pl.program_id(0), pl.program_id(1)
        dwin[0:tt, :] = dc_ref[...]
        dwin[tt:tt + SHALO, :] = jnp.where(i == nt - 1, 0.0, nxt_ref[...])
        pwin[0:SHALO, :] = jnp.where(i == 0, 0.0, halo_ref[...])
        pwin[SHALO:SHALO + tt, :] = qkv_ref[...]
        for k in range(SCW - 1):
            rotd[k] = dwin[pl.ds(k + 1, tt), :]
            rotp[k] = pwin[pl.ds(SHALO - (SCW - 1) + k, tt), :]
        accw[...] = jnp.zeros_like(accw)

        def chunk(c, carry):
            r0 = pl.multiple_of(c * SROWS, SROWS)
            rows = pl.ds(r0, SROWS)
            dc = dc_ref[rows, :]
            dpre = w_ref[SCW - 1:SCW, :] * dc
            accw[SCW - 1] += dc * pwin[pl.ds(pl.multiple_of(r0 + SHALO, SROWS), SROWS), :]
            for k in range(SCW - 1):
                dpre += w_ref[k:k + 1, :] * rotd[SCW - 2 - k, rows, :]
                accw[k] += dc * rotp[k, rows, :]
            dsc[rows, :] = dpre
            return carry

        lax.fori_loop(0, tt // SROWS, chunk, 0)
        dpre_ref[...] = dsc[...].astype(BF16)
        dws = [jnp.sum(accw[k], axis=0, keepdims=True) for k in range(SCW)]
        dws += [jnp.zeros((1, w3), F32)] * (SHALO - SCW)
        _acc_rows(dw_ref, jnp.logical_and(b == 0, i == 0), dws)

    row = lambda b, i: (b * nt + i, 0)
    return _pcall(
        body, name="dn_sconv_bwd", grid=(BL, nt),
        in_specs=[pl.BlockSpec((tt, w3), row), _next_halo_spec(w3, tt, SHALO),
                  pl.BlockSpec((tt, w3), row), _prev_halo_spec(w3, tt, SHALO),
                  pl.BlockSpec((SHALO, w3), lambda b, i: (0, 0))],
        out_specs=[pl.BlockSpec((tt, w3), row), pl.BlockSpec((SHALO, w3), lambda b, i: (0, 0))],
        out_shape=[_sds((n, w3), BF16), _sds((SHALO, w3), F32)],
        scratch_shapes=[pltpu.VMEM((tt + SHALO, w3), F32), pltpu.VMEM((SHALO + tt, w3), F32),
                        pltpu.VMEM((SCW - 1, tt, w3), F32), pltpu.VMEM((SCW - 1, tt, w3), F32),
                        pltpu.VMEM((tt, w3), F32), pltpu.VMEM((SCW, SROWS, w3), F32)],
        compiler_params=_cp(("arbitrary", "arbitrary")),
    )(dconv, dconv, proj, proj, w_sc)


def _ada_fwd(c_all, w_ada, b_cols):
    nl, _, m = w_ada.shape
    nb = c_all.shape[0]

    def body(c_ref, w_ref, b_ref, o_ref):
        cv = c_ref[...]
        cs = (cv * jax.nn.sigmoid(cv)).astype(BF16)
        o_ref[...] = _dot(cs, w_ref[...].astype(BF16)) + b_ref[...]

    return _pcall(
        body, name="ada_fwd", grid=(nl,),
        in_specs=[pl.BlockSpec((nb, D), lambda l: (0, 0)), pl.BlockSpec((None, D, m), lambda l: (l, 0, 0)),
                  pl.BlockSpec((None, 1, m), lambda l: (l, 0, 0))],
        out_specs=pl.BlockSpec((None, nb, m), lambda l: (l, 0, 0)),
        out_shape=_sds((nl, nb, m), F32),
        compiler_params=_cp(("arbitrary",)),
    )(c_all, w_ada, b_cols)


def _ada_bwd(c_all, dmod_cols):
    nl, nb, m = dmod_cols.shape

    def body(c_ref, d_ref, o_ref):
        cv = c_ref[...]
        cs = (cv * jax.nn.sigmoid(cv)).astype(BF16)
        o_ref[0] = _dot_tn(cs, d_ref[...].astype(BF16))

    return _pcall(
        body, name="ada_bwd", grid=(nl,),
        in_specs=[pl.BlockSpec((nb, D), lambda l: (0, 0)), pl.BlockSpec((None, nb, m), lambda l: (l, 0, 0))],
        out_specs=pl.BlockSpec((1, D, m), lambda l: (0, l, 0)),
        out_shape=_sds((1, nl * D, m), F32),
        compiler_params=_cp(("arbitrary",)),
    )(c_all, dmod_cols)


def _loss_head(x, tgt, fg):
    n = x.shape[0]
    tm = min(512, T)

    def f(xv, g, t):
        r = lax.rsqrt(jnp.mean(xv * xv, axis=-1, keepdims=True) + EPS)
        e = xv * r * g - t
        return 0.5 * jnp.sum(e * e, axis=0, keepdims=True) * (1.0 / D)

    def body(x_ref, t_ref, g_ref, dx_ref, st_ref):
        t = t_ref[...]
        lrow, vjp = jax.vjp(lambda xv, g: f(xv, g, t), x_ref[...], g_ref[...])
        dx, dg = vjp(jnp.ones_like(lrow))
        dx_ref[...] = dx
        _acc_rows(st_ref, pl.program_id(0) == 0, [dg, lrow])

    return _pcall(
        body, name="loss_head", grid=(n // tm,),
        in_specs=[pl.BlockSpec((tm, D), lambda i: (i, 0)), pl.BlockSpec((tm, D), lambda i: (i, 0)),
                  pl.BlockSpec((1, D), lambda i: (0, 0))],
        out_specs=[pl.BlockSpec((tm, D), lambda i: (i, 0)), pl.BlockSpec((2, D), lambda i: (0, 0))],
        out_shape=[_sds((n, D), F32), _sds((2, D), F32)],
        compiler_params=_cp(("arbitrary",)),
    )(x, tgt, fg)


def _adamw(parts, w, m, v):
    p, r, c = parts.shape
    tr = r
    for cand in (256, 128, 64, 32, 16, 8):
        if r % cand == 0:
            tr = cand
            break
    k1 = 1.0 - B1 ** STEP
    k2 = 1.0 - B2 ** STEP

    def body(p_ref, w_ref, m_ref, v_ref, g_ref, d_ref, nm_ref, nv_ref):
        g = p_ref[0].astype(F32)
        for q in range(1, p):
            g += p_ref[q].astype(F32)
        mn = B1 * m_ref[...] + (1.0 - B1) * g
        vn = B2 * v_ref[...] + (1.0 - B2) * (g * g)
        g_ref[...] = g
        nm_ref[...] = mn
        nv_ref[...] = vn
        d_ref[...] = -LR * ((mn / k1) / (jnp.sqrt(vn / k2) + AEPS) + WD * w_ref[...])

    blk = pl.BlockSpec((tr, c), lambda i: (i, 0))
    return _pcall(
        body, name="adamw", grid=(r // tr,),
        in_specs=[pl.BlockSpec((p, tr, c), lambda i: (0, i, 0)), blk, blk, blk],
        out_specs=[blk] * 4, out_shape=[_sds((r, c), F32)] * 4,
        compiler_params=_cp(("arbitrary",)),
    )(parts, w, m, v)


def _sum_parts(parts):
    p, r, c = parts.shape

    def body(p_ref, o_ref):
        acc = p_ref[0]
        for q in range(1, p):
            acc += p_ref[q]
        o_ref[...] = acc

    return _pcall(body, name="sum_parts", out_shape=_sds((r, c), F32))(parts)


def _adamw_slot(parts, w, m, v, outs, row0, col):
    p, r, c = parts.shape
    tr = r
    for cand in (256, 128, 64, 32, 16, 8):
        if r % cand == 0:
            tr = cand
            break
    if r % 352 == 0:
        tr = 352
    nt = r // tr
    k1 = 1.0 - B1 ** STEP
    k2 = 1.0 - B2 ** STEP

    def body(p_ref, w_ref, m_ref, v_ref, g0, d0, m0, v0, g_ref, d_ref, nm_ref, nv_ref):
        g = p_ref[0].astype(F32)
        for q in range(1, p):
            g += p_ref[q].astype(F32)
        mn = B1 * m_ref[...] + (1.0 - B1) * g
        vn = B2 * v_ref[...] + (1.0 - B2) * (g * g)
        g_ref[...] = g
        nm_ref[...] = mn
        nv_ref[...] = vn
        d_ref[...] = -LR * ((mn / k1) / (jnp.sqrt(vn / k2) + AEPS) + WD * w_ref[...])

    blk = pl.BlockSpec((tr, c), lambda i: (row0 * nt + i, col))
    anyspec = pl.BlockSpec(memory_space=pl.ANY)
    return _pcall(
        body, name="adamw_slot", grid=(nt,),
        in_specs=[pl.BlockSpec((p, tr, c), lambda i: (0, i, 0)), blk, blk, blk] + [anyspec] * 4,
        out_specs=[blk] * 4, out_shape=[_sds(w.shape, F32)] * 4,
        input_output_aliases={4: 0, 5: 1, 6: 2, 7: 3},
        compiler_params=_cp(("arbitrary",)),
    )(parts, w, m, v, *outs)


def _pack(arrs):
    flat = jnp.concatenate([a.reshape(-1) for a in arrs])
    pad = (-flat.shape[0]) % 1024
    return jnp.pad(flat, (0, pad)).reshape(-1, 128)


def _unpack(buf, shapes):
    flat = buf.reshape(-1)
    out, off = [], 0
    for s in shapes:
        size = 1
        for d in s:
            size *= d
        out.append(flat[off:off + size].reshape(s))
        off += size
    return out


def kernel(x, c, norm_g, w_ada, b_ada, w_ffn_in, w_ffn_out, cm_w_glu, cm_b_glu, cm_w_dw, cm_b_dw, cm_ln_g, cm_ln_b, cm_w_pw, cm_b_pw, dn_w_in, dn_w_sconv, dn_a_log, dn_dt_bias, dn_o_g, dn_w_out, final_g, loss_target, m_norm_g, m_w_ada, m_b_ada, m_w_ffn_in, m_w_ffn_out, m_cm_w_glu, m_cm_b_glu, m_cm_w_dw, m_cm_b_dw, m_cm_ln_g, m_cm_ln_b, m_cm_w_pw, m_cm_b_pw, m_dn_w_in, m_dn_w_sconv, m_dn_a_log, m_dn_dt_bias, m_dn_o_g, m_dn_w_out, m_final_g, v_norm_g, v_w_ada, v_b_ada, v_w_ffn_in, v_w_ffn_out, v_cm_w_glu, v_cm_b_glu, v_cm_w_dw, v_cm_b_dw, v_cm_ln_g, v_cm_ln_b, v_cm_w_pw, v_cm_b_pw, v_dn_w_in, v_dn_w_sconv, v_dn_a_log, v_dn_dt_bias, v_dn_o_g, v_dn_w_out, v_final_g):
    me = 4 * lax.axis_index("x") + 2 * lax.axis_index("y") + lax.axis_index("c")
    n = BL * T
    nf = 4
    tf = FF // nf
    na, nb = cm_w_glu.shape[0], dn_w_in.shape[0]
    mcols = w_ada.shape[2]
    dsh = D // NDEV

    tr_ffn = lambda a: jnp.swapaxes(a, 2, 3)
    tr_dn = lambda a: jnp.transpose(a, (2, 0, 1))
    wt_ffn_in, wt_dn_in = tr_ffn(w_ffn_in), tr_dn(dn_w_in)

    def unit_weights(l, part):
        if part == 0:
            ws = (wt_ffn_in[l, 0], w_ffn_out[l, 0])
        else:
            mix = (cm_w_glu[l // 2], cm_w_pw[l // 2]) if l % 2 == 0 else (wt_dn_in[:, l // 2], dn_w_out[l // 2])
            ws = (wt_ffn_in[l, 1], w_ffn_out[l, 1], *mix)
        return [w.astype(BF16) for w in ws]

    gathers, all_started = {}, jnp.zeros((8, 128), F32)
    for l in range(DEPTH):
        for part in range(2):
            gathers[l, part], tok = _exchange_start(unit_weights(l, part), True, f"gather_start_{l}_{part}")
            all_started = all_started + tok

    c_g, ng_g, dw_g, sc_g = _all_gather([c, norm_g, cm_w_dw, dn_w_sconv], "gather_small")
    whole = lambda g: jnp.moveaxis(g, 0, -2).reshape(*g.shape[1:-1], -1)
    c_all = c_g.reshape(NDEV * BL, D)
    norm_g_f, w_dw_f, w_sc_f = whole(ng_g), whole(dw_g), whole(sc_g)

    b_cols = lax.dynamic_slice_in_dim(b_ada, me * mcols, mcols, axis=1)[:, None, :]
    mod_cols = _ada_fwd(c_all, w_ada, b_cols)
    mod_g, = _all_gather([mod_cols], "gather_mod")
    mod_all = jnp.transpose(mod_g, (1, 2, 0, 3)).reshape(DEPTH, NDEV * BL, 9 * D)
    mod = lax.dynamic_slice_in_dim(mod_all, me * BL, BL, axis=1).reshape(DEPTH, BL, 3, 3, D)

    gathered = [None] * DEPTH

    def ffn_weights(l, s):
        return gathered[l][s].reshape(2, nf, tf, D), gathered[l][2 + s].reshape(nf, tf, D)

    xs = x.reshape(n, D)
    saved = []
    for l in range(DEPTH):
        rec = {}
        ga = _exchange_wait(gathers[l, 0], all_started if l == 0 else xs, f"gather_wait_{l}_0")
        gathered[l] = [ga[0], None, ga[1], None, None, None]
        for s, j in ((0, 0), (1, 2)):
            if j == 2:
                gb = _exchange_wait(gathers[l, 1], xs, f"gather_wait_{l}_1")
                gathered[l] = [ga[0], gb[0], ga[1], gb[1], gb[2], gb[3]]
            w_in, w_out = ffn_weights(l, s)
            ssg, ng = mod[l, :, j], norm_g_f[l, j][None]
            if j == 2:
                ssg1, ng1 = mod[l, :, 1], norm_g_f[l, 1][None]
                if l % 2 == 0:
                    a = l // 2
                    w_glu = gathered[l][4]
                    w_pw = gathered[l][5].reshape(D, D)
                    w_dw = jnp.pad(w_dw_f[a], ((0, HALO - CW), (0, 0)))
                    h1, ab = _premod_matmul(xs, ssg1, ng1, w_glu, cm_b_glu[a][None], w_glu.shape[2])
                    u1, u2 = _cm_mid_fwd(ab, w_dw, cm_b_dw[a][None], cm_ln_g[a][None], cm_ln_b[a][None])
                    xn, ymix = _matmul_res(xs, u2, ssg1, w_pw, cm_b_pw[a][None])
                    rec["mix"] = dict(x=xs, h=h1, ab=ab, u1=u1, u2=u2, y=ymix, w_glu=w_glu, w_pw=w_pw, w_dw=w_dw)
                else:
                    mi = l // 2
                    w_proj = jnp.pad(gathered[l][4].reshape(4 * D + 2 * NH, D), ((0, 128 - 2 * NH), (0, 0)))
                    w_o = gathered[l][5].reshape(D, D)
                    w_sc = jnp.pad(w_sc_f[mi], ((0, SHALO - SCW), (0, 0)))
                    alog = jnp.pad(dn_a_log[mi], (0, 128 - NH))[None]
                    dtb = jnp.pad(dn_dt_bias[mi], (0, 128 - NH))[None]
                    h1, proj = _premod_matmul(xs, ssg1, ng1, w_proj, jnp.zeros((1, w_proj.shape[0]), F32),
                                              (4 * D + 128) // 3 if (4 * D + 128) % 384 == 0 else 128)
                    conv, q, k, v, gb, bb = _dn_sconv_fwd(proj, w_sc, alog, dtb)
                    u, w, qg, kd, intra, egl = _dn_pre_fwd(q, k, v, gb, bb)
                    o, vn, s0 = _dn_scan_fwd(u, w, qg, kd, intra, egl)
                    og = _dn_gnorm_fwd(o, proj, dn_o_g[mi][None])
                    xn, ymix = _matmul_res(xs, og, ssg1, w_o, jnp.zeros((1, D), F32))
                    rec["mix"] = dict(x=xs, h=h1, proj=proj, conv=conv, q=q, k=k, v=v, gb=gb, bb=bb, w=w, qg=qg, kd=kd,
                                      intra=intra, egl=egl, o=o, vn=vn, s0=s0, og=og, y=ymix, w_proj=w_proj, w_o=w_o,
                                      w_sc=w_sc, alog=alog, dtb=dtb)
                xs = xn
            xn, gu, hid, y = _ffn_fwd(xs, ssg, ng, w_in, w_out)
            rec[s] = dict(x=xs, gu=gu, hid=hid, y=y)
            xs = xn
        saved.append(rec)

    dx, stats = _loss_head(xs, loss_target.reshape(n, D), final_g[None])
    loss = lax.psum(jnp.sum(stats[1]), AXES)
    d_final_g = stats[0]

    d_mod = [[None] * 3 for _ in range(DEPTH)]
    d_norm = [[None] * 3 for _ in range(DEPTH)]
    dw_ffn_in = [[None] * 2 for _ in range(DEPTH)]
    dw_ffn_out = [[None] * 2 for _ in range(DEPTH)]
    dcm = [dict() for _ in range(na)]
    ddn = [dict() for _ in range(nb)]
    exchanges = {}

    def gather_small_grads():
        dmod_loc = jnp.stack([jnp.stack(d_mod[l], axis=1) for l in range(DEPTH)]).reshape(DEPTH, BL, 9 * D)
        small = [jnp.sum(dmod_loc, axis=1), jnp.stack([jnp.stack(d_norm[l]) for l in range(DEPTH)]),
                 jnp.stack([d["b_glu"] for d in dcm]), jnp.stack([d["w_dw"] for d in dcm]), jnp.stack([d["b_dw"] for d in dcm]),
                 jnp.stack([d["ln_g"] for d in dcm]), jnp.stack([d["ln_b"] for d in dcm]), jnp.stack([d["b_pw"] for d in dcm]),
                 jnp.stack([d["w_sconv"] for d in ddn]), jnp.stack([d["a_log"] for d in ddn]),
                 jnp.stack([d["dt_bias"] for d in ddn]), jnp.stack([d["o_g"] for d in ddn]), d_final_g]
        dmod_g, small_parts = _all_gather([dmod_loc, _pack(small)], "gather_small_grads")
        return dmod_g, small_parts, [a.shape for a in small]

    token = jnp.zeros((), F32)
    for l in reversed(range(DEPTH)):
        rec = saved[l]
        for s, j in ((1, 2), (0, 0)):
            w_in, w_out = ffn_weights(l, s)
            ssg, ng = mod[l, :, j] + token, norm_g_f[l, j][None]
            r = rec[s]
            dx, dgu, hb, dout, dssg, dng = _ffn_bwd_a(r["x"], dx, ssg, ng, r["y"], r["gu"], w_in, w_out)
            dw_ffn_in[l][s], dw_ffn_out[l][s] = _ffn_bwd_w(hb, dgu, r["hid"], dout)
            d_mod[l][j], d_norm[l][j] = dssg, dng[0]
            if j == 2:
                ssg1, ng1 = mod[l, :, 1], norm_g_f[l, 1][None]
                r = rec["mix"]
                if l % 2 == 0:
                    a = l // 2
                    du2, dy, dgate, db_pw = _matmul_res_bwd(dx, r["y"], ssg1, r["w_pw"])
                    du1, dln = _cm_mid_bwd_a(du2, r["u1"], cm_ln_g[a][None], cm_ln_b[a][None])
                    dab, dw_dw, db_dw, db_glu = _cm_mid_bwd_b(du1, r["ab"], r["w_dw"])
                    dx, dssg, dng = _premod_matmul_bwd(r["x"], dx, ssg1, ng1, [dab], r["w_glu"])
                    dcm[a] = dict(w_glu=_wgrad_shards(r["h"], dab, NDEV), w_pw=_wgrad(r["u2"], dy).reshape(NDEV, dsh, D),
                                  b_glu=db_glu[0], w_dw=dw_dw[:CW], b_dw=db_dw[0], ln_g=dln[0], ln_b=dln[1], b_pw=db_pw[0])
                else:
                    mi = l // 2
                    dog, dy, dgate, _ = _matmul_res_bwd(dx, r["y"], ssg1, r["w_o"])
                    do, dz, d_og = _dn_gnorm_bwd(dog, r["o"], r["proj"], dn_o_g[mi][None])
                    du, dw, dqg, dkd, dintra, degl = _dn_scan_bwd(do, r["w"], r["qg"], r["kd"], r["intra"], r["egl"],
                                                                   r["vn"], r["s0"])
                    dq, dk, dv, dgb, dbb = _dn_pre_bwd(r["q"], r["k"], r["v"], r["gb"], r["bb"], du, dw, dqg, dkd, dintra, degl)
                    dconv, dab16, dhead = _dn_prep_bwd(dq, dk, dv, dgb, dbb, r["conv"], r["proj"], r["alog"], r["dtb"])
                    dpre, dw_sc = _dn_sconv_bwd(dconv, r["proj"], r["w_sc"])
                    dx, dssg, dng = _premod_matmul_bwd(r["x"], dx, ssg1, ng1, [dpre, dz, dab16], r["w_proj"])
                    dw_in = jnp.concatenate([_wgrad(dpre, r["h"]), _wgrad(dz, r["h"]),
                                             _wgrad(dab16, r["h"])[:2 * NH]], axis=0)
                    ddn[mi] = dict(w_in=dw_in.reshape(NDEV, -1, D), w_out=_wgrad(r["og"], dy).reshape(NDEV, dsh, D),
                                   w_sconv=dw_sc[:SCW], a_log=dhead[0, :NH], dt_bias=dhead[1, :NH], o_g=d_og[0])
                d_mod[l][1] = dssg.at[:, 2].set(dgate[:, 0])
                d_norm[l][1] = dng[0]
            unit = [dw_ffn_in[l][s].reshape(NDEV, tf, D), dw_ffn_out[l][s].reshape(NDEV, FF // NDEV, D)]
            if j == 2:
                g = dcm[l // 2] if l % 2 == 0 else ddn[l // 2]
                unit += [g["w_glu"], g["w_pw"]] if l % 2 == 0 else [g["w_in"], g["w_out"]]
            if l == 0 and s == 0:
                dmod_g, small_parts, full_shapes = gather_small_grads()
                unit[0], dmod_g, small_parts = lax.optimization_barrier((unit[0], dmod_g, small_parts))
                small_gathered = (dmod_g, small_parts, full_shapes)
            exchanges[l, s], token = _exchange_start(unit, False, f"grads_start_{l}_{s}")
    grad_x = dx.reshape(BL, T, D)

    dmod_g, small_parts, full_shapes = small_gathered
    dmod_all = jnp.transpose(dmod_g, (1, 0, 2, 3)).reshape(DEPTH, NDEV * BL, 9 * D)
    g_w_ada = _ada_bwd(c_all, lax.dynamic_slice_in_dim(dmod_all, me * mcols, mcols, axis=2))

    got = []
    for l in range(DEPTH):
        ea = _exchange_wait(exchanges[l, 0], dx, f"grads_wait_{l}_0") if l > 0 else [None, None]
        eb = _exchange_wait(exchanges[l, 1], dx, f"grads_wait_{l}_1")
        got.append([ea[0], eb[0], ea[1], eb[1], eb[2], eb[3]])

    names = ["b_ada", "norm_g", "cm_b_glu", "cm_w_dw", "cm_b_dw", "cm_ln_g", "cm_ln_b", "cm_b_pw",
             "dn_w_sconv", "dn_a_log", "dn_dt_bias", "dn_o_g", "final_g"]
    cols = lambda a, width: lax.dynamic_slice_in_dim(a, me * width, width, axis=a.ndim - 1)
    local = {"norm_g": lambda a: cols(a, dsh), "cm_w_dw": lambda a: cols(a, dsh), "dn_w_sconv": lambda a: cols(a, 3 * dsh)}
    summed = _unpack(_sum_parts(small_parts), full_shapes)
    mine = [local.get(nm, lambda a: a)(p) for nm, p in zip(names, summed)]
    small_w = dict(b_ada=(b_ada, m_b_ada, v_b_ada), norm_g=(norm_g, m_norm_g, v_norm_g),
                   cm_b_glu=(cm_b_glu, m_cm_b_glu, v_cm_b_glu), cm_w_dw=(cm_w_dw, m_cm_w_dw, v_cm_w_dw),
                   cm_b_dw=(cm_b_dw, m_cm_b_dw, v_cm_b_dw), cm_ln_g=(cm_ln_g, m_cm_ln_g, v_cm_ln_g),
                   cm_ln_b=(cm_ln_b, m_cm_ln_b, v_cm_ln_b), cm_b_pw=(cm_b_pw, m_cm_b_pw, v_cm_b_pw),
                   dn_w_sconv=(dn_w_sconv, m_dn_w_sconv, v_dn_w_sconv), dn_a_log=(dn_a_log, m_dn_a_log, v_dn_a_log),
                   dn_dt_bias=(dn_dt_bias, m_dn_dt_bias, v_dn_dt_bias), dn_o_g=(dn_o_g, m_dn_o_g, v_dn_o_g),
                   final_g=(final_g, m_final_g, v_final_g))
    loc_shapes = [small_w[nm][0].shape for nm in names]
    sres_raw = _adamw(_pack(mine)[None], *[_pack([small_w[nm][q] for nm in names]) for q in range(3)])
    sres = [dict(zip(names, _unpack(r, loc_shapes))) for r in sres_raw]

    res = {}

    def update(slots, wmv, view, back=None, outs=None):
        w2, m2, v2 = [view(a) for a in wmv]
        outs = [lax.empty(w2.shape, F32) for _ in range(4)] if outs is None else outs
        for p, row0, col in slots:
            outs = _adamw_slot(p, w2, m2, v2, outs, row0, col)
        return outs if back is None else [back(o) for o in outs]

    ffn_slots = [(l, s) for l in reversed(range(DEPTH)) for s in (1, 0)][:-1]
    wmv_in, view_in = (w_ffn_in, m_w_ffn_in, v_w_ffn_in), lambda a: tr_ffn(a).reshape(-1, D)
    wmv_out, view_out = (w_ffn_out, m_w_ffn_out, v_w_ffn_out), lambda a: a.reshape(-1, D)
    part_in = update([(got[l][s], 2 * l + s, 0) for l, s in ffn_slots], wmv_in, view_in)
    part_out = update([(got[l][2 + s], 2 * l + s, 0) for l, s in ffn_slots], wmv_out, view_out)
    cgl = cm_w_glu.shape[2]
    res["cm_w_glu"] = update([(got[2 * a][4], a, 0) for a in range(na)], (cm_w_glu, m_cm_w_glu, v_cm_w_glu),
                             lambda a: a.reshape(-1, cgl), lambda o: o.reshape(cm_w_glu.shape))
    res["cm_w_pw"] = update([(got[2 * a][5], a, 0) for a in range(na)], (cm_w_pw, m_cm_w_pw, v_cm_w_pw),
                            lambda a: a.reshape(-1, D), lambda o: o.reshape(cm_w_pw.shape))
    cdn = dn_w_in.shape[2]
    res["dn_w_in"] = update([(got[2 * i + 1][4], 0, i) for i in range(nb)], (dn_w_in, m_dn_w_in, v_dn_w_in),
                            lambda a: tr_dn(a).reshape(cdn, nb * D),
                            lambda o: jnp.transpose(o.reshape(cdn, nb, D), (1, 2, 0)))
    res["dn_w_out"] = update([(got[2 * i + 1][5], i, 0) for i in range(nb)], (dn_w_out, m_dn_w_out, v_dn_w_out),
                             lambda a: a.reshape(-1, D), lambda o: o.reshape(dn_w_out.shape))
    res["w_ada"] = [o.reshape(w_ada.shape) for o in
                    _adamw(g_w_ada, *[a.reshape(-1, mcols) for a in (w_ada, m_w_ada, v_w_ada)])]
    done = [part_in[0], part_out[0], sres_raw[0]] + [res[nm][0] for nm in ("cm_w_glu", "cm_w_pw", "dn_w_in", "dn_w_out", "w_ada")]
    last = _exchange_wait(exchanges[0, 0], done, "grads_wait_0_0")
    res["w_ffn_in"] = update([(last[0], 0, 0)], wmv_in, view_in,
                             lambda o: jnp.swapaxes(o.reshape(DEPTH, 2, tf, D), 2, 3), part_in)
    res["w_ffn_out"] = update([(last[1], 0, 0)], wmv_out, view_out, lambda o: o.reshape(w_ffn_out.shape), part_out)
    for nm in names:
        res[nm] = [sres[q][nm] for q in range(4)]

    order = ["norm_g", "w_ada", "b_ada", "w_ffn_in", "w_ffn_out", "cm_w_glu", "cm_b_glu", "cm_w_dw", "cm_b_dw", "cm_ln_g",
             "cm_ln_b", "cm_w_pw", "cm_b_pw", "dn_w_in", "dn_w_sconv", "dn_a_log", "dn_dt_bias", "dn_o_g", "dn_w_out", "final_g"]
    return (loss, grad_x, *[res[nm][0] for nm in order], *[res[nm][1] for nm in order],
            *[res[nm][2] for nm in order], *[res[nm][3] for nm in order])
```

```python
import functools

import jax
import jax.numpy as jnp
from jax import lax
from jax.experimental import pallas as pl
from jax.experimental.pallas import tpu as pltpu

F32 = jnp.float32
BF16 = jnp.bfloat16
HI = lax.Precision.HIGHEST
INV_PREC = None
MESH = pl.DeviceIdType.MESH
AXES = ("x", "y", "c")

NDEV = 8
D = 1024
T = 2048
BL = 2
FF = 2816
NH = 8
DH = 128
CW = 31
SCW = 4
CHUNK = 64
DEPTH = 4
EPS = 1e-6
LR, B1, B2, AEPS, WD, STEP = 0.001, 0.9, 0.999, 1e-08, 0.01, 10

VMEM_LIMIT_BYTES = 56 * 1024 * 1024
HALO = 32
SHALO = 8


def _pcall(body, **kw):
    return pl.pallas_call(body, **kw)


def _cp(sem=None):
    return pltpu.CompilerParams(dimension_semantics=sem, vmem_limit_bytes=VMEM_LIMIT_BYTES)


def _sds(shape, dtype):
    return jax.ShapeDtypeStruct(tuple(shape), dtype)


def _dot(a, b):
    return jnp.dot(a, b, preferred_element_type=F32)


def _dot_nt(a, b):
    return lax.dot_general(a, b, (((1,), (1,)), ((), ())), preferred_element_type=F32)


def _dot_tn(a, b):
    return lax.dot_general(a, b, (((0,), (0,)), ((), ())), preferred_element_type=F32)


def _modulate(x, ng, scale, shift):
    r = lax.rsqrt(jnp.mean(x * x, axis=-1, keepdims=True) + EPS)
    return (x * r * ng) * (1.0 + scale) + shift


def _modulate_bwd(x, ng, scale, shift, dh):
    r = lax.rsqrt(jnp.mean(x * x, axis=-1, keepdims=True) + EPS)
    xh = x * r
    xg = xh * ng
    h = xg * (1.0 + scale) + shift
    a = dh * (1.0 + scale)
    dxh = a * ng
    dx = r * (dxh - xh * jnp.mean(dxh * xh, axis=-1, keepdims=True))
    return (h, dx, jnp.sum(a * xh, axis=0, keepdims=True), jnp.sum(dh * xg, axis=0, keepdims=True),
            jnp.sum(dh, axis=0, keepdims=True))


def _acc_rows(ref, first, rows):
    @pl.when(first)
    def _():
        ref[...] = jnp.zeros_like(ref)

    for r, val in enumerate(rows):
        ref[r:r + 1, :] += val


def _my_pos():
    return lax.axis_index("x"), lax.axis_index("y"), lax.axis_index("c")


def _all_gather(arrs, name):
    n = len(arrs)

    def body(*refs):
        ins, outs = refs[:n], refs[n:2 * n]
        send, recv, loc = refs[2 * n:]
        x, y, c = _my_pos()
        me, sibling = (x, y, c), (x, y, 1 - c)
        chips = [(1 - x, y), (x, 1 - y), (1 - x, 1 - y)]

        def copy(a, k, block, to, src=None):
            dst = outs[a].at[4 * block[0] + 2 * block[1] + block[2]]
            return pltpu.make_async_remote_copy(
                src_ref=dst if src is None else src, dst_ref=dst,
                send_sem=send.at[7 * a + k], recv_sem=recv.at[7 * a + k],
                device_id=to, device_id_type=MESH)

        mine, first, passed = [], [], []
        for a in range(n):
            m = pltpu.make_async_copy(ins[a], outs[a].at[4 * x + 2 * y + c], loc.at[a])
            m.start()
            mine.append(m)
            f = [copy(a, 0, me, sibling, src=ins[a])]
            f += [copy(a, 1 + j, me, (*chip, c), src=ins[a]) for j, chip in enumerate(chips)]
            for cp in f:
                cp.start()
            first += f
        for a in range(n):
            for j, chip in enumerate(chips):
                copy(a, 1 + j, (*chip, c), me).wait_recv()
                p = copy(a, 4 + j, (*chip, c), sibling)
                p.start()
                passed.append(p)
        for a in range(n):
            copy(a, 0, sibling, me).wait_recv()
            for j, chip in enumerate(chips):
                copy(a, 4 + j, (*chip, 1 - c), me).wait_recv()
        for cp in first + passed:
            cp.wait_send()
        for m in mine:
            m.wait()

    hbm = pl.BlockSpec(memory_space=pl.ANY)
    return _pcall(
        body, name=name,
        out_shape=[_sds((NDEV,) + a.shape, a.dtype) for a in arrs],
        in_specs=[hbm] * n, out_specs=[hbm] * n,
        scratch_shapes=[pltpu.SemaphoreType.DMA((7 * n,)), pltpu.SemaphoreType.DMA((7 * n,)),
                        pltpu.SemaphoreType.DMA((n,))],
    )(*arrs)


def _peer(k):
    x, y, c = _my_pos()
    return (1 - x if k & 4 else x, 1 - y if k & 2 else y, 1 - c if k & 1 else c)


def _dev_index(p):
    return 4 * p[0] + 2 * p[1] + p[2]


_HBM = pl.BlockSpec(memory_space=pltpu.HBM)
_SEM = pl.BlockSpec(memory_space=pltpu.SEMAPHORE)
_EFFECT = pltpu.SideEffectType.DATAFLOW_SIDE_EFFECTING


def _exchange_start(srcs, gather, name):
    n = len(srcs)
    me = _dev_index(_my_pos())
    lands = []
    for s in srcs:
        own = s if gather else lax.dynamic_index_in_dim(s, me, 0, keepdims=False)
        shape = (NDEV,) + s.shape if gather else s.shape
        lands.append(lax.dynamic_update_index_in_dim(lax.empty(shape, s.dtype), own, me, 0))

    def body(*refs):
        src_refs, land_refs = refs[:n], refs[n:2 * n]
        sends, recvs = refs[2 * n:3 * n], refs[3 * n:4 * n]
        token = refs[-1]
        mine = _dev_index(_my_pos())
        for a in range(n):
            for k in range(1, 8):
                p = _peer(k)
                pltpu.make_async_remote_copy(
                    src_ref=src_refs[a] if gather else src_refs[a].at[_dev_index(p)],
                    dst_ref=land_refs[a].at[mine], send_sem=sends[a], recv_sem=recvs[a],
                    device_id=p, device_id_type=MESH).start()
        token[...] = jnp.zeros_like(token)

    out = pl.pallas_call(
        body, name=name,
        out_shape=(*[pltpu.SemaphoreType.DMA(())] * (2 * n),
                   *[pltpu.HBM(a.shape, a.dtype) for a in srcs], *[pltpu.HBM(a.shape, a.dtype) for a in lands],
                   _sds((8, 128), F32)),
        in_specs=[_HBM] * (2 * n),
        out_specs=(*[_SEM] * (2 * n), *[_HBM] * (2 * n), pl.BlockSpec(memory_space=pltpu.VMEM)),
        input_output_aliases={i: 2 * n + i for i in range(2 * n)},
        compiler_params=pltpu.CompilerParams(has_side_effects=_EFFECT),
    )(*[pltpu.with_memory_space_constraint(a, pltpu.HBM) for a in srcs],
      *[pltpu.with_memory_space_constraint(a, pltpu.HBM) for a in lands])
    state = (out[:n], out[n:2 * n], out[2 * n:3 * n], out[3 * n:4 * n])
    return state, out[-1][0, 0]


def _exchange_wait(state, after, name):
    sends, recvs, srcs, lands = state
    n = len(srcs)
    after = list(after) if isinstance(after, (list, tuple)) else [after]

    def body(*refs):
        land_refs = refs[n:2 * n]
        send_refs, recv_refs = refs[2 * n:3 * n], refs[3 * n:4 * n]
        for a in range(n):
            seven = land_refs[a].at[pl.ds(0, NDEV - 1)]
            cp = pltpu.make_async_remote_copy(src_ref=seven, dst_ref=seven, send_sem=send_refs[a], recv_sem=recv_refs[a],
                                              device_id=_peer(1), device_id_type=MESH)
            cp.wait_send()
            cp.wait_recv()

    out = pl.pallas_call(
        body, name=name,
        out_shape=(*[pltpu.HBM(a.shape, a.dtype) for a in srcs], *[pltpu.HBM(a.shape, a.dtype) for a in lands]),
        in_specs=(*[_HBM] * (2 * n), *[_SEM] * (2 * n), *[pl.BlockSpec(memory_space=pl.ANY)] * len(after)),
        out_specs=tuple([_HBM] * (2 * n)),
        input_output_aliases={i: i for i in range(2 * n)},
        compiler_params=pltpu.CompilerParams(has_side_effects=_EFFECT),
    )(*srcs, *lands, *sends, *recvs, *after)
    return list(out[n:])


def _ffn_tiles():
    tm = min(512, T)
    return tm, T // tm


def _ffn_fwd(x, ssg, ng, w_in, w_out):
    n = x.shape[0]
    _, nf, tf, _ = w_in.shape
    tm, tpb = _ffn_tiles()

    def body(x_ref, ssg_ref, ng_ref, win_ref, wout_ref, xn_ref, gu_ref, hid_ref, y_ref, h_scr, acc):
        j = pl.program_id(1)

        @pl.when(j == 0)
        def _():
            s = ssg_ref[0]
            h_scr[...] = _modulate(x_ref[...], ng_ref[...], s[1:2], s[0:1]).astype(BF16)
            acc[...] = jnp.zeros_like(acc)

        h = h_scr[...]
        g = _dot_nt(h, win_ref[0])
        u = _dot_nt(h, win_ref[1])
        gu_ref[0] = g.astype(BF16)
        gu_ref[1] = u.astype(BF16)
        hid = (g * jax.nn.sigmoid(g) * u).astype(BF16)
        hid_ref[...] = hid
        acc[...] += _dot(hid, wout_ref[...])

        @pl.when(j == nf - 1)
        def _():
            yv = acc[...]
            y_ref[...] = yv.astype(BF16)
            xn_ref[...] = x_ref[...] + (0.5 * (1.0 + ssg_ref[0][2:3])) * yv

    return _pcall(
        body, name="ffn_fwd", grid=(n // tm, nf),
        in_specs=[pl.BlockSpec((tm, D), lambda i, j: (i, 0)),
                  pl.BlockSpec((1, 3, D), lambda i, j: (i // tpb, 0, 0)),
                  pl.BlockSpec((1, D), lambda i, j: (0, 0)),
                  pl.BlockSpec((2, None, tf, D), lambda i, j: (0, j, 0, 0)),
                  pl.BlockSpec((None, tf, D), lambda i, j: (j, 0, 0))],
        out_specs=[pl.BlockSpec((tm, D), lambda i, j: (i, 0)),
                   pl.BlockSpec((2, None, tm, tf), lambda i, j: (0, j, i, 0)),
                   pl.BlockSpec((None, tm, tf), lambda i, j: (j, i, 0)),
                   pl.BlockSpec((tm, D), lambda i, j: (i, 0))],
        out_shape=[_sds((n, D), F32), _sds((2, nf, n, tf), BF16), _sds((nf, n, tf), BF16), _sds((n, D), BF16)],
        scratch_shapes=[pltpu.VMEM((tm, D), BF16), pltpu.VMEM((tm, D), F32)],
        compiler_params=_cp(("arbitrary", "arbitrary")),
    )(x, ssg, ng, w_in, w_out)


def _ffn_bwd_a(x, dxn, ssg, ng, y, gu, w_in, w_out):
    n = x.shape[0]
    _, nf, tf, _ = w_in.shape
    tm, tpb = _ffn_tiles()

    def body(x_ref, dxn_ref, ssg_ref, ng_ref, y_ref, gu_ref, win_ref, wout_ref,
             dx_ref, dgu_ref, h_ref, dout_ref, dssg_ref, dng_ref, dout_scr, dh_acc):
        i, j = pl.program_id(0), pl.program_id(1)

        @pl.when(j == 0)
        def _():
            db = ((0.5 * (1.0 + ssg_ref[0][2:3])) * dxn_ref[...]).astype(BF16)
            dout_scr[...] = db
            dout_ref[...] = db
            dh_acc[...] = jnp.zeros_like(dh_acc)

        dhid = _dot_nt(dout_scr[...], wout_ref[...]).astype(BF16)
        g = gu_ref[0]
        u = gu_ref[1]
        sig = jax.nn.sigmoid(g)
        dg = dhid * u * (sig * (1.0 + g * (1.0 - sig)))
        du = dhid * (g * sig)
        dgu_ref[0] = dg
        dgu_ref[1] = du
        dh_acc[...] += _dot(dg, win_ref[0])
        dh_acc[...] += _dot(du, win_ref[1])

        @pl.when(j == nf - 1)
        def _():
            s = ssg_ref[0]
            h, dx_, dng_, dsc_, dsh_ = _modulate_bwd(x_ref[...], ng_ref[...], s[1:2], s[0:1], dh_acc[...])
            h_ref[...] = h.astype(BF16)
            dxn = dxn_ref[...]
            dx_ref[...] = dxn + dx_
            dgate = jnp.sum(0.5 * dxn * y_ref[...].astype(F32), axis=0, keepdims=True)
            _acc_rows(dssg_ref.at[0], i % tpb == 0, [dsh_, dsc_, dgate])
            _acc_rows(dng_ref, i == 0, [dng_])

    return _pcall(
        body, name="ffn_bwd_a", grid=(n // tm, nf),
        in_specs=[pl.BlockSpec((tm, D), lambda i, j: (i, 0)),
                  pl.BlockSpec((tm, D), lambda i, j: (i, 0)),
                  pl.BlockSpec((1, 3, D), lambda i, j: (i // tpb, 0, 0)),
                  pl.BlockSpec((1, D), lambda i, j: (0, 0)),
                  pl.BlockSpec((tm, D), lambda i, j: (i, 0)),
                  pl.BlockSpec((2, None, tm, tf), lambda i, j: (0, j, i, 0)),
                  pl.BlockSpec((2, None, tf, D), lambda i, j: (0, j, 0, 0)),
                  pl.BlockSpec((None, tf, D), lambda i, j: (j, 0, 0))],
        out_specs=[pl.BlockSpec((tm, D), lambda i, j: (i, 0)),
                   pl.BlockSpec((2, None, tm, tf), lambda i, j: (0, j, i, 0)),
                   pl.BlockSpec((tm, D), lambda i, j: (i, 0)),
                   pl.BlockSpec((tm, D), lambda i, j: (i, 0)),
                   pl.BlockSpec((1, 3, D), lambda i, j: (i // tpb, 0, 0)),
                   pl.BlockSpec((1, D), lambda i, j: (0, 0))],
        out_shape=[_sds((n, D), F32), _sds((2, nf, n, tf), BF16), _sds((n, D), BF16), _sds((n, D), BF16),
                   _sds((BL, 3, D), F32), _sds((1, D), F32)],
        scratch_shapes=[pltpu.VMEM((tm, D), BF16), pltpu.VMEM((tm, D), F32)],
        compiler_params=_cp(("arbitrary", "arbitrary")),
    )(x, dxn, ssg, ng, y, gu, w_in, w_out)


def _ffn_bwd_w(h, dgu, hid, dout):
    n = h.shape[0]
    _, nf, _, tf = dgu.shape
    tm, _ = _ffn_tiles()
    ni = n // tm

    def body(h_ref, dgu_ref, hid_ref, dout_ref, dwin_ref, dwout_ref, acc_g, acc_u, acc_o):
        i = pl.program_id(1)

        @pl.when(i == 0)
        def _():
            acc_g[...] = jnp.zeros_like(acc_g)
            acc_u[...] = jnp.zeros_like(acc_u)
            acc_o[...] = jnp.zeros_like(acc_o)

        hv = h_ref[...]
        acc_g[...] += _dot_tn(dgu_ref[0], hv)
        acc_u[...] += _dot_tn(dgu_ref[1], hv)
        acc_o[...] += _dot_tn(hid_ref[...], dout_ref[...])

        @pl.when(i == ni - 1)
        def _():
            dwin_ref[0] = acc_g[...].astype(BF16)
            dwin_ref[1] = acc_u[...].astype(BF16)
            dwout_ref[...] = acc_o[...].astype(BF16)

    return _pcall(
        body, name="ffn_bwd_w", grid=(nf, ni),
        in_specs=[pl.BlockSpec((tm, D), lambda j, i: (i, 0)),
                  pl.BlockSpec((2, None, tm, tf), lambda j, i: (0, j, i, 0)),
                  pl.BlockSpec((None, tm, tf), lambda j, i: (j, i, 0)),
                  pl.BlockSpec((tm, D), lambda j, i: (i, 0))],
        out_specs=[pl.BlockSpec((2, None, tf, D), lambda j, i: (0, j, 0, 0)),
                   pl.BlockSpec((None, tf, D), lambda j, i: (j, 0, 0))],
        out_shape=[_sds((2, nf, tf, D), BF16), _sds((nf, tf, D), BF16)],
        scratch_shapes=[pltpu.VMEM((tf, D), F32), pltpu.VMEM((tf, D), F32), pltpu.VMEM((tf, D), F32)],
        compiler_params=_cp(("arbitrary", "arbitrary")),
    )(h, dgu, hid, dout)


def _premod_matmul(x, ssg, ng, w, bias, tn):
    n = x.shape[0]
    shards = w.ndim == 3
    m = w.shape[0] * w.shape[2] if shards else w.shape[0]
    tm = min(512, T)
    tpb = T // tm
    to = m if shards else tn
    w_spec = (pl.BlockSpec(w.shape, lambda i, j: (0, 0, 0)) if shards
              else pl.BlockSpec((tn, D), lambda i, j: (j, 0)))

    def body(x_ref, ssg_ref, ng_ref, w_ref, b_ref, h_ref, o_ref, h_scr):
        @pl.when(pl.program_id(1) == 0)
        def _():
            s = ssg_ref[0]
            hb = _modulate(x_ref[...], ng_ref[...], s[1:2], s[0:1]).astype(BF16)
            h_scr[...] = hb
            h_ref[...] = hb

        hv = h_scr[...]
        if shards:
            for q in range(w.shape[0]):
                cols = slice(q * tn, (q + 1) * tn)
                o_ref[:, cols] = _dot(hv, w_ref[q]) + b_ref[:, cols]
        else:
            o_ref[...] = _dot_nt(hv, w_ref[...]) + b_ref[...]

    return _pcall(
        body, name="premod_matmul", grid=(n // tm, m // to),
        in_specs=[pl.BlockSpec((tm, D), lambda i, j: (i, 0)),
                  pl.BlockSpec((1, 3, D), lambda i, j: (i // tpb, 0, 0)),
                  pl.BlockSpec((1, D), lambda i, j: (0, 0)),
                  w_spec,
                  pl.BlockSpec((1, to), lambda i, j: (0, j))],
        out_specs=[pl.BlockSpec((tm, D), lambda i, j: (i, 0)),
                   pl.BlockSpec((tm, to), lambda i, j: (i, j))],
        out_shape=[_sds((n, D), BF16), _sds((n, m), F32)],
        scratch_shapes=[pltpu.VMEM((tm, D), BF16)],
        compiler_params=_cp(("arbitrary", "arbitrary")),
    )(x, ssg, ng, w, bias)


def _premod_matmul_bwd(x, dxn, ssg, ng, douts, w):
    n = x.shape[0]
    k = len(douts)
    shards = w.ndim == 3
    tm = min(512, T)
    tpb = T // tm

    def body(*refs):
        x_ref, dxn_ref, ssg_ref, ng_ref = refs[:4]
        do_refs, w_ref = refs[4:4 + k], refs[4 + k]
        dx_ref, dssg_ref, dng_ref = refs[5 + k:]
        i = pl.program_id(0)
        dh = jnp.zeros((tm, D), F32)
        if shards:
            cs = w.shape[2]
            dov = do_refs[0][...]
            for j in range(w.shape[0]):
                dh += _dot_nt(dov[:, j * cs:(j + 1) * cs], w_ref[j])
        else:
            off = 0
            for q in range(k):
                mk = douts[q].shape[1]
                dh += _dot(do_refs[q][...], w_ref[off:off + mk, :])
                off += mk
        s = ssg_ref[0]
        _, dx_, dng_, dsc_, dsh_ = _modulate_bwd(x_ref[...], ng_ref[...], s[1:2], s[0:1], dh)
        dx_ref[...] = dxn_ref[...] + dx_
        _acc_rows(dssg_ref.at[0], i % tpb == 0, [dsh_, dsc_, jnp.zeros_like(dsh_)])
        _acc_rows(dng_ref, i == 0, [dng_])

    return _pcall(
        body, name="premod_matmul_bwd", grid=(n // tm,),
        in_specs=[pl.BlockSpec((tm, D), lambda i: (i, 0)),
                  pl.BlockSpec((tm, D), lambda i: (i, 0)),
                  pl.BlockSpec((1, 3, D), lambda i: (i // tpb, 0, 0)),
                  pl.BlockSpec((1, D), lambda i: (0, 0))]
                 + [pl.BlockSpec((tm, a.shape[1]), lambda i: (i, 0)) for a in douts]
                 + [pl.BlockSpec(w.shape, (lambda i: (0, 0, 0)) if shards else (lambda i: (0, 0)))],
        out_specs=[pl.BlockSpec((tm, D), lambda i: (i, 0)),
                   pl.BlockSpec((1, 3, D), lambda i: (i // tpb, 0, 0)),
                   pl.BlockSpec((1, D), lambda i: (0, 0))],
        out_shape=[_sds((n, D), F32), _sds((BL, 3, D), F32), _sds((1, D), F32)],
        compiler_params=_cp(("arbitrary",)),
    )(x, dxn, ssg, ng, *douts, w)


def _matmul_res(x, a, ssg, w, bias):
    n, kd = a.shape
    tm = min(512, T)
    tpb = T // tm

    def body(x_ref, a_ref, ssg_ref, w_ref, b_ref, xn_ref, y_ref):
        yv = _dot(a_ref[...], w_ref[...]) + b_ref[...]
        y_ref[...] = yv.astype(BF16)
        xn_ref[...] = x_ref[...] + (1.0 + ssg_ref[0][2:3]) * yv

    return _pcall(
        body, name="matmul_res", grid=(n // tm,),
        in_specs=[pl.BlockSpec((tm, D), lambda i: (i, 0)),
                  pl.BlockSpec((tm, kd), lambda i: (i, 0)),
                  pl.BlockSpec((1, 3, D), lambda i: (i // tpb, 0, 0)),
                  pl.BlockSpec((kd, D), lambda i: (0, 0)),
                  pl.BlockSpec((1, D), lambda i: (0, 0))],
        out_specs=[pl.BlockSpec((tm, D), lambda i: (i, 0)), pl.BlockSpec((tm, D), lambda i: (i, 0))],
        out_shape=[_sds((n, D), F32), _sds((n, D), BF16)],
        compiler_params=_cp(("arbitrary",)),
    )(x, a, ssg, w, bias)


def _matmul_res_bwd(dxn, y, ssg, w):
    n = dxn.shape[0]
    kd = w.shape[0]
    tm = min(512, T)
    tpb = T // tm

    def body(dxn_ref, y_ref, ssg_ref, w_ref, da_ref, dy_ref, dgate_ref, dbias_ref):
        i = pl.program_id(0)
        dxn = dxn_ref[...]
        dy = (1.0 + ssg_ref[0][2:3]) * dxn
        dyb = dy.astype(BF16)
        dy_ref[...] = dyb
        da_ref[...] = _dot_nt(dyb, w_ref[...])
        _acc_rows(dgate_ref.at[0], i % tpb == 0, [jnp.sum(dxn * y_ref[...].astype(F32), axis=0, keepdims=True)])
        _acc_rows(dbias_ref, i == 0, [jnp.sum(dy, axis=0, keepdims=True)])

    return _pcall(
        body, name="matmul_res_bwd", grid=(n // tm,),
        in_specs=[pl.BlockSpec((tm, D), lambda i: (i, 0)),
                  pl.BlockSpec((tm, D), lambda i: (i, 0)),
                  pl.BlockSpec((1, 3, D), lambda i: (i // tpb, 0, 0)),
                  pl.BlockSpec((kd, D), lambda i: (0, 0))],
        out_specs=[pl.BlockSpec((tm, kd), lambda i: (i, 0)),
                   pl.BlockSpec((tm, D), lambda i: (i, 0)),
                   pl.BlockSpec((1, 1, D), lambda i: (i // tpb, 0, 0)),
                   pl.BlockSpec((1, D), lambda i: (0, 0))],
        out_shape=[_sds((n, kd), F32), _sds((n, D), BF16), _sds((BL, 1, D), F32), _sds((1, D), F32)],
        compiler_params=_cp(("arbitrary",)),
    )(dxn, y, ssg, w)


def _wgrad_shards(a, b, ns):
    n, kd = a.shape
    cs = b.shape[1] // ns
    tm = min(512, T)
    ni = n // tm

    def body(a_ref, b_ref, o_ref, acc):
        i = pl.program_id(0)

        @pl.when(i == 0)
        def _():
            acc[...] = jnp.zeros_like(acc)

        at = a_ref[...].T
        for q in range(ns):
            acc[q] += _dot(at, b_ref[:, q * cs:(q + 1) * cs])

        @pl.when(i == ni - 1)
        def _():
            o_ref[...] = acc[...].astype(BF16)

    return _pcall(
        body, name="wgrad_shards", grid=(ni,),
        in_specs=[pl.BlockSpec((tm, kd), lambda i: (i, 0)), pl.BlockSpec((tm, ns * cs), lambda i: (i, 0))],
        out_specs=pl.BlockSpec((ns, kd, cs), lambda i: (0, 0, 0)),
        out_shape=_sds((ns, kd, cs), BF16),
        scratch_shapes=[pltpu.VMEM((ns, kd, cs), F32)],
        compiler_params=_cp(("arbitrary",)),
    )(a, b)


def _wgrad(a, b):
    n, kd = a.shape
    m = b.shape[1]
    tm = min(512, T)
    tk = min(512, kd)
    ni = n // tm

    def body(a_ref, b_ref, o_ref, acc):
        i = pl.program_id(1)

        @pl.when(i == 0)
        def _():
            acc[...] = jnp.zeros_like(acc)

        acc[...] += _dot_tn(a_ref[...], b_ref[...])

        @pl.when(i == ni - 1)
        def _():
            o_ref[...] = acc[...].astype(BF16)

    return _pcall(
        body, name="wgrad", grid=(kd // tk, ni),
        in_specs=[pl.BlockSpec((tm, tk), lambda q, i: (i, q)), pl.BlockSpec((tm, m), lambda q, i: (i, 0))],
        out_specs=pl.BlockSpec((tk, m), lambda q, i: (q, 0)),
        out_shape=_sds((kd, m), BF16),
        scratch_shapes=[pltpu.VMEM((tk, m), F32)],
        compiler_params=_cp(("arbitrary", "arbitrary")),
    )(a, b)


def _ln_silu(u1, g, b):
    mu = jnp.mean(u1, axis=-1, keepdims=True)
    xc = u1 - mu
    var = jnp.mean(xc * xc, axis=-1, keepdims=True)
    ln = xc * lax.rsqrt(var + EPS) * g + b
    return ln * jax.nn.sigmoid(ln)


def _conv_tiles():
    tt = min(256, T)
    return tt, T // tt


def _prev_halo_spec(cols, tt, halo):
    r = tt // halo
    return pl.BlockSpec((halo, cols), lambda b, i: (jnp.maximum(b * (T // halo) + i * r - 1, 0), 0))


def _next_halo_spec(cols, tt, halo):
    r = tt // halo
    last = BL * T // halo - 1
    return pl.BlockSpec((halo, cols), lambda b, i: (jnp.minimum(b * (T // halo) + (i + 1) * r, last), 0))


ROWS = 32
SROWS = 8


def _fill_rotations(rot, win, rows):
    for r in range(8):
        rot[r, 0:rows, :] = win[pl.ds(r, rows), :]


def _window(rot, off, start, size):
    return rot[off % 8, pl.ds(pl.multiple_of(start + (off // 8) * 8, 8), size), :]


def _cm_mid_fwd(ab, w_dw, b_dw, ln_g, ln_b):
    n = ab.shape[0]
    tt, nt = _conv_tiles()

    def body(ab_ref, halo_ref, w_ref, bdw_ref, g_ref, b_ref, u1_ref, u2_ref, win, rot):
        i = pl.program_id(1)
        hv = halo_ref[...]
        u0h = hv[:, :D] * jax.nn.sigmoid(hv[:, D:])
        win[0:HALO, :] = jnp.where(i == 0, 0.0, u0h)
        cv = ab_ref[...]
        win[HALO:HALO + tt, :] = cv[:, :D] * jax.nn.sigmoid(cv[:, D:])
        win[HALO + tt:, :] = jnp.zeros((8, D), F32)
        _fill_rotations(rot, win, tt + HALO)

        def chunk(c, carry):
            r0 = pl.multiple_of(c * ROWS, ROWS)
            acc = jnp.zeros((ROWS, D), F32) + bdw_ref[...]
            for k in range(CW):
                acc += w_ref[k:k + 1, :] * _window(rot, HALO - (CW - 1) + k, r0, ROWS)
            u1_ref[pl.ds(r0, ROWS), :] = acc
            u2_ref[pl.ds(r0, ROWS), :] = _ln_silu(acc, g_ref[...], b_ref[...]).astype(BF16)
            return carry

        lax.fori_loop(0, tt // ROWS, chunk, 0)

    row = lambda b, i: (b * nt + i, 0)
    vec = pl.BlockSpec((1, D), lambda b, i: (0, 0))
    return _pcall(
        body, name="cm_mid_fwd", grid=(BL, nt),
        in_specs=[pl.BlockSpec((tt, 2 * D), row), _prev_halo_spec(2 * D, tt, HALO),
                  pl.BlockSpec((HALO, D), lambda b, i: (0, 0)), vec, vec, vec],
        out_specs=[pl.BlockSpec((tt, D), row), pl.BlockSpec((tt, D), row)],
        out_shape=[_sds((n, D), F32), _sds((n, D), BF16)],
        scratch_shapes=[pltpu.VMEM((HALO + tt + 8, D), F32), pltpu.VMEM((8, tt + HALO, D), F32)],
        compiler_params=_cp(("arbitrary", "arbitrary")),
    )(ab, ab, w_dw, b_dw, ln_g, ln_b)


def _cm_mid_bwd_a(du2, u1, ln_g, ln_b):
    n = du2.shape[0]
    tm = min(256, T)

    def body(du2_ref, u1_ref, g_ref, b_ref, du1_ref, dln_ref):
        _, vjp = jax.vjp(_ln_silu, u1_ref[...], g_ref[...], b_ref[...])
        du1, dg, db = vjp(du2_ref[...])
        du1_ref[...] = du1
        _acc_rows(dln_ref, pl.program_id(0) == 0, [dg, db])

    vec = pl.BlockSpec((1, D), lambda i: (0, 0))
    return _pcall(
        body, name="cm_mid_bwd_a", grid=(n // tm,),
        in_specs=[pl.BlockSpec((tm, D), lambda i: (i, 0)), pl.BlockSpec((tm, D), lambda i: (i, 0)), vec, vec],
        out_specs=[pl.BlockSpec((tm, D), lambda i: (i, 0)), pl.BlockSpec((2, D), lambda i: (0, 0))],
        out_shape=[_sds((n, D), F32), _sds((2, D), F32)],
        compiler_params=_cp(("arbitrary",)),
    )(du2, u1, ln_g, ln_b)


def _cm_mid_bwd_b(du1, ab, w_dw):
    n = du1.shape[0]
    tt, nt = _conv_tiles()

    def body(du1_ref, nxt_ref, ab_ref, halo_ref, w_ref, dab_ref, dw_ref, dbdw_ref, dbglu_ref,
             dwin, uwin, rotd, rotu, accw, accv):
        b, i = pl.program_id(0), pl.program_id(1)
        first = jnp.logical_and(b == 0, i == 0)
        dwin[0:tt, :] = du1_ref[...]
        dwin[tt:tt + HALO, :] = jnp.where(i == nt - 1, 0.0, nxt_ref[...])
        dwin[tt + HALO:, :] = jnp.zeros((8, D), F32)
        hv = halo_ref[...]
        uwin[0:HALO, :] = jnp.where(i == 0, 0.0, hv[:, :D] * jax.nn.sigmoid(hv[:, D:]))
        cv = ab_ref[...]
        uwin[HALO:HALO + tt, :] = cv[:, :D] * jax.nn.sigmoid(cv[:, D:])
        uwin[HALO + tt:, :] = jnp.zeros((8, D), F32)
        _fill_rotations(rotd, dwin, tt + HALO)
        _fill_rotations(rotu, uwin, tt + HALO)
        accw[...] = jnp.zeros_like(accw)
        accv[...] = jnp.zeros_like(accv)

        def fold(v):
            return jnp.sum(v.reshape(ROWS // 8, 8, D), axis=0)

        def chunk(c, carry):
            r0 = pl.multiple_of(c * ROWS, ROWS)
            d1 = du1_ref[pl.ds(r0, ROWS), :]
            du0 = jnp.zeros((ROWS, D), F32)
            for k in range(CW):
                du0 += w_ref[k:k + 1, :] * _window(rotd, CW - 1 - k, r0, ROWS)
                accw[k] += fold(d1 * _window(rotu, HALO - (CW - 1) + k, r0, ROWS))
            cvc = ab_ref[pl.ds(r0, ROWS), :]
            av, sg = cvc[:, :D], jax.nn.sigmoid(cvc[:, D:])
            da = du0 * sg
            db = du0 * av * sg * (1.0 - sg)
            dab_ref[pl.ds(r0, ROWS), 0:D] = da.astype(BF16)
            dab_ref[pl.ds(r0, ROWS), D:2 * D] = db.astype(BF16)
            accv[0] += fold(d1)
            accv[1] += fold(da)
            accv[2] += fold(db)
            return carry

        lax.fori_loop(0, tt // ROWS, chunk, 0)
        dws = [jnp.sum(accw[k], axis=0, keepdims=True) for k in range(CW)]
        dws += [jnp.zeros((1, D), F32)] * (HALO - CW)
        _acc_rows(dw_ref, first, dws)
        _acc_rows(dbdw_ref, first, [jnp.sum(accv[0], axis=0, keepdims=True)])
        _acc_rows(dbglu_ref.at[:, 0:D], first, [jnp.sum(accv[1], axis=0, keepdims=True)])
        _acc_rows(dbglu_ref.at[:, D:2 * D], first, [jnp.sum(accv[2], axis=0, keepdims=True)])

    row = lambda b, i: (b * nt + i, 0)
    return _pcall(
        body, name="cm_mid_bwd_b", grid=(BL, nt),
        in_specs=[pl.BlockSpec((tt, D), row), _next_halo_spec(D, tt, HALO),
                  pl.BlockSpec((tt, 2 * D), row), _prev_halo_spec(2 * D, tt, HALO),
                  pl.BlockSpec((HALO, D), lambda b, i: (0, 0))],
        out_specs=[pl.BlockSpec((tt, 2 * D), row), pl.BlockSpec((HALO, D), lambda b, i: (0, 0)),
                   pl.BlockSpec((1, D), lambda b, i: (0, 0)), pl.BlockSpec((1, 2 * D), lambda b, i: (0, 0))],
        out_shape=[_sds((n, 2 * D), BF16), _sds((HALO, D), F32), _sds((1, D), F32), _sds((1, 2 * D), F32)],
        scratch_shapes=[pltpu.VMEM((tt + HALO + 8, D), F32), pltpu.VMEM((HALO + tt + 8, D), F32),
                        pltpu.VMEM((8, tt + HALO, D), F32), pltpu.VMEM((8, tt + HALO, D), F32),
                        pltpu.VMEM((HALO, 8, D), F32), pltpu.VMEM((3, 8, D), F32)],
        compiler_params=_cp(("arbitrary", "arbitrary")),
    )(du1, du1, ab, ab, w_dw)


def _softplus(v):
    return jnp.maximum(v, 0.0) + jnp.log(1.0 + jnp.exp(-jnp.abs(v)))


def _g_beta(ab, alog, dtb):
    return -jnp.exp(alog) * _softplus(ab + dtb), jax.nn.sigmoid(ab)


def _dn_sconv_fwd(proj, w_sc, alog, dtb):
    n = proj.shape[0]
    tt, nt = _conv_tiles()
    w3 = 3 * D

    def body(qkv_ref, halo_ref, ab_ref, w_ref, alog_ref, dtb_ref, conv_ref, q_ref, k_ref, v_ref, gb_ref, bb_ref,
             win, rot, gsc, bsc):
        i = pl.program_id(1)
        win[0:SHALO, :] = jnp.where(i == 0, 0.0, halo_ref[...])
        win[SHALO:SHALO + tt, :] = qkv_ref[...]
        for k in range(SCW - 1):
            rot[k] = win[pl.ds(SHALO - (SCW - 1) + k, tt), :]
        gsc[...], bsc[...] = _g_beta(ab_ref[...], alog_ref[...], dtb_ref[...])

        def chunk(c, carry):
            rows = pl.ds(pl.multiple_of(c * SROWS, SROWS), SROWS)
            acc = w_ref[SCW - 1:SCW, :] * win[pl.ds(pl.multiple_of(c * SROWS + SHALO, SROWS), SROWS), :]
            for k in range(SCW - 1):
                acc += w_ref[k:k + 1, :] * rot[k, rows, :]
            conv_ref[rows, :] = acc
            act = acc * jax.nn.sigmoid(acc)
            gfull, bfull = gsc[rows, :], bsc[rows, :]
            for h in range(NH):
                q_ref[0, h, rows, :] = act[:, h * DH:(h + 1) * DH]
                k_ref[0, h, rows, :] = act[:, D + h * DH:D + (h + 1) * DH]
                v_ref[0, h, rows, :] = act[:, 2 * D + h * DH:2 * D + (h + 1) * DH]
                gb_ref[0, h, rows, :] = jnp.broadcast_to(gfull[:, h:h + 1], (SROWS, DH))
                bb_ref[0, h, rows, :] = jnp.broadcast_to(bfull[:, NH + h:NH + h + 1], (SROWS, DH))
            return carry

        lax.fori_loop(0, tt // SROWS, chunk, 0)

    row = lambda b, i: (b * nt + i, 0)
    head = pl.BlockSpec((1, NH, tt, DH), lambda b, i: (b, 0, i, 0))
    vec = pl.BlockSpec((1, 128), lambda b, i: (0, 0))
    hs = _sds((BL, NH, T, DH), F32)
    return _pcall(
        body, name="dn_sconv_fwd", grid=(BL, nt),
        in_specs=[pl.BlockSpec((tt, w3), row), _prev_halo_spec(w3, tt, SHALO),
                  pl.BlockSpec((tt, 128), lambda b, i: (b * nt + i, 4 * D // 128)),
                  pl.BlockSpec((SHALO, w3), lambda b, i: (0, 0)), vec, vec],
        out_specs=[pl.BlockSpec((tt, w3), row), head, head, head, head, head],
        out_shape=[_sds((n, w3), F32), hs, hs, hs, hs, hs],
        scratch_shapes=[pltpu.VMEM((SHALO + tt, w3), F32), pltpu.VMEM((SCW - 1, tt, w3), F32),
                        pltpu.VMEM((tt, 128), F32), pltpu.VMEM((tt, 128), F32)],
        compiler_params=_cp(("arbitrary", "arbitrary")),
    )(proj, proj, proj, w_sc, alog, dtb)


_BMM_SPEC = {"nn": "gij,gjk->gik", "nt": "gid,gjd->gij", "tn": "gcd,gce->gde"}


def _mm(kind, a, b, prec):
    if prec is None:
        return jnp.einsum(_BMM_SPEC[kind], a.astype(BF16), b.astype(BF16), preferred_element_type=F32)
    return jnp.einsum(_BMM_SPEC[kind], a, b, preferred_element_type=F32, precision=prec)


@functools.partial(jax.custom_vjp, nondiff_argnums=(0, 3))
def _bmm_k(kind, a, b, prec):
    return _mm(kind, a, b, prec)


def _bmm_k_fwd(kind, a, b, prec):
    return _mm(kind, a, b, prec), (a, b)


def _bmm_k_bwd(kind, prec, res, dc):
    a, b = res
    if kind == "nn":
        return _bmm_k("nt", dc, b, prec), _bmm_k("tn", a, dc, prec)
    if kind == "nt":
        return _bmm_k("nn", dc, b, prec), _bmm_k("tn", dc, a, prec)
    return _bmm_k("nt", b, dc, prec), _bmm_k("nn", a, dc, prec)


_bmm_k.defvjp(_bmm_k_fwd, _bmm_k_bwd)


def _bmm(a, b, prec=None):
    return _bmm_k("nn", a, b, prec)


def _bmm_nt(a, b, prec=None):
    return _bmm_k("nt", a, b, prec)


def _bmm_tn(a, b, prec=None):
    return _bmm_k("tn", a, b, prec)


def _bmm_raw(a, b):
    return _mm("nn", a, b, None)


def _bmm_nt_raw(a, b):
    return _mm("nt", a, b, None)


def _bmm_tn_raw(a, b):
    return _mm("tn", a, b, None)


@jax.custom_vjp
def _unit_lower_inverse(a):
    eye = (lax.broadcasted_iota(jnp.int32, a.shape, 1) == lax.broadcasted_iota(jnp.int32, a.shape, 2)).astype(F32)
    t = eye - a
    p = a
    for _ in range(CHUNK.bit_length() - 2):
        p = _mm("nn", p, p, INV_PREC)
        t = _mm("nn", t, eye + p, INV_PREC)
    return t


def _uli_fwd(a):
    t = _unit_lower_inverse(a)
    return t, t


def _uli_bwd(t, dt):
    return (-_bmm_nt(_bmm_tn(t, dt, lax.Precision.HIGH), t, lax.Precision.HIGH),)


_unit_lower_inverse.defvjp(_uli_fwd, _uli_bwd)


@jax.custom_vjp
def _known_inverse(a, t):
    return t


_known_inverse.defvjp(lambda a, t: (t, t), lambda t, dt: (_uli_bwd(t, dt)[0], jnp.zeros_like(t)))


def _dn_pre(q, k, v, gb, bb, tm_known=None):
    shape = (q.shape[0], CHUNK, CHUNK)
    ri = lax.broadcasted_iota(jnp.int32, shape, 1)
    ci = lax.broadcasted_iota(jnp.int32, shape, 2)
    causal, strict = ri >= ci, ri > ci
    qn = q * lax.rsqrt(jnp.sum(q * q, axis=-1, keepdims=True) + EPS) * (DH ** -0.5)
    kn = k * lax.rsqrt(jnp.sum(k * k, axis=-1, keepdims=True) + EPS)
    gcs = _bmm(causal.astype(F32), gb, HI)
    gcol = gcs[:, :, :CHUNK]
    decay = jnp.exp(jnp.where(causal, gcol - jnp.swapaxes(gcol, 1, 2), -jnp.inf))
    eg = jnp.exp(gcs)
    kb = kn * bb
    a = jnp.where(strict, _bmm_nt(kb, kn) * decay, 0.0)
    tm = _unit_lower_inverse(a) if tm_known is None else _known_inverse(a, tm_known)
    u = _bmm(tm, v * bb)
    w = _bmm(tm, kb * eg)
    qg = qn * eg
    intra = _bmm_nt(qn, kn) * decay
    glast = gcs[:, CHUNK - 1:CHUNK, :]
    kd = kn * jnp.exp(glast - gcs)
    egl = jnp.broadcast_to(jnp.exp(glast), (q.shape[0], 8, DH))
    return u, w, qg, kd, intra, egl, tm


def _pre_tiles():
    gcn = min(16, T // CHUNK)
    return gcn, T // (CHUNK * gcn)


def _dn_pre_specs():
    gcn, _ = _pre_tiles()
    tok = pl.BlockSpec((None, None, gcn * CHUNK, DH), lambda b, h, i: (b, h, i, 0))
    sq = pl.BlockSpec((None, None, gcn * CHUNK, CHUNK), lambda b, h, i: (b, h, i, 0))
    per = pl.BlockSpec((None, None, gcn * 8, DH), lambda b, h, i: (b, h, i, 0))
    return tok, sq, per


def _dn_pre_fwd(q, k, v, gb, bb):
    gcn, ng = _pre_tiles()
    tok, sq, per = _dn_pre_specs()

    def body(q_ref, k_ref, v_ref, gb_ref, bb_ref, u_ref, w_ref, qg_ref, kd_ref, in_ref, egl_ref, tinv_ref):
        args = [r[...].reshape(gcn, CHUNK, DH) for r in (q_ref, k_ref, v_ref, gb_ref, bb_ref)]
        u, w, qg, kd, intra, egl, tinv = _dn_pre(*args)
        for r, val in ((u_ref, u), (w_ref, w), (qg_ref, qg), (kd_ref, kd)):
            r[...] = val.reshape(gcn * CHUNK, DH)
        in_ref[...] = intra.reshape(gcn * CHUNK, CHUNK)
        tinv_ref[...] = tinv.reshape(gcn * CHUNK, CHUNK)
        egl_ref[...] = egl.reshape(gcn * 8, DH)

    hs = _sds((BL, NH, T, DH), F32)
    sqs = _sds((BL, NH, T, CHUNK), F32)
    return _pcall(
        body, name="dn_pre_fwd", grid=(BL, NH, ng),
        in_specs=[tok] * 5, out_specs=[tok, tok, tok, tok, sq, per, sq],
        out_shape=[hs, hs, hs, hs, sqs, _sds((BL, NH, T // CHUNK * 8, DH), F32), sqs],
        compiler_params=_cp(("arbitrary",) * 3),
    )(q, k, v, gb, bb)


def _dn_pre_bwd(q, k, v, gb, bb, tinv, du, dw, dqg, dkd, dintra, degl):
    gcn, ng = _pre_tiles()
    tok, sq, per = _dn_pre_specs()

    def body(q_ref, k_ref, v_ref, gb_ref, bb_ref, tinv_ref, du_ref, dw_ref, dqg_ref, dkd_ref, din_ref, degl_ref,
             dq_ref, dk_ref, dv_ref, dgb_ref, dbb_ref):
        args = [r[...].reshape(gcn, CHUNK, DH) for r in (q_ref, k_ref, v_ref, gb_ref, bb_ref)]
        known = tinv_ref[...].reshape(gcn, CHUNK, CHUNK)
        _, vjp = jax.vjp(lambda *a: _dn_pre(*a, tm_known=known)[:6], *args)
        cts = [r[...].reshape(gcn, CHUNK, DH) for r in (du_ref, dw_ref, dqg_ref, dkd_ref)]
        de = degl_ref[...].reshape(gcn, 8, DH)
        one = jnp.logical_and(lax.broadcasted_iota(jnp.int32, de.shape, 1) == 0,
                              lax.broadcasted_iota(jnp.int32, de.shape, 2) == 0)
        outs = vjp((*cts, din_ref[...].reshape(gcn, CHUNK, CHUNK), jnp.where(one, de, 0.0)))
        for r, val in zip((dq_ref, dk_ref, dv_ref, dgb_ref, dbb_ref), outs):
            r[...] = val.reshape(gcn * CHUNK, DH)

    hs = _sds((BL, NH, T, DH), F32)
    return _pcall(
        body, name="dn_pre_bwd", grid=(BL, NH, ng),
        in_specs=[tok] * 5 + [sq] + [tok] * 4 + [sq, per], out_specs=[tok] * 5, out_shape=[hs] * 5,
        compiler_params=_cp(("arbitrary",) * 3),
    )(q, k, v, gb, bb, tinv, du, dw, dqg, dkd, dintra, degl)


def _scan_tiles():
    cs = min(2, T // CHUNK)
    return cs, T // (CHUNK * cs)


def _dn_scan_fwd(u, w, qg, kd, intra, egl):
    cs, ns = _scan_tiles()
    g = BL * NH
    nc = T // CHUNK

    def body(u_ref, w_ref, qg_ref, kd_ref, in_ref, egl_ref, o_ref, vn_ref, s0_ref, s_scr):
        @pl.when(pl.program_id(0) == 0)
        def _():
            s_scr[...] = jnp.zeros_like(s_scr)

        for c in range(cs):
            rows = pl.ds(c * CHUNK, CHUNK)
            s = s_scr[...]
            s0_ref[:, :, c] = s.reshape(BL, NH, DH, DH)

            def ld(r, m=DH):
                return r[:, :, rows, :].reshape(g, CHUNK, m)

            vn = ld(u_ref) - _bmm_raw(ld(w_ref), s)
            o = _bmm_raw(ld(qg_ref), s) + _bmm_raw(ld(in_ref, CHUNK), vn)
            e = egl_ref[:, :, pl.ds(c * 8, 1), :].reshape(g, 1, DH)
            s_scr[...] = s * e + _bmm_tn_raw(ld(kd_ref), vn)
            vn_ref[:, :, rows, :] = vn.reshape(BL, NH, CHUNK, DH)
            o_ref[:, :, rows, :] = o.reshape(BL, NH, CHUNK, DH)

    tok = pl.BlockSpec((BL, NH, cs * CHUNK, DH), lambda i: (0, 0, i, 0))
    hs = _sds((BL, NH, T, DH), F32)
    return _pcall(
        body, name="dn_scan_fwd", grid=(ns,),
        in_specs=[tok, tok, tok, tok, pl.BlockSpec((BL, NH, cs * CHUNK, CHUNK), lambda i: (0, 0, i, 0)),
                  pl.BlockSpec((BL, NH, cs * 8, DH), lambda i: (0, 0, i, 0))],
        out_specs=[tok, tok, pl.BlockSpec((BL, NH, cs, DH, DH), lambda i: (0, 0, i, 0, 0))],
        out_shape=[hs, hs, _sds((BL, NH, nc, DH, DH), F32)],
        scratch_shapes=[pltpu.VMEM((g, DH, DH), F32)],
        compiler_params=_cp(("arbitrary",)),
    )(u, w, qg, kd, intra, egl)


def _dn_scan_bwd(do, w, qg, kd, intra, egl, vn, s0):
    cs, ns = _scan_tiles()
    g = BL * NH
    nc = T // CHUNK

    def body(do_ref, w_ref, qg_ref, kd_ref, in_ref, egl_ref, vn_ref, s0_ref,
             du_ref, dw_ref, dqg_ref, dkd_ref, din_ref, degl_ref, ds_scr):
        @pl.when(pl.program_id(0) == 0)
        def _():
            ds_scr[...] = jnp.zeros_like(ds_scr)

        for c in reversed(range(cs)):
            rows = pl.ds(c * CHUNK, CHUNK)

            def ld(r, m=DH):
                return r[:, :, rows, :].reshape(g, CHUNK, m)

            def st(r, val, m=DH):
                r[:, :, rows, :] = val.reshape(BL, NH, CHUNK, m)

            s = s0_ref[:, :, c].reshape(g, DH, DH)
            ds = ds_scr[...]
            dov, vnv, kdv, wv, qgv, inv = ld(do_ref), ld(vn_ref), ld(kd_ref), ld(w_ref), ld(qg_ref), ld(in_ref, CHUNK)
            dv = _bmm_tn_raw(inv, dov) + _bmm_raw(kdv, ds)
            st(din_ref, _bmm_nt_raw(dov, vnv), CHUNK)
            st(dqg_ref, _bmm_nt_raw(dov, s))
            st(dkd_ref, _bmm_nt_raw(vnv, ds))
            st(du_ref, dv)
            st(dw_ref, -_bmm_nt_raw(dv, s))
            de = jnp.sum(jnp.sum(ds * s, axis=2, keepdims=True), axis=1, keepdims=True)
            degl_ref[:, :, pl.ds(c * 8, 8), :] = jnp.broadcast_to(de, (g, 8, DH)).reshape(BL, NH, 8, DH)
            e = egl_ref[:, :, pl.ds(c * 8, 1), :].reshape(g, 1, DH)
            ds_scr[...] = ds * e + _bmm_tn_raw(qgv, dov) - _bmm_tn_raw(wv, dv)

    rev = lambda i: (0, 0, ns - 1 - i, 0)
    tok = pl.BlockSpec((BL, NH, cs * CHUNK, DH), rev)
    sq = pl.BlockSpec((BL, NH, cs * CHUNK, CHUNK), rev)
    per = pl.BlockSpec((BL, NH, cs * 8, DH), rev)
    hs = _sds((BL, NH, T, DH), F32)
    return _pcall(
        body, name="dn_scan_bwd", grid=(ns,),
        in_specs=[tok, tok, tok, tok, sq, per, tok,
                  pl.BlockSpec((BL, NH, cs, DH, DH), lambda i: (0, 0, ns - 1 - i, 0, 0))],
        out_specs=[tok, tok, tok, tok, sq, per],
        out_shape=[hs, hs, hs, hs, _sds((BL, NH, T, CHUNK), F32), _sds((BL, NH, nc * 8, DH), F32)],
        scratch_shapes=[pltpu.VMEM((g, DH, DH), F32)],
        compiler_params=_cp(("arbitrary",)),
    )(do, w, qg, kd, intra, egl, vn, s0)


def _gated_norm(o_h, z_h, og):
    r = lax.rsqrt(jnp.mean(o_h * o_h, axis=-1, keepdims=True) + EPS)
    return (o_h * r * og) * (z_h * jax.nn.sigmoid(z_h))


def _dn_gnorm_fwd(o, proj, o_g):
    tm = min(256, T)
    nt = T // tm

    def body(o_ref, z_ref, g_ref, og_ref):
        z = z_ref[...]
        for h in range(NH):
            og_ref[:, h * DH:(h + 1) * DH] = _gated_norm(o_ref[0, h], z[:, h * DH:(h + 1) * DH], g_ref[...]).astype(BF16)

    return _pcall(
        body, name="dn_gnorm_fwd", grid=(BL, nt),
        in_specs=[pl.BlockSpec((1, NH, tm, DH), lambda b, i: (b, 0, i, 0)),
                  pl.BlockSpec((tm, D), lambda b, i: (b * nt + i, 3)),
                  pl.BlockSpec((1, DH), lambda b, i: (0, 0))],
        out_specs=pl.BlockSpec((tm, D), lambda b, i: (b * nt + i, 0)),
        out_shape=_sds((BL * T, D), BF16),
        compiler_params=_cp(("arbitrary", "arbitrary")),
    )(o, proj, o_g)


def _dn_gnorm_bwd(dog, o, proj, o_g):
    tm = min(256, T)
    nt = T // tm

    def body(dog_ref, o_ref, z_ref, g_ref, do_ref, dz_ref, dg_ref):
        z = z_ref[...]
        dog = dog_ref[...]
        dg = jnp.zeros((1, DH), F32)
        for h in range(NH):
            cols = slice(h * DH, (h + 1) * DH)
            _, vjp = jax.vjp(_gated_norm, o_ref[0, h], z[:, cols], g_ref[...])
            do_h, dz_h, dg_h = vjp(dog[:, cols])
            do_ref[0, h] = do_h
            dz_ref[:, cols] = dz_h.astype(BF16)
            dg += dg_h
        _acc_rows(dg_ref, jnp.logical_and(pl.program_id(0) == 0, pl.program_id(1) == 0), [dg])

    return _pcall(
        body, name="dn_gnorm_bwd", grid=(BL, nt),
        in_specs=[pl.BlockSpec((tm, D), lambda b, i: (b * nt + i, 0)),
                  pl.BlockSpec((1, NH, tm, DH), lambda b, i: (b, 0, i, 0)),
                  pl.BlockSpec((tm, D), lambda b, i: (b * nt + i, 3)),
                  pl.BlockSpec((1, DH), lambda b, i: (0, 0))],
        out_specs=[pl.BlockSpec((1, NH, tm, DH), lambda b, i: (b, 0, i, 0)),
                   pl.BlockSpec((tm, D), lambda b, i: (b * nt + i, 0)),
                   pl.BlockSpec((1, DH), lambda b, i: (0, 0))],
        out_shape=[_sds((BL, NH, T, DH), F32), _sds((BL * T, D), BF16), _sds((1, DH), F32)],
        compiler_params=_cp(("arbitrary", "arbitrary")),
    )(dog, o, proj, o_g)


def _dn_prep_bwd(dq, dk, dv, dgb, dbb, conv, proj, alog, dtb):
    n = conv.shape[0]
    tt, nt = _conv_tiles()
    w3 = 3 * D

    def body(dq_ref, dk_ref, dv_ref, dgb_ref, dbb_ref, conv_ref, ab_ref, alog_ref, dtb_ref, dconv_ref, dab_ref, dhead_ref):
        cv = conv_ref[...]
        sg = jax.nn.sigmoid(cv)
        dact = sg * (1.0 + cv * (1.0 - sg))
        lane = lax.broadcasted_iota(jnp.int32, (tt, 128), 1)
        cg = jnp.zeros((tt, 128), F32)
        cb = jnp.zeros((tt, 128), F32)
        for h in range(NH):
            cols = slice(h * DH, (h + 1) * DH)
            dconv_ref[:, h * DH:(h + 1) * DH] = dq_ref[0, h] * dact[:, cols]
            dconv_ref[:, D + h * DH:D + (h + 1) * DH] = dk_ref[0, h] * dact[:, D + h * DH:D + (h + 1) * DH]
            dconv_ref[:, 2 * D + h * DH:2 * D + (h + 1) * DH] = dv_ref[0, h] * dact[:, 2 * D + h * DH:2 * D + (h + 1) * DH]
            cg = jnp.where(lane == h, jnp.sum(dgb_ref[0, h], axis=-1, keepdims=True), cg)
            cb = jnp.where(lane == NH + h, jnp.sum(dbb_ref[0, h], axis=-1, keepdims=True), cb)
        _, vjp = jax.vjp(_g_beta, ab_ref[...], alog_ref[...], dtb_ref[...])
        dab, dalog, ddtb = vjp((cg, cb))
        dab_ref[...] = dab.astype(BF16)
        _acc_rows(dhead_ref, jnp.logical_and(pl.program_id(0) == 0, pl.program_id(1) == 0), [dalog, ddtb])

    row = lambda b, i: (b * nt + i, 0)
    head = pl.BlockSpec((1, NH, tt, DH), lambda b, i: (b, 0, i, 0))
    vec = pl.BlockSpec((1, 128), lambda b, i: (0, 0))
    return _pcall(
        body, name="dn_prep_bwd", grid=(BL, nt),
        in_specs=[head] * 5 + [pl.BlockSpec((tt, w3), row),
                               pl.BlockSpec((tt, 128), lambda b, i: (b * nt + i, 4 * D // 128)), vec, vec],
        out_specs=[pl.BlockSpec((tt, w3), row), pl.BlockSpec((tt, 128), row), pl.BlockSpec((2, 128), lambda b, i: (0, 0))],
        out_shape=[_sds((n, w3), F32), _sds((n, 128), BF16), _sds((2, 128), F32)],
        compiler_params=_cp(("arbitrary", "arbitrary")),
    )(dq, dk, dv, dgb, dbb, conv, proj, alog, dtb)


def _dn_sconv_bwd(dconv, proj, w_sc):
    n = dconv.shape[0]
    tt, nt = _conv_tiles()
    w3 = 3 * D

    def body(dc_ref, nxt_ref, qkv_ref, halo_ref, w_ref, dpre_ref, dw_ref, dwin, pwin, rotd, rotp, dsc, accw):
        b, i = pl.program_id(0), pl.program_id(1)
        dwin[0:tt, :] = dc_ref[...]
        dwin[tt:tt + SHALO, :] = jnp.where(i == nt - 1, 0.0, nxt_ref[...])
        pwin[0:SHALO, :] = jnp.where(i == 0, 0.0, halo_ref[...])
        pwin[SHALO:SHALO + tt, :] = qkv_ref[...]
        for k in range(SCW - 1):
            rotd[k] = dwin[pl.ds(k + 1, tt), :]
            rotp[k] = pwin[pl.ds(SHALO - (SCW - 1) + k, tt), :]
        accw[...] = jnp.zeros_like(accw)

        def chunk(c, carry):
            r0 = pl.multiple_of(c * SROWS, SROWS)
            rows = pl.ds(r0, SROWS)
            dc = dc_ref[rows, :]
            dpre = w_ref[SCW - 1:SCW, :] * dc
            accw[SCW - 1] += dc * pwin[pl.ds(pl.multiple_of(r0 + SHALO, SROWS), SROWS), :]
            for k in range(SCW - 1):
                dpre += w_ref[k:k + 1, :] * rotd[SCW - 2 - k, rows, :]
                accw[k] += dc * rotp[k, rows, :]
            dsc[rows, :] = dpre
            return carry

        lax.fori_loop(0, tt // SROWS, chunk, 0)
        dpre_ref[...] = dsc[...].astype(BF16)
        dws = [jnp.sum(accw[k], axis=0, keepdims=True) for k in range(SCW)]
        dws += [jnp.zeros((1, w3), F32)] * (SHALO - SCW)
        _acc_rows(dw_ref, jnp.logical_and(b == 0, i == 0), dws)

    row = lambda b, i: (b * nt + i, 0)
    return _pcall(
        body, name="dn_sconv_bwd", grid=(BL, nt),
        in_specs=[pl.BlockSpec((tt, w3), row), _next_halo_spec(w3, tt, SHALO),
                  pl.BlockSpec((tt, w3), row), _prev_halo_spec(w3, tt, SHALO),
                  pl.BlockSpec((SHALO, w3), lambda b, i: (0, 0))],
        out_specs=[pl.BlockSpec((tt, w3), row), pl.BlockSpec((SHALO, w3), lambda b, i: (0, 0))],
        out_shape=[_sds((n, w3), BF16), _sds((SHALO, w3), F32)],
        scratch_shapes=[pltpu.VMEM((tt + SHALO, w3), F32), pltpu.VMEM((SHALO + tt, w3), F32),
                        pltpu.VMEM((SCW - 1, tt, w3), F32), pltpu.VMEM((SCW - 1, tt, w3), F32),
                        pltpu.VMEM((tt, w3), F32), pltpu.VMEM((SCW, SROWS, w3), F32)],
        compiler_params=_cp(("arbitrary", "arbitrary")),
    )(dconv, dconv, proj, proj, w_sc)


def _ada_fwd(c_all, w_ada, b_cols):
    nl, _, m = w_ada.shape
    nb = c_all.shape[0]

    def body(c_ref, w_ref, b_ref, o_ref):
        cv = c_ref[...]
        cs = (cv * jax.nn.sigmoid(cv)).astype(BF16)
        o_ref[...] = _dot(cs, w_ref[...].astype(BF16)) + b_ref[...]

    return _pcall(
        body, name="ada_fwd", grid=(nl,),
        in_specs=[pl.BlockSpec((nb, D), lambda l: (0, 0)), pl.BlockSpec((None, D, m), lambda l: (l, 0, 0)),
                  pl.BlockSpec((None, 1, m), lambda l: (l, 0, 0))],
        out_specs=pl.BlockSpec((None, nb, m), lambda l: (l, 0, 0)),
        out_shape=_sds((nl, nb, m), F32),
        compiler_params=_cp(("arbitrary",)),
    )(c_all, w_ada, b_cols)


def _ada_bwd(c_all, dmod_cols):
    nl, nb, m = dmod_cols.shape

    def body(c_ref, d_ref, o_ref):
        cv = c_ref[...]
        cs = (cv * jax.nn.sigmoid(cv)).astype(BF16)
        o_ref[0] = _dot_tn(cs, d_ref[...].astype(BF16))

    return _pcall(
        body, name="ada_bwd", grid=(nl,),
        in_specs=[pl.BlockSpec((nb, D), lambda l: (0, 0)), pl.BlockSpec((None, nb, m), lambda l: (l, 0, 0))],
        out_specs=pl.BlockSpec((1, D, m), lambda l: (0, l, 0)),
        out_shape=_sds((1, nl * D, m), F32),
        compiler_params=_cp(("arbitrary",)),
    )(c_all, dmod_cols)


def _loss_head(x, tgt, fg):
    n = x.shape[0]
    tm = min(512, T)

    def f(xv, g, t):
        r = lax.rsqrt(jnp.mean(xv * xv, axis=-1, keepdims=True) + EPS)
        e = xv * r * g - t
        return 0.5 * jnp.sum(e * e, axis=0, keepdims=True) * (1.0 / D)

    def body(x_ref, t_ref, g_ref, dx_ref, st_ref):
        t = t_ref[...]
        lrow, vjp = jax.vjp(lambda xv, g: f(xv, g, t), x_ref[...], g_ref[...])
        dx, dg = vjp(jnp.ones_like(lrow))
        dx_ref[...] = dx
        _acc_rows(st_ref, pl.program_id(0) == 0, [dg, lrow])

    return _pcall(
        body, name="loss_head", grid=(n // tm,),
        in_specs=[pl.BlockSpec((tm, D), lambda i: (i, 0)), pl.BlockSpec((tm, D), lambda i: (i, 0)),
                  pl.BlockSpec((1, D), lambda i: (0, 0))],
        out_specs=[pl.BlockSpec((tm, D), lambda i: (i, 0)), pl.BlockSpec((2, D), lambda i: (0, 0))],
        out_shape=[_sds((n, D), F32), _sds((2, D), F32)],
        compiler_params=_cp(("arbitrary",)),
    )(x, tgt, fg)


def _adamw(parts, w, m, v):
    p, r, c = parts.shape
    tr = r
    for cand in (256, 128, 64, 32, 16, 8):
        if r % cand == 0:
            tr = cand
            break
    k1 = 1.0 - B1 ** STEP
    k2 = 1.0 - B2 ** STEP

    def body(p_ref, w_ref, m_ref, v_ref, g_ref, d_ref, nm_ref, nv_ref):
        g = p_ref[0].astype(F32)
        for q in range(1, p):
            g += p_ref[q].astype(F32)
        mn = B1 * m_ref[...] + (1.0 - B1) * g
        vn = B2 * v_ref[...] + (1.0 - B2) * (g * g)
        g_ref[...] = g
        nm_ref[...] = mn
        nv_ref[...] = vn
        d_ref[...] = -LR * ((mn / k1) / (jnp.sqrt(vn / k2) + AEPS) + WD * w_ref[...])

    blk = pl.BlockSpec((tr, c), lambda i: (i, 0))
    return _pcall(
        body, name="adamw", grid=(r // tr,),
        in_specs=[pl.BlockSpec((p, tr, c), lambda i: (0, i, 0)), blk, blk, blk],
        out_specs=[blk] * 4, out_shape=[_sds((r, c), F32)] * 4,
        compiler_params=_cp(("arbitrary",)),
    )(parts, w, m, v)


def _sum_parts(parts):
    p, r, c = parts.shape

    def body(p_ref, o_ref):
        acc = p_ref[0]
        for q in range(1, p):
            acc += p_ref[q]
        o_ref[...] = acc

    return _pcall(body, name="sum_parts", out_shape=_sds((r, c), F32))(parts)


def _adamw_slot(parts, w, m, v, outs, row0, col):
    p, r, c = parts.shape
    tr = r
    for cand in (256, 128, 64, 32, 16, 8):
        if r % cand == 0:
            tr = cand
            break
    if r % 352 == 0:
        tr = 352
    nt = r // tr
    k1 = 1.0 - B1 ** STEP
    k2 = 1.0 - B2 ** STEP

    def body(p_ref, w_ref, m_ref, v_ref, g0, d0, m0, v0, g_ref, d_ref, nm_ref, nv_ref):
        g = p_ref[0].astype(F32)
        for q in range(1, p):
            g += p_ref[q].astype(F32)
        mn = B1 * m_ref[...] + (1.0 - B1) * g
        vn = B2 * v_ref[...] + (1.0 - B2) * (g * g)
        g_ref[...] = g
        nm_ref[...] = mn
        nv_ref[...] = vn
        d_ref[...] = -LR * ((mn / k1) / (jnp.sqrt(vn / k2) + AEPS) + WD * w_ref[...])

    blk = pl.BlockSpec((tr, c), lambda i: (row0 * nt + i, col))
    anyspec = pl.BlockSpec(memory_space=pl.ANY)
    return _pcall(
        body, name="adamw_slot", grid=(nt,),
        in_specs=[pl.BlockSpec((p, tr, c), lambda i: (0, i, 0)), blk, blk, blk] + [anyspec] * 4,
        out_specs=[blk] * 4, out_shape=[_sds(w.shape, F32)] * 4,
        input_output_aliases={4: 0, 5: 1, 6: 2, 7: 3},
        compiler_params=_cp(("arbitrary",)),
    )(parts, w, m, v, *outs)


def _pack(arrs):
    flat = jnp.concatenate([a.reshape(-1) for a in arrs])
    pad = (-flat.shape[0]) % 1024
    return jnp.pad(flat, (0, pad)).reshape(-1, 128)


def _unpack(buf, shapes):
    flat = buf.reshape(-1)
    out, off = [], 0
    for s in shapes:
        size = 1
        for d in s:
            size *= d
        out.append(flat[off:off + size].reshape(s))
        off += size
    return out


def kernel(x, c, norm_g, w_ada, b_ada, w_ffn_in, w_ffn_out, cm_w_glu, cm_b_glu, cm_w_dw, cm_b_dw, cm_ln_g, cm_ln_b, cm_w_pw, cm_b_pw, dn_w_in, dn_w_sconv, dn_a_log, dn_dt_bias, dn_o_g, dn_w_out, final_g, loss_target, m_norm_g, m_w_ada, m_b_ada, m_w_ffn_in, m_w_ffn_out, m_cm_w_glu, m_cm_b_glu, m_cm_w_dw, m_cm_b_dw, m_cm_ln_g, m_cm_ln_b, m_cm_w_pw, m_cm_b_pw, m_dn_w_in, m_dn_w_sconv, m_dn_a_log, m_dn_dt_bias, m_dn_o_g, m_dn_w_out, m_final_g, v_norm_g, v_w_ada, v_b_ada, v_w_ffn_in, v_w_ffn_out, v_cm_w_glu, v_cm_b_glu, v_cm_w_dw, v_cm_b_dw, v_cm_ln_g, v_cm_ln_b, v_cm_w_pw, v_cm_b_pw, v_dn_w_in, v_dn_w_sconv, v_dn_a_log, v_dn_dt_bias, v_dn_o_g, v_dn_w_out, v_final_g):
    me = 4 * lax.axis_index("x") + 2 * lax.axis_index("y") + lax.axis_index("c")
    n = BL * T
    nf = 4
    tf = FF // nf
    na, nb = cm_w_glu.shape[0], dn_w_in.shape[0]
    mcols = w_ada.shape[2]
    dsh = D // NDEV

    tr_ffn = lambda a: jnp.swapaxes(a, 2, 3)
    tr_dn = lambda a: jnp.transpose(a, (2, 0, 1))
    wt_ffn_in, wt_dn_in = tr_ffn(w_ffn_in), tr_dn(dn_w_in)

    def unit_weights(l, part):
        if part == 0:
            ws = (wt_ffn_in[l, 0], w_ffn_out[l, 0])
        else:
            mix = (cm_w_glu[l // 2], cm_w_pw[l // 2]) if l % 2 == 0 else (wt_dn_in[:, l // 2], dn_w_out[l // 2])
            ws = (wt_ffn_in[l, 1], w_ffn_out[l, 1], *mix)
        return [w.astype(BF16) for w in ws]

    gathers, all_started = {}, jnp.zeros((8, 128), F32)
    for l in range(DEPTH):
        for part in range(2):
            gathers[l, part], tok = _exchange_start(unit_weights(l, part), True, f"gather_start_{l}_{part}")
            all_started = all_started + tok

    c_g, ng_g, dw_g, sc_g = _all_gather([c, norm_g, cm_w_dw, dn_w_sconv], "gather_small")
    whole = lambda g: jnp.moveaxis(g, 0, -2).reshape(*g.shape[1:-1], -1)
    c_all = c_g.reshape(NDEV * BL, D)
    norm_g_f, w_dw_f, w_sc_f = whole(ng_g), whole(dw_g), whole(sc_g)

    b_cols = lax.dynamic_slice_in_dim(b_ada, me * mcols, mcols, axis=1)[:, None, :]
    mod_cols = _ada_fwd(c_all, w_ada, b_cols)
    mod_g, = _all_gather([mod_cols], "gather_mod")
    mod_all = jnp.transpose(mod_g, (1, 2, 0, 3)).reshape(DEPTH, NDEV * BL, 9 * D)
    mod = lax.dynamic_slice_in_dim(mod_all, me * BL, BL, axis=1).reshape(DEPTH, BL, 3, 3, D)

    gathered = [None] * DEPTH

    def ffn_weights(l, s):
        return gathered[l][s].reshape(2, nf, tf, D), gathered[l][2 + s].reshape(nf, tf, D)

    xs = x.reshape(n, D)
    saved = []
    for l in range(DEPTH):
        rec = {}
        ga = _exchange_wait(gathers[l, 0], all_started if l == 0 else xs, f"gather_wait_{l}_0")
        gathered[l] = [ga[0], None, ga[1], None, None, None]
        for s, j in ((0, 0), (1, 2)):
            if j == 2:
                gb = _exchange_wait(gathers[l, 1], xs, f"gather_wait_{l}_1")
                gathered[l] = [ga[0], gb[0], ga[1], gb[1], gb[2], gb[3]]
            w_in, w_out = ffn_weights(l, s)
            ssg, ng = mod[l, :, j], norm_g_f[l, j][None]
            if j == 2:
                ssg1, ng1 = mod[l, :, 1], norm_g_f[l, 1][None]
                if l % 2 == 0:
                    a = l // 2
                    w_glu = gathered[l][4]
                    w_pw = gathered[l][5].reshape(D, D)
                    w_dw = jnp.pad(w_dw_f[a], ((0, HALO - CW), (0, 0)))
                    h1, ab = _premod_matmul(xs, ssg1, ng1, w_glu, cm_b_glu[a][None], w_glu.shape[2])
                    u1, u2 = _cm_mid_fwd(ab, w_dw, cm_b_dw[a][None], cm_ln_g[a][None], cm_ln_b[a][None])
                    xn, ymix = _matmul_res(xs, u2, ssg1, w_pw, cm_b_pw[a][None])
                    rec["mix"] = dict(x=xs, h=h1, ab=ab, u1=u1, u2=u2, y=ymix, w_glu=w_glu, w_pw=w_pw, w_dw=w_dw)
                else:
                    mi = l // 2
                    w_proj = jnp.pad(gathered[l][4].reshape(4 * D + 2 * NH, D), ((0, 128 - 2 * NH), (0, 0)))
                    w_o = gathered[l][5].reshape(D, D)
                    w_sc = jnp.pad(w_sc_f[mi], ((0, SHALO - SCW), (0, 0)))
                    alog = jnp.pad(dn_a_log[mi], (0, 128 - NH))[None]
                    dtb = jnp.pad(dn_dt_bias[mi], (0, 128 - NH))[None]
                    h1, proj = _premod_matmul(xs, ssg1, ng1, w_proj, jnp.zeros((1, w_proj.shape[0]), F32),
                                              (4 * D + 128) // 3 if (4 * D + 128) % 384 == 0 else 128)
                    conv, q, k, v, gb, bb = _dn_sconv_fwd(proj, w_sc, alog, dtb)
                    u, w, qg, kd, intra, egl, tinv = _dn_pre_fwd(q, k, v, gb, bb)
                    o, vn, s0 = _dn_scan_fwd(u, w, qg, kd, intra, egl)
                    og = _dn_gnorm_fwd(o, proj, dn_o_g[mi][None])
                    xn, ymix = _matmul_res(xs, og, ssg1, w_o, jnp.zeros((1, D), F32))
                    rec["mix"] = dict(x=xs, h=h1, proj=proj, conv=conv, q=q, k=k, v=v, gb=gb, bb=bb, w=w, qg=qg, kd=kd,
                                      intra=intra, egl=egl, tinv=tinv, o=o, vn=vn, s0=s0, og=og, y=ymix, w_proj=w_proj, w_o=w_o,
                                      w_sc=w_sc, alog=alog, dtb=dtb)
                xs = xn
            xn, gu, hid, y = _ffn_fwd(xs, ssg, ng, w_in, w_out)
            rec[s] = dict(x=xs, gu=gu, hid=hid, y=y)
            xs = xn
        saved.append(rec)

    dx, stats = _loss_head(xs, loss_target.reshape(n, D), final_g[None])
    loss = lax.psum(jnp.sum(stats[1]), AXES)
    d_final_g = stats[0]

    d_mod = [[None] * 3 for _ in range(DEPTH)]
    d_norm = [[None] * 3 for _ in range(DEPTH)]
    dw_ffn_in = [[None] * 2 for _ in range(DEPTH)]
    dw_ffn_out = [[None] * 2 for _ in range(DEPTH)]
    dcm = [dict() for _ in range(na)]
    ddn = [dict() for _ in range(nb)]
    exchanges = {}

    def gather_small_grads():
        dmod_loc = jnp.stack([jnp.stack(d_mod[l], axis=1) for l in range(DEPTH)]).reshape(DEPTH, BL, 9 * D)
        small = [jnp.sum(dmod_loc, axis=1), jnp.stack([jnp.stack(d_norm[l]) for l in range(DEPTH)]),
                 jnp.stack([d["b_glu"] for d in dcm]), jnp.stack([d["w_dw"] for d in dcm]), jnp.stack([d["b_dw"] for d in dcm]),
                 jnp.stack([d["ln_g"] for d in dcm]), jnp.stack([d["ln_b"] for d in dcm]), jnp.stack([d["b_pw"] for d in dcm]),
                 jnp.stack([d["w_sconv"] for d in ddn]), jnp.stack([d["a_log"] for d in ddn]),
                 jnp.stack([d["dt_bias"] for d in ddn]), jnp.stack([d["o_g"] for d in ddn]), d_final_g]
        dmod_g, small_parts = _all_gather([dmod_loc, _pack(small)], "gather_small_grads")
        return dmod_g, small_parts, [a.shape for a in small]

    token = jnp.zeros((), F32)
    for l in reversed(range(DEPTH)):
        rec = saved[l]
        for s, j in ((1, 2), (0, 0)):
            w_in, w_out = ffn_weights(l, s)
            ssg, ng = mod[l, :, j] + token, norm_g_f[l, j][None]
            r = rec[s]
            dx, dgu, hb, dout, dssg, dng = _ffn_bwd_a(r["x"], dx, ssg, ng, r["y"], r["gu"], w_in, w_out)
            dw_ffn_in[l][s], dw_ffn_out[l][s] = _ffn_bwd_w(hb, dgu, r["hid"], dout)
            d_mod[l][j], d_norm[l][j] = dssg, dng[0]
            if j == 2:
                ssg1, ng1 = mod[l, :, 1], norm_g_f[l, 1][None]
                r = rec["mix"]
                if l % 2 == 0:
                    a = l // 2
                    du2, dy, dgate, db_pw = _matmul_res_bwd(dx, r["y"], ssg1, r["w_pw"])
                    du1, dln = _cm_mid_bwd_a(du2, r["u1"], cm_ln_g[a][None], cm_ln_b[a][None])
                    dab, dw_dw, db_dw, db_glu = _cm_mid_bwd_b(du1, r["ab"], r["w_dw"])
                    dx, dssg, dng = _premod_matmul_bwd(r["x"], dx, ssg1, ng1, [dab], r["w_glu"])
                    dcm[a] = dict(w_glu=_wgrad_shards(r["h"], dab, NDEV), w_pw=_wgrad(r["u2"], dy).reshape(NDEV, dsh, D),
                                  b_glu=db_glu[0], w_dw=dw_dw[:CW], b_dw=db_dw[0], ln_g=dln[0], ln_b=dln[1], b_pw=db_pw[0])
                else:
                    mi = l // 2
                    dog, dy, dgate, _ = _matmul_res_bwd(dx, r["y"], ssg1, r["w_o"])
                    do, dz, d_og = _dn_gnorm_bwd(dog, r["o"], r["proj"], dn_o_g[mi][None])
                    du, dw, dqg, dkd, dintra, degl = _dn_scan_bwd(do, r["w"], r["qg"], r["kd"], r["intra"], r["egl"],
                                                                   r["vn"], r["s0"])
                    dq, dk, dv, dgb, dbb = _dn_pre_bwd(r["q"], r["k"], r["v"], r["gb"], r["bb"], r["tinv"],
                                                       du, dw, dqg, dkd, dintra, degl)
                    dconv, dab16, dhead = _dn_prep_bwd(dq, dk, dv, dgb, dbb, r["conv"], r["proj"], r["alog"], r["dtb"])
                    dpre, dw_sc = _dn_sconv_bwd(dconv, r["proj"], r["w_sc"])
                    dx, dssg, dng = _premod_matmul_bwd(r["x"], dx, ssg1, ng1, [dpre, dz, dab16], r["w_proj"])
                    dw_in = jnp.concatenate([_wgrad(dpre, r["h"]), _wgrad(dz, r["h"]),
                                             _wgrad(dab16, r["h"])[:2 * NH]], axis=0)
                    ddn[mi] = dict(w_in=dw_in.reshape(NDEV, -1, D), w_out=_wgrad(r["og"], dy).reshape(NDEV, dsh, D),
                                   w_sconv=dw_sc[:SCW], a_log=dhead[0, :NH], dt_bias=dhead[1, :NH], o_g=d_og[0])
                d_mod[l][1] = dssg.at[:, 2].set(dgate[:, 0])
                d_norm[l][1] = dng[0]
            unit = [dw_ffn_in[l][s].reshape(NDEV, tf, D), dw_ffn_out[l][s].reshape(NDEV, FF // NDEV, D)]
            if j == 2:
                g = dcm[l // 2] if l % 2 == 0 else ddn[l // 2]
                unit += [g["w_glu"], g["w_pw"]] if l % 2 == 0 else [g["w_in"], g["w_out"]]
            if l == 0 and s == 0:
                dmod_g, small_parts, full_shapes = gather_small_grads()
                unit[0], dmod_g, small_parts = lax.optimization_barrier((unit[0], dmod_g, small_parts))
                small_gathered = (dmod_g, small_parts, full_shapes)
            exchanges[l, s], token = _exchange_start(unit, False, f"grads_start_{l}_{s}")
    grad_x = dx.reshape(BL, T, D)

    dmod_g, small_parts, full_shapes = small_gathered
    dmod_all = jnp.transpose(dmod_g, (1, 0, 2, 3)).reshape(DEPTH, NDEV * BL, 9 * D)
    g_w_ada = _ada_bwd(c_all, lax.dynamic_slice_in_dim(dmod_all, me * mcols, mcols, axis=2))

    got = []
    for l in range(DEPTH):
        ea = _exchange_wait(exchanges[l, 0], dx, f"grads_wait_{l}_0") if l > 0 else [None, None]
        eb = _exchange_wait(exchanges[l, 1], dx, f"grads_wait_{l}_1")
        got.append([ea[0], eb[0], ea[1], eb[1], eb[2], eb[3]])

    names = ["b_ada", "norm_g", "cm_b_glu", "cm_w_dw", "cm_b_dw", "cm_ln_g", "cm_ln_b", "cm_b_pw",
             "dn_w_sconv", "dn_a_log", "dn_dt_bias", "dn_o_g", "final_g"]
    cols = lambda a, width: lax.dynamic_slice_in_dim(a, me * width, width, axis=a.ndim - 1)
    local = {"norm_g": lambda a: cols(a, dsh), "cm_w_dw": lambda a: cols(a, dsh), "dn_w_sconv": lambda a: cols(a, 3 * dsh)}
    summed = _unpack(_sum_parts(small_parts), full_shapes)
    mine = [local.get(nm, lambda a: a)(p) for nm, p in zip(names, summed)]
    small_w = dict(b_ada=(b_ada, m_b_ada, v_b_ada), norm_g=(norm_g, m_norm_g, v_norm_g),
                   cm_b_glu=(cm_b_glu, m_cm_b_glu, v_cm_b_glu), cm_w_dw=(cm_w_dw, m_cm_w_dw, v_cm_w_dw),
                   cm_b_dw=(cm_b_dw, m_cm_b_dw, v_cm_b_dw), cm_ln_g=(cm_ln_g, m_cm_ln_g, v_cm_ln_g),
                   cm_ln_b=(cm_ln_b, m_cm_ln_b, v_cm_ln_b), cm_b_pw=(cm_b_pw, m_cm_b_pw, v_cm_b_pw),
                   dn_w_sconv=(dn_w_sconv, m_dn_w_sconv, v_dn_w_sconv), dn_a_log=(dn_a_log, m_dn_a_log, v_dn_a_log),
                   dn_dt_bias=(dn_dt_bias, m_dn_dt_bias, v_dn_dt_bias), dn_o_g=(dn_o_g, m_dn_o_g, v_dn_o_g),
                   final_g=(final_g, m_final_g, v_final_g))
    loc_shapes = [small_w[nm][0].shape for nm in names]
    sres_raw = _adamw(_pack(mine)[None], *[_pack([small_w[nm][q] for nm in names]) for q in range(3)])
    sres = [dict(zip(names, _unpack(r, loc_shapes))) for r in sres_raw]

    res = {}

    def update(slots, wmv, view, back=None, outs=None):
        w2, m2, v2 = [view(a) for a in wmv]
        outs = [lax.empty(w2.shape, F32) for _ in range(4)] if outs is None else outs
        for p, row0, col in slots:
            outs = _adamw_slot(p, w2, m2, v2, outs, row0, col)
        return outs if back is None else [back(o) for o in outs]

    ffn_slots = [(l, s) for l in reversed(range(DEPTH)) for s in (1, 0)][:-1]
    wmv_in, view_in = (w_ffn_in, m_w_ffn_in, v_w_ffn_in), lambda a: tr_ffn(a).reshape(-1, D)
    wmv_out, view_out = (w_ffn_out, m_w_ffn_out, v_w_ffn_out), lambda a: a.reshape(-1, D)
    part_in = update([(got[l][s], 2 * l + s, 0) for l, s in ffn_slots], wmv_in, view_in)
    part_out = update([(got[l][2 + s], 2 * l + s, 0) for l, s in ffn_slots], wmv_out, view_out)
    cgl = cm_w_glu.shape[2]
    res["cm_w_glu"] = update([(got[2 * a][4], a, 0) for a in range(na)], (cm_w_glu, m_cm_w_glu, v_cm_w_glu),
                             lambda a: a.reshape(-1, cgl), lambda o: o.reshape(cm_w_glu.shape))
    res["cm_w_pw"] = update([(got[2 * a][5], a, 0) for a in range(na)], (cm_w_pw, m_cm_w_pw, v_cm_w_pw),
                            lambda a: a.reshape(-1, D), lambda o: o.reshape(cm_w_pw.shape))
    cdn = dn_w_in.shape[2]
    res["dn_w_in"] = update([(got[2 * i + 1][4], 0, i) for i in range(nb)], (dn_w_in, m_dn_w_in, v_dn_w_in),
                            lambda a: tr_dn(a).reshape(cdn, nb * D),
                            lambda o: jnp.transpose(o.reshape(cdn, nb, D), (1, 2, 0)))
    res["dn_w_out"] = update([(got[2 * i + 1][5], i, 0) for i in range(nb)], (dn_w_out, m_dn_w_out, v_dn_w_out),
                             lambda a: a.reshape(-1, D), lambda o: o.reshape(dn_w_out.shape))
    res["w_ada"] = [o.reshape(w_ada.shape) for o in
                    _adamw(g_w_ada, *[a.reshape(-1, mcols) for a in (w_ada, m_w_ada, v_w_ada)])]
    done = [part_in[0], part_out[0], sres_raw[0]] + [res[nm][0] for nm in ("cm_w_glu", "cm_w_pw", "dn_w_in", "dn_w_out", "w_ada")]
    last = _exchange_wait(exchanges[0, 0], done, "grads_wait_0_0")
    res["w_ffn_in"] = update([(last[0], 0, 0)], wmv_in, view_in,
                             lambda o: jnp.swapaxes(o.reshape(DEPTH, 2, tf, D), 2, 3), part_in)
    res["w_ffn_out"] = update([(last[1], 0, 0)], wmv_out, view_out, lambda o: o.reshape(w_ffn_out.shape), part_out)
    for nm in names:
        res[nm] = [sres[q][nm] for q in range(4)]

    order = ["norm_g", "w_ada", "b_ada", "w_ffn_in", "w_ffn_out", "cm_w_glu", "cm_b_glu", "cm_w_dw", "cm_b_dw", "cm_ln_g",
             "cm_ln_b", "cm_w_pw", "cm_b_pw", "dn_w_in", "dn_w_sconv", "dn_a_log", "dn_dt_bias", "dn_o_g", "dn_w_out", "final_g"]
    return (loss, grad_x, *[res[nm][0] for nm in order], *[res[nm][1] for nm in order],
            *[res[nm][2] for nm in order], *[res[nm][3] for nm in order])
```

```python
import functools

import jax
import jax.numpy as jnp
from jax import lax
from jax.experimental import pallas as pl
from jax.experimental.pallas import tpu as pltpu

F32 = jnp.float32
BF16 = jnp.bfloat16
HI = lax.Precision.HIGHEST
INV_PREC = None
MESH = pl.DeviceIdType.MESH
AXES = ("x", "y", "c")

NDEV = 8
D = 1024
T = 2048
BL = 2
FF = 2816
NH = 8
DH = 128
CW = 31
SCW = 4
CHUNK = 64
DEPTH = 4
EPS = 1e-6
LR, B1, B2, AEPS, WD, STEP = 0.001, 0.9, 0.999, 1e-08, 0.01, 10

VMEM_LIMIT_BYTES = 56 * 1024 * 1024
RELAY_LAYERS = 2
HALO = 32
SHALO = 8


def _pcall(body, **kw):
    return pl.pallas_call(body, **kw)


def _cp(sem=None):
    return pltpu.CompilerParams(dimension_semantics=sem, vmem_limit_bytes=VMEM_LIMIT_BYTES)


def _sds(shape, dtype):
    return jax.ShapeDtypeStruct(tuple(shape), dtype)


def _dot(a, b):
    return jnp.dot(a, b, preferred_element_type=F32)


def _dot_nt(a, b):
    return lax.dot_general(a, b, (((1,), (1,)), ((), ())), preferred_element_type=F32)


def _dot_tn(a, b):
    return lax.dot_general(a, b, (((0,), (0,)), ((), ())), preferred_element_type=F32)


def _modulate(x, ng, scale, shift):
    r = lax.rsqrt(jnp.mean(x * x, axis=-1, keepdims=True) + EPS)
    return (x * r * ng) * (1.0 + scale) + shift


def _modulate_bwd(x, ng, scale, shift, dh):
    r = lax.rsqrt(jnp.mean(x * x, axis=-1, keepdims=True) + EPS)
    xh = x * r
    xg = xh * ng
    h = xg * (1.0 + scale) + shift
    a = dh * (1.0 + scale)
    dxh = a * ng
    dx = r * (dxh - xh * jnp.mean(dxh * xh, axis=-1, keepdims=True))
    return (h, dx, jnp.sum(a * xh, axis=0, keepdims=True), jnp.sum(dh * xg, axis=0, keepdims=True),
            jnp.sum(dh, axis=0, keepdims=True))


def _acc_rows(ref, first, rows):
    @pl.when(first)
    def _():
        ref[...] = jnp.zeros_like(ref)

    for r, val in enumerate(rows):
        ref[r:r + 1, :] += val


def _my_pos():
    return lax.axis_index("x"), lax.axis_index("y"), lax.axis_index("c")


def _all_gather(arrs, name):
    n = len(arrs)

    def body(*refs):
        ins, outs = refs[:n], refs[n:2 * n]
        send, recv, loc = refs[2 * n:]
        x, y, c = _my_pos()
        me, sibling = (x, y, c), (x, y, 1 - c)
        chips = [(1 - x, y), (x, 1 - y), (1 - x, 1 - y)]

        def copy(a, k, block, to, src=None):
            dst = outs[a].at[4 * block[0] + 2 * block[1] + block[2]]
            return pltpu.make_async_remote_copy(
                src_ref=dst if src is None else src, dst_ref=dst,
                send_sem=send.at[7 * a + k], recv_sem=recv.at[7 * a + k],
                device_id=to, device_id_type=MESH)

        mine, first, passed = [], [], []
        for a in range(n):
            m = pltpu.make_async_copy(ins[a], outs[a].at[4 * x + 2 * y + c], loc.at[a])
            m.start()
            mine.append(m)
            f = [copy(a, 0, me, sibling, src=ins[a])]
            f += [copy(a, 1 + j, me, (*chip, c), src=ins[a]) for j, chip in enumerate(chips)]
            for cp in f:
                cp.start()
            first += f
        for a in range(n):
            for j, chip in enumerate(chips):
                copy(a, 1 + j, (*chip, c), me).wait_recv()
                p = copy(a, 4 + j, (*chip, c), sibling)
                p.start()
                passed.append(p)
        for a in range(n):
            copy(a, 0, sibling, me).wait_recv()
            for j, chip in enumerate(chips):
                copy(a, 4 + j, (*chip, 1 - c), me).wait_recv()
        for cp in first + passed:
            cp.wait_send()
        for m in mine:
            m.wait()

    hbm = pl.BlockSpec(memory_space=pl.ANY)
    return _pcall(
        body, name=name,
        out_shape=[_sds((NDEV,) + a.shape, a.dtype) for a in arrs],
        in_specs=[hbm] * n, out_specs=[hbm] * n,
        scratch_shapes=[pltpu.SemaphoreType.DMA((7 * n,)), pltpu.SemaphoreType.DMA((7 * n,)),
                        pltpu.SemaphoreType.DMA((n,))],
    )(*arrs)


def _peer(k):
    x, y, c = _my_pos()
    return (1 - x if k & 4 else x, 1 - y if k & 2 else y, 1 - c if k & 1 else c)


def _dev_index(p):
    return 4 * p[0] + 2 * p[1] + p[2]


_HBM = pl.BlockSpec(memory_space=pltpu.HBM)
_SEM = pl.BlockSpec(memory_space=pltpu.SEMAPHORE)
_EFFECT = pltpu.SideEffectType.DATAFLOW_SIDE_EFFECTING


def _exchange_start(srcs, gather, name):
    n = len(srcs)
    me = _dev_index(_my_pos())
    lands = []
    for s in srcs:
        own = s if gather else lax.dynamic_index_in_dim(s, me, 0, keepdims=False)
        shape = (NDEV,) + s.shape if gather else s.shape
        lands.append(lax.dynamic_update_index_in_dim(lax.empty(shape, s.dtype), own, me, 0))

    def body(*refs):
        src_refs, land_refs = refs[:n], refs[n:2 * n]
        sends, recvs = refs[2 * n:3 * n], refs[3 * n:4 * n]
        token = refs[-1]
        mine = _dev_index(_my_pos())
        for a in range(n):
            for k in range(1, 8):
                p = _peer(k)
                pltpu.make_async_remote_copy(
                    src_ref=src_refs[a] if gather else src_refs[a].at[_dev_index(p)],
                    dst_ref=land_refs[a].at[mine], send_sem=sends[a], recv_sem=recvs[a],
                    device_id=p, device_id_type=MESH).start()
        token[...] = jnp.zeros_like(token)

    out = pl.pallas_call(
        body, name=name,
        out_shape=(*[pltpu.SemaphoreType.DMA(())] * (2 * n),
                   *[pltpu.HBM(a.shape, a.dtype) for a in srcs], *[pltpu.HBM(a.shape, a.dtype) for a in lands],
                   _sds((8, 128), F32)),
        in_specs=[_HBM] * (2 * n),
        out_specs=(*[_SEM] * (2 * n), *[_HBM] * (2 * n), pl.BlockSpec(memory_space=pltpu.VMEM)),
        input_output_aliases={i: 2 * n + i for i in range(2 * n)},
        compiler_params=pltpu.CompilerParams(has_side_effects=_EFFECT),
    )(*[pltpu.with_memory_space_constraint(a, pltpu.HBM) for a in srcs],
      *[pltpu.with_memory_space_constraint(a, pltpu.HBM) for a in lands])
    state = (out[:n], out[n:2 * n], out[2 * n:3 * n], out[3 * n:4 * n])
    return state, out[-1][0, 0]


def _exchange_wait(state, after, name, blocks=NDEV - 1):
    sends, recvs, srcs, lands = state
    n = len(srcs)
    after = list(after) if isinstance(after, (list, tuple)) else [after]

    def body(*refs):
        land_refs = refs[n:2 * n]
        send_refs, recv_refs = refs[2 * n:3 * n], refs[3 * n:4 * n]
        for a in range(n):
            seven = land_refs[a].at[pl.ds(0, blocks)]
            cp = pltpu.make_async_remote_copy(src_ref=seven, dst_ref=seven, send_sem=send_refs[a], recv_sem=recv_refs[a],
                                              device_id=_peer(1), device_id_type=MESH)
            cp.wait_send()
            cp.wait_recv()

    out = pl.pallas_call(
        body, name=name,
        out_shape=(*[pltpu.HBM(a.shape, a.dtype) for a in srcs], *[pltpu.HBM(a.shape, a.dtype) for a in lands]),
        in_specs=(*[_HBM] * (2 * n), *[_SEM] * (2 * n), *[pl.BlockSpec(memory_space=pl.ANY)] * len(after)),
        out_specs=tuple([_HBM] * (2 * n)),
        input_output_aliases={i: i for i in range(2 * n)},
        compiler_params=pltpu.CompilerParams(has_side_effects=_EFFECT),
    )(*srcs, *lands, *sends, *recvs, *after)
    return list(out[n:])


def _other_chips():
    x, y, _ = _my_pos()
    return [(1 - x, y), (x, 1 - y), (1 - x, 1 - y)]


def _relay_gather_start(srcs, name):
    n = len(srcs)
    me = _dev_index(_my_pos())
    lands = [lax.dynamic_update_index_in_dim(lax.empty((NDEV,) + s.shape, s.dtype), s, me, 0) for s in srcs]

    def body(*refs):
        src_refs, land_refs = refs[:n], refs[n:2 * n]
        sa, ra, sb, rb = refs[2 * n:3 * n], refs[3 * n:4 * n], refs[4 * n:5 * n], refs[5 * n:6 * n]
        token = refs[-1]
        x, y, c = _my_pos()
        mine = _dev_index((x, y, c))
        for a in range(n):
            pltpu.make_async_remote_copy(src_ref=src_refs[a], dst_ref=land_refs[a].at[mine], send_sem=sa[a], recv_sem=ra[a],
                                         device_id=(x, y, 1 - c), device_id_type=MESH).start()
            for chip in _other_chips():
                pltpu.make_async_remote_copy(src_ref=src_refs[a], dst_ref=land_refs[a].at[mine], send_sem=sb[a],
                                             recv_sem=rb[a], device_id=(*chip, c), device_id_type=MESH).start()
        token[...] = jnp.zeros_like(token)

    out = pl.pallas_call(
        body, name=name,
        out_shape=(*[pltpu.SemaphoreType.DMA(())] * (4 * n),
                   *[pltpu.HBM(a.shape, a.dtype) for a in srcs], *[pltpu.HBM(a.shape, a.dtype) for a in lands],
                   _sds((8, 128), F32)),
        in_specs=[_HBM] * (2 * n),
        out_specs=(*[_SEM] * (4 * n), *[_HBM] * (2 * n), pl.BlockSpec(memory_space=pltpu.VMEM)),
        input_output_aliases={i: 4 * n + i for i in range(2 * n)},
        compiler_params=pltpu.CompilerParams(has_side_effects=_EFFECT),
    )(*[pltpu.with_memory_space_constraint(a, pltpu.HBM) for a in srcs],
      *[pltpu.with_memory_space_constraint(a, pltpu.HBM) for a in lands])
    sems = [out[q * n:(q + 1) * n] for q in range(4)]
    return (*sems, out[4 * n:5 * n], out[5 * n:6 * n]), out[-1][0, 0]


def _relay_gather_pass(state, after, name):
    sa, ra, sb, rb, srcs, lands = state
    n = len(srcs)
    after = list(after) if isinstance(after, (list, tuple)) else [after]

    def body(*refs):
        land_refs = refs[n:2 * n]
        sa_r, ra_r, sb_r, rb_r = [refs[(2 + q) * n:(3 + q) * n] for q in range(4)]
        outs = refs[6 * n + len(after):]
        sc, rc = outs[2 * n:3 * n], outs[3 * n:4 * n]
        x, y, c = _my_pos()
        for a in range(n):
            one, three = land_refs[a].at[pl.ds(0, 1)], land_refs[a].at[pl.ds(0, 3)]
            for blocks, s_sem, r_sem in ((one, sa_r[a], ra_r[a]), (three, sb_r[a], rb_r[a])):
                cp = pltpu.make_async_remote_copy(src_ref=blocks, dst_ref=blocks, send_sem=s_sem, recv_sem=r_sem,
                                                  device_id=(x, y, 1 - c), device_id_type=MESH)
                cp.wait_send()
                cp.wait_recv()
            for chip in _other_chips():
                blk = land_refs[a].at[_dev_index((*chip, c))]
                pltpu.make_async_remote_copy(src_ref=blk, dst_ref=blk, send_sem=sc[a], recv_sem=rc[a],
                                             device_id=(x, y, 1 - c), device_id_type=MESH).start()

    out = pl.pallas_call(
        body, name=name,
        out_shape=(*[pltpu.HBM(a.shape, a.dtype) for a in srcs], *[pltpu.HBM(a.shape, a.dtype) for a in lands],
                   *[pltpu.SemaphoreType.DMA(())] * (2 * n)),
        in_specs=(*[_HBM] * (2 * n), *[_SEM] * (4 * n), *[pl.BlockSpec(memory_space=pl.ANY)] * len(after)),
        out_specs=(*[_HBM] * (2 * n), *[_SEM] * (2 * n)),
        input_output_aliases={i: i for i in range(2 * n)},
        compiler_params=pltpu.CompilerParams(has_side_effects=_EFFECT),
    )(*srcs, *lands, *sa, *ra, *sb, *rb, *after)
    return (out[2 * n:3 * n], out[3 * n:4 * n], out[:n], out[n:2 * n])


def _ffn_tiles():
    tm = min(512, T)
    return tm, T // tm


def _ffn_fwd(x, ssg, ng, w_in, w_out):
    n = x.shape[0]
    _, nf, tf, _ = w_in.shape
    tm, tpb = _ffn_tiles()

    def body(x_ref, ssg_ref, ng_ref, win_ref, wout_ref, xn_ref, gu_ref, hid_ref, y_ref, h_scr, acc):
        j = pl.program_id(1)

        @pl.when(j == 0)
        def _():
            s = ssg_ref[0]
            h_scr[...] = _modulate(x_ref[...], ng_ref[...], s[1:2], s[0:1]).astype(BF16)
            acc[...] = jnp.zeros_like(acc)

        h = h_scr[...]
        g = _dot_nt(h, win_ref[0])
        u = _dot_nt(h, win_ref[1])
        gu_ref[0] = g.astype(BF16)
        gu_ref[1] = u.astype(BF16)
        hid = (g * jax.nn.sigmoid(g) * u).astype(BF16)
        hid_ref[...] = hid
        acc[...] += _dot(hid, wout_ref[...])

        @pl.when(j == nf - 1)
        def _():
            yv = acc[...]
            y_ref[...] = yv.astype(BF16)
            xn_ref[...] = x_ref[...] + (0.5 * (1.0 + ssg_ref[0][2:3])) * yv

    return _pcall(
        body, name="ffn_fwd", grid=(n // tm, nf),
        in_specs=[pl.BlockSpec((tm, D), lambda i, j: (i, 0)),
                  pl.BlockSpec((1, 3, D), lambda i, j: (i // tpb, 0, 0)),
                  pl.BlockSpec((1, D), lambda i, j: (0, 0)),
                  pl.BlockSpec((2, None, tf, D), lambda i, j: (0, j, 0, 0)),
                  pl.BlockSpec((None, tf, D), lambda i, j: (j, 0, 0))],
        out_specs=[pl.BlockSpec((tm, D), lambda i, j: (i, 0)),
                   pl.BlockSpec((2, None, tm, tf), lambda i, j: (0, j, i, 0)),
                   pl.BlockSpec((None, tm, tf), lambda i, j: (j, i, 0)),
                   pl.BlockSpec((tm, D), lambda i, j: (i, 0))],
        out_shape=[_sds((n, D), F32), _sds((2, nf, n, tf), BF16), _sds((nf, n, tf), BF16), _sds((n, D), BF16)],
        scratch_shapes=[pltpu.VMEM((tm, D), BF16), pltpu.VMEM((tm, D), F32)],
        compiler_params=_cp(("arbitrary", "arbitrary")),
    )(x, ssg, ng, w_in, w_out)


def _ffn_bwd_a(x, dxn, ssg, ng, y, gu, w_in, w_out):
    n = x.shape[0]
    _, nf, tf, _ = w_in.shape
    tm, tpb = _ffn_tiles()

    def body(x_ref, dxn_ref, ssg_ref, ng_ref, y_ref, gu_ref, win_ref, wout_ref,
             dx_ref, dgu_ref, h_ref, dout_ref, dssg_ref, dng_ref, dout_scr, dh_acc):
        i, j = pl.program_id(0), pl.program_id(1)

        @pl.when(j == 0)
        def _():
            db = ((0.5 * (1.0 + ssg_ref[0][2:3])) * dxn_ref[...]).astype(BF16)
            dout_scr[...] = db
            dout_ref[...] = db
            dh_acc[...] = jnp.zeros_like(dh_acc)

        dhid = _dot_nt(dout_scr[...], wout_ref[...]).astype(BF16)
        g = gu_ref[0]
        u = gu_ref[1]
        sig = jax.nn.sigmoid(g)
        dg = dhid * u * (sig * (1.0 + g * (1.0 - sig)))
        du = dhid * (g * sig)
        dgu_ref[0] = dg
        dgu_ref[1] = du
        dh_acc[...] += _dot(dg, win_ref[0])
        dh_acc[...] += _dot(du, win_ref[1])

        @pl.when(j == nf - 1)
        def _():
            s = ssg_ref[0]
            h, dx_, dng_, dsc_, dsh_ = _modulate_bwd(x_ref[...], ng_ref[...], s[1:2], s[0:1], dh_acc[...])
            h_ref[...] = h.astype(BF16)
            dxn = dxn_ref[...]
            dx_ref[...] = dxn + dx_
            dgate = jnp.sum(0.5 * dxn * y_ref[...].astype(F32), axis=0, keepdims=True)
            _acc_rows(dssg_ref.at[0], i % tpb == 0, [dsh_, dsc_, dgate])
            _acc_rows(dng_ref, i == 0, [dng_])

    return _pcall(
        body, name="ffn_bwd_a", grid=(n // tm, nf),
        in_specs=[pl.BlockSpec((tm, D), lambda i, j: (i, 0)),
                  pl.BlockSpec((tm, D), lambda i, j: (i, 0)),
                  pl.BlockSpec((1, 3, D), lambda i, j: (i // tpb, 0, 0)),
                  pl.BlockSpec((1, D), lambda i, j: (0, 0)),
                  pl.BlockSpec((tm, D), lambda i, j: (i, 0)),
                  pl.BlockSpec((2, None, tm, tf), lambda i, j: (0, j, i, 0)),
                  pl.BlockSpec((2, None, tf, D), lambda i, j: (0, j, 0, 0)),
                  pl.BlockSpec((None, tf, D), lambda i, j: (j, 0, 0))],
        out_specs=[pl.BlockSpec((tm, D), lambda i, j: (i, 0)),
                   pl.BlockSpec((2, None, tm, tf), lambda i, j: (0, j, i, 0)),
                   pl.BlockSpec((tm, D), lambda i, j: (i, 0)),
                   pl.BlockSpec((tm, D), lambda i, j: (i, 0)),
                   pl.BlockSpec((1, 3, D), lambda i, j: (i // tpb, 0, 0)),
                   pl.BlockSpec((1, D), lambda i, j: (0, 0))],
        out_shape=[_sds((n, D), F32), _sds((2, nf, n, tf), BF16), _sds((n, D), BF16), _sds((n, D), BF16),
                   _sds((BL, 3, D), F32), _sds((1, D), F32)],
        scratch_shapes=[pltpu.VMEM((tm, D), BF16), pltpu.VMEM((tm, D), F32)],
        compiler_params=_cp(("arbitrary", "arbitrary")),
    )(x, dxn, ssg, ng, y, gu, w_in, w_out)


def _ffn_bwd_w(h, dgu, hid, dout):
    n = h.shape[0]
    _, nf, _, tf = dgu.shape
    tm, _ = _ffn_tiles()
    ni = n // tm

    def body(h_ref, dgu_ref, hid_ref, dout_ref, dwin_ref, dwout_ref, acc_g, acc_u, acc_o):
        i = pl.program_id(1)

        @pl.when(i == 0)
        def _():
            acc_g[...] = jnp.zeros_like(acc_g)
            acc_u[...] = jnp.zeros_like(acc_u)
            acc_o[...] = jnp.zeros_like(acc_o)

        hv = h_ref[...]
        acc_g[...] += _dot_tn(dgu_ref[0], hv)
        acc_u[...] += _dot_tn(dgu_ref[1], hv)
        acc_o[...] += _dot_tn(hid_ref[...], dout_ref[...])

        @pl.when(i == ni - 1)
        def _():
            dwin_ref[0] = acc_g[...].astype(BF16)
            dwin_ref[1] = acc_u[...].astype(BF16)
            dwout_ref[...] = acc_o[...].astype(BF16)

    return _pcall(
        body, name="ffn_bwd_w", grid=(nf, ni),
        in_specs=[pl.BlockSpec((tm, D), lambda j, i: (i, 0)),
                  pl.BlockSpec((2, None, tm, tf), lambda j, i: (0, j, i, 0)),
                  pl.BlockSpec((None, tm, tf), lambda j, i: (j, i, 0)),
                  pl.BlockSpec((tm, D), lambda j, i: (i, 0))],
        out_specs=[pl.BlockSpec((2, None, tf, D), lambda j, i: (0, j, 0, 0)),
                   pl.BlockSpec((None, tf, D), lambda j, i: (j, 0, 0))],
        out_shape=[_sds((2, nf, tf, D), BF16), _sds((nf, tf, D), BF16)],
        scratch_shapes=[pltpu.VMEM((tf, D), F32), pltpu.VMEM((tf, D), F32), pltpu.VMEM((tf, D), F32)],
        compiler_params=_cp(("arbitrary", "arbitrary")),
    )(h, dgu, hid, dout)


def _premod_matmul(x, ssg, ng, w, bias, tn):
    n = x.shape[0]
    shards = w.ndim == 3
    m = w.shape[0] * w.shape[2] if shards else w.shape[0]
    tm = min(512, T)
    tpb = T // tm
    to = m if shards else tn
    w_spec = (pl.BlockSpec(w.shape, lambda i, j: (0, 0, 0)) if shards
              else pl.BlockSpec((tn, D), lambda i, j: (j, 0)))

    def body(x_ref, ssg_ref, ng_ref, w_ref, b_ref, h_ref, o_ref, h_scr):
        @pl.when(pl.program_id(1) == 0)
        def _():
            s = ssg_ref[0]
            hb = _modulate(x_ref[...], ng_ref[...], s[1:2], s[0:1]).astype(BF16)
            h_scr[...] = hb
            h_ref[...] = hb

        hv = h_scr[...]
        if shards:
            for q in range(w.shape[0]):
                cols = slice(q * tn, (q + 1) * tn)
                o_ref[:, cols] = _dot(hv, w_ref[q]) + b_ref[:, cols]
        else:
            o_ref[...] = _dot_nt(hv, w_ref[...]) + b_ref[...]

    return _pcall(
        body, name="premod_matmul", grid=(n // tm, m // to),
        in_specs=[pl.BlockSpec((tm, D), lambda i, j: (i, 0)),
                  pl.BlockSpec((1, 3, D), lambda i, j: (i // tpb, 0, 0)),
                  pl.BlockSpec((1, D), lambda i, j: (0, 0)),
                  w_spec,
                  pl.BlockSpec((1, to), lambda i, j: (0, j))],
        out_specs=[pl.BlockSpec((tm, D), lambda i, j: (i, 0)),
                   pl.BlockSpec((tm, to), lambda i, j: (i, j))],
        out_shape=[_sds((n, D), BF16), _sds((n, m), F32)],
        scratch_shapes=[pltpu.VMEM((tm, D), BF16)],
        compiler_params=_cp(("arbitrary", "arbitrary")),
    )(x, ssg, ng, w, bias)


def _premod_matmul_bwd(x, dxn, ssg, ng, douts, w):
    n = x.shape[0]
    k = len(douts)
    shards = w.ndim == 3
    tm = min(512, T)
    tpb = T // tm

    def body(*refs):
        x_ref, dxn_ref, ssg_ref, ng_ref = refs[:4]
        do_refs, w_ref = refs[4:4 + k], refs[4 + k]
        dx_ref, dssg_ref, dng_ref = refs[5 + k:]
        i = pl.program_id(0)
        dh = jnp.zeros((tm, D), F32)
        if shards:
            cs = w.shape[2]
            dov = do_refs[0][...]
            for j in range(w.shape[0]):
                dh += _dot_nt(dov[:, j * cs:(j + 1) * cs], w_ref[j])
        else:
            off = 0
            for q in range(k):
                mk = douts[q].shape[1]
                dh += _dot(do_refs[q][...], w_ref[off:off + mk, :])
                off += mk
        s = ssg_ref[0]
        _, dx_, dng_, dsc_, dsh_ = _modulate_bwd(x_ref[...], ng_ref[...], s[1:2], s[0:1], dh)
        dx_ref[...] = dxn_ref[...] + dx_
        _acc_rows(dssg_ref.at[0], i % tpb == 0, [dsh_, dsc_, jnp.zeros_like(dsh_)])
        _acc_rows(dng_ref, i == 0, [dng_])

    return _pcall(
        body, name="premod_matmul_bwd", grid=(n // tm,),
        in_specs=[pl.BlockSpec((tm, D), lambda i: (i, 0)),
                  pl.BlockSpec((tm, D), lambda i: (i, 0)),
                  pl.BlockSpec((1, 3, D), lambda i: (i // tpb, 0, 0)),
                  pl.BlockSpec((1, D), lambda i: (0, 0))]
                 + [pl.BlockSpec((tm, a.shape[1]), lambda i: (i, 0)) for a in douts]
                 + [pl.BlockSpec(w.shape, (lambda i: (0, 0, 0)) if shards else (lambda i: (0, 0)))],
        out_specs=[pl.BlockSpec((tm, D), lambda i: (i, 0)),
                   pl.BlockSpec((1, 3, D), lambda i: (i // tpb, 0, 0)),
                   pl.BlockSpec((1, D), lambda i: (0, 0))],
        out_shape=[_sds((n, D), F32), _sds((BL, 3, D), F32), _sds((1, D), F32)],
        compiler_params=_cp(("arbitrary",)),
    )(x, dxn, ssg, ng, *douts, w)


def _matmul_res(x, a, ssg, w, bias):
    n, kd = a.shape
    tm = min(512, T)
    tpb = T // tm

    def body(x_ref, a_ref, ssg_ref, w_ref, b_ref, xn_ref, y_ref):
        yv = _dot(a_ref[...], w_ref[...]) + b_ref[...]
        y_ref[...] = yv.astype(BF16)
        xn_ref[...] = x_ref[...] + (1.0 + ssg_ref[0][2:3]) * yv

    return _pcall(
        body, name="matmul_res", grid=(n // tm,),
        in_specs=[pl.BlockSpec((tm, D), lambda i: (i, 0)),
                  pl.BlockSpec((tm, kd), lambda i: (i, 0)),
                  pl.BlockSpec((1, 3, D), lambda i: (i // tpb, 0, 0)),
                  pl.BlockSpec((kd, D), lambda i: (0, 0)),
                  pl.BlockSpec((1, D), lambda i: (0, 0))],
        out_specs=[pl.BlockSpec((tm, D), lambda i: (i, 0)), pl.BlockSpec((tm, D), lambda i: (i, 0))],
        out_shape=[_sds((n, D), F32), _sds((n, D), BF16)],
        compiler_params=_cp(("arbitrary",)),
    )(x, a, ssg, w, bias)


def _matmul_res_bwd(dxn, y, ssg, w):
    n = dxn.shape[0]
    kd = w.shape[0]
    tm = min(512, T)
    tpb = T // tm

    def body(dxn_ref, y_ref, ssg_ref, w_ref, da_ref, dy_ref, dgate_ref, dbias_ref):
        i = pl.program_id(0)
        dxn = dxn_ref[...]
        dy = (1.0 + ssg_ref[0][2:3]) * dxn
        dyb = dy.astype(BF16)
        dy_ref[...] = dyb
        da_ref[...] = _dot_nt(dyb, w_ref[...])
        _acc_rows(dgate_ref.at[0], i % tpb == 0, [jnp.sum(dxn * y_ref[...].astype(F32), axis=0, keepdims=True)])
        _acc_rows(dbias_ref, i == 0, [jnp.sum(dy, axis=0, keepdims=True)])

    return _pcall(
        body, name="matmul_res_bwd", grid=(n // tm,),
        in_specs=[pl.BlockSpec((tm, D), lambda i: (i, 0)),
                  pl.BlockSpec((tm, D), lambda i: (i, 0)),
                  pl.BlockSpec((1, 3, D), lambda i: (i // tpb, 0, 0)),
                  pl.BlockSpec((kd, D), lambda i: (0, 0))],
        out_specs=[pl.BlockSpec((tm, kd), lambda i: (i, 0)),
                   pl.BlockSpec((tm, D), lambda i: (i, 0)),
                   pl.BlockSpec((1, 1, D), lambda i: (i // tpb, 0, 0)),
                   pl.BlockSpec((1, D), lambda i: (0, 0))],
        out_shape=[_sds((n, kd), F32), _sds((n, D), BF16), _sds((BL, 1, D), F32), _sds((1, D), F32)],
        compiler_params=_cp(("arbitrary",)),
    )(dxn, y, ssg, w)


def _wgrad_shards(a, b, ns):
    n, kd = a.shape
    cs = b.shape[1] // ns
    tm = min(512, T)
    ni = n // tm

    def body(a_ref, b_ref, o_ref, acc):
        i = pl.program_id(0)

        @pl.when(i == 0)
        def _():
            acc[...] = jnp.zeros_like(acc)

        at = a_ref[...].T
        for q in range(ns):
            acc[q] += _dot(at, b_ref[:, q * cs:(q + 1) * cs])

        @pl.when(i == ni - 1)
        def _():
            o_ref[...] = acc[...].astype(BF16)

    return _pcall(
        body, name="wgrad_shards", grid=(ni,),
        in_specs=[pl.BlockSpec((tm, kd), lambda i: (i, 0)), pl.BlockSpec((tm, ns * cs), lambda i: (i, 0))],
        out_specs=pl.BlockSpec((ns, kd, cs), lambda i: (0, 0, 0)),
        out_shape=_sds((ns, kd, cs), BF16),
        scratch_shapes=[pltpu.VMEM((ns, kd, cs), F32)],
        compiler_params=_cp(("arbitrary",)),
    )(a, b)


def _wgrad(a, b):
    n, kd = a.shape
    m = b.shape[1]
    tm = min(512, T)
    tk = min(512, kd)
    ni = n // tm

    def body(a_ref, b_ref, o_ref, acc):
        i = pl.program_id(1)

        @pl.when(i == 0)
        def _():
            acc[...] = jnp.zeros_like(acc)

        acc[...] += _dot_tn(a_ref[...], b_ref[...])

        @pl.when(i == ni - 1)
        def _():
            o_ref[...] = acc[...].astype(BF16)

    return _pcall(
        body, name="wgrad", grid=(kd // tk, ni),
        in_specs=[pl.BlockSpec((tm, tk), lambda q, i: (i, q)), pl.BlockSpec((tm, m), lambda q, i: (i, 0))],
        out_specs=pl.BlockSpec((tk, m), lambda q, i: (q, 0)),
        out_shape=_sds((kd, m), BF16),
        scratch_shapes=[pltpu.VMEM((tk, m), F32)],
        compiler_params=_cp(("arbitrary", "arbitrary")),
    )(a, b)


def _ln_silu(u1, g, b):
    mu = jnp.mean(u1, axis=-1, keepdims=True)
    xc = u1 - mu
    var = jnp.mean(xc * xc, axis=-1, keepdims=True)
    ln = xc * lax.rsqrt(var + EPS) * g + b
    return ln * jax.nn.sigmoid(ln)


def _conv_tiles():
    tt = min(256, T)
    return tt, T // tt


def _prev_halo_spec(cols, tt, halo):
    r = tt // halo
    return pl.BlockSpec((halo, cols), lambda b, i: (jnp.maximum(b * (T // halo) + i * r - 1, 0), 0))


def _next_halo_spec(cols, tt, halo):
    r = tt // halo
    last = BL * T // halo - 1
    return pl.BlockSpec((halo, cols), lambda b, i: (jnp.minimum(b * (T // halo) + (i + 1) * r, last), 0))


ROWS = 32
SROWS = 8


def _fill_rotations(rot, win, rows):
    for r in range(8):
        rot[r, 0:rows, :] = win[pl.ds(r, rows), :]


def _window(rot, off, start, size):
    return rot[off % 8, pl.ds(pl.multiple_of(start + (off // 8) * 8, 8), size), :]


def _cm_mid_fwd(ab, w_dw, b_dw, ln_g, ln_b):
    n = ab.shape[0]
    tt, nt = _conv_tiles()

    def body(ab_ref, halo_ref, w_ref, bdw_ref, g_ref, b_ref, u1_ref, u2_ref, win, rot):
        i = pl.program_id(1)
        hv = halo_ref[...]
        u0h = hv[:, :D] * jax.nn.sigmoid(hv[:, D:])
        win[0:HALO, :] = jnp.where(i == 0, 0.0, u0h)
        cv = ab_ref[...]
        win[HALO:HALO + tt, :] = cv[:, :D] * jax.nn.sigmoid(cv[:, D:])
        win[HALO + tt:, :] = jnp.zeros((8, D), F32)
        _fill_rotations(rot, win, tt + HALO)

        def chunk(c, carry):
            r0 = pl.multiple_of(c * ROWS, ROWS)
            acc = jnp.zeros((ROWS, D), F32) + bdw_ref[...]
            for k in range(CW):
                acc += w_ref[k:k + 1, :] * _window(rot, HALO - (CW - 1) + k, r0, ROWS)
            u1_ref[pl.ds(r0, ROWS), :] = acc
            u2_ref[pl.ds(r0, ROWS), :] = _ln_silu(acc, g_ref[...], b_ref[...]).astype(BF16)
            return carry

        lax.fori_loop(0, tt // ROWS, chunk, 0)

    row = lambda b, i: (b * nt + i, 0)
    vec = pl.BlockSpec((1, D), lambda b, i: (0, 0))
    return _pcall(
        body, name="cm_mid_fwd", grid=(BL, nt),
        in_specs=[pl.BlockSpec((tt, 2 * D), row), _prev_halo_spec(2 * D, tt, HALO),
                  pl.BlockSpec((HALO, D), lambda b, i: (0, 0)), vec, vec, vec],
        out_specs=[pl.BlockSpec((tt, D), row), pl.BlockSpec((tt, D), row)],
        out_shape=[_sds((n, D), F32), _sds((n, D), BF16)],
        scratch_shapes=[pltpu.VMEM((HALO + tt + 8, D), F32), pltpu.VMEM((8, tt + HALO, D), F32)],
        compiler_params=_cp(("arbitrary", "arbitrary")),
    )(ab, ab, w_dw, b_dw, ln_g, ln_b)


def _cm_mid_bwd_a(du2, u1, ln_g, ln_b):
    n = du2.shape[0]
    tm = min(256, T)

    def body(du2_ref, u1_ref, g_ref, b_ref, du1_ref, dln_ref):
        _, vjp = jax.vjp(_ln_silu, u1_ref[...], g_ref[...], b_ref[...])
        du1, dg, db = vjp(du2_ref[...])
        du1_ref[...] = du1
        _acc_rows(dln_ref, pl.program_id(0) == 0, [dg, db])

    vec = pl.BlockSpec((1, D), lambda i: (0, 0))
    return _pcall(
        body, name="cm_mid_bwd_a", grid=(n // tm,),
        in_specs=[pl.BlockSpec((tm, D), lambda i: (i, 0)), pl.BlockSpec((tm, D), lambda i: (i, 0)), vec, vec],
        out_specs=[pl.BlockSpec((tm, D), lambda i: (i, 0)), pl.BlockSpec((2, D), lambda i: (0, 0))],
        out_shape=[_sds((n, D), F32), _sds((2, D), F32)],
        compiler_params=_cp(("arbitrary",)),
    )(du2, u1, ln_g, ln_b)


def _cm_mid_bwd_b(du1, ab, w_dw):
    n = du1.shape[0]
    tt, nt = _conv_tiles()

    def body(du1_ref, nxt_ref, ab_ref, halo_ref, w_ref, dab_ref, dw_ref, dbdw_ref, dbglu_ref,
             dwin, uwin, rotd, rotu, accw, accv):
        b, i = pl.program_id(0), pl.program_id(1)
        first = jnp.logical_and(b == 0, i == 0)
        dwin[0:tt, :] = du1_ref[...]
        dwin[tt:tt + HALO, :] = jnp.where(i == nt - 1, 0.0, nxt_ref[...])
        dwin[tt + HALO:, :] = jnp.zeros((8, D), F32)
        hv = halo_ref[...]
        uwin[0:HALO, :] = jnp.where(i == 0, 0.0, hv[:, :D] * jax.nn.sigmoid(hv[:, D:]))
        cv = ab_ref[...]
        uwin[HALO:HALO + tt, :] = cv[:, :D] * jax.nn.sigmoid(cv[:, D:])
        uwin[HALO + tt:, :] = jnp.zeros((8, D), F32)
        _fill_rotations(rotd, dwin, tt + HALO)
        _fill_rotations(rotu, uwin, tt + HALO)
        accw[...] = jnp.zeros_like(accw)
        accv[...] = jnp.zeros_like(accv)

        def fold(v):
            return jnp.sum(v.reshape(ROWS // 8, 8, D), axis=0)

        def chunk(c, carry):
            r0 = pl.multiple_of(c * ROWS, ROWS)
            d1 = du1_ref[pl.ds(r0, ROWS), :]
            du0 = jnp.zeros((ROWS, D), F32)
            for k in range(CW):
                du0 += w_ref[k:k + 1, :] * _window(rotd, CW - 1 - k, r0, ROWS)
                accw[k] += fold(d1 * _window(rotu, HALO - (CW - 1) + k, r0, ROWS))
            cvc = ab_ref[pl.ds(r0, ROWS), :]
            av, sg = cvc[:, :D], jax.nn.sigmoid(cvc[:, D:])
            da = du0 * sg
            db = du0 * av * sg * (1.0 - sg)
            dab_ref[pl.ds(r0, ROWS), 0:D] = da.astype(BF16)
            dab_ref[pl.ds(r0, ROWS), D:2 * D] = db.astype(BF16)
            accv[0] += fold(d1)
            accv[1] += fold(da)
            accv[2] += fold(db)
            return carry

        lax.fori_loop(0, tt // ROWS, chunk, 0)
        dws = [jnp.sum(accw[k], axis=0, keepdims=True) for k in range(CW)]
        dws += [jnp.zeros((1, D), F32)] * (HALO - CW)
        _acc_rows(dw_ref, first, dws)
        _acc_rows(dbdw_ref, first, [jnp.sum(accv[0], axis=0, keepdims=True)])
        _acc_rows(dbglu_ref.at[:, 0:D], first, [jnp.sum(accv[1], axis=0, keepdims=True)])
        _acc_rows(dbglu_ref.at[:, D:2 * D], first, [jnp.sum(accv[2], axis=0, keepdims=True)])

    row = lambda b, i: (b * nt + i, 0)
    return _pcall(
        body, name="cm_mid_bwd_b", grid=(BL, nt),
        in_specs=[pl.BlockSpec((tt, D), row), _next_halo_spec(D, tt, HALO),
                  pl.BlockSpec((tt, 2 * D), row), _prev_halo_spec(2 * D, tt, HALO),
                  pl.BlockSpec((HALO, D), lambda b, i: (0, 0))],
        out_specs=[pl.BlockSpec((tt, 2 * D), row), pl.BlockSpec((HALO, D), lambda b, i: (0, 0)),
                   pl.BlockSpec((1, D), lambda b, i: (0, 0)), pl.BlockSpec((1, 2 * D), lambda b, i: (0, 0))],
        out_shape=[_sds((n, 2 * D), BF16), _sds((HALO, D), F32), _sds((1, D), F32), _sds((1, 2 * D), F32)],
        scratch_shapes=[pltpu.VMEM((tt + HALO + 8, D), F32), pltpu.VMEM((HALO + tt + 8, D), F32),
                        pltpu.VMEM((8, tt + HALO, D), F32), pltpu.VMEM((8, tt + HALO, D), F32),
                        pltpu.VMEM((HALO, 8, D), F32), pltpu.VMEM((3, 8, D), F32)],
        compiler_params=_cp(("arbitrary", "arbitrary")),
    )(du1, du1, ab, ab, w_dw)


def _softplus(v):
    return jnp.maximum(v, 0.0) + jnp.log(1.0 + jnp.exp(-jnp.abs(v)))


def _g_beta(ab, alog, dtb):
    return -jnp.exp(alog) * _softplus(ab + dtb), jax.nn.sigmoid(ab)


def _dn_sconv_fwd(proj, w_sc, alog, dtb):
    n = proj.shape[0]
    tt, nt = _conv_tiles()
    w3 = 3 * D

    def body(qkv_ref, halo_ref, ab_ref, w_ref, alog_ref, dtb_ref, conv_ref, q_ref, k_ref, v_ref, gb_ref, bb_ref,
             win, rot, gsc, bsc):
        i = pl.program_id(1)
        win[0:SHALO, :] = jnp.where(i == 0, 0.0, halo_ref[...])
        win[SHALO:SHALO + tt, :] = qkv_ref[...]
        for k in range(SCW - 1):
            rot[k] = win[pl.ds(SHALO - (SCW - 1) + k, tt), :]
        gsc[...], bsc[...] = _g_beta(ab_ref[...], alog_ref[...], dtb_ref[...])

        def chunk(c, carry):
            rows = pl.ds(pl.multiple_of(c * SROWS, SROWS), SROWS)
            acc = w_ref[SCW - 1:SCW, :] * win[pl.ds(pl.multiple_of(c * SROWS + SHALO, SROWS), SROWS), :]
            for k in range(SCW - 1):
                acc += w_ref[k:k + 1, :] * rot[k, rows, :]
            conv_ref[rows, :] = acc
            act = acc * jax.nn.sigmoid(acc)
            gfull, bfull = gsc[rows, :], bsc[rows, :]
            for h in range(NH):
                q_ref[0, h, rows, :] = act[:, h * DH:(h + 1) * DH]
                k_ref[0, h, rows, :] = act[:, D + h * DH:D + (h + 1) * DH]
                v_ref[0, h, rows, :] = act[:, 2 * D + h * DH:2 * D + (h + 1) * DH]
                gb_ref[0, h, rows, :] = jnp.broadcast_to(gfull[:, h:h + 1], (SROWS, DH))
                bb_ref[0, h, rows, :] = jnp.broadcast_to(bfull[:, NH + h:NH + h + 1], (SROWS, DH))
            return carry

        lax.fori_loop(0, tt // SROWS, chunk, 0)

    row = lambda b, i: (b * nt + i, 0)
    head = pl.BlockSpec((1, NH, tt, DH), lambda b, i: (b, 0, i, 0))
    vec = pl.BlockSpec((1, 128), lambda b, i: (0, 0))
    hs = _sds((BL, NH, T, DH), F32)
    return _pcall(
        body, name="dn_sconv_fwd", grid=(BL, nt),
        in_specs=[pl.BlockSpec((tt, w3), row), _prev_halo_spec(w3, tt, SHALO),
                  pl.BlockSpec((tt, 128), lambda b, i: (b * nt + i, 4 * D // 128)),
                  pl.BlockSpec((SHALO, w3), lambda b, i: (0, 0)), vec, vec],
        out_specs=[pl.BlockSpec((tt, w3), row), head, head, head, head, head],
        out_shape=[_sds((n, w3), F32), hs, hs, hs, hs, hs],
        scratch_shapes=[pltpu.VMEM((SHALO + tt, w3), F32), pltpu.VMEM((SCW - 1, tt, w3), F32),
                        pltpu.VMEM((tt, 128), F32), pltpu.VMEM((tt, 128), F32)],
        compiler_params=_cp(("arbitrary", "arbitrary")),
    )(proj, proj, proj, w_sc, alog, dtb)


_BMM_SPEC = {"nn": "gij,gjk->gik", "nt": "gid,gjd->gij", "tn": "gcd,gce->gde"}


def _mm(kind, a, b, prec):
    if prec is None:
        return jnp.einsum(_BMM_SPEC[kind], a.astype(BF16), b.astype(BF16), preferred_element_type=F32)
    return jnp.einsum(_BMM_SPEC[kind], a, b, preferred_element_type=F32, precision=prec)


@functools.partial(jax.custom_vjp, nondiff_argnums=(0, 3))
def _bmm_k(kind, a, b, prec):
    return _mm(kind, a, b, prec)


def _bmm_k_fwd(kind, a, b, prec):
    return _mm(kind, a, b, prec), (a, b)


def _bmm_k_bwd(kind, prec, res, dc):
    a, b = res
    if kind == "nn":
        return _bmm_k("nt", dc, b, prec), _bmm_k("tn", a, dc, prec)
    if kind == "nt":
        return _bmm_k("nn", dc, b, prec), _bmm_k("tn", dc, a, prec)
    return _bmm_k("nt", b, dc, prec), _bmm_k("nn", a, dc, prec)


_bmm_k.defvjp(_bmm_k_fwd, _bmm_k_bwd)


def _bmm(a, b, prec=None):
    return _bmm_k("nn", a, b, prec)


def _bmm_nt(a, b, prec=None):
    return _bmm_k("nt", a, b, prec)


def _bmm_tn(a, b, prec=None):
    return _bmm_k("tn", a, b, prec)


def _bmm_raw(a, b):
    return _mm("nn", a, b, None)


def _bmm_nt_raw(a, b):
    return _mm("nt", a, b, None)


def _bmm_tn_raw(a, b):
    return _mm("tn", a, b, None)


@jax.custom_vjp
def _unit_lower_inverse(a):
    eye = (lax.broadcasted_iota(jnp.int32, a.shape, 1) == lax.broadcasted_iota(jnp.int32, a.shape, 2)).astype(F32)
    t = eye - a
    p = a
    for _ in range(CHUNK.bit_length() - 2):
        p = _mm("nn", p, p, INV_PREC)
        t = _mm("nn", t, eye + p, INV_PREC)
    return t


def _uli_fwd(a):
    t = _unit_lower_inverse(a)
    return t, t


def _uli_bwd(t, dt):
    return (-_bmm_nt(_bmm_tn(t, dt, lax.Precision.HIGH), t, lax.Precision.HIGH),)


_unit_lower_inverse.defvjp(_uli_fwd, _uli_bwd)


@jax.custom_vjp
def _known_inverse(a, t):
    return t


_known_inverse.defvjp(lambda a, t: (t, t), lambda t, dt: (_uli_bwd(t, dt)[0], jnp.zeros_like(t)))


def _dn_pre(q, k, v, gb, bb, tm_known=None):
    shape = (q.shape[0], CHUNK, CHUNK)
    ri = lax.broadcasted_iota(jnp.int32, shape, 1)
    ci = lax.broadcasted_iota(jnp.int32, shape, 2)
    causal, strict = ri >= ci, ri > ci
    qn = q * lax.rsqrt(jnp.sum(q * q, axis=-1, keepdims=True) + EPS) * (DH ** -0.5)
    kn = k * lax.rsqrt(jnp.sum(k * k, axis=-1, keepdims=True) + EPS)
    gcs = _bmm(causal.astype(F32), gb, HI)
    gcol = gcs[:, :, :CHUNK]
    decay = jnp.exp(jnp.where(causal, gcol - jnp.swapaxes(gcol, 1, 2), -jnp.inf))
    eg = jnp.exp(gcs)
    kb = kn * bb
    a = jnp.where(strict, _bmm_nt(kb, kn) * decay, 0.0)
    tm = _unit_lower_inverse(a) if tm_known is None else _known_inverse(a, tm_known)
    u = _bmm(tm, v * bb)
    w = _bmm(tm, kb * eg)
    qg = qn * eg
    intra = _bmm_nt(qn, kn) * decay
    glast = gcs[:, CHUNK - 1:CHUNK, :]
    kd = kn * jnp.exp(glast - gcs)
    egl = jnp.broadcast_to(jnp.exp(glast), (q.shape[0], 8, DH))
    return u, w, qg, kd, intra, egl, tm


def _pre_tiles():
    gcn = min(16, T // CHUNK)
    return gcn, T // (CHUNK * gcn)


def _dn_pre_specs():
    gcn, _ = _pre_tiles()
    tok = pl.BlockSpec((None, None, gcn * CHUNK, DH), lambda b, h, i: (b, h, i, 0))
    sq = pl.BlockSpec((None, None, gcn * CHUNK, CHUNK), lambda b, h, i: (b, h, i, 0))
    per = pl.BlockSpec((None, None, gcn * 8, DH), lambda b, h, i: (b, h, i, 0))
    return tok, sq, per


def _dn_pre_fwd(q, k, v, gb, bb):
    gcn, ng = _pre_tiles()
    tok, sq, per = _dn_pre_specs()

    def body(q_ref, k_ref, v_ref, gb_ref, bb_ref, u_ref, w_ref, qg_ref, kd_ref, in_ref, egl_ref, tinv_ref):
        args = [r[...].reshape(gcn, CHUNK, DH) for r in (q_ref, k_ref, v_ref, gb_ref, bb_ref)]
        u, w, qg, kd, intra, egl, tinv = _dn_pre(*args)
        for r, val in ((u_ref, u), (w_ref, w), (qg_ref, qg), (kd_ref, kd)):
            r[...] = val.reshape(gcn * CHUNK, DH)
        in_ref[...] = intra.reshape(gcn * CHUNK, CHUNK)
        tinv_ref[...] = tinv.reshape(gcn * CHUNK, CHUNK)
        egl_ref[...] = egl.reshape(gcn * 8, DH)

    hs = _sds((BL, NH, T, DH), F32)
    sqs = _sds((BL, NH, T, CHUNK), F32)
    return _pcall(
        body, name="dn_pre_fwd", grid=(BL, NH, ng),
        in_specs=[tok] * 5, out_specs=[tok, tok, tok, tok, sq, per, sq],
        out_shape=[hs, hs, hs, hs, sqs, _sds((BL, NH, T // CHUNK * 8, DH), F32), sqs],
        compiler_params=_cp(("arbitrary",) * 3),
    )(q, k, v, gb, bb)


def _dn_pre_bwd(q, k, v, gb, bb, tinv, du, dw, dqg, dkd, dintra, degl):
    gcn, ng = _pre_tiles()
    tok, sq, per = _dn_pre_specs()

    def body(q_ref, k_ref, v_ref, gb_ref, bb_ref, tinv_ref, du_ref, dw_ref, dqg_ref, dkd_ref, din_ref, degl_ref,
             dq_ref, dk_ref, dv_ref, dgb_ref, dbb_ref):
        args = [r[...].reshape(gcn, CHUNK, DH) for r in (q_ref, k_ref, v_ref, gb_ref, bb_ref)]
        known = tinv_ref[...].reshape(gcn, CHUNK, CHUNK)
        _, vjp = jax.vjp(lambda *a: _dn_pre(*a, tm_known=known)[:6], *args)
        cts = [r[...].reshape(gcn, CHUNK, DH) for r in (du_ref, dw_ref, dqg_ref, dkd_ref)]
        de = degl_ref[...].reshape(gcn, 8, DH)
        one = jnp.logical_and(lax.broadcasted_iota(jnp.int32, de.shape, 1) == 0,
                              lax.broadcasted_iota(jnp.int32, de.shape, 2) == 0)
        outs = vjp((*cts, din_ref[...].reshape(gcn, CHUNK, CHUNK), jnp.where(one, de, 0.0)))
        for r, val in zip((dq_ref, dk_ref, dv_ref, dgb_ref, dbb_ref), outs):
            r[...] = val.reshape(gcn * CHUNK, DH)

    hs = _sds((BL, NH, T, DH), F32)
    return _pcall(
        body, name="dn_pre_bwd", grid=(BL, NH, ng),
        in_specs=[tok] * 5 + [sq] + [tok] * 4 + [sq, per], out_specs=[tok] * 5, out_shape=[hs] * 5,
        compiler_params=_cp(("arbitrary",) * 3),
    )(q, k, v, gb, bb, tinv, du, dw, dqg, dkd, dintra, degl)


def _scan_tiles():
    cs = min(2, T // CHUNK)
    return cs, T // (CHUNK * cs)


def _dn_scan_fwd(u, w, qg, kd, intra, egl):
    cs, ns = _scan_tiles()
    g = BL * NH
    nc = T // CHUNK

    def body(u_ref, w_ref, qg_ref, kd_ref, in_ref, egl_ref, o_ref, vn_ref, s0_ref, s_scr):
        @pl.when(pl.program_id(0) == 0)
        def _():
            s_scr[...] = jnp.zeros_like(s_scr)

        for c in range(cs):
            rows = pl.ds(c * CHUNK, CHUNK)
            s = s_scr[...]
            s0_ref[:, :, c] = s.reshape(BL, NH, DH, DH)

            def ld(r, m=DH):
                return r[:, :, rows, :].reshape(g, CHUNK, m)

            vn = ld(u_ref) - _bmm_raw(ld(w_ref), s)
            o = _bmm_raw(ld(qg_ref), s) + _bmm_raw(ld(in_ref, CHUNK), vn)
            e = egl_ref[:, :, pl.ds(c * 8, 1), :].reshape(g, 1, DH)
            s_scr[...] = s * e + _bmm_tn_raw(ld(kd_ref), vn)
            vn_ref[:, :, rows, :] = vn.reshape(BL, NH, CHUNK, DH)
            o_ref[:, :, rows, :] = o.reshape(BL, NH, CHUNK, DH)

    tok = pl.BlockSpec((BL, NH, cs * CHUNK, DH), lambda i: (0, 0, i, 0))
    hs = _sds((BL, NH, T, DH), F32)
    return _pcall(
        body, name="dn_scan_fwd", grid=(ns,),
        in_specs=[tok, tok, tok, tok, pl.BlockSpec((BL, NH, cs * CHUNK, CHUNK), lambda i: (0, 0, i, 0)),
                  pl.BlockSpec((BL, NH, cs * 8, DH), lambda i: (0, 0, i, 0))],
        out_specs=[tok, tok, pl.BlockSpec((BL, NH, cs, DH, DH), lambda i: (0, 0, i, 0, 0))],
        out_shape=[hs, hs, _sds((BL, NH, nc, DH, DH), F32)],
        scratch_shapes=[pltpu.VMEM((g, DH, DH), F32)],
        compiler_params=_cp(("arbitrary",)),
    )(u, w, qg, kd, intra, egl)


def _dn_scan_bwd(do, w, qg, kd, intra, egl, vn, s0):
    cs, ns = _scan_tiles()
    g = BL * NH
    nc = T // CHUNK

    def body(do_ref, w_ref, qg_ref, kd_ref, in_ref, egl_ref, vn_ref, s0_ref,
             du_ref, dw_ref, dqg_ref, dkd_ref, din_ref, degl_ref, ds_scr):
        @pl.when(pl.program_id(0) == 0)
        def _():
            ds_scr[...] = jnp.zeros_like(ds_scr)

        for c in reversed(range(cs)):
            rows = pl.ds(c * CHUNK, CHUNK)

            def ld(r, m=DH):
                return r[:, :, rows, :].reshape(g, CHUNK, m)

            def st(r, val, m=DH):
                r[:, :, rows, :] = val.reshape(BL, NH, CHUNK, m)

            s = s0_ref[:, :, c].reshape(g, DH, DH)
            ds = ds_scr[...]
            dov, vnv, kdv, wv, qgv, inv = ld(do_ref), ld(vn_ref), ld(kd_ref), ld(w_ref), ld(qg_ref), ld(in_ref, CHUNK)
            dv = _bmm_tn_raw(inv, dov) + _bmm_raw(kdv, ds)
            st(din_ref, _bmm_nt_raw(dov, vnv), CHUNK)
            st(dqg_ref, _bmm_nt_raw(dov, s))
            st(dkd_ref, _bmm_nt_raw(vnv, ds))
            st(du_ref, dv)
            st(dw_ref, -_bmm_nt_raw(dv, s))
            de = jnp.sum(jnp.sum(ds * s, axis=2, keepdims=True), axis=1, keepdims=True)
            degl_ref[:, :, pl.ds(c * 8, 8), :] = jnp.broadcast_to(de, (g, 8, DH)).reshape(BL, NH, 8, DH)
            e = egl_ref[:, :, pl.ds(c * 8, 1), :].reshape(g, 1, DH)
            ds_scr[...] = ds * e + _bmm_tn_raw(qgv, dov) - _bmm_tn_raw(wv, dv)

    rev = lambda i: (0, 0, ns - 1 - i, 0)
    tok = pl.BlockSpec((BL, NH, cs * CHUNK, DH), rev)
    sq = pl.BlockSpec((BL, NH, cs * CHUNK, CHUNK), rev)
    per = pl.BlockSpec((BL, NH, cs * 8, DH), rev)
    hs = _sds((BL, NH, T, DH), F32)
    return _pcall(
        body, name="dn_scan_bwd", grid=(ns,),
        in_specs=[tok, tok, tok, tok, sq, per, tok,
                  pl.BlockSpec((BL, NH, cs, DH, DH), lambda i: (0, 0, ns - 1 - i, 0, 0))],
        out_specs=[tok, tok, tok, tok, sq, per],
        out_shape=[hs, hs, hs, hs, _sds((BL, NH, T, CHUNK), F32), _sds((BL, NH, nc * 8, DH), F32)],
        scratch_shapes=[pltpu.VMEM((g, DH, DH), F32)],
        compiler_params=_cp(("arbitrary",)),
    )(do, w, qg, kd, intra, egl, vn, s0)


def _gated_norm(o_h, z_h, og):
    r = lax.rsqrt(jnp.mean(o_h * o_h, axis=-1, keepdims=True) + EPS)
    return (o_h * r * og) * (z_h * jax.nn.sigmoid(z_h))


def _dn_gnorm_fwd(o, proj, o_g):
    tm = min(256, T)
    nt = T // tm

    def body(o_ref, z_ref, g_ref, og_ref):
        z = z_ref[...]
        for h in range(NH):
            og_ref[:, h * DH:(h + 1) * DH] = _gated_norm(o_ref[0, h], z[:, h * DH:(h + 1) * DH], g_ref[...]).astype(BF16)

    return _pcall(
        body, name="dn_gnorm_fwd", grid=(BL, nt),
        in_specs=[pl.BlockSpec((1, NH, tm, DH), lambda b, i: (b, 0, i, 0)),
                  pl.BlockSpec((tm, D), lambda b, i: (b * nt + i, 3)),
                  pl.BlockSpec((1, DH), lambda b, i: (0, 0))],
        out_specs=pl.BlockSpec((tm, D), lambda b, i: (b * nt + i, 0)),
        out_shape=_sds((BL * T, D), BF16),
        compiler_params=_cp(("arbitrary", "arbitrary")),
    )(o, proj, o_g)


def _dn_gnorm_bwd(dog, o, proj, o_g):
    tm = min(256, T)
    nt = T // tm

    def body(dog_ref, o_ref, z_ref, g_ref, do_ref, dz_ref, dg_ref):
        z = z_ref[...]
        dog = dog_ref[...]
        dg = jnp.zeros((1, DH), F32)
        for h in range(NH):
            cols = slice(h * DH, (h + 1) * DH)
            _, vjp = jax.vjp(_gated_norm, o_ref[0, h], z[:, cols], g_ref[...])
            do_h, dz_h, dg_h = vjp(dog[:, cols])
            do_ref[0, h] = do_h
            dz_ref[:, cols] = dz_h.astype(BF16)
            dg += dg_h
        _acc_rows(dg_ref, jnp.logical_and(pl.program_id(0) == 0, pl.program_id(1) == 0), [dg])

    return _pcall(
        body, name="dn_gnorm_bwd", grid=(BL, nt),
        in_specs=[pl.BlockSpec((tm, D), lambda b, i: (b * nt + i, 0)),
                  pl.BlockSpec((1, NH, tm, DH), lambda b, i: (b, 0, i, 0)),
                  pl.BlockSpec((tm, D), lambda b, i: (b * nt + i, 3)),
                  pl.BlockSpec((1, DH), lambda b, i: (0, 0))],
        out_specs=[pl.BlockSpec((1, NH, tm, DH), lambda b, i: (b, 0, i, 0)),
                   pl.BlockSpec((tm, D), lambda b, i: (b * nt + i, 0)),
                   pl.BlockSpec((1, DH), lambda b, i: (0, 0))],
        out_shape=[_sds((BL, NH, T, DH), F32), _sds((BL * T, D), BF16), _sds((1, DH), F32)],
        compiler_params=_cp(("arbitrary", "arbitrary")),
    )(dog, o, proj, o_g)


def _dn_prep_bwd(dq, dk, dv, dgb, dbb, conv, proj, alog, dtb):
    n = conv.shape[0]
    tt, nt = _conv_tiles()
    w3 = 3 * D

    def body(dq_ref, dk_ref, dv_ref, dgb_ref, dbb_ref, conv_ref, ab_ref, alog_ref, dtb_ref, dconv_ref, dab_ref, dhead_ref):
        cv = conv_ref[...]
        sg = jax.nn.sigmoid(cv)
        dact = sg * (1.0 + cv * (1.0 - sg))
        lane = lax.broadcasted_iota(jnp.int32, (tt, 128), 1)
        cg = jnp.zeros((tt, 128), F32)
        cb = jnp.zeros((tt, 128), F32)
        for h in range(NH):
            cols = slice(h * DH, (h + 1) * DH)
            dconv_ref[:, h * DH:(h + 1) * DH] = dq_ref[0, h] * dact[:, cols]
            dconv_ref[:, D + h * DH:D + (h + 1) * DH] = dk_ref[0, h] * dact[:, D + h * DH:D + (h + 1) * DH]
            dconv_ref[:, 2 * D + h * DH:2 * D + (h + 1) * DH] = dv_ref[0, h] * dact[:, 2 * D + h * DH:2 * D + (h + 1) * DH]
            cg = jnp.where(lane == h, jnp.sum(dgb_ref[0, h], axis=-1, keepdims=True), cg)
            cb = jnp.where(lane == NH + h, jnp.sum(dbb_ref[0, h], axis=-1, keepdims=True), cb)
        _, vjp = jax.vjp(_g_beta, ab_ref[...], alog_ref[...], dtb_ref[...])
        dab, dalog, ddtb = vjp((cg, cb))
        dab_ref[...] = dab.astype(BF16)
        _acc_rows(dhead_ref, jnp.logical_and(pl.program_id(0) == 0, pl.program_id(1) == 0), [dalog, ddtb])

    row = lambda b, i: (b * nt + i, 0)
    head = pl.BlockSpec((1, NH, tt, DH), lambda b, i: (b, 0, i, 0))
    vec = pl.BlockSpec((1, 128), lambda b, i: (0, 0))
    return _pcall(
        body, name="dn_prep_bwd", grid=(BL, nt),
        in_specs=[head] * 5 + [pl.BlockSpec((tt, w3), row),
                               pl.BlockSpec((tt, 128), lambda b, i: (b * nt + i, 4 * D // 128)), vec, vec],
        out_specs=[pl.BlockSpec((tt, w3), row), pl.BlockSpec((tt, 128), row), pl.BlockSpec((2, 128), lambda b, i: (0, 0))],
        out_shape=[_sds((n, w3), F32), _sds((n, 128), BF16), _sds((2, 128), F32)],
        compiler_params=_cp(("arbitrary", "arbitrary")),
    )(dq, dk, dv, dgb, dbb, conv, proj, alog, dtb)


def _dn_sconv_bwd(dconv, proj, w_sc):
    n = dconv.shape[0]
    tt, nt = _conv_tiles()
    w3 = 3 * D

    def body(dc_ref, nxt_ref, qkv_ref, halo_ref, w_ref, dpre_ref, dw_ref, dwin, pwin, rotd, rotp, dsc, accw):
        b, i = pl.program_id(0), pl.program_id(1)
        dwin[0:tt, :] = dc_ref[...]
        dwin[tt:tt + SHALO, :] = jnp.where(i == nt - 1, 0.0, nxt_ref[...])
        pwin[0:SHALO, :] = jnp.where(i == 0, 0.0, halo_ref[...])
        pwin[SHALO:SHALO + tt, :] = qkv_ref[...]
        for k in range(SCW - 1):
            rotd[k] = dwin[pl.ds(k + 1, tt), :]
            rotp[k] = pwin[pl.ds(SHALO - (SCW - 1) + k, tt), :]
        accw[...] = jnp.zeros_like(accw)

        def chunk(c, carry):
            r0 = pl.multiple_of(c * SROWS, SROWS)
            rows = pl.ds(r0, SROWS)
            dc = dc_ref[rows, :]
            dpre = w_ref[SCW - 1:SCW, :] * dc
            accw[SCW - 1] += dc * pwin[pl.ds(pl.multiple_of(r0 + SHALO, SROWS), SROWS), :]
            for k in range(SCW - 1):
                dpre += w_ref[k:k + 1, :] * rotd[SCW - 2 - k, rows, :]
                accw[k] += dc * rotp[k, rows, :]
            dsc[rows, :] = dpre
            return carry

        lax.fori_loop(0, tt // SROWS, chunk, 0)
        dpre_ref[...] = dsc[...].astype(BF16)
        dws = [jnp.sum(accw[k], axis=0, keepdims=True) for k in range(SCW)]
        dws += [jnp.zeros((1, w3), F32)] * (SHALO - SCW)
        _acc_rows(dw_ref, jnp.logical_and(b == 0, i == 0), dws)

    row = lambda b, i: (b * nt + i, 0)
    return _pcall(
        body, name="dn_sconv_bwd", grid=(BL, nt),
        in_specs=[pl.BlockSpec((tt, w3), row), _next_halo_spec(w3, tt, SHALO),
                  pl.BlockSpec((tt, w3), row), _prev_halo_spec(w3, tt, SHALO),
                  pl.BlockSpec((SHALO, w3), lambda b, i: (0, 0))],
        out_specs=[pl.BlockSpec((tt, w3), row), pl.BlockSpec((SHALO, w3), lambda b, i: (0, 0))],
        out_shape=[_sds((n, w3), BF16), _sds((SHALO, w3), F32)],
        scratch_shapes=[pltpu.VMEM((tt + SHALO, w3), F32), pltpu.VMEM((SHALO + tt, w3), F32),
                        pltpu.VMEM((SCW - 1, tt, w3), F32), pltpu.VMEM((SCW - 1, tt, w3), F32),
                        pltpu.VMEM((tt, w3), F32), pltpu.VMEM((SCW, SROWS, w3), F32)],
        compiler_params=_cp(("arbitrary", "arbitrary")),
    )(dconv, dconv, proj, proj, w_sc)


def _ada_fwd(c_all, w_ada, b_cols):
    nl, _, m = w_ada.shape
    nb = c_all.shape[0]

    def body(c_ref, w_ref, b_ref, o_ref):
        cv = c_ref[...]
        cs = (cv * jax.nn.sigmoid(cv)).astype(BF16)
        o_ref[...] = _dot(cs, w_ref[...].astype(BF16)) + b_ref[...]

    return _pcall(
        body, name="ada_fwd", grid=(nl,),
        in_specs=[pl.BlockSpec((nb, D), lambda l: (0, 0)), pl.BlockSpec((None, D, m), lambda l: (l, 0, 0)),
                  pl.BlockSpec((None, 1, m), lambda l: (l, 0, 0))],
        out_specs=pl.BlockSpec((None, nb, m), lambda l: (l, 0, 0)),
        out_shape=_sds((nl, nb, m), F32),
        compiler_params=_cp(("arbitrary",)),
    )(c_all, w_ada, b_cols)


def _ada_bwd(c_all, dmod_cols):
    nl, nb, m = dmod_cols.shape

    def body(c_ref, d_ref, o_ref):
        cv = c_ref[...]
        cs = (cv * jax.nn.sigmoid(cv)).astype(BF16)
        o_ref[0] = _dot_tn(cs, d_ref[...].astype(BF16))

    return _pcall(
        body, name="ada_bwd", grid=(nl,),
        in_specs=[pl.BlockSpec((nb, D), lambda l: (0, 0)), pl.BlockSpec((None, nb, m), lambda l: (l, 0, 0))],
        out_specs=pl.BlockSpec((1, D, m), lambda l: (0, l, 0)),
        out_shape=_sds((1, nl * D, m), F32),
        compiler_params=_cp(("arbitrary",)),
    )(c_all, dmod_cols)


def _loss_head(x, tgt, fg):
    n = x.shape[0]
    tm = min(512, T)

    def f(xv, g, t):
        r = lax.rsqrt(jnp.mean(xv * xv, axis=-1, keepdims=True) + EPS)
        e = xv * r * g - t
        return 0.5 * jnp.sum(e * e, axis=0, keepdims=True) * (1.0 / D)

    def body(x_ref, t_ref, g_ref, dx_ref, st_ref):
        t = t_ref[...]
        lrow, vjp = jax.vjp(lambda xv, g: f(xv, g, t), x_ref[...], g_ref[...])
        dx, dg = vjp(jnp.ones_like(lrow))
        dx_ref[...] = dx
        _acc_rows(st_ref, pl.program_id(0) == 0, [dg, lrow])

    return _pcall(
        body, name="loss_head", grid=(n // tm,),
        in_specs=[pl.BlockSpec((tm, D), lambda i: (i, 0)), pl.BlockSpec((tm, D), lambda i: (i, 0)),
                  pl.BlockSpec((1, D), lambda i: (0, 0))],
        out_specs=[pl.BlockSpec((tm, D), lambda i: (i, 0)), pl.BlockSpec((2, D), lambda i: (0, 0))],
        out_shape=[_sds((n, D), F32), _sds((2, D), F32)],
        compiler_params=_cp(("arbitrary",)),
    )(x, tgt, fg)


def _adamw(parts, w, m, v):
    p, r, c = parts.shape
    tr = r
    for cand in (256, 128, 64, 32, 16, 8):
        if r % cand == 0:
            tr = cand
            break
    k1 = 1.0 - B1 ** STEP
    k2 = 1.0 - B2 ** STEP

    def body(p_ref, w_ref, m_ref, v_ref, g_ref, d_ref, nm_ref, nv_ref):
        g = p_ref[0].astype(F32)
        for q in range(1, p):
            g += p_ref[q].astype(F32)
        mn = B1 * m_ref[...] + (1.0 - B1) * g
        vn = B2 * v_ref[...] + (1.0 - B2) * (g * g)
        g_ref[...] = g
        nm_ref[...] = mn
        nv_ref[...] = vn
        d_ref[...] = -LR * ((mn / k1) / (jnp.sqrt(vn / k2) + AEPS) + WD * w_ref[...])

    blk = pl.BlockSpec((tr, c), lambda i: (i, 0))
    return _pcall(
        body, name="adamw", grid=(r // tr,),
        in_specs=[pl.BlockSpec((p, tr, c), lambda i: (0, i, 0)), blk, blk, blk],
        out_specs=[blk] * 4, out_shape=[_sds((r, c), F32)] * 4,
        compiler_params=_cp(("arbitrary",)),
    )(parts, w, m, v)


def _sum_parts(parts):
    p, r, c = parts.shape

    def body(p_ref, o_ref):
        acc = p_ref[0]
        for q in range(1, p):
            acc += p_ref[q]
        o_ref[...] = acc

    return _pcall(body, name="sum_parts", out_shape=_sds((r, c), F32))(parts)


def _adamw_slot(parts, w, m, v, outs, row0, col):
    p, r, c = parts.shape
    tr = r
    for cand in (256, 128, 64, 32, 16, 8):
        if r % cand == 0:
            tr = cand
            break
    if r % 352 == 0:
        tr = 352
    nt = r // tr
    k1 = 1.0 - B1 ** STEP
    k2 = 1.0 - B2 ** STEP

    def body(p_ref, w_ref, m_ref, v_ref, g0, d0, m0, v0, g_ref, d_ref, nm_ref, nv_ref):
        g = p_ref[0].astype(F32)
        for q in range(1, p):
            g += p_ref[q].astype(F32)
        mn = B1 * m_ref[...] + (1.0 - B1) * g
        vn = B2 * v_ref[...] + (1.0 - B2) * (g * g)
        g_ref[...] = g
        nm_ref[...] = mn
        nv_ref[...] = vn
        d_ref[...] = -LR * ((mn / k1) / (jnp.sqrt(vn / k2) + AEPS) + WD * w_ref[...])

    blk = pl.BlockSpec((tr, c), lambda i: (row0 * nt + i, col))
    anyspec = pl.BlockSpec(memory_space=pl.ANY)
    return _pcall(
        body, name="adamw_slot", grid=(nt,),
        in_specs=[pl.BlockSpec((p, tr, c), lambda i: (0, i, 0)), blk, blk, blk] + [anyspec] * 4,
        out_specs=[blk] * 4, out_shape=[_sds(w.shape, F32)] * 4,
        input_output_aliases={4: 0, 5: 1, 6: 2, 7: 3},
        compiler_params=_cp(("arbitrary",)),
    )(parts, w, m, v, *outs)


def _pack(arrs):
    flat = jnp.concatenate([a.reshape(-1) for a in arrs])
    pad = (-flat.shape[0]) % 1024
    return jnp.pad(flat, (0, pad)).reshape(-1, 128)


def _unpack(buf, shapes):
    flat = buf.reshape(-1)
    out, off = [], 0
    for s in shapes:
        size = 1
        for d in s:
            size *= d
        out.append(flat[off:off + size].reshape(s))
        off += size
    return out


def kernel(x, c, norm_g, w_ada, b_ada, w_ffn_in, w_ffn_out, cm_w_glu, cm_b_glu, cm_w_dw, cm_b_dw, cm_ln_g, cm_ln_b, cm_w_pw, cm_b_pw, dn_w_in, dn_w_sconv, dn_a_log, dn_dt_bias, dn_o_g, dn_w_out, final_g, loss_target, m_norm_g, m_w_ada, m_b_ada, m_w_ffn_in, m_w_ffn_out, m_cm_w_glu, m_cm_b_glu, m_cm_w_dw, m_cm_b_dw, m_cm_ln_g, m_cm_ln_b, m_cm_w_pw, m_cm_b_pw, m_dn_w_in, m_dn_w_sconv, m_dn_a_log, m_dn_dt_bias, m_dn_o_g, m_dn_w_out, m_final_g, v_norm_g, v_w_ada, v_b_ada, v_w_ffn_in, v_w_ffn_out, v_cm_w_glu, v_cm_b_glu, v_cm_w_dw, v_cm_b_dw, v_cm_ln_g, v_cm_ln_b, v_cm_w_pw, v_cm_b_pw, v_dn_w_in, v_dn_w_sconv, v_dn_a_log, v_dn_dt_bias, v_dn_o_g, v_dn_w_out, v_final_g):
    me = 4 * lax.axis_index("x") + 2 * lax.axis_index("y") + lax.axis_index("c")
    n = BL * T
    nf = 4
    tf = FF // nf
    na, nb = cm_w_glu.shape[0], dn_w_in.shape[0]
    mcols = w_ada.shape[2]
    dsh = D // NDEV

    tr_ffn = lambda a: jnp.swapaxes(a, 2, 3)
    tr_dn = lambda a: jnp.transpose(a, (2, 0, 1))
    wt_ffn_in, wt_dn_in = tr_ffn(w_ffn_in), tr_dn(dn_w_in)

    def unit_weights(l, part):
        if part == 0:
            ws = (wt_ffn_in[l, 0], w_ffn_out[l, 0])
        else:
            mix = (cm_w_glu[l // 2], cm_w_pw[l // 2]) if l % 2 == 0 else (wt_dn_in[:, l // 2], dn_w_out[l // 2])
            ws = (wt_ffn_in[l, 1], w_ffn_out[l, 1], *mix)
        return [w.astype(BF16) for w in ws]

    gathers, all_started = {}, jnp.zeros((8, 128), F32)
    for l in range(DEPTH):
        for part in range(2):
            if l < RELAY_LAYERS:
                gathers[l, part], tok = _relay_gather_start(unit_weights(l, part), f"gather_start_{l}_{part}")
            else:
                gathers[l, part], tok = _exchange_start(unit_weights(l, part), True, f"gather_start_{l}_{part}")
            all_started = all_started + tok

    c_g, ng_g, dw_g, sc_g = _all_gather([c, norm_g, cm_w_dw, dn_w_sconv], "gather_small")
    whole = lambda g: jnp.moveaxis(g, 0, -2).reshape(*g.shape[1:-1], -1)
    c_all = c_g.reshape(NDEV * BL, D)
    norm_g_f, w_dw_f, w_sc_f = whole(ng_g), whole(dw_g), whole(sc_g)

    b_cols = lax.dynamic_slice_in_dim(b_ada, me * mcols, mcols, axis=1)[:, None, :]
    mod_cols = _ada_fwd(c_all, w_ada, b_cols)
    mod_g, = _all_gather([mod_cols], "gather_mod")
    mod_all = jnp.transpose(mod_g, (1, 2, 0, 3)).reshape(DEPTH, NDEV * BL, 9 * D)
    mod = lax.dynamic_slice_in_dim(mod_all, me * BL, BL, axis=1).reshape(DEPTH, BL, 3, 3, D)

    gathered = [None] * DEPTH

    def gather_wait(l, part, after):
        if l < RELAY_LAYERS:
            passed = _relay_gather_pass(gathers[l, part], after, f"gather_pass_{l}_{part}")
            return _exchange_wait(passed, after, f"gather_wait_{l}_{part}", blocks=3)
        return _exchange_wait(gathers[l, part], after, f"gather_wait_{l}_{part}")

    def ffn_weights(l, s):
        return gathered[l][s].reshape(2, nf, tf, D), gathered[l][2 + s].reshape(nf, tf, D)

    xs = x.reshape(n, D)
    saved = []
    for l in range(DEPTH):
        rec = {}
        ga = gather_wait(l, 0, all_started if l == 0 else xs)
        gathered[l] = [ga[0], None, ga[1], None, None, None]
        for s, j in ((0, 0), (1, 2)):
            if j == 2:
                gb = gather_wait(l, 1, xs)
                gathered[l] = [ga[0], gb[0], ga[1], gb[1], gb[2], gb[3]]
            w_in, w_out = ffn_weights(l, s)
            ssg, ng = mod[l, :, j], norm_g_f[l, j][None]
            if j == 2:
                ssg1, ng1 = mod[l, :, 1], norm_g_f[l, 1][None]
                if l % 2 == 0:
                    a = l // 2
                    w_glu = gathered[l][4]
                    w_pw = gathered[l][5].reshape(D, D)
                    w_dw = jnp.pad(w_dw_f[a], ((0, HALO - CW), (0, 0)))
                    h1, ab = _premod_matmul(xs, ssg1, ng1, w_glu, cm_b_glu[a][None], w_glu.shape[2])
                    u1, u2 = _cm_mid_fwd(ab, w_dw, cm_b_dw[a][None], cm_ln_g[a][None], cm_ln_b[a][None])
                    xn, ymix = _matmul_res(xs, u2, ssg1, w_pw, cm_b_pw[a][None])
                    rec["mix"] = dict(x=xs, h=h1, ab=ab, u1=u1, u2=u2, y=ymix, w_glu=w_glu, w_pw=w_pw, w_dw=w_dw)
                else:
                    mi = l // 2
                    w_proj = jnp.pad(gathered[l][4].reshape(4 * D + 2 * NH, D), ((0, 128 - 2 * NH), (0, 0)))
                    w_o = gathered[l][5].reshape(D, D)
                    w_sc = jnp.pad(w_sc_f[mi], ((0, SHALO - SCW), (0, 0)))
                    alog = jnp.pad(dn_a_log[mi], (0, 128 - NH))[None]
                    dtb = jnp.pad(dn_dt_bias[mi], (0, 128 - NH))[None]
                    h1, proj = _premod_matmul(xs, ssg1, ng1, w_proj, jnp.zeros((1, w_proj.shape[0]), F32),
                                              (4 * D + 128) // 3 if (4 * D + 128) % 384 == 0 else 128)
                    conv, q, k, v, gb, bb = _dn_sconv_fwd(proj, w_sc, alog, dtb)
                    u, w, qg, kd, intra, egl, tinv = _dn_pre_fwd(q, k, v, gb, bb)
                    o, vn, s0 = _dn_scan_fwd(u, w, qg, kd, intra, egl)
                    og = _dn_gnorm_fwd(o, proj, dn_o_g[mi][None])
                    xn, ymix = _matmul_res(xs, og, ssg1, w_o, jnp.zeros((1, D), F32))
                    rec["mix"] = dict(x=xs, h=h1, proj=proj, conv=conv, q=q, k=k, v=v, gb=gb, bb=bb, w=w, qg=qg, kd=kd,
                                      intra=intra, egl=egl, tinv=tinv, o=o, vn=vn, s0=s0, og=og, y=ymix, w_proj=w_proj, w_o=w_o,
                                      w_sc=w_sc, alog=alog, dtb=dtb)
                xs = xn
            xn, gu, hid, y = _ffn_fwd(xs, ssg, ng, w_in, w_out)
            rec[s] = dict(x=xs, gu=gu, hid=hid, y=y)
            xs = xn
        saved.append(rec)

    dx, stats = _loss_head(xs, loss_target.reshape(n, D), final_g[None])
    loss = lax.psum(jnp.sum(stats[1]), AXES)
    d_final_g = stats[0]

    d_mod = [[None] * 3 for _ in range(DEPTH)]
    d_norm = [[None] * 3 for _ in range(DEPTH)]
    dw_ffn_in = [[None] * 2 for _ in range(DEPTH)]
    dw_ffn_out = [[None] * 2 for _ in range(DEPTH)]
    dcm = [dict() for _ in range(na)]
    ddn = [dict() for _ in range(nb)]
    exchanges = {}

    def gather_small_grads():
        dmod_loc = jnp.stack([jnp.stack(d_mod[l], axis=1) for l in range(DEPTH)]).reshape(DEPTH, BL, 9 * D)
        small = [jnp.sum(dmod_loc, axis=1), jnp.stack([jnp.stack(d_norm[l]) for l in range(DEPTH)]),
                 jnp.stack([d["b_glu"] for d in dcm]), jnp.stack([d["w_dw"] for d in dcm]), jnp.stack([d["b_dw"] for d in dcm]),
                 jnp.stack([d["ln_g"] for d in dcm]), jnp.stack([d["ln_b"] for d in dcm]), jnp.stack([d["b_pw"] for d in dcm]),
                 jnp.stack([d["w_sconv"] for d in ddn]), jnp.stack([d["a_log"] for d in ddn]),
                 jnp.stack([d["dt_bias"] for d in ddn]), jnp.stack([d["o_g"] for d in ddn]), d_final_g]
        dmod_g, small_parts = _all_gather([dmod_loc, _pack(small)], "gather_small_grads")
        return dmod_g, small_parts, [a.shape for a in small]

    token = jnp.zeros((), F32)
    for l in reversed(range(DEPTH)):
        rec = saved[l]
        for s, j in ((1, 2), (0, 0)):
            w_in, w_out = ffn_weights(l, s)
            ssg, ng = mod[l, :, j] + token, norm_g_f[l, j][None]
            r = rec[s]
            dx, dgu, hb, dout, dssg, dng = _ffn_bwd_a(r["x"], dx, ssg, ng, r["y"], r["gu"], w_in, w_out)
            dw_ffn_in[l][s], dw_ffn_out[l][s] = _ffn_bwd_w(hb, dgu, r["hid"], dout)
            d_mod[l][j], d_norm[l][j] = dssg, dng[0]
            if j == 2:
                ssg1, ng1 = mod[l, :, 1], norm_g_f[l, 1][None]
                r = rec["mix"]
                if l % 2 == 0:
                    a = l // 2
                    du2, dy, dgate, db_pw = _matmul_res_bwd(dx, r["y"], ssg1, r["w_pw"])
                    du1, dln = _cm_mid_bwd_a(du2, r["u1"], cm_ln_g[a][None], cm_ln_b[a][None])
                    dab, dw_dw, db_dw, db_glu = _cm_mid_bwd_b(du1, r["ab"], r["w_dw"])
                    dx, dssg, dng = _premod_matmul_bwd(r["x"], dx, ssg1, ng1, [dab], r["w_glu"])
                    dcm[a] = dict(w_glu=_wgrad_shards(r["h"], dab, NDEV), w_pw=_wgrad(r["u2"], dy).reshape(NDEV, dsh, D),
                                  b_glu=db_glu[0], w_dw=dw_dw[:CW], b_dw=db_dw[0], ln_g=dln[0], ln_b=dln[1], b_pw=db_pw[0])
                else:
                    mi = l // 2
                    dog, dy, dgate, _ = _matmul_res_bwd(dx, r["y"], ssg1, r["w_o"])
                    do, dz, d_og = _dn_gnorm_bwd(dog, r["o"], r["proj"], dn_o_g[mi][None])
                    du, dw, dqg, dkd, dintra, degl = _dn_scan_bwd(do, r["w"], r["qg"], r["kd"], r["intra"], r["egl"],
                                                                   r["vn"], r["s0"])
                    dq, dk, dv, dgb, dbb = _dn_pre_bwd(r["q"], r["k"], r["v"], r["gb"], r["bb"], r["tinv"],
                                                       du, dw, dqg, dkd, dintra, degl)
                    dconv, dab16, dhead = _dn_prep_bwd(dq, dk, dv, dgb, dbb, r["conv"], r["proj"], r["alog"], r["dtb"])
                    dpre, dw_sc = _dn_sconv_bwd(dconv, r["proj"], r["w_sc"])
                    dx, dssg, dng = _premod_matmul_bwd(r["x"], dx, ssg1, ng1, [dpre, dz, dab16], r["w_proj"])
                    dw_in = jnp.concatenate([_wgrad(dpre, r["h"]), _wgrad(dz, r["h"]),
                                             _wgrad(dab16, r["h"])[:2 * NH]], axis=0)
                    ddn[mi] = dict(w_in=dw_in.reshape(NDEV, -1, D), w_out=_wgrad(r["og"], dy).reshape(NDEV, dsh, D),
                                   w_sconv=dw_sc[:SCW], a_log=dhead[0, :NH], dt_bias=dhead[1, :NH], o_g=d_og[0])
                d_mod[l][1] = dssg.at[:, 2].set(dgate[:, 0])
                d_norm[l][1] = dng[0]
            unit = [dw_ffn_in[l][s].reshape(NDEV, tf, D), dw_ffn_out[l][s].reshape(NDEV, FF // NDEV, D)]
            if j == 2:
                g = dcm[l // 2] if l % 2 == 0 else ddn[l // 2]
                unit += [g["w_glu"], g["w_pw"]] if l % 2 == 0 else [g["w_in"], g["w_out"]]
            if l == 0 and s == 0:
                dmod_g, small_parts, full_shapes = gather_small_grads()
                unit[0], dmod_g, small_parts = lax.optimization_barrier((unit[0], dmod_g, small_parts))
                small_gathered = (dmod_g, small_parts, full_shapes)
            exchanges[l, s], token = _exchange_start(unit, False, f"grads_start_{l}_{s}")
    grad_x = dx.reshape(BL, T, D)

    dmod_g, small_parts, full_shapes = small_gathered
    dmod_all = jnp.transpose(dmod_g, (1, 0, 2, 3)).reshape(DEPTH, NDEV * BL, 9 * D)
    g_w_ada = _ada_bwd(c_all, lax.dynamic_slice_in_dim(dmod_all, me * mcols, mcols, axis=2))

    got = []
    for l in range(DEPTH):
        ea = _exchange_wait(exchanges[l, 0], dx, f"grads_wait_{l}_0") if l > 0 else [None, None]
        eb = _exchange_wait(exchanges[l, 1], dx, f"grads_wait_{l}_1")
        got.append([ea[0], eb[0], ea[1], eb[1], eb[2], eb[3]])

    names = ["b_ada", "norm_g", "cm_b_glu", "cm_w_dw", "cm_b_dw", "cm_ln_g", "cm_ln_b", "cm_b_pw",
             "dn_w_sconv", "dn_a_log", "dn_dt_bias", "dn_o_g", "final_g"]
    cols = lambda a, width: lax.dynamic_slice_in_dim(a, me * width, width, axis=a.ndim - 1)
    local = {"norm_g": lambda a: cols(a, dsh), "cm_w_dw": lambda a: cols(a, dsh), "dn_w_sconv": lambda a: cols(a, 3 * dsh)}
    summed = _unpack(_sum_parts(small_parts), full_shapes)
    mine = [local.get(nm, lambda a: a)(p) for nm, p in zip(names, summed)]
    small_w = dict(b_ada=(b_ada, m_b_ada, v_b_ada), norm_g=(norm_g, m_norm_g, v_norm_g),
                   cm_b_glu=(cm_b_glu, m_cm_b_glu, v_cm_b_glu), cm_w_dw=(cm_w_dw, m_cm_w_dw, v_cm_w_dw),
                   cm_b_dw=(cm_b_dw, m_cm_b_dw, v_cm_b_dw), cm_ln_g=(cm_ln_g, m_cm_ln_g, v_cm_ln_g),
                   cm_ln_b=(cm_ln_b, m_cm_ln_b, v_cm_ln_b), cm_b_pw=(cm_b_pw, m_cm_b_pw, v_cm_b_pw),
                   dn_w_sconv=(dn_w_sconv, m_dn_w_sconv, v_dn_w_sconv), dn_a_log=(dn_a_log, m_dn_a_log, v_dn_a_log),
                   dn_dt_bias=(dn_dt_bias, m_dn_dt_bias, v_dn_dt_bias), dn_o_g=(dn_o_g, m_dn_o_g, v_dn_o_g),
                   final_g=(final_g, m_final_g, v_final_g))
    loc_shapes = [small_w[nm][0].shape for nm in names]
    sres_raw = _adamw(_pack(mine)[None], *[_pack([small_w[nm][q] for nm in names]) for q in range(3)])
    sres = [dict(zip(names, _unpack(r, loc_shapes))) for r in sres_raw]

    res = {}

    def update(slots, wmv, view, back=None, outs=None):
        w2, m2, v2 = [view(a) for a in wmv]
        outs = [lax.empty(w2.shape, F32) for _ in range(4)] if outs is None else outs
        for p, row0, col in slots:
            outs = _adamw_slot(p, w2, m2, v2, outs, row0, col)
        return outs if back is None else [back(o) for o in outs]

    ffn_slots = [(l, s) for l in reversed(range(DEPTH)) for s in (1, 0)][:-1]
    wmv_in, view_in = (w_ffn_in, m_w_ffn_in, v_w_ffn_in), lambda a: tr_ffn(a).reshape(-1, D)
    wmv_out, view_out = (w_ffn_out, m_w_ffn_out, v_w_ffn_out), lambda a: a.reshape(-1, D)
    part_in = update([(got[l][s], 2 * l + s, 0) for l, s in ffn_slots], wmv_in, view_in)
    part_out = update([(got[l][2 + s], 2 * l + s, 0) for l, s in ffn_slots], wmv_out, view_out)
    cgl = cm_w_glu.shape[2]
    res["cm_w_glu"] = update([(got[2 * a][4], a, 0) for a in range(na)], (cm_w_glu, m_cm_w_glu, v_cm_w_glu),
                             lambda a: a.reshape(-1, cgl), lambda o: o.reshape(cm_w_glu.shape))
    res["cm_w_pw"] = update([(got[2 * a][5], a, 0) for a in range(na)], (cm_w_pw, m_cm_w_pw, v_cm_w_pw),
                            lambda a: a.reshape(-1, D), lambda o: o.reshape(cm_w_pw.shape))
    cdn = dn_w_in.shape[2]
    res["dn_w_in"] = update([(got[2 * i + 1][4], 0, i) for i in range(nb)], (dn_w_in, m_dn_w_in, v_dn_w_in),
                            lambda a: tr_dn(a).reshape(cdn, nb * D),
                            lambda o: jnp.transpose(o.reshape(cdn, nb, D), (1, 2, 0)))
    res["dn_w_out"] = update([(got[2 * i + 1][5], i, 0) for i in range(nb)], (dn_w_out, m_dn_w_out, v_dn_w_out),
                             lambda a: a.reshape(-1, D), lambda o: o.reshape(dn_w_out.shape))
    res["w_ada"] = [o.reshape(w_ada.shape) for o in
                    _adamw(g_w_ada, *[a.reshape(-1, mcols) for a in (w_ada, m_w_ada, v_w_ada)])]
    done = [part_in[0], part_out[0], sres_raw[0]] + [res[nm][0] for nm in ("cm_w_glu", "cm_w_pw", "dn_w_in", "dn_w_out", "w_ada")]
    last = _exchange_wait(exchanges[0, 0], done, "grads_wait_0_0")
    res["w_ffn_in"] = update([(last[0], 0, 0)], wmv_in, view_in,
                             lambda o: jnp.swapaxes(o.reshape(DEPTH, 2, tf, D), 2, 3), part_in)
    res["w_ffn_out"] = update([(last[1], 0, 0)], wmv_out, view_out, lambda o: o.reshape(w_ffn_out.shape), part_out)
    for nm in names:
        res[nm] = [sres[q][nm] for q in range(4)]

    order = ["norm_g", "w_ada", "b_ada", "w_ffn_in", "w_ffn_out", "cm_w_glu", "cm_b_glu", "cm_w_dw", "cm_b_dw", "cm_ln_g",
             "cm_ln_b", "cm_w_pw", "cm_b_pw", "dn_w_in", "dn_w_sconv", "dn_a_log", "dn_dt_bias", "dn_o_g", "dn_w_out", "final_g"]
    return (loss, grad_x, *[res[nm][0] for nm in order], *[res[nm][1] for nm in order],
            *[res[nm][2] for nm in order], *[res[nm][3] for nm in order])
```

```python
import functools

import jax
import jax.numpy as jnp
from jax import lax
from jax.experimental import pallas as pl
from jax.experimental.pallas import tpu as pltpu

F32 = jnp.float32
BF16 = jnp.bfloat16
HI = lax.Precision.HIGHEST
INV_PREC = None
MESH = pl.DeviceIdType.MESH
AXES = ("x", "y", "c")

NDEV = 8
D = 1024
T = 2048
BL = 2
FF = 2816
NH = 8
DH = 128
CW = 31
SCW = 4
CHUNK = 64
DEPTH = 4
EPS = 1e-6
LR, B1, B2, AEPS, WD, STEP = 0.001, 0.9, 0.999, 1e-08, 0.01, 10

VMEM_LIMIT_BYTES = 56 * 1024 * 1024
RELAY_LAYERS = 2
HALO = 32
SHALO = 8


def _pcall(body, **kw):
    return pl.pallas_call(body, **kw)


def _cp(sem=None):
    return pltpu.CompilerParams(dimension_semantics=sem, vmem_limit_bytes=VMEM_LIMIT_BYTES)


def _sds(shape, dtype):
    return jax.ShapeDtypeStruct(tuple(shape), dtype)


def _dot(a, b):
    return jnp.dot(a, b, preferred_element_type=F32)


def _dot_nt(a, b):
    return lax.dot_general(a, b, (((1,), (1,)), ((), ())), preferred_element_type=F32)


def _dot_tn(a, b):
    return lax.dot_general(a, b, (((0,), (0,)), ((), ())), preferred_element_type=F32)


def _modulate(x, ng, scale, shift):
    r = lax.rsqrt(jnp.mean(x * x, axis=-1, keepdims=True) + EPS)
    return (x * r * ng) * (1.0 + scale) + shift


def _modulate_bwd(x, ng, scale, shift, dh):
    r = lax.rsqrt(jnp.mean(x * x, axis=-1, keepdims=True) + EPS)
    xh = x * r
    xg = xh * ng
    h = xg * (1.0 + scale) + shift
    a = dh * (1.0 + scale)
    dxh = a * ng
    dx = r * (dxh - xh * jnp.mean(dxh * xh, axis=-1, keepdims=True))
    return (h, dx, jnp.sum(a * xh, axis=0, keepdims=True), jnp.sum(dh * xg, axis=0, keepdims=True),
            jnp.sum(dh, axis=0, keepdims=True))


def _acc_rows(ref, first, rows):
    @pl.when(first)
    def _():
        ref[...] = jnp.zeros_like(ref)

    for r, val in enumerate(rows):
        ref[r:r + 1, :] += val


def _my_pos():
    return lax.axis_index("x"), lax.axis_index("y"), lax.axis_index("c")


def _all_gather(arrs, name):
    n = len(arrs)

    def body(*refs):
        ins, outs = refs[:n], refs[n:2 * n]
        send, recv, loc = refs[2 * n:]
        x, y, c = _my_pos()
        me, sibling = (x, y, c), (x, y, 1 - c)
        chips = [(1 - x, y), (x, 1 - y), (1 - x, 1 - y)]

        def copy(a, k, block, to, src=None):
            dst = outs[a].at[4 * block[0] + 2 * block[1] + block[2]]
            return pltpu.make_async_remote_copy(
                src_ref=dst if src is None else src, dst_ref=dst,
                send_sem=send.at[7 * a + k], recv_sem=recv.at[7 * a + k],
                device_id=to, device_id_type=MESH)

        mine, first, passed = [], [], []
        for a in range(n):
            m = pltpu.make_async_copy(ins[a], outs[a].at[4 * x + 2 * y + c], loc.at[a])
            m.start()
            mine.append(m)
            f = [copy(a, 0, me, sibling, src=ins[a])]
            f += [copy(a, 1 + j, me, (*chip, c), src=ins[a]) for j, chip in enumerate(chips)]
            for cp in f:
                cp.start()
            first += f
        for a in range(n):
            for j, chip in enumerate(chips):
                copy(a, 1 + j, (*chip, c), me).wait_recv()
                p = copy(a, 4 + j, (*chip, c), sibling)
                p.start()
                passed.append(p)
        for a in range(n):
            copy(a, 0, sibling, me).wait_recv()
            for j, chip in enumerate(chips):
                copy(a, 4 + j, (*chip, 1 - c), me).wait_recv()
        for cp in first + passed:
            cp.wait_send()
        for m in mine:
            m.wait()

    hbm = pl.BlockSpec(memory_space=pl.ANY)
    return _pcall(
        body, name=name,
        out_shape=[_sds((NDEV,) + a.shape, a.dtype) for a in arrs],
        in_specs=[hbm] * n, out_specs=[hbm] * n,
        scratch_shapes=[pltpu.SemaphoreType.DMA((7 * n,)), pltpu.SemaphoreType.DMA((7 * n,)),
                        pltpu.SemaphoreType.DMA((n,))],
    )(*arrs)


def _peer(k):
    x, y, c = _my_pos()
    return (1 - x if k & 4 else x, 1 - y if k & 2 else y, 1 - c if k & 1 else c)


def _dev_index(p):
    return 4 * p[0] + 2 * p[1] + p[2]


_HBM = pl.BlockSpec(memory_space=pltpu.HBM)
_SEM = pl.BlockSpec(memory_space=pltpu.SEMAPHORE)
_EFFECT = pltpu.SideEffectType.DATAFLOW_SIDE_EFFECTING


def _exchange_start(srcs, gather, name):
    n = len(srcs)
    me = _dev_index(_my_pos())
    lands = []
    for s in srcs:
        own = s if gather else lax.dynamic_index_in_dim(s, me, 0, keepdims=False)
        shape = (NDEV,) + s.shape if gather else s.shape
        lands.append(lax.dynamic_update_index_in_dim(lax.empty(shape, s.dtype), own, me, 0))

    def body(*refs):
        src_refs, land_refs = refs[:n], refs[n:2 * n]
        sends, recvs = refs[2 * n:3 * n], refs[3 * n:4 * n]
        token = refs[-1]
        mine = _dev_index(_my_pos())
        for a in range(n):
            for k in range(1, 8):
                p = _peer(k)
                pltpu.make_async_remote_copy(
                    src_ref=src_refs[a] if gather else src_refs[a].at[_dev_index(p)],
                    dst_ref=land_refs[a].at[mine], send_sem=sends[a], recv_sem=recvs[a],
                    device_id=p, device_id_type=MESH).start()
        token[...] = jnp.zeros_like(token)

    out = pl.pallas_call(
        body, name=name,
        out_shape=(*[pltpu.SemaphoreType.DMA(())] * (2 * n),
                   *[pltpu.HBM(a.shape, a.dtype) for a in srcs], *[pltpu.HBM(a.shape, a.dtype) for a in lands],
                   _sds((8, 128), F32)),
        in_specs=[_HBM] * (2 * n),
        out_specs=(*[_SEM] * (2 * n), *[_HBM] * (2 * n), pl.BlockSpec(memory_space=pltpu.VMEM)),
        input_output_aliases={i: 2 * n + i for i in range(2 * n)},
        compiler_params=pltpu.CompilerParams(has_side_effects=_EFFECT),
    )(*[pltpu.with_memory_space_constraint(a, pltpu.HBM) for a in srcs],
      *[pltpu.with_memory_space_constraint(a, pltpu.HBM) for a in lands])
    state = (out[:n], out[n:2 * n], out[2 * n:3 * n], out[3 * n:4 * n])
    return state, out[-1][0, 0]


def _exchange_wait(state, after, name, blocks=NDEV - 1):
    sends, recvs, srcs, lands = state
    n = len(srcs)
    after = list(after) if isinstance(after, (list, tuple)) else [after]

    def body(*refs):
        land_refs = refs[n:2 * n]
        send_refs, recv_refs = refs[2 * n:3 * n], refs[3 * n:4 * n]
        for a in range(n):
            seven = land_refs[a].at[pl.ds(0, blocks)]
            cp = pltpu.make_async_remote_copy(src_ref=seven, dst_ref=seven, send_sem=send_refs[a], recv_sem=recv_refs[a],
                                              device_id=_peer(1), device_id_type=MESH)
            cp.wait_send()
            cp.wait_recv()

    out = pl.pallas_call(
        body, name=name,
        out_shape=(*[pltpu.HBM(a.shape, a.dtype) for a in srcs], *[pltpu.HBM(a.shape, a.dtype) for a in lands]),
        in_specs=(*[_HBM] * (2 * n), *[_SEM] * (2 * n), *[pl.BlockSpec(memory_space=pl.ANY)] * len(after)),
        out_specs=tuple([_HBM] * (2 * n)),
        input_output_aliases={i: i for i in range(2 * n)},
        compiler_params=pltpu.CompilerParams(has_side_effects=_EFFECT),
    )(*srcs, *lands, *sends, *recvs, *after)
    return list(out[n:])


def _other_chips():
    x, y, _ = _my_pos()
    return [(1 - x, y), (x, 1 - y), (1 - x, 1 - y)]


def _relay_gather_start(srcs, name):
    n = len(srcs)
    me = _dev_index(_my_pos())
    lands = [lax.dynamic_update_index_in_dim(lax.empty((NDEV,) + s.shape, s.dtype), s, me, 0) for s in srcs]

    def body(*refs):
        src_refs, land_refs = refs[:n], refs[n:2 * n]
        sa, ra, sb, rb = refs[2 * n:3 * n], refs[3 * n:4 * n], refs[4 * n:5 * n], refs[5 * n:6 * n]
        token = refs[-1]
        x, y, c = _my_pos()
        mine = _dev_index((x, y, c))
        for a in range(n):
            pltpu.make_async_remote_copy(src_ref=src_refs[a], dst_ref=land_refs[a].at[mine], send_sem=sa[a], recv_sem=ra[a],
                                         device_id=(x, y, 1 - c), device_id_type=MESH).start()
            for chip in _other_chips():
                pltpu.make_async_remote_copy(src_ref=src_refs[a], dst_ref=land_refs[a].at[mine], send_sem=sb[a],
                                             recv_sem=rb[a], device_id=(*chip, c), device_id_type=MESH).start()
        token[...] = jnp.zeros_like(token)

    out = pl.pallas_call(
        body, name=name,
        out_shape=(*[pltpu.SemaphoreType.DMA(())] * (4 * n),
                   *[pltpu.HBM(a.shape, a.dtype) for a in srcs], *[pltpu.HBM(a.shape, a.dtype) for a in lands],
                   _sds((8, 128), F32)),
        in_specs=[_HBM] * (2 * n),
        out_specs=(*[_SEM] * (4 * n), *[_HBM] * (2 * n), pl.BlockSpec(memory_space=pltpu.VMEM)),
        input_output_aliases={i: 4 * n + i for i in range(2 * n)},
        compiler_params=pltpu.CompilerParams(has_side_effects=_EFFECT),
    )(*[pltpu.with_memory_space_constraint(a, pltpu.HBM) for a in srcs],
      *[pltpu.with_memory_space_constraint(a, pltpu.HBM) for a in lands])
    sems = [out[q * n:(q + 1) * n] for q in range(4)]
    return (*sems, out[4 * n:5 * n], out[5 * n:6 * n]), out[-1][0, 0]


def _relay_gather_pass(state, after, name):
    sa, ra, sb, rb, srcs, lands = state
    n = len(srcs)
    after = list(after) if isinstance(after, (list, tuple)) else [after]

    def body(*refs):
        land_refs = refs[n:2 * n]
        sa_r, ra_r, sb_r, rb_r = [refs[(2 + q) * n:(3 + q) * n] for q in range(4)]
        outs = refs[6 * n + len(after):]
        sc, rc = outs[2 * n:3 * n], outs[3 * n:4 * n]
        x, y, c = _my_pos()
        for a in range(n):
            one, three = land_refs[a].at[pl.ds(0, 1)], land_refs[a].at[pl.ds(0, 3)]
            for blocks, s_sem, r_sem in ((one, sa_r[a], ra_r[a]), (three, sb_r[a], rb_r[a])):
                cp = pltpu.make_async_remote_copy(src_ref=blocks, dst_ref=blocks, send_sem=s_sem, recv_sem=r_sem,
                                                  device_id=(x, y, 1 - c), device_id_type=MESH)
                cp.wait_send()
                cp.wait_recv()
            for chip in _other_chips():
                blk = land_refs[a].at[_dev_index((*chip, c))]
                pltpu.make_async_remote_copy(src_ref=blk, dst_ref=blk, send_sem=sc[a], recv_sem=rc[a],
                                             device_id=(x, y, 1 - c), device_id_type=MESH).start()

    out = pl.pallas_call(
        body, name=name,
        out_shape=(*[pltpu.HBM(a.shape, a.dtype) for a in srcs], *[pltpu.HBM(a.shape, a.dtype) for a in lands],
                   *[pltpu.SemaphoreType.DMA(())] * (2 * n)),
        in_specs=(*[_HBM] * (2 * n), *[_SEM] * (4 * n), *[pl.BlockSpec(memory_space=pl.ANY)] * len(after)),
        out_specs=(*[_HBM] * (2 * n), *[_SEM] * (2 * n)),
        input_output_aliases={i: i for i in range(2 * n)},
        compiler_params=pltpu.CompilerParams(has_side_effects=_EFFECT),
    )(*srcs, *lands, *sa, *ra, *sb, *rb, *after)
    return (out[2 * n:3 * n], out[3 * n:4 * n], out[:n], out[n:2 * n])


def _ffn_tiles():
    tm = min(512, T)
    return tm, T // tm


def _ffn_fwd(x, ssg, ng, w_in, w_out):
    n = x.shape[0]
    _, nf, tf, _ = w_in.shape
    tm, tpb = _ffn_tiles()

    def body(x_ref, ssg_ref, ng_ref, win_ref, wout_ref, xn_ref, gu_ref, hid_ref, y_ref, h_scr, acc):
        j = pl.program_id(1)

        @pl.when(j == 0)
        def _():
            s = ssg_ref[0]
            h_scr[...] = _modulate(x_ref[...], ng_ref[...], s[1:2], s[0:1]).astype(BF16)
            acc[...] = jnp.zeros_like(acc)

        h = h_scr[...]
        g = _dot_nt(h, win_ref[0])
        u = _dot_nt(h, win_ref[1])
        gu_ref[0] = g.astype(BF16)
        gu_ref[1] = u.astype(BF16)
        hid = (g * jax.nn.sigmoid(g) * u).astype(BF16)
        hid_ref[...] = hid
        acc[...] += _dot(hid, wout_ref[...])

        @pl.when(j == nf - 1)
        def _():
            yv = acc[...]
            y_ref[...] = yv.astype(BF16)
            xn_ref[...] = x_ref[...] + (0.5 * (1.0 + ssg_ref[0][2:3])) * yv

    return _pcall(
        body, name="ffn_fwd", grid=(n // tm, nf),
        in_specs=[pl.BlockSpec((tm, D), lambda i, j: (i, 0)),
                  pl.BlockSpec((1, 3, D), lambda i, j: (i // tpb, 0, 0)),
                  pl.BlockSpec((1, D), lambda i, j: (0, 0)),
                  pl.BlockSpec((2, None, tf, D), lambda i, j: (0, j, 0, 0)),
                  pl.BlockSpec((None, tf, D), lambda i, j: (j, 0, 0))],
        out_specs=[pl.BlockSpec((tm, D), lambda i, j: (i, 0)),
                   pl.BlockSpec((2, None, tm, tf), lambda i, j: (0, j, i, 0)),
                   pl.BlockSpec((None, tm, tf), lambda i, j: (j, i, 0)),
                   pl.BlockSpec((tm, D), lambda i, j: (i, 0))],
        out_shape=[_sds((n, D), F32), _sds((2, nf, n, tf), BF16), _sds((nf, n, tf), BF16), _sds((n, D), BF16)],
        scratch_shapes=[pltpu.VMEM((tm, D), BF16), pltpu.VMEM((tm, D), F32)],
        compiler_params=_cp(("arbitrary", "arbitrary")),
    )(x, ssg, ng, w_in, w_out)


def _ffn_bwd_a(x, dxn, ssg, ng, y, gu, w_in, w_out):
    n = x.shape[0]
    _, nf, tf, _ = w_in.shape
    tm, tpb = _ffn_tiles()

    def body(x_ref, dxn_ref, ssg_ref, ng_ref, y_ref, gu_ref, win_ref, wout_ref,
             dx_ref, dgu_ref, h_ref, dout_ref, dssg_ref, dng_ref, dout_scr, dh_acc):
        i, j = pl.program_id(0), pl.program_id(1)

        @pl.when(j == 0)
        def _():
            db = ((0.5 * (1.0 + ssg_ref[0][2:3])) * dxn_ref[...]).astype(BF16)
            dout_scr[...] = db
            dout_ref[...] = db
            dh_acc[...] = jnp.zeros_like(dh_acc)

        dhid = _dot_nt(dout_scr[...], wout_ref[...]).astype(BF16)
        g = gu_ref[0]
        u = gu_ref[1]
        sig = jax.nn.sigmoid(g)
        dg = dhid * u * (sig * (1.0 + g * (1.0 - sig)))
        du = dhid * (g * sig)
        dgu_ref[0] = dg
        dgu_ref[1] = du
        dh_acc[...] += _dot(dg, win_ref[0])
        dh_acc[...] += _dot(du, win_ref[1])

        @pl.when(j == nf - 1)
        def _():
            s = ssg_ref[0]
            h, dx_, dng_, dsc_, dsh_ = _modulate_bwd(x_ref[...], ng_ref[...], s[1:2], s[0:1], dh_acc[...])
            h_ref[...] = h.astype(BF16)
            dxn = dxn_ref[...]
            dx_ref[...] = dxn + dx_
            dgate = jnp.sum(0.5 * dxn * y_ref[...].astype(F32), axis=0, keepdims=True)
            _acc_rows(dssg_ref.at[0], i % tpb == 0, [dsh_, dsc_, dgate])
            _acc_rows(dng_ref, i == 0, [dng_])

    return _pcall(
        body, name="ffn_bwd_a", grid=(n // tm, nf),
        in_specs=[pl.BlockSpec((tm, D), lambda i, j: (i, 0)),
                  pl.BlockSpec((tm, D), lambda i, j: (i, 0)),
                  pl.BlockSpec((1, 3, D), lambda i, j: (i // tpb, 0, 0)),
                  pl.BlockSpec((1, D), lambda i, j: (0, 0)),
                  pl.BlockSpec((tm, D), lambda i, j: (i, 0)),
                  pl.BlockSpec((2, None, tm, tf), lambda i, j: (0, j, i, 0)),
                  pl.BlockSpec((2, None, tf, D), lambda i, j: (0, j, 0, 0)),
                  pl.BlockSpec((None, tf, D), lambda i, j: (j, 0, 0))],
        out_specs=[pl.BlockSpec((tm, D), lambda i, j: (i, 0)),
                   pl.BlockSpec((2, None, tm, tf), lambda i, j: (0, j, i, 0)),
                   pl.BlockSpec((tm, D), lambda i, j: (i, 0)),
                   pl.BlockSpec((tm, D), lambda i, j: (i, 0)),
                   pl.BlockSpec((1, 3, D), lambda i, j: (i // tpb, 0, 0)),
                   pl.BlockSpec((1, D), lambda i, j: (0, 0))],
        out_shape=[_sds((n, D), F32), _sds((2, nf, n, tf), BF16), _sds((n, D), BF16), _sds((n, D), BF16),
                   _sds((BL, 3, D), F32), _sds((1, D), F32)],
        scratch_shapes=[pltpu.VMEM((tm, D), BF16), pltpu.VMEM((tm, D), F32)],
        compiler_params=_cp(("arbitrary", "arbitrary")),
    )(x, dxn, ssg, ng, y, gu, w_in, w_out)


def _ffn_bwd_w(h, dgu, hid, dout):
    n = h.shape[0]
    _, nf, _, tf = dgu.shape
    tm, _ = _ffn_tiles()
    ni = n // tm

    def body(h_ref, dgu_ref, hid_ref, dout_ref, dwin_ref, dwout_ref, acc_g, acc_u, acc_o):
        i = pl.program_id(1)

        @pl.when(i == 0)
        def _():
            acc_g[...] = jnp.zeros_like(acc_g)
            acc_u[...] = jnp.zeros_like(acc_u)
            acc_o[...] = jnp.zeros_like(acc_o)

        hv = h_ref[...]
        acc_g[...] += _dot_tn(dgu_ref[0], hv)
        acc_u[...] += _dot_tn(dgu_ref[1], hv)
        acc_o[...] += _dot_tn(hid_ref[...], dout_ref[...])

        @pl.when(i == ni - 1)
        def _():
            dwin_ref[0] = acc_g[...].astype(BF16)
            dwin_ref[1] = acc_u[...].astype(BF16)
            dwout_ref[...] = acc_o[...].astype(BF16)

    return _pcall(
        body, name="ffn_bwd_w", grid=(nf, ni),
        in_specs=[pl.BlockSpec((tm, D), lambda j, i: (i, 0)),
                  pl.BlockSpec((2, None, tm, tf), lambda j, i: (0, j, i, 0)),
                  pl.BlockSpec((None, tm, tf), lambda j, i: (j, i, 0)),
                  pl.BlockSpec((tm, D), lambda j, i: (i, 0))],
        out_specs=[pl.BlockSpec((2, None, tf, D), lambda j, i: (0, j, 0, 0)),
                   pl.BlockSpec((None, tf, D), lambda j, i: (j, 0, 0))],
        out_shape=[_sds((2, nf, tf, D), BF16), _sds((nf, tf, D), BF16)],
        scratch_shapes=[pltpu.VMEM((tf, D), F32), pltpu.VMEM((tf, D), F32), pltpu.VMEM((tf, D), F32)],
        compiler_params=_cp(("arbitrary", "arbitrary")),
    )(h, dgu, hid, dout)


def _premod_matmul(x, ssg, ng, w, bias, tn):
    n = x.shape[0]
    shards = w.ndim == 3
    m = w.shape[0] * w.shape[2] if shards else w.shape[0]
    tm = min(512, T)
    tpb = T // tm
    to = m if shards else tn
    w_spec = (pl.BlockSpec(w.shape, lambda i, j: (0, 0, 0)) if shards
              else pl.BlockSpec((tn, D), lambda i, j: (j, 0)))

    def body(x_ref, ssg_ref, ng_ref, w_ref, b_ref, h_ref, o_ref, h_scr):
        @pl.when(pl.program_id(1) == 0)
        def _():
            s = ssg_ref[0]
            hb = _modulate(x_ref[...], ng_ref[...], s[1:2], s[0:1]).astype(BF16)
            h_scr[...] = hb
            h_ref[...] = hb

        hv = h_scr[...]
        if shards:
            for q in range(w.shape[0]):
                cols = slice(q * tn, (q + 1) * tn)
                o_ref[:, cols] = _dot(hv, w_ref[q]) + b_ref[:, cols]
        else:
            o_ref[...] = _dot_nt(hv, w_ref[...]) + b_ref[...]

    return _pcall(
        body, name="premod_matmul", grid=(n // tm, m // to),
        in_specs=[pl.BlockSpec((tm, D), lambda i, j: (i, 0)),
                  pl.BlockSpec((1, 3, D), lambda i, j: (i // tpb, 0, 0)),
                  pl.BlockSpec((1, D), lambda i, j: (0, 0)),
                  w_spec,
                  pl.BlockSpec((1, to), lambda i, j: (0, j))],
        out_specs=[pl.BlockSpec((tm, D), lambda i, j: (i, 0)),
                   pl.BlockSpec((tm, to), lambda i, j: (i, j))],
        out_shape=[_sds((n, D), BF16), _sds((n, m), F32)],
        scratch_shapes=[pltpu.VMEM((tm, D), BF16)],
        compiler_params=_cp(("arbitrary", "arbitrary")),
    )(x, ssg, ng, w, bias)


def _premod_matmul_bwd(x, dxn, ssg, ng, douts, w):
    n = x.shape[0]
    k = len(douts)
    shards = w.ndim == 3
    tm = min(512, T)
    tpb = T // tm

    def body(*refs):
        x_ref, dxn_ref, ssg_ref, ng_ref = refs[:4]
        do_refs, w_ref = refs[4:4 + k], refs[4 + k]
        dx_ref, dssg_ref, dng_ref = refs[5 + k:]
        i = pl.program_id(0)
        dh = jnp.zeros((tm, D), F32)
        if shards:
            cs = w.shape[2]
            dov = do_refs[0][...]
            for j in range(w.shape[0]):
                dh += _dot_nt(dov[:, j * cs:(j + 1) * cs], w_ref[j])
        else:
            off = 0
            for q in range(k):
                mk = douts[q].shape[1]
                dh += _dot(do_refs[q][...], w_ref[off:off + mk, :])
                off += mk
        s = ssg_ref[0]
        _, dx_, dng_, dsc_, dsh_ = _modulate_bwd(x_ref[...], ng_ref[...], s[1:2], s[0:1], dh)
        dx_ref[...] = dxn_ref[...] + dx_
        _acc_rows(dssg_ref.at[0], i % tpb == 0, [dsh_, dsc_, jnp.zeros_like(dsh_)])
        _acc_rows(dng_ref, i == 0, [dng_])

    return _pcall(
        body, name="premod_matmul_bwd", grid=(n // tm,),
        in_specs=[pl.BlockSpec((tm, D), lambda i: (i, 0)),
                  pl.BlockSpec((tm, D), lambda i: (i, 0)),
                  pl.BlockSpec((1, 3, D), lambda i: (i // tpb, 0, 0)),
                  pl.BlockSpec((1, D), lambda i: (0, 0))]
                 + [pl.BlockSpec((tm, a.shape[1]), lambda i: (i, 0)) for a in douts]
                 + [pl.BlockSpec(w.shape, (lambda i: (0, 0, 0)) if shards else (lambda i: (0, 0)))],
        out_specs=[pl.BlockSpec((tm, D), lambda i: (i, 0)),
                   pl.BlockSpec((1, 3, D), lambda i: (i // tpb, 0, 0)),
                   pl.BlockSpec((1, D), lambda i: (0, 0))],
        out_shape=[_sds((n, D), F32), _sds((BL, 3, D), F32), _sds((1, D), F32)],
        compiler_params=_cp(("arbitrary",)),
    )(x, dxn, ssg, ng, *douts, w)


def _matmul_res(x, a, ssg, w, bias):
    n, kd = a.shape
    tm = min(512, T)
    tpb = T // tm

    def body(x_ref, a_ref, ssg_ref, w_ref, b_ref, xn_ref, y_ref):
        yv = _dot(a_ref[...], w_ref[...]) + b_ref[...]
        y_ref[...] = yv.astype(BF16)
        xn_ref[...] = x_ref[...] + (1.0 + ssg_ref[0][2:3]) * yv

    return _pcall(
        body, name="matmul_res", grid=(n // tm,),
        in_specs=[pl.BlockSpec((tm, D), lambda i: (i, 0)),
                  pl.BlockSpec((tm, kd), lambda i: (i, 0)),
                  pl.BlockSpec((1, 3, D), lambda i: (i // tpb, 0, 0)),
                  pl.BlockSpec((kd, D), lambda i: (0, 0)),
                  pl.BlockSpec((1, D), lambda i: (0, 0))],
        out_specs=[pl.BlockSpec((tm, D), lambda i: (i, 0)), pl.BlockSpec((tm, D), lambda i: (i, 0))],
        out_shape=[_sds((n, D), F32), _sds((n, D), BF16)],
        compiler_params=_cp(("arbitrary",)),
    )(x, a, ssg, w, bias)


def _matmul_res_bwd(dxn, y, ssg, w):
    n = dxn.shape[0]
    kd = w.shape[0]
    tm = min(512, T)
    tpb = T // tm

    def body(dxn_ref, y_ref, ssg_ref, w_ref, da_ref, dy_ref, dgate_ref, dbias_ref):
        i = pl.program_id(0)
        dxn = dxn_ref[...]
        dy = (1.0 + ssg_ref[0][2:3]) * dxn
        dyb = dy.astype(BF16)
        dy_ref[...] = dyb
        da_ref[...] = _dot_nt(dyb, w_ref[...])
        _acc_rows(dgate_ref.at[0], i % tpb == 0, [jnp.sum(dxn * y_ref[...].astype(F32), axis=0, keepdims=True)])
        _acc_rows(dbias_ref, i == 0, [jnp.sum(dy, axis=0, keepdims=True)])

    return _pcall(
        body, name="matmul_res_bwd", grid=(n // tm,),
        in_specs=[pl.BlockSpec((tm, D), lambda i: (i, 0)),
                  pl.BlockSpec((tm, D), lambda i: (i, 0)),
                  pl.BlockSpec((1, 3, D), lambda i: (i // tpb, 0, 0)),
                  pl.BlockSpec((kd, D), lambda i: (0, 0))],
        out_specs=[pl.BlockSpec((tm, kd), lambda i: (i, 0)),
                   pl.BlockSpec((tm, D), lambda i: (i, 0)),
                   pl.BlockSpec((1, 1, D), lambda i: (i // tpb, 0, 0)),
                   pl.BlockSpec((1, D), lambda i: (0, 0))],
        out_shape=[_sds((n, kd), F32), _sds((n, D), BF16), _sds((BL, 1, D), F32), _sds((1, D), F32)],
        compiler_params=_cp(("arbitrary",)),
    )(dxn, y, ssg, w)


def _wgrad_shards(a, b, ns):
    n, kd = a.shape
    cs = b.shape[1] // ns
    tm = min(512, T)
    ni = n // tm

    def body(a_ref, b_ref, o_ref, acc):
        i = pl.program_id(0)

        @pl.when(i == 0)
        def _():
            acc[...] = jnp.zeros_like(acc)

        at = a_ref[...].T
        for q in range(ns):
            acc[q] += _dot(at, b_ref[:, q * cs:(q + 1) * cs])

        @pl.when(i == ni - 1)
        def _():
            o_ref[...] = acc[...].astype(BF16)

    return _pcall(
        body, name="wgrad_shards", grid=(ni,),
        in_specs=[pl.BlockSpec((tm, kd), lambda i: (i, 0)), pl.BlockSpec((tm, ns * cs), lambda i: (i, 0))],
        out_specs=pl.BlockSpec((ns, kd, cs), lambda i: (0, 0, 0)),
        out_shape=_sds((ns, kd, cs), BF16),
        scratch_shapes=[pltpu.VMEM((ns, kd, cs), F32)],
        compiler_params=_cp(("arbitrary",)),
    )(a, b)


def _wgrad(a, b):
    n, kd = a.shape
    m = b.shape[1]
    tm = min(512, T)
    tk = min(1024, kd)
    ni = n // tm

    def body(a_ref, b_ref, o_ref, acc):
        i = pl.program_id(1)

        @pl.when(i == 0)
        def _():
            acc[...] = jnp.zeros_like(acc)

        acc[...] += _dot_tn(a_ref[...], b_ref[...])

        @pl.when(i == ni - 1)
        def _():
            o_ref[...] = acc[...].astype(BF16)

    return _pcall(
        body, name="wgrad", grid=(kd // tk, ni),
        in_specs=[pl.BlockSpec((tm, tk), lambda q, i: (i, q)), pl.BlockSpec((tm, m), lambda q, i: (i, 0))],
        out_specs=pl.BlockSpec((tk, m), lambda q, i: (q, 0)),
        out_shape=_sds((kd, m), BF16),
        scratch_shapes=[pltpu.VMEM((tk, m), F32)],
        compiler_params=_cp(("arbitrary", "arbitrary")),
    )(a, b)


def _ln_silu(u1, g, b):
    mu = jnp.mean(u1, axis=-1, keepdims=True)
    xc = u1 - mu
    var = jnp.mean(xc * xc, axis=-1, keepdims=True)
    ln = xc * lax.rsqrt(var + EPS) * g + b
    return ln * jax.nn.sigmoid(ln)


def _conv_tiles():
    tt = min(256, T)
    return tt, T // tt


def _prev_halo_spec(cols, tt, halo):
    r = tt // halo
    return pl.BlockSpec((halo, cols), lambda b, i: (jnp.maximum(b * (T // halo) + i * r - 1, 0), 0))


def _next_halo_spec(cols, tt, halo):
    r = tt // halo
    last = BL * T // halo - 1
    return pl.BlockSpec((halo, cols), lambda b, i: (jnp.minimum(b * (T // halo) + (i + 1) * r, last), 0))


ROWS = 32
SROWS = 8


def _fill_rotations(rot, win, rows):
    for r in range(8):
        rot[r, 0:rows, :] = win[pl.ds(r, rows), :]


def _window(rot, off, start, size):
    return rot[off % 8, pl.ds(pl.multiple_of(start + (off // 8) * 8, 8), size), :]


def _cm_mid_fwd(ab, w_dw, b_dw, ln_g, ln_b):
    n = ab.shape[0]
    tt, nt = _conv_tiles()

    def body(ab_ref, halo_ref, w_ref, bdw_ref, g_ref, b_ref, u1_ref, u2_ref, win, rot):
        i = pl.program_id(1)
        hv = halo_ref[...]
        u0h = hv[:, :D] * jax.nn.sigmoid(hv[:, D:])
        win[0:HALO, :] = jnp.where(i == 0, 0.0, u0h)
        cv = ab_ref[...]
        win[HALO:HALO + tt, :] = cv[:, :D] * jax.nn.sigmoid(cv[:, D:])
        win[HALO + tt:, :] = jnp.zeros((8, D), F32)
        _fill_rotations(rot, win, tt + HALO)

        def chunk(c, carry):
            r0 = pl.multiple_of(c * ROWS, ROWS)
            acc = jnp.zeros((ROWS, D), F32) + bdw_ref[...]
            for k in range(CW):
                acc += w_ref[k:k + 1, :] * _window(rot, HALO - (CW - 1) + k, r0, ROWS)
            u1_ref[pl.ds(r0, ROWS), :] = acc
            u2_ref[pl.ds(r0, ROWS), :] = _ln_silu(acc, g_ref[...], b_ref[...]).astype(BF16)
            return carry

        lax.fori_loop(0, tt // ROWS, chunk, 0)

    row = lambda b, i: (b * nt + i, 0)
    vec = pl.BlockSpec((1, D), lambda b, i: (0, 0))
    return _pcall(
        body, name="cm_mid_fwd", grid=(BL, nt),
        in_specs=[pl.BlockSpec((tt, 2 * D), row), _prev_halo_spec(2 * D, tt, HALO),
                  pl.BlockSpec((HALO, D), lambda b, i: (0, 0)), vec, vec, vec],
        out_specs=[pl.BlockSpec((tt, D), row), pl.BlockSpec((tt, D), row)],
        out_shape=[_sds((n, D), F32), _sds((n, D), BF16)],
        scratch_shapes=[pltpu.VMEM((HALO + tt + 8, D), F32), pltpu.VMEM((8, tt + HALO, D), F32)],
        compiler_params=_cp(("arbitrary", "arbitrary")),
    )(ab, ab, w_dw, b_dw, ln_g, ln_b)


def _cm_mid_bwd_a(du2, u1, ln_g, ln_b):
    n = du2.shape[0]
    tm = min(256, T)

    def body(du2_ref, u1_ref, g_ref, b_ref, du1_ref, dln_ref):
        _, vjp = jax.vjp(_ln_silu, u1_ref[...], g_ref[...], b_ref[...])
        du1, dg, db = vjp(du2_ref[...])
        du1_ref[...] = du1
        _acc_rows(dln_ref, pl.program_id(0) == 0, [dg, db])

    vec = pl.BlockSpec((1, D), lambda i: (0, 0))
    return _pcall(
        body, name="cm_mid_bwd_a", grid=(n // tm,),
        in_specs=[pl.BlockSpec((tm, D), lambda i: (i, 0)), pl.BlockSpec((tm, D), lambda i: (i, 0)), vec, vec],
        out_specs=[pl.BlockSpec((tm, D), lambda i: (i, 0)), pl.BlockSpec((2, D), lambda i: (0, 0))],
        out_shape=[_sds((n, D), F32), _sds((2, D), F32)],
        compiler_params=_cp(("arbitrary",)),
    )(du2, u1, ln_g, ln_b)


def _cm_mid_bwd_b(du1, ab, w_dw):
    n = du1.shape[0]
    tt, nt = _conv_tiles()

    def body(du1_ref, nxt_ref, ab_ref, halo_ref, w_ref, dab_ref, dw_ref, dbdw_ref, dbglu_ref,
             dwin, uwin, rotd, rotu, accw, accv):
        b, i = pl.program_id(0), pl.program_id(1)
        first = jnp.logical_and(b == 0, i == 0)
        dwin[0:tt, :] = du1_ref[...]
        dwin[tt:tt + HALO, :] = jnp.where(i == nt - 1, 0.0, nxt_ref[...])
        dwin[tt + HALO:, :] = jnp.zeros((8, D), F32)
        hv = halo_ref[...]
        uwin[0:HALO, :] = jnp.where(i == 0, 0.0, hv[:, :D] * jax.nn.sigmoid(hv[:, D:]))
        cv = ab_ref[...]
        uwin[HALO:HALO + tt, :] = cv[:, :D] * jax.nn.sigmoid(cv[:, D:])
        uwin[HALO + tt:, :] = jnp.zeros((8, D), F32)
        _fill_rotations(rotd, dwin, tt + HALO)
        _fill_rotations(rotu, uwin, tt + HALO)
        accw[...] = jnp.zeros_like(accw)
        accv[...] = jnp.zeros_like(accv)

        def fold(v):
            return jnp.sum(v.reshape(ROWS // 8, 8, D), axis=0)

        def chunk(c, carry):
            r0 = pl.multiple_of(c * ROWS, ROWS)
            d1 = du1_ref[pl.ds(r0, ROWS), :]
            du0 = jnp.zeros((ROWS, D), F32)
            for k in range(CW):
                du0 += w_ref[k:k + 1, :] * _window(rotd, CW - 1 - k, r0, ROWS)
                accw[k] += fold(d1 * _window(rotu, HALO - (CW - 1) + k, r0, ROWS))
            cvc = ab_ref[pl.ds(r0, ROWS), :]
            av, sg = cvc[:, :D], jax.nn.sigmoid(cvc[:, D:])
            da = du0 * sg
            db = du0 * av * sg * (1.0 - sg)
            dab_ref[pl.ds(r0, ROWS), 0:D] = da.astype(BF16)
            dab_ref[pl.ds(r0, ROWS), D:2 * D] = db.astype(BF16)
            accv[0] += fold(d1)
            accv[1] += fold(da)
            accv[2] += fold(db)
            return carry

        lax.fori_loop(0, tt // ROWS, chunk, 0)
        dws = [jnp.sum(accw[k], axis=0, keepdims=True) for k in range(CW)]
        dws += [jnp.zeros((1, D), F32)] * (HALO - CW)
        _acc_rows(dw_ref, first, dws)
        _acc_rows(dbdw_ref, first, [jnp.sum(accv[0], axis=0, keepdims=True)])
        _acc_rows(dbglu_ref.at[:, 0:D], first, [jnp.sum(accv[1], axis=0, keepdims=True)])
        _acc_rows(dbglu_ref.at[:, D:2 * D], first, [jnp.sum(accv[2], axis=0, keepdims=True)])

    row = lambda b, i: (b * nt + i, 0)
    return _pcall(
        body, name="cm_mid_bwd_b", grid=(BL, nt),
        in_specs=[pl.BlockSpec((tt, D), row), _next_halo_spec(D, tt, HALO),
                  pl.BlockSpec((tt, 2 * D), row), _prev_halo_spec(2 * D, tt, HALO),
                  pl.BlockSpec((HALO, D), lambda b, i: (0, 0))],
        out_specs=[pl.BlockSpec((tt, 2 * D), row), pl.BlockSpec((HALO, D), lambda b, i: (0, 0)),
                   pl.BlockSpec((1, D), lambda b, i: (0, 0)), pl.BlockSpec((1, 2 * D), lambda b, i: (0, 0))],
        out_shape=[_sds((n, 2 * D), BF16), _sds((HALO, D), F32), _sds((1, D), F32), _sds((1, 2 * D), F32)],
        scratch_shapes=[pltpu.VMEM((tt + HALO + 8, D), F32), pltpu.VMEM((HALO + tt + 8, D), F32),
                        pltpu.VMEM((8, tt + HALO, D), F32), pltpu.VMEM((8, tt + HALO, D), F32),
                        pltpu.VMEM((HALO, 8, D), F32), pltpu.VMEM((3, 8, D), F32)],
        compiler_params=_cp(("arbitrary", "arbitrary")),
    )(du1, du1, ab, ab, w_dw)


def _softplus(v):
    return jnp.maximum(v, 0.0) + jnp.log(1.0 + jnp.exp(-jnp.abs(v)))


def _g_beta(ab, alog, dtb):
    return -jnp.exp(alog) * _softplus(ab + dtb), jax.nn.sigmoid(ab)


def _dn_sconv_fwd(proj, w_sc, alog, dtb):
    n = proj.shape[0]
    tt, nt = _conv_tiles()
    w3 = 3 * D

    def body(qkv_ref, halo_ref, ab_ref, w_ref, alog_ref, dtb_ref, conv_ref, q_ref, k_ref, v_ref, gb_ref, bb_ref,
             win, rot, gsc, bsc):
        i = pl.program_id(1)
        win[0:SHALO, :] = jnp.where(i == 0, 0.0, halo_ref[...])
        win[SHALO:SHALO + tt, :] = qkv_ref[...]
        for k in range(SCW - 1):
            rot[k] = win[pl.ds(SHALO - (SCW - 1) + k, tt), :]
        gsc[...], bsc[...] = _g_beta(ab_ref[...], alog_ref[...], dtb_ref[...])

        def chunk(c, carry):
            rows = pl.ds(pl.multiple_of(c * SROWS, SROWS), SROWS)
            acc = w_ref[SCW - 1:SCW, :] * win[pl.ds(pl.multiple_of(c * SROWS + SHALO, SROWS), SROWS), :]
            for k in range(SCW - 1):
                acc += w_ref[k:k + 1, :] * rot[k, rows, :]
            conv_ref[rows, :] = acc
            act = acc * jax.nn.sigmoid(acc)
            gfull, bfull = gsc[rows, :], bsc[rows, :]
            for h in range(NH):
                q_ref[0, h, rows, :] = act[:, h * DH:(h + 1) * DH]
                k_ref[0, h, rows, :] = act[:, D + h * DH:D + (h + 1) * DH]
                v_ref[0, h, rows, :] = act[:, 2 * D + h * DH:2 * D + (h + 1) * DH]
                gb_ref[0, h, rows, :] = jnp.broadcast_to(gfull[:, h:h + 1], (SROWS, DH))
                bb_ref[0, h, rows, :] = jnp.broadcast_to(bfull[:, NH + h:NH + h + 1], (SROWS, DH))
            return carry

        lax.fori_loop(0, tt // SROWS, chunk, 0)

    row = lambda b, i: (b * nt + i, 0)
    head = pl.BlockSpec((1, NH, tt, DH), lambda b, i: (b, 0, i, 0))
    vec = pl.BlockSpec((1, 128), lambda b, i: (0, 0))
    hs = _sds((BL, NH, T, DH), F32)
    return _pcall(
        body, name="dn_sconv_fwd", grid=(BL, nt),
        in_specs=[pl.BlockSpec((tt, w3), row), _prev_halo_spec(w3, tt, SHALO),
                  pl.BlockSpec((tt, 128), lambda b, i: (b * nt + i, 4 * D // 128)),
                  pl.BlockSpec((SHALO, w3), lambda b, i: (0, 0)), vec, vec],
        out_specs=[pl.BlockSpec((tt, w3), row), head, head, head, head, head],
        out_shape=[_sds((n, w3), F32), hs, hs, hs, hs, hs],
        scratch_shapes=[pltpu.VMEM((SHALO + tt, w3), F32), pltpu.VMEM((SCW - 1, tt, w3), F32),
                        pltpu.VMEM((tt, 128), F32), pltpu.VMEM((tt, 128), F32)],
        compiler_params=_cp(("arbitrary", "arbitrary")),
    )(proj, proj, proj, w_sc, alog, dtb)


_BMM_SPEC = {"nn": "gij,gjk->gik", "nt": "gid,gjd->gij", "tn": "gcd,gce->gde"}


def _mm(kind, a, b, prec):
    if prec is None:
        return jnp.einsum(_BMM_SPEC[kind], a.astype(BF16), b.astype(BF16), preferred_element_type=F32)
    return jnp.einsum(_BMM_SPEC[kind], a, b, preferred_element_type=F32, precision=prec)


@functools.partial(jax.custom_vjp, nondiff_argnums=(0, 3))
def _bmm_k(kind, a, b, prec):
    return _mm(kind, a, b, prec)


def _bmm_k_fwd(kind, a, b, prec):
    return _mm(kind, a, b, prec), (a, b)


def _bmm_k_bwd(kind, prec, res, dc):
    a, b = res
    if kind == "nn":
        return _bmm_k("nt", dc, b, prec), _bmm_k("tn", a, dc, prec)
    if kind == "nt":
        return _bmm_k("nn", dc, b, prec), _bmm_k("tn", dc, a, prec)
    return _bmm_k("nt", b, dc, prec), _bmm_k("nn", a, dc, prec)


_bmm_k.defvjp(_bmm_k_fwd, _bmm_k_bwd)


def _bmm(a, b, prec=None):
    return _bmm_k("nn", a, b, prec)


def _bmm_nt(a, b, prec=None):
    return _bmm_k("nt", a, b, prec)


def _bmm_tn(a, b, prec=None):
    return _bmm_k("tn", a, b, prec)


def _bmm_raw(a, b):
    return _mm("nn", a, b, None)


def _bmm_nt_raw(a, b):
    return _mm("nt", a, b, None)


def _bmm_tn_raw(a, b):
    return _mm("tn", a, b, None)


@jax.custom_vjp
def _unit_lower_inverse(a):
    eye = (lax.broadcasted_iota(jnp.int32, a.shape, 1) == lax.broadcasted_iota(jnp.int32, a.shape, 2)).astype(F32)
    t = eye - a
    p = a
    for _ in range(CHUNK.bit_length() - 2):
        p = _mm("nn", p, p, INV_PREC)
        t = _mm("nn", t, eye + p, INV_PREC)
    return t


def _uli_fwd(a):
    t = _unit_lower_inverse(a)
    return t, t


def _uli_bwd(t, dt):
    return (-_bmm_nt(_bmm_tn(t, dt, lax.Precision.HIGH), t, lax.Precision.HIGH),)


_unit_lower_inverse.defvjp(_uli_fwd, _uli_bwd)


@jax.custom_vjp
def _known_inverse(a, t):
    return t


_known_inverse.defvjp(lambda a, t: (t, t), lambda t, dt: (_uli_bwd(t, dt)[0], jnp.zeros_like(t)))


def _dn_pre(q, k, v, gb, bb, tm_known=None):
    shape = (q.shape[0], CHUNK, CHUNK)
    ri = lax.broadcasted_iota(jnp.int32, shape, 1)
    ci = lax.broadcasted_iota(jnp.int32, shape, 2)
    causal, strict = ri >= ci, ri > ci
    qn = q * lax.rsqrt(jnp.sum(q * q, axis=-1, keepdims=True) + EPS) * (DH ** -0.5)
    kn = k * lax.rsqrt(jnp.sum(k * k, axis=-1, keepdims=True) + EPS)
    gcs = _bmm(causal.astype(F32), gb, HI)
    gcol = gcs[:, :, :CHUNK]
    decay = jnp.exp(jnp.where(causal, gcol - jnp.swapaxes(gcol, 1, 2), -jnp.inf))
    eg = jnp.exp(gcs)
    kb = kn * bb
    a = jnp.where(strict, _bmm_nt(kb, kn) * decay, 0.0)
    tm = _unit_lower_inverse(a) if tm_known is None else _known_inverse(a, tm_known)
    u = _bmm(tm, v * bb)
    w = _bmm(tm, kb * eg)
    qg = qn * eg
    intra = _bmm_nt(qn, kn) * decay
    glast = gcs[:, CHUNK - 1:CHUNK, :]
    kd = kn * jnp.exp(glast - gcs)
    egl = jnp.broadcast_to(jnp.exp(glast), (q.shape[0], 8, DH))
    return u, w, qg, kd, intra, egl, tm


def _pre_tiles():
    gcn = min(16, T // CHUNK)
    return gcn, T // (CHUNK * gcn)


def _dn_pre_specs():
    gcn, _ = _pre_tiles()
    tok = pl.BlockSpec((None, None, gcn * CHUNK, DH), lambda b, h, i: (b, h, i, 0))
    sq = pl.BlockSpec((None, None, gcn * CHUNK, CHUNK), lambda b, h, i: (b, h, i, 0))
    per = pl.BlockSpec((None, None, gcn * 8, DH), lambda b, h, i: (b, h, i, 0))
    return tok, sq, per


def _dn_pre_fwd(q, k, v, gb, bb):
    gcn, ng = _pre_tiles()
    tok, sq, per = _dn_pre_specs()

    def body(q_ref, k_ref, v_ref, gb_ref, bb_ref, u_ref, w_ref, qg_ref, kd_ref, in_ref, egl_ref, tinv_ref):
        args = [r[...].reshape(gcn, CHUNK, DH) for r in (q_ref, k_ref, v_ref, gb_ref, bb_ref)]
        u, w, qg, kd, intra, egl, tinv = _dn_pre(*args)
        for r, val in ((u_ref, u), (w_ref, w), (qg_ref, qg), (kd_ref, kd)):
            r[...] = val.reshape(gcn * CHUNK, DH)
        in_ref[...] = intra.reshape(gcn * CHUNK, CHUNK)
        tinv_ref[...] = tinv.reshape(gcn * CHUNK, CHUNK)
        egl_ref[...] = egl.reshape(gcn * 8, DH)

    hs = _sds((BL, NH, T, DH), F32)
    sqs = _sds((BL, NH, T, CHUNK), F32)
    return _pcall(
        body, name="dn_pre_fwd", grid=(BL, NH, ng),
        in_specs=[tok] * 5, out_specs=[tok, tok, tok, tok, sq, per, sq],
        out_shape=[hs, hs, hs, hs, sqs, _sds((BL, NH, T // CHUNK * 8, DH), F32), sqs],
        compiler_params=_cp(("arbitrary",) * 3),
    )(q, k, v, gb, bb)


def _dn_pre_bwd(q, k, v, gb, bb, tinv, du, dw, dqg, dkd, dintra, degl):
    gcn, ng = _pre_tiles()
    tok, sq, per = _dn_pre_specs()

    def body(q_ref, k_ref, v_ref, gb_ref, bb_ref, tinv_ref, du_ref, dw_ref, dqg_ref, dkd_ref, din_ref, degl_ref,
             dq_ref, dk_ref, dv_ref, dgb_ref, dbb_ref):
        args = [r[...].reshape(gcn, CHUNK, DH) for r in (q_ref, k_ref, v_ref, gb_ref, bb_ref)]
        known = tinv_ref[...].reshape(gcn, CHUNK, CHUNK)
        _, vjp = jax.vjp(lambda *a: _dn_pre(*a, tm_known=known)[:6], *args)
        cts = [r[...].reshape(gcn, CHUNK, DH) for r in (du_ref, dw_ref, dqg_ref, dkd_ref)]
        de = degl_ref[...].reshape(gcn, 8, DH)
        one = jnp.logical_and(lax.broadcasted_iota(jnp.int32, de.shape, 1) == 0,
                              lax.broadcasted_iota(jnp.int32, de.shape, 2) == 0)
        outs = vjp((*cts, din_ref[...].reshape(gcn, CHUNK, CHUNK), jnp.where(one, de, 0.0)))
        for r, val in zip((dq_ref, dk_ref, dv_ref, dgb_ref, dbb_ref), outs):
            r[...] = val.reshape(gcn * CHUNK, DH)

    hs = _sds((BL, NH, T, DH), F32)
    return _pcall(
        body, name="dn_pre_bwd", grid=(BL, NH, ng),
        in_specs=[tok] * 5 + [sq] + [tok] * 4 + [sq, per], out_specs=[tok] * 5, out_shape=[hs] * 5,
        compiler_params=_cp(("arbitrary",) * 3),
    )(q, k, v, gb, bb, tinv, du, dw, dqg, dkd, dintra, degl)


def _scan_tiles():
    cs = min(2, T // CHUNK)
    return cs, T // (CHUNK * cs)


def _dn_scan_fwd(u, w, qg, kd, intra, egl):
    cs, ns = _scan_tiles()
    g = BL * NH
    nc = T // CHUNK

    def body(u_ref, w_ref, qg_ref, kd_ref, in_ref, egl_ref, o_ref, vn_ref, s0_ref, s_scr):
        @pl.when(pl.program_id(0) == 0)
        def _():
            s_scr[...] = jnp.zeros_like(s_scr)

        for c in range(cs):
            rows = pl.ds(c * CHUNK, CHUNK)
            s = s_scr[...]
            s0_ref[:, :, c] = s.reshape(BL, NH, DH, DH)

            def ld(r, m=DH):
                return r[:, :, rows, :].reshape(g, CHUNK, m)

            vn = ld(u_ref) - _bmm_raw(ld(w_ref), s)
            o = _bmm_raw(ld(qg_ref), s) + _bmm_raw(ld(in_ref, CHUNK), vn)
            e = egl_ref[:, :, pl.ds(c * 8, 1), :].reshape(g, 1, DH)
            s_scr[...] = s * e + _bmm_tn_raw(ld(kd_ref), vn)
            vn_ref[:, :, rows, :] = vn.reshape(BL, NH, CHUNK, DH)
            o_ref[:, :, rows, :] = o.reshape(BL, NH, CHUNK, DH)

    tok = pl.BlockSpec((BL, NH, cs * CHUNK, DH), lambda i: (0, 0, i, 0))
    hs = _sds((BL, NH, T, DH), F32)
    return _pcall(
        body, name="dn_scan_fwd", grid=(ns,),
        in_specs=[tok, tok, tok, tok, pl.BlockSpec((BL, NH, cs * CHUNK, CHUNK), lambda i: (0, 0, i, 0)),
                  pl.BlockSpec((BL, NH, cs * 8, DH), lambda i: (0, 0, i, 0))],
        out_specs=[tok, tok, pl.BlockSpec((BL, NH, cs, DH, DH), lambda i: (0, 0, i, 0, 0))],
        out_shape=[hs, hs, _sds((BL, NH, nc, DH, DH), F32)],
        scratch_shapes=[pltpu.VMEM((g, DH, DH), F32)],
        compiler_params=_cp(("arbitrary",)),
    )(u, w, qg, kd, intra, egl)


def _dn_scan_bwd(do, w, qg, kd, intra, egl, vn, s0):
    cs, ns = _scan_tiles()
    g = BL * NH
    nc = T // CHUNK

    def body(do_ref, w_ref, qg_ref, kd_ref, in_ref, egl_ref, vn_ref, s0_ref,
             du_ref, dw_ref, dqg_ref, dkd_ref, din_ref, degl_ref, ds_scr):
        @pl.when(pl.program_id(0) == 0)
        def _():
            ds_scr[...] = jnp.zeros_like(ds_scr)

        for c in reversed(range(cs)):
            rows = pl.ds(c * CHUNK, CHUNK)

            def ld(r, m=DH):
                return r[:, :, rows, :].reshape(g, CHUNK, m)

            def st(r, val, m=DH):
                r[:, :, rows, :] = val.reshape(BL, NH, CHUNK, m)

            s = s0_ref[:, :, c].reshape(g, DH, DH)
            ds = ds_scr[...]
            dov, vnv, kdv, wv, qgv, inv = ld(do_ref), ld(vn_ref), ld(kd_ref), ld(w_ref), ld(qg_ref), ld(in_ref, CHUNK)
            dv = _bmm_tn_raw(inv, dov) + _bmm_raw(kdv, ds)
            st(din_ref, _bmm_nt_raw(dov, vnv), CHUNK)
            st(dqg_ref, _bmm_nt_raw(dov, s))
            st(dkd_ref, _bmm_nt_raw(vnv, ds))
            st(du_ref, dv)
            st(dw_ref, -_bmm_nt_raw(dv, s))
            de = jnp.sum(jnp.sum(ds * s, axis=2, keepdims=True), axis=1, keepdims=True)
            degl_ref[:, :, pl.ds(c * 8, 8), :] = jnp.broadcast_to(de, (g, 8, DH)).reshape(BL, NH, 8, DH)
            e = egl_ref[:, :, pl.ds(c * 8, 1), :].reshape(g, 1, DH)
            ds_scr[...] = ds * e + _bmm_tn_raw(qgv, dov) - _bmm_tn_raw(wv, dv)

    rev = lambda i: (0, 0, ns - 1 - i, 0)
    tok = pl.BlockSpec((BL, NH, cs * CHUNK, DH), rev)
    sq = pl.BlockSpec((BL, NH, cs * CHUNK, CHUNK), rev)
    per = pl.BlockSpec((BL, NH, cs * 8, DH), rev)
    hs = _sds((BL, NH, T, DH), F32)
    return _pcall(
        body, name="dn_scan_bwd", grid=(ns,),
        in_specs=[tok, tok, tok, tok, sq, per, tok,
                  pl.BlockSpec((BL, NH, cs, DH, DH), lambda i: (0, 0, ns - 1 - i, 0, 0))],
        out_specs=[tok, tok, tok, tok, sq, per],
        out_shape=[hs, hs, hs, hs, _sds((BL, NH, T, CHUNK), F32), _sds((BL, NH, nc * 8, DH), F32)],
        scratch_shapes=[pltpu.VMEM((g, DH, DH), F32)],
        compiler_params=_cp(("arbitrary",)),
    )(do, w, qg, kd, intra, egl, vn, s0)


def _gated_norm(o_h, z_h, og):
    r = lax.rsqrt(jnp.mean(o_h * o_h, axis=-1, keepdims=True) + EPS)
    return (o_h * r * og) * (z_h * jax.nn.sigmoid(z_h))


def _dn_gnorm_fwd(o, proj, o_g):
    tm = min(256, T)
    nt = T // tm

    def body(o_ref, z_ref, g_ref, og_ref):
        z = z_ref[...]
        for h in range(NH):
            og_ref[:, h * DH:(h + 1) * DH] = _gated_norm(o_ref[0, h], z[:, h * DH:(h + 1) * DH], g_ref[...]).astype(BF16)

    return _pcall(
        body, name="dn_gnorm_fwd", grid=(BL, nt),
        in_specs=[pl.BlockSpec((1, NH, tm, DH), lambda b, i: (b, 0, i, 0)),
                  pl.BlockSpec((tm, D), lambda b, i: (b * nt + i, 3)),
                  pl.BlockSpec((1, DH), lambda b, i: (0, 0))],
        out_specs=pl.BlockSpec((tm, D), lambda b, i: (b * nt + i, 0)),
        out_shape=_sds((BL * T, D), BF16),
        compiler_params=_cp(("arbitrary", "arbitrary")),
    )(o, proj, o_g)


def _dn_gnorm_bwd(dog, o, proj, o_g):
    tm = min(256, T)
    nt = T // tm

    def body(dog_ref, o_ref, z_ref, g_ref, do_ref, dz_ref, dg_ref):
        z = z_ref[...]
        dog = dog_ref[...]
        dg = jnp.zeros((1, DH), F32)
        for h in range(NH):
            cols = slice(h * DH, (h + 1) * DH)
            _, vjp = jax.vjp(_gated_norm, o_ref[0, h], z[:, cols], g_ref[...])
            do_h, dz_h, dg_h = vjp(dog[:, cols])
            do_ref[0, h] = do_h
            dz_ref[:, cols] = dz_h.astype(BF16)
            dg += dg_h
        _acc_rows(dg_ref, jnp.logical_and(pl.program_id(0) == 0, pl.program_id(1) == 0), [dg])

    return _pcall(
        body, name="dn_gnorm_bwd", grid=(BL, nt),
        in_specs=[pl.BlockSpec((tm, D), lambda b, i: (b * nt + i, 0)),
                  pl.BlockSpec((1, NH, tm, DH), lambda b, i: (b, 0, i, 0)),
                  pl.BlockSpec((tm, D), lambda b, i: (b * nt + i, 3)),
                  pl.BlockSpec((1, DH), lambda b, i: (0, 0))],
        out_specs=[pl.BlockSpec((1, NH, tm, DH), lambda b, i: (b, 0, i, 0)),
                   pl.BlockSpec((tm, D), lambda b, i: (b * nt + i, 0)),
                   pl.BlockSpec((1, DH), lambda b, i: (0, 0))],
        out_shape=[_sds((BL, NH, T, DH), F32), _sds((BL * T, D), BF16), _sds((1, DH), F32)],
        compiler_params=_cp(("arbitrary", "arbitrary")),
    )(dog, o, proj, o_g)


def _dn_prep_bwd(dq, dk, dv, dgb, dbb, conv, proj, alog, dtb):
    n = conv.shape[0]
    tt, nt = _conv_tiles()
    w3 = 3 * D

    def body(dq_ref, dk_ref, dv_ref, dgb_ref, dbb_ref, conv_ref, ab_ref, alog_ref, dtb_ref, dconv_ref, dab_ref, dhead_ref):
        cv = conv_ref[...]
        sg = jax.nn.sigmoid(cv)
        dact = sg * (1.0 + cv * (1.0 - sg))
        lane = lax.broadcasted_iota(jnp.int32, (tt, 128), 1)
        cg = jnp.zeros((tt, 128), F32)
        cb = jnp.zeros((tt, 128), F32)
        for h in range(NH):
            cols = slice(h * DH, (h + 1) * DH)
            dconv_ref[:, h * DH:(h + 1) * DH] = dq_ref[0, h] * dact[:, cols]
            dconv_ref[:, D + h * DH:D + (h + 1) * DH] = dk_ref[0, h] * dact[:, D + h * DH:D + (h + 1) * DH]
            dconv_ref[:, 2 * D + h * DH:2 * D + (h + 1) * DH] = dv_ref[0, h] * dact[:, 2 * D + h * DH:2 * D + (h + 1) * DH]
            cg = jnp.where(lane == h, jnp.sum(dgb_ref[0, h], axis=-1, keepdims=True), cg)
            cb = jnp.where(lane == NH + h, jnp.sum(dbb_ref[0, h], axis=-1, keepdims=True), cb)
        _, vjp = jax.vjp(_g_beta, ab_ref[...], alog_ref[...], dtb_ref[...])
        dab, dalog, ddtb = vjp((cg, cb))
        dab_ref[...] = dab.astype(BF16)
        _acc_rows(dhead_ref, jnp.logical_and(pl.program_id(0) == 0, pl.program_id(1) == 0), [dalog, ddtb])

    row = lambda b, i: (b * nt + i, 0)
    head = pl.BlockSpec((1, NH, tt, DH), lambda b, i: (b, 0, i, 0))
    vec = pl.BlockSpec((1, 128), lambda b, i: (0, 0))
    return _pcall(
        body, name="dn_prep_bwd", grid=(BL, nt),
        in_specs=[head] * 5 + [pl.BlockSpec((tt, w3), row),
                               pl.BlockSpec((tt, 128), lambda b, i: (b * nt + i, 4 * D // 128)), vec, vec],
        out_specs=[pl.BlockSpec((tt, w3), row), pl.BlockSpec((tt, 128), row), pl.BlockSpec((2, 128), lambda b, i: (0, 0))],
        out_shape=[_sds((n, w3), F32), _sds((n, 128), BF16), _sds((2, 128), F32)],
        compiler_params=_cp(("arbitrary", "arbitrary")),
    )(dq, dk, dv, dgb, dbb, conv, proj, alog, dtb)


def _dn_sconv_bwd(dconv, proj, w_sc):
    n = dconv.shape[0]
    tt, nt = _conv_tiles()
    w3 = 3 * D

    def body(dc_ref, nxt_ref, qkv_ref, halo_ref, w_ref, dpre_ref, dw_ref, dwin, pwin, rotd, rotp, dsc, accw):
        b, i = pl.program_id(0), pl.program_id(1)
        dwin[0:tt, :] = dc_ref[...]
        dwin[tt:tt + SHALO, :] = jnp.where(i == nt - 1, 0.0, nxt_ref[...])
        pwin[0:SHALO, :] = jnp.where(i == 0, 0.0, halo_ref[...])
        pwin[SHALO:SHALO + tt, :] = qkv_ref[...]
        for k in range(SCW - 1):
            rotd[k] = dwin[pl.ds(k + 1, tt), :]
            rotp[k] = pwin[pl.ds(SHALO - (SCW - 1) + k, tt), :]
        accw[...] = jnp.zeros_like(accw)

        def chunk(c, carry):
            r0 = pl.multiple_of(c * SROWS, SROWS)
            rows = pl.ds(r0, SROWS)
            dc = dc_ref[rows, :]
            dpre = w_ref[SCW - 1:SCW, :] * dc
            accw[SCW - 1] += dc * pwin[pl.ds(pl.multiple_of(r0 + SHALO, SROWS), SROWS), :]
            for k in range(SCW - 1):
                dpre += w_ref[k:k + 1, :] * rotd[SCW - 2 - k, rows, :]
                accw[k] += dc * rotp[k, rows, :]
            dsc[rows, :] = dpre
            return carry

        lax.fori_loop(0, tt // SROWS, chunk, 0)
        dpre_ref[...] = dsc[...].astype(BF16)
        dws = [jnp.sum(accw[k], axis=0, keepdims=True) for k in range(SCW)]
        dws += [jnp.zeros((1, w3), F32)] * (SHALO - SCW)
        _acc_rows(dw_ref, jnp.logical_and(b == 0, i == 0), dws)

    row = lambda b, i: (b * nt + i, 0)
    return _pcall(
        body, name="dn_sconv_bwd", grid=(BL, nt),
        in_specs=[pl.BlockSpec((tt, w3), row), _next_halo_spec(w3, tt, SHALO),
                  pl.BlockSpec((tt, w3), row), _prev_halo_spec(w3, tt, SHALO),
                  pl.BlockSpec((SHALO, w3), lambda b, i: (0, 0))],
        out_specs=[pl.BlockSpec((tt, w3), row), pl.BlockSpec((SHALO, w3), lambda b, i: (0, 0))],
        out_shape=[_sds((n, w3), BF16), _sds((SHALO, w3), F32)],
        scratch_shapes=[pltpu.VMEM((tt + SHALO, w3), F32), pltpu.VMEM((SHALO + tt, w3), F32),
                        pltpu.VMEM((SCW - 1, tt, w3), F32), pltpu.VMEM((SCW - 1, tt, w3), F32),
                        pltpu.VMEM((tt, w3), F32), pltpu.VMEM((SCW, SROWS, w3), F32)],
        compiler_params=_cp(("arbitrary", "arbitrary")),
    )(dconv, dconv, proj, proj, w_sc)


def _ada_fwd(c_all, w_ada, b_cols):
    nl, _, m = w_ada.shape
    nb = c_all.shape[0]

    def body(c_ref, w_ref, b_ref, o_ref):
        cv = c_ref[...]
        cs = (cv * jax.nn.sigmoid(cv)).astype(BF16)
        o_ref[...] = _dot(cs, w_ref[...].astype(BF16)) + b_ref[...]

    return _pcall(
        body, name="ada_fwd", grid=(nl,),
        in_specs=[pl.BlockSpec((nb, D), lambda l: (0, 0)), pl.BlockSpec((None, D, m), lambda l: (l, 0, 0)),
                  pl.BlockSpec((None, 1, m), lambda l: (l, 0, 0))],
        out_specs=pl.BlockSpec((None, nb, m), lambda l: (l, 0, 0)),
        out_shape=_sds((nl, nb, m), F32),
        compiler_params=_cp(("arbitrary",)),
    )(c_all, w_ada, b_cols)


def _ada_bwd(c_all, dmod_cols):
    nl, nb, m = dmod_cols.shape

    def body(c_ref, d_ref, o_ref):
        cv = c_ref[...]
        cs = (cv * jax.nn.sigmoid(cv)).astype(BF16)
        o_ref[0] = _dot_tn(cs, d_ref[...].astype(BF16))

    return _pcall(
        body, name="ada_bwd", grid=(nl,),
        in_specs=[pl.BlockSpec((nb, D), lambda l: (0, 0)), pl.BlockSpec((None, nb, m), lambda l: (l, 0, 0))],
        out_specs=pl.BlockSpec((1, D, m), lambda l: (0, l, 0)),
        out_shape=_sds((1, nl * D, m), F32),
        compiler_params=_cp(("arbitrary",)),
    )(c_all, dmod_cols)


def _loss_head(x, tgt, fg):
    n = x.shape[0]
    tm = min(512, T)

    def f(xv, g, t):
        r = lax.rsqrt(jnp.mean(xv * xv, axis=-1, keepdims=True) + EPS)
        e = xv * r * g - t
        return 0.5 * jnp.sum(e * e, axis=0, keepdims=True) * (1.0 / D)

    def body(x_ref, t_ref, g_ref, dx_ref, st_ref):
        t = t_ref[...]
        lrow, vjp = jax.vjp(lambda xv, g: f(xv, g, t), x_ref[...], g_ref[...])
        dx, dg = vjp(jnp.ones_like(lrow))
        dx_ref[...] = dx
        _acc_rows(st_ref, pl.program_id(0) == 0, [dg, lrow])

    return _pcall(
        body, name="loss_head", grid=(n // tm,),
        in_specs=[pl.BlockSpec((tm, D), lambda i: (i, 0)), pl.BlockSpec((tm, D), lambda i: (i, 0)),
                  pl.BlockSpec((1, D), lambda i: (0, 0))],
        out_specs=[pl.BlockSpec((tm, D), lambda i: (i, 0)), pl.BlockSpec((2, D), lambda i: (0, 0))],
        out_shape=[_sds((n, D), F32), _sds((2, D), F32)],
        compiler_params=_cp(("arbitrary",)),
    )(x, tgt, fg)


def _adamw(parts, w, m, v):
    p, r, c = parts.shape
    tr = r
    for cand in (256, 128, 64, 32, 16, 8):
        if r % cand == 0:
            tr = cand
            break
    k1 = 1.0 - B1 ** STEP
    k2 = 1.0 - B2 ** STEP

    def body(p_ref, w_ref, m_ref, v_ref, g_ref, d_ref, nm_ref, nv_ref):
        g = p_ref[0].astype(F32)
        for q in range(1, p):
            g += p_ref[q].astype(F32)
        mn = B1 * m_ref[...] + (1.0 - B1) * g
        vn = B2 * v_ref[...] + (1.0 - B2) * (g * g)
        g_ref[...] = g
        nm_ref[...] = mn
        nv_ref[...] = vn
        d_ref[...] = -LR * ((mn / k1) / (jnp.sqrt(vn / k2) + AEPS) + WD * w_ref[...])

    blk = pl.BlockSpec((tr, c), lambda i: (i, 0))
    return _pcall(
        body, name="adamw", grid=(r // tr,),
        in_specs=[pl.BlockSpec((p, tr, c), lambda i: (0, i, 0)), blk, blk, blk],
        out_specs=[blk] * 4, out_shape=[_sds((r, c), F32)] * 4,
        compiler_params=_cp(("arbitrary",)),
    )(parts, w, m, v)


def _sum_parts(parts):
    p, r, c = parts.shape

    def body(p_ref, o_ref):
        acc = p_ref[0]
        for q in range(1, p):
            acc += p_ref[q]
        o_ref[...] = acc

    return _pcall(body, name="sum_parts", out_shape=_sds((r, c), F32))(parts)


def _adamw_slot(parts, w, m, v, outs, row0, col):
    p, r, c = parts.shape
    tr = r
    for cand in (256, 128, 64, 32, 16, 8):
        if r % cand == 0:
            tr = cand
            break
    if r % 352 == 0:
        tr = 352
    nt = r // tr
    k1 = 1.0 - B1 ** STEP
    k2 = 1.0 - B2 ** STEP

    def body(p_ref, w_ref, m_ref, v_ref, g0, d0, m0, v0, g_ref, d_ref, nm_ref, nv_ref):
        g = p_ref[0].astype(F32)
        for q in range(1, p):
            g += p_ref[q].astype(F32)
        mn = B1 * m_ref[...] + (1.0 - B1) * g
        vn = B2 * v_ref[...] + (1.0 - B2) * (g * g)
        g_ref[...] = g
        nm_ref[...] = mn
        nv_ref[...] = vn
        d_ref[...] = -LR * ((mn / k1) / (jnp.sqrt(vn / k2) + AEPS) + WD * w_ref[...])

    blk = pl.BlockSpec((tr, c), lambda i: (row0 * nt + i, col))
    anyspec = pl.BlockSpec(memory_space=pl.ANY)
    return _pcall(
        body, name="adamw_slot", grid=(nt,),
        in_specs=[pl.BlockSpec((p, tr, c), lambda i: (0, i, 0)), blk, blk, blk] + [anyspec] * 4,
        out_specs=[blk] * 4, out_shape=[_sds(w.shape, F32)] * 4,
        input_output_aliases={4: 0, 5: 1, 6: 2, 7: 3},
        compiler_params=_cp(("arbitrary",)),
    )(parts, w, m, v, *outs)


def _pack(arrs):
    flat = jnp.concatenate([a.reshape(-1) for a in arrs])
    pad = (-flat.shape[0]) % 1024
    return jnp.pad(flat, (0, pad)).reshape(-1, 128)


def _unpack(buf, shapes):
    flat = buf.reshape(-1)
    out, off = [], 0
    for s in shapes:
        size = 1
        for d in s:
            size *= d
        out.append(flat[off:off + size].reshape(s))
        off += size
    return out


def kernel(x, c, norm_g, w_ada, b_ada, w_ffn_in, w_ffn_out, cm_w_glu, cm_b_glu, cm_w_dw, cm_b_dw, cm_ln_g, cm_ln_b, cm_w_pw, cm_b_pw, dn_w_in, dn_w_sconv, dn_a_log, dn_dt_bias, dn_o_g, dn_w_out, final_g, loss_target, m_norm_g, m_w_ada, m_b_ada, m_w_ffn_in, m_w_ffn_out, m_cm_w_glu, m_cm_b_glu, m_cm_w_dw, m_cm_b_dw, m_cm_ln_g, m_cm_ln_b, m_cm_w_pw, m_cm_b_pw, m_dn_w_in, m_dn_w_sconv, m_dn_a_log, m_dn_dt_bias, m_dn_o_g, m_dn_w_out, m_final_g, v_norm_g, v_w_ada, v_b_ada, v_w_ffn_in, v_w_ffn_out, v_cm_w_glu, v_cm_b_glu, v_cm_w_dw, v_cm_b_dw, v_cm_ln_g, v_cm_ln_b, v_cm_w_pw, v_cm_b_pw, v_dn_w_in, v_dn_w_sconv, v_dn_a_log, v_dn_dt_bias, v_dn_o_g, v_dn_w_out, v_final_g):
    me = 4 * lax.axis_index("x") + 2 * lax.axis_index("y") + lax.axis_index("c")
    n = BL * T
    nf = 4
    tf = FF // nf
    na, nb = cm_w_glu.shape[0], dn_w_in.shape[0]
    mcols = w_ada.shape[2]
    dsh = D // NDEV

    tr_ffn = lambda a: jnp.swapaxes(a, 2, 3)
    tr_dn = lambda a: jnp.transpose(a, (2, 0, 1))
    wt_ffn_in, wt_dn_in = tr_ffn(w_ffn_in), tr_dn(dn_w_in)

    def unit_weights(l, part):
        if part == 0:
            ws = (wt_ffn_in[l, 0], w_ffn_out[l, 0])
        else:
            mix = (cm_w_glu[l // 2], cm_w_pw[l // 2]) if l % 2 == 0 else (wt_dn_in[:, l // 2], dn_w_out[l // 2])
            ws = (wt_ffn_in[l, 1], w_ffn_out[l, 1], *mix)
        return [w.astype(BF16) for w in ws]

    gathers, all_started = {}, jnp.zeros((8, 128), F32)
    for l in range(DEPTH):
        for part in range(2):
            if l < RELAY_LAYERS:
                gathers[l, part], tok = _relay_gather_start(unit_weights(l, part), f"gather_start_{l}_{part}")
            else:
                gathers[l, part], tok = _exchange_start(unit_weights(l, part), True, f"gather_start_{l}_{part}")
            all_started = all_started + tok

    c_g, ng_g, dw_g, sc_g = _all_gather([c, norm_g, cm_w_dw, dn_w_sconv], "gather_small")
    whole = lambda g: jnp.moveaxis(g, 0, -2).reshape(*g.shape[1:-1], -1)
    c_all = c_g.reshape(NDEV * BL, D)
    norm_g_f, w_dw_f, w_sc_f = whole(ng_g), whole(dw_g), whole(sc_g)

    b_cols = lax.dynamic_slice_in_dim(b_ada, me * mcols, mcols, axis=1)[:, None, :]
    mod_cols = _ada_fwd(c_all, w_ada, b_cols)
    mod_g, = _all_gather([mod_cols], "gather_mod")
    mod_all = jnp.transpose(mod_g, (1, 2, 0, 3)).reshape(DEPTH, NDEV * BL, 9 * D)
    mod = lax.dynamic_slice_in_dim(mod_all, me * BL, BL, axis=1).reshape(DEPTH, BL, 3, 3, D)

    gathered = [None] * DEPTH

    def gather_wait(l, part, after):
        if l < RELAY_LAYERS:
            passed = _relay_gather_pass(gathers[l, part], after, f"gather_pass_{l}_{part}")
            return _exchange_wait(passed, after, f"gather_wait_{l}_{part}", blocks=3)
        return _exchange_wait(gathers[l, part], after, f"gather_wait_{l}_{part}")

    def ffn_weights(l, s):
        return gathered[l][s].reshape(2, nf, tf, D), gathered[l][2 + s].reshape(nf, tf, D)

    xs = x.reshape(n, D)
    saved = []
    for l in range(DEPTH):
        rec = {}
        ga = gather_wait(l, 0, all_started if l == 0 else xs)
        gathered[l] = [ga[0], None, ga[1], None, None, None]
        for s, j in ((0, 0), (1, 2)):
            if j == 2:
                gb = gather_wait(l, 1, xs)
                gathered[l] = [ga[0], gb[0], ga[1], gb[1], gb[2], gb[3]]
            w_in, w_out = ffn_weights(l, s)
            ssg, ng = mod[l, :, j], norm_g_f[l, j][None]
            if j == 2:
                ssg1, ng1 = mod[l, :, 1], norm_g_f[l, 1][None]
                if l % 2 == 0:
                    a = l // 2
                    w_glu = gathered[l][4]
                    w_pw = gathered[l][5].reshape(D, D)
                    w_dw = jnp.pad(w_dw_f[a], ((0, HALO - CW), (0, 0)))
                    h1, ab = _premod_matmul(xs, ssg1, ng1, w_glu, cm_b_glu[a][None], w_glu.shape[2])
                    u1, u2 = _cm_mid_fwd(ab, w_dw, cm_b_dw[a][None], cm_ln_g[a][None], cm_ln_b[a][None])
                    xn, ymix = _matmul_res(xs, u2, ssg1, w_pw, cm_b_pw[a][None])
                    rec["mix"] = dict(x=xs, h=h1, ab=ab, u1=u1, u2=u2, y=ymix, w_glu=w_glu, w_pw=w_pw, w_dw=w_dw)
                else:
                    mi = l // 2
                    w_proj = jnp.pad(gathered[l][4].reshape(4 * D + 2 * NH, D), ((0, 128 - 2 * NH), (0, 0)))
                    w_o = gathered[l][5].reshape(D, D)
                    w_sc = jnp.pad(w_sc_f[mi], ((0, SHALO - SCW), (0, 0)))
                    alog = jnp.pad(dn_a_log[mi], (0, 128 - NH))[None]
                    dtb = jnp.pad(dn_dt_bias[mi], (0, 128 - NH))[None]
                    h1, proj = _premod_matmul(xs, ssg1, ng1, w_proj, jnp.zeros((1, w_proj.shape[0]), F32),
                                              (4 * D + 128) // 3 if (4 * D + 128) % 384 == 0 else 128)
                    conv, q, k, v, gb, bb = _dn_sconv_fwd(proj, w_sc, alog, dtb)
                    u, w, qg, kd, intra, egl, tinv = _dn_pre_fwd(q, k, v, gb, bb)
                    o, vn, s0 = _dn_scan_fwd(u, w, qg, kd, intra, egl)
                    og = _dn_gnorm_fwd(o, proj, dn_o_g[mi][None])
                    xn, ymix = _matmul_res(xs, og, ssg1, w_o, jnp.zeros((1, D), F32))
                    rec["mix"] = dict(x=xs, h=h1, proj=proj, conv=conv, q=q, k=k, v=v, gb=gb, bb=bb, w=w, qg=qg, kd=kd,
                                      intra=intra, egl=egl, tinv=tinv, o=o, vn=vn, s0=s0, og=og, y=ymix, w_proj=w_proj, w_o=w_o,
                                      w_sc=w_sc, alog=alog, dtb=dtb)
                xs = xn
            xn, gu, hid, y = _ffn_fwd(xs, ssg, ng, w_in, w_out)
            rec[s] = dict(x=xs, gu=gu, hid=hid, y=y)
            xs = xn
        saved.append(rec)

    dx, stats = _loss_head(xs, loss_target.reshape(n, D), final_g[None])
    loss = lax.psum(jnp.sum(stats[1]), AXES)
    d_final_g = stats[0]

    d_mod = [[None] * 3 for _ in range(DEPTH)]
    d_norm = [[None] * 3 for _ in range(DEPTH)]
    dw_ffn_in = [[None] * 2 for _ in range(DEPTH)]
    dw_ffn_out = [[None] * 2 for _ in range(DEPTH)]
    dcm = [dict() for _ in range(na)]
    ddn = [dict() for _ in range(nb)]
    exchanges = {}

    def gather_small_grads():
        dmod_loc = jnp.stack([jnp.stack(d_mod[l], axis=1) for l in range(DEPTH)]).reshape(DEPTH, BL, 9 * D)
        small = [jnp.sum(dmod_loc, axis=1), jnp.stack([jnp.stack(d_norm[l]) for l in range(DEPTH)]),
                 jnp.stack([d["b_glu"] for d in dcm]), jnp.stack([d["w_dw"] for d in dcm]), jnp.stack([d["b_dw"] for d in dcm]),
                 jnp.stack([d["ln_g"] for d in dcm]), jnp.stack([d["ln_b"] for d in dcm]), jnp.stack([d["b_pw"] for d in dcm]),
                 jnp.stack([d["w_sconv"] for d in ddn]), jnp.stack([d["a_log"] for d in ddn]),
                 jnp.stack([d["dt_bias"] for d in ddn]), jnp.stack([d["o_g"] for d in ddn]), d_final_g]
        dmod_g, small_parts = _all_gather([dmod_loc, _pack(small)], "gather_small_grads")
        return dmod_g, small_parts, [a.shape for a in small]

    token = jnp.zeros((), F32)
    for l in reversed(range(DEPTH)):
        rec = saved[l]
        for s, j in ((1, 2), (0, 0)):
            w_in, w_out = ffn_weights(l, s)
            ssg, ng = mod[l, :, j] + token, norm_g_f[l, j][None]
            r = rec[s]
            dx, dgu, hb, dout, dssg, dng = _ffn_bwd_a(r["x"], dx, ssg, ng, r["y"], r["gu"], w_in, w_out)
            dw_ffn_in[l][s], dw_ffn_out[l][s] = _ffn_bwd_w(hb, dgu, r["hid"], dout)
            d_mod[l][j], d_norm[l][j] = dssg, dng[0]
            if j == 2:
                ssg1, ng1 = mod[l, :, 1], norm_g_f[l, 1][None]
                r = rec["mix"]
                if l % 2 == 0:
                    a = l // 2
                    du2, dy, dgate, db_pw = _matmul_res_bwd(dx, r["y"], ssg1, r["w_pw"])
                    du1, dln = _cm_mid_bwd_a(du2, r["u1"], cm_ln_g[a][None], cm_ln_b[a][None])
                    dab, dw_dw, db_dw, db_glu = _cm_mid_bwd_b(du1, r["ab"], r["w_dw"])
                    dx, dssg, dng = _premod_matmul_bwd(r["x"], dx, ssg1, ng1, [dab], r["w_glu"])
                    dcm[a] = dict(w_glu=_wgrad_shards(r["h"], dab, NDEV), w_pw=_wgrad(r["u2"], dy).reshape(NDEV, dsh, D),
                                  b_glu=db_glu[0], w_dw=dw_dw[:CW], b_dw=db_dw[0], ln_g=dln[0], ln_b=dln[1], b_pw=db_pw[0])
                else:
                    mi = l // 2
                    dog, dy, dgate, _ = _matmul_res_bwd(dx, r["y"], ssg1, r["w_o"])
                    do, dz, d_og = _dn_gnorm_bwd(dog, r["o"], r["proj"], dn_o_g[mi][None])
                    du, dw, dqg, dkd, dintra, degl = _dn_scan_bwd(do, r["w"], r["qg"], r["kd"], r["intra"], r["egl"],
                                                                   r["vn"], r["s0"])
                    dq, dk, dv, dgb, dbb = _dn_pre_bwd(r["q"], r["k"], r["v"], r["gb"], r["bb"], r["tinv"],
                                                       du, dw, dqg, dkd, dintra, degl)
                    dconv, dab16, dhead = _dn_prep_bwd(dq, dk, dv, dgb, dbb, r["conv"], r["proj"], r["alog"], r["dtb"])
                    dpre, dw_sc = _dn_sconv_bwd(dconv, r["proj"], r["w_sc"])
                    dx, dssg, dng = _premod_matmul_bwd(r["x"], dx, ssg1, ng1, [dpre, dz, dab16], r["w_proj"])
                    dw_in = jnp.concatenate([_wgrad(dpre, r["h"]), _wgrad(dz, r["h"]),
                                             _wgrad(dab16, r["h"])[:2 * NH]], axis=0)
                    ddn[mi] = dict(w_in=dw_in.reshape(NDEV, -1, D), w_out=_wgrad(r["og"], dy).reshape(NDEV, dsh, D),
                                   w_sconv=dw_sc[:SCW], a_log=dhead[0, :NH], dt_bias=dhead[1, :NH], o_g=d_og[0])
                d_mod[l][1] = dssg.at[:, 2].set(dgate[:, 0])
                d_norm[l][1] = dng[0]
            unit = [dw_ffn_in[l][s].reshape(NDEV, tf, D), dw_ffn_out[l][s].reshape(NDEV, FF // NDEV, D)]
            if j == 2:
                g = dcm[l // 2] if l % 2 == 0 else ddn[l // 2]
                unit += [g["w_glu"], g["w_pw"]] if l % 2 == 0 else [g["w_in"], g["w_out"]]
            if l == 0 and s == 0:
                dmod_g, small_parts, full_shapes = gather_small_grads()
                unit[0], dmod_g, small_parts = lax.optimization_barrier((unit[0], dmod_g, small_parts))
                small_gathered = (dmod_g, small_parts, full_shapes)
            exchanges[l, s], token = _exchange_start(unit, False, f"grads_start_{l}_{s}")
    grad_x = dx.reshape(BL, T, D)

    dmod_g, small_parts, full_shapes = small_gathered
    dmod_all = jnp.transpose(dmod_g, (1, 0, 2, 3)).reshape(DEPTH, NDEV * BL, 9 * D)
    g_w_ada = _ada_bwd(c_all, lax.dynamic_slice_in_dim(dmod_all, me * mcols, mcols, axis=2))

    got = []
    for l in range(DEPTH):
        ea = _exchange_wait(exchanges[l, 0], dx, f"grads_wait_{l}_0") if l > 0 else [None, None]
        eb = _exchange_wait(exchanges[l, 1], dx, f"grads_wait_{l}_1")
        got.append([ea[0], eb[0], ea[1], eb[1], eb[2], eb[3]])

    names = ["b_ada", "norm_g", "cm_b_glu", "cm_w_dw", "cm_b_dw", "cm_ln_g", "cm_ln_b", "cm_b_pw",
             "dn_w_sconv", "dn_a_log", "dn_dt_bias", "dn_o_g", "final_g"]
    cols = lambda a, width: lax.dynamic_slice_in_dim(a, me * width, width, axis=a.ndim - 1)
    local = {"norm_g": lambda a: cols(a, dsh), "cm_w_dw": lambda a: cols(a, dsh), "dn_w_sconv": lambda a: cols(a, 3 * dsh)}
    summed = _unpack(_sum_parts(small_parts), full_shapes)
    mine = [local.get(nm, lambda a: a)(p) for nm, p in zip(names, summed)]
    small_w = dict(b_ada=(b_ada, m_b_ada, v_b_ada), norm_g=(norm_g, m_norm_g, v_norm_g),
                   cm_b_glu=(cm_b_glu, m_cm_b_glu, v_cm_b_glu), cm_w_dw=(cm_w_dw, m_cm_w_dw, v_cm_w_dw),
                   cm_b_dw=(cm_b_dw, m_cm_b_dw, v_cm_b_dw), cm_ln_g=(cm_ln_g, m_cm_ln_g, v_cm_ln_g),
                   cm_ln_b=(cm_ln_b, m_cm_ln_b, v_cm_ln_b), cm_b_pw=(cm_b_pw, m_cm_b_pw, v_cm_b_pw),
                   dn_w_sconv=(dn_w_sconv, m_dn_w_sconv, v_dn_w_sconv), dn_a_log=(dn_a_log, m_dn_a_log, v_dn_a_log),
                   dn_dt_bias=(dn_dt_bias, m_dn_dt_bias, v_dn_dt_bias), dn_o_g=(dn_o_g, m_dn_o_g, v_dn_o_g),
                   final_g=(final_g, m_final_g, v_final_g))
    loc_shapes = [small_w[nm][0].shape for nm in names]
    sres_raw = _adamw(_pack(mine)[None], *[_pack([small_w[nm][q] for nm in names]) for q in range(3)])
    sres = [dict(zip(names, _unpack(r, loc_shapes))) for r in sres_raw]

    res = {}

    def update(slots, wmv, view, back=None, outs=None):
        w2, m2, v2 = [view(a) for a in wmv]
        outs = [lax.empty(w2.shape, F32) for _ in range(4)] if outs is None else outs
        for p, row0, col in slots:
            outs = _adamw_slot(p, w2, m2, v2, outs, row0, col)
        return outs if back is None else [back(o) for o in outs]

    ffn_slots = [(l, s) for l in reversed(range(DEPTH)) for s in (1, 0)][:-1]
    wmv_in, view_in = (w_ffn_in, m_w_ffn_in, v_w_ffn_in), lambda a: tr_ffn(a).reshape(-1, D)
    wmv_out, view_out = (w_ffn_out, m_w_ffn_out, v_w_ffn_out), lambda a: a.reshape(-1, D)
    part_in = update([(got[l][s], 2 * l + s, 0) for l, s in ffn_slots], wmv_in, view_in)
    part_out = update([(got[l][2 + s], 2 * l + s, 0) for l, s in ffn_slots], wmv_out, view_out)
    cgl = cm_w_glu.shape[2]
    res["cm_w_glu"] = update([(got[2 * a][4], a, 0) for a in range(na)], (cm_w_glu, m_cm_w_glu, v_cm_w_glu),
                             lambda a: a.reshape(-1, cgl), lambda o: o.reshape(cm_w_glu.shape))
    res["cm_w_pw"] = update([(got[2 * a][5], a, 0) for a in range(na)], (cm_w_pw, m_cm_w_pw, v_cm_w_pw),
                            lambda a: a.reshape(-1, D), lambda o: o.reshape(cm_w_pw.shape))
    cdn = dn_w_in.shape[2]
    res["dn_w_in"] = update([(got[2 * i + 1][4], 0, i) for i in range(nb)], (dn_w_in, m_dn_w_in, v_dn_w_in),
                            lambda a: tr_dn(a).reshape(cdn, nb * D),
                            lambda o: jnp.transpose(o.reshape(cdn, nb, D), (1, 2, 0)))
    res["dn_w_out"] = update([(got[2 * i + 1][5], i, 0) for i in range(nb)], (dn_w_out, m_dn_w_out, v_dn_w_out),
                             lambda a: a.reshape(-1, D), lambda o: o.reshape(dn_w_out.shape))
    res["w_ada"] = [o.reshape(w_ada.shape) for o in
                    _adamw(g_w_ada, *[a.reshape(-1, mcols) for a in (w_ada, m_w_ada, v_w_ada)])]
    done = [part_in[0], part_out[0], sres_raw[0]] + [res[nm][0] for nm in ("cm_w_glu", "cm_w_pw", "dn_w_in", "dn_w_out", "w_ada")]
    last = _exchange_wait(exchanges[0, 0], done, "grads_wait_0_0")
    res["w_ffn_in"] = update([(last[0], 0, 0)], wmv_in, view_in,
                             lambda o: jnp.swapaxes(o.reshape(DEPTH, 2, tf, D), 2, 3), part_in)
    res["w_ffn_out"] = update([(last[1], 0, 0)], wmv_out, view_out, lambda o: o.reshape(w_ffn_out.shape), part_out)
    for nm in names:
        res[nm] = [sres[q][nm] for q in range(4)]

    order = ["norm_g", "w_ada", "b_ada", "w_ffn_in", "w_ffn_out", "cm_w_glu", "cm_b_glu", "cm_w_dw", "cm_b_dw", "cm_ln_g",
             "cm_ln_b", "cm_w_pw", "cm_b_pw", "dn_w_in", "dn_w_sconv", "dn_a_log", "dn_dt_bias", "dn_o_g", "dn_w_out", "final_g"]
    return (loss, grad_x, *[res[nm][0] for nm in order], *[res[nm][1] for nm in order],
            *[res[nm][2] for nm in order], *[res[nm][3] for nm in order])
```

```python
import functools

import jax
import jax.numpy as jnp
from jax import lax
from jax.experimental import pallas as pl
from jax.experimental.pallas import tpu as pltpu

F32 = jnp.float32
BF16 = jnp.bfloat16
HI = lax.Precision.HIGHEST
INV_PREC = None
MESH = pl.DeviceIdType.MESH
AXES = ("x", "y", "c")

NDEV = 8
D = 1024
T = 2048
BL = 2
FF = 2816
NH = 8
DH = 128
CW = 31
SCW = 4
CHUNK = 64
DEPTH = 4
EPS = 1e-6
LR, B1, B2, AEPS, WD, STEP = 0.001, 0.9, 0.999, 1e-08, 0.01, 10

VMEM_LIMIT_BYTES = 56 * 1024 * 1024
RELAY_LAYERS = 2
HALO = 32
SHALO = 8


def _pcall(body, **kw):
    return pl.pallas_call(body, **kw)


def _cp(sem=None):
    return pltpu.CompilerParams(dimension_semantics=sem, vmem_limit_bytes=VMEM_LIMIT_BYTES)


def _sds(shape, dtype):
    return jax.ShapeDtypeStruct(tuple(shape), dtype)


def _dot(a, b):
    return jnp.dot(a, b, preferred_element_type=F32)


def _dot_nt(a, b):
    return lax.dot_general(a, b, (((1,), (1,)), ((), ())), preferred_element_type=F32)


def _dot_tn(a, b):
    return lax.dot_general(a, b, (((0,), (0,)), ((), ())), preferred_element_type=F32)


def _modulate(x, ng, scale, shift):
    r = lax.rsqrt(jnp.mean(x * x, axis=-1, keepdims=True) + EPS)
    return (x * r * ng) * (1.0 + scale) + shift


def _modulate_bwd(x, ng, scale, shift, dh):
    r = lax.rsqrt(jnp.mean(x * x, axis=-1, keepdims=True) + EPS)
    xh = x * r
    xg = xh * ng
    h = xg * (1.0 + scale) + shift
    a = dh * (1.0 + scale)
    dxh = a * ng
    dx = r * (dxh - xh * jnp.mean(dxh * xh, axis=-1, keepdims=True))
    return (h, dx, jnp.sum(a * xh, axis=0, keepdims=True), jnp.sum(dh * xg, axis=0, keepdims=True),
            jnp.sum(dh, axis=0, keepdims=True))


def _acc_rows(ref, first, rows):
    @pl.when(first)
    def _():
        ref[...] = jnp.zeros_like(ref)

    for r, val in enumerate(rows):
        ref[r:r + 1, :] += val


def _my_pos():
    return lax.axis_index("x"), lax.axis_index("y"), lax.axis_index("c")


def _all_gather(arrs, name):
    n = len(arrs)

    def body(*refs):
        ins, outs = refs[:n], refs[n:2 * n]
        send, recv, loc = refs[2 * n:]
        x, y, c = _my_pos()
        me, sibling = (x, y, c), (x, y, 1 - c)
        chips = [(1 - x, y), (x, 1 - y), (1 - x, 1 - y)]

        def copy(a, k, block, to, src=None):
            dst = outs[a].at[4 * block[0] + 2 * block[1] + block[2]]
            return pltpu.make_async_remote_copy(
                src_ref=dst if src is None else src, dst_ref=dst,
                send_sem=send.at[7 * a + k], recv_sem=recv.at[7 * a + k],
                device_id=to, device_id_type=MESH)

        mine, first, passed = [], [], []
        for a in range(n):
            m = pltpu.make_async_copy(ins[a], outs[a].at[4 * x + 2 * y + c], loc.at[a])
            m.start()
            mine.append(m)
            f = [copy(a, 0, me, sibling, src=ins[a])]
            f += [copy(a, 1 + j, me, (*chip, c), src=ins[a]) for j, chip in enumerate(chips)]
            for cp in f:
                cp.start()
            first += f
        for a in range(n):
            for j, chip in enumerate(chips):
                copy(a, 1 + j, (*chip, c), me).wait_recv()
                p = copy(a, 4 + j, (*chip, c), sibling)
                p.start()
                passed.append(p)
        for a in range(n):
            copy(a, 0, sibling, me).wait_recv()
            for j, chip in enumerate(chips):
                copy(a, 4 + j, (*chip, 1 - c), me).wait_recv()
        for cp in first + passed:
            cp.wait_send()
        for m in mine:
            m.wait()

    hbm = pl.BlockSpec(memory_space=pl.ANY)
    return _pcall(
        body, name=name,
        out_shape=[_sds((NDEV,) + a.shape, a.dtype) for a in arrs],
        in_specs=[hbm] * n, out_specs=[hbm] * n,
        scratch_shapes=[pltpu.SemaphoreType.DMA((7 * n,)), pltpu.SemaphoreType.DMA((7 * n,)),
                        pltpu.SemaphoreType.DMA((n,))],
    )(*arrs)


def _peer(k):
    x, y, c = _my_pos()
    return (1 - x if k & 4 else x, 1 - y if k & 2 else y, 1 - c if k & 1 else c)


def _dev_index(p):
    return 4 * p[0] + 2 * p[1] + p[2]


_HBM = pl.BlockSpec(memory_space=pltpu.HBM)
_SEM = pl.BlockSpec(memory_space=pltpu.SEMAPHORE)
_EFFECT = pltpu.SideEffectType.DATAFLOW_SIDE_EFFECTING


def _exchange_start(srcs, gather, name):
    n = len(srcs)
    me = _dev_index(_my_pos())
    lands = []
    for s in srcs:
        own = s if gather else lax.dynamic_index_in_dim(s, me, 0, keepdims=False)
        shape = (NDEV,) + s.shape if gather else s.shape
        lands.append(lax.dynamic_update_index_in_dim(lax.empty(shape, s.dtype), own, me, 0))

    def body(*refs):
        src_refs, land_refs = refs[:n], refs[n:2 * n]
        sends, recvs = refs[2 * n:3 * n], refs[3 * n:4 * n]
        token = refs[-1]
        mine = _dev_index(_my_pos())
        for a in range(n):
            for k in range(1, 8):
                p = _peer(k)
                pltpu.make_async_remote_copy(
                    src_ref=src_refs[a] if gather else src_refs[a].at[_dev_index(p)],
                    dst_ref=land_refs[a].at[mine], send_sem=sends[a], recv_sem=recvs[a],
                    device_id=p, device_id_type=MESH).start()
        token[...] = jnp.zeros_like(token)

    out = pl.pallas_call(
        body, name=name,
        out_shape=(*[pltpu.SemaphoreType.DMA(())] * (2 * n),
                   *[pltpu.HBM(a.shape, a.dtype) for a in srcs], *[pltpu.HBM(a.shape, a.dtype) for a in lands],
                   _sds((8, 128), F32)),
        in_specs=[_HBM] * (2 * n),
        out_specs=(*[_SEM] * (2 * n), *[_HBM] * (2 * n), pl.BlockSpec(memory_space=pltpu.VMEM)),
        input_output_aliases={i: 2 * n + i for i in range(2 * n)},
        compiler_params=pltpu.CompilerParams(has_side_effects=_EFFECT),
    )(*[pltpu.with_memory_space_constraint(a, pltpu.HBM) for a in srcs],
      *[pltpu.with_memory_space_constraint(a, pltpu.HBM) for a in lands])
    state = (out[:n], out[n:2 * n], out[2 * n:3 * n], out[3 * n:4 * n])
    return state, out[-1][0, 0]


def _exchange_wait(state, after, name, blocks=NDEV - 1):
    sends, recvs, srcs, lands = state
    n = len(srcs)
    after = list(after) if isinstance(after, (list, tuple)) else [after]

    def body(*refs):
        land_refs = refs[n:2 * n]
        send_refs, recv_refs = refs[2 * n:3 * n], refs[3 * n:4 * n]
        for a in range(n):
            seven = land_refs[a].at[pl.ds(0, blocks)]
            cp = pltpu.make_async_remote_copy(src_ref=seven, dst_ref=seven, send_sem=send_refs[a], recv_sem=recv_refs[a],
                                              device_id=_peer(1), device_id_type=MESH)
            cp.wait_send()
            cp.wait_recv()

    out = pl.pallas_call(
        body, name=name,
        out_shape=(*[pltpu.HBM(a.shape, a.dtype) for a in srcs], *[pltpu.HBM(a.shape, a.dtype) for a in lands]),
        in_specs=(*[_HBM] * (2 * n), *[_SEM] * (2 * n), *[pl.BlockSpec(memory_space=pl.ANY)] * len(after)),
        out_specs=tuple([_HBM] * (2 * n)),
        input_output_aliases={i: i for i in range(2 * n)},
        compiler_params=pltpu.CompilerParams(has_side_effects=_EFFECT),
    )(*srcs, *lands, *sends, *recvs, *after)
    return list(out[n:])


def _other_chips():
    x, y, _ = _my_pos()
    return [(1 - x, y), (x, 1 - y), (1 - x, 1 - y)]


def _relay_gather_start(srcs, name):
    n = len(srcs)
    me = _dev_index(_my_pos())
    lands = [lax.dynamic_update_index_in_dim(lax.empty((NDEV,) + s.shape, s.dtype), s, me, 0) for s in srcs]

    def body(*refs):
        src_refs, land_refs = refs[:n], refs[n:2 * n]
        sa, ra, sb, rb = refs[2 * n:3 * n], refs[3 * n:4 * n], refs[4 * n:5 * n], refs[5 * n:6 * n]
        token = refs[-1]
        x, y, c = _my_pos()
        mine = _dev_index((x, y, c))
        for a in range(n):
            pltpu.make_async_remote_copy(src_ref=src_refs[a], dst_ref=land_refs[a].at[mine], send_sem=sa[a], recv_sem=ra[a],
                                         device_id=(x, y, 1 - c), device_id_type=MESH).start()
            for chip in _other_chips():
                pltpu.make_async_remote_copy(src_ref=src_refs[a], dst_ref=land_refs[a].at[mine], send_sem=sb[a],
                                             recv_sem=rb[a], device_id=(*chip, c), device_id_type=MESH).start()
        token[...] = jnp.zeros_like(token)

    out = pl.pallas_call(
        body, name=name,
        out_shape=(*[pltpu.SemaphoreType.DMA(())] * (4 * n),
                   *[pltpu.HBM(a.shape, a.dtype) for a in srcs], *[pltpu.HBM(a.shape, a.dtype) for a in lands],
                   _sds((8, 128), F32)),
        in_specs=[_HBM] * (2 * n),
        out_specs=(*[_SEM] * (4 * n), *[_HBM] * (2 * n), pl.BlockSpec(memory_space=pltpu.VMEM)),
        input_output_aliases={i: 4 * n + i for i in range(2 * n)},
        compiler_params=pltpu.CompilerParams(has_side_effects=_EFFECT),
    )(*[pltpu.with_memory_space_constraint(a, pltpu.HBM) for a in srcs],
      *[pltpu.with_memory_space_constraint(a, pltpu.HBM) for a in lands])
    sems = [out[q * n:(q + 1) * n] for q in range(4)]
    return (*sems, out[4 * n:5 * n], out[5 * n:6 * n]), out[-1][0, 0]


def _relay_gather_pass(state, after, name):
    sa, ra, sb, rb, srcs, lands = state
    n = len(srcs)
    after = list(after) if isinstance(after, (list, tuple)) else [after]

    def body(*refs):
        land_refs = refs[n:2 * n]
        sa_r, ra_r, sb_r, rb_r = [refs[(2 + q) * n:(3 + q) * n] for q in range(4)]
        outs = refs[6 * n + len(after):]
        sc, rc = outs[2 * n:3 * n], outs[3 * n:4 * n]
        x, y, c = _my_pos()
        for a in range(n):
            one, three = land_refs[a].at[pl.ds(0, 1)], land_refs[a].at[pl.ds(0, 3)]
            for blocks, s_sem, r_sem in ((one, sa_r[a], ra_r[a]), (three, sb_r[a], rb_r[a])):
                cp = pltpu.make_async_remote_copy(src_ref=blocks, dst_ref=blocks, send_sem=s_sem, recv_sem=r_sem,
                                                  device_id=(x, y, 1 - c), device_id_type=MESH)
                cp.wait_send()
                cp.wait_recv()
            for chip in _other_chips():
                blk = land_refs[a].at[_dev_index((*chip, c))]
                pltpu.make_async_remote_copy(src_ref=blk, dst_ref=blk, send_sem=sc[a], recv_sem=rc[a],
                                             device_id=(x, y, 1 - c), device_id_type=MESH).start()

    out = pl.pallas_call(
        body, name=name,
        out_shape=(*[pltpu.HBM(a.shape, a.dtype) for a in srcs], *[pltpu.HBM(a.shape, a.dtype) for a in lands],
                   *[pltpu.SemaphoreType.DMA(())] * (2 * n)),
        in_specs=(*[_HBM] * (2 * n), *[_SEM] * (4 * n), *[pl.BlockSpec(memory_space=pl.ANY)] * len(after)),
        out_specs=(*[_HBM] * (2 * n), *[_SEM] * (2 * n)),
        input_output_aliases={i: i for i in range(2 * n)},
        compiler_params=pltpu.CompilerParams(has_side_effects=_EFFECT),
    )(*srcs, *lands, *sa, *ra, *sb, *rb, *after)
    return (out[2 * n:3 * n], out[3 * n:4 * n], out[:n], out[n:2 * n])


def _ffn_tiles():
    tm = min(512, T)
    return tm, T // tm


def _ffn_fwd(x, ssg, ng, w_in, w_out):
    n = x.shape[0]
    _, nf, tf, _ = w_in.shape
    tm, tpb = _ffn_tiles()

    def body(x_ref, ssg_ref, ng_ref, win_ref, wout_ref, xn_ref, gu_ref, hid_ref, y_ref, h_scr, acc):
        j = pl.program_id(1)

        @pl.when(j == 0)
        def _():
            s = ssg_ref[0]
            h_scr[...] = _modulate(x_ref[...], ng_ref[...], s[1:2], s[0:1]).astype(BF16)
            acc[...] = jnp.zeros_like(acc)

        h = h_scr[...]
        g = _dot_nt(h, win_ref[0])
        u = _dot_nt(h, win_ref[1])
        gu_ref[0] = g.astype(BF16)
        gu_ref[1] = u.astype(BF16)
        hid = (g * jax.nn.sigmoid(g) * u).astype(BF16)
        hid_ref[...] = hid
        acc[...] += _dot(hid, wout_ref[...])

        @pl.when(j == nf - 1)
        def _():
            yv = acc[...]
            y_ref[...] = yv.astype(BF16)
            xn_ref[...] = x_ref[...] + (0.5 * (1.0 + ssg_ref[0][2:3])) * yv

    return _pcall(
        body, name="ffn_fwd", grid=(n // tm, nf),
        in_specs=[pl.BlockSpec((tm, D), lambda i, j: (i, 0)),
                  pl.BlockSpec((1, 3, D), lambda i, j: (i // tpb, 0, 0)),
                  pl.BlockSpec((1, D), lambda i, j: (0, 0)),
                  pl.BlockSpec((2, None, tf, D), lambda i, j: (0, j, 0, 0)),
                  pl.BlockSpec((None, tf, D), lambda i, j: (j, 0, 0))],
        out_specs=[pl.BlockSpec((tm, D), lambda i, j: (i, 0)),
                   pl.BlockSpec((2, None, tm, tf), lambda i, j: (0, j, i, 0)),
                   pl.BlockSpec((None, tm, tf), lambda i, j: (j, i, 0)),
                   pl.BlockSpec((tm, D), lambda i, j: (i, 0))],
        out_shape=[_sds((n, D), F32), _sds((2, nf, n, tf), BF16), _sds((nf, n, tf), BF16), _sds((n, D), BF16)],
        scratch_shapes=[pltpu.VMEM((tm, D), BF16), pltpu.VMEM((tm, D), F32)],
        compiler_params=_cp(("arbitrary", "arbitrary")),
    )(x, ssg, ng, w_in, w_out)


def _ffn_bwd_a(x, dxn, ssg, ng, y, gu, w_in, w_out):
    n = x.shape[0]
    _, nf, tf, _ = w_in.shape
    tm, tpb = _ffn_tiles()

    def body(x_ref, dxn_ref, ssg_ref, ng_ref, y_ref, gu_ref, win_ref, wout_ref,
             dx_ref, dgu_ref, h_ref, dout_ref, dssg_ref, dng_ref, dout_scr, dh_acc):
        i, j = pl.program_id(0), pl.program_id(1)

        @pl.when(j == 0)
        def _():
            db = ((0.5 * (1.0 + ssg_ref[0][2:3])) * dxn_ref[...]).astype(BF16)
            dout_scr[...] = db
            dout_ref[...] = db
            dh_acc[...] = jnp.zeros_like(dh_acc)

        dhid = _dot_nt(dout_scr[...], wout_ref[...]).astype(BF16)
        g = gu_ref[0]
        u = gu_ref[1]
        sig = jax.nn.sigmoid(g)
        dg = dhid * u * (sig * (1.0 + g * (1.0 - sig)))
        du = dhid * (g * sig)
        dgu_ref[0] = dg
        dgu_ref[1] = du
        dh_acc[...] += _dot(dg, win_ref[0])
        dh_acc[...] += _dot(du, win_ref[1])

        @pl.when(j == nf - 1)
        def _():
            s = ssg_ref[0]
            h, dx_, dng_, dsc_, dsh_ = _modulate_bwd(x_ref[...], ng_ref[...], s[1:2], s[0:1], dh_acc[...])
            h_ref[...] = h.astype(BF16)
            dxn = dxn_ref[...]
            dx_ref[...] = dxn + dx_
            dgate = jnp.sum(0.5 * dxn * y_ref[...].astype(F32), axis=0, keepdims=True)
            _acc_rows(dssg_ref.at[0], i % tpb == 0, [dsh_, dsc_, dgate])
            _acc_rows(dng_ref, i == 0, [dng_])

    return _pcall(
        body, name="ffn_bwd_a", grid=(n // tm, nf),
        in_specs=[pl.BlockSpec((tm, D), lambda i, j: (i, 0)),
                  pl.BlockSpec((tm, D), lambda i, j: (i, 0)),
                  pl.BlockSpec((1, 3, D), lambda i, j: (i // tpb, 0, 0)),
                  pl.BlockSpec((1, D), lambda i, j: (0, 0)),
                  pl.BlockSpec((tm, D), lambda i, j: (i, 0)),
                  pl.BlockSpec((2, None, tm, tf), lambda i, j: (0, j, i, 0)),
                  pl.BlockSpec((2, None, tf, D), lambda i, j: (0, j, 0, 0)),
                  pl.BlockSpec((None, tf, D), lambda i, j: (j, 0, 0))],
        out_specs=[pl.BlockSpec((tm, D), lambda i, j: (i, 0)),
                   pl.BlockSpec((2, None, tm, tf), lambda i, j: (0, j, i, 0)),
                   pl.BlockSpec((tm, D), lambda i, j: (i, 0)),
                   pl.BlockSpec((tm, D), lambda i, j: (i, 0)),
                   pl.BlockSpec((1, 3, D), lambda i, j: (i // tpb, 0, 0)),
                   pl.BlockSpec((1, D), lambda i, j: (0, 0))],
        out_shape=[_sds((n, D), F32), _sds((2, nf, n, tf), BF16), _sds((n, D), BF16), _sds((n, D), BF16),
                   _sds((BL, 3, D), F32), _sds((1, D), F32)],
        scratch_shapes=[pltpu.VMEM((tm, D), BF16), pltpu.VMEM((tm, D), F32)],
        compiler_params=_cp(("arbitrary", "arbitrary")),
    )(x, dxn, ssg, ng, y, gu, w_in, w_out)


def _ffn_bwd_w(h, dgu, hid, dout):
    n = h.shape[0]
    _, nf, _, tf = dgu.shape
    tm = min(1024, T)
    ni = n // tm

    def body(h_ref, dgu_ref, hid_ref, dout_ref, dwin_ref, dwout_ref, acc_g, acc_u, acc_o):
        i = pl.program_id(1)

        @pl.when(i == 0)
        def _():
            acc_g[...] = jnp.zeros_like(acc_g)
            acc_u[...] = jnp.zeros_like(acc_u)
            acc_o[...] = jnp.zeros_like(acc_o)

        hv = h_ref[...]
        acc_g[...] += _dot_tn(dgu_ref[0], hv)
        acc_u[...] += _dot_tn(dgu_ref[1], hv)
        acc_o[...] += _dot_tn(hid_ref[...], dout_ref[...])

        @pl.when(i == ni - 1)
        def _():
            dwin_ref[0] = acc_g[...].astype(BF16)
            dwin_ref[1] = acc_u[...].astype(BF16)
            dwout_ref[...] = acc_o[...].astype(BF16)

    return _pcall(
        body, name="ffn_bwd_w", grid=(nf, ni),
        in_specs=[pl.BlockSpec((tm, D), lambda j, i: (i, 0)),
                  pl.BlockSpec((2, None, tm, tf), lambda j, i: (0, j, i, 0)),
                  pl.BlockSpec((None, tm, tf), lambda j, i: (j, i, 0)),
                  pl.BlockSpec((tm, D), lambda j, i: (i, 0))],
        out_specs=[pl.BlockSpec((2, None, tf, D), lambda j, i: (0, j, 0, 0)),
                   pl.BlockSpec((None, tf, D), lambda j, i: (j, 0, 0))],
        out_shape=[_sds((2, nf, tf, D), BF16), _sds((nf, tf, D), BF16)],
        scratch_shapes=[pltpu.VMEM((tf, D), F32), pltpu.VMEM((tf, D), F32), pltpu.VMEM((tf, D), F32)],
        compiler_params=_cp(("arbitrary", "arbitrary")),
    )(h, dgu, hid, dout)


def _premod_matmul(x, ssg, ng, w, bias, tn):
    n = x.shape[0]
    shards = w.ndim == 3
    m = w.shape[0] * w.shape[2] if shards else w.shape[0]
    tm = min(512, T)
    tpb = T // tm
    to = m if shards else tn
    w_spec = (pl.BlockSpec(w.shape, lambda i, j: (0, 0, 0)) if shards
              else pl.BlockSpec((tn, D), lambda i, j: (j, 0)))

    def body(x_ref, ssg_ref, ng_ref, w_ref, b_ref, h_ref, o_ref, h_scr):
        @pl.when(pl.program_id(1) == 0)
        def _():
            s = ssg_ref[0]
            hb = _modulate(x_ref[...], ng_ref[...], s[1:2], s[0:1]).astype(BF16)
            h_scr[...] = hb
            h_ref[...] = hb

        hv = h_scr[...]
        if shards:
            for q in range(w.shape[0]):
                cols = slice(q * tn, (q + 1) * tn)
                o_ref[:, cols] = _dot(hv, w_ref[q]) + b_ref[:, cols]
        else:
            o_ref[...] = _dot_nt(hv, w_ref[...]) + b_ref[...]

    return _pcall(
        body, name="premod_matmul", grid=(n // tm, m // to),
        in_specs=[pl.BlockSpec((tm, D), lambda i, j: (i, 0)),
                  pl.BlockSpec((1, 3, D), lambda i, j: (i // tpb, 0, 0)),
                  pl.BlockSpec((1, D), lambda i, j: (0, 0)),
                  w_spec,
                  pl.BlockSpec((1, to), lambda i, j: (0, j))],
        out_specs=[pl.BlockSpec((tm, D), lambda i, j: (i, 0)),
                   pl.BlockSpec((tm, to), lambda i, j: (i, j))],
        out_shape=[_sds((n, D), BF16), _sds((n, m), F32)],
        scratch_shapes=[pltpu.VMEM((tm, D), BF16)],
        compiler_params=_cp(("arbitrary", "arbitrary")),
    )(x, ssg, ng, w, bias)


def _premod_matmul_bwd(x, dxn, ssg, ng, douts, w):
    n = x.shape[0]
    k = len(douts)
    shards = w.ndim == 3
    tm = min(512, T)
    tpb = T // tm

    def body(*refs):
        x_ref, dxn_ref, ssg_ref, ng_ref = refs[:4]
        do_refs, w_ref = refs[4:4 + k], refs[4 + k]
        dx_ref, dssg_ref, dng_ref = refs[5 + k:]
        i = pl.program_id(0)
        dh = jnp.zeros((tm, D), F32)
        if shards:
            cs = w.shape[2]
            dov = do_refs[0][...]
            for j in range(w.shape[0]):
                dh += _dot_nt(dov[:, j * cs:(j + 1) * cs], w_ref[j])
        else:
            off = 0
            for q in range(k):
                mk = douts[q].shape[1]
                dh += _dot(do_refs[q][...], w_ref[off:off + mk, :])
                off += mk
        s = ssg_ref[0]
        _, dx_, dng_, dsc_, dsh_ = _modulate_bwd(x_ref[...], ng_ref[...], s[1:2], s[0:1], dh)
        dx_ref[...] = dxn_ref[...] + dx_
        _acc_rows(dssg_ref.at[0], i % tpb == 0, [dsh_, dsc_, jnp.zeros_like(dsh_)])
        _acc_rows(dng_ref, i == 0, [dng_])

    return _pcall(
        body, name="premod_matmul_bwd", grid=(n // tm,),
        in_specs=[pl.BlockSpec((tm, D), lambda i: (i, 0)),
                  pl.BlockSpec((tm, D), lambda i: (i, 0)),
                  pl.BlockSpec((1, 3, D), lambda i: (i // tpb, 0, 0)),
                  pl.BlockSpec((1, D), lambda i: (0, 0))]
                 + [pl.BlockSpec((tm, a.shape[1]), lambda i: (i, 0)) for a in douts]
                 + [pl.BlockSpec(w.shape, (lambda i: (0, 0, 0)) if shards else (lambda i: (0, 0)))],
        out_specs=[pl.BlockSpec((tm, D), lambda i: (i, 0)),
                   pl.BlockSpec((1, 3, D), lambda i: (i // tpb, 0, 0)),
                   pl.BlockSpec((1, D), lambda i: (0, 0))],
        out_shape=[_sds((n, D), F32), _sds((BL, 3, D), F32), _sds((1, D), F32)],
        compiler_params=_cp(("arbitrary",)),
    )(x, dxn, ssg, ng, *douts, w)


def _matmul_res(x, a, ssg, w, bias):
    n, kd = a.shape
    tm = min(512, T)
    tpb = T // tm

    def body(x_ref, a_ref, ssg_ref, w_ref, b_ref, xn_ref, y_ref):
        yv = _dot(a_ref[...], w_ref[...]) + b_ref[...]
        y_ref[...] = yv.astype(BF16)
        xn_ref[...] = x_ref[...] + (1.0 + ssg_ref[0][2:3]) * yv

    return _pcall(
        body, name="matmul_res", grid=(n // tm,),
        in_specs=[pl.BlockSpec((tm, D), lambda i: (i, 0)),
                  pl.BlockSpec((tm, kd), lambda i: (i, 0)),
                  pl.BlockSpec((1, 3, D), lambda i: (i // tpb, 0, 0)),
                  pl.BlockSpec((kd, D), lambda i: (0, 0)),
                  pl.BlockSpec((1, D), lambda i: (0, 0))],
        out_specs=[pl.BlockSpec((tm, D), lambda i: (i, 0)), pl.BlockSpec((tm, D), lambda i: (i, 0))],
        out_shape=[_sds((n, D), F32), _sds((n, D), BF16)],
        compiler_params=_cp(("arbitrary",)),
    )(x, a, ssg, w, bias)


def _matmul_res_bwd(dxn, y, ssg, w):
    n = dxn.shape[0]
    kd = w.shape[0]
    tm = min(512, T)
    tpb = T // tm

    def body(dxn_ref, y_ref, ssg_ref, w_ref, da_ref, dy_ref, dgate_ref, dbias_ref):
        i = pl.program_id(0)
        dxn = dxn_ref[...]
        dy = (1.0 + ssg_ref[0][2:3]) * dxn
        dyb = dy.astype(BF16)
        dy_ref[...] = dyb
        da_ref[...] = _dot_nt(dyb, w_ref[...])
        _acc_rows(dgate_ref.at[0], i % tpb == 0, [jnp.sum(dxn * y_ref[...].astype(F32), axis=0, keepdims=True)])
        _acc_rows(dbias_ref, i == 0, [jnp.sum(dy, axis=0, keepdims=True)])

    return _pcall(
        body, name="matmul_res_bwd", grid=(n // tm,),
        in_specs=[pl.BlockSpec((tm, D), lambda i: (i, 0)),
                  pl.BlockSpec((tm, D), lambda i: (i, 0)),
                  pl.BlockSpec((1, 3, D), lambda i: (i // tpb, 0, 0)),
                  pl.BlockSpec((kd, D), lambda i: (0, 0))],
        out_specs=[pl.BlockSpec((tm, kd), lambda i: (i, 0)),
                   pl.BlockSpec((tm, D), lambda i: (i, 0)),
                   pl.BlockSpec((1, 1, D), lambda i: (i // tpb, 0, 0)),
                   pl.BlockSpec((1, D), lambda i: (0, 0))],
        out_shape=[_sds((n, kd), F32), _sds((n, D), BF16), _sds((BL, 1, D), F32), _sds((1, D), F32)],
        compiler_params=_cp(("arbitrary",)),
    )(dxn, y, ssg, w)


def _wgrad_shards(a, b, ns):
    n, kd = a.shape
    cs = b.shape[1] // ns
    tm = min(512, T)
    ni = n // tm

    def body(a_ref, b_ref, o_ref, acc):
        i = pl.program_id(0)

        @pl.when(i == 0)
        def _():
            acc[...] = jnp.zeros_like(acc)

        at = a_ref[...].T
        for q in range(ns):
            acc[q] += _dot(at, b_ref[:, q * cs:(q + 1) * cs])

        @pl.when(i == ni - 1)
        def _():
            o_ref[...] = acc[...].astype(BF16)

    return _pcall(
        body, name="wgrad_shards", grid=(ni,),
        in_specs=[pl.BlockSpec((tm, kd), lambda i: (i, 0)), pl.BlockSpec((tm, ns * cs), lambda i: (i, 0))],
        out_specs=pl.BlockSpec((ns, kd, cs), lambda i: (0, 0, 0)),
        out_shape=_sds((ns, kd, cs), BF16),
        scratch_shapes=[pltpu.VMEM((ns, kd, cs), F32)],
        compiler_params=_cp(("arbitrary",)),
    )(a, b)


def _wgrad(a, b):
    n, kd = a.shape
    m = b.shape[1]
    tm = min(1024, T)
    tk = min(1024, kd)
    ni = n // tm

    def body(a_ref, b_ref, o_ref, acc):
        i = pl.program_id(1)

        @pl.when(i == 0)
        def _():
            acc[...] = jnp.zeros_like(acc)

        acc[...] += _dot_tn(a_ref[...], b_ref[...])

        @pl.when(i == ni - 1)
        def _():
            o_ref[...] = acc[...].astype(BF16)

    return _pcall(
        body, name="wgrad", grid=(kd // tk, ni),
        in_specs=[pl.BlockSpec((tm, tk), lambda q, i: (i, q)), pl.BlockSpec((tm, m), lambda q, i: (i, 0))],
        out_specs=pl.BlockSpec((tk, m), lambda q, i: (q, 0)),
        out_shape=_sds((kd, m), BF16),
        scratch_shapes=[pltpu.VMEM((tk, m), F32)],
        compiler_params=_cp(("arbitrary", "arbitrary")),
    )(a, b)


def _ln_silu(u1, g, b):
    mu = jnp.mean(u1, axis=-1, keepdims=True)
    xc = u1 - mu
    var = jnp.mean(xc * xc, axis=-1, keepdims=True)
    ln = xc * lax.rsqrt(var + EPS) * g + b
    return ln * jax.nn.sigmoid(ln)


def _conv_tiles():
    tt = min(256, T)
    return tt, T // tt


def _prev_halo_spec(cols, tt, halo):
    r = tt // halo
    return pl.BlockSpec((halo, cols), lambda b, i: (jnp.maximum(b * (T // halo) + i * r - 1, 0), 0))


def _next_halo_spec(cols, tt, halo):
    r = tt // halo
    last = BL * T // halo - 1
    return pl.BlockSpec((halo, cols), lambda b, i: (jnp.minimum(b * (T // halo) + (i + 1) * r, last), 0))


ROWS = 32
SROWS = 8


def _fill_rotations(rot, win, rows):
    for r in range(8):
        rot[r, 0:rows, :] = win[pl.ds(r, rows), :]


def _window(rot, off, start, size):
    return rot[off % 8, pl.ds(pl.multiple_of(start + (off // 8) * 8, 8), size), :]


def _cm_mid_fwd(ab, w_dw, b_dw, ln_g, ln_b):
    n = ab.shape[0]
    tt, nt = _conv_tiles()

    def body(ab_ref, halo_ref, w_ref, bdw_ref, g_ref, b_ref, u1_ref, u2_ref, win, rot):
        i = pl.program_id(1)
        hv = halo_ref[...]
        u0h = hv[:, :D] * jax.nn.sigmoid(hv[:, D:])
        win[0:HALO, :] = jnp.where(i == 0, 0.0, u0h)
        cv = ab_ref[...]
        win[HALO:HALO + tt, :] = cv[:, :D] * jax.nn.sigmoid(cv[:, D:])
        win[HALO + tt:, :] = jnp.zeros((8, D), F32)
        _fill_rotations(rot, win, tt + HALO)

        def chunk(c, carry):
            r0 = pl.multiple_of(c * ROWS, ROWS)
            acc = jnp.zeros((ROWS, D), F32) + bdw_ref[...]
            for k in range(CW):
                acc += w_ref[k:k + 1, :] * _window(rot, HALO - (CW - 1) + k, r0, ROWS)
            u1_ref[pl.ds(r0, ROWS), :] = acc
            u2_ref[pl.ds(r0, ROWS), :] = _ln_silu(acc, g_ref[...], b_ref[...]).astype(BF16)
            return carry

        lax.fori_loop(0, tt // ROWS, chunk, 0)

    row = lambda b, i: (b * nt + i, 0)
    vec = pl.BlockSpec((1, D), lambda b, i: (0, 0))
    return _pcall(
        body, name="cm_mid_fwd", grid=(BL, nt),
        in_specs=[pl.BlockSpec((tt, 2 * D), row), _prev_halo_spec(2 * D, tt, HALO),
                  pl.BlockSpec((HALO, D), lambda b, i: (0, 0)), vec, vec, vec],
        out_specs=[pl.BlockSpec((tt, D), row), pl.BlockSpec((tt, D), row)],
        out_shape=[_sds((n, D), F32), _sds((n, D), BF16)],
        scratch_shapes=[pltpu.VMEM((HALO + tt + 8, D), F32), pltpu.VMEM((8, tt + HALO, D), F32)],
        compiler_params=_cp(("arbitrary", "arbitrary")),
    )(ab, ab, w_dw, b_dw, ln_g, ln_b)


def _cm_mid_bwd_a(du2, u1, ln_g, ln_b):
    n = du2.shape[0]
    tm = min(256, T)

    def body(du2_ref, u1_ref, g_ref, b_ref, du1_ref, dln_ref):
        _, vjp = jax.vjp(_ln_silu, u1_ref[...], g_ref[...], b_ref[...])
        du1, dg, db = vjp(du2_ref[...])
        du1_ref[...] = du1
        _acc_rows(dln_ref, pl.program_id(0) == 0, [dg, db])

    vec = pl.BlockSpec((1, D), lambda i: (0, 0))
    return _pcall(
        body, name="cm_mid_bwd_a", grid=(n // tm,),
        in_specs=[pl.BlockSpec((tm, D), lambda i: (i, 0)), pl.BlockSpec((tm, D), lambda i: (i, 0)), vec, vec],
        out_specs=[pl.BlockSpec((tm, D), lambda i: (i, 0)), pl.BlockSpec((2, D), lambda i: (0, 0))],
        out_shape=[_sds((n, D), F32), _sds((2, D), F32)],
        compiler_params=_cp(("arbitrary",)),
    )(du2, u1, ln_g, ln_b)


def _cm_mid_bwd_b(du1, ab, w_dw):
    n = du1.shape[0]
    tt, nt = _conv_tiles()

    def body(du1_ref, nxt_ref, ab_ref, halo_ref, w_ref, dab_ref, dw_ref, dbdw_ref, dbglu_ref,
             dwin, uwin, rotd, rotu, accw, accv):
        b, i = pl.program_id(0), pl.program_id(1)
        first = jnp.logical_and(b == 0, i == 0)
        dwin[0:tt, :] = du1_ref[...]
        dwin[tt:tt + HALO, :] = jnp.where(i == nt - 1, 0.0, nxt_ref[...])
        dwin[tt + HALO:, :] = jnp.zeros((8, D), F32)
        hv = halo_ref[...]
        uwin[0:HALO, :] = jnp.where(i == 0, 0.0, hv[:, :D] * jax.nn.sigmoid(hv[:, D:]))
        cv = ab_ref[...]
        uwin[HALO:HALO + tt, :] = cv[:, :D] * jax.nn.sigmoid(cv[:, D:])
        uwin[HALO + tt:, :] = jnp.zeros((8, D), F32)
        _fill_rotations(rotd, dwin, tt + HALO)
        _fill_rotations(rotu, uwin, tt + HALO)
        accw[...] = jnp.zeros_like(accw)
        accv[...] = jnp.zeros_like(accv)

        def fold(v):
            return jnp.sum(v.reshape(ROWS // 8, 8, D), axis=0)

        def chunk(c, carry):
            r0 = pl.multiple_of(c * ROWS, ROWS)
            d1 = du1_ref[pl.ds(r0, ROWS), :]
            du0 = jnp.zeros((ROWS, D), F32)
            for k in range(CW):
                du0 += w_ref[k:k + 1, :] * _window(rotd, CW - 1 - k, r0, ROWS)
                accw[k] += fold(d1 * _window(rotu, HALO - (CW - 1) + k, r0, ROWS))
            cvc = ab_ref[pl.ds(r0, ROWS), :]
            av, sg = cvc[:, :D], jax.nn.sigmoid(cvc[:, D:])
            da = du0 * sg
            db = du0 * av * sg * (1.0 - sg)
            dab_ref[pl.ds(r0, ROWS), 0:D] = da.astype(BF16)
            dab_ref[pl.ds(r0, ROWS), D:2 * D] = db.astype(BF16)
            accv[0] += fold(d1)
            accv[1] += fold(da)
            accv[2] += fold(db)
            return carry

        lax.fori_loop(0, tt // ROWS, chunk, 0)
        dws = [jnp.sum(accw[k], axis=0, keepdims=True) for k in range(CW)]
        dws += [jnp.zeros((1, D), F32)] * (HALO - CW)
        _acc_rows(dw_ref, first, dws)
        _acc_rows(dbdw_ref, first, [jnp.sum(accv[0], axis=0, keepdims=True)])
        _acc_rows(dbglu_ref.at[:, 0:D], first, [jnp.sum(accv[1], axis=0, keepdims=True)])
        _acc_rows(dbglu_ref.at[:, D:2 * D], first, [jnp.sum(accv[2], axis=0, keepdims=True)])

    row = lambda b, i: (b * nt + i, 0)
    return _pcall(
        body, name="cm_mid_bwd_b", grid=(BL, nt),
        in_specs=[pl.BlockSpec((tt, D), row), _next_halo_spec(D, tt, HALO),
                  pl.BlockSpec((tt, 2 * D), row), _prev_halo_spec(2 * D, tt, HALO),
                  pl.BlockSpec((HALO, D), lambda b, i: (0, 0))],
        out_specs=[pl.BlockSpec((tt, 2 * D), row), pl.BlockSpec((HALO, D), lambda b, i: (0, 0)),
                   pl.BlockSpec((1, D), lambda b, i: (0, 0)), pl.BlockSpec((1, 2 * D), lambda b, i: (0, 0))],
        out_shape=[_sds((n, 2 * D), BF16), _sds((HALO, D), F32), _sds((1, D), F32), _sds((1, 2 * D), F32)],
        scratch_shapes=[pltpu.VMEM((tt + HALO + 8, D), F32), pltpu.VMEM((HALO + tt + 8, D), F32),
                        pltpu.VMEM((8, tt + HALO, D), F32), pltpu.VMEM((8, tt + HALO, D), F32),
                        pltpu.VMEM((HALO, 8, D), F32), pltpu.VMEM((3, 8, D), F32)],
        compiler_params=_cp(("arbitrary", "arbitrary")),
    )(du1, du1, ab, ab, w_dw)


def _softplus(v):
    return jnp.maximum(v, 0.0) + jnp.log(1.0 + jnp.exp(-jnp.abs(v)))


def _g_beta(ab, alog, dtb):
    return -jnp.exp(alog) * _softplus(ab + dtb), jax.nn.sigmoid(ab)


def _dn_sconv_fwd(proj, w_sc, alog, dtb):
    n = proj.shape[0]
    tt, nt = _conv_tiles()
    w3 = 3 * D

    def body(qkv_ref, halo_ref, ab_ref, w_ref, alog_ref, dtb_ref, conv_ref, q_ref, k_ref, v_ref, gb_ref, bb_ref,
             win, rot, gsc, bsc):
        i = pl.program_id(1)
        win[0:SHALO, :] = jnp.where(i == 0, 0.0, halo_ref[...])
        win[SHALO:SHALO + tt, :] = qkv_ref[...]
        for k in range(SCW - 1):
            rot[k] = win[pl.ds(SHALO - (SCW - 1) + k, tt), :]
        gsc[...], bsc[...] = _g_beta(ab_ref[...], alog_ref[...], dtb_ref[...])

        def chunk(c, carry):
            rows = pl.ds(pl.multiple_of(c * SROWS, SROWS), SROWS)
            acc = w_ref[SCW - 1:SCW, :] * win[pl.ds(pl.multiple_of(c * SROWS + SHALO, SROWS), SROWS), :]
            for k in range(SCW - 1):
                acc += w_ref[k:k + 1, :] * rot[k, rows, :]
            conv_ref[rows, :] = acc
            act = acc * jax.nn.sigmoid(acc)
            gfull, bfull = gsc[rows, :], bsc[rows, :]
            for h in range(NH):
                q_ref[0, h, rows, :] = act[:, h * DH:(h + 1) * DH]
                k_ref[0, h, rows, :] = act[:, D + h * DH:D + (h + 1) * DH]
                v_ref[0, h, rows, :] = act[:, 2 * D + h * DH:2 * D + (h + 1) * DH]
                gb_ref[0, h, rows, :] = jnp.broadcast_to(gfull[:, h:h + 1], (SROWS, DH))
                bb_ref[0, h, rows, :] = jnp.broadcast_to(bfull[:, NH + h:NH + h + 1], (SROWS, DH))
            return carry

        lax.fori_loop(0, tt // SROWS, chunk, 0)

    row = lambda b, i: (b * nt + i, 0)
    head = pl.BlockSpec((1, NH, tt, DH), lambda b, i: (b, 0, i, 0))
    vec = pl.BlockSpec((1, 128), lambda b, i: (0, 0))
    hs = _sds((BL, NH, T, DH), F32)
    return _pcall(
        body, name="dn_sconv_fwd", grid=(BL, nt),
        in_specs=[pl.BlockSpec((tt, w3), row), _prev_halo_spec(w3, tt, SHALO),
                  pl.BlockSpec((tt, 128), lambda b, i: (b * nt + i, 4 * D // 128)),
                  pl.BlockSpec((SHALO, w3), lambda b, i: (0, 0)), vec, vec],
        out_specs=[pl.BlockSpec((tt, w3), row), head, head, head, head, head],
        out_shape=[_sds((n, w3), F32), hs, hs, hs, hs, hs],
        scratch_shapes=[pltpu.VMEM((SHALO + tt, w3), F32), pltpu.VMEM((SCW - 1, tt, w3), F32),
                        pltpu.VMEM((tt, 128), F32), pltpu.VMEM((tt, 128), F32)],
        compiler_params=_cp(("arbitrary", "arbitrary")),
    )(proj, proj, proj, w_sc, alog, dtb)


_BMM_SPEC = {"nn": "gij,gjk->gik", "nt": "gid,gjd->gij", "tn": "gcd,gce->gde"}


def _mm(kind, a, b, prec):
    if prec is None:
        return jnp.einsum(_BMM_SPEC[kind], a.astype(BF16), b.astype(BF16), preferred_element_type=F32)
    return jnp.einsum(_BMM_SPEC[kind], a, b, preferred_element_type=F32, precision=prec)


@functools.partial(jax.custom_vjp, nondiff_argnums=(0, 3))
def _bmm_k(kind, a, b, prec):
    return _mm(kind, a, b, prec)


def _bmm_k_fwd(kind, a, b, prec):
    return _mm(kind, a, b, prec), (a, b)


def _bmm_k_bwd(kind, prec, res, dc):
    a, b = res
    if kind == "nn":
        return _bmm_k("nt", dc, b, prec), _bmm_k("tn", a, dc, prec)
    if kind == "nt":
        return _bmm_k("nn", dc, b, prec), _bmm_k("tn", dc, a, prec)
    return _bmm_k("nt", b, dc, prec), _bmm_k("nn", a, dc, prec)


_bmm_k.defvjp(_bmm_k_fwd, _bmm_k_bwd)


def _bmm(a, b, prec=None):
    return _bmm_k("nn", a, b, prec)


def _bmm_nt(a, b, prec=None):
    return _bmm_k("nt", a, b, prec)


def _bmm_tn(a, b, prec=None):
    return _bmm_k("tn", a, b, prec)


def _bmm_raw(a, b):
    return _mm("nn", a, b, None)


def _bmm_nt_raw(a, b):
    return _mm("nt", a, b, None)


def _bmm_tn_raw(a, b):
    return _mm("tn", a, b, None)


@jax.custom_vjp
def _unit_lower_inverse(a):
    eye = (lax.broadcasted_iota(jnp.int32, a.shape, 1) == lax.broadcasted_iota(jnp.int32, a.shape, 2)).astype(F32)
    t = eye - a
    p = a
    for _ in range(CHUNK.bit_length() - 2):
        p = _mm("nn", p, p, INV_PREC)
        t = _mm("nn", t, eye + p, INV_PREC)
    return t


def _uli_fwd(a):
    t = _unit_lower_inverse(a)
    return t, t


def _uli_bwd(t, dt):
    return (-_bmm_nt(_bmm_tn(t, dt, lax.Precision.HIGH), t, lax.Precision.HIGH),)


_unit_lower_inverse.defvjp(_uli_fwd, _uli_bwd)


@jax.custom_vjp
def _known_inverse(a, t):
    return t


_known_inverse.defvjp(lambda a, t: (t, t), lambda t, dt: (_uli_bwd(t, dt)[0], jnp.zeros_like(t)))


def _dn_pre(q, k, v, gb, bb, tm_known=None):
    shape = (q.shape[0], CHUNK, CHUNK)
    ri = lax.broadcasted_iota(jnp.int32, shape, 1)
    ci = lax.broadcasted_iota(jnp.int32, shape, 2)
    causal, strict = ri >= ci, ri > ci
    qn = q * lax.rsqrt(jnp.sum(q * q, axis=-1, keepdims=True) + EPS) * (DH ** -0.5)
    kn = k * lax.rsqrt(jnp.sum(k * k, axis=-1, keepdims=True) + EPS)
    gcs = _bmm(causal.astype(F32), gb, HI)
    gcol = gcs[:, :, :CHUNK]
    decay = jnp.exp(jnp.where(causal, gcol - jnp.swapaxes(gcol, 1, 2), -jnp.inf))
    eg = jnp.exp(gcs)
    kb = kn * bb
    a = jnp.where(strict, _bmm_nt(kb, kn) * decay, 0.0)
    tm = _unit_lower_inverse(a) if tm_known is None else _known_inverse(a, tm_known)
    u = _bmm(tm, v * bb)
    w = _bmm(tm, kb * eg)
    qg = qn * eg
    intra = _bmm_nt(qn, kn) * decay
    glast = gcs[:, CHUNK - 1:CHUNK, :]
    kd = kn * jnp.exp(glast - gcs)
    egl = jnp.broadcast_to(jnp.exp(glast), (q.shape[0], 8, DH))
    return u, w, qg, kd, intra, egl, tm


def _pre_tiles():
    gcn = min(16, T // CHUNK)
    return gcn, T // (CHUNK * gcn)


def _dn_pre_specs():
    gcn, _ = _pre_tiles()
    tok = pl.BlockSpec((None, None, gcn * CHUNK, DH), lambda b, h, i: (b, h, i, 0))
    sq = pl.BlockSpec((None, None, gcn * CHUNK, CHUNK), lambda b, h, i: (b, h, i, 0))
    per = pl.BlockSpec((None, None, gcn * 8, DH), lambda b, h, i: (b, h, i, 0))
    return tok, sq, per


def _dn_pre_fwd(q, k, v, gb, bb):
    gcn, ng = _pre_tiles()
    tok, sq, per = _dn_pre_specs()

    def body(q_ref, k_ref, v_ref, gb_ref, bb_ref, u_ref, w_ref, qg_ref, kd_ref, in_ref, egl_ref, tinv_ref):
        args = [r[...].reshape(gcn, CHUNK, DH) for r in (q_ref, k_ref, v_ref, gb_ref, bb_ref)]
        u, w, qg, kd, intra, egl, tinv = _dn_pre(*args)
        for r, val in ((u_ref, u), (w_ref, w), (qg_ref, qg), (kd_ref, kd)):
            r[...] = val.reshape(gcn * CHUNK, DH)
        in_ref[...] = intra.reshape(gcn * CHUNK, CHUNK)
        tinv_ref[...] = tinv.reshape(gcn * CHUNK, CHUNK)
        egl_ref[...] = egl.reshape(gcn * 8, DH)

    hs = _sds((BL, NH, T, DH), F32)
    sqs = _sds((BL, NH, T, CHUNK), F32)
    return _pcall(
        body, name="dn_pre_fwd", grid=(BL, NH, ng),
        in_specs=[tok] * 5, out_specs=[tok, tok, tok, tok, sq, per, sq],
        out_shape=[hs, hs, hs, hs, sqs, _sds((BL, NH, T // CHUNK * 8, DH), F32), sqs],
        compiler_params=_cp(("arbitrary",) * 3),
    )(q, k, v, gb, bb)


def _dn_pre_bwd(q, k, v, gb, bb, tinv, du, dw, dqg, dkd, dintra, degl):
    gcn, ng = _pre_tiles()
    tok, sq, per = _dn_pre_specs()

    def body(q_ref, k_ref, v_ref, gb_ref, bb_ref, tinv_ref, du_ref, dw_ref, dqg_ref, dkd_ref, din_ref, degl_ref,
             dq_ref, dk_ref, dv_ref, dgb_ref, dbb_ref):
        args = [r[...].reshape(gcn, CHUNK, DH) for r in (q_ref, k_ref, v_ref, gb_ref, bb_ref)]
        known = tinv_ref[...].reshape(gcn, CHUNK, CHUNK)
        _, vjp = jax.vjp(lambda *a: _dn_pre(*a, tm_known=known)[:6], *args)
        cts = [r[...].reshape(gcn, CHUNK, DH) for r in (du_ref, dw_ref, dqg_ref, dkd_ref)]
        de = degl_ref[...].reshape(gcn, 8, DH)
        one = jnp.logical_and(lax.broadcasted_iota(jnp.int32, de.shape, 1) == 0,
                              lax.broadcasted_iota(jnp.int32, de.shape, 2) == 0)
        outs = vjp((*cts, din_ref[...].reshape(gcn, CHUNK, CHUNK), jnp.where(one, de, 0.0)))
        for r, val in zip((dq_ref, dk_ref, dv_ref, dgb_ref, dbb_ref), outs):
            r[...] = val.reshape(gcn * CHUNK, DH)

    hs = _sds((BL, NH, T, DH), F32)
    return _pcall(
        body, name="dn_pre_bwd", grid=(BL, NH, ng),
        in_specs=[tok] * 5 + [sq] + [tok] * 4 + [sq, per], out_specs=[tok] * 5, out_shape=[hs] * 5,
        compiler_params=_cp(("arbitrary",) * 3),
    )(q, k, v, gb, bb, tinv, du, dw, dqg, dkd, dintra, degl)


def _scan_tiles():
    cs = min(2, T // CHUNK)
    return cs, T // (CHUNK * cs)


def _dn_scan_fwd(u, w, qg, kd, intra, egl):
    cs, ns = _scan_tiles()
    g = BL * NH
    nc = T // CHUNK

    def body(u_ref, w_ref, qg_ref, kd_ref, in_ref, egl_ref, o_ref, vn_ref, s0_ref, s_scr):
        @pl.when(pl.program_id(0) == 0)
        def _():
            s_scr[...] = jnp.zeros_like(s_scr)

        for c in range(cs):
            rows = pl.ds(c * CHUNK, CHUNK)
            s = s_scr[...]
            s0_ref[:, :, c] = s.reshape(BL, NH, DH, DH)

            def ld(r, m=DH):
                return r[:, :, rows, :].reshape(g, CHUNK, m)

            vn = ld(u_ref) - _bmm_raw(ld(w_ref), s)
            o = _bmm_raw(ld(qg_ref), s) + _bmm_raw(ld(in_ref, CHUNK), vn)
            e = egl_ref[:, :, pl.ds(c * 8, 1), :].reshape(g, 1, DH)
            s_scr[...] = s * e + _bmm_tn_raw(ld(kd_ref), vn)
            vn_ref[:, :, rows, :] = vn.reshape(BL, NH, CHUNK, DH)
            o_ref[:, :, rows, :] = o.reshape(BL, NH, CHUNK, DH)

    tok = pl.BlockSpec((BL, NH, cs * CHUNK, DH), lambda i: (0, 0, i, 0))
    hs = _sds((BL, NH, T, DH), F32)
    return _pcall(
        body, name="dn_scan_fwd", grid=(ns,),
        in_specs=[tok, tok, tok, tok, pl.BlockSpec((BL, NH, cs * CHUNK, CHUNK), lambda i: (0, 0, i, 0)),
                  pl.BlockSpec((BL, NH, cs * 8, DH), lambda i: (0, 0, i, 0))],
        out_specs=[tok, tok, pl.BlockSpec((BL, NH, cs, DH, DH), lambda i: (0, 0, i, 0, 0))],
        out_shape=[hs, hs, _sds((BL, NH, nc, DH, DH), F32)],
        scratch_shapes=[pltpu.VMEM((g, DH, DH), F32)],
        compiler_params=_cp(("arbitrary",)),
    )(u, w, qg, kd, intra, egl)


def _dn_scan_bwd(do, w, qg, kd, intra, egl, vn, s0):
    cs, ns = _scan_tiles()
    g = BL * NH
    nc = T // CHUNK

    def body(do_ref, w_ref, qg_ref, kd_ref, in_ref, egl_ref, vn_ref, s0_ref,
             du_ref, dw_ref, dqg_ref, dkd_ref, din_ref, degl_ref, ds_scr):
        @pl.when(pl.program_id(0) == 0)
        def _():
            ds_scr[...] = jnp.zeros_like(ds_scr)

        for c in reversed(range(cs)):
            rows = pl.ds(c * CHUNK, CHUNK)

            def ld(r, m=DH):
                return r[:, :, rows, :].reshape(g, CHUNK, m)

            def st(r, val, m=DH):
                r[:, :, rows, :] = val.reshape(BL, NH, CHUNK, m)

            s = s0_ref[:, :, c].reshape(g, DH, DH)
            ds = ds_scr[...]
            dov, vnv, kdv, wv, qgv, inv = ld(do_ref), ld(vn_ref), ld(kd_ref), ld(w_ref), ld(qg_ref), ld(in_ref, CHUNK)
            dv = _bmm_tn_raw(inv, dov) + _bmm_raw(kdv, ds)
            st(din_ref, _bmm_nt_raw(dov, vnv), CHUNK)
            st(dqg_ref, _bmm_nt_raw(dov, s))
            st(dkd_ref, _bmm_nt_raw(vnv, ds))
            st(du_ref, dv)
            st(dw_ref, -_bmm_nt_raw(dv, s))
            de = jnp.sum(jnp.sum(ds * s, axis=2, keepdims=True), axis=1, keepdims=True)
            degl_ref[:, :, pl.ds(c * 8, 8), :] = jnp.broadcast_to(de, (g, 8, DH)).reshape(BL, NH, 8, DH)
            e = egl_ref[:, :, pl.ds(c * 8, 1), :].reshape(g, 1, DH)
            ds_scr[...] = ds * e + _bmm_tn_raw(qgv, dov) - _bmm_tn_raw(wv, dv)

    rev = lambda i: (0, 0, ns - 1 - i, 0)
    tok = pl.BlockSpec((BL, NH, cs * CHUNK, DH), rev)
    sq = pl.BlockSpec((BL, NH, cs * CHUNK, CHUNK), rev)
    per = pl.BlockSpec((BL, NH, cs * 8, DH), rev)
    hs = _sds((BL, NH, T, DH), F32)
    return _pcall(
        body, name="dn_scan_bwd", grid=(ns,),
        in_specs=[tok, tok, tok, tok, sq, per, tok,
                  pl.BlockSpec((BL, NH, cs, DH, DH), lambda i: (0, 0, ns - 1 - i, 0, 0))],
        out_specs=[tok, tok, tok, tok, sq, per],
        out_shape=[hs, hs, hs, hs, _sds((BL, NH, T, CHUNK), F32), _sds((BL, NH, nc * 8, DH), F32)],
        scratch_shapes=[pltpu.VMEM((g, DH, DH), F32)],
        compiler_params=_cp(("arbitrary",)),
    )(do, w, qg, kd, intra, egl, vn, s0)


def _gated_norm(o_h, z_h, og):
    r = lax.rsqrt(jnp.mean(o_h * o_h, axis=-1, keepdims=True) + EPS)
    return (o_h * r * og) * (z_h * jax.nn.sigmoid(z_h))


def _dn_gnorm_fwd(o, proj, o_g):
    tm = min(256, T)
    nt = T // tm

    def body(o_ref, z_ref, g_ref, og_ref):
        z = z_ref[...]
        for h in range(NH):
            og_ref[:, h * DH:(h + 1) * DH] = _gated_norm(o_ref[0, h], z[:, h * DH:(h + 1) * DH], g_ref[...]).astype(BF16)

    return _pcall(
        body, name="dn_gnorm_fwd", grid=(BL, nt),
        in_specs=[pl.BlockSpec((1, NH, tm, DH), lambda b, i: (b, 0, i, 0)),
                  pl.BlockSpec((tm, D), lambda b, i: (b * nt + i, 3)),
                  pl.BlockSpec((1, DH), lambda b, i: (0, 0))],
        out_specs=pl.BlockSpec((tm, D), lambda b, i: (b * nt + i, 0)),
        out_shape=_sds((BL * T, D), BF16),
        compiler_params=_cp(("arbitrary", "arbitrary")),
    )(o, proj, o_g)


def _dn_gnorm_bwd(dog, o, proj, o_g):
    tm = min(256, T)
    nt = T // tm

    def body(dog_ref, o_ref, z_ref, g_ref, do_ref, dz_ref, dg_ref):
        z = z_ref[...]
        dog = dog_ref[...]
        dg = jnp.zeros((1, DH), F32)
        for h in range(NH):
            cols = slice(h * DH, (h + 1) * DH)
            _, vjp = jax.vjp(_gated_norm, o_ref[0, h], z[:, cols], g_ref[...])
            do_h, dz_h, dg_h = vjp(dog[:, cols])
            do_ref[0, h] = do_h
            dz_ref[:, cols] = dz_h.astype(BF16)
            dg += dg_h
        _acc_rows(dg_ref, jnp.logical_and(pl.program_id(0) == 0, pl.program_id(1) == 0), [dg])

    return _pcall(
        body, name="dn_gnorm_bwd", grid=(BL, nt),
        in_specs=[pl.BlockSpec((tm, D), lambda b, i: (b * nt + i, 0)),
                  pl.BlockSpec((1, NH, tm, DH), lambda b, i: (b, 0, i, 0)),
                  pl.BlockSpec((tm, D), lambda b, i: (b * nt + i, 3)),
                  pl.BlockSpec((1, DH), lambda b, i: (0, 0))],
        out_specs=[pl.BlockSpec((1, NH, tm, DH), lambda b, i: (b, 0, i, 0)),
                   pl.BlockSpec((tm, D), lambda b, i: (b * nt + i, 0)),
                   pl.BlockSpec((1, DH), lambda b, i: (0, 0))],
        out_shape=[_sds((BL, NH, T, DH), F32), _sds((BL * T, D), BF16), _sds((1, DH), F32)],
        compiler_params=_cp(("arbitrary", "arbitrary")),
    )(dog, o, proj, o_g)


def _dn_prep_bwd(dq, dk, dv, dgb, dbb, conv, proj, alog, dtb):
    n = conv.shape[0]
    tt, nt = _conv_tiles()
    w3 = 3 * D

    def body(dq_ref, dk_ref, dv_ref, dgb_ref, dbb_ref, conv_ref, ab_ref, alog_ref, dtb_ref, dconv_ref, dab_ref, dhead_ref):
        cv = conv_ref[...]
        sg = jax.nn.sigmoid(cv)
        dact = sg * (1.0 + cv * (1.0 - sg))
        lane = lax.broadcasted_iota(jnp.int32, (tt, 128), 1)
        cg = jnp.zeros((tt, 128), F32)
        cb = jnp.zeros((tt, 128), F32)
        for h in range(NH):
            cols = slice(h * DH, (h + 1) * DH)
            dconv_ref[:, h * DH:(h + 1) * DH] = dq_ref[0, h] * dact[:, cols]
            dconv_ref[:, D + h * DH:D + (h + 1) * DH] = dk_ref[0, h] * dact[:, D + h * DH:D + (h + 1) * DH]
            dconv_ref[:, 2 * D + h * DH:2 * D + (h + 1) * DH] = dv_ref[0, h] * dact[:, 2 * D + h * DH:2 * D + (h + 1) * DH]
            cg = jnp.where(lane == h, jnp.sum(dgb_ref[0, h], axis=-1, keepdims=True), cg)
            cb = jnp.where(lane == NH + h, jnp.sum(dbb_ref[0, h], axis=-1, keepdims=True), cb)
        _, vjp = jax.vjp(_g_beta, ab_ref[...], alog_ref[...], dtb_ref[...])
        dab, dalog, ddtb = vjp((cg, cb))
        dab_ref[...] = dab.astype(BF16)
        _acc_rows(dhead_ref, jnp.logical_and(pl.program_id(0) == 0, pl.program_id(1) == 0), [dalog, ddtb])

    row = lambda b, i: (b * nt + i, 0)
    head = pl.BlockSpec((1, NH, tt, DH), lambda b, i: (b, 0, i, 0))
    vec = pl.BlockSpec((1, 128), lambda b, i: (0, 0))
    return _pcall(
        body, name="dn_prep_bwd", grid=(BL, nt),
        in_specs=[head] * 5 + [pl.BlockSpec((tt, w3), row),
                               pl.BlockSpec((tt, 128), lambda b, i: (b * nt + i, 4 * D // 128)), vec, vec],
        out_specs=[pl.BlockSpec((tt, w3), row), pl.BlockSpec((tt, 128), row), pl.BlockSpec((2, 128), lambda b, i: (0, 0))],
        out_shape=[_sds((n, w3), F32), _sds((n, 128), BF16), _sds((2, 128), F32)],
        compiler_params=_cp(("arbitrary", "arbitrary")),
    )(dq, dk, dv, dgb, dbb, conv, proj, alog, dtb)


def _dn_sconv_bwd(dconv, proj, w_sc):
    n = dconv.shape[0]
    tt, nt = _conv_tiles()
    w3 = 3 * D

    def body(dc_ref, nxt_ref, qkv_ref, halo_ref, w_ref, dpre_ref, dw_ref, dwin, pwin, rotd, rotp, dsc, accw):
        b, i = pl.program_id(0), pl.program_id(1)
        dwin[0:tt, :] = dc_ref[...]
        dwin[tt:tt + SHALO, :] = jnp.where(i == nt - 1, 0.0, nxt_ref[...])
        pwin[0:SHALO, :] = jnp.where(i == 0, 0.0, halo_ref[...])
        pwin[SHALO:SHALO + tt, :] = qkv_ref[...]
        for k in range(SCW - 1):
            rotd[k] = dwin[pl.ds(k + 1, tt), :]
            rotp[k] = pwin[pl.ds(SHALO - (SCW - 1) + k, tt), :]
        accw[...] = jnp.zeros_like(accw)

        def chunk(c, carry):
            r0 = pl.multiple_of(c * SROWS, SROWS)
            rows = pl.ds(r0, SROWS)
            dc = dc_ref[rows, :]
            dpre = w_ref[SCW - 1:SCW, :] * dc
            accw[SCW - 1] += dc * pwin[pl.ds(pl.multiple_of(r0 + SHALO, SROWS), SROWS), :]
            for k in range(SCW - 1):
                dpre += w_ref[k:k + 1, :] * rotd[SCW - 2 - k, rows, :]
                accw[k] += dc * rotp[k, rows, :]
            dsc[rows, :] = dpre
            return carry

        lax.fori_loop(0, tt // SROWS, chunk, 0)
        dpre_ref[...] = dsc[...].astype(BF16)
        dws = [jnp.sum(accw[k], axis=0, keepdims=True) for k in range(SCW)]
        dws += [jnp.zeros((1, w3), F32)] * (SHALO - SCW)
        _acc_rows(dw_ref, jnp.logical_and(b == 0, i == 0), dws)

    row = lambda b, i: (b * nt + i, 0)
    return _pcall(
        body, name="dn_sconv_bwd", grid=(BL, nt),
        in_specs=[pl.BlockSpec((tt, w3), row), _next_halo_spec(w3, tt, SHALO),
                  pl.BlockSpec((tt, w3), row), _prev_halo_spec(w3, tt, SHALO),
                  pl.BlockSpec((SHALO, w3), lambda b, i: (0, 0))],
        out_specs=[pl.BlockSpec((tt, w3), row), pl.BlockSpec((SHALO, w3), lambda b, i: (0, 0))],
        out_shape=[_sds((n, w3), BF16), _sds((SHALO, w3), F32)],
        scratch_shapes=[pltpu.VMEM((tt + SHALO, w3), F32), pltpu.VMEM((SHALO + tt, w3), F32),
                        pltpu.VMEM((SCW - 1, tt, w3), F32), pltpu.VMEM((SCW - 1, tt, w3), F32),
                        pltpu.VMEM((tt, w3), F32), pltpu.VMEM((SCW, SROWS, w3), F32)],
        compiler_params=_cp(("arbitrary", "arbitrary")),
    )(dconv, dconv, proj, proj, w_sc)


def _ada_fwd(c_all, w_ada, b_cols):
    nl, _, m = w_ada.shape
    nb = c_all.shape[0]

    def body(c_ref, w_ref, b_ref, o_ref):
        cv = c_ref[...]
        cs = (cv * jax.nn.sigmoid(cv)).astype(BF16)
        o_ref[...] = _dot(cs, w_ref[...].astype(BF16)) + b_ref[...]

    return _pcall(
        body, name="ada_fwd", grid=(nl,),
        in_specs=[pl.BlockSpec((nb, D), lambda l: (0, 0)), pl.BlockSpec((None, D, m), lambda l: (l, 0, 0)),
                  pl.BlockSpec((None, 1, m), lambda l: (l, 0, 0))],
        out_specs=pl.BlockSpec((None, nb, m), lambda l: (l, 0, 0)),
        out_shape=_sds((nl, nb, m), F32),
        compiler_params=_cp(("arbitrary",)),
    )(c_all, w_ada, b_cols)


def _ada_bwd(c_all, dmod_cols):
    nl, nb, m = dmod_cols.shape

    def body(c_ref, d_ref, o_ref):
        cv = c_ref[...]
        cs = (cv * jax.nn.sigmoid(cv)).astype(BF16)
        o_ref[0] = _dot_tn(cs, d_ref[...].astype(BF16))

    return _pcall(
        body, name="ada_bwd", grid=(nl,),
        in_specs=[pl.BlockSpec((nb, D), lambda l: (0, 0)), pl.BlockSpec((None, nb, m), lambda l: (l, 0, 0))],
        out_specs=pl.BlockSpec((1, D, m), lambda l: (0, l, 0)),
        out_shape=_sds((1, nl * D, m), F32),
        compiler_params=_cp(("arbitrary",)),
    )(c_all, dmod_cols)


def _loss_head(x, tgt, fg):
    n = x.shape[0]
    tm = min(512, T)

    def f(xv, g, t):
        r = lax.rsqrt(jnp.mean(xv * xv, axis=-1, keepdims=True) + EPS)
        e = xv * r * g - t
        return 0.5 * jnp.sum(e * e, axis=0, keepdims=True) * (1.0 / D)

    def body(x_ref, t_ref, g_ref, dx_ref, st_ref):
        t = t_ref[...]
        lrow, vjp = jax.vjp(lambda xv, g: f(xv, g, t), x_ref[...], g_ref[...])
        dx, dg = vjp(jnp.ones_like(lrow))
        dx_ref[...] = dx
        _acc_rows(st_ref, pl.program_id(0) == 0, [dg, lrow])

    return _pcall(
        body, name="loss_head", grid=(n // tm,),
        in_specs=[pl.BlockSpec((tm, D), lambda i: (i, 0)), pl.BlockSpec((tm, D), lambda i: (i, 0)),
                  pl.BlockSpec((1, D), lambda i: (0, 0))],
        out_specs=[pl.BlockSpec((tm, D), lambda i: (i, 0)), pl.BlockSpec((2, D), lambda i: (0, 0))],
        out_shape=[_sds((n, D), F32), _sds((2, D), F32)],
        compiler_params=_cp(("arbitrary",)),
    )(x, tgt, fg)


def _adamw(parts, w, m, v):
    p, r, c = parts.shape
    tr = r
    for cand in (256, 128, 64, 32, 16, 8):
        if r % cand == 0:
            tr = cand
            break
    k1 = 1.0 - B1 ** STEP
    k2 = 1.0 - B2 ** STEP

    def body(p_ref, w_ref, m_ref, v_ref, g_ref, d_ref, nm_ref, nv_ref):
        g = p_ref[0].astype(F32)
        for q in range(1, p):
            g += p_ref[q].astype(F32)
        mn = B1 * m_ref[...] + (1.0 - B1) * g
        vn = B2 * v_ref[...] + (1.0 - B2) * (g * g)
        g_ref[...] = g
        nm_ref[...] = mn
        nv_ref[...] = vn
        d_ref[...] = -LR * ((mn / k1) / (jnp.sqrt(vn / k2) + AEPS) + WD * w_ref[...])

    blk = pl.BlockSpec((tr, c), lambda i: (i, 0))
    return _pcall(
        body, name="adamw", grid=(r // tr,),
        in_specs=[pl.BlockSpec((p, tr, c), lambda i: (0, i, 0)), blk, blk, blk],
        out_specs=[blk] * 4, out_shape=[_sds((r, c), F32)] * 4,
        compiler_params=_cp(("arbitrary",)),
    )(parts, w, m, v)


def _sum_parts(parts):
    p, r, c = parts.shape

    def body(p_ref, o_ref):
        acc = p_ref[0]
        for q in range(1, p):
            acc += p_ref[q]
        o_ref[...] = acc

    return _pcall(body, name="sum_parts", out_shape=_sds((r, c), F32))(parts)


def _adamw_slot(parts, w, m, v, outs, row0, col):
    p, r, c = parts.shape
    tr = r
    for cand in (256, 128, 64, 32, 16, 8):
        if r % cand == 0:
            tr = cand
            break
    if r % 352 == 0:
        tr = 352
    nt = r // tr
    k1 = 1.0 - B1 ** STEP
    k2 = 1.0 - B2 ** STEP

    def body(p_ref, w_ref, m_ref, v_ref, g0, d0, m0, v0, g_ref, d_ref, nm_ref, nv_ref):
        g = p_ref[0].astype(F32)
        for q in range(1, p):
            g += p_ref[q].astype(F32)
        mn = B1 * m_ref[...] + (1.0 - B1) * g
        vn = B2 * v_ref[...] + (1.0 - B2) * (g * g)
        g_ref[...] = g
        nm_ref[...] = mn
        nv_ref[...] = vn
        d_ref[...] = -LR * ((mn / k1) / (jnp.sqrt(vn / k2) + AEPS) + WD * w_ref[...])

    blk = pl.BlockSpec((tr, c), lambda i: (row0 * nt + i, col))
    anyspec = pl.BlockSpec(memory_space=pl.ANY)
    return _pcall(
        body, name="adamw_slot", grid=(nt,),
        in_specs=[pl.BlockSpec((p, tr, c), lambda i: (0, i, 0)), blk, blk, blk] + [anyspec] * 4,
        out_specs=[blk] * 4, out_shape=[_sds(w.shape, F32)] * 4,
        input_output_aliases={4: 0, 5: 1, 6: 2, 7: 3},
        compiler_params=_cp(("arbitrary",)),
    )(parts, w, m, v, *outs)


def _pack(arrs):
    flat = jnp.concatenate([a.reshape(-1) for a in arrs])
    pad = (-flat.shape[0]) % 1024
    return jnp.pad(flat, (0, pad)).reshape(-1, 128)


def _unpack(buf, shapes):
    flat = buf.reshape(-1)
    out, off = [], 0
    for s in shapes:
        size = 1
        for d in s:
            size *= d
        out.append(flat[off:off + size].reshape(s))
        off += size
    return out


def kernel(x, c, norm_g, w_ada, b_ada, w_ffn_in, w_ffn_out, cm_w_glu, cm_b_glu, cm_w_dw, cm_b_dw, cm_ln_g, cm_ln_b, cm_w_pw, cm_b_pw, dn_w_in, dn_w_sconv, dn_a_log, dn_dt_bias, dn_o_g, dn_w_out, final_g, loss_target, m_norm_g, m_w_ada, m_b_ada, m_w_ffn_in, m_w_ffn_out, m_cm_w_glu, m_cm_b_glu, m_cm_w_dw, m_cm_b_dw, m_cm_ln_g, m_cm_ln_b, m_cm_w_pw, m_cm_b_pw, m_dn_w_in, m_dn_w_sconv, m_dn_a_log, m_dn_dt_bias, m_dn_o_g, m_dn_w_out, m_final_g, v_norm_g, v_w_ada, v_b_ada, v_w_ffn_in, v_w_ffn_out, v_cm_w_glu, v_cm_b_glu, v_cm_w_dw, v_cm_b_dw, v_cm_ln_g, v_cm_ln_b, v_cm_w_pw, v_cm_b_pw, v_dn_w_in, v_dn_w_sconv, v_dn_a_log, v_dn_dt_bias, v_dn_o_g, v_dn_w_out, v_final_g):
    me = 4 * lax.axis_index("x") + 2 * lax.axis_index("y") + lax.axis_index("c")
    n = BL * T
    nf = 4
    tf = FF // nf
    na, nb = cm_w_glu.shape[0], dn_w_in.shape[0]
    mcols = w_ada.shape[2]
    dsh = D // NDEV

    tr_ffn = lambda a: jnp.swapaxes(a, 2, 3)
    tr_dn = lambda a: jnp.transpose(a, (2, 0, 1))
    wt_ffn_in, wt_dn_in = tr_ffn(w_ffn_in), tr_dn(dn_w_in)

    def unit_weights(l, part):
        if part == 0:
            ws = (wt_ffn_in[l, 0], w_ffn_out[l, 0])
        else:
            mix = (cm_w_glu[l // 2], cm_w_pw[l // 2]) if l % 2 == 0 else (wt_dn_in[:, l // 2], dn_w_out[l // 2])
            ws = (wt_ffn_in[l, 1], w_ffn_out[l, 1], *mix)
        return [w.astype(BF16) for w in ws]

    gathers, all_started = {}, jnp.zeros((8, 128), F32)
    for l in range(DEPTH):
        for part in range(2):
            if l < RELAY_LAYERS:
                gathers[l, part], tok = _relay_gather_start(unit_weights(l, part), f"gather_start_{l}_{part}")
            else:
                gathers[l, part], tok = _exchange_start(unit_weights(l, part), True, f"gather_start_{l}_{part}")
            all_started = all_started + tok

    c_g, ng_g, dw_g, sc_g = _all_gather([c, norm_g, cm_w_dw, dn_w_sconv], "gather_small")
    whole = lambda g: jnp.moveaxis(g, 0, -2).reshape(*g.shape[1:-1], -1)
    c_all = c_g.reshape(NDEV * BL, D)
    norm_g_f, w_dw_f, w_sc_f = whole(ng_g), whole(dw_g), whole(sc_g)

    b_cols = lax.dynamic_slice_in_dim(b_ada, me * mcols, mcols, axis=1)[:, None, :]
    mod_cols = _ada_fwd(c_all, w_ada, b_cols)
    mod_g, = _all_gather([mod_cols], "gather_mod")
    mod_all = jnp.transpose(mod_g, (1, 2, 0, 3)).reshape(DEPTH, NDEV * BL, 9 * D)
    mod = lax.dynamic_slice_in_dim(mod_all, me * BL, BL, axis=1).reshape(DEPTH, BL, 3, 3, D)

    gathered = [None] * DEPTH

    def gather_wait(l, part, after):
        if l < RELAY_LAYERS:
            passed = _relay_gather_pass(gathers[l, part], after, f"gather_pass_{l}_{part}")
            return _exchange_wait(passed, after, f"gather_wait_{l}_{part}", blocks=3)
        return _exchange_wait(gathers[l, part], after, f"gather_wait_{l}_{part}")

    def ffn_weights(l, s):
        return gathered[l][s].reshape(2, nf, tf, D), gathered[l][2 + s].reshape(nf, tf, D)

    xs = x.reshape(n, D)
    saved = []
    for l in range(DEPTH):
        rec = {}
        ga = gather_wait(l, 0, all_started if l == 0 else xs)
        gathered[l] = [ga[0], None, ga[1], None, None, None]
        for s, j in ((0, 0), (1, 2)):
            if j == 2:
                gb = gather_wait(l, 1, xs)
                gathered[l] = [ga[0], gb[0], ga[1], gb[1], gb[2], gb[3]]
            w_in, w_out = ffn_weights(l, s)
            ssg, ng = mod[l, :, j], norm_g_f[l, j][None]
            if j == 2:
                ssg1, ng1 = mod[l, :, 1], norm_g_f[l, 1][None]
                if l % 2 == 0:
                    a = l // 2
                    w_glu = gathered[l][4]
                    w_pw = gathered[l][5].reshape(D, D)
                    w_dw = jnp.pad(w_dw_f[a], ((0, HALO - CW), (0, 0)))
                    h1, ab = _premod_matmul(xs, ssg1, ng1, w_glu, cm_b_glu[a][None], w_glu.shape[2])
                    u1, u2 = _cm_mid_fwd(ab, w_dw, cm_b_dw[a][None], cm_ln_g[a][None], cm_ln_b[a][None])
                    xn, ymix = _matmul_res(xs, u2, ssg1, w_pw, cm_b_pw[a][None])
                    rec["mix"] = dict(x=xs, h=h1, ab=ab, u1=u1, u2=u2, y=ymix, w_glu=w_glu, w_pw=w_pw, w_dw=w_dw)
                else:
                    mi = l // 2
                    w_proj = jnp.pad(gathered[l][4].reshape(4 * D + 2 * NH, D), ((0, 128 - 2 * NH), (0, 0)))
                    w_o = gathered[l][5].reshape(D, D)
                    w_sc = jnp.pad(w_sc_f[mi], ((0, SHALO - SCW), (0, 0)))
                    alog = jnp.pad(dn_a_log[mi], (0, 128 - NH))[None]
                    dtb = jnp.pad(dn_dt_bias[mi], (0, 128 - NH))[None]
                    h1, proj = _premod_matmul(xs, ssg1, ng1, w_proj, jnp.zeros((1, w_proj.shape[0]), F32),
                                              (4 * D + 128) // 3 if (4 * D + 128) % 384 == 0 else 128)
                    conv, q, k, v, gb, bb = _dn_sconv_fwd(proj, w_sc, alog, dtb)
                    u, w, qg, kd, intra, egl, tinv = _dn_pre_fwd(q, k, v, gb, bb)
                    o, vn, s0 = _dn_scan_fwd(u, w, qg, kd, intra, egl)
                    og = _dn_gnorm_fwd(o, proj, dn_o_g[mi][None])
                    xn, ymix = _matmul_res(xs, og, ssg1, w_o, jnp.zeros((1, D), F32))
                    rec["mix"] = dict(x=xs, h=h1, proj=proj, conv=conv, q=q, k=k, v=v, gb=gb, bb=bb, w=w, qg=qg, kd=kd,
                                      intra=intra, egl=egl, tinv=tinv, o=o, vn=vn, s0=s0, og=og, y=ymix, w_proj=w_proj, w_o=w_o,
                                      w_sc=w_sc, alog=alog, dtb=dtb)
                xs = xn
            xn, gu, hid, y = _ffn_fwd(xs, ssg, ng, w_in, w_out)
            rec[s] = dict(x=xs, gu=gu, hid=hid, y=y)
            xs = xn
        saved.append(rec)

    dx, stats = _loss_head(xs, loss_target.reshape(n, D), final_g[None])
    loss = lax.psum(jnp.sum(stats[1]), AXES)
    d_final_g = stats[0]

    d_mod = [[None] * 3 for _ in range(DEPTH)]
    d_norm = [[None] * 3 for _ in range(DEPTH)]
    dw_ffn_in = [[None] * 2 for _ in range(DEPTH)]
    dw_ffn_out = [[None] * 2 for _ in range(DEPTH)]
    dcm = [dict() for _ in range(na)]
    ddn = [dict() for _ in range(nb)]
    exchanges = {}

    def gather_small_grads():
        dmod_loc = jnp.stack([jnp.stack(d_mod[l], axis=1) for l in range(DEPTH)]).reshape(DEPTH, BL, 9 * D)
        small = [jnp.sum(dmod_loc, axis=1), jnp.stack([jnp.stack(d_norm[l]) for l in range(DEPTH)]),
                 jnp.stack([d["b_glu"] for d in dcm]), jnp.stack([d["w_dw"] for d in dcm]), jnp.stack([d["b_dw"] for d in dcm]),
                 jnp.stack([d["ln_g"] for d in dcm]), jnp.stack([d["ln_b"] for d in dcm]), jnp.stack([d["b_pw"] for d in dcm]),
                 jnp.stack([d["w_sconv"] for d in ddn]), jnp.stack([d["a_log"] for d in ddn]),
                 jnp.stack([d["dt_bias"] for d in ddn]), jnp.stack([d["o_g"] for d in ddn]), d_final_g]
        dmod_g, small_parts = _all_gather([dmod_loc, _pack(small)], "gather_small_grads")
        return dmod_g, small_parts, [a.shape for a in small]

    token = jnp.zeros((), F32)
    for l in reversed(range(DEPTH)):
        rec = saved[l]
        for s, j in ((1, 2), (0, 0)):
            w_in, w_out = ffn_weights(l, s)
            ssg, ng = mod[l, :, j] + token, norm_g_f[l, j][None]
            r = rec[s]
            dx, dgu, hb, dout, dssg, dng = _ffn_bwd_a(r["x"], dx, ssg, ng, r["y"], r["gu"], w_in, w_out)
            dw_ffn_in[l][s], dw_ffn_out[l][s] = _ffn_bwd_w(hb, dgu, r["hid"], dout)
            d_mod[l][j], d_norm[l][j] = dssg, dng[0]
            if j == 2:
                ssg1, ng1 = mod[l, :, 1], norm_g_f[l, 1][None]
                r = rec["mix"]
                if l % 2 == 0:
                    a = l // 2
                    du2, dy, dgate, db_pw = _matmul_res_bwd(dx, r["y"], ssg1, r["w_pw"])
                    du1, dln = _cm_mid_bwd_a(du2, r["u1"], cm_ln_g[a][None], cm_ln_b[a][None])
                    dab, dw_dw, db_dw, db_glu = _cm_mid_bwd_b(du1, r["ab"], r["w_dw"])
                    dx, dssg, dng = _premod_matmul_bwd(r["x"], dx, ssg1, ng1, [dab], r["w_glu"])
                    dcm[a] = dict(w_glu=_wgrad_shards(r["h"], dab, NDEV), w_pw=_wgrad(r["u2"], dy).reshape(NDEV, dsh, D),
                                  b_glu=db_glu[0], w_dw=dw_dw[:CW], b_dw=db_dw[0], ln_g=dln[0], ln_b=dln[1], b_pw=db_pw[0])
                else:
                    mi = l // 2
                    dog, dy, dgate, _ = _matmul_res_bwd(dx, r["y"], ssg1, r["w_o"])
                    do, dz, d_og = _dn_gnorm_bwd(dog, r["o"], r["proj"], dn_o_g[mi][None])
                    du, dw, dqg, dkd, dintra, degl = _dn_scan_bwd(do, r["w"], r["qg"], r["kd"], r["intra"], r["egl"],
                                                                   r["vn"], r["s0"])
                    dq, dk, dv, dgb, dbb = _dn_pre_bwd(r["q"], r["k"], r["v"], r["gb"], r["bb"], r["tinv"],
                                                       du, dw, dqg, dkd, dintra, degl)
                    dconv, dab16, dhead = _dn_prep_bwd(dq, dk, dv, dgb, dbb, r["conv"], r["proj"], r["alog"], r["dtb"])
                    dpre, dw_sc = _dn_sconv_bwd(dconv, r["proj"], r["w_sc"])
                    dx, dssg, dng = _premod_matmul_bwd(r["x"], dx, ssg1, ng1, [dpre, dz, dab16], r["w_proj"])
                    dw_in = jnp.concatenate([_wgrad(dpre, r["h"]), _wgrad(dz, r["h"]),
                                             _wgrad(dab16, r["h"])[:2 * NH]], axis=0)
                    ddn[mi] = dict(w_in=dw_in.reshape(NDEV, -1, D), w_out=_wgrad(r["og"], dy).reshape(NDEV, dsh, D),
                                   w_sconv=dw_sc[:SCW], a_log=dhead[0, :NH], dt_bias=dhead[1, :NH], o_g=d_og[0])
                d_mod[l][1] = dssg.at[:, 2].set(dgate[:, 0])
                d_norm[l][1] = dng[0]
            unit = [dw_ffn_in[l][s].reshape(NDEV, tf, D), dw_ffn_out[l][s].reshape(NDEV, FF // NDEV, D)]
            if j == 2:
                g = dcm[l // 2] if l % 2 == 0 else ddn[l // 2]
                unit += [g["w_glu"], g["w_pw"]] if l % 2 == 0 else [g["w_in"], g["w_out"]]
            if l == 0 and s == 0:
                dmod_g, small_parts, full_shapes = gather_small_grads()
                unit[0], dmod_g, small_parts = lax.optimization_barrier((unit[0], dmod_g, small_parts))
                small_gathered = (dmod_g, small_parts, full_shapes)
            exchanges[l, s], token = _exchange_start(unit, False, f"grads_start_{l}_{s}")
    grad_x = dx.reshape(BL, T, D)

    dmod_g, small_parts, full_shapes = small_gathered
    dmod_all = jnp.transpose(dmod_g, (1, 0, 2, 3)).reshape(DEPTH, NDEV * BL, 9 * D)
    g_w_ada = _ada_bwd(c_all, lax.dynamic_slice_in_dim(dmod_all, me * mcols, mcols, axis=2))

    got = []
    for l in range(DEPTH):
        ea = _exchange_wait(exchanges[l, 0], dx, f"grads_wait_{l}_0") if l > 0 else [None, None]
        eb = _exchange_wait(exchanges[l, 1], dx, f"grads_wait_{l}_1")
        got.append([ea[0], eb[0], ea[1], eb[1], eb[2], eb[3]])

    names = ["b_ada", "norm_g", "cm_b_glu", "cm_w_dw", "cm_b_dw", "cm_ln_g", "cm_ln_b", "cm_b_pw",
             "dn_w_sconv", "dn_a_log", "dn_dt_bias", "dn_o_g", "final_g"]
    cols = lambda a, width: lax.dynamic_slice_in_dim(a, me * width, width, axis=a.ndim - 1)
    local = {"norm_g": lambda a: cols(a, dsh), "cm_w_dw": lambda a: cols(a, dsh), "dn_w_sconv": lambda a: cols(a, 3 * dsh)}
    summed = _unpack(_sum_parts(small_parts), full_shapes)
    mine = [local.get(nm, lambda a: a)(p) for nm, p in zip(names, summed)]
    small_w = dict(b_ada=(b_ada, m_b_ada, v_b_ada), norm_g=(norm_g, m_norm_g, v_norm_g),
                   cm_b_glu=(cm_b_glu, m_cm_b_glu, v_cm_b_glu), cm_w_dw=(cm_w_dw, m_cm_w_dw, v_cm_w_dw),
                   cm_b_dw=(cm_b_dw, m_cm_b_dw, v_cm_b_dw), cm_ln_g=(cm_ln_g, m_cm_ln_g, v_cm_ln_g),
                   cm_ln_b=(cm_ln_b, m_cm_ln_b, v_cm_ln_b), cm_b_pw=(cm_b_pw, m_cm_b_pw, v_cm_b_pw),
                   dn_w_sconv=(dn_w_sconv, m_dn_w_sconv, v_dn_w_sconv), dn_a_log=(dn_a_log, m_dn_a_log, v_dn_a_log),
                   dn_dt_bias=(dn_dt_bias, m_dn_dt_bias, v_dn_dt_bias), dn_o_g=(dn_o_g, m_dn_o_g, v_dn_o_g),
                   final_g=(final_g, m_final_g, v_final_g))
    loc_shapes = [small_w[nm][0].shape for nm in names]
    sres_raw = _adamw(_pack(mine)[None], *[_pack([small_w[nm][q] for nm in names]) for q in range(3)])
    sres = [dict(zip(names, _unpack(r, loc_shapes))) for r in sres_raw]

    res = {}

    def update(slots, wmv, view, back=None, outs=None):
        w2, m2, v2 = [view(a) for a in wmv]
        outs = [lax.empty(w2.shape, F32) for _ in range(4)] if outs is None else outs
        for p, row0, col in slots:
            outs = _adamw_slot(p, w2, m2, v2, outs, row0, col)
        return outs if back is None else [back(o) for o in outs]

    ffn_slots = [(l, s) for l in reversed(range(DEPTH)) for s in (1, 0)][:-1]
    wmv_in, view_in = (w_ffn_in, m_w_ffn_in, v_w_ffn_in), lambda a: tr_ffn(a).reshape(-1, D)
    wmv_out, view_out = (w_ffn_out, m_w_ffn_out, v_w_ffn_out), lambda a: a.reshape(-1, D)
    part_in = update([(got[l][s], 2 * l + s, 0) for l, s in ffn_slots], wmv_in, view_in)
    part_out = update([(got[l][2 + s], 2 * l + s, 0) for l, s in ffn_slots], wmv_out, view_out)
    cgl = cm_w_glu.shape[2]
    res["cm_w_glu"] = update([(got[2 * a][4], a, 0) for a in range(na)], (cm_w_glu, m_cm_w_glu, v_cm_w_glu),
                             lambda a: a.reshape(-1, cgl), lambda o: o.reshape(cm_w_glu.shape))
    res["cm_w_pw"] = update([(got[2 * a][5], a, 0) for a in range(na)], (cm_w_pw, m_cm_w_pw, v_cm_w_pw),
                            lambda a: a.reshape(-1, D), lambda o: o.reshape(cm_w_pw.shape))
    cdn = dn_w_in.shape[2]
    res["dn_w_in"] = update([(got[2 * i + 1][4], 0, i) for i in range(nb)], (dn_w_in, m_dn_w_in, v_dn_w_in),
                            lambda a: tr_dn(a).reshape(cdn, nb * D),
                            lambda o: jnp.transpose(o.reshape(cdn, nb, D), (1, 2, 0)))
    res["dn_w_out"] = update([(got[2 * i + 1][5], i, 0) for i in range(nb)], (dn_w_out, m_dn_w_out, v_dn_w_out),
                             lambda a: a.reshape(-1, D), lambda o: o.reshape(dn_w_out.shape))
    res["w_ada"] = [o.reshape(w_ada.shape) for o in
                    _adamw(g_w_ada, *[a.reshape(-1, mcols) for a in (w_ada, m_w_ada, v_w_ada)])]
    done = [part_in[0], part_out[0], sres_raw[0]] + [res[nm][0] for nm in ("cm_w_glu", "cm_w_pw", "dn_w_in", "dn_w_out", "w_ada")]
    last = _exchange_wait(exchanges[0, 0], done, "grads_wait_0_0")
    res["w_ffn_in"] = update([(last[0], 0, 0)], wmv_in, view_in,
                             lambda o: jnp.swapaxes(o.reshape(DEPTH, 2, tf, D), 2, 3), part_in)
    res["w_ffn_out"] = update([(last[1], 0, 0)], wmv_out, view_out, lambda o: o.reshape(w_ffn_out.shape), part_out)
    for nm in names:
        res[nm] = [sres[q][nm] for q in range(4)]

    order = ["norm_g", "w_ada", "b_ada", "w_ffn_in", "w_ffn_out", "cm_w_glu", "cm_b_glu", "cm_w_dw", "cm_b_dw", "cm_ln_g",
             "cm_ln_b", "cm_w_pw", "cm_b_pw", "dn_w_in", "dn_w_sconv", "dn_a_log", "dn_dt_bias", "dn_o_g", "dn_w_out", "final_g"]
    return (loss, grad_x, *[res[nm][0] for nm in order], *[res[nm][1] for nm in order],
            *[res[nm][2] for nm in order], *[res[nm][3] for nm in order])
```

```python
import functools

import jax
import jax.numpy as jnp
from jax import lax
from jax.experimental import pallas as pl
from jax.experimental.pallas import tpu as pltpu

F32 = jnp.float32
BF16 = jnp.bfloat16
HI = lax.Precision.HIGHEST
INV_PREC = None
MESH = pl.DeviceIdType.MESH
AXES = ("x", "y", "c")

NDEV = 8
D = 1024
T = 2048
BL = 2
FF = 2816
NH = 8
DH = 128
CW = 31
SCW = 4
CHUNK = 64
DEPTH = 4
EPS = 1e-6
LR, B1, B2, AEPS, WD, STEP = 0.001, 0.9, 0.999, 1e-08, 0.01, 10

VMEM_LIMIT_BYTES = 56 * 1024 * 1024
RELAY_LAYERS = 2
HALO = 32
SHALO = 8


def _pcall(body, **kw):
    return pl.pallas_call(body, **kw)


def _cp(sem=None):
    return pltpu.CompilerParams(dimension_semantics=sem, vmem_limit_bytes=VMEM_LIMIT_BYTES)


def _sds(shape, dtype):
    return jax.ShapeDtypeStruct(tuple(shape), dtype)


def _dot(a, b):
    return jnp.dot(a, b, preferred_element_type=F32)


def _dot_nt(a, b):
    return lax.dot_general(a, b, (((1,), (1,)), ((), ())), preferred_element_type=F32)


def _dot_tn(a, b):
    return lax.dot_general(a, b, (((0,), (0,)), ((), ())), preferred_element_type=F32)


def _modulate(x, ng, scale, shift):
    r = lax.rsqrt(jnp.mean(x * x, axis=-1, keepdims=True) + EPS)
    return (x * r * ng) * (1.0 + scale) + shift


def _modulate_bwd(x, ng, scale, shift, dh):
    r = lax.rsqrt(jnp.mean(x * x, axis=-1, keepdims=True) + EPS)
    xh = x * r
    xg = xh * ng
    h = xg * (1.0 + scale) + shift
    a = dh * (1.0 + scale)
    dxh = a * ng
    dx = r * (dxh - xh * jnp.mean(dxh * xh, axis=-1, keepdims=True))
    return (h, dx, jnp.sum(a * xh, axis=0, keepdims=True), jnp.sum(dh * xg, axis=0, keepdims=True),
            jnp.sum(dh, axis=0, keepdims=True))


def _acc_rows(ref, first, rows):
    @pl.when(first)
    def _():
        ref[...] = jnp.zeros_like(ref)

    for r, val in enumerate(rows):
        ref[r:r + 1, :] += val


def _my_pos():
    return lax.axis_index("x"), lax.axis_index("y"), lax.axis_index("c")


def _all_gather(arrs, name):
    n = len(arrs)

    def body(*refs):
        ins, outs = refs[:n], refs[n:2 * n]
        send, recv, loc = refs[2 * n:]
        x, y, c = _my_pos()
        me, sibling = (x, y, c), (x, y, 1 - c)
        chips = [(1 - x, y), (x, 1 - y), (1 - x, 1 - y)]

        def copy(a, k, block, to, src=None):
            dst = outs[a].at[4 * block[0] + 2 * block[1] + block[2]]
            return pltpu.make_async_remote_copy(
                src_ref=dst if src is None else src, dst_ref=dst,
                send_sem=send.at[7 * a + k], recv_sem=recv.at[7 * a + k],
                device_id=to, device_id_type=MESH)

        mine, first, passed = [], [], []
        for a in range(n):
            m = pltpu.make_async_copy(ins[a], outs[a].at[4 * x + 2 * y + c], loc.at[a])
            m.start()
            mine.append(m)
            f = [copy(a, 0, me, sibling, src=ins[a])]
            f += [copy(a, 1 + j, me, (*chip, c), src=ins[a]) for j, chip in enumerate(chips)]
            for cp in f:
                cp.start()
            first += f
        for a in range(n):
            for j, chip in enumerate(chips):
                copy(a, 1 + j, (*chip, c), me).wait_recv()
                p = copy(a, 4 + j, (*chip, c), sibling)
                p.start()
                passed.append(p)
        for a in range(n):
            copy(a, 0, sibling, me).wait_recv()
            for j, chip in enumerate(chips):
                copy(a, 4 + j, (*chip, 1 - c), me).wait_recv()
        for cp in first + passed:
            cp.wait_send()
        for m in mine:
            m.wait()

    hbm = pl.BlockSpec(memory_space=pl.ANY)
    return _pcall(
        body, name=name,
        out_shape=[_sds((NDEV,) + a.shape, a.dtype) for a in arrs],
        in_specs=[hbm] * n, out_specs=[hbm] * n,
        scratch_shapes=[pltpu.SemaphoreType.DMA((7 * n,)), pltpu.SemaphoreType.DMA((7 * n,)),
                        pltpu.SemaphoreType.DMA((n,))],
    )(*arrs)


def _peer(k):
    x, y, c = _my_pos()
    return (1 - x if k & 4 else x, 1 - y if k & 2 else y, 1 - c if k & 1 else c)


def _dev_index(p):
    return 4 * p[0] + 2 * p[1] + p[2]


_HBM = pl.BlockSpec(memory_space=pltpu.HBM)
_SEM = pl.BlockSpec(memory_space=pltpu.SEMAPHORE)
_EFFECT = pltpu.SideEffectType.DATAFLOW_SIDE_EFFECTING


def _exchange_start(srcs, gather, name):
    n = len(srcs)
    me = _dev_index(_my_pos())
    lands = []
    for s in srcs:
        own = s if gather else lax.dynamic_index_in_dim(s, me, 0, keepdims=False)
        shape = (NDEV,) + s.shape if gather else s.shape
        lands.append(lax.dynamic_update_index_in_dim(lax.empty(shape, s.dtype), own, me, 0))

    def body(*refs):
        src_refs, land_refs = refs[:n], refs[n:2 * n]
        sends, recvs = refs[2 * n:3 * n], refs[3 * n:4 * n]
        token = refs[-1]
        mine = _dev_index(_my_pos())
        for a in range(n):
            for k in range(1, 8):
                p = _peer(k)
                pltpu.make_async_remote_copy(
                    src_ref=src_refs[a] if gather else src_refs[a].at[_dev_index(p)],
                    dst_ref=land_refs[a].at[mine], send_sem=sends[a], recv_sem=recvs[a],
                    device_id=p, device_id_type=MESH).start()
        token[...] = jnp.zeros_like(token)

    out = pl.pallas_call(
        body, name=name,
        out_shape=(*[pltpu.SemaphoreType.DMA(())] * (2 * n),
                   *[pltpu.HBM(a.shape, a.dtype) for a in srcs], *[pltpu.HBM(a.shape, a.dtype) for a in lands],
                   _sds((8, 128), F32)),
        in_specs=[_HBM] * (2 * n),
        out_specs=(*[_SEM] * (2 * n), *[_HBM] * (2 * n), pl.BlockSpec(memory_space=pltpu.VMEM)),
        input_output_aliases={i: 2 * n + i for i in range(2 * n)},
        compiler_params=pltpu.CompilerParams(has_side_effects=_EFFECT),
    )(*[pltpu.with_memory_space_constraint(a, pltpu.HBM) for a in srcs],
      *[pltpu.with_memory_space_constraint(a, pltpu.HBM) for a in lands])
    state = (out[:n], out[n:2 * n], out[2 * n:3 * n], out[3 * n:4 * n])
    return state, out[-1][0, 0]


def _exchange_wait(state, after, name, blocks=NDEV - 1):
    sends, recvs, srcs, lands = state
    n = len(srcs)
    after = list(after) if isinstance(after, (list, tuple)) else [after]

    def body(*refs):
        land_refs = refs[n:2 * n]
        send_refs, recv_refs = refs[2 * n:3 * n], refs[3 * n:4 * n]
        for a in range(n):
            seven = land_refs[a].at[pl.ds(0, blocks)]
            cp = pltpu.make_async_remote_copy(src_ref=seven, dst_ref=seven, send_sem=send_refs[a], recv_sem=recv_refs[a],
                                              device_id=_peer(1), device_id_type=MESH)
            cp.wait_send()
            cp.wait_recv()

    out = pl.pallas_call(
        body, name=name,
        out_shape=(*[pltpu.HBM(a.shape, a.dtype) for a in srcs], *[pltpu.HBM(a.shape, a.dtype) for a in lands]),
        in_specs=(*[_HBM] * (2 * n), *[_SEM] * (2 * n), *[pl.BlockSpec(memory_space=pl.ANY)] * len(after)),
        out_specs=tuple([_HBM] * (2 * n)),
        input_output_aliases={i: i for i in range(2 * n)},
        compiler_params=pltpu.CompilerParams(has_side_effects=_EFFECT),
    )(*srcs, *lands, *sends, *recvs, *after)
    return list(out[n:])


def _other_chips():
    x, y, _ = _my_pos()
    return [(1 - x, y), (x, 1 - y), (1 - x, 1 - y)]


def _relay_gather_start(srcs, name):
    n = len(srcs)
    me = _dev_index(_my_pos())
    lands = [lax.dynamic_update_index_in_dim(lax.empty((NDEV,) + s.shape, s.dtype), s, me, 0) for s in srcs]

    def body(*refs):
        src_refs, land_refs = refs[:n], refs[n:2 * n]
        sa, ra, sb, rb = refs[2 * n:3 * n], refs[3 * n:4 * n], refs[4 * n:5 * n], refs[5 * n:6 * n]
        token = refs[-1]
        x, y, c = _my_pos()
        mine = _dev_index((x, y, c))
        for a in range(n):
            pltpu.make_async_remote_copy(src_ref=src_refs[a], dst_ref=land_refs[a].at[mine], send_sem=sa[a], recv_sem=ra[a],
                                         device_id=(x, y, 1 - c), device_id_type=MESH).start()
            for chip in _other_chips():
                pltpu.make_async_remote_copy(src_ref=src_refs[a], dst_ref=land_refs[a].at[mine], send_sem=sb[a],
                                             recv_sem=rb[a], device_id=(*chip, c), device_id_type=MESH).start()
        token[...] = jnp.zeros_like(token)

    out = pl.pallas_call(
        body, name=name,
        out_shape=(*[pltpu.SemaphoreType.DMA(())] * (4 * n),
                   *[pltpu.HBM(a.shape, a.dtype) for a in srcs], *[pltpu.HBM(a.shape, a.dtype) for a in lands],
                   _sds((8, 128), F32)),
        in_specs=[_HBM] * (2 * n),
        out_specs=(*[_SEM] * (4 * n), *[_HBM] * (2 * n), pl.BlockSpec(memory_space=pltpu.VMEM)),
        input_output_aliases={i: 4 * n + i for i in range(2 * n)},
        compiler_params=pltpu.CompilerParams(has_side_effects=_EFFECT),
    )(*[pltpu.with_memory_space_constraint(a, pltpu.HBM) for a in srcs],
      *[pltpu.with_memory_space_constraint(a, pltpu.HBM) for a in lands])
    sems = [out[q * n:(q + 1) * n] for q in range(4)]
    return (*sems, out[4 * n:5 * n], out[5 * n:6 * n]), out[-1][0, 0]


def _relay_gather_pass(state, after, name):
    sa, ra, sb, rb, srcs, lands = state
    n = len(srcs)
    after = list(after) if isinstance(after, (list, tuple)) else [after]

    def body(*refs):
        land_refs = refs[n:2 * n]
        sa_r, ra_r, sb_r, rb_r = [refs[(2 + q) * n:(3 + q) * n] for q in range(4)]
        outs = refs[6 * n + len(after):]
        sc, rc = outs[2 * n:3 * n], outs[3 * n:4 * n]
        x, y, c = _my_pos()
        for a in range(n):
            one, three = land_refs[a].at[pl.ds(0, 1)], land_refs[a].at[pl.ds(0, 3)]
            for blocks, s_sem, r_sem in ((one, sa_r[a], ra_r[a]), (three, sb_r[a], rb_r[a])):
                cp = pltpu.make_async_remote_copy(src_ref=blocks, dst_ref=blocks, send_sem=s_sem, recv_sem=r_sem,
                                                  device_id=(x, y, 1 - c), device_id_type=MESH)
                cp.wait_send()
                cp.wait_recv()
            for chip in _other_chips():
                blk = land_refs[a].at[_dev_index((*chip, c))]
                pltpu.make_async_remote_copy(src_ref=blk, dst_ref=blk, send_sem=sc[a], recv_sem=rc[a],
                                             device_id=(x, y, 1 - c), device_id_type=MESH).start()

    out = pl.pallas_call(
        body, name=name,
        out_shape=(*[pltpu.HBM(a.shape, a.dtype) for a in srcs], *[pltpu.HBM(a.shape, a.dtype) for a in lands],
                   *[pltpu.SemaphoreType.DMA(())] * (2 * n)),
        in_specs=(*[_HBM] * (2 * n), *[_SEM] * (4 * n), *[pl.BlockSpec(memory_space=pl.ANY)] * len(after)),
        out_specs=(*[_HBM] * (2 * n), *[_SEM] * (2 * n)),
        input_output_aliases={i: i for i in range(2 * n)},
        compiler_params=pltpu.CompilerParams(has_side_effects=_EFFECT),
    )(*srcs, *lands, *sa, *ra, *sb, *rb, *after)
    return (out[2 * n:3 * n], out[3 * n:4 * n], out[:n], out[n:2 * n])


def _ffn_tiles():
    tm = min(512, T)
    return tm, T // tm


def _ffn_fwd(x, ssg, ng, w_in, w_out):
    n = x.shape[0]
    _, nf, tf, _ = w_in.shape
    tm, tpb = _ffn_tiles()

    def body(x_ref, ssg_ref, ng_ref, win_ref, wout_ref, xn_ref, gu_ref, hid_ref, y_ref, h_scr, acc):
        j = pl.program_id(1)

        @pl.when(j == 0)
        def _():
            s = ssg_ref[0]
            h_scr[...] = _modulate(x_ref[...], ng_ref[...], s[1:2], s[0:1]).astype(BF16)
            acc[...] = jnp.zeros_like(acc)

        h = h_scr[...]
        g = _dot_nt(h, win_ref[0])
        u = _dot_nt(h, win_ref[1])
        gu_ref[0] = g.astype(BF16)
        gu_ref[1] = u.astype(BF16)
        hid = (g * jax.nn.sigmoid(g) * u).astype(BF16)
        hid_ref[...] = hid
        acc[...] += _dot(hid, wout_ref[...])

        @pl.when(j == nf - 1)
        def _():
            yv = acc[...]
            y_ref[...] = yv.astype(BF16)
            xn_ref[...] = x_ref[...] + (0.5 * (1.0 + ssg_ref[0][2:3])) * yv

    return _pcall(
        body, name="ffn_fwd", grid=(n // tm, nf),
        in_specs=[pl.BlockSpec((tm, D), lambda i, j: (i, 0)),
                  pl.BlockSpec((1, 3, D), lambda i, j: (i // tpb, 0, 0)),
                  pl.BlockSpec((1, D), lambda i, j: (0, 0)),
                  pl.BlockSpec((2, None, tf, D), lambda i, j: (0, j, 0, 0)),
                  pl.BlockSpec((None, tf, D), lambda i, j: (j, 0, 0))],
        out_specs=[pl.BlockSpec((tm, D), lambda i, j: (i, 0)),
                   pl.BlockSpec((2, None, tm, tf), lambda i, j: (0, j, i, 0)),
                   pl.BlockSpec((None, tm, tf), lambda i, j: (j, i, 0)),
                   pl.BlockSpec((tm, D), lambda i, j: (i, 0))],
        out_shape=[_sds((n, D), F32), _sds((2, nf, n, tf), BF16), _sds((nf, n, tf), BF16), _sds((n, D), BF16)],
        scratch_shapes=[pltpu.VMEM((tm, D), BF16), pltpu.VMEM((tm, D), F32)],
        compiler_params=_cp(("arbitrary", "arbitrary")),
    )(x, ssg, ng, w_in, w_out)


def _ffn_bwd_a(x, dxn, ssg, ng, y, gu, w_in, w_out):
    n = x.shape[0]
    _, nf, tf, _ = w_in.shape
    tm, tpb = _ffn_tiles()

    def body(x_ref, dxn_ref, ssg_ref, ng_ref, y_ref, gu_ref, win_ref, wout_ref,
             dx_ref, dgu_ref, h_ref, dout_ref, dssg_ref, dng_ref, dout_scr, dh_acc):
        i, j = pl.program_id(0), pl.program_id(1)

        @pl.when(j == 0)
        def _():
            db = ((0.5 * (1.0 + ssg_ref[0][2:3])) * dxn_ref[...]).astype(BF16)
            dout_scr[...] = db
            dout_ref[...] = db
            dh_acc[...] = jnp.zeros_like(dh_acc)

        dhid = _dot_nt(dout_scr[...], wout_ref[...]).astype(BF16)
        g = gu_ref[0]
        u = gu_ref[1]
        sig = jax.nn.sigmoid(g)
        dg = dhid * u * (sig * (1.0 + g * (1.0 - sig)))
        du = dhid * (g * sig)
        dgu_ref[0] = dg
        dgu_ref[1] = du
        dh_acc[...] += _dot(dg, win_ref[0])
        dh_acc[...] += _dot(du, win_ref[1])

        @pl.when(j == nf - 1)
        def _():
            s = ssg_ref[0]
            h, dx_, dng_, dsc_, dsh_ = _modulate_bwd(x_ref[...], ng_ref[...], s[1:2], s[0:1], dh_acc[...])
            h_ref[...] = h.astype(BF16)
            dxn = dxn_ref[...]
            dx_ref[...] = dxn + dx_
            dgate = jnp.sum(0.5 * dxn * y_ref[...].astype(F32), axis=0, keepdims=True)
            _acc_rows(dssg_ref.at[0], i % tpb == 0, [dsh_, dsc_, dgate])
            _acc_rows(dng_ref, i == 0, [dng_])

    return _pcall(
        body, name="ffn_bwd_a", grid=(n // tm, nf),
        in_specs=[pl.BlockSpec((tm, D), lambda i, j: (i, 0)),
                  pl.BlockSpec((tm, D), lambda i, j: (i, 0)),
                  pl.BlockSpec((1, 3, D), lambda i, j: (i // tpb, 0, 0)),
                  pl.BlockSpec((1, D), lambda i, j: (0, 0)),
                  pl.BlockSpec((tm, D), lambda i, j: (i, 0)),
                  pl.BlockSpec((2, None, tm, tf), lambda i, j: (0, j, i, 0)),
                  pl.BlockSpec((2, None, tf, D), lambda i, j: (0, j, 0, 0)),
                  pl.BlockSpec((None, tf, D), lambda i, j: (j, 0, 0))],
        out_specs=[pl.BlockSpec((tm, D), lambda i, j: (i, 0)),
                   pl.BlockSpec((2, None, tm, tf), lambda i, j: (0, j, i, 0)),
                   pl.BlockSpec((tm, D), lambda i, j: (i, 0)),
                   pl.BlockSpec((tm, D), lambda i, j: (i, 0)),
                   pl.BlockSpec((1, 3, D), lambda i, j: (i // tpb, 0, 0)),
                   pl.BlockSpec((1, D), lambda i, j: (0, 0))],
        out_shape=[_sds((n, D), F32), _sds((2, nf, n, tf), BF16), _sds((n, D), BF16), _sds((n, D), BF16),
                   _sds((BL, 3, D), F32), _sds((1, D), F32)],
        scratch_shapes=[pltpu.VMEM((tm, D), BF16), pltpu.VMEM((tm, D), F32)],
        compiler_params=_cp(("arbitrary", "arbitrary")),
    )(x, dxn, ssg, ng, y, gu, w_in, w_out)


def _ffn_bwd_w(h, dgu, hid, dout):
    n = h.shape[0]
    _, nf, _, tf = dgu.shape
    tm = min(1024, T)
    ni = n // tm

    def body(h_ref, dgu_ref, hid_ref, dout_ref, dwin_ref, dwout_ref, acc_g, acc_u, acc_o):
        i = pl.program_id(1)

        @pl.when(i == 0)
        def _():
            acc_g[...] = jnp.zeros_like(acc_g)
            acc_u[...] = jnp.zeros_like(acc_u)
            acc_o[...] = jnp.zeros_like(acc_o)

        hv = h_ref[...]
        acc_g[...] += _dot_tn(dgu_ref[0], hv)
        acc_u[...] += _dot_tn(dgu_ref[1], hv)
        acc_o[...] += _dot_tn(hid_ref[...], dout_ref[...])

        @pl.when(i == ni - 1)
        def _():
            dwin_ref[0] = acc_g[...].astype(BF16)
            dwin_ref[1] = acc_u[...].astype(BF16)
            dwout_ref[...] = acc_o[...].astype(BF16)

    return _pcall(
        body, name="ffn_bwd_w", grid=(nf, ni),
        in_specs=[pl.BlockSpec((tm, D), lambda j, i: (i, 0)),
                  pl.BlockSpec((2, None, tm, tf), lambda j, i: (0, j, i, 0)),
                  pl.BlockSpec((None, tm, tf), lambda j, i: (j, i, 0)),
                  pl.BlockSpec((tm, D), lambda j, i: (i, 0))],
        out_specs=[pl.BlockSpec((2, None, tf, D), lambda j, i: (0, j, 0, 0)),
                   pl.BlockSpec((None, tf, D), lambda j, i: (j, 0, 0))],
        out_shape=[_sds((2, nf, tf, D), BF16), _sds((nf, tf, D), BF16)],
        scratch_shapes=[pltpu.VMEM((tf, D), F32), pltpu.VMEM((tf, D), F32), pltpu.VMEM((tf, D), F32)],
        compiler_params=_cp(("arbitrary", "arbitrary")),
    )(h, dgu, hid, dout)


def _premod_matmul(x, ssg, ng, w, bias, tn):
    n = x.shape[0]
    shards = w.ndim == 3
    m = w.shape[0] * w.shape[2] if shards else w.shape[0]
    tm = min(512, T)
    tpb = T // tm
    to = m if shards else tn
    w_spec = (pl.BlockSpec(w.shape, lambda i, j: (0, 0, 0)) if shards
              else pl.BlockSpec((tn, D), lambda i, j: (j, 0)))

    def body(x_ref, ssg_ref, ng_ref, w_ref, b_ref, h_ref, o_ref, h_scr):
        @pl.when(pl.program_id(1) == 0)
        def _():
            s = ssg_ref[0]
            hb = _modulate(x_ref[...], ng_ref[...], s[1:2], s[0:1]).astype(BF16)
            h_scr[...] = hb
            h_ref[...] = hb

        hv = h_scr[...]
        if shards:
            for q in range(w.shape[0]):
                cols = slice(q * tn, (q + 1) * tn)
                o_ref[:, cols] = _dot(hv, w_ref[q]) + b_ref[:, cols]
        else:
            o_ref[...] = _dot_nt(hv, w_ref[...]) + b_ref[...]

    return _pcall(
        body, name="premod_matmul", grid=(n // tm, m // to),
        in_specs=[pl.BlockSpec((tm, D), lambda i, j: (i, 0)),
                  pl.BlockSpec((1, 3, D), lambda i, j: (i // tpb, 0, 0)),
                  pl.BlockSpec((1, D), lambda i, j: (0, 0)),
                  w_spec,
                  pl.BlockSpec((1, to), lambda i, j: (0, j))],
        out_specs=[pl.BlockSpec((tm, D), lambda i, j: (i, 0)),
                   pl.BlockSpec((tm, to), lambda i, j: (i, j))],
        out_shape=[_sds((n, D), BF16), _sds((n, m), F32)],
        scratch_shapes=[pltpu.VMEM((tm, D), BF16)],
        compiler_params=_cp(("arbitrary", "arbitrary")),
    )(x, ssg, ng, w, bias)


def _premod_matmul_bwd(x, dxn, ssg, ng, douts, w):
    n = x.shape[0]
    k = len(douts)
    shards = w.ndim == 3
    tm = min(512, T)
    tpb = T // tm

    def body(*refs):
        x_ref, dxn_ref, ssg_ref, ng_ref = refs[:4]
        do_refs, w_ref = refs[4:4 + k], refs[4 + k]
        dx_ref, dssg_ref, dng_ref = refs[5 + k:]
        i = pl.program_id(0)
        dh = jnp.zeros((tm, D), F32)
        if shards:
            cs = w.shape[2]
            dov = do_refs[0][...]
            for j in range(w.shape[0]):
                dh += _dot_nt(dov[:, j * cs:(j + 1) * cs], w_ref[j])
        else:
            off = 0
            for q in range(k):
                mk = douts[q].shape[1]
                dh += _dot(do_refs[q][...], w_ref[off:off + mk, :])
                off += mk
        s = ssg_ref[0]
        _, dx_, dng_, dsc_, dsh_ = _modulate_bwd(x_ref[...], ng_ref[...], s[1:2], s[0:1], dh)
        dx_ref[...] = dxn_ref[...] + dx_
        _acc_rows(dssg_ref.at[0], i % tpb == 0, [dsh_, dsc_, jnp.zeros_like(dsh_)])
        _acc_rows(dng_ref, i == 0, [dng_])

    return _pcall(
        body, name="premod_matmul_bwd", grid=(n // tm,),
        in_specs=[pl.BlockSpec((tm, D), lambda i: (i, 0)),
                  pl.BlockSpec((tm, D), lambda i: (i, 0)),
                  pl.BlockSpec((1, 3, D), lambda i: (i // tpb, 0, 0)),
                  pl.BlockSpec((1, D), lambda i: (0, 0))]
                 + [pl.BlockSpec((tm, a.shape[1]), lambda i: (i, 0)) for a in douts]
                 + [pl.BlockSpec(w.shape, (lambda i: (0, 0, 0)) if shards else (lambda i: (0, 0)))],
        out_specs=[pl.BlockSpec((tm, D), lambda i: (i, 0)),
                   pl.BlockSpec((1, 3, D), lambda i: (i // tpb, 0, 0)),
                   pl.BlockSpec((1, D), lambda i: (0, 0))],
        out_shape=[_sds((n, D), F32), _sds((BL, 3, D), F32), _sds((1, D), F32)],
        compiler_params=_cp(("arbitrary",)),
    )(x, dxn, ssg, ng, *douts, w)


def _matmul_res(x, a, ssg, w, bias):
    n, kd = a.shape
    tm = min(512, T)
    tpb = T // tm

    def body(x_ref, a_ref, ssg_ref, w_ref, b_ref, xn_ref, y_ref):
        yv = _dot(a_ref[...], w_ref[...]) + b_ref[...]
        y_ref[...] = yv.astype(BF16)
        xn_ref[...] = x_ref[...] + (1.0 + ssg_ref[0][2:3]) * yv

    return _pcall(
        body, name="matmul_res", grid=(n // tm,),
        in_specs=[pl.BlockSpec((tm, D), lambda i: (i, 0)),
                  pl.BlockSpec((tm, kd), lambda i: (i, 0)),
                  pl.BlockSpec((1, 3, D), lambda i: (i // tpb, 0, 0)),
                  pl.BlockSpec((kd, D), lambda i: (0, 0)),
                  pl.BlockSpec((1, D), lambda i: (0, 0))],
        out_specs=[pl.BlockSpec((tm, D), lambda i: (i, 0)), pl.BlockSpec((tm, D), lambda i: (i, 0))],
        out_shape=[_sds((n, D), F32), _sds((n, D), BF16)],
        compiler_params=_cp(("arbitrary",)),
    )(x, a, ssg, w, bias)


def _matmul_res_bwd(dxn, y, ssg, w):
    n = dxn.shape[0]
    kd = w.shape[0]
    tm = min(512, T)
    tpb = T // tm

    def body(dxn_ref, y_ref, ssg_ref, w_ref, da_ref, dy_ref, dgate_ref, dbias_ref):
        i = pl.program_id(0)
        dxn = dxn_ref[...]
        dy = (1.0 + ssg_ref[0][2:3]) * dxn
        dyb = dy.astype(BF16)
        dy_ref[...] = dyb
        da_ref[...] = _dot_nt(dyb, w_ref[...])
        _acc_rows(dgate_ref.at[0], i % tpb == 0, [jnp.sum(dxn * y_ref[...].astype(F32), axis=0, keepdims=True)])
        _acc_rows(dbias_ref, i == 0, [jnp.sum(dy, axis=0, keepdims=True)])

    return _pcall(
        body, name="matmul_res_bwd", grid=(n // tm,),
        in_specs=[pl.BlockSpec((tm, D), lambda i: (i, 0)),
                  pl.BlockSpec((tm, D), lambda i: (i, 0)),
                  pl.BlockSpec((1, 3, D), lambda i: (i // tpb, 0, 0)),
                  pl.BlockSpec((kd, D), lambda i: (0, 0))],
        out_specs=[pl.BlockSpec((tm, kd), lambda i: (i, 0)),
                   pl.BlockSpec((tm, D), lambda i: (i, 0)),
                   pl.BlockSpec((1, 1, D), lambda i: (i // tpb, 0, 0)),
                   pl.BlockSpec((1, D), lambda i: (0, 0))],
        out_shape=[_sds((n, kd), F32), _sds((n, D), BF16), _sds((BL, 1, D), F32), _sds((1, D), F32)],
        compiler_params=_cp(("arbitrary",)),
    )(dxn, y, ssg, w)


def _wgrad_shards(a, b, ns):
    n, kd = a.shape
    cs = b.shape[1] // ns
    tm = min(512, T)
    ni = n // tm

    def body(a_ref, b_ref, o_ref, acc):
        i = pl.program_id(0)

        @pl.when(i == 0)
        def _():
            acc[...] = jnp.zeros_like(acc)

        at = a_ref[...].T
        for q in range(ns):
            acc[q] += _dot(at, b_ref[:, q * cs:(q + 1) * cs])

        @pl.when(i == ni - 1)
        def _():
            o_ref[...] = acc[...].astype(BF16)

    return _pcall(
        body, name="wgrad_shards", grid=(ni,),
        in_specs=[pl.BlockSpec((tm, kd), lambda i: (i, 0)), pl.BlockSpec((tm, ns * cs), lambda i: (i, 0))],
        out_specs=pl.BlockSpec((ns, kd, cs), lambda i: (0, 0, 0)),
        out_shape=_sds((ns, kd, cs), BF16),
        scratch_shapes=[pltpu.VMEM((ns, kd, cs), F32)],
        compiler_params=_cp(("arbitrary",)),
    )(a, b)


def _wgrad(a, b):
    n, kd = a.shape
    m = b.shape[1]
    tm = min(1024, T)
    tk = min(1024, kd)
    ni = n // tm

    def body(a_ref, b_ref, o_ref, acc):
        i = pl.program_id(1)

        @pl.when(i == 0)
        def _():
            acc[...] = jnp.zeros_like(acc)

        acc[...] += _dot_tn(a_ref[...], b_ref[...])

        @pl.when(i == ni - 1)
        def _():
            o_ref[...] = acc[...].astype(BF16)

    return _pcall(
        body, name="wgrad", grid=(kd // tk, ni),
        in_specs=[pl.BlockSpec((tm, tk), lambda q, i: (i, q)), pl.BlockSpec((tm, m), lambda q, i: (i, 0))],
        out_specs=pl.BlockSpec((tk, m), lambda q, i: (q, 0)),
        out_shape=_sds((kd, m), BF16),
        scratch_shapes=[pltpu.VMEM((tk, m), F32)],
        compiler_params=_cp(("arbitrary", "arbitrary")),
    )(a, b)


def _ln_silu(u1, g, b):
    mu = jnp.mean(u1, axis=-1, keepdims=True)
    xc = u1 - mu
    var = jnp.mean(xc * xc, axis=-1, keepdims=True)
    ln = xc * lax.rsqrt(var + EPS) * g + b
    return ln * jax.nn.sigmoid(ln)


def _conv_tiles():
    tt = min(256, T)
    return tt, T // tt


def _prev_halo_spec(cols, tt, halo):
    r = tt // halo
    return pl.BlockSpec((halo, cols), lambda b, i: (jnp.maximum(b * (T // halo) + i * r - 1, 0), 0))


def _next_halo_spec(cols, tt, halo):
    r = tt // halo
    last = BL * T // halo - 1
    return pl.BlockSpec((halo, cols), lambda b, i: (jnp.minimum(b * (T // halo) + (i + 1) * r, last), 0))


ROWS = 32
SROWS = 8


def _fill_rotations(rot, win, rows):
    for r in range(8):
        rot[r, 0:rows, :] = win[pl.ds(r, rows), :]


def _window(rot, off, start, size):
    return rot[off % 8, pl.ds(pl.multiple_of(start + (off // 8) * 8, 8), size), :]


def _cm_mid_fwd(ab, w_dw, b_dw, ln_g, ln_b):
    n = ab.shape[0]
    tt, nt = _conv_tiles()

    def body(ab_ref, halo_ref, w_ref, bdw_ref, g_ref, b_ref, u1_ref, u2_ref, win, rot):
        i = pl.program_id(1)
        hv = halo_ref[...]
        u0h = hv[:, :D] * jax.nn.sigmoid(hv[:, D:])
        win[0:HALO, :] = jnp.where(i == 0, 0.0, u0h)
        cv = ab_ref[...]
        win[HALO:HALO + tt, :] = cv[:, :D] * jax.nn.sigmoid(cv[:, D:])
        win[HALO + tt:, :] = jnp.zeros((8, D), F32)
        _fill_rotations(rot, win, tt + HALO)

        def chunk(c, carry):
            r0 = pl.multiple_of(c * ROWS, ROWS)
            acc = jnp.zeros((ROWS, D), F32) + bdw_ref[...]
            for k in range(CW):
                acc += w_ref[k:k + 1, :] * _window(rot, HALO - (CW - 1) + k, r0, ROWS)
            u1_ref[pl.ds(r0, ROWS), :] = acc
            u2_ref[pl.ds(r0, ROWS), :] = _ln_silu(acc, g_ref[...], b_ref[...]).astype(BF16)
            return carry

        lax.fori_loop(0, tt // ROWS, chunk, 0)

    row = lambda b, i: (b * nt + i, 0)
    vec = pl.BlockSpec((1, D), lambda b, i: (0, 0))
    return _pcall(
        body, name="cm_mid_fwd", grid=(BL, nt),
        in_specs=[pl.BlockSpec((tt, 2 * D), row), _prev_halo_spec(2 * D, tt, HALO),
                  pl.BlockSpec((HALO, D), lambda b, i: (0, 0)), vec, vec, vec],
        out_specs=[pl.BlockSpec((tt, D), row), pl.BlockSpec((tt, D), row)],
        out_shape=[_sds((n, D), F32), _sds((n, D), BF16)],
        scratch_shapes=[pltpu.VMEM((HALO + tt + 8, D), F32), pltpu.VMEM((8, tt + HALO, D), F32)],
        compiler_params=_cp(("arbitrary", "arbitrary")),
    )(ab, ab, w_dw, b_dw, ln_g, ln_b)


def _cm_mid_bwd_a(du2, u1, ln_g, ln_b):
    n = du2.shape[0]
    tm = min(256, T)

    def body(du2_ref, u1_ref, g_ref, b_ref, du1_ref, dln_ref):
        _, vjp = jax.vjp(_ln_silu, u1_ref[...], g_ref[...], b_ref[...])
        du1, dg, db = vjp(du2_ref[...])
        du1_ref[...] = du1
        _acc_rows(dln_ref, pl.program_id(0) == 0, [dg, db])

    vec = pl.BlockSpec((1, D), lambda i: (0, 0))
    return _pcall(
        body, name="cm_mid_bwd_a", grid=(n // tm,),
        in_specs=[pl.BlockSpec((tm, D), lambda i: (i, 0)), pl.BlockSpec((tm, D), lambda i: (i, 0)), vec, vec],
        out_specs=[pl.BlockSpec((tm, D), lambda i: (i, 0)), pl.BlockSpec((2, D), lambda i: (0, 0))],
        out_shape=[_sds((n, D), F32), _sds((2, D), F32)],
        compiler_params=_cp(("arbitrary",)),
    )(du2, u1, ln_g, ln_b)


def _cm_mid_bwd_b(du1, ab, w_dw):
    n = du1.shape[0]
    tt, nt = _conv_tiles()

    def body(du1_ref, nxt_ref, ab_ref, halo_ref, w_ref, dab_ref, dw_ref, dbdw_ref, dbglu_ref,
             dwin, uwin, rotd, rotu, accw, accv):
        b, i = pl.program_id(0), pl.program_id(1)
        first = jnp.logical_and(b == 0, i == 0)
        dwin[0:tt, :] = du1_ref[...]
        dwin[tt:tt + HALO, :] = jnp.where(i == nt - 1, 0.0, nxt_ref[...])
        dwin[tt + HALO:, :] = jnp.zeros((8, D), F32)
        hv = halo_ref[...]
        uwin[0:HALO, :] = jnp.where(i == 0, 0.0, hv[:, :D] * jax.nn.sigmoid(hv[:, D:]))
        cv = ab_ref[...]
        uwin[HALO:HALO + tt, :] = cv[:, :D] * jax.nn.sigmoid(cv[:, D:])
        uwin[HALO + tt:, :] = jnp.zeros((8, D), F32)
        _fill_rotations(rotd, dwin, tt + HALO)
        _fill_rotations(rotu, uwin, tt + HALO)
        accw[...] = jnp.zeros_like(accw)
        accv[...] = jnp.zeros_like(accv)

        def fold(v):
            return jnp.sum(v.reshape(ROWS // 8, 8, D), axis=0)

        def chunk(c, carry):
            r0 = pl.multiple_of(c * ROWS, ROWS)
            d1 = du1_ref[pl.ds(r0, ROWS), :]
            du0 = jnp.zeros((ROWS, D), F32)
            for k in range(CW):
                du0 += w_ref[k:k + 1, :] * _window(rotd, CW - 1 - k, r0, ROWS)
                accw[k] += fold(d1 * _window(rotu, HALO - (CW - 1) + k, r0, ROWS))
            cvc = ab_ref[pl.ds(r0, ROWS), :]
            av, sg = cvc[:, :D], jax.nn.sigmoid(cvc[:, D:])
            da = du0 * sg
            db = du0 * av * sg * (1.0 - sg)
            dab_ref[pl.ds(r0, ROWS), 0:D] = da.astype(BF16)
            dab_ref[pl.ds(r0, ROWS), D:2 * D] = db.astype(BF16)
            accv[0] += fold(d1)
            accv[1] += fold(da)
            accv[2] += fold(db)
            return carry

        lax.fori_loop(0, tt // ROWS, chunk, 0)
        dws = [jnp.sum(accw[k], axis=0, keepdims=True) for k in range(CW)]
        dws += [jnp.zeros((1, D), F32)] * (HALO - CW)
        _acc_rows(dw_ref, first, dws)
        _acc_rows(dbdw_ref, first, [jnp.sum(accv[0], axis=0, keepdims=True)])
        _acc_rows(dbglu_ref.at[:, 0:D], first, [jnp.sum(accv[1], axis=0, keepdims=True)])
        _acc_rows(dbglu_ref.at[:, D:2 * D], first, [jnp.sum(accv[2], axis=0, keepdims=True)])

    row = lambda b, i: (b * nt + i, 0)
    return _pcall(
        body, name="cm_mid_bwd_b", grid=(BL, nt),
        in_specs=[pl.BlockSpec((tt, D), row), _next_halo_spec(D, tt, HALO),
                  pl.BlockSpec((tt, 2 * D), row), _prev_halo_spec(2 * D, tt, HALO),
                  pl.BlockSpec((HALO, D), lambda b, i: (0, 0))],
        out_specs=[pl.BlockSpec((tt, 2 * D), row), pl.BlockSpec((HALO, D), lambda b, i: (0, 0)),
                   pl.BlockSpec((1, D), lambda b, i: (0, 0)), pl.BlockSpec((1, 2 * D), lambda b, i: (0, 0))],
        out_shape=[_sds((n, 2 * D), BF16), _sds((HALO, D), F32), _sds((1, D), F32), _sds((1, 2 * D), F32)],
        scratch_shapes=[pltpu.VMEM((tt + HALO + 8, D), F32), pltpu.VMEM((HALO + tt + 8, D), F32),
                        pltpu.VMEM((8, tt + HALO, D), F32), pltpu.VMEM((8, tt + HALO, D), F32),
                        pltpu.VMEM((HALO, 8, D), F32), pltpu.VMEM((3, 8, D), F32)],
        compiler_params=_cp(("arbitrary", "arbitrary")),
    )(du1, du1, ab, ab, w_dw)


def _softplus(v):
    return jnp.maximum(v, 0.0) + jnp.log(1.0 + jnp.exp(-jnp.abs(v)))


def _g_beta(ab, alog, dtb):
    return -jnp.exp(alog) * _softplus(ab + dtb), jax.nn.sigmoid(ab)


def _dn_sconv_fwd(proj, w_sc, alog, dtb):
    n = proj.shape[0]
    tt, nt = _conv_tiles()
    w3 = 3 * D

    def body(qkv_ref, halo_ref, ab_ref, w_ref, alog_ref, dtb_ref, conv_ref, q_ref, k_ref, v_ref, gb_ref, bb_ref,
             win, rot, gsc, bsc):
        i = pl.program_id(1)
        win[0:SHALO, :] = jnp.where(i == 0, 0.0, halo_ref[...])
        win[SHALO:SHALO + tt, :] = qkv_ref[...]
        for k in range(SCW - 1):
            rot[k] = win[pl.ds(SHALO - (SCW - 1) + k, tt), :]
        gsc[...], bsc[...] = _g_beta(ab_ref[...], alog_ref[...], dtb_ref[...])

        def chunk(c, carry):
            rows = pl.ds(pl.multiple_of(c * SROWS, SROWS), SROWS)
            acc = w_ref[SCW - 1:SCW, :] * win[pl.ds(pl.multiple_of(c * SROWS + SHALO, SROWS), SROWS), :]
            for k in range(SCW - 1):
                acc += w_ref[k:k + 1, :] * rot[k, rows, :]
            conv_ref[rows, :] = acc
            act = acc * jax.nn.sigmoid(acc)
            gfull, bfull = gsc[rows, :], bsc[rows, :]
            for h in range(NH):
                q_ref[0, h, rows, :] = act[:, h * DH:(h + 1) * DH]
                k_ref[0, h, rows, :] = act[:, D + h * DH:D + (h + 1) * DH]
                v_ref[0, h, rows, :] = act[:, 2 * D + h * DH:2 * D + (h + 1) * DH]
                gb_ref[0, h, rows, :] = jnp.broadcast_to(gfull[:, h:h + 1], (SROWS, DH))
                bb_ref[0, h, rows, :] = jnp.broadcast_to(bfull[:, NH + h:NH + h + 1], (SROWS, DH))
            return carry

        lax.fori_loop(0, tt // SROWS, chunk, 0)

    row = lambda b, i: (b * nt + i, 0)
    head = pl.BlockSpec((1, NH, tt, DH), lambda b, i: (b, 0, i, 0))
    vec = pl.BlockSpec((1, 128), lambda b, i: (0, 0))
    hs = _sds((BL, NH, T, DH), F32)
    return _pcall(
        body, name="dn_sconv_fwd", grid=(BL, nt),
        in_specs=[pl.BlockSpec((tt, w3), row), _prev_halo_spec(w3, tt, SHALO),
                  pl.BlockSpec((tt, 128), lambda b, i: (b * nt + i, 4 * D // 128)),
                  pl.BlockSpec((SHALO, w3), lambda b, i: (0, 0)), vec, vec],
        out_specs=[pl.BlockSpec((tt, w3), row), head, head, head, head, head],
        out_shape=[_sds((n, w3), F32), hs, hs, hs, hs, hs],
        scratch_shapes=[pltpu.VMEM((SHALO + tt, w3), F32), pltpu.VMEM((SCW - 1, tt, w3), F32),
                        pltpu.VMEM((tt, 128), F32), pltpu.VMEM((tt, 128), F32)],
        compiler_params=_cp(("arbitrary", "arbitrary")),
    )(proj, proj, proj, w_sc, alog, dtb)


_BMM_SPEC = {"nn": "gij,gjk->gik", "nt": "gid,gjd->gij", "tn": "gcd,gce->gde"}


def _mm(kind, a, b, prec):
    if prec is None:
        return jnp.einsum(_BMM_SPEC[kind], a.astype(BF16), b.astype(BF16), preferred_element_type=F32)
    return jnp.einsum(_BMM_SPEC[kind], a, b, preferred_element_type=F32, precision=prec)


@functools.partial(jax.custom_vjp, nondiff_argnums=(0, 3))
def _bmm_k(kind, a, b, prec):
    return _mm(kind, a, b, prec)


def _bmm_k_fwd(kind, a, b, prec):
    return _mm(kind, a, b, prec), (a, b)


def _bmm_k_bwd(kind, prec, res, dc):
    a, b = res
    if kind == "nn":
        return _bmm_k("nt", dc, b, prec), _bmm_k("tn", a, dc, prec)
    if kind == "nt":
        return _bmm_k("nn", dc, b, prec), _bmm_k("tn", dc, a, prec)
    return _bmm_k("nt", b, dc, prec), _bmm_k("nn", a, dc, prec)


_bmm_k.defvjp(_bmm_k_fwd, _bmm_k_bwd)


def _bmm(a, b, prec=None):
    return _bmm_k("nn", a, b, prec)


def _bmm_nt(a, b, prec=None):
    return _bmm_k("nt", a, b, prec)


def _bmm_tn(a, b, prec=None):
    return _bmm_k("tn", a, b, prec)


def _bmm_raw(a, b):
    return _mm("nn", a, b, None)


def _bmm_nt_raw(a, b):
    return _mm("nt", a, b, None)


def _bmm_tn_raw(a, b):
    return _mm("tn", a, b, None)


@jax.custom_vjp
def _unit_lower_inverse(a):
    eye = (lax.broadcasted_iota(jnp.int32, a.shape, 1) == lax.broadcasted_iota(jnp.int32, a.shape, 2)).astype(F32)
    t = eye - a
    p = a
    for _ in range(CHUNK.bit_length() - 2):
        p = _mm("nn", p, p, INV_PREC)
        t = _mm("nn", t, eye + p, INV_PREC)
    return t


def _uli_fwd(a):
    t = _unit_lower_inverse(a)
    return t, t


def _uli_bwd(t, dt):
    return (-_bmm_nt(_bmm_tn(t, dt, lax.Precision.HIGH), t, lax.Precision.HIGH),)


_unit_lower_inverse.defvjp(_uli_fwd, _uli_bwd)


@jax.custom_vjp
def _known_inverse(a, t):
    return t


_known_inverse.defvjp(lambda a, t: (t, t), lambda t, dt: (_uli_bwd(t, dt)[0], jnp.zeros_like(t)))


def _dn_pre(q, k, v, gb, bb, tm_known=None):
    shape = (q.shape[0], CHUNK, CHUNK)
    ri = lax.broadcasted_iota(jnp.int32, shape, 1)
    ci = lax.broadcasted_iota(jnp.int32, shape, 2)
    causal, strict = ri >= ci, ri > ci
    qn = q * lax.rsqrt(jnp.sum(q * q, axis=-1, keepdims=True) + EPS) * (DH ** -0.5)
    kn = k * lax.rsqrt(jnp.sum(k * k, axis=-1, keepdims=True) + EPS)
    gcs = _bmm(causal.astype(F32), gb, HI)
    gcol = gcs[:, :, :CHUNK]
    decay = jnp.exp(jnp.where(causal, gcol - jnp.swapaxes(gcol, 1, 2), -jnp.inf))
    eg = jnp.exp(gcs)
    kb = kn * bb
    a = jnp.where(strict, _bmm_nt(kb, kn) * decay, 0.0)
    tm = _unit_lower_inverse(a) if tm_known is None else _known_inverse(a, tm_known)
    u = _bmm(tm, v * bb)
    w = _bmm(tm, kb * eg)
    qg = qn * eg
    intra = _bmm_nt(qn, kn) * decay
    glast = gcs[:, CHUNK - 1:CHUNK, :]
    kd = kn * jnp.exp(glast - gcs)
    egl = jnp.broadcast_to(jnp.exp(glast), (q.shape[0], 8, DH))
    return u, w, qg, kd, intra, egl, tm


def _pre_tiles():
    gcn = min(16, T // CHUNK)
    return gcn, T // (CHUNK * gcn)


def _dn_pre_specs():
    gcn, _ = _pre_tiles()
    tok = pl.BlockSpec((None, None, gcn * CHUNK, DH), lambda b, h, i: (b, h, i, 0))
    sq = pl.BlockSpec((None, None, gcn * CHUNK, CHUNK), lambda b, h, i: (b, h, i, 0))
    per = pl.BlockSpec((None, None, gcn * 8, DH), lambda b, h, i: (b, h, i, 0))
    return tok, sq, per


def _dn_pre_fwd(q, k, v, gb, bb):
    gcn, ng = _pre_tiles()
    tok, sq, per = _dn_pre_specs()

    def body(q_ref, k_ref, v_ref, gb_ref, bb_ref, u_ref, w_ref, qg_ref, kd_ref, in_ref, egl_ref, tinv_ref):
        args = [r[...].reshape(gcn, CHUNK, DH) for r in (q_ref, k_ref, v_ref, gb_ref, bb_ref)]
        u, w, qg, kd, intra, egl, tinv = _dn_pre(*args)
        for r, val in ((u_ref, u), (w_ref, w), (qg_ref, qg), (kd_ref, kd)):
            r[...] = val.reshape(gcn * CHUNK, DH).astype(r.dtype)
        in_ref[...] = intra.reshape(gcn * CHUNK, CHUNK).astype(BF16)
        tinv_ref[...] = tinv.reshape(gcn * CHUNK, CHUNK)
        egl_ref[...] = egl.reshape(gcn * 8, DH)

    hs = _sds((BL, NH, T, DH), F32)
    hb = _sds((BL, NH, T, DH), BF16)
    sqs = _sds((BL, NH, T, CHUNK), F32)
    return _pcall(
        body, name="dn_pre_fwd", grid=(BL, NH, ng),
        in_specs=[tok] * 5, out_specs=[tok, tok, tok, tok, sq, per, sq],
        out_shape=[hs, hb, hb, hb, _sds((BL, NH, T, CHUNK), BF16), _sds((BL, NH, T // CHUNK * 8, DH), F32), sqs],
        compiler_params=_cp(("arbitrary",) * 3),
    )(q, k, v, gb, bb)


def _dn_pre_bwd(q, k, v, gb, bb, tinv, du, dw, dqg, dkd, dintra, degl):
    gcn, ng = _pre_tiles()
    tok, sq, per = _dn_pre_specs()

    def body(q_ref, k_ref, v_ref, gb_ref, bb_ref, tinv_ref, du_ref, dw_ref, dqg_ref, dkd_ref, din_ref, degl_ref,
             dq_ref, dk_ref, dv_ref, dgb_ref, dbb_ref):
        args = [r[...].reshape(gcn, CHUNK, DH) for r in (q_ref, k_ref, v_ref, gb_ref, bb_ref)]
        known = tinv_ref[...].reshape(gcn, CHUNK, CHUNK)
        _, vjp = jax.vjp(lambda *a: _dn_pre(*a, tm_known=known)[:6], *args)
        cts = [r[...].reshape(gcn, CHUNK, DH) for r in (du_ref, dw_ref, dqg_ref, dkd_ref)]
        de = degl_ref[...].reshape(gcn, 8, DH)
        one = jnp.logical_and(lax.broadcasted_iota(jnp.int32, de.shape, 1) == 0,
                              lax.broadcasted_iota(jnp.int32, de.shape, 2) == 0)
        outs = vjp((*cts, din_ref[...].reshape(gcn, CHUNK, CHUNK), jnp.where(one, de, 0.0)))
        for r, val in zip((dq_ref, dk_ref, dv_ref, dgb_ref, dbb_ref), outs):
            r[...] = val.reshape(gcn * CHUNK, DH)

    hs = _sds((BL, NH, T, DH), F32)
    return _pcall(
        body, name="dn_pre_bwd", grid=(BL, NH, ng),
        in_specs=[tok] * 5 + [sq] + [tok] * 4 + [sq, per], out_specs=[tok] * 5, out_shape=[hs] * 5,
        compiler_params=_cp(("arbitrary",) * 3),
    )(q, k, v, gb, bb, tinv, du, dw, dqg, dkd, dintra, degl)


def _scan_tiles():
    cs = min(2, T // CHUNK)
    return cs, T // (CHUNK * cs)


def _dn_scan_fwd(u, w, qg, kd, intra, egl):
    cs, ns = _scan_tiles()
    g = BL * NH
    nc = T // CHUNK

    def body(u_ref, w_ref, qg_ref, kd_ref, in_ref, egl_ref, o_ref, vn_ref, s0_ref, s_scr):
        @pl.when(pl.program_id(0) == 0)
        def _():
            s_scr[...] = jnp.zeros_like(s_scr)

        for c in range(cs):
            rows = pl.ds(c * CHUNK, CHUNK)
            s = s_scr[...]
            s0_ref[:, :, c] = s.reshape(BL, NH, DH, DH)

            def ld(r, m=DH):
                return r[:, :, rows, :].reshape(g, CHUNK, m)

            vn = ld(u_ref) - _bmm_raw(ld(w_ref), s)
            o = _bmm_raw(ld(qg_ref), s) + _bmm_raw(ld(in_ref, CHUNK), vn)
            e = egl_ref[:, :, pl.ds(c * 8, 1), :].reshape(g, 1, DH)
            s_scr[...] = s * e + _bmm_tn_raw(ld(kd_ref), vn)
            vn_ref[:, :, rows, :] = vn.reshape(BL, NH, CHUNK, DH)
            o_ref[:, :, rows, :] = o.reshape(BL, NH, CHUNK, DH)

    tok = pl.BlockSpec((BL, NH, cs * CHUNK, DH), lambda i: (0, 0, i, 0))
    hs = _sds((BL, NH, T, DH), F32)
    return _pcall(
        body, name="dn_scan_fwd", grid=(ns,),
        in_specs=[tok, tok, tok, tok, pl.BlockSpec((BL, NH, cs * CHUNK, CHUNK), lambda i: (0, 0, i, 0)),
                  pl.BlockSpec((BL, NH, cs * 8, DH), lambda i: (0, 0, i, 0))],
        out_specs=[tok, tok, pl.BlockSpec((BL, NH, cs, DH, DH), lambda i: (0, 0, i, 0, 0))],
        out_shape=[hs, hs, _sds((BL, NH, nc, DH, DH), F32)],
        scratch_shapes=[pltpu.VMEM((g, DH, DH), F32)],
        compiler_params=_cp(("arbitrary",)),
    )(u, w, qg, kd, intra, egl)


def _dn_scan_bwd(do, w, qg, kd, intra, egl, vn, s0):
    cs, ns = _scan_tiles()
    g = BL * NH
    nc = T // CHUNK

    def body(do_ref, w_ref, qg_ref, kd_ref, in_ref, egl_ref, vn_ref, s0_ref,
             du_ref, dw_ref, dqg_ref, dkd_ref, din_ref, degl_ref, ds_scr):
        @pl.when(pl.program_id(0) == 0)
        def _():
            ds_scr[...] = jnp.zeros_like(ds_scr)

        for c in reversed(range(cs)):
            rows = pl.ds(c * CHUNK, CHUNK)

            def ld(r, m=DH):
                return r[:, :, rows, :].reshape(g, CHUNK, m)

            def st(r, val, m=DH):
                r[:, :, rows, :] = val.reshape(BL, NH, CHUNK, m)

            s = s0_ref[:, :, c].reshape(g, DH, DH)
            ds = ds_scr[...]
            dov, vnv, kdv, wv, qgv, inv = ld(do_ref), ld(vn_ref), ld(kd_ref), ld(w_ref), ld(qg_ref), ld(in_ref, CHUNK)
            dv = _bmm_tn_raw(inv, dov) + _bmm_raw(kdv, ds)
            st(din_ref, _bmm_nt_raw(dov, vnv), CHUNK)
            st(dqg_ref, _bmm_nt_raw(dov, s))
            st(dkd_ref, _bmm_nt_raw(vnv, ds))
            st(du_ref, dv)
            st(dw_ref, -_bmm_nt_raw(dv, s))
            de = jnp.sum(jnp.sum(ds * s, axis=2, keepdims=True), axis=1, keepdims=True)
            degl_ref[:, :, pl.ds(c * 8, 8), :] = jnp.broadcast_to(de, (g, 8, DH)).reshape(BL, NH, 8, DH)
            e = egl_ref[:, :, pl.ds(c * 8, 1), :].reshape(g, 1, DH)
            ds_scr[...] = ds * e + _bmm_tn_raw(qgv, dov) - _bmm_tn_raw(wv, dv)

    rev = lambda i: (0, 0, ns - 1 - i, 0)
    tok = pl.BlockSpec((BL, NH, cs * CHUNK, DH), rev)
    sq = pl.BlockSpec((BL, NH, cs * CHUNK, CHUNK), rev)
    per = pl.BlockSpec((BL, NH, cs * 8, DH), rev)
    hs = _sds((BL, NH, T, DH), F32)
    return _pcall(
        body, name="dn_scan_bwd", grid=(ns,),
        in_specs=[tok, tok, tok, tok, sq, per, tok,
                  pl.BlockSpec((BL, NH, cs, DH, DH), lambda i: (0, 0, ns - 1 - i, 0, 0))],
        out_specs=[tok, tok, tok, tok, sq, per],
        out_shape=[hs, hs, hs, hs, _sds((BL, NH, T, CHUNK), F32), _sds((BL, NH, nc * 8, DH), F32)],
        scratch_shapes=[pltpu.VMEM((g, DH, DH), F32)],
        compiler_params=_cp(("arbitrary",)),
    )(do, w, qg, kd, intra, egl, vn, s0)


def _gated_norm(o_h, z_h, og):
    r = lax.rsqrt(jnp.mean(o_h * o_h, axis=-1, keepdims=True) + EPS)
    return (o_h * r * og) * (z_h * jax.nn.sigmoid(z_h))


def _dn_gnorm_fwd(o, proj, o_g):
    tm = min(256, T)
    nt = T // tm

    def body(o_ref, z_ref, g_ref, og_ref):
        z = z_ref[...]
        for h in range(NH):
            og_ref[:, h * DH:(h + 1) * DH] = _gated_norm(o_ref[0, h], z[:, h * DH:(h + 1) * DH], g_ref[...]).astype(BF16)

    return _pcall(
        body, name="dn_gnorm_fwd", grid=(BL, nt),
        in_specs=[pl.BlockSpec((1, NH, tm, DH), lambda b, i: (b, 0, i, 0)),
                  pl.BlockSpec((tm, D), lambda b, i: (b * nt + i, 3)),
                  pl.BlockSpec((1, DH), lambda b, i: (0, 0))],
        out_specs=pl.BlockSpec((tm, D), lambda b, i: (b * nt + i, 0)),
        out_shape=_sds((BL * T, D), BF16),
        compiler_params=_cp(("arbitrary", "arbitrary")),
    )(o, proj, o_g)


def _dn_gnorm_bwd(dog, o, proj, o_g):
    tm = min(256, T)
    nt = T // tm

    def body(dog_ref, o_ref, z_ref, g_ref, do_ref, dz_ref, dg_ref):
        z = z_ref[...]
        dog = dog_ref[...]
        dg = jnp.zeros((1, DH), F32)
        for h in range(NH):
            cols = slice(h * DH, (h + 1) * DH)
            _, vjp = jax.vjp(_gated_norm, o_ref[0, h], z[:, cols], g_ref[...])
            do_h, dz_h, dg_h = vjp(dog[:, cols])
            do_ref[0, h] = do_h
            dz_ref[:, cols] = dz_h.astype(BF16)
            dg += dg_h
        _acc_rows(dg_ref, jnp.logical_and(pl.program_id(0) == 0, pl.program_id(1) == 0), [dg])

    return _pcall(
        body, name="dn_gnorm_bwd", grid=(BL, nt),
        in_specs=[pl.BlockSpec((tm, D), lambda b, i: (b * nt + i, 0)),
                  pl.BlockSpec((1, NH, tm, DH), lambda b, i: (b, 0, i, 0)),
                  pl.BlockSpec((tm, D), lambda b, i: (b * nt + i, 3)),
                  pl.BlockSpec((1, DH), lambda b, i: (0, 0))],
        out_specs=[pl.BlockSpec((1, NH, tm, DH), lambda b, i: (b, 0, i, 0)),
                   pl.BlockSpec((tm, D), lambda b, i: (b * nt + i, 0)),
                   pl.BlockSpec((1, DH), lambda b, i: (0, 0))],
        out_shape=[_sds((BL, NH, T, DH), F32), _sds((BL * T, D), BF16), _sds((1, DH), F32)],
        compiler_params=_cp(("arbitrary", "arbitrary")),
    )(dog, o, proj, o_g)


def _dn_prep_bwd(dq, dk, dv, dgb, dbb, conv, proj, alog, dtb):
    n = conv.shape[0]
    tt, nt = _conv_tiles()
    w3 = 3 * D

    def body(dq_ref, dk_ref, dv_ref, dgb_ref, dbb_ref, conv_ref, ab_ref, alog_ref, dtb_ref, dconv_ref, dab_ref, dhead_ref):
        cv = conv_ref[...]
        sg = jax.nn.sigmoid(cv)
        dact = sg * (1.0 + cv * (1.0 - sg))
        lane = lax.broadcasted_iota(jnp.int32, (tt, 128), 1)
        cg = jnp.zeros((tt, 128), F32)
        cb = jnp.zeros((tt, 128), F32)
        for h in range(NH):
            cols = slice(h * DH, (h + 1) * DH)
            dconv_ref[:, h * DH:(h + 1) * DH] = dq_ref[0, h] * dact[:, cols]
            dconv_ref[:, D + h * DH:D + (h + 1) * DH] = dk_ref[0, h] * dact[:, D + h * DH:D + (h + 1) * DH]
            dconv_ref[:, 2 * D + h * DH:2 * D + (h + 1) * DH] = dv_ref[0, h] * dact[:, 2 * D + h * DH:2 * D + (h + 1) * DH]
            cg = jnp.where(lane == h, jnp.sum(dgb_ref[0, h], axis=-1, keepdims=True), cg)
            cb = jnp.where(lane == NH + h, jnp.sum(dbb_ref[0, h], axis=-1, keepdims=True), cb)
        _, vjp = jax.vjp(_g_beta, ab_ref[...], alog_ref[...], dtb_ref[...])
        dab, dalog, ddtb = vjp((cg, cb))
        dab_ref[...] = dab.astype(BF16)
        _acc_rows(dhead_ref, jnp.logical_and(pl.program_id(0) == 0, pl.program_id(1) == 0), [dalog, ddtb])

    row = lambda b, i: (b * nt + i, 0)
    head = pl.BlockSpec((1, NH, tt, DH), lambda b, i: (b, 0, i, 0))
    vec = pl.BlockSpec((1, 128), lambda b, i: (0, 0))
    return _pcall(
        body, name="dn_prep_bwd", grid=(BL, nt),
        in_specs=[head] * 5 + [pl.BlockSpec((tt, w3), row),
                               pl.BlockSpec((tt, 128), lambda b, i: (b * nt + i, 4 * D // 128)), vec, vec],
        out_specs=[pl.BlockSpec((tt, w3), row), pl.BlockSpec((tt, 128), row), pl.BlockSpec((2, 128), lambda b, i: (0, 0))],
        out_shape=[_sds((n, w3), F32), _sds((n, 128), BF16), _sds((2, 128), F32)],
        compiler_params=_cp(("arbitrary", "arbitrary")),
    )(dq, dk, dv, dgb, dbb, conv, proj, alog, dtb)


def _dn_sconv_bwd(dconv, proj, w_sc):
    n = dconv.shape[0]
    tt, nt = _conv_tiles()
    w3 = 3 * D

    def body(dc_ref, nxt_ref, qkv_ref, halo_ref, w_ref, dpre_ref, dw_ref, dwin, pwin, rotd, rotp, dsc, accw):
        b, i = pl.program_id(0), pl.program_id(1)
        dwin[0:tt, :] = dc_ref[...]
        dwin[tt:tt + SHALO, :] = jnp.where(i == nt - 1, 0.0, nxt_ref[...])
        pwin[0:SHALO, :] = jnp.where(i == 0, 0.0, halo_ref[...])
        pwin[SHALO:SHALO + tt, :] = qkv_ref[...]
        for k in range(SCW - 1):
            rotd[k] = dwin[pl.ds(k + 1, tt), :]
            rotp[k] = pwin[pl.ds(SHALO - (SCW - 1) + k, tt), :]
        accw[...] = jnp.zeros_like(accw)

        def chunk(c, carry):
            r0 = pl.multiple_of(c * SROWS, SROWS)
            rows = pl.ds(r0, SROWS)
            dc = dc_ref[rows, :]
            dpre = w_ref[SCW - 1:SCW, :] * dc
            accw[SCW - 1] += dc * pwin[pl.ds(pl.multiple_of(r0 + SHALO, SROWS), SROWS), :]
            for k in range(SCW - 1):
                dpre += w_ref[k:k + 1, :] * rotd[SCW - 2 - k, rows, :]
                accw[k] += dc * rotp[k, rows, :]
            dsc[rows, :] = dpre
            return carry

        lax.fori_loop(0, tt // SROWS, chunk, 0)
        dpre_ref[...] = dsc[...].astype(BF16)
        dws = [jnp.sum(accw[k], axis=0, keepdims=True) for k in range(SCW)]
        dws += [jnp.zeros((1, w3), F32)] * (SHALO - SCW)
        _acc_rows(dw_ref, jnp.logical_and(b == 0, i == 0), dws)

    row = lambda b, i: (b * nt + i, 0)
    return _pcall(
        body, name="dn_sconv_bwd", grid=(BL, nt),
        in_specs=[pl.BlockSpec((tt, w3), row), _next_halo_spec(w3, tt, SHALO),
                  pl.BlockSpec((tt, w3), row), _prev_halo_spec(w3, tt, SHALO),
                  pl.BlockSpec((SHALO, w3), lambda b, i: (0, 0))],
        out_specs=[pl.BlockSpec((tt, w3), row), pl.BlockSpec((SHALO, w3), lambda b, i: (0, 0))],
        out_shape=[_sds((n, w3), BF16), _sds((SHALO, w3), F32)],
        scratch_shapes=[pltpu.VMEM((tt + SHALO, w3), F32), pltpu.VMEM((SHALO + tt, w3), F32),
                        pltpu.VMEM((SCW - 1, tt, w3), F32), pltpu.VMEM((SCW - 1, tt, w3), F32),
                        pltpu.VMEM((tt, w3), F32), pltpu.VMEM((SCW, SROWS, w3), F32)],
        compiler_params=_cp(("arbitrary", "arbitrary")),
    )(dconv, dconv, proj, proj, w_sc)


def _ada_fwd(c_all, w_ada, b_cols):
    nl, _, m = w_ada.shape
    nb = c_all.shape[0]

    def body(c_ref, w_ref, b_ref, o_ref):
        cv = c_ref[...]
        cs = (cv * jax.nn.sigmoid(cv)).astype(BF16)
        o_ref[...] = _dot(cs, w_ref[...].astype(BF16)) + b_ref[...]

    return _pcall(
        body, name="ada_fwd", grid=(nl,),
        in_specs=[pl.BlockSpec((nb, D), lambda l: (0, 0)), pl.BlockSpec((None, D, m), lambda l: (l, 0, 0)),
                  pl.BlockSpec((None, 1, m), lambda l: (l, 0, 0))],
        out_specs=pl.BlockSpec((None, nb, m), lambda l: (l, 0, 0)),
        out_shape=_sds((nl, nb, m), F32),
        compiler_params=_cp(("arbitrary",)),
    )(c_all, w_ada, b_cols)


def _ada_bwd(c_all, dmod_cols):
    nl, nb, m = dmod_cols.shape

    def body(c_ref, d_ref, o_ref):
        cv = c_ref[...]
        cs = (cv * jax.nn.sigmoid(cv)).astype(BF16)
        o_ref[0] = _dot_tn(cs, d_ref[...].astype(BF16))

    return _pcall(
        body, name="ada_bwd", grid=(nl,),
        in_specs=[pl.BlockSpec((nb, D), lambda l: (0, 0)), pl.BlockSpec((None, nb, m), lambda l: (l, 0, 0))],
        out_specs=pl.BlockSpec((1, D, m), lambda l: (0, l, 0)),
        out_shape=_sds((1, nl * D, m), F32),
        compiler_params=_cp(("arbitrary",)),
    )(c_all, dmod_cols)


def _loss_head(x, tgt, fg):
    n = x.shape[0]
    tm = min(512, T)

    def f(xv, g, t):
        r = lax.rsqrt(jnp.mean(xv * xv, axis=-1, keepdims=True) + EPS)
        e = xv * r * g - t
        return 0.5 * jnp.sum(e * e, axis=0, keepdims=True) * (1.0 / D)

    def body(x_ref, t_ref, g_ref, dx_ref, st_ref):
        t = t_ref[...]
        lrow, vjp = jax.vjp(lambda xv, g: f(xv, g, t), x_ref[...], g_ref[...])
        dx, dg = vjp(jnp.ones_like(lrow))
        dx_ref[...] = dx
        _acc_rows(st_ref, pl.program_id(0) == 0, [dg, lrow])

    return _pcall(
        body, name="loss_head", grid=(n // tm,),
        in_specs=[pl.BlockSpec((tm, D), lambda i: (i, 0)), pl.BlockSpec((tm, D), lambda i: (i, 0)),
                  pl.BlockSpec((1, D), lambda i: (0, 0))],
        out_specs=[pl.BlockSpec((tm, D), lambda i: (i, 0)), pl.BlockSpec((2, D), lambda i: (0, 0))],
        out_shape=[_sds((n, D), F32), _sds((2, D), F32)],
        compiler_params=_cp(("arbitrary",)),
    )(x, tgt, fg)


def _adamw(parts, w, m, v):
    p, r, c = parts.shape
    tr = r
    for cand in (256, 128, 64, 32, 16, 8):
        if r % cand == 0:
            tr = cand
            break
    k1 = 1.0 - B1 ** STEP
    k2 = 1.0 - B2 ** STEP

    def body(p_ref, w_ref, m_ref, v_ref, g_ref, d_ref, nm_ref, nv_ref):
        g = p_ref[0].astype(F32)
        for q in range(1, p):
            g += p_ref[q].astype(F32)
        mn = B1 * m_ref[...] + (1.0 - B1) * g
        vn = B2 * v_ref[...] + (1.0 - B2) * (g * g)
        g_ref[...] = g
        nm_ref[...] = mn
        nv_ref[...] = vn
        d_ref[...] = -LR * ((mn / k1) / (jnp.sqrt(vn / k2) + AEPS) + WD * w_ref[...])

    blk = pl.BlockSpec((tr, c), lambda i: (i, 0))
    return _pcall(
        body, name="adamw", grid=(r // tr,),
        in_specs=[pl.BlockSpec((p, tr, c), lambda i: (0, i, 0)), blk, blk, blk],
        out_specs=[blk] * 4, out_shape=[_sds((r, c), F32)] * 4,
        compiler_params=_cp(("arbitrary",)),
    )(parts, w, m, v)


def _sum_parts(parts):
    p, r, c = parts.shape

    def body(p_ref, o_ref):
        acc = p_ref[0]
        for q in range(1, p):
            acc += p_ref[q]
        o_ref[...] = acc

    return _pcall(body, name="sum_parts", out_shape=_sds((r, c), F32))(parts)


def _adamw_slot(parts, w, m, v, outs, row0, col):
    p, r, c = parts.shape
    tr = r
    for cand in (256, 128, 64, 32, 16, 8):
        if r % cand == 0:
            tr = cand
            break
    if r % 352 == 0:
        tr = 352
    nt = r // tr
    k1 = 1.0 - B1 ** STEP
    k2 = 1.0 - B2 ** STEP

    def body(p_ref, w_ref, m_ref, v_ref, g0, d0, m0, v0, g_ref, d_ref, nm_ref, nv_ref):
        g = p_ref[0].astype(F32)
        for q in range(1, p):
            g += p_ref[q].astype(F32)
        mn = B1 * m_ref[...] + (1.0 - B1) * g
        vn = B2 * v_ref[...] + (1.0 - B2) * (g * g)
        g_ref[...] = g
        nm_ref[...] = mn
        nv_ref[...] = vn
        d_ref[...] = -LR * ((mn / k1) / (jnp.sqrt(vn / k2) + AEPS) + WD * w_ref[...])

    blk = pl.BlockSpec((tr, c), lambda i: (row0 * nt + i, col))
    anyspec = pl.BlockSpec(memory_space=pl.ANY)
    return _pcall(
        body, name="adamw_slot", grid=(nt,),
        in_specs=[pl.BlockSpec((p, tr, c), lambda i: (0, i, 0)), blk, blk, blk] + [anyspec] * 4,
        out_specs=[blk] * 4, out_shape=[_sds(w.shape, F32)] * 4,
        input_output_aliases={4: 0, 5: 1, 6: 2, 7: 3},
        compiler_params=_cp(("arbitrary",)),
    )(parts, w, m, v, *outs)


def _pack(arrs):
    flat = jnp.concatenate([a.reshape(-1) for a in arrs])
    pad = (-flat.shape[0]) % 1024
    return jnp.pad(flat, (0, pad)).reshape(-1, 128)


def _unpack(buf, shapes):
    flat = buf.reshape(-1)
    out, off = [], 0
    for s in shapes:
        size = 1
        for d in s:
            size *= d
        out.append(flat[off:off + size].reshape(s))
        off += size
    return out


def kernel(x, c, norm_g, w_ada, b_ada, w_ffn_in, w_ffn_out, cm_w_glu, cm_b_glu, cm_w_dw, cm_b_dw, cm_ln_g, cm_ln_b, cm_w_pw, cm_b_pw, dn_w_in, dn_w_sconv, dn_a_log, dn_dt_bias, dn_o_g, dn_w_out, final_g, loss_target, m_norm_g, m_w_ada, m_b_ada, m_w_ffn_in, m_w_ffn_out, m_cm_w_glu, m_cm_b_glu, m_cm_w_dw, m_cm_b_dw, m_cm_ln_g, m_cm_ln_b, m_cm_w_pw, m_cm_b_pw, m_dn_w_in, m_dn_w_sconv, m_dn_a_log, m_dn_dt_bias, m_dn_o_g, m_dn_w_out, m_final_g, v_norm_g, v_w_ada, v_b_ada, v_w_ffn_in, v_w_ffn_out, v_cm_w_glu, v_cm_b_glu, v_cm_w_dw, v_cm_b_dw, v_cm_ln_g, v_cm_ln_b, v_cm_w_pw, v_cm_b_pw, v_dn_w_in, v_dn_w_sconv, v_dn_a_log, v_dn_dt_bias, v_dn_o_g, v_dn_w_out, v_final_g):
    me = 4 * lax.axis_index("x") + 2 * lax.axis_index("y") + lax.axis_index("c")
    n = BL * T
    nf = 4
    tf = FF // nf
    na, nb = cm_w_glu.shape[0], dn_w_in.shape[0]
    mcols = w_ada.shape[2]
    dsh = D // NDEV

    tr_ffn = lambda a: jnp.swapaxes(a, 2, 3)
    tr_dn = lambda a: jnp.transpose(a, (2, 0, 1))
    wt_ffn_in, wt_dn_in = tr_ffn(w_ffn_in), tr_dn(dn_w_in)

    def unit_weights(l, part):
        if part == 0:
            ws = (wt_ffn_in[l, 0], w_ffn_out[l, 0])
        else:
            mix = (cm_w_glu[l // 2], cm_w_pw[l // 2]) if l % 2 == 0 else (wt_dn_in[:, l // 2], dn_w_out[l // 2])
            ws = (wt_ffn_in[l, 1], w_ffn_out[l, 1], *mix)
        return [w.astype(BF16) for w in ws]

    gathers, all_started = {}, jnp.zeros((8, 128), F32)
    for l in range(DEPTH):
        for part in range(2):
            if l < RELAY_LAYERS:
                gathers[l, part], tok = _relay_gather_start(unit_weights(l, part), f"gather_start_{l}_{part}")
            else:
                gathers[l, part], tok = _exchange_start(unit_weights(l, part), True, f"gather_start_{l}_{part}")
            all_started = all_started + tok

    c_g, ng_g, dw_g, sc_g = _all_gather([c, norm_g, cm_w_dw, dn_w_sconv], "gather_small")
    whole = lambda g: jnp.moveaxis(g, 0, -2).reshape(*g.shape[1:-1], -1)
    c_all = c_g.reshape(NDEV * BL, D)
    norm_g_f, w_dw_f, w_sc_f = whole(ng_g), whole(dw_g), whole(sc_g)

    b_cols = lax.dynamic_slice_in_dim(b_ada, me * mcols, mcols, axis=1)[:, None, :]
    mod_cols = _ada_fwd(c_all, w_ada, b_cols)
    mod_g, = _all_gather([mod_cols], "gather_mod")
    mod_all = jnp.transpose(mod_g, (1, 2, 0, 3)).reshape(DEPTH, NDEV * BL, 9 * D)
    mod = lax.dynamic_slice_in_dim(mod_all, me * BL, BL, axis=1).reshape(DEPTH, BL, 3, 3, D)

    gathered = [None] * DEPTH

    def gather_wait(l, part, after):
        if l < RELAY_LAYERS:
            passed = _relay_gather_pass(gathers[l, part], after, f"gather_pass_{l}_{part}")
            return _exchange_wait(passed, after, f"gather_wait_{l}_{part}", blocks=3)
        return _exchange_wait(gathers[l, part], after, f"gather_wait_{l}_{part}")

    def ffn_weights(l, s):
        return gathered[l][s].reshape(2, nf, tf, D), gathered[l][2 + s].reshape(nf, tf, D)

    xs = x.reshape(n, D)
    saved = []
    for l in range(DEPTH):
        rec = {}
        ga = gather_wait(l, 0, all_started if l == 0 else xs)
        gathered[l] = [ga[0], None, ga[1], None, None, None]
        for s, j in ((0, 0), (1, 2)):
            if j == 2:
                gb = gather_wait(l, 1, xs)
                gathered[l] = [ga[0], gb[0], ga[1], gb[1], gb[2], gb[3]]
            w_in, w_out = ffn_weights(l, s)
            ssg, ng = mod[l, :, j], norm_g_f[l, j][None]
            if j == 2:
                ssg1, ng1 = mod[l, :, 1], norm_g_f[l, 1][None]
                if l % 2 == 0:
                    a = l // 2
                    w_glu = gathered[l][4]
                    w_pw = gathered[l][5].reshape(D, D)
                    w_dw = jnp.pad(w_dw_f[a], ((0, HALO - CW), (0, 0)))
                    h1, ab = _premod_matmul(xs, ssg1, ng1, w_glu, cm_b_glu[a][None], w_glu.shape[2])
                    u1, u2 = _cm_mid_fwd(ab, w_dw, cm_b_dw[a][None], cm_ln_g[a][None], cm_ln_b[a][None])
                    xn, ymix = _matmul_res(xs, u2, ssg1, w_pw, cm_b_pw[a][None])
                    rec["mix"] = dict(x=xs, h=h1, ab=ab, u1=u1, u2=u2, y=ymix, w_glu=w_glu, w_pw=w_pw, w_dw=w_dw)
                else:
                    mi = l // 2
                    w_proj = jnp.pad(gathered[l][4].reshape(4 * D + 2 * NH, D), ((0, 128 - 2 * NH), (0, 0)))
                    w_o = gathered[l][5].reshape(D, D)
                    w_sc = jnp.pad(w_sc_f[mi], ((0, SHALO - SCW), (0, 0)))
                    alog = jnp.pad(dn_a_log[mi], (0, 128 - NH))[None]
                    dtb = jnp.pad(dn_dt_bias[mi], (0, 128 - NH))[None]
                    h1, proj = _premod_matmul(xs, ssg1, ng1, w_proj, jnp.zeros((1, w_proj.shape[0]), F32),
                                              (4 * D + 128) // 3 if (4 * D + 128) % 384 == 0 else 128)
                    conv, q, k, v, gb, bb = _dn_sconv_fwd(proj, w_sc, alog, dtb)
                    u, w, qg, kd, intra, egl, tinv = _dn_pre_fwd(q, k, v, gb, bb)
                    o, vn, s0 = _dn_scan_fwd(u, w, qg, kd, intra, egl)
                    og = _dn_gnorm_fwd(o, proj, dn_o_g[mi][None])
                    xn, ymix = _matmul_res(xs, og, ssg1, w_o, jnp.zeros((1, D), F32))
                    rec["mix"] = dict(x=xs, h=h1, proj=proj, conv=conv, q=q, k=k, v=v, gb=gb, bb=bb, w=w, qg=qg, kd=kd,
                                      intra=intra, egl=egl, tinv=tinv, o=o, vn=vn, s0=s0, og=og, y=ymix, w_proj=w_proj, w_o=w_o,
                                      w_sc=w_sc, alog=alog, dtb=dtb)
                xs = xn
            xn, gu, hid, y = _ffn_fwd(xs, ssg, ng, w_in, w_out)
            rec[s] = dict(x=xs, gu=gu, hid=hid, y=y)
            xs = xn
        saved.append(rec)

    dx, stats = _loss_head(xs, loss_target.reshape(n, D), final_g[None])
    loss = lax.psum(jnp.sum(stats[1]), AXES)
    d_final_g = stats[0]

    d_mod = [[None] * 3 for _ in range(DEPTH)]
    d_norm = [[None] * 3 for _ in range(DEPTH)]
    dw_ffn_in = [[None] * 2 for _ in range(DEPTH)]
    dw_ffn_out = [[None] * 2 for _ in range(DEPTH)]
    dcm = [dict() for _ in range(na)]
    ddn = [dict() for _ in range(nb)]
    exchanges = {}

    def gather_small_grads():
        dmod_loc = jnp.stack([jnp.stack(d_mod[l], axis=1) for l in range(DEPTH)]).reshape(DEPTH, BL, 9 * D)
        small = [jnp.sum(dmod_loc, axis=1), jnp.stack([jnp.stack(d_norm[l]) for l in range(DEPTH)]),
                 jnp.stack([d["b_glu"] for d in dcm]), jnp.stack([d["w_dw"] for d in dcm]), jnp.stack([d["b_dw"] for d in dcm]),
                 jnp.stack([d["ln_g"] for d in dcm]), jnp.stack([d["ln_b"] for d in dcm]), jnp.stack([d["b_pw"] for d in dcm]),
                 jnp.stack([d["w_sconv"] for d in ddn]), jnp.stack([d["a_log"] for d in ddn]),
                 jnp.stack([d["dt_bias"] for d in ddn]), jnp.stack([d["o_g"] for d in ddn]), d_final_g]
        dmod_g, small_parts = _all_gather([dmod_loc, _pack(small)], "gather_small_grads")
        return dmod_g, small_parts, [a.shape for a in small]

    token = jnp.zeros((), F32)
    for l in reversed(range(DEPTH)):
        rec = saved[l]
        for s, j in ((1, 2), (0, 0)):
            w_in, w_out = ffn_weights(l, s)
            ssg, ng = mod[l, :, j] + token, norm_g_f[l, j][None]
            r = rec[s]
            dx, dgu, hb, dout, dssg, dng = _ffn_bwd_a(r["x"], dx, ssg, ng, r["y"], r["gu"], w_in, w_out)
            dw_ffn_in[l][s], dw_ffn_out[l][s] = _ffn_bwd_w(hb, dgu, r["hid"], dout)
            d_mod[l][j], d_norm[l][j] = dssg, dng[0]
            if j == 2:
                ssg1, ng1 = mod[l, :, 1], norm_g_f[l, 1][None]
                r = rec["mix"]
                if l % 2 == 0:
                    a = l // 2
                    du2, dy, dgate, db_pw = _matmul_res_bwd(dx, r["y"], ssg1, r["w_pw"])
                    du1, dln = _cm_mid_bwd_a(du2, r["u1"], cm_ln_g[a][None], cm_ln_b[a][None])
                    dab, dw_dw, db_dw, db_glu = _cm_mid_bwd_b(du1, r["ab"], r["w_dw"])
                    dx, dssg, dng = _premod_matmul_bwd(r["x"], dx, ssg1, ng1, [dab], r["w_glu"])
                    dcm[a] = dict(w_glu=_wgrad_shards(r["h"], dab, NDEV), w_pw=_wgrad(r["u2"], dy).reshape(NDEV, dsh, D),
                                  b_glu=db_glu[0], w_dw=dw_dw[:CW], b_dw=db_dw[0], ln_g=dln[0], ln_b=dln[1], b_pw=db_pw[0])
                else:
                    mi = l // 2
                    dog, dy, dgate, _ = _matmul_res_bwd(dx, r["y"], ssg1, r["w_o"])
                    do, dz, d_og = _dn_gnorm_bwd(dog, r["o"], r["proj"], dn_o_g[mi][None])
                    du, dw, dqg, dkd, dintra, degl = _dn_scan_bwd(do, r["w"], r["qg"], r["kd"], r["intra"], r["egl"],
                                                                   r["vn"], r["s0"])
                    dq, dk, dv, dgb, dbb = _dn_pre_bwd(r["q"], r["k"], r["v"], r["gb"], r["bb"], r["tinv"],
                                                       du, dw, dqg, dkd, dintra, degl)
                    dconv, dab16, dhead = _dn_prep_bwd(dq, dk, dv, dgb, dbb, r["conv"], r["proj"], r["alog"], r["dtb"])
                    dpre, dw_sc = _dn_sconv_bwd(dconv, r["proj"], r["w_sc"])
                    dx, dssg, dng = _premod_matmul_bwd(r["x"], dx, ssg1, ng1, [dpre, dz, dab16], r["w_proj"])
                    dw_in = jnp.concatenate([_wgrad(dpre, r["h"]), _wgrad(dz, r["h"]),
                                             _wgrad(dab16, r["h"])[:2 * NH]], axis=0)
                    ddn[mi] = dict(w_in=dw_in.reshape(NDEV, -1, D), w_out=_wgrad(r["og"], dy).reshape(NDEV, dsh, D),
                                   w_sconv=dw_sc[:SCW], a_log=dhead[0, :NH], dt_bias=dhead[1, :NH], o_g=d_og[0])
                d_mod[l][1] = dssg.at[:, 2].set(dgate[:, 0])
                d_norm[l][1] = dng[0]
            unit = [dw_ffn_in[l][s].reshape(NDEV, tf, D), dw_ffn_out[l][s].reshape(NDEV, FF // NDEV, D)]
            if j == 2:
                g = dcm[l // 2] if l % 2 == 0 else ddn[l // 2]
                unit += [g["w_glu"], g["w_pw"]] if l % 2 == 0 else [g["w_in"], g["w_out"]]
            if l == 0 and s == 0:
                dmod_g, small_parts, full_shapes = gather_small_grads()
                unit[0], dmod_g, small_parts = lax.optimization_barrier((unit[0], dmod_g, small_parts))
                small_gathered = (dmod_g, small_parts, full_shapes)
            exchanges[l, s], token = _exchange_start(unit, False, f"grads_start_{l}_{s}")
    grad_x = dx.reshape(BL, T, D)

    dmod_g, small_parts, full_shapes = small_gathered
    dmod_all = jnp.transpose(dmod_g, (1, 0, 2, 3)).reshape(DEPTH, NDEV * BL, 9 * D)
    g_w_ada = _ada_bwd(c_all, lax.dynamic_slice_in_dim(dmod_all, me * mcols, mcols, axis=2))

    got = []
    for l in range(DEPTH):
        ea = _exchange_wait(exchanges[l, 0], dx, f"grads_wait_{l}_0") if l > 0 else [None, None]
        eb = _exchange_wait(exchanges[l, 1], dx, f"grads_wait_{l}_1")
        got.append([ea[0], eb[0], ea[1], eb[1], eb[2], eb[3]])

    names = ["b_ada", "norm_g", "cm_b_glu", "cm_w_dw", "cm_b_dw", "cm_ln_g", "cm_ln_b", "cm_b_pw",
             "dn_w_sconv", "dn_a_log", "dn_dt_bias", "dn_o_g", "final_g"]
    cols = lambda a, width: lax.dynamic_slice_in_dim(a, me * width, width, axis=a.ndim - 1)
    local = {"norm_g": lambda a: cols(a, dsh), "cm_w_dw": lambda a: cols(a, dsh), "dn_w_sconv": lambda a: cols(a, 3 * dsh)}
    summed = _unpack(_sum_parts(small_parts), full_shapes)
    mine = [local.get(nm, lambda a: a)(p) for nm, p in zip(names, summed)]
    small_w = dict(b_ada=(b_ada, m_b_ada, v_b_ada), norm_g=(norm_g, m_norm_g, v_norm_g),
                   cm_b_glu=(cm_b_glu, m_cm_b_glu, v_cm_b_glu), cm_w_dw=(cm_w_dw, m_cm_w_dw, v_cm_w_dw),
                   cm_b_dw=(cm_b_dw, m_cm_b_dw, v_cm_b_dw), cm_ln_g=(cm_ln_g, m_cm_ln_g, v_cm_ln_g),
                   cm_ln_b=(cm_ln_b, m_cm_ln_b, v_cm_ln_b), cm_b_pw=(cm_b_pw, m_cm_b_pw, v_cm_b_pw),
                   dn_w_sconv=(dn_w_sconv, m_dn_w_sconv, v_dn_w_sconv), dn_a_log=(dn_a_log, m_dn_a_log, v_dn_a_log),
                   dn_dt_bias=(dn_dt_bias, m_dn_dt_bias, v_dn_dt_bias), dn_o_g=(dn_o_g, m_dn_o_g, v_dn_o_g),
                   final_g=(final_g, m_final_g, v_final_g))
    loc_shapes = [small_w[nm][0].shape for nm in names]
    sres_raw = _adamw(_pack(mine)[None], *[_pack([small_w[nm][q] for nm in names]) for q in range(3)])
    sres = [dict(zip(names, _unpack(r, loc_shapes))) for r in sres_raw]

    res = {}

    def update(slots, wmv, view, back=None, outs=None):
        w2, m2, v2 = [view(a) for a in wmv]
        outs = [lax.empty(w2.shape, F32) for _ in range(4)] if outs is None else outs
        for p, row0, col in slots:
            outs = _adamw_slot(p, w2, m2, v2, outs, row0, col)
        return outs if back is None else [back(o) for o in outs]

    ffn_slots = [(l, s) for l in reversed(range(DEPTH)) for s in (1, 0)][:-1]
    wmv_in, view_in = (w_ffn_in, m_w_ffn_in, v_w_ffn_in), lambda a: tr_ffn(a).reshape(-1, D)
    wmv_out, view_out = (w_ffn_out, m_w_ffn_out, v_w_ffn_out), lambda a: a.reshape(-1, D)
    part_in = update([(got[l][s], 2 * l + s, 0) for l, s in ffn_slots], wmv_in, view_in)
    part_out = update([(got[l][2 + s], 2 * l + s, 0) for l, s in ffn_slots], wmv_out, view_out)
    cgl = cm_w_glu.shape[2]
    res["cm_w_glu"] = update([(got[2 * a][4], a, 0) for a in range(na)], (cm_w_glu, m_cm_w_glu, v_cm_w_glu),
                             lambda a: a.reshape(-1, cgl), lambda o: o.reshape(cm_w_glu.shape))
    res["cm_w_pw"] = update([(got[2 * a][5], a, 0) for a in range(na)], (cm_w_pw, m_cm_w_pw, v_cm_w_pw),
                            lambda a: a.reshape(-1, D), lambda o: o.reshape(cm_w_pw.shape))
    cdn = dn_w_in.shape[2]
    res["dn_w_in"] = update([(got[2 * i + 1][4], 0, i) for i in range(nb)], (dn_w_in, m_dn_w_in, v_dn_w_in),
                            lambda a: tr_dn(a).reshape(cdn, nb * D),
                            lambda o: jnp.transpose(o.reshape(cdn, nb, D), (1, 2, 0)))
    res["dn_w_out"] = update([(got[2 * i + 1][5], i, 0) for i in range(nb)], (dn_w_out, m_dn_w_out, v_dn_w_out),
                             lambda a: a.reshape(-1, D), lambda o: o.reshape(dn_w_out.shape))
    res["w_ada"] = [o.reshape(w_ada.shape) for o in
                    _adamw(g_w_ada, *[a.reshape(-1, mcols) for a in (w_ada, m_w_ada, v_w_ada)])]
    done = [part_in[0], part_out[0], sres_raw[0]] + [res[nm][0] for nm in ("cm_w_glu", "cm_w_pw", "dn_w_in", "dn_w_out", "w_ada")]
    last = _exchange_wait(exchanges[0, 0], done, "grads_wait_0_0")
    res["w_ffn_in"] = update([(last[0], 0, 0)], wmv_in, view_in,
                             lambda o: jnp.swapaxes(o.reshape(DEPTH, 2, tf, D), 2, 3), part_in)
    res["w_ffn_out"] = update([(last[1], 0, 0)], wmv_out, view_out, lambda o: o.reshape(w_ffn_out.shape), part_out)
    for nm in names:
        res[nm] = [sres[q][nm] for q in range(4)]

    order = ["norm_g", "w_ada", "b_ada", "w_ffn_in", "w_ffn_out", "cm_w_glu", "cm_b_glu", "cm_w_dw", "cm_b_dw", "cm_ln_g",
             "cm_ln_b", "cm_w_pw", "cm_b_pw", "dn_w_in", "dn_w_sconv", "dn_a_log", "dn_dt_bias", "dn_o_g", "dn_w_out", "final_g"]
    return (loss, grad_x, *[res[nm][0] for nm in order], *[res[nm][1] for nm in order],
            *[res[nm][2] for nm in order], *[res[nm][3] for nm in order])
```
